```python
import jax, jax.numpy as jnp
from jax import lax
import numpy as np

D_MODEL = 1024
BATCH = 8
SEQ = 4096
DEPTH = 2

EPS = 1e-6
CONV_WIDTH = 3
A_HEADS = 8
A_HEAD_DIM = 64
A_WIDTH = A_HEADS * A_HEAD_DIM
POOL_WINDOWS = (2, 4, 8, 16)
B_GROUPS = len(POOL_WINDOWS)
B_GROUP_DIM = 128
B_WIDTH = B_GROUPS * B_GROUP_DIM
EVEN_IN = 3 * A_WIDTH + B_WIDTH
EVEN_MIX = A_WIDTH + B_WIDTH
CHUNK = 128
C_HEADS = 8
C_HEAD_DIM = 128
C_WIDTH = C_HEADS * C_HEAD_DIM
D_FF = 2816

N_EVEN = (DEPTH + 1) // 2
N_ODD = DEPTH // 2

kernel_name = "hybrid_conv_pool_sgu_trunk"


def rms_norm(x, g):
    x32 = x.astype(jnp.float32)
    y = x32 * lax.rsqrt(jnp.mean(x32 * x32, axis=-1, keepdims=True) + EPS)
    return y.astype(x.dtype) * g


def causal_dwconv3(x, w):
    s = x.shape[1]
    xp = jnp.pad(x, ((0, 0), (CONV_WIDTH - 1, 0), (0, 0)))
    y = xp[:, 0:s] * w[0]
    for k in range(1, CONV_WIDTH):
        y = y + xp[:, k:k + s] * w[k]
    return y


def short_gated_conv(gate_b, gate_c, val, conv_w):
    return gate_b * causal_dwconv3(gate_c * val, conv_w)


def multiscale_pool(z, w_pool, pool_scale):
    b, s, _ = z.shape
    z32 = z.astype(jnp.float32).reshape(b, s, B_GROUPS, B_GROUP_DIM)
    cs = jnp.cumsum(z32, axis=1)
    pos = jnp.arange(1, s + 1, dtype=jnp.float32)[None, :, None]
    outs = []
    for g, w in enumerate(POOL_WINDOWS):
        csg = cs[:, :, g]
        shifted = jnp.pad(csg, ((0, 0), (w, 0), (0, 0)))[:, :s]
        count = jnp.minimum(pos, jnp.float32(w))
        outs.append((csg - shifted) / count - z32[:, :, g])
    pooled = jnp.stack(outs, axis=2).astype(z.dtype)
    mixed = jnp.einsum('bsgc,gcd->bsgd', pooled, w_pool)
    return mixed.reshape(b, s, B_WIDTH) * pool_scale


def chunked_spatial_gating(u, v, sgu_norm, w_spatial, b_spatial):
    b, s, _ = u.shape
    n = s // CHUNK
    vn = rms_norm(v, sgu_norm).reshape(b, n, CHUNK, C_HEADS, C_HEAD_DIM)
    mask = jnp.tril(jnp.ones((CHUNK, CHUNK), dtype=w_spatial.dtype))
    ws = w_spatial * mask
    gate = jnp.einsum('hts,bnshc->bnthc', ws, vn) + b_spatial.T[None, None, :, :, None]
    return u * gate.reshape(b, s, C_WIDTH)


def gated_conv_ffn(x, w_gate, w_up, conv_w, conv_b, w_down):
    g = jnp.einsum('bsd,df->bsf', x, w_gate)
    g = causal_dwconv3(g, conv_w) + conv_b
    up = jnp.einsum('bsd,df->bsf', x, w_up)
    return jnp.einsum('bsf,fd->bsd', jax.nn.silu(g) * up, w_down)


def _fwd_setup_inputs(seed: int = 0) -> dict:
    key = jax.random.key(seed)
    ks = jax.random.split(key, 20)
    f32 = jnp.float32
    nrm = lambda k, shape, s: jax.random.normal(k, shape, f32) * s
    return {
        "x": nrm(ks[0], (BATCH, SEQ, D_MODEL), 1.0),
        "norm_mix": 1.0 + nrm(ks[1], (DEPTH, D_MODEL), 0.05),
        "norm_ffn": 1.0 + nrm(ks[2], (DEPTH, D_MODEL), 0.05),
        "final_norm": 1.0 + nrm(ks[3], (D_MODEL,), 0.05),
        "w_in_even": nrm(ks[4], (N_EVEN, D_MODEL, EVEN_IN), D_MODEL ** -0.5),
        "conv_a": nrm(ks[5], (N_EVEN, CONV_WIDTH, A_WIDTH), CONV_WIDTH ** -0.5),
        "w_pool": nrm(ks[6], (N_EVEN, B_GROUPS, B_GROUP_DIM, B_GROUP_DIM), B_GROUP_DIM ** -0.5),
        "pool_scale": 1.0 + nrm(ks[7], (N_EVEN, B_WIDTH), 0.1),
        "w_out_even": nrm(ks[8], (N_EVEN, EVEN_MIX, D_MODEL), EVEN_MIX ** -0.5),
        "w_in_odd": nrm(ks[9], (N_ODD, D_MODEL, 2 * C_WIDTH), D_MODEL ** -0.5),
        "sgu_norm": 1.0 + nrm(ks[10], (N_ODD, C_WIDTH), 0.05),
        "w_spatial": nrm(ks[11], (N_ODD, C_HEADS, CHUNK, CHUNK), CHUNK ** -0.5),
        "b_spatial": 1.0 + nrm(ks[12], (N_ODD, C_HEADS, CHUNK), 0.05),
        "w_out_odd": nrm(ks[13], (N_ODD, C_WIDTH, D_MODEL), C_WIDTH ** -0.5),
        "w_ffn_gate": nrm(ks[14], (DEPTH, D_MODEL, D_FF), D_MODEL ** -0.5),
        "w_ffn_up": nrm(ks[15], (DEPTH, D_MODEL, D_FF), D_MODEL ** -0.5),
        "conv_ffn": nrm(ks[16], (DEPTH, CONV_WIDTH, D_FF), CONV_WIDTH ** -0.5),
        "b_conv_ffn": nrm(ks[17], (DEPTH, D_FF), 0.02),
        "w_ffn_down": nrm(ks[18], (DEPTH, D_FF, D_MODEL), D_FF ** -0.5),
    }


def _fwd_reference(x, norm_mix, norm_ffn, final_norm, w_in_even, conv_a, w_pool, pool_scale,
              w_out_even, w_in_odd, sgu_norm, w_spatial, b_spatial, w_out_odd,
              w_ffn_gate, w_ffn_up, conv_ffn, b_conv_ffn, w_ffn_down):
    h = x
    for layer in range(DEPTH):
        xn = rms_norm(h, norm_mix[layer])
        if layer % 2 == 0:
            i = layer // 2
            proj = jnp.einsum('bsd,de->bse', xn, w_in_even[i])
            a_b = proj[..., 0:A_WIDTH]
            a_c = proj[..., A_WIDTH:2 * A_WIDTH]
            a_v = proj[..., 2 * A_WIDTH:3 * A_WIDTH]
            z_b = proj[..., 3 * A_WIDTH:]
            y_a = short_gated_conv(a_b, a_c, a_v, conv_a[i])
            y_b = multiscale_pool(z_b, w_pool[i], pool_scale[i])
            mix = jnp.concatenate([y_a, y_b], axis=-1)
            h = h + jnp.einsum('bse,ed->bsd', mix, w_out_even[i])
        else:
            i = layer // 2
            proj = jax.nn.gelu(jnp.einsum('bsd,de->bse', xn, w_in_odd[i]), approximate=False)
            u = proj[..., :C_WIDTH]
            v = proj[..., C_WIDTH:]
            mix = chunked_spatial_gating(u, v, sgu_norm[i], w_spatial[i], b_spatial[i])
            h = h + jnp.einsum('bse,ed->bsd', mix, w_out_odd[i])
        hn = rms_norm(h, norm_ffn[layer])
        h = h + gated_conv_ffn(hn, w_ffn_gate[layer], w_ffn_up[layer], conv_ffn[layer],
                               b_conv_ffn[layer], w_ffn_down[layer])
    return rms_norm(h, final_norm)


import jax as _jax
import jax.numpy as _jnp

TWIN_FORMAT = 'train_step'
FWD_PARAMS = ['x', 'norm_mix', 'norm_ffn', 'final_norm', 'w_in_even', 'conv_a', 'w_pool', 'pool_scale', 'w_out_even', 'w_in_odd', 'sgu_norm', 'w_spatial', 'b_spatial', 'w_out_odd', 'w_ffn_gate', 'w_ffn_up', 'conv_ffn', 'b_conv_ffn', 'w_ffn_down']
TWIN_WEIGHTS = ['norm_mix', 'norm_ffn', 'final_norm', 'w_in_even', 'conv_a', 'w_pool', 'pool_scale', 'w_out_even', 'w_in_odd', 'sgu_norm', 'w_spatial', 'b_spatial', 'w_out_odd', 'w_ffn_gate', 'w_ffn_up', 'conv_ffn', 'b_conv_ffn', 'w_ffn_down']
TWIN_DIFF_INPUT = 'x'
TWIN_INPUTS = ['x', 'norm_mix', 'norm_ffn', 'final_norm', 'w_in_even', 'conv_a', 'w_pool', 'pool_scale', 'w_out_even', 'w_in_odd', 'sgu_norm', 'w_spatial', 'b_spatial', 'w_out_odd', 'w_ffn_gate', 'w_ffn_up', 'conv_ffn', 'b_conv_ffn', 'w_ffn_down', 'loss_target', 'm_norm_mix', 'm_norm_ffn', 'm_final_norm', 'm_w_in_even', 'm_conv_a', 'm_w_pool', 'm_pool_scale', 'm_w_out_even', 'm_w_in_odd', 'm_sgu_norm', 'm_w_spatial', 'm_b_spatial', 'm_w_out_odd', 'm_w_ffn_gate', 'm_w_ffn_up', 'm_conv_ffn', 'm_b_conv_ffn', 'm_w_ffn_down', 'v_norm_mix', 'v_norm_ffn', 'v_final_norm', 'v_w_in_even', 'v_conv_a', 'v_w_pool', 'v_pool_scale', 'v_w_out_even', 'v_w_in_odd', 'v_sgu_norm', 'v_w_spatial', 'v_b_spatial', 'v_w_out_odd', 'v_w_ffn_gate', 'v_w_ffn_up', 'v_conv_ffn', 'v_b_conv_ffn', 'v_w_ffn_down']
TWIN_OUTPUTS = ['loss', 'grad_x', 'grad_norm_mix', 'grad_norm_ffn', 'grad_final_norm', 'grad_w_in_even', 'grad_conv_a', 'grad_w_pool', 'grad_pool_scale', 'grad_w_out_even', 'grad_w_in_odd', 'grad_sgu_norm', 'grad_w_spatial', 'grad_b_spatial', 'grad_w_out_odd', 'grad_w_ffn_gate', 'grad_w_ffn_up', 'grad_conv_ffn', 'grad_b_conv_ffn', 'grad_w_ffn_down', 'delta_norm_mix', 'delta_norm_ffn', 'delta_final_norm', 'delta_w_in_even', 'delta_conv_a', 'delta_w_pool', 'delta_pool_scale', 'delta_w_out_even', 'delta_w_in_odd', 'delta_sgu_norm', 'delta_w_spatial', 'delta_b_spatial', 'delta_w_out_odd', 'delta_w_ffn_gate', 'delta_w_ffn_up', 'delta_conv_ffn', 'delta_b_conv_ffn', 'delta_w_ffn_down', 'new_m_norm_mix', 'new_m_norm_ffn', 'new_m_final_norm', 'new_m_w_in_even', 'new_m_conv_a', 'new_m_w_pool', 'new_m_pool_scale', 'new_m_w_out_even', 'new_m_w_in_odd', 'new_m_sgu_norm', 'new_m_w_spatial', 'new_m_b_spatial', 'new_m_w_out_odd', 'new_m_w_ffn_gate', 'new_m_w_ffn_up', 'new_m_conv_ffn', 'new_m_b_conv_ffn', 'new_m_w_ffn_down', 'new_v_norm_mix', 'new_v_norm_ffn', 'new_v_final_norm', 'new_v_w_in_even', 'new_v_conv_a', 'new_v_w_pool', 'new_v_pool_scale', 'new_v_w_out_even', 'new_v_w_in_odd', 'new_v_sgu_norm', 'new_v_w_spatial', 'new_v_b_spatial', 'new_v_w_out_odd', 'new_v_w_ffn_gate', 'new_v_w_ffn_up', 'new_v_conv_ffn', 'new_v_b_conv_ffn', 'new_v_w_ffn_down']
TWIN_LEAF_KINDS = {'loss': 'loss', 'grad_x': 'grad_x', 'grad_norm_mix': 'grad_w', 'grad_norm_ffn': 'grad_w', 'grad_final_norm': 'grad_w', 'grad_w_in_even': 'grad_w', 'grad_conv_a': 'grad_w', 'grad_w_pool': 'grad_w', 'grad_pool_scale': 'grad_w', 'grad_w_out_even': 'grad_w', 'grad_w_in_odd': 'grad_w', 'grad_sgu_norm': 'grad_w', 'grad_w_spatial': 'grad_w', 'grad_b_spatial': 'grad_w', 'grad_w_out_odd': 'grad_w', 'grad_w_ffn_gate': 'grad_w', 'grad_w_ffn_up': 'grad_w', 'grad_conv_ffn': 'grad_w', 'grad_b_conv_ffn': 'grad_w', 'grad_w_ffn_down': 'grad_w', 'delta_norm_mix': 'delta_w', 'delta_norm_ffn': 'delta_w', 'delta_final_norm': 'delta_w', 'delta_w_in_even': 'delta_w', 'delta_conv_a': 'delta_w', 'delta_w_pool': 'delta_w', 'delta_pool_scale': 'delta_w', 'delta_w_out_even': 'delta_w', 'delta_w_in_odd': 'delta_w', 'delta_sgu_norm': 'delta_w', 'delta_w_spatial': 'delta_w', 'delta_b_spatial': 'delta_w', 'delta_w_out_odd': 'delta_w', 'delta_w_ffn_gate': 'delta_w', 'delta_w_ffn_up': 'delta_w', 'delta_conv_ffn': 'delta_w', 'delta_b_conv_ffn': 'delta_w', 'delta_w_ffn_down': 'delta_w', 'new_m_norm_mix': 'new_m', 'new_m_norm_ffn': 'new_m', 'new_m_final_norm': 'new_m', 'new_m_w_in_even': 'new_m', 'new_m_conv_a': 'new_m', 'new_m_w_pool': 'new_m', 'new_m_pool_scale': 'new_m', 'new_m_w_out_even': 'new_m', 'new_m_w_in_odd': 'new_m', 'new_m_sgu_norm': 'new_m', 'new_m_w_spatial': 'new_m', 'new_m_b_spatial': 'new_m', 'new_m_w_out_odd': 'new_m', 'new_m_w_ffn_gate': 'new_m', 'new_m_w_ffn_up': 'new_m', 'new_m_conv_ffn': 'new_m', 'new_m_b_conv_ffn': 'new_m', 'new_m_w_ffn_down': 'new_m', 'new_v_norm_mix': 'new_v', 'new_v_norm_ffn': 'new_v', 'new_v_final_norm': 'new_v', 'new_v_w_in_even': 'new_v', 'new_v_conv_a': 'new_v', 'new_v_w_pool': 'new_v', 'new_v_pool_scale': 'new_v', 'new_v_w_out_even': 'new_v', 'new_v_w_in_odd': 'new_v', 'new_v_sgu_norm': 'new_v', 'new_v_w_spatial': 'new_v', 'new_v_b_spatial': 'new_v', 'new_v_w_out_odd': 'new_v', 'new_v_w_ffn_gate': 'new_v', 'new_v_w_ffn_up': 'new_v', 'new_v_conv_ffn': 'new_v', 'new_v_b_conv_ffn': 'new_v', 'new_v_w_ffn_down': 'new_v'}


def _forward(args):
    return _fwd_reference(*[args[k] for k in FWD_PARAMS])


def _output_shape():
    out = _jax.eval_shape(lambda: _forward(_fwd_setup_inputs(0)))
    return out.shape, out.dtype

N_MICROBATCH = 1
ADAM_LR = 0.001
ADAM_B1 = 0.9
ADAM_B2 = 0.999
ADAM_EPS = 1e-08
ADAM_WD = 0.01
ADAM_STEP = 10
PER_EXAMPLE_BATCH_AXIS = {'x': 0, 'loss_target': 0}
SHARED_INPUTS = []
_WEIGHT_DTYPES = {'norm_mix': _jnp.float32, 'norm_ffn': _jnp.float32, 'final_norm': _jnp.float32, 'w_in_even': _jnp.float32, 'conv_a': _jnp.float32, 'w_pool': _jnp.float32, 'pool_scale': _jnp.float32, 'w_out_even': _jnp.float32, 'w_in_odd': _jnp.float32, 'sgu_norm': _jnp.float32, 'w_spatial': _jnp.float32, 'b_spatial': _jnp.float32, 'w_out_odd': _jnp.float32, 'w_ffn_gate': _jnp.float32, 'w_ffn_up': _jnp.float32, 'conv_ffn': _jnp.float32, 'b_conv_ffn': _jnp.float32, 'w_ffn_down': _jnp.float32}
MOMENT_SCALE = {'norm_mix': 1.689689e-01, 'norm_ffn': 1.060169e-01, 'final_norm': 3.201108e+01, 'w_in_even': 1.532967e-01, 'conv_a': 1.565446e-01, 'w_pool': 1.408487e-01, 'pool_scale': 1.469043e-01, 'w_out_even': 1.489401e-01, 'w_in_odd': 7.522448e-02, 'sgu_norm': 5.100395e-02, 'w_spatial': 5.118431e-02, 'b_spatial': 7.250093e-02, 'w_out_odd': 9.942670e-02, 'w_ffn_gate': 4.432708e-02, 'w_ffn_up': 4.306572e-02, 'conv_ffn': 4.444356e-02, 'b_conv_ffn': 4.309474e-02, 'w_ffn_down': 7.186878e-02}


def _to_microbatches(a, axis):
    t = _jnp.moveaxis(a, axis, 0)
    t = t.reshape((N_MICROBATCH, t.shape[0] // N_MICROBATCH) + t.shape[1:])
    return _jnp.moveaxis(t, 1, axis + 1)


def setup_inputs(seed: int = 0) -> dict:
    inp = _fwd_setup_inputs(seed)
    key = _jax.random.fold_in(_jax.random.key(seed), 7919)
    shape, _ = _output_shape()
    out = dict(inp)
    out["loss_target"] = _jax.random.normal(_jax.random.fold_in(key, 0), shape, _jnp.float32)
    for i, name in enumerate(TWIN_WEIGHTS):
        w = inp[name].astype(_jnp.float32)
        if MOMENT_SCALE is None:
            s = _jnp.sqrt(_jnp.mean(_jnp.square(w)) + 1e-30)
        else:
            s = MOMENT_SCALE[name]
        km, kv = _jax.random.split(_jax.random.fold_in(key, i + 1))
        out[name] = w
        out["m_" + name] = s * _jax.random.normal(km, w.shape, _jnp.float32)
        out["v_" + name] = (s * s) * _jax.random.uniform(kv, w.shape, _jnp.float32, 0.5, 1.5)
    if N_MICROBATCH > 1:
        for name, axis in PER_EXAMPLE_BATCH_AXIS.items():
            out[name] = _to_microbatches(out[name], axis)
    return {'x': out['x'], 'norm_mix': out['norm_mix'], 'norm_ffn': out['norm_ffn'], 'final_norm': out['final_norm'], 'w_in_even': out['w_in_even'], 'conv_a': out['conv_a'], 'w_pool': out['w_pool'], 'pool_scale': out['pool_scale'], 'w_out_even': out['w_out_even'], 'w_in_odd': out['w_in_odd'], 'sgu_norm': out['sgu_norm'], 'w_spatial': out['w_spatial'], 'b_spatial': out['b_spatial'], 'w_out_odd': out['w_out_odd'], 'w_ffn_gate': out['w_ffn_gate'], 'w_ffn_up': out['w_ffn_up'], 'conv_ffn': out['conv_ffn'], 'b_conv_ffn': out['b_conv_ffn'], 'w_ffn_down': out['w_ffn_down'], 'loss_target': out['loss_target'], 'm_norm_mix': out['m_norm_mix'], 'm_norm_ffn': out['m_norm_ffn'], 'm_final_norm': out['m_final_norm'], 'm_w_in_even': out['m_w_in_even'], 'm_conv_a': out['m_conv_a'], 'm_w_pool': out['m_w_pool'], 'm_pool_scale': out['m_pool_scale'], 'm_w_out_even': out['m_w_out_even'], 'm_w_in_odd': out['m_w_in_odd'], 'm_sgu_norm': out['m_sgu_norm'], 'm_w_spatial': out['m_w_spatial'], 'm_b_spatial': out['m_b_spatial'], 'm_w_out_odd': out['m_w_out_odd'], 'm_w_ffn_gate': out['m_w_ffn_gate'], 'm_w_ffn_up': out['m_w_ffn_up'], 'm_conv_ffn': out['m_conv_ffn'], 'm_b_conv_ffn': out['m_b_conv_ffn'], 'm_w_ffn_down': out['m_w_ffn_down'], 'v_norm_mix': out['v_norm_mix'], 'v_norm_ffn': out['v_norm_ffn'], 'v_final_norm': out['v_final_norm'], 'v_w_in_even': out['v_w_in_even'], 'v_conv_a': out['v_conv_a'], 'v_w_pool': out['v_w_pool'], 'v_pool_scale': out['v_pool_scale'], 'v_w_out_even': out['v_w_out_even'], 'v_w_in_odd': out['v_w_in_odd'], 'v_sgu_norm': out['v_sgu_norm'], 'v_w_spatial': out['v_w_spatial'], 'v_b_spatial': out['v_b_spatial'], 'v_w_out_odd': out['v_w_out_odd'], 'v_w_ffn_gate': out['v_w_ffn_gate'], 'v_w_ffn_up': out['v_w_ffn_up'], 'v_conv_ffn': out['v_conv_ffn'], 'v_b_conv_ffn': out['v_b_conv_ffn'], 'v_w_ffn_down': out['v_w_ffn_down']}


def _loss(weights, diff, rest, loss_target):
    with _jax.named_scope("forward"):
        args = {**rest, TWIN_DIFF_INPUT: diff, **{k: w.astype(_WEIGHT_DTYPES[k]) for k, w in weights.items()}}
        y = _forward(args)
    with _jax.named_scope("loss_head"):
        err = _jnp.square(y.astype(_jnp.float32) - loss_target)
        return 0.5 * _jnp.sum(_jnp.mean(err, axis=-1)) if err.ndim else 0.5 * err


def _adamw(w, g, m, v):
    m = ADAM_B1 * m + (1.0 - ADAM_B1) * g
    v = ADAM_B2 * v + (1.0 - ADAM_B2) * _jnp.square(g)
    m_hat = m / (1.0 - ADAM_B1 ** ADAM_STEP)
    v_hat = v / (1.0 - ADAM_B2 ** ADAM_STEP)
    delta = -ADAM_LR * (m_hat / (_jnp.sqrt(v_hat) + ADAM_EPS) + ADAM_WD * w)
    return delta, m, v


def reference(x, norm_mix, norm_ffn, final_norm, w_in_even, conv_a, w_pool, pool_scale, w_out_even, w_in_odd, sgu_norm, w_spatial, b_spatial, w_out_odd, w_ffn_gate, w_ffn_up, conv_ffn, b_conv_ffn, w_ffn_down, loss_target, m_norm_mix, m_norm_ffn, m_final_norm, m_w_in_even, m_conv_a, m_w_pool, m_pool_scale, m_w_out_even, m_w_in_odd, m_sgu_norm, m_w_spatial, m_b_spatial, m_w_out_odd, m_w_ffn_gate, m_w_ffn_up, m_conv_ffn, m_b_conv_ffn, m_w_ffn_down, v_norm_mix, v_norm_ffn, v_final_norm, v_w_in_even, v_conv_a, v_w_pool, v_pool_scale, v_w_out_even, v_w_in_odd, v_sgu_norm, v_w_spatial, v_b_spatial, v_w_out_odd, v_w_ffn_gate, v_w_ffn_up, v_conv_ffn, v_b_conv_ffn, v_w_ffn_down):
    given = dict(x=x, norm_mix=norm_mix, norm_ffn=norm_ffn, final_norm=final_norm, w_in_even=w_in_even, conv_a=conv_a, w_pool=w_pool, pool_scale=pool_scale, w_out_even=w_out_even, w_in_odd=w_in_odd, sgu_norm=sgu_norm, w_spatial=w_spatial, b_spatial=b_spatial, w_out_odd=w_out_odd, w_ffn_gate=w_ffn_gate, w_ffn_up=w_ffn_up, conv_ffn=conv_ffn, b_conv_ffn=b_conv_ffn, w_ffn_down=w_ffn_down, loss_target=loss_target, m_norm_mix=m_norm_mix, m_norm_ffn=m_norm_ffn, m_final_norm=m_final_norm, m_w_in_even=m_w_in_even, m_conv_a=m_conv_a, m_w_pool=m_w_pool, m_pool_scale=m_pool_scale, m_w_out_even=m_w_out_even, m_w_in_odd=m_w_in_odd, m_sgu_norm=m_sgu_norm, m_w_spatial=m_w_spatial, m_b_spatial=m_b_spatial, m_w_out_odd=m_w_out_odd, m_w_ffn_gate=m_w_ffn_gate, m_w_ffn_up=m_w_ffn_up, m_conv_ffn=m_conv_ffn, m_b_conv_ffn=m_b_conv_ffn, m_w_ffn_down=m_w_ffn_down, v_norm_mix=v_norm_mix, v_norm_ffn=v_norm_ffn, v_final_norm=v_final_norm, v_w_in_even=v_w_in_even, v_conv_a=v_conv_a, v_w_pool=v_w_pool, v_pool_scale=v_pool_scale, v_w_out_even=v_w_out_even, v_w_in_odd=v_w_in_odd, v_sgu_norm=v_sgu_norm, v_w_spatial=v_w_spatial, v_b_spatial=v_b_spatial, v_w_out_odd=v_w_out_odd, v_w_ffn_gate=v_w_ffn_gate, v_w_ffn_up=v_w_ffn_up, v_conv_ffn=v_conv_ffn, v_b_conv_ffn=v_b_conv_ffn, v_w_ffn_down=v_w_ffn_down)
    weights = {n: given[n] for n in TWIN_WEIGHTS}
    shared = {n: given[n] for n in SHARED_INPUTS}
    per_example = {n: given[n] for n in ['x']}
    grad_fn = _jax.value_and_grad(_loss, argnums=(0, 1))

    def one_microbatch(ex, loss_target):
        ex = dict(ex)
        diff = ex.pop(TWIN_DIFF_INPUT)
        return grad_fn(weights, diff, {**shared, **ex}, loss_target)

    if N_MICROBATCH == 1:
        loss, (grad_w, grad_x) = one_microbatch(per_example, given["loss_target"])
    else:
        def body(carry, xs):
            loss_sum, grad_sum = carry
            l_k, (gw_k, gx_k) = one_microbatch(xs[0], xs[1])
            with _jax.named_scope("update"):
                return (loss_sum + l_k, _jax.tree.map(_jnp.add, grad_sum, gw_k)), gx_k

        init = (_jnp.zeros((), _jnp.float32), _jax.tree.map(_jnp.zeros_like, weights))
        (loss, grad_w), grad_x = _jax.lax.scan(body, init, (per_example, given["loss_target"]))
    with _jax.named_scope("update"):
        delta_w, new_m, new_v = {}, {}, {}
        for n in TWIN_WEIGHTS:
            delta_w[n], new_m[n], new_v[n] = _adamw(weights[n], grad_w[n], given["m_" + n], given["v_" + n])
    return (loss, grad_x, *[grad_w[n] for n in TWIN_WEIGHTS], *[delta_w[n] for n in TWIN_WEIGHTS],
            *[new_m[n] for n in TWIN_WEIGHTS], *[new_v[n] for n in TWIN_WEIGHTS])
```

```python
import jax
import jax.numpy as jnp
from jax import lax
from jax.experimental import pallas as pl
from jax.experimental.pallas import tpu as pltpu

F32 = jnp.float32
BF16 = jnp.bfloat16
MESH = pl.DeviceIdType.MESH
ANY = pl.BlockSpec(memory_space=pl.ANY)

EPS = 1e-6
POOL_WINDOWS = (2, 4, 8, 16)
GROUP = 128
CHUNK = 128
N_CHIPS = 4
N_DEV = 8
ROW_TILE = 512
HALO = 16
VMEM_LIMIT = 56 * 1024 * 1024

ADAM_LR = 0.001
ADAM_B1 = 0.9
ADAM_B2 = 0.999
ADAM_EPS = 1e-08
ADAM_WD = 0.01
ADAM_STEP = 10


def _params(*sem):
    return pltpu.CompilerParams(dimension_semantics=sem, vmem_limit_bytes=VMEM_LIMIT)


def _layer_spec(block, l, idx):
    if l is None:
        return pl.BlockSpec(block, idx)
    return pl.BlockSpec((None,) + block, lambda *g: (l,) + idx(*g))


def mm_nn(a, b, *, l=None, name):
    s, k = a.shape
    j_n, n = b.shape[-3], b.shape[-1]
    tm = min(ROW_TILE, s)

    def body(a_ref, b_ref, o_ref):
        o_ref[...] = jnp.dot(a_ref[...], b_ref[...], preferred_element_type=F32)

    return pl.pallas_call(
        body, name=name, grid=(j_n, s // tm),
        in_specs=[pl.BlockSpec((tm, k), lambda j, i: (i, 0)),
                  _layer_spec((None, k, n), l, lambda j, i: (j, 0, 0))],
        out_specs=pl.BlockSpec((None, tm, n), lambda j, i: (j, i, 0)),
        out_shape=jax.ShapeDtypeStruct((j_n, s, n), F32),
        compiler_params=_params("parallel", "parallel"))(a, b)


def mm_acc(a, b, res, *, l=None, name):
    j_n, s, kj = a.shape
    n = b.shape[-1]
    tm = min(ROW_TILE, s)

    def body(a_ref, b_ref, r_ref, o_ref):
        acc = r_ref[...]
        for j in range(j_n):
            acc = acc + jnp.dot(a_ref[j], b_ref[j], preferred_element_type=F32)
        o_ref[...] = acc

    return pl.pallas_call(
        body, name=name, grid=(s // tm,),
        in_specs=[pl.BlockSpec((j_n, tm, kj), lambda i: (0, i, 0)),
                  _layer_spec((j_n, kj, n), l, lambda i: (0, 0, 0)),
                  pl.BlockSpec((tm, n), lambda i: (i, 0))],
        out_specs=pl.BlockSpec((tm, n), lambda i: (i, 0)),
        out_shape=jax.ShapeDtypeStruct((s, n), F32),
        compiler_params=_params("parallel"))(a, b, res)


_NT = (((1,), (1,)), ((), ()))
_TN = (((0,), (0,)), ((), ()))


def mm_nt_sum(pairs, *, l=None, name):
    j_n, s, nj = pairs[0][0].shape
    k = pairs[0][1].shape[-2]
    tm = min(ROW_TILE, s)
    n_p = len(pairs)

    def body(*refs):
        o_ref = refs[-1]
        acc = jnp.zeros((tm, k), F32)
        for p in range(n_p):
            dy_ref, w_ref = refs[2 * p], refs[2 * p + 1]
            for j in range(j_n):
                acc = acc + lax.dot_general(dy_ref[j], w_ref[j], _NT, preferred_element_type=F32)
        o_ref[...] = acc

    in_specs, args = [], []
    for dy, w in pairs:
        in_specs += [pl.BlockSpec((j_n, tm, nj), lambda i: (0, i, 0)),
                     _layer_spec((j_n, k, nj), l, lambda i: (0, 0, 0))]
        args += [dy, w]
    return pl.pallas_call(
        body, name=name, grid=(s // tm,), in_specs=in_specs,
        out_specs=pl.BlockSpec((tm, k), lambda i: (i, 0)),
        out_shape=jax.ShapeDtypeStruct((s, k), F32),
        compiler_params=_params("parallel"))(*args)


def mm_nt_each(a, b, *, l=None, name):
    s, k = a.shape
    j_n, nj = b.shape[-3], b.shape[-2]
    tm = min(ROW_TILE, s)

    def body(a_ref, b_ref, o_ref):
        o_ref[...] = lax.dot_general(a_ref[...], b_ref[...], _NT, preferred_element_type=F32)

    return pl.pallas_call(
        body, name=name, grid=(j_n, s // tm),
        in_specs=[pl.BlockSpec((tm, k), lambda j, i: (i, 0)),
                  _layer_spec((None, nj, k), l, lambda j, i: (j, 0, 0))],
        out_specs=pl.BlockSpec((None, tm, nj), lambda j, i: (j, i, 0)),
        out_shape=jax.ShapeDtypeStruct((j_n, s, nj), F32),
        compiler_params=_params("parallel", "parallel"))(a, b)


def mm_tn(a, dy, *, name):
    ja, s, k = a.shape
    jd, _, n = dy.shape
    j_n = max(ja, jd)
    tk = min(ROW_TILE, s)

    def body(a_ref, d_ref, o_ref):
        @pl.when(pl.program_id(1) == 0)
        def _():
            o_ref[...] = jnp.zeros_like(o_ref)
        o_ref[...] += lax.dot_general(a_ref[...], d_ref[...], _TN, preferred_element_type=F32)

    return pl.pallas_call(
        body, name=name, grid=(j_n, s // tk),
        in_specs=[pl.BlockSpec((None, tk, k), (lambda j, i: (j, i, 0)) if ja > 1 else (lambda j, i: (0, i, 0))),
                  pl.BlockSpec((None, tk, n), (lambda j, i: (j, i, 0)) if jd > 1 else (lambda j, i: (0, i, 0)))],
        out_specs=pl.BlockSpec((None, k, n), lambda j, i: (j, 0, 0)),
        out_shape=jax.ShapeDtypeStruct((j_n, k, n), F32),
        compiler_params=_params("parallel", "arbitrary"))(a, dy)


def _back(x, k):
    return pltpu.roll(x, k, 0)


def _fwd(x, k):
    return pltpu.roll(x, x.shape[0] - k, 0)


def _causal_conv(x, w_ref):
    return w_ref[0:1, :] * _back(x, 2) + w_ref[1:2, :] * _back(x, 1) + w_ref[2:3, :] * x


def _causal_conv_t(dy, w_ref):
    return w_ref[2:3, :] * dy + w_ref[1:2, :] * _fwd(dy, 1) + w_ref[0:1, :] * _fwd(dy, 2)


def _gelu(x):
    return 0.5 * x * (1.0 + lax.erf(x * 0.7071067811865476))


def _gelu_grad(x):
    return 0.5 * (1.0 + lax.erf(x * 0.7071067811865476)) + x * jnp.exp(-0.5 * x * x) * 0.3989422804014327


def _colsum(x):
    return jnp.sum(x, axis=0, keepdims=True)


def rms_fwd(h, gain, *, name):
    s, d = h.shape
    ts = min(ROW_TILE, s)

    def body(h_ref, g_ref, o_ref, r_ref):
        x = h_ref[...]
        rstd = lax.rsqrt(jnp.mean(x * x, axis=-1, keepdims=True) + EPS)
        o_ref[...] = (x * rstd * g_ref[...]).astype(BF16)
        r_ref[...] = rstd

    return pl.pallas_call(
        body, name=name, grid=(s // ts,),
        in_specs=[pl.BlockSpec((ts, d), lambda i: (i, 0)), pl.BlockSpec((1, d), lambda i: (0, 0))],
        out_specs=[pl.BlockSpec((ts, d), lambda i: (i, 0)), pl.BlockSpec((ts, 1), lambda i: (i, 0))],
        out_shape=[jax.ShapeDtypeStruct((s, d), BF16), jax.ShapeDtypeStruct((s, 1), F32)],
        compiler_params=_params("parallel"))(h, gain)


def rms_bwd(dxn, h, gain, rstd, dres, *, name):
    s, d = h.shape
    ts = min(ROW_TILE, s)

    def body(dx_ref, h_ref, g_ref, r_ref, dr_ref, o_ref, ob_ref, dg_ref):
        @pl.when(pl.program_id(0) == 0)
        def _():
            dg_ref[...] = jnp.zeros_like(dg_ref)
        rstd_v = r_ref[...]
        hhat = h_ref[...] * rstd_v
        dx = dx_ref[...]
        dg_ref[...] += _colsum(dx * hhat)
        dxg = dx * g_ref[...]
        dh = dr_ref[...] + rstd_v * (dxg - hhat * jnp.mean(dxg * hhat, axis=-1, keepdims=True))
        o_ref[...] = dh
        ob_ref[...] = dh.astype(BF16)

    row = pl.BlockSpec((ts, d), lambda i: (i, 0))
    vec = pl.BlockSpec((1, d), lambda i: (0, 0))
    return pl.pallas_call(
        body, name=name, grid=(s // ts,),
        in_specs=[row, row, vec, pl.BlockSpec((ts, 1), lambda i: (i, 0)), row],
        out_specs=[row, row, vec],
        out_shape=[jax.ShapeDtypeStruct((s, d), F32), jax.ShapeDtypeStruct((s, d), BF16),
                   jax.ShapeDtypeStruct((1, d), F32)],
        compiler_params=_params("arbitrary"))(dxn, h, gain, rstd, dres)


def final_loss(h, target, gain, *, name):
    s, d = h.shape
    ts = min(ROW_TILE, s)

    def body(h_ref, t_ref, g_ref, o_ref, ob_ref, l_ref, dg_ref):
        @pl.when(pl.program_id(0) == 0)
        def _():
            l_ref[...] = jnp.zeros_like(l_ref)
            dg_ref[...] = jnp.zeros_like(dg_ref)
        x = h_ref[...]
        rstd = lax.rsqrt(jnp.mean(x * x, axis=-1, keepdims=True) + EPS)
        hhat = x * rstd
        err = hhat * g_ref[...] - t_ref[...]
        l_ref[...] += 0.5 * jnp.sum(jnp.mean(err * err, axis=-1, keepdims=True), axis=0, keepdims=True)
        dy = err * (1.0 / d)
        dg_ref[...] += _colsum(dy * hhat)
        dyg = dy * g_ref[...]
        dh = rstd * (dyg - hhat * jnp.mean(dyg * hhat, axis=-1, keepdims=True))
        o_ref[...] = dh
        ob_ref[...] = dh.astype(BF16)

    row = pl.BlockSpec((ts, d), lambda i: (i, 0))
    vec = pl.BlockSpec((1, d), lambda i: (0, 0))
    return pl.pallas_call(
        body, name=name, grid=(s // ts,),
        in_specs=[row, row, vec],
        out_specs=[row, row, pl.BlockSpec((1, 128), lambda i: (0, 0)), vec],
        out_shape=[jax.ShapeDtypeStruct((s, d), F32), jax.ShapeDtypeStruct((s, d), BF16),
                   jax.ShapeDtypeStruct((1, 128), F32), jax.ShapeDtypeStruct((1, d), F32)],
        compiler_params=_params("arbitrary"))(h, target, gain)


def _halo_specs(n_lead, ts, width, n_tiles):
    hb = ts // HALO
    prev = pl.BlockSpec((n_lead, HALO, width), lambda i: (0, jnp.maximum(i * hb - 1, 0), 0))
    nxt = pl.BlockSpec((n_lead, HALO, width), lambda i: (0, jnp.minimum((i + 1) * hb, n_tiles * hb - 1), 0))
    return prev, nxt


def _pool_fwd(z_ext, g, pos):
    w = POOL_WINDOWS[g]
    zg = z_ext[:, g * GROUP:(g + 1) * GROUP]
    acc = zg
    sh = 1
    while sh < w:
        acc = acc + _back(acc, sh)
        sh *= 2
    return acc[HALO:] / jnp.minimum(pos, float(w)) - zg[HALO:]


def even_fwd(proj, conv_a, w_pool, pool_scale, *, name):
    _, s, w = proj.shape
    ts = min(ROW_TILE, s)
    n_t = s // ts
    prev, _ = _halo_specs(4, ts, w, n_t)

    def body(p_ref, ph_ref, ca_ref, wp_ref, ps_ref, o_ref):
        i = pl.program_id(0)
        keep = jnp.where(i > 0, 1.0, 0.0)
        cv_ext = jnp.concatenate([ph_ref[1] * ph_ref[2] * keep, p_ref[1] * p_ref[2]], axis=0)
        o_ref[:, 0:w] = (p_ref[0] * _causal_conv(cv_ext, ca_ref)[HALO:]).astype(BF16)
        z_ext = jnp.concatenate([ph_ref[3] * keep, p_ref[3]], axis=0)
        pos = (i * ts + lax.broadcasted_iota(jnp.int32, (ts, 1), 0) + 1).astype(F32)
        for g in range(len(POOL_WINDOWS)):
            pooled = _pool_fwd(z_ext, g, pos)
            mixed = jnp.dot(pooled.astype(BF16), wp_ref[g], preferred_element_type=F32)
            cols = slice(g * GROUP, (g + 1) * GROUP)
            o_ref[:, w + g * GROUP:w + (g + 1) * GROUP] = (mixed * ps_ref[:, cols]).astype(BF16)

    return pl.pallas_call(
        body, name=name, grid=(n_t,),
        in_specs=[pl.BlockSpec((4, ts, w), lambda i: (0, i, 0)), prev,
                  pl.BlockSpec((3, w), lambda i: (0, 0)),
                  pl.BlockSpec((4, GROUP, GROUP), lambda i: (0, 0, 0)),
                  pl.BlockSpec((1, w), lambda i: (0, 0))],
        out_specs=pl.BlockSpec((ts, 2 * w), lambda i: (i, 0)),
        out_shape=jax.ShapeDtypeStruct((s, 2 * w), BF16),
        compiler_params=_params("parallel"))(proj, proj, conv_a, w_pool, pool_scale)


def even_bwd(proj, dmix, conv_a, w_pool, w_pool_t, pool_scale, *, name):
    _, s, w = proj.shape
    ts = min(ROW_TILE, s)
    n_t = s // ts
    prev, nxt = _halo_specs(4, ts, w, n_t)
    hb = ts // HALO
    dm_next = pl.BlockSpec((HALO, 2 * w), lambda i: (jnp.minimum((i + 1) * hb, n_t * hb - 1), 0))
    n_ext = ts + HALO

    def body(p_ref, pp_ref, pn_ref, dm_ref, dmn_ref, ca_ref, wp_ref, wpt_ref, ps_ref,
             dp_ref, dca_ref, dwp_ref, dps_ref):
        i = pl.program_id(0)

        @pl.when(i == 0)
        def _():
            dca_ref[...] = jnp.zeros_like(dca_ref)
            dwp_ref[...] = jnp.zeros_like(dwp_ref)
            dps_ref[...] = jnp.zeros_like(dps_ref)

        keep_p = jnp.where(i > 0, 1.0, 0.0)
        keep_n = jnp.where(i < n_t - 1, 1.0, 0.0)
        a_b, a_c, a_v = p_ref[0], p_ref[1], p_ref[2]
        cv_ext = jnp.concatenate([pp_ref[1] * pp_ref[2] * keep_p, a_c * a_v], axis=0)
        dy_a = dm_ref[:, 0:w]
        dp_ref[0] = (dy_a * _causal_conv(cv_ext, ca_ref)[HALO:]).astype(BF16)
        dcc = dy_a * a_b
        dca_ref[2:3, :] += _colsum(dcc * cv_ext[HALO:])
        dca_ref[1:2, :] += _colsum(dcc * _back(cv_ext, 1)[HALO:])
        dca_ref[0:1, :] += _colsum(dcc * _back(cv_ext, 2)[HALO:])
        dcc_ext = jnp.concatenate([dcc, dmn_ref[:, 0:w] * pn_ref[0] * keep_n], axis=0)
        dcv = _causal_conv_t(dcc_ext, ca_ref)[:ts]
        dp_ref[1] = (dcv * a_v).astype(BF16)
        dp_ref[2] = (dcv * a_c).astype(BF16)
        z_ext = jnp.concatenate([pp_ref[3] * keep_p, p_ref[3]], axis=0)
        pos = (i * ts + lax.broadcasted_iota(jnp.int32, (ts, 1), 0) + 1).astype(F32)
        pos_ext = (i * ts + lax.broadcasted_iota(jnp.int32, (n_ext, 1), 0) + 1).astype(F32)
        for g, win in enumerate(POOL_WINDOWS):
            cols = slice(g * GROUP, (g + 1) * GROUP)
            ycols = slice(w + g * GROUP, w + (g + 1) * GROUP)
            pooled = _pool_fwd(z_ext, g, pos).astype(BF16)
            mixed = jnp.dot(pooled, wp_ref[g], preferred_element_type=F32)
            dy_b = dm_ref[:, ycols]
            dps_ref[:, cols] += _colsum(dy_b * mixed)
            dmixed_ext = jnp.concatenate([dy_b, dmn_ref[:, ycols] * keep_n], axis=0) * ps_ref[:, cols]
            dmixed_ext = dmixed_ext.astype(BF16)
            dwp_ref[g] += lax.dot_general(pooled, dmixed_ext[:ts], _TN, preferred_element_type=F32)
            dpooled = jnp.dot(dmixed_ext, wpt_ref[g], preferred_element_type=F32)
            acc = dpooled / jnp.minimum(pos_ext, float(win))
            sh = 1
            while sh < win:
                acc = acc + _fwd(acc, sh)
                sh *= 2
            dp_ref[3, :, cols] = (acc[:ts] - dpooled[:ts]).astype(BF16)

    tile4 = pl.BlockSpec((4, ts, w), lambda i: (0, i, 0))
    const = lambda shape: pl.BlockSpec(shape, lambda i: (0,) * len(shape))
    return pl.pallas_call(
        body, name=name, grid=(n_t,),
        in_specs=[tile4, prev, nxt, pl.BlockSpec((ts, 2 * w), lambda i: (i, 0)), dm_next,
                  const((3, w)), const((4, GROUP, GROUP)), const((4, GROUP, GROUP)), const((1, w))],
        out_specs=[tile4, const((3, w)), const((4, GROUP, GROUP)), const((1, w))],
        out_shape=[jax.ShapeDtypeStruct((4, s, w), BF16), jax.ShapeDtypeStruct((3, w), F32),
                   jax.ShapeDtypeStruct((4, GROUP, GROUP), F32), jax.ShapeDtypeStruct((1, w), F32)],
        compiler_params=_params("arbitrary"))(proj, proj, proj, dmix, dmix, conv_a, w_pool, w_pool_t, pool_scale)


def _ffn_halo(ts, f, n_t, nxt):
    hb = ts // HALO
    if nxt:
        return pl.BlockSpec((None, HALO, f), lambda j, i: (j, jnp.minimum((i + 1) * hb, n_t * hb - 1), 0))
    return pl.BlockSpec((None, HALO, f), lambda j, i: (j, jnp.maximum(i * hb - 1, 0), 0))


def ffn_act_fwd(g, up, cw, cb, *, l, name):
    j_n, s, f = g.shape
    ts = min(ROW_TILE, s)
    n_t = s // ts

    def body(g_ref, gp_ref, u_ref, cw_ref, cb_ref, o_ref):
        keep = jnp.where(pl.program_id(1) > 0, 1.0, 0.0)
        g_ext = jnp.concatenate([gp_ref[...] * keep, g_ref[...]], axis=0)
        gc = _causal_conv(g_ext, cw_ref)[HALO:] + cb_ref[...]
        o_ref[...] = (gc * jax.nn.sigmoid(gc) * u_ref[...]).astype(BF16)

    tile = pl.BlockSpec((None, ts, f), lambda j, i: (j, i, 0))
    return pl.pallas_call(
        body, name=name, grid=(j_n, n_t),
        in_specs=[tile, _ffn_halo(ts, f, n_t, False), tile,
                  pl.BlockSpec((None, None, 3, f), lambda j, i: (l, j, 0, 0)),
                  pl.BlockSpec((None, None, 1, f), lambda j, i: (l, j, 0, 0))],
        out_specs=tile,
        out_shape=jax.ShapeDtypeStruct((j_n, s, f), BF16),
        compiler_params=_params("parallel", "parallel"))(g, g, up, cw, cb)


def ffn_act_bwd(g, up, dact, cw, cb, *, l, name):
    j_n, s, f = g.shape
    ts = min(ROW_TILE, s)
    n_t = s // ts

    def body(g_ref, gp_ref, gn_ref, u_ref, un_ref, d_ref, dn_ref, cw_ref, cb_ref,
             dg_ref, du_ref, dcw_ref, dcb_ref):
        i = pl.program_id(1)

        @pl.when(i == 0)
        def _():
            dcw_ref[...] = jnp.zeros_like(dcw_ref)
            dcb_ref[...] = jnp.zeros_like(dcb_ref)

        keep_p = jnp.where(i > 0, 1.0, 0.0)
        keep_n = jnp.where(i < n_t - 1, 1.0, 0.0)
        g_ext = jnp.concatenate([gp_ref[...] * keep_p, g_ref[...], gn_ref[...]], axis=0)
        gc = _causal_conv(g_ext, cw_ref)[HALO:] + cb_ref[...]
        sig = jax.nn.sigmoid(gc)
        dact_ext = jnp.concatenate([d_ref[...], dn_ref[...] * keep_n], axis=0)
        du_ref[...] = (dact_ext * gc * sig)[:ts].astype(BF16)
        up_ext = jnp.concatenate([u_ref[...], un_ref[...]], axis=0)
        dgc = dact_ext * up_ext * (sig * (1.0 + gc * (1.0 - sig)))
        dg_ref[...] = _causal_conv_t(dgc, cw_ref)[:ts].astype(BF16)
        dgc_t = dgc[:ts]
        dcb_ref[...] += _colsum(dgc_t)
        dcw_ref[2:3, :] += _colsum(dgc_t * g_ext[HALO:HALO + ts])
        dcw_ref[1:2, :] += _colsum(dgc_t * _back(g_ext, 1)[HALO:HALO + ts])
        dcw_ref[0:1, :] += _colsum(dgc_t * _back(g_ext, 2)[HALO:HALO + ts])

    tile = pl.BlockSpec((None, ts, f), lambda j, i: (j, i, 0))
    prev, nxt = _ffn_halo(ts, f, n_t, False), _ffn_halo(ts, f, n_t, True)
    return pl.pallas_call(
        body, name=name, grid=(j_n, n_t),
        in_specs=[tile, prev, nxt, tile, nxt, tile, nxt,
                  pl.BlockSpec((None, None, 3, f), lambda j, i: (l, j, 0, 0)),
                  pl.BlockSpec((None, None, 1, f), lambda j, i: (l, j, 0, 0))],
        out_specs=[tile, tile, pl.BlockSpec((None, 3, f), lambda j, i: (j, 0, 0)),
                   pl.BlockSpec((None, 1, f), lambda j, i: (j, 0, 0))],
        out_shape=[jax.ShapeDtypeStruct((j_n, s, f), BF16), jax.ShapeDtypeStruct((j_n, s, f), BF16),
                   jax.ShapeDtypeStruct((j_n, 3, f), F32), jax.ShapeDtypeStruct((j_n, 1, f), F32)],
        compiler_params=_params("parallel", "arbitrary"))(g, g, g, up, up, dact, dact, cw, cb)


def _sgu_gate(vn_bf, ws_ref, bs_ref, h, rows):
    tri = lax.broadcasted_iota(jnp.int32, (CHUNK, CHUNK), 0) >= lax.broadcasted_iota(jnp.int32, (CHUNK, CHUNK), 1)
    ws = jnp.where(tri, ws_ref[h], 0.0).astype(BF16)
    cols = slice((h % 4) * GROUP, (h % 4 + 1) * GROUP)
    return ws, jnp.dot(ws, vn_bf[h // 4][rows, cols], preferred_element_type=F32) + bs_ref[h]


def sgu_fwd(p, sgu_norm, w_spatial, b_spatial, *, name):
    _, s, w = p.shape
    ts = min(ROW_TILE, s)
    n_heads = w_spatial.shape[0]

    def body(p_ref, n_ref, ws_ref, bs_ref, o_ref, r_ref):
        v = [_gelu(p_ref[2]), _gelu(p_ref[3])]
        ms = (jnp.sum(v[0] * v[0], axis=-1, keepdims=True) + jnp.sum(v[1] * v[1], axis=-1, keepdims=True)) / (2 * w)
        rstd = lax.rsqrt(ms + EPS)
        r_ref[...] = rstd
        vn = [(v[k] * rstd * n_ref[:, k * w:(k + 1) * w]).astype(BF16) for k in range(2)]
        for h in range(n_heads):
            cols = slice((h % 4) * GROUP, (h % 4 + 1) * GROUP)
            for c in range(ts // CHUNK):
                rows = slice(c * CHUNK, (c + 1) * CHUNK)
                _, gate = _sgu_gate(vn, ws_ref, bs_ref, h, rows)
                u = _gelu(p_ref[h // 4, rows, cols])
                o_ref[rows, h * GROUP:(h + 1) * GROUP] = (u * gate).astype(BF16)

    const = lambda shape: pl.BlockSpec(shape, lambda i: (0,) * len(shape))
    return pl.pallas_call(
        body, name=name, grid=(s // ts,),
        in_specs=[pl.BlockSpec((4, ts, w), lambda i: (0, i, 0)), const((1, 2 * w)),
                  const((n_heads, CHUNK, CHUNK)), const((n_heads, CHUNK, 1))],
        out_specs=[pl.BlockSpec((ts, 2 * w), lambda i: (i, 0)), pl.BlockSpec((ts, 1), lambda i: (i, 0))],
        out_shape=[jax.ShapeDtypeStruct((s, 2 * w), BF16), jax.ShapeDtypeStruct((s, 1), F32)],
        compiler_params=_params("parallel"))(p, sgu_norm, w_spatial, b_spatial)


def sgu_bwd(p, dmix, rstd, sgu_norm, w_spatial, b_spatial, *, name):
    _, s, w = p.shape
    ts = min(ROW_TILE, s)
    n_heads = w_spatial.shape[0]

    def body(p_ref, dm_ref, r_ref, n_ref, ws_ref, bs_ref, dp_ref, dn_ref, dws_ref, dbs_ref, dvn_ref):
        @pl.when(pl.program_id(0) == 0)
        def _():
            dn_ref[...] = jnp.zeros_like(dn_ref)
            dws_ref[...] = jnp.zeros_like(dws_ref)
            dbs_ref[...] = jnp.zeros_like(dbs_ref)

        rstd_v = r_ref[...]
        vhat = [_gelu(p_ref[2 + k]) * rstd_v for k in range(2)]
        vn = [(vhat[k] * n_ref[:, k * w:(k + 1) * w]).astype(BF16) for k in range(2)]
        tri = lax.broadcasted_iota(jnp.int32, (CHUNK, CHUNK), 0) >= lax.broadcasted_iota(jnp.int32, (CHUNK, CHUNK), 1)
        for h in range(n_heads):
            cols = slice((h % 4) * GROUP, (h % 4 + 1) * GROUP)
            ocols = slice(h * GROUP, (h + 1) * GROUP)
            for c in range(ts // CHUNK):
                rows = slice(c * CHUNK, (c + 1) * CHUNK)
                ws, gate = _sgu_gate(vn, ws_ref, bs_ref, h, rows)
                pu = p_ref[h // 4, rows, cols]
                dm = dm_ref[rows, ocols]
                dp_ref[h // 4, rows, cols] = (dm * gate * _gelu_grad(pu)).astype(BF16)
                dgate = dm * _gelu(pu)
                dbs_ref[h] += jnp.sum(dgate, axis=-1, keepdims=True)
                dgate_bf = dgate.astype(BF16)
                dws = lax.dot_general(dgate_bf, vn[h // 4][rows, cols], _NT, preferred_element_type=F32)
                dws_ref[h] += jnp.where(tri, dws, 0.0)
                dvn_ref[rows, ocols] = lax.dot_general(ws, dgate_bf, _TN, preferred_element_type=F32)
        for k in range(2):
            kc = slice(k * w, (k + 1) * w)
            dvn = dvn_ref[:, kc]
            dn_ref[:, kc] += _colsum(dvn * vhat[k])
        dvh = [dvn_ref[:, k * w:(k + 1) * w] * n_ref[:, k * w:(k + 1) * w] for k in range(2)]
        dot = (jnp.sum(dvh[0] * vhat[0], axis=-1, keepdims=True)
               + jnp.sum(dvh[1] * vhat[1], axis=-1, keepdims=True)) / (2 * w)
        for k in range(2):
            dv = rstd_v * (dvh[k] - vhat[k] * dot)
            dp_ref[2 + k] = (dv * _gelu_grad(p_ref[2 + k])).astype(BF16)

    const = lambda shape: pl.BlockSpec(shape, lambda i: (0,) * len(shape))
    tile4 = pl.BlockSpec((4, ts, w), lambda i: (0, i, 0))
    return pl.pallas_call(
        body, name=name, grid=(s // ts,),
        in_specs=[tile4, pl.BlockSpec((ts, 2 * w), lambda i: (i, 0)), pl.BlockSpec((ts, 1), lambda i: (i, 0)),
                  const((1, 2 * w)), const((n_heads, CHUNK, CHUNK)), const((n_heads, CHUNK, 1))],
        out_specs=[tile4, const((1, 2 * w)), const((n_heads, CHUNK, CHUNK)), const((n_heads, CHUNK, 1))],
        out_shape=[jax.ShapeDtypeStruct((4, s, w), BF16), jax.ShapeDtypeStruct((1, 2 * w), F32),
                   jax.ShapeDtypeStruct((n_heads, CHUNK, CHUNK), F32),
                   jax.ShapeDtypeStruct((n_heads, CHUNK, 1), F32)],
        scratch_shapes=[pltpu.VMEM((ts, 2 * w), F32)],
        compiler_params=_params("arbitrary"))(p, dmix, rstd, sgu_norm, w_spatial, b_spatial)


def _row_tile(rows):
    if rows <= ROW_TILE:
        return rows
    for t in (512, 384, 352, 256, 128, 64, 32, 16, 8):
        if rows % t == 0:
            return t
    return rows


def adamw(w, g, m, v, *, name):
    shape = w.shape
    cols = shape[-1]
    rows = w.size // cols
    w2, g2, m2, v2 = (a.reshape(rows, cols) for a in (w, g, m, v))
    tr = _row_tile(rows)
    bc1 = 1.0 - ADAM_B1 ** ADAM_STEP
    bc2 = 1.0 - ADAM_B2 ** ADAM_STEP

    def body(w_ref, g_ref, m_ref, v_ref, d_ref, nm_ref, nv_ref):
        grad = g_ref[...]
        m_new = ADAM_B1 * m_ref[...] + (1.0 - ADAM_B1) * grad
        v_new = ADAM_B2 * v_ref[...] + (1.0 - ADAM_B2) * (grad * grad)
        nm_ref[...] = m_new
        nv_ref[...] = v_new
        d_ref[...] = -ADAM_LR * ((m_new / bc1) / (jnp.sqrt(v_new / bc2) + ADAM_EPS) + ADAM_WD * w_ref[...])

    spec = pl.BlockSpec((tr, cols), lambda i: (i, 0))
    outs = pl.pallas_call(
        body, name=name, grid=(rows // tr,),
        in_specs=[spec] * 4, out_specs=[spec] * 3,
        out_shape=[jax.ShapeDtypeStruct((rows, cols), F32)] * 3,
        compiler_params=_params("parallel"))(w2, g2, m2, v2)
    return tuple(o.reshape(shape) for o in outs)


def _place():
    return lax.axis_index("x"), lax.axis_index("y"), lax.axis_index("c")


def _other_chips(x, y):
    return [(1 - x, y), (x, 1 - y), (1 - x, 1 - y)]


def _slot(ref, axis, idx):
    return ref.at[idx] if axis == 0 else ref.at[:, idx]


def gather_weights(shards, axes):
    n = len(shards)

    def body(*refs):
        srcs, outs = refs[:n], refs[n:2 * n]
        send_sems, recv_sems, local_sems = refs[2 * n:]
        x, y, c = _place()
        me = 2 * x + y
        chips = _other_chips(x, y)
        pending = []
        for t in range(n):
            local = pltpu.make_async_copy(srcs[t], _slot(outs[t], axes[t], me), local_sems.at[t])
            local.start()
            pending.append(local)
            for k, (px, py) in enumerate(chips):
                pltpu.make_async_remote_copy(
                    src_ref=srcs[t], dst_ref=_slot(outs[t], axes[t], me),
                    send_sem=send_sems.at[3 * t + k], recv_sem=recv_sems.at[3 * t + k],
                    device_id=(px, py, c), device_id_type=MESH).start()
        for t in range(n):
            for k, (px, py) in enumerate(chips):
                arrived = pltpu.make_async_remote_copy(
                    src_ref=srcs[t], dst_ref=_slot(outs[t], axes[t], 2 * px + py),
                    send_sem=send_sems.at[3 * t + k], recv_sem=recv_sems.at[3 * t + k],
                    device_id=(px, py, c), device_id_type=MESH)
                arrived.wait_send()
                arrived.wait_recv()
        for local in pending:
            local.wait()

    out_shape = []
    for a, ax in zip(shards, axes):
        shape = (N_CHIPS,) + a.shape if ax == 0 else a.shape[:1] + (N_CHIPS,) + a.shape[1:]
        out_shape.append(jax.ShapeDtypeStruct(shape, a.dtype))
    return pl.pallas_call(
        body, name="gather_weights", in_specs=[ANY] * n, out_specs=[ANY] * n, out_shape=out_shape,
        scratch_shapes=[pltpu.SemaphoreType.DMA((3 * n,)), pltpu.SemaphoreType.DMA((3 * n,)),
                        pltpu.SemaphoreType.DMA((n,))],
        compiler_params=pltpu.CompilerParams(has_side_effects=True))(*shards)


def _half_rows(ref, c, half):
    return ref.at[:, :, pl.ds(c * half, half), :]


def reduce_stage_a(grads_bf, small):
    n = len(grads_bf)

    def body(*refs):
        srcs, small_ref = refs[:n], refs[n]
        outs, small_out = refs[n + 1:2 * n + 1], refs[2 * n + 1]
        send_sems, recv_sems, s_send, s_recv, local_sem = refs[2 * n + 2:]
        x, y, c = _place()
        dev = 4 * x + 2 * y + c
        local = pltpu.make_async_copy(small_ref, small_out.at[dev], local_sem)
        local.start()
        for t in range(n):
            half = srcs[t].shape[2] // 2
            pltpu.make_async_remote_copy(
                src_ref=_half_rows(srcs[t], 1 - c, half), dst_ref=outs[t],
                send_sem=send_sems.at[t], recv_sem=recv_sems.at[t],
                device_id=(x, y, 1 - c), device_id_type=MESH).start()
        peers = []
        for k in range(1, N_DEV):
            peer = (x ^ ((k >> 2) & 1), y ^ ((k >> 1) & 1), c ^ (k & 1))
            peers.append(peer)
            pltpu.make_async_remote_copy(
                src_ref=small_ref, dst_ref=small_out.at[dev],
                send_sem=s_send.at[k - 1], recv_sem=s_recv.at[k - 1],
                device_id=peer, device_id_type=MESH).start()
        for t in range(n):
            half = srcs[t].shape[2] // 2
            done = pltpu.make_async_remote_copy(
                src_ref=_half_rows(srcs[t], 1 - c, half), dst_ref=outs[t],
                send_sem=send_sems.at[t], recv_sem=recv_sems.at[t],
                device_id=(x, y, 1 - c), device_id_type=MESH)
            done.wait_send()
            done.wait_recv()
        for k, (px, py, pc) in enumerate(peers):
            done = pltpu.make_async_remote_copy(
                src_ref=small_ref, dst_ref=small_out.at[4 * px + 2 * py + pc],
                send_sem=s_send.at[k], recv_sem=s_recv.at[k],
                device_id=(px, py, pc), device_id_type=MESH)
            done.wait_send()
            done.wait_recv()
        local.wait()

    out_shape = [jax.ShapeDtypeStruct(a.shape[:2] + (a.shape[2] // 2, a.shape[3]), a.dtype) for a in grads_bf]
    out_shape.append(jax.ShapeDtypeStruct((N_DEV,) + small.shape, small.dtype))
    res = pl.pallas_call(
        body, name="reduce_stage_a", in_specs=[ANY] * (n + 1), out_specs=[ANY] * (n + 1), out_shape=out_shape,
        scratch_shapes=[pltpu.SemaphoreType.DMA((n,)), pltpu.SemaphoreType.DMA((n,)),
                        pltpu.SemaphoreType.DMA((N_DEV - 1,)), pltpu.SemaphoreType.DMA((N_DEV - 1,)),
                        pltpu.SemaphoreType.DMA],
        compiler_params=pltpu.CompilerParams(has_side_effects=True))(*grads_bf, small)
    return res[:n], res[n]


def sum_stage_a(grad, recv, core, *, name):
    l_n, j_n, half, cols = recv.shape

    def body(core_ref, g_ref, r_ref, o_ref, ob_ref):
        acc = g_ref[...] + r_ref[...].astype(F32)
        o_ref[...] = acc
        ob_ref[...] = acc.astype(BF16)

    blk = (None, None, half, cols)
    return pl.pallas_call(
        body, name=name,
        grid_spec=pltpu.PrefetchScalarGridSpec(
            num_scalar_prefetch=1, grid=(l_n, j_n),
            in_specs=[pl.BlockSpec(blk, lambda l, j, core_ref: (l, j, core_ref[0], 0)),
                      pl.BlockSpec(blk, lambda l, j, core_ref: (l, j, 0, 0))],
            out_specs=[pl.BlockSpec(blk, lambda l, j, core_ref: (l, j, 0, 0))] * 2),
        out_shape=[jax.ShapeDtypeStruct(recv.shape, F32), jax.ShapeDtypeStruct(recv.shape, BF16)],
        compiler_params=_params("parallel", "parallel"))(core, grad, recv)


def reduce_stage_b(parts_bf):
    n = len(parts_bf)

    def body(*refs):
        srcs, outs = refs[:n], refs[n:2 * n]
        send_sems, recv_sems = refs[2 * n:]
        x, y, c = _place()
        chips = _other_chips(x, y)
        for t in range(n):
            for k, (px, py) in enumerate(chips):
                pltpu.make_async_remote_copy(
                    src_ref=srcs[t].at[:, 2 * px + py], dst_ref=outs[t].at[k],
                    send_sem=send_sems.at[3 * t + k], recv_sem=recv_sems.at[3 * t + k],
                    device_id=(px, py, c), device_id_type=MESH).start()
        for t in range(n):
            for k, (px, py) in enumerate(chips):
                done = pltpu.make_async_remote_copy(
                    src_ref=srcs[t].at[:, 2 * px + py], dst_ref=outs[t].at[k],
                    send_sem=send_sems.at[3 * t + k], recv_sem=recv_sems.at[3 * t + k],
                    device_id=(px, py, c), device_id_type=MESH)
                done.wait_send()
                done.wait_recv()

    out_shape = [jax.ShapeDtypeStruct((3, a.shape[0]) + a.shape[2:], a.dtype) for a in parts_bf]
    return pl.pallas_call(
        body, name="reduce_stage_b", in_specs=[ANY] * n, out_specs=[ANY] * n, out_shape=out_shape,
        scratch_shapes=[pltpu.SemaphoreType.DMA((3 * n,)), pltpu.SemaphoreType.DMA((3 * n,))],
        compiler_params=pltpu.CompilerParams(has_side_effects=True))(*parts_bf)


def sum_stage_b(part, recv, chip, *, name):
    l_n, _, half, cols = part.shape

    def body(chip_ref, p_ref, r_ref, o_ref):
        acc = p_ref[...]
        for k in range(3):
            acc = acc + r_ref[k].astype(F32)
        o_ref[...] = acc

    return pl.pallas_call(
        body, name=name,
        grid_spec=pltpu.PrefetchScalarGridSpec(
            num_scalar_prefetch=1, grid=(l_n,),
            in_specs=[pl.BlockSpec((None, None, half, cols), lambda l, chip_ref: (l, chip_ref[0], 0, 0)),
                      pl.BlockSpec((3, None, half, cols), lambda l, chip_ref: (0, l, 0, 0))],
            out_specs=pl.BlockSpec((None, half, cols), lambda l, chip_ref: (l, 0, 0))),
        out_shape=jax.ShapeDtypeStruct((l_n, half, cols), F32),
        compiler_params=_params("parallel"))(chip, part, recv)


def reduce_stage_c(halves):
    n = len(halves)

    def body(*refs):
        srcs, outs = refs[:n], refs[n:2 * n]
        send_sems, recv_sems, local_sems = refs[2 * n:]
        x, y, c = _place()
        pending = []
        for t in range(n):
            half = srcs[t].shape[1]
            mine = outs[t].at[:, pl.ds(c * half, half), :]
            local = pltpu.make_async_copy(srcs[t], mine, local_sems.at[t])
            local.start()
            pending.append(local)
            pltpu.make_async_remote_copy(
                src_ref=srcs[t], dst_ref=mine, send_sem=send_sems.at[t], recv_sem=recv_sems.at[t],
                device_id=(x, y, 1 - c), device_id_type=MESH).start()
        for t in range(n):
            half = srcs[t].shape[1]
            done = pltpu.make_async_remote_copy(
                src_ref=srcs[t], dst_ref=outs[t].at[:, pl.ds((1 - c) * half, half), :],
                send_sem=send_sems.at[t], recv_sem=recv_sems.at[t],
                device_id=(x, y, 1 - c), device_id_type=MESH)
            done.wait_send()
            done.wait_recv()
        for local in pending:
            local.wait()

    out_shape = [jax.ShapeDtypeStruct((a.shape[0], 2 * a.shape[1], a.shape[2]), a.dtype) for a in halves]
    return pl.pallas_call(
        body, name="reduce_stage_c", in_specs=[ANY] * n, out_specs=[ANY] * n, out_shape=out_shape,
        scratch_shapes=[pltpu.SemaphoreType.DMA((n,)), pltpu.SemaphoreType.DMA((n,)),
                        pltpu.SemaphoreType.DMA((n,))],
        compiler_params=pltpu.CompilerParams(has_side_effects=True))(*halves)


def sum_small(gathered, *, name):
    _, rows, cols = gathered.shape

    def body(g_ref, o_ref):
        acc = g_ref[0]
        for d in range(1, N_DEV):
            acc = acc + g_ref[d]
        o_ref[...] = acc

    return pl.pallas_call(
        body, name=name, grid=(1,),
        in_specs=[pl.BlockSpec((N_DEV, rows, cols), lambda i: (0, 0, 0))],
        out_specs=pl.BlockSpec((rows, cols), lambda i: (0, 0)),
        out_shape=jax.ShapeDtypeStruct((rows, cols), F32),
        compiler_params=_params("arbitrary"))(gathered)


def _ffn_fwd(h, gain, wg, wu, wd, cw, cb, l):
    hn, rstd = rms_fwd(h, gain, name=f"ffn{l}_rms")
    g = mm_nn(hn, wg, l=l, name=f"ffn{l}_gate")
    up = mm_nn(hn, wu, l=l, name=f"ffn{l}_up")
    act = ffn_act_fwd(g, up, cw, cb, l=l, name=f"ffn{l}_act")
    out = mm_acc(act, wd, h, l=l, name=f"ffn{l}_down")
    return out, (hn, rstd, g, up, act)


def _ffn_bwd(dh, dh_bf, h, gain, saved, wg, wu, wd, cw, cb, l):
    hn, rstd, g, up, act = saved
    dwd = mm_tn(act, dh_bf[None], name=f"ffn{l}_dwd")
    dact = mm_nt_each(dh_bf, wd, l=l, name=f"ffn{l}_dact")
    dg, dup, dcw, dcb = ffn_act_bwd(g, up, dact, cw, cb, l=l, name=f"ffn{l}_act_bwd")
    dwg = mm_tn(hn[None], dg, name=f"ffn{l}_dwg")
    dwu = mm_tn(hn[None], dup, name=f"ffn{l}_dwu")
    dhn = mm_nt_sum([(dg, wg), (dup, wu)], l=l, name=f"ffn{l}_dhn")
    dh_in, dh_in_bf, dgain = rms_bwd(dhn, h, gain, rstd, dh, name=f"ffn{l}_rms_bwd")
    return dh_in, dh_in_bf, (dwg, dwu, dwd, dcw, dcb, dgain)


def kernel(x, norm_mix, norm_ffn, final_norm, w_in_even, conv_a, w_pool, pool_scale, w_out_even, w_in_odd, sgu_norm, w_spatial, b_spatial, w_out_odd, w_ffn_gate, w_ffn_up, conv_ffn, b_conv_ffn, w_ffn_down, loss_target, m_norm_mix, m_norm_ffn, m_final_norm, m_w_in_even, m_conv_a, m_w_pool, m_pool_scale, m_w_out_even, m_w_in_odd, m_sgu_norm, m_w_spatial, m_b_spatial, m_w_out_odd, m_w_ffn_gate, m_w_ffn_up, m_conv_ffn, m_b_conv_ffn, m_w_ffn_down, v_norm_mix, v_norm_ffn, v_final_norm, v_w_in_even, v_conv_a, v_w_pool, v_pool_scale, v_w_out_even, v_w_in_odd, v_sgu_norm, v_w_spatial, v_b_spatial, v_w_out_odd, v_w_ffn_gate, v_w_ffn_up, v_conv_ffn, v_b_conv_ffn, v_w_ffn_down):
    weights = dict(norm_mix=norm_mix, norm_ffn=norm_ffn, final_norm=final_norm, w_in_even=w_in_even,
                   conv_a=conv_a, w_pool=w_pool, pool_scale=pool_scale, w_out_even=w_out_even,
                   w_in_odd=w_in_odd, sgu_norm=sgu_norm, w_spatial=w_spatial, b_spatial=b_spatial,
                   w_out_odd=w_out_odd, w_ffn_gate=w_ffn_gate, w_ffn_up=w_ffn_up, conv_ffn=conv_ffn,
                   b_conv_ffn=b_conv_ffn, w_ffn_down=w_ffn_down)
    m_in = dict(norm_mix=m_norm_mix, norm_ffn=m_norm_ffn, final_norm=m_final_norm, w_in_even=m_w_in_even,
                conv_a=m_conv_a, w_pool=m_w_pool, pool_scale=m_pool_scale, w_out_even=m_w_out_even,
                w_in_odd=m_w_in_odd, sgu_norm=m_sgu_norm, w_spatial=m_w_spatial, b_spatial=m_b_spatial,
                w_out_odd=m_w_out_odd, w_ffn_gate=m_w_ffn_gate, w_ffn_up=m_w_ffn_up, conv_ffn=m_conv_ffn,
                b_conv_ffn=m_b_conv_ffn, w_ffn_down=m_w_ffn_down)
    v_in = dict(norm_mix=v_norm_mix, norm_ffn=v_norm_ffn, final_norm=v_final_norm, w_in_even=v_w_in_even,
                conv_a=v_conv_a, w_pool=v_w_pool, pool_scale=v_pool_scale, w_out_even=v_w_out_even,
                w_in_odd=v_w_in_odd, sgu_norm=v_sgu_norm, w_spatial=v_w_spatial, b_spatial=v_b_spatial,
                w_out_odd=v_w_out_odd, w_ffn_gate=v_w_ffn_gate, w_ffn_up=v_w_ffn_up, conv_ffn=v_conv_ffn,
                b_conv_ffn=v_b_conv_ffn, w_ffn_down=v_w_ffn_down)
    order = list(weights)

    chip = 2 * lax.axis_index("x") + lax.axis_index("y")
    core = lax.axis_index("c")
    chip_arr = jnp.reshape(chip, (1,)).astype(jnp.int32)
    core_arr = jnp.reshape(core, (1,)).astype(jnp.int32)

    h0 = x[0]
    target = loss_target[0]
    n_layers = w_ffn_gate.shape[0]
    f_shard = w_ffn_gate.shape[-1]

    shards = [w_in_even[0].astype(BF16), w_out_even[0].astype(BF16), w_in_odd[0].astype(BF16),
              w_out_odd[0].astype(BF16), w_ffn_gate.astype(BF16), w_ffn_up.astype(BF16),
              w_ffn_down.astype(BF16), conv_a[0], sgu_norm, conv_ffn]
    axes = [0, 0, 0, 0, 1, 1, 1, 0, 0, 1]
    win_e, wout_e, win_o, wout_o, wg, wu, wd, ca_g, sn_g, cw = gather_weights(shards, axes)
    d_model = h0.shape[1]
    wout_e = wout_e.reshape(1, -1, d_model)
    wout_o = wout_o.reshape(1, -1, d_model)
    ca_full = jnp.transpose(ca_g, (1, 0, 2)).reshape(ca_g.shape[1], -1)
    sn_full = sn_g.reshape(1, -1)
    cb = b_conv_ffn.reshape(n_layers, N_CHIPS, 1, f_shard)
    wp_bf = w_pool[0].astype(BF16)
    wp_t_bf = jnp.transpose(w_pool[0], (0, 2, 1)).astype(BF16)
    ws = w_spatial[0]
    bs = b_spatial[0][:, :, None]

    xn0, rstd0 = rms_fwd(h0, norm_mix[0:1], name="l0_rms")
    proj0 = mm_nn(xn0, win_e, name="l0_in")
    mix0 = even_fwd(proj0, ca_full, wp_bf, pool_scale, name="l0_mix")
    h1 = mm_acc(mix0[None], wout_e, h0, name="l0_out")
    h2, ffn0 = _ffn_fwd(h1, norm_ffn[0:1], wg, wu, wd, cw, cb, 0)
    xn1, rstd1 = rms_fwd(h2, norm_mix[1:2], name="l1_rms")
    p1 = mm_nn(xn1, win_o, name="l1_in")
    mix1, rstd_v = sgu_fwd(p1, sn_full, ws, bs, name="l1_mix")
    h3 = mm_acc(mix1[None], wout_o, h2, name="l1_out")
    h4, ffn1 = _ffn_fwd(h3, norm_ffn[1:2], wg, wu, wd, cw, cb, 1)

    dh4, dh4_bf, loss_row, d_final = final_loss(h4, target, final_norm[None], name="loss")
    loss = lax.psum(loss_row[0, 0], ("x", "y", "c"))

    dh3, dh3_bf, (dwg1, dwu1, dwd1, dcw1, dcb1, dnf1) = _ffn_bwd(
        dh4, dh4_bf, h3, norm_ffn[1:2], ffn1, wg, wu, wd, cw, cb, 1)
    dwout_o = mm_tn(mix1[None], dh3_bf[None], name="l1_dwout")
    dmix1 = mm_nt_each(dh3_bf, wout_o, name="l1_dmix")[0]
    dp1, dsn, dws, dbs = sgu_bwd(p1, dmix1, rstd_v, sn_full, ws, bs, name="l1_mix_bwd")
    dwin_o = mm_tn(xn1[None], dp1, name="l1_dwin")
    dxn1 = mm_nt_sum([(dp1, win_o)], name="l1_dxn")
    dh2, dh2_bf, dnm1 = rms_bwd(dxn1, h2, norm_mix[1:2], rstd1, dh3, name="l1_rms_bwd")

    dh1, dh1_bf, (dwg0, dwu0, dwd0, dcw0, dcb0, dnf0) = _ffn_bwd(
        dh2, dh2_bf, h1, norm_ffn[0:1], ffn0, wg, wu, wd, cw, cb, 0)
    dwout_e = mm_tn(mix0[None], dh1_bf[None], name="l0_dwout")
    dmix0 = mm_nt_each(dh1_bf, wout_e, name="l0_dmix")[0]
    dproj0, dca, dwp, dps = even_bwd(proj0, dmix0, ca_full, wp_bf, wp_t_bf, pool_scale, name="l0_mix_bwd")
    dwin_e = mm_tn(xn0[None], dproj0, name="l0_dwin")
    dxn0 = mm_nt_sum([(dproj0, win_e)], name="l0_dxn")
    dh0, _, dnm0 = rms_bwd(dxn0, h0, norm_mix[0:1], rstd0, dh1, name="l0_rms_bwd")
    grad_x = dh0[None]

    big = [dwin_e[None], dwout_e.reshape(1, N_CHIPS, -1, d_model), dwin_o[None],
           dwout_o.reshape(1, N_CHIPS, -1, d_model),
           jnp.stack([dwg0, dwg1]), jnp.stack([dwu0, dwu1]), jnp.stack([dwd0, dwd1])]
    big_names = ["w_in_even", "w_out_even", "w_in_odd", "w_out_odd", "w_ffn_gate", "w_ffn_up", "w_ffn_down"]
    small_parts = dict(
        norm_mix=jnp.concatenate([dnm0, dnm1]), norm_ffn=jnp.concatenate([dnf0, dnf1]), final_norm=d_final,
        conv_a=dca, w_pool=dwp, pool_scale=dps, sgu_norm=dsn, w_spatial=dws, b_spatial=dbs,
        conv_ffn=jnp.stack([dcw0, dcw1]), b_conv_ffn=jnp.stack([dcb0, dcb1]))
    flat = jnp.concatenate([v.reshape(-1) for v in small_parts.values()])
    pad = (-flat.shape[0]) % (8 * 128)
    small = jnp.pad(flat, (0, pad)).reshape(-1, 128)

    recv_a, small_all = reduce_stage_a([g.astype(BF16) for g in big], small)
    parts = [sum_stage_a(g, r, core_arr, name=f"sum_a_{nm}") for g, r, nm in zip(big, recv_a, big_names)]
    recv_b = reduce_stage_b([p[1] for p in parts])
    halves = [sum_stage_b(p[0], r, chip_arr, name=f"sum_b_{nm}") for p, r, nm in zip(parts, recv_b, big_names)]
    full = reduce_stage_c(halves)
    small_sum = sum_small(small_all, name="sum_small").reshape(-1)

    grads = {}
    for nm, g in zip(big_names, full):
        grads[nm] = g.reshape(weights[nm].shape)
    off = 0
    small_red = {}
    for nm, v in small_parts.items():
        small_red[nm] = small_sum[off:off + v.size].reshape(v.shape)
        off += v.size
    for nm in ("norm_mix", "norm_ffn", "pool_scale"):
        grads[nm] = small_red[nm].reshape(weights[nm].shape)
    grads["final_norm"] = small_red["final_norm"].reshape(weights["final_norm"].shape)
    grads["w_pool"] = small_red["w_pool"][None]
    grads["w_spatial"] = small_red["w_spatial"][None]
    grads["b_spatial"] = small_red["b_spatial"].reshape(weights["b_spatial"].shape)
    grads["b_conv_ffn"] = small_red["b_conv_ffn"].reshape(weights["b_conv_ffn"].shape)
    grads["conv_a"] = lax.dynamic_slice_in_dim(small_red["conv_a"], chip * conv_a.shape[-1], conv_a.shape[-1], 1)[None]
    grads["sgu_norm"] = lax.dynamic_slice_in_dim(small_red["sgu_norm"], chip * sgu_norm.shape[-1], sgu_norm.shape[-1], 1)
    grads["conv_ffn"] = lax.dynamic_index_in_dim(small_red["conv_ffn"], chip, 1, keepdims=False)

    deltas, new_m, new_v = {}, {}, {}
    for nm in order:
        w = weights[nm]
        w2 = w[None] if w.ndim == 1 else w
        shp = w2.shape
        d, nm_, nv_ = adamw(w2, grads[nm].reshape(shp), m_in[nm].reshape(shp), v_in[nm].reshape(shp),
                            name=f"adamw_{nm}")
        deltas[nm], new_m[nm], new_v[nm] = d.reshape(w.shape), nm_.reshape(w.shape), nv_.reshape(w.shape)

    return (loss, grad_x, *[grads[n] for n in order], *[deltas[n] for n in order],
            *[new_m[n] for n in order], *[new_v[n] for n in order])
```

```python
import jax
import jax.numpy as jnp
from jax import lax
from jax.experimental import pallas as pl
from jax.experimental.pallas import tpu as pltpu

F32 = jnp.float32
BF16 = jnp.bfloat16
MESH = pl.DeviceIdType.MESH
ANY = pl.BlockSpec(memory_space=pl.ANY)

EPS = 1e-6
POOL_WINDOWS = (2, 4, 8, 16)
GROUP = 128
CHUNK = 128
N_CHIPS = 4
N_DEV = 8
ROW_TILE = 512
HALO = 16
VMEM_LIMIT = 56 * 1024 * 1024

ADAM_LR = 0.001
ADAM_B1 = 0.9
ADAM_B2 = 0.999
ADAM_EPS = 1e-08
ADAM_WD = 0.01
ADAM_STEP = 10


def _params(*sem):
    return pltpu.CompilerParams(dimension_semantics=sem, vmem_limit_bytes=VMEM_LIMIT)


def _layer_spec(block, l, idx):
    if l is None:
        return pl.BlockSpec(block, idx)
    return pl.BlockSpec((None,) + block, lambda *g: (l,) + idx(*g))


def mm_nn(a, b, *, l=None, name):
    s, k = a.shape
    j_n, n = b.shape[-3], b.shape[-1]
    tm = min(ROW_TILE, s)

    def body(a_ref, b_ref, o_ref):
        o_ref[...] = jnp.dot(a_ref[...], b_ref[...], preferred_element_type=F32)

    return pl.pallas_call(
        body, name=name, grid=(j_n, s // tm),
        in_specs=[pl.BlockSpec((tm, k), lambda j, i: (i, 0)),
                  _layer_spec((None, k, n), l, lambda j, i: (j, 0, 0))],
        out_specs=pl.BlockSpec((None, tm, n), lambda j, i: (j, i, 0)),
        out_shape=jax.ShapeDtypeStruct((j_n, s, n), F32),
        compiler_params=_params("parallel", "parallel"))(a, b)


def mm_acc(a, b, res, *, l=None, name):
    j_n, s, kj = a.shape
    n = b.shape[-1]
    tm = min(ROW_TILE, s)

    def body(a_ref, b_ref, r_ref, o_ref):
        acc = r_ref[...]
        for j in range(j_n):
            acc = acc + jnp.dot(a_ref[j], b_ref[j], preferred_element_type=F32)
        o_ref[...] = acc

    return pl.pallas_call(
        body, name=name, grid=(s // tm,),
        in_specs=[pl.BlockSpec((j_n, tm, kj), lambda i: (0, i, 0)),
                  _layer_spec((j_n, kj, n), l, lambda i: (0, 0, 0)),
                  pl.BlockSpec((tm, n), lambda i: (i, 0))],
        out_specs=pl.BlockSpec((tm, n), lambda i: (i, 0)),
        out_shape=jax.ShapeDtypeStruct((s, n), F32),
        compiler_params=_params("parallel"))(a, b, res)


_NT = (((1,), (1,)), ((), ()))
_TN = (((0,), (0,)), ((), ()))


def mm_nt_sum(pairs, *, l=None, name):
    j_n, s, nj = pairs[0][0].shape
    k = pairs[0][1].shape[-2]
    tm = min(ROW_TILE, s)
    n_p = len(pairs)

    def body(*refs):
        o_ref = refs[-1]
        acc = jnp.zeros((tm, k), F32)
        for p in range(n_p):
            dy_ref, w_ref = refs[2 * p], refs[2 * p + 1]
            for j in range(j_n):
                acc = acc + lax.dot_general(dy_ref[j], w_ref[j], _NT, preferred_element_type=F32)
        o_ref[...] = acc

    in_specs, args = [], []
    for dy, w in pairs:
        in_specs += [pl.BlockSpec((j_n, tm, nj), lambda i: (0, i, 0)),
                     _layer_spec((j_n, k, nj), l, lambda i: (0, 0, 0))]
        args += [dy, w]
    return pl.pallas_call(
        body, name=name, grid=(s // tm,), in_specs=in_specs,
        out_specs=pl.BlockSpec((tm, k), lambda i: (i, 0)),
        out_shape=jax.ShapeDtypeStruct((s, k), F32),
        compiler_params=_params("parallel"))(*args)


def mm_nt_each(a, b, *, l=None, name):
    s, k = a.shape
    j_n, nj = b.shape[-3], b.shape[-2]
    tm = min(ROW_TILE, s)

    def body(a_ref, b_ref, o_ref):
        o_ref[...] = lax.dot_general(a_ref[...], b_ref[...], _NT, preferred_element_type=F32)

    return pl.pallas_call(
        body, name=name, grid=(j_n, s // tm),
        in_specs=[pl.BlockSpec((tm, k), lambda j, i: (i, 0)),
                  _layer_spec((None, nj, k), l, lambda j, i: (j, 0, 0))],
        out_specs=pl.BlockSpec((None, tm, nj), lambda j, i: (j, i, 0)),
        out_shape=jax.ShapeDtypeStruct((j_n, s, nj), F32),
        compiler_params=_params("parallel", "parallel"))(a, b)


def mm_tn(a, dy, *, name, pin=None):
    ja, s, k = a.shape
    jd, _, n = dy.shape
    j_n = max(ja, jd)
    tk = min(ROW_TILE, s)
    n_steps = s // tk

    def body(a_ref, d_ref, *rest):
        o_ref, ob_ref = rest[-2:]

        @pl.when(pl.program_id(1) == 0)
        def _():
            o_ref[...] = jnp.zeros_like(o_ref)
        o_ref[...] += lax.dot_general(a_ref[...], d_ref[...], _TN, preferred_element_type=F32)

        @pl.when(pl.program_id(1) == n_steps - 1)
        def _():
            ob_ref[...] = o_ref[...].astype(BF16)

    out_spec = pl.BlockSpec((None, k, n), lambda j, i: (j, 0, 0))
    return pl.pallas_call(
        body, name=name, grid=(j_n, n_steps),
        in_specs=[pl.BlockSpec((None, tk, k), (lambda j, i: (j, i, 0)) if ja > 1 else (lambda j, i: (0, i, 0))),
                  pl.BlockSpec((None, tk, n), (lambda j, i: (j, i, 0)) if jd > 1 else (lambda j, i: (0, i, 0)))]
        + ([ANY] if pin is not None else []),
        out_specs=[out_spec, out_spec],
        out_shape=[jax.ShapeDtypeStruct((j_n, k, n), F32), jax.ShapeDtypeStruct((j_n, k, n), BF16)],
        compiler_params=_params("parallel", "arbitrary"))(a, dy, *([pin] if pin is not None else []))


def _back(x, k):
    return pltpu.roll(x, k, 0)


def _fwd(x, k):
    return pltpu.roll(x, x.shape[0] - k, 0)


def _causal_conv(x, w_ref):
    return w_ref[0:1, :] * _back(x, 2) + w_ref[1:2, :] * _back(x, 1) + w_ref[2:3, :] * x


def _causal_conv_t(dy, w_ref):
    return w_ref[2:3, :] * dy + w_ref[1:2, :] * _fwd(dy, 1) + w_ref[0:1, :] * _fwd(dy, 2)


def _gelu(x):
    return 0.5 * x * (1.0 + lax.erf(x * 0.7071067811865476))


def _gelu_grad(x):
    return 0.5 * (1.0 + lax.erf(x * 0.7071067811865476)) + x * jnp.exp(-0.5 * x * x) * 0.3989422804014327


def _colsum(x):
    return jnp.sum(x, axis=0, keepdims=True)


def rms_fwd(h, gain, *, name, pin=None):
    s, d = h.shape
    ts = min(ROW_TILE, s)

    def body(h_ref, g_ref, *rest):
        o_ref, r_ref = rest[-2:]
        x = h_ref[...]
        rstd = lax.rsqrt(jnp.mean(x * x, axis=-1, keepdims=True) + EPS)
        o_ref[...] = (x * rstd * g_ref[...]).astype(BF16)
        r_ref[...] = rstd

    return pl.pallas_call(
        body, name=name, grid=(s // ts,),
        in_specs=[pl.BlockSpec((ts, d), lambda i: (i, 0)), pl.BlockSpec((1, d), lambda i: (0, 0))]
        + ([ANY] if pin is not None else []),
        out_specs=[pl.BlockSpec((ts, d), lambda i: (i, 0)), pl.BlockSpec((ts, 1), lambda i: (i, 0))],
        out_shape=[jax.ShapeDtypeStruct((s, d), BF16), jax.ShapeDtypeStruct((s, 1), F32)],
        compiler_params=_params("parallel"))(h, gain, *([pin] if pin is not None else []))


def rms_bwd(dxn, h, gain, rstd, dres, *, name):
    s, d = h.shape
    ts = min(ROW_TILE, s)

    def body(dx_ref, h_ref, g_ref, r_ref, dr_ref, o_ref, ob_ref, dg_ref):
        @pl.when(pl.program_id(0) == 0)
        def _():
            dg_ref[...] = jnp.zeros_like(dg_ref)
        rstd_v = r_ref[...]
        hhat = h_ref[...] * rstd_v
        dx = dx_ref[...]
        dg_ref[...] += _colsum(dx * hhat)
        dxg = dx * g_ref[...]
        dh = dr_ref[...] + rstd_v * (dxg - hhat * jnp.mean(dxg * hhat, axis=-1, keepdims=True))
        o_ref[...] = dh
        ob_ref[...] = dh.astype(BF16)

    row = pl.BlockSpec((ts, d), lambda i: (i, 0))
    vec = pl.BlockSpec((1, d), lambda i: (0, 0))
    return pl.pallas_call(
        body, name=name, grid=(s // ts,),
        in_specs=[row, row, vec, pl.BlockSpec((ts, 1), lambda i: (i, 0)), row],
        out_specs=[row, row, vec],
        out_shape=[jax.ShapeDtypeStruct((s, d), F32), jax.ShapeDtypeStruct((s, d), BF16),
                   jax.ShapeDtypeStruct((1, d), F32)],
        compiler_params=_params("arbitrary"))(dxn, h, gain, rstd, dres)


def final_loss(h, target, gain, *, name):
    s, d = h.shape
    ts = min(ROW_TILE, s)

    def body(h_ref, t_ref, g_ref, o_ref, ob_ref, l_ref, dg_ref):
        @pl.when(pl.program_id(0) == 0)
        def _():
            l_ref[...] = jnp.zeros_like(l_ref)
            dg_ref[...] = jnp.zeros_like(dg_ref)
        x = h_ref[...]
        rstd = lax.rsqrt(jnp.mean(x * x, axis=-1, keepdims=True) + EPS)
        hhat = x * rstd
        err = hhat * g_ref[...] - t_ref[...]
        l_ref[...] += 0.5 * jnp.sum(jnp.mean(err * err, axis=-1, keepdims=True), axis=0, keepdims=True)
        dy = err * (1.0 / d)
        dg_ref[...] += _colsum(dy * hhat)
        dyg = dy * g_ref[...]
        dh = rstd * (dyg - hhat * jnp.mean(dyg * hhat, axis=-1, keepdims=True))
        o_ref[...] = dh
        ob_ref[...] = dh.astype(BF16)

    row = pl.BlockSpec((ts, d), lambda i: (i, 0))
    vec = pl.BlockSpec((1, d), lambda i: (0, 0))
    return pl.pallas_call(
        body, name=name, grid=(s // ts,),
        in_specs=[row, row, vec],
        out_specs=[row, row, pl.BlockSpec((1, 128), lambda i: (0, 0)), vec],
        out_shape=[jax.ShapeDtypeStruct((s, d), F32), jax.ShapeDtypeStruct((s, d), BF16),
                   jax.ShapeDtypeStruct((1, 128), F32), jax.ShapeDtypeStruct((1, d), F32)],
        compiler_params=_params("arbitrary"))(h, target, gain)


def _halo_specs(n_lead, ts, width, n_tiles):
    hb = ts // HALO
    prev = pl.BlockSpec((n_lead, HALO, width), lambda i: (0, jnp.maximum(i * hb - 1, 0), 0))
    nxt = pl.BlockSpec((n_lead, HALO, width), lambda i: (0, jnp.minimum((i + 1) * hb, n_tiles * hb - 1), 0))
    return prev, nxt


def _pool_fwd(z_ext, g, pos):
    w = POOL_WINDOWS[g]
    zg = z_ext[:, g * GROUP:(g + 1) * GROUP]
    acc = zg
    sh = 1
    while sh < w:
        acc = acc + _back(acc, sh)
        sh *= 2
    return acc[HALO:] / jnp.minimum(pos, float(w)) - zg[HALO:]


def even_fwd(proj, conv_a, w_pool, pool_scale, *, name):
    _, s, w = proj.shape
    ts = min(ROW_TILE, s)
    n_t = s // ts
    prev, _ = _halo_specs(4, ts, w, n_t)

    def body(p_ref, ph_ref, ca_ref, wp_ref, ps_ref, o_ref):
        i = pl.program_id(0)
        keep = jnp.where(i > 0, 1.0, 0.0)
        cv_ext = jnp.concatenate([ph_ref[1] * ph_ref[2] * keep, p_ref[1] * p_ref[2]], axis=0)
        o_ref[:, 0:w] = (p_ref[0] * _causal_conv(cv_ext, ca_ref)[HALO:]).astype(BF16)
        z_ext = jnp.concatenate([ph_ref[3] * keep, p_ref[3]], axis=0)
        pos = (i * ts + lax.broadcasted_iota(jnp.int32, (ts, 1), 0) + 1).astype(F32)
        for g in range(len(POOL_WINDOWS)):
            pooled = _pool_fwd(z_ext, g, pos)
            mixed = jnp.dot(pooled.astype(BF16), wp_ref[g], preferred_element_type=F32)
            cols = slice(g * GROUP, (g + 1) * GROUP)
            o_ref[:, w + g * GROUP:w + (g + 1) * GROUP] = (mixed * ps_ref[:, cols]).astype(BF16)

    return pl.pallas_call(
        body, name=name, grid=(n_t,),
        in_specs=[pl.BlockSpec((4, ts, w), lambda i: (0, i, 0)), prev,
                  pl.BlockSpec((3, w), lambda i: (0, 0)),
                  pl.BlockSpec((4, GROUP, GROUP), lambda i: (0, 0, 0)),
                  pl.BlockSpec((1, w), lambda i: (0, 0))],
        out_specs=pl.BlockSpec((ts, 2 * w), lambda i: (i, 0)),
        out_shape=jax.ShapeDtypeStruct((s, 2 * w), BF16),
        compiler_params=_params("parallel"))(proj, proj, conv_a, w_pool, pool_scale)


def even_bwd(proj, dmix, conv_a, w_pool, w_pool_t, pool_scale, *, name):
    _, s, w = proj.shape
    ts = min(ROW_TILE, s)
    n_t = s // ts
    prev, nxt = _halo_specs(4, ts, w, n_t)
    hb = ts // HALO
    dm_next = pl.BlockSpec((HALO, 2 * w), lambda i: (jnp.minimum((i + 1) * hb, n_t * hb - 1), 0))
    n_ext = ts + HALO

    def body(p_ref, pp_ref, pn_ref, dm_ref, dmn_ref, ca_ref, wp_ref, wpt_ref, ps_ref,
             dp_ref, dca_ref, dwp_ref, dps_ref):
        i = pl.program_id(0)

        @pl.when(i == 0)
        def _():
            dca_ref[...] = jnp.zeros_like(dca_ref)
            dwp_ref[...] = jnp.zeros_like(dwp_ref)
            dps_ref[...] = jnp.zeros_like(dps_ref)

        keep_p = jnp.where(i > 0, 1.0, 0.0)
        keep_n = jnp.where(i < n_t - 1, 1.0, 0.0)
        a_b, a_c, a_v = p_ref[0], p_ref[1], p_ref[2]
        cv_ext = jnp.concatenate([pp_ref[1] * pp_ref[2] * keep_p, a_c * a_v], axis=0)
        dy_a = dm_ref[:, 0:w]
        dp_ref[0] = (dy_a * _causal_conv(cv_ext, ca_ref)[HALO:]).astype(BF16)
        dcc = dy_a * a_b
        dca_ref[2:3, :] += _colsum(dcc * cv_ext[HALO:])
        dca_ref[1:2, :] += _colsum(dcc * _back(cv_ext, 1)[HALO:])
        dca_ref[0:1, :] += _colsum(dcc * _back(cv_ext, 2)[HALO:])
        dcc_ext = jnp.concatenate([dcc, dmn_ref[:, 0:w] * pn_ref[0] * keep_n], axis=0)
        dcv = _causal_conv_t(dcc_ext, ca_ref)[:ts]
        dp_ref[1] = (dcv * a_v).astype(BF16)
        dp_ref[2] = (dcv * a_c).astype(BF16)
        z_ext = jnp.concatenate([pp_ref[3] * keep_p, p_ref[3]], axis=0)
        pos = (i * ts + lax.broadcasted_iota(jnp.int32, (ts, 1), 0) + 1).astype(F32)
        pos_ext = (i * ts + lax.broadcasted_iota(jnp.int32, (n_ext, 1), 0) + 1).astype(F32)
        for g, win in enumerate(POOL_WINDOWS):
            cols = slice(g * GROUP, (g + 1) * GROUP)
            ycols = slice(w + g * GROUP, w + (g + 1) * GROUP)
            pooled = _pool_fwd(z_ext, g, pos).astype(BF16)
            mixed = jnp.dot(pooled, wp_ref[g], preferred_element_type=F32)
            dy_b = dm_ref[:, ycols]
            dps_ref[:, cols] += _colsum(dy_b * mixed)
            dmixed_ext = jnp.concatenate([dy_b, dmn_ref[:, ycols] * keep_n], axis=0) * ps_ref[:, cols]
            dmixed_ext = dmixed_ext.astype(BF16)
            dwp_ref[g] += lax.dot_general(pooled, dmixed_ext[:ts], _TN, preferred_element_type=F32)
            dpooled = jnp.dot(dmixed_ext, wpt_ref[g], preferred_element_type=F32)
            acc = dpooled / jnp.minimum(pos_ext, float(win))
            sh = 1
            while sh < win:
                acc = acc + _fwd(acc, sh)
                sh *= 2
            dp_ref[3, :, cols] = (acc[:ts] - dpooled[:ts]).astype(BF16)

    tile4 = pl.BlockSpec((4, ts, w), lambda i: (0, i, 0))
    const = lambda shape: pl.BlockSpec(shape, lambda i: (0,) * len(shape))
    return pl.pallas_call(
        body, name=name, grid=(n_t,),
        in_specs=[tile4, prev, nxt, pl.BlockSpec((ts, 2 * w), lambda i: (i, 0)), dm_next,
                  const((3, w)), const((4, GROUP, GROUP)), const((4, GROUP, GROUP)), const((1, w))],
        out_specs=[tile4, const((3, w)), const((4, GROUP, GROUP)), const((1, w))],
        out_shape=[jax.ShapeDtypeStruct((4, s, w), BF16), jax.ShapeDtypeStruct((3, w), F32),
                   jax.ShapeDtypeStruct((4, GROUP, GROUP), F32), jax.ShapeDtypeStruct((1, w), F32)],
        compiler_params=_params("arbitrary"))(proj, proj, proj, dmix, dmix, conv_a, w_pool, w_pool_t, pool_scale)


def _ffn_halo(ts, f, n_t, nxt):
    hb = ts // HALO
    if nxt:
        return pl.BlockSpec((None, HALO, f), lambda j, i: (j, jnp.minimum((i + 1) * hb, n_t * hb - 1), 0))
    return pl.BlockSpec((None, HALO, f), lambda j, i: (j, jnp.maximum(i * hb - 1, 0), 0))


def ffn_act_fwd(g, up, cw, cb, *, name):
    j_n, s, f = g.shape
    ts = min(ROW_TILE, s)
    n_t = s // ts

    def body(g_ref, gp_ref, u_ref, cw_ref, cb_ref, o_ref):
        keep = jnp.where(pl.program_id(1) > 0, 1.0, 0.0)
        g_ext = jnp.concatenate([gp_ref[...] * keep, g_ref[...]], axis=0)
        gc = _causal_conv(g_ext, cw_ref)[HALO:] + cb_ref[...]
        o_ref[...] = (gc * jax.nn.sigmoid(gc) * u_ref[...]).astype(BF16)

    tile = pl.BlockSpec((None, ts, f), lambda j, i: (j, i, 0))
    return pl.pallas_call(
        body, name=name, grid=(j_n, n_t),
        in_specs=[tile, _ffn_halo(ts, f, n_t, False), tile,
                  pl.BlockSpec((None, 3, f), lambda j, i: (j, 0, 0)),
                  pl.BlockSpec((None, 1, f), lambda j, i: (j, 0, 0))],
        out_specs=tile,
        out_shape=jax.ShapeDtypeStruct((j_n, s, f), BF16),
        compiler_params=_params("parallel", "parallel"))(g, g, up, cw, cb)


def ffn_act_bwd(g, up, dact, cw, cb, *, name):
    j_n, s, f = g.shape
    ts = min(ROW_TILE, s)
    n_t = s // ts

    def body(g_ref, gp_ref, gn_ref, u_ref, un_ref, d_ref, dn_ref, cw_ref, cb_ref,
             dg_ref, du_ref, dcw_ref, dcb_ref):
        i = pl.program_id(1)

        @pl.when(i == 0)
        def _():
            dcw_ref[...] = jnp.zeros_like(dcw_ref)
            dcb_ref[...] = jnp.zeros_like(dcb_ref)

        keep_p = jnp.where(i > 0, 1.0, 0.0)
        keep_n = jnp.where(i < n_t - 1, 1.0, 0.0)
        g_ext = jnp.concatenate([gp_ref[...] * keep_p, g_ref[...], gn_ref[...]], axis=0)
        gc = _causal_conv(g_ext, cw_ref)[HALO:] + cb_ref[...]
        sig = jax.nn.sigmoid(gc)
        dact_ext = jnp.concatenate([d_ref[...], dn_ref[...] * keep_n], axis=0)
        du_ref[...] = (dact_ext * gc * sig)[:ts].astype(BF16)
        up_ext = jnp.concatenate([u_ref[...], un_ref[...]], axis=0)
        dgc = dact_ext * up_ext * (sig * (1.0 + gc * (1.0 - sig)))
        dg_ref[...] = _causal_conv_t(dgc, cw_ref)[:ts].astype(BF16)
        dgc_t = dgc[:ts]
        dcb_ref[...] += _colsum(dgc_t)
        dcw_ref[2:3, :] += _colsum(dgc_t * g_ext[HALO:HALO + ts])
        dcw_ref[1:2, :] += _colsum(dgc_t * _back(g_ext, 1)[HALO:HALO + ts])
        dcw_ref[0:1, :] += _colsum(dgc_t * _back(g_ext, 2)[HALO:HALO + ts])

    tile = pl.BlockSpec((None, ts, f), lambda j, i: (j, i, 0))
    prev, nxt = _ffn_halo(ts, f, n_t, False), _ffn_halo(ts, f, n_t, True)
    return pl.pallas_call(
        body, name=name, grid=(j_n, n_t),
        in_specs=[tile, prev, nxt, tile, nxt, tile, nxt,
                  pl.BlockSpec((None, 3, f), lambda j, i: (j, 0, 0)),
                  pl.BlockSpec((None, 1, f), lambda j, i: (j, 0, 0))],
        out_specs=[tile, tile, pl.BlockSpec((None, 3, f), lambda j, i: (j, 0, 0)),
                   pl.BlockSpec((None, 1, f), lambda j, i: (j, 0, 0))],
        out_shape=[jax.ShapeDtypeStruct((j_n, s, f), BF16), jax.ShapeDtypeStruct((j_n, s, f), BF16),
                   jax.ShapeDtypeStruct((j_n, 3, f), F32), jax.ShapeDtypeStruct((j_n, 1, f), F32)],
        compiler_params=_params("parallel", "arbitrary"))(g, g, g, up, up, dact, dact, cw, cb)


def _sgu_gate(vn_bf, ws_ref, bs_ref, h, rows):
    tri = lax.broadcasted_iota(jnp.int32, (CHUNK, CHUNK), 0) >= lax.broadcasted_iota(jnp.int32, (CHUNK, CHUNK), 1)
    ws = jnp.where(tri, ws_ref[h], 0.0).astype(BF16)
    cols = slice((h % 4) * GROUP, (h % 4 + 1) * GROUP)
    return ws, jnp.dot(ws, vn_bf[h // 4][rows, cols], preferred_element_type=F32) + bs_ref[h]


def sgu_fwd(p, sgu_norm, w_spatial, b_spatial, *, name):
    _, s, w = p.shape
    ts = min(ROW_TILE, s)
    n_heads = w_spatial.shape[0]

    def body(p_ref, n_ref, ws_ref, bs_ref, o_ref, r_ref):
        v = [_gelu(p_ref[2]), _gelu(p_ref[3])]
        ms = (jnp.sum(v[0] * v[0], axis=-1, keepdims=True) + jnp.sum(v[1] * v[1], axis=-1, keepdims=True)) / (2 * w)
        rstd = lax.rsqrt(ms + EPS)
        r_ref[...] = rstd
        vn = [(v[k] * rstd * n_ref[:, k * w:(k + 1) * w]).astype(BF16) for k in range(2)]
        for h in range(n_heads):
            cols = slice((h % 4) * GROUP, (h % 4 + 1) * GROUP)
            for c in range(ts // CHUNK):
                rows = slice(c * CHUNK, (c + 1) * CHUNK)
                _, gate = _sgu_gate(vn, ws_ref, bs_ref, h, rows)
                u = _gelu(p_ref[h // 4, rows, cols])
                o_ref[rows, h * GROUP:(h + 1) * GROUP] = (u * gate).astype(BF16)

    const = lambda shape: pl.BlockSpec(shape, lambda i: (0,) * len(shape))
    return pl.pallas_call(
        body, name=name, grid=(s // ts,),
        in_specs=[pl.BlockSpec((4, ts, w), lambda i: (0, i, 0)), const((1, 2 * w)),
                  const((n_heads, CHUNK, CHUNK)), const((n_heads, CHUNK, 1))],
        out_specs=[pl.BlockSpec((ts, 2 * w), lambda i: (i, 0)), pl.BlockSpec((ts, 1), lambda i: (i, 0))],
        out_shape=[jax.ShapeDtypeStruct((s, 2 * w), BF16), jax.ShapeDtypeStruct((s, 1), F32)],
        compiler_params=_params("parallel"))(p, sgu_norm, w_spatial, b_spatial)


def sgu_bwd(p, dmix, rstd, sgu_norm, w_spatial, b_spatial, *, name):
    _, s, w = p.shape
    ts = min(ROW_TILE, s)
    n_heads = w_spatial.shape[0]

    def body(p_ref, dm_ref, r_ref, n_ref, ws_ref, bs_ref, dp_ref, dn_ref, dws_ref, dbs_ref, dvn_ref):
        @pl.when(pl.program_id(0) == 0)
        def _():
            dn_ref[...] = jnp.zeros_like(dn_ref)
            dws_ref[...] = jnp.zeros_like(dws_ref)
            dbs_ref[...] = jnp.zeros_like(dbs_ref)

        rstd_v = r_ref[...]
        vhat = [_gelu(p_ref[2 + k]) * rstd_v for k in range(2)]
        vn = [(vhat[k] * n_ref[:, k * w:(k + 1) * w]).astype(BF16) for k in range(2)]
        tri = lax.broadcasted_iota(jnp.int32, (CHUNK, CHUNK), 0) >= lax.broadcasted_iota(jnp.int32, (CHUNK, CHUNK), 1)
        for h in range(n_heads):
            cols = slice((h % 4) * GROUP, (h % 4 + 1) * GROUP)
            ocols = slice(h * GROUP, (h + 1) * GROUP)
            for c in range(ts // CHUNK):
                rows = slice(c * CHUNK, (c + 1) * CHUNK)
                ws, gate = _sgu_gate(vn, ws_ref, bs_ref, h, rows)
                pu = p_ref[h // 4, rows, cols]
                dm = dm_ref[rows, ocols]
                dp_ref[h // 4, rows, cols] = (dm * gate * _gelu_grad(pu)).astype(BF16)
                dgate = dm * _gelu(pu)
                dbs_ref[h] += jnp.sum(dgate, axis=-1, keepdims=True)
                dgate_bf = dgate.astype(BF16)
                dws = lax.dot_general(dgate_bf, vn[h // 4][rows, cols], _NT, preferred_element_type=F32)
                dws_ref[h] += jnp.where(tri, dws, 0.0)
                dvn_ref[rows, ocols] = lax.dot_general(ws, dgate_bf, _TN, preferred_element_type=F32)
        for k in range(2):
            kc = slice(k * w, (k + 1) * w)
            dvn = dvn_ref[:, kc]
            dn_ref[:, kc] += _colsum(dvn * vhat[k])
        dvh = [dvn_ref[:, k * w:(k + 1) * w] * n_ref[:, k * w:(k + 1) * w] for k in range(2)]
        dot = (jnp.sum(dvh[0] * vhat[0], axis=-1, keepdims=True)
               + jnp.sum(dvh[1] * vhat[1], axis=-1, keepdims=True)) / (2 * w)
        for k in range(2):
            dv = rstd_v * (dvh[k] - vhat[k] * dot)
            dp_ref[2 + k] = (dv * _gelu_grad(p_ref[2 + k])).astype(BF16)

    const = lambda shape: pl.BlockSpec(shape, lambda i: (0,) * len(shape))
    tile4 = pl.BlockSpec((4, ts, w), lambda i: (0, i, 0))
    return pl.pallas_call(
        body, name=name, grid=(s // ts,),
        in_specs=[tile4, pl.BlockSpec((ts, 2 * w), lambda i: (i, 0)), pl.BlockSpec((ts, 1), lambda i: (i, 0)),
                  const((1, 2 * w)), const((n_heads, CHUNK, CHUNK)), const((n_heads, CHUNK, 1))],
        out_specs=[tile4, const((1, 2 * w)), const((n_heads, CHUNK, CHUNK)), const((n_heads, CHUNK, 1))],
        out_shape=[jax.ShapeDtypeStruct((4, s, w), BF16), jax.ShapeDtypeStruct((1, 2 * w), F32),
                   jax.ShapeDtypeStruct((n_heads, CHUNK, CHUNK), F32),
                   jax.ShapeDtypeStruct((n_heads, CHUNK, 1), F32)],
        scratch_shapes=[pltpu.VMEM((ts, 2 * w), F32)],
        compiler_params=_params("arbitrary"))(p, dmix, rstd, sgu_norm, w_spatial, b_spatial)


def _row_tile(rows):
    if rows <= ROW_TILE:
        return rows
    for t in (512, 384, 352, 256, 128, 64, 32, 16, 8):
        if rows % t == 0:
            return t
    return rows


def adamw(w, g, m, v, *, name):
    shape = w.shape
    cols = shape[-1]
    rows = w.size // cols
    w2, g2, m2, v2 = (a.reshape(rows, cols) for a in (w, g, m, v))
    tr = _row_tile(rows)
    bc1 = 1.0 - ADAM_B1 ** ADAM_STEP
    bc2 = 1.0 - ADAM_B2 ** ADAM_STEP

    def body(w_ref, g_ref, m_ref, v_ref, d_ref, nm_ref, nv_ref):
        grad = g_ref[...]
        m_new = ADAM_B1 * m_ref[...] + (1.0 - ADAM_B1) * grad
        v_new = ADAM_B2 * v_ref[...] + (1.0 - ADAM_B2) * (grad * grad)
        nm_ref[...] = m_new
        nv_ref[...] = v_new
        d_ref[...] = -ADAM_LR * ((m_new / bc1) / (jnp.sqrt(v_new / bc2) + ADAM_EPS) + ADAM_WD * w_ref[...])

    spec = pl.BlockSpec((tr, cols), lambda i: (i, 0))
    outs = pl.pallas_call(
        body, name=name, grid=(rows // tr,),
        in_specs=[spec] * 4, out_specs=[spec] * 3,
        out_shape=[jax.ShapeDtypeStruct((rows, cols), F32)] * 3,
        compiler_params=_params("parallel"))(w2, g2, m2, v2)
    return tuple(o.reshape(shape) for o in outs)


def _place():
    return lax.axis_index("x"), lax.axis_index("y"), lax.axis_index("c")


def _other_chips(x, y):
    return [(1 - x, y), (x, 1 - y), (1 - x, 1 - y)]


HBM = pl.BlockSpec(memory_space=pltpu.HBM)
SEM = pl.BlockSpec(memory_space=pltpu.SEMAPHORE)
DATAFLOW = pltpu.SideEffectType.DATAFLOW_SIDE_EFFECTING


def _in_hbm(a):
    return pltpu.with_memory_space_constraint(a, pltpu.HBM)


def cast_into_slot(w, chip, *, l=None, name):
    rows, cols = w.shape[-2:]
    tr = _row_tile(rows)

    def body(chip_ref, w_ref, o_ref):
        o_ref[...] = w_ref[...].astype(BF16)

    in_spec = (pl.BlockSpec((tr, cols), lambda i, chip_ref: (i, 0)) if l is None
               else pl.BlockSpec((None, tr, cols), lambda i, chip_ref: (l, i, 0)))
    return pl.pallas_call(
        body, name=name,
        grid_spec=pltpu.PrefetchScalarGridSpec(
            num_scalar_prefetch=1, grid=(rows // tr,), in_specs=[in_spec],
            out_specs=pl.BlockSpec((None, tr, cols), lambda i, chip_ref: (chip_ref[0], i, 0))),
        out_shape=jax.ShapeDtypeStruct((N_CHIPS, rows, cols), BF16),
        compiler_params=_params("parallel"))(chip, w)


def _half(ref, slot, c):
    half = ref.shape[1] // 2
    return ref.at[slot, pl.ds(c * half, half), :]


def gather_start(groups, smalls):
    flat = [b for g in groups for b in g]
    n_b, n_s, n_g = len(flat), len(smalls), len(groups)

    def body(*refs):
        bufs, small_refs = refs[:n_b], refs[n_b:n_b + n_s]
        sems = refs[n_b + n_s:n_b + n_s + 2 * n_g + 2]
        token = refs[-1]
        x, y, c = _place()
        me = 2 * x + y
        chips = _other_chips(x, y)
        t = 0
        for gi, group in enumerate(groups):
            for ti in range(len(group)):
                piece = _half(bufs[t], me, c)
                t += 1
                for k, (px, py) in enumerate(chips):
                    pltpu.make_async_remote_copy(
                        src_ref=piece, dst_ref=piece,
                        send_sem=sems[2 * gi].at[3 * ti + k], recv_sem=sems[2 * gi + 1].at[3 * ti + k],
                        device_id=(px, py, c), device_id_type=MESH).start()
        for si in range(n_s):
            piece = small_refs[si].at[me]
            for k, (px, py) in enumerate(chips):
                pltpu.make_async_remote_copy(
                    src_ref=piece, dst_ref=piece,
                    send_sem=sems[2 * n_g].at[3 * si + k], recv_sem=sems[2 * n_g + 1].at[3 * si + k],
                    device_id=(px, py, c), device_id_type=MESH).start()
        token[...] = jnp.zeros_like(token)

    sem_shapes = []
    for group in groups:
        sem_shapes += [pltpu.SemaphoreType.DMA((3 * len(group),))] * 2
    sem_shapes += [pltpu.SemaphoreType.DMA((3 * n_s,))] * 2
    arrays = flat + list(smalls)
    n_sem = len(sem_shapes)
    res = pl.pallas_call(
        body, name="gather_start",
        out_shape=tuple(sem_shapes) + tuple(pltpu.HBM(a.shape, a.dtype) for a in arrays)
        + (jax.ShapeDtypeStruct((8, 128), F32),),
        in_specs=[HBM] * len(arrays),
        out_specs=tuple([SEM] * n_sem + [HBM] * len(arrays) + [pl.BlockSpec(memory_space=pltpu.VMEM)]),
        input_output_aliases={i: n_sem + i for i in range(len(arrays))},
        compiler_params=pltpu.CompilerParams(has_side_effects=DATAFLOW))(*[_in_hbm(a) for a in arrays])
    sems, thru, token = res[:n_sem], res[n_sem:-1], res[-1]
    out_groups, t = [], 0
    for group in groups:
        out_groups.append(list(thru[t:t + len(group)]))
        t += len(group)
    return sems, out_groups, list(thru[n_b:]), token


def gather_wait(bufs, send, recv, after, *, name, smalls=(), small_send=None, small_recv=None):
    n_b, n_s = len(bufs), len(smalls)
    arrays = list(bufs) + list(smalls)
    sem_ops = [send, recv] + ([small_send, small_recv] if n_s else [])

    def body(*refs):
        buf_refs, small_refs = refs[:n_b], refs[n_b:n_b + n_s]
        sems = refs[n_b + n_s:n_b + n_s + len(sem_ops)]
        x, y, c = _place()
        me = 2 * x + y
        chips = _other_chips(x, y)
        for ti in range(n_b):
            for k, (px, py) in enumerate(chips):
                done = pltpu.make_async_remote_copy(
                    src_ref=_half(buf_refs[ti], me, c), dst_ref=_half(buf_refs[ti], 2 * px + py, c),
                    send_sem=sems[0].at[3 * ti + k], recv_sem=sems[1].at[3 * ti + k],
                    device_id=(px, py, c), device_id_type=MESH)
                done.wait_send()
                done.wait_recv()
        for si in range(n_s):
            for k, (px, py) in enumerate(chips):
                done = pltpu.make_async_remote_copy(
                    src_ref=small_refs[si].at[me], dst_ref=small_refs[si].at[2 * px + py],
                    send_sem=sems[2].at[3 * si + k], recv_sem=sems[3].at[3 * si + k],
                    device_id=(px, py, c), device_id_type=MESH)
                done.wait_send()
                done.wait_recv()

    res = pl.pallas_call(
        body, name=name,
        out_shape=tuple(pltpu.HBM(a.shape, a.dtype) for a in arrays),
        in_specs=[HBM] * len(arrays) + [SEM] * len(sem_ops) + [ANY],
        out_specs=tuple([HBM] * len(arrays)),
        input_output_aliases={i: i for i in range(len(arrays))},
        compiler_params=pltpu.CompilerParams(has_side_effects=DATAFLOW))(*arrays, *sem_ops, after)
    return list(res[:n_b]), list(res[n_b:])


def gather_forward(bufs, *, name):
    n = len(bufs)

    def body(*refs):
        ins, outs = refs[:n], refs[n:2 * n]
        send_sems, recv_sems = refs[2 * n:]
        x, y, c = _place()
        chips = _other_chips(x, y)
        for t in range(n):
            for k, (px, py) in enumerate(chips):
                pltpu.make_async_remote_copy(
                    src_ref=_half(ins[t], 2 * px + py, c), dst_ref=_half(outs[t], 2 * px + py, c),
                    send_sem=send_sems.at[3 * t + k], recv_sem=recv_sems.at[3 * t + k],
                    device_id=(x, y, 1 - c), device_id_type=MESH).start()
        for t in range(n):
            for k, (px, py) in enumerate(chips):
                done = pltpu.make_async_remote_copy(
                    src_ref=_half(ins[t], 2 * px + py, c), dst_ref=_half(outs[t], 2 * px + py, 1 - c),
                    send_sem=send_sems.at[3 * t + k], recv_sem=recv_sems.at[3 * t + k],
                    device_id=(x, y, 1 - c), device_id_type=MESH)
                done.wait_send()
                done.wait_recv()

    return pl.pallas_call(
        body, name=name, in_specs=[ANY] * n, out_specs=[ANY] * n,
        out_shape=[jax.ShapeDtypeStruct(a.shape, a.dtype) for a in bufs],
        input_output_aliases={i: i for i in range(n)},
        scratch_shapes=[pltpu.SemaphoreType.DMA((3 * n,)), pltpu.SemaphoreType.DMA((3 * n,))],
        compiler_params=pltpu.CompilerParams(has_side_effects=True))(*bufs)


def reduce_stage_a(grads_bf, *, name):
    n = len(grads_bf)

    def body(*refs):
        srcs, outs = refs[:n], refs[n:2 * n]
        send_sems, recv_sems = refs[2 * n:]
        x, y, c = _place()
        for t in range(n):
            half = srcs[t].shape[1] // 2
            pltpu.make_async_remote_copy(
                src_ref=srcs[t].at[:, pl.ds((1 - c) * half, half), :], dst_ref=outs[t],
                send_sem=send_sems.at[t], recv_sem=recv_sems.at[t],
                device_id=(x, y, 1 - c), device_id_type=MESH).start()
        for t in range(n):
            half = srcs[t].shape[1] // 2
            done = pltpu.make_async_remote_copy(
                src_ref=srcs[t].at[:, pl.ds((1 - c) * half, half), :], dst_ref=outs[t],
                send_sem=send_sems.at[t], recv_sem=recv_sems.at[t],
                device_id=(x, y, 1 - c), device_id_type=MESH)
            done.wait_send()
            done.wait_recv()

    out_shape = [jax.ShapeDtypeStruct((a.shape[0], a.shape[1] // 2, a.shape[2]), a.dtype) for a in grads_bf]
    return pl.pallas_call(
        body, name=name, in_specs=[ANY] * n, out_specs=[ANY] * n, out_shape=out_shape,
        scratch_shapes=[pltpu.SemaphoreType.DMA((n,)), pltpu.SemaphoreType.DMA((n,))],
        compiler_params=pltpu.CompilerParams(has_side_effects=True))(*grads_bf)


def sum_stage_a(grad, recv, place, *, name):
    j_n, half, cols = recv.shape

    def body(place_ref, g_ref, r_ref, o_ref, ob_ref):
        acc = g_ref[...] + r_ref[...].astype(F32)
        o_ref[...] = acc
        ob_ref[...] = acc.astype(BF16)

    blk = (None, half, cols)
    return pl.pallas_call(
        body, name=name,
        grid_spec=pltpu.PrefetchScalarGridSpec(
            num_scalar_prefetch=1, grid=(j_n,),
            in_specs=[pl.BlockSpec(blk, lambda j, place_ref: (j, place_ref[1], 0)),
                      pl.BlockSpec(blk, lambda j, place_ref: (j, 0, 0))],
            out_specs=[pl.BlockSpec(blk, lambda j, place_ref: (j, 0, 0))] * 2),
        out_shape=[jax.ShapeDtypeStruct(recv.shape, F32), jax.ShapeDtypeStruct(recv.shape, BF16)],
        compiler_params=_params("parallel"))(place, grad, recv)


def reduce_b_start(parts_bf, *, name):
    n = len(parts_bf)
    lands = [lax.empty((3,) + a.shape[1:], a.dtype) for a in parts_bf]

    def body(*refs):
        srcs, land = refs[:n], refs[n:2 * n]
        send_sems, recv_sems = refs[2 * n], refs[2 * n + 1]
        token = refs[-1]
        x, y, c = _place()
        chips = _other_chips(x, y)
        for t in range(n):
            for k, (px, py) in enumerate(chips):
                pltpu.make_async_remote_copy(
                    src_ref=srcs[t].at[2 * px + py], dst_ref=land[t].at[k],
                    send_sem=send_sems.at[3 * t + k], recv_sem=recv_sems.at[3 * t + k],
                    device_id=(px, py, c), device_id_type=MESH).start()
        token[...] = jnp.zeros_like(token)

    arrays = list(parts_bf) + lands
    res = pl.pallas_call(
        body, name=name,
        out_shape=(pltpu.SemaphoreType.DMA((3 * n,)), pltpu.SemaphoreType.DMA((3 * n,)))
        + tuple(pltpu.HBM(a.shape, a.dtype) for a in arrays) + (jax.ShapeDtypeStruct((8, 128), F32),),
        in_specs=[HBM] * (2 * n),
        out_specs=tuple([SEM, SEM] + [HBM] * (2 * n) + [pl.BlockSpec(memory_space=pltpu.VMEM)]),
        input_output_aliases={i: 2 + i for i in range(2 * n)},
        compiler_params=pltpu.CompilerParams(has_side_effects=DATAFLOW))(*[_in_hbm(a) for a in arrays])
    return res[0], res[1], list(res[2:2 + n]), list(res[2 + n:2 + 2 * n]), res[-1]


def reduce_b_wait(srcs, lands, send, recv, after, *, name):
    n = len(srcs)

    def body(*refs):
        src_refs, land = refs[:n], refs[n:2 * n]
        send_sems, recv_sems = refs[2 * n], refs[2 * n + 1]
        x, y, c = _place()
        chips = _other_chips(x, y)
        for t in range(n):
            for k, (px, py) in enumerate(chips):
                done = pltpu.make_async_remote_copy(
                    src_ref=src_refs[t].at[2 * px + py], dst_ref=land[t].at[k],
                    send_sem=send_sems.at[3 * t + k], recv_sem=recv_sems.at[3 * t + k],
                    device_id=(px, py, c), device_id_type=MESH)
                done.wait_send()
                done.wait_recv()

    arrays = list(srcs) + list(lands)
    res = pl.pallas_call(
        body, name=name,
        out_shape=tuple(pltpu.HBM(a.shape, a.dtype) for a in arrays),
        in_specs=[HBM] * (2 * n) + [SEM, SEM, ANY],
        out_specs=tuple([HBM] * (2 * n)),
        input_output_aliases={i: i for i in range(2 * n)},
        compiler_params=pltpu.CompilerParams(has_side_effects=DATAFLOW))(*arrays, send, recv, after)
    return list(res[n:])


def sum_stage_b(part, recv, place, *, name):
    _, half, cols = part.shape

    def body(place_ref, p_ref, r_ref, o_ref):
        acc = p_ref[...]
        for k in range(3):
            acc = acc + r_ref[k].astype(F32)
        o_ref[...] = acc

    return pl.pallas_call(
        body, name=name,
        grid_spec=pltpu.PrefetchScalarGridSpec(
            num_scalar_prefetch=1, grid=(1,),
            in_specs=[pl.BlockSpec((None, half, cols), lambda i, place_ref: (place_ref[0], 0, 0)),
                      pl.BlockSpec((3, half, cols), lambda i, place_ref: (0, 0, 0))],
            out_specs=pl.BlockSpec((half, cols), lambda i, place_ref: (place_ref[1], 0))),
        out_shape=jax.ShapeDtypeStruct((2 * half, cols), F32),
        compiler_params=_params("arbitrary"))(place, part, recv)


def reduce_stage_c(fulls, *, name):
    n = len(fulls)

    def body(*refs):
        ins, outs = refs[:n], refs[n:2 * n]
        send_sems, recv_sems = refs[2 * n:]
        x, y, c = _place()
        for t in range(n):
            half = ins[t].shape[0] // 2
            pltpu.make_async_remote_copy(
                src_ref=ins[t].at[pl.ds(c * half, half), :], dst_ref=outs[t].at[pl.ds(c * half, half), :],
                send_sem=send_sems.at[t], recv_sem=recv_sems.at[t],
                device_id=(x, y, 1 - c), device_id_type=MESH).start()
        for t in range(n):
            half = ins[t].shape[0] // 2
            done = pltpu.make_async_remote_copy(
                src_ref=ins[t].at[pl.ds(c * half, half), :], dst_ref=outs[t].at[pl.ds((1 - c) * half, half), :],
                send_sem=send_sems.at[t], recv_sem=recv_sems.at[t],
                device_id=(x, y, 1 - c), device_id_type=MESH)
            done.wait_send()
            done.wait_recv()

    return pl.pallas_call(
        body, name=name, in_specs=[ANY] * n, out_specs=[ANY] * n,
        out_shape=[jax.ShapeDtypeStruct(a.shape, a.dtype) for a in fulls],
        input_output_aliases={i: i for i in range(n)},
        scratch_shapes=[pltpu.SemaphoreType.DMA((n,)), pltpu.SemaphoreType.DMA((n,))],
        compiler_params=pltpu.CompilerParams(has_side_effects=True))(*fulls)


def gather_small(slots, *, name):
    def body(in_ref, out_ref, send_sems, recv_sems):
        x, y, c = _place()
        dev = 4 * x + 2 * y + c
        peers = [(x ^ ((k >> 2) & 1), y ^ ((k >> 1) & 1), c ^ (k & 1)) for k in range(1, N_DEV)]
        for k, peer in enumerate(peers):
            pltpu.make_async_remote_copy(
                src_ref=in_ref.at[dev], dst_ref=out_ref.at[dev],
                send_sem=send_sems.at[k], recv_sem=recv_sems.at[k],
                device_id=peer, device_id_type=MESH).start()
        for k, (px, py, pc) in enumerate(peers):
            done = pltpu.make_async_remote_copy(
                src_ref=in_ref.at[dev], dst_ref=out_ref.at[4 * px + 2 * py + pc],
                send_sem=send_sems.at[k], recv_sem=recv_sems.at[k],
                device_id=(px, py, pc), device_id_type=MESH)
            done.wait_send()
            done.wait_recv()

    return pl.pallas_call(
        body, name=name, in_specs=[ANY], out_specs=ANY,
        out_shape=jax.ShapeDtypeStruct(slots.shape, slots.dtype),
        input_output_aliases={0: 0},
        scratch_shapes=[pltpu.SemaphoreType.DMA((N_DEV - 1,)), pltpu.SemaphoreType.DMA((N_DEV - 1,))],
        compiler_params=pltpu.CompilerParams(has_side_effects=True))(slots)


def sum_small(gathered, *, name):
    _, rows, cols = gathered.shape

    def body(g_ref, o_ref):
        acc = g_ref[0]
        for d in range(1, N_DEV):
            acc = acc + g_ref[d]
        o_ref[...] = acc

    return pl.pallas_call(
        body, name=name, grid=(1,),
        in_specs=[pl.BlockSpec((N_DEV, rows, cols), lambda i: (0, 0, 0))],
        out_specs=pl.BlockSpec((rows, cols), lambda i: (0, 0)),
        out_shape=jax.ShapeDtypeStruct((rows, cols), F32),
        compiler_params=_params("arbitrary"))(gathered)


def _ffn_fwd(h, hn, rstd, wg, wu, wd, cw, cb, l):
    g = mm_nn(hn, wg, name=f"ffn{l}_gate")
    up = mm_nn(hn, wu, name=f"ffn{l}_up")
    act = ffn_act_fwd(g, up, cw, cb, name=f"ffn{l}_act")
    out = mm_acc(act, wd, h, name=f"ffn{l}_down")
    return out, (hn, rstd, g, up, act)


def _ffn_bwd(dh, dh_bf, h, gain, saved, wg, wu, wd, cw, cb, l, pin):
    hn, rstd, g, up, act = saved
    dwd = mm_tn(act, dh_bf[None], name=f"ffn{l}_dwd", pin=pin)
    dact = mm_nt_each(dh_bf, wd, name=f"ffn{l}_dact")
    dg, dup, dcw, dcb = ffn_act_bwd(g, up, dact, cw, cb, name=f"ffn{l}_act_bwd")
    dwg = mm_tn(hn[None], dg, name=f"ffn{l}_dwg")
    dwu = mm_tn(hn[None], dup, name=f"ffn{l}_dwu")
    dhn = mm_nt_sum([(dg, wg), (dup, wu)], name=f"ffn{l}_dhn")
    dh_in, dh_in_bf, dgain = rms_bwd(dhn, h, gain, rstd, dh, name=f"ffn{l}_rms_bwd")
    return dh_in, dh_in_bf, (dwg, dwu, dwd), (dcw, dcb, dgain)


def _reduce_to_chip(grads, place, *, tag):
    recv_a = reduce_stage_a([g[1] for g in grads], name=f"reduce_a_{tag}")
    parts = [sum_stage_a(g[0], r, place, name=f"sum_a_{tag}{i}") for i, (g, r) in enumerate(zip(grads, recv_a))]
    send, recv, srcs, lands, token = reduce_b_start([p[1] for p in parts], name=f"reduce_b_start_{tag}")
    return [p[0] for p in parts], send, recv, srcs, lands, token


def _reduce_finish(state, place, after, *, tag):
    parts, send, recv, srcs, lands, _ = state
    recv_b = reduce_b_wait(srcs, lands, send, recv, after, name=f"reduce_b_wait_{tag}")
    halves = [sum_stage_b(p, r, place, name=f"sum_b_{tag}{i}") for i, (p, r) in enumerate(zip(parts, recv_b))]
    return reduce_stage_c(halves, name=f"reduce_c_{tag}")


def kernel(x, norm_mix, norm_ffn, final_norm, w_in_even, conv_a, w_pool, pool_scale, w_out_even, w_in_odd, sgu_norm, w_spatial, b_spatial, w_out_odd, w_ffn_gate, w_ffn_up, conv_ffn, b_conv_ffn, w_ffn_down, loss_target, m_norm_mix, m_norm_ffn, m_final_norm, m_w_in_even, m_conv_a, m_w_pool, m_pool_scale, m_w_out_even, m_w_in_odd, m_sgu_norm, m_w_spatial, m_b_spatial, m_w_out_odd, m_w_ffn_gate, m_w_ffn_up, m_conv_ffn, m_b_conv_ffn, m_w_ffn_down, v_norm_mix, v_norm_ffn, v_final_norm, v_w_in_even, v_conv_a, v_w_pool, v_pool_scale, v_w_out_even, v_w_in_odd, v_sgu_norm, v_w_spatial, v_b_spatial, v_w_out_odd, v_w_ffn_gate, v_w_ffn_up, v_conv_ffn, v_b_conv_ffn, v_w_ffn_down):
    weights = dict(norm_mix=norm_mix, norm_ffn=norm_ffn, final_norm=final_norm, w_in_even=w_in_even,
                   conv_a=conv_a, w_pool=w_pool, pool_scale=pool_scale, w_out_even=w_out_even,
                   w_in_odd=w_in_odd, sgu_norm=sgu_norm, w_spatial=w_spatial, b_spatial=b_spatial,
                   w_out_odd=w_out_odd, w_ffn_gate=w_ffn_gate, w_ffn_up=w_ffn_up, conv_ffn=conv_ffn,
                   b_conv_ffn=b_conv_ffn, w_ffn_down=w_ffn_down)
    m_in = dict(norm_mix=m_norm_mix, norm_ffn=m_norm_ffn, final_norm=m_final_norm, w_in_even=m_w_in_even,
                conv_a=m_conv_a, w_pool=m_w_pool, pool_scale=m_pool_scale, w_out_even=m_w_out_even,
                w_in_odd=m_w_in_odd, sgu_norm=m_sgu_norm, w_spatial=m_w_spatial, b_spatial=m_b_spatial,
                w_out_odd=m_w_out_odd, w_ffn_gate=m_w_ffn_gate, w_ffn_up=m_w_ffn_up, conv_ffn=m_conv_ffn,
                b_conv_ffn=m_b_conv_ffn, w_ffn_down=m_w_ffn_down)
    v_in = dict(norm_mix=v_norm_mix, norm_ffn=v_norm_ffn, final_norm=v_final_norm, w_in_even=v_w_in_even,
                conv_a=v_conv_a, w_pool=v_w_pool, pool_scale=v_pool_scale, w_out_even=v_w_out_even,
                w_in_odd=v_w_in_odd, sgu_norm=v_sgu_norm, w_spatial=v_w_spatial, b_spatial=v_b_spatial,
                w_out_odd=v_w_out_odd, w_ffn_gate=v_w_ffn_gate, w_ffn_up=v_w_ffn_up, conv_ffn=v_conv_ffn,
                b_conv_ffn=v_b_conv_ffn, w_ffn_down=v_w_ffn_down)
    order = list(weights)

    chip = 2 * lax.axis_index("x") + lax.axis_index("y")
    core = lax.axis_index("c")
    place = jnp.stack([chip, core]).astype(jnp.int32)
    chip_arr = place[:1]

    h0 = x[0]
    target = loss_target[0]
    d_model = h0.shape[1]
    f_shard = w_ffn_gate.shape[-1]

    def own_slot(v):
        return lax.dynamic_update_index_in_dim(jnp.zeros((N_CHIPS,) + v.shape, v.dtype), v, chip, 0)

    groups = [
        [cast_into_slot(w_in_even[0], chip_arr, name="cast_win_e"),
         cast_into_slot(w_out_even[0], chip_arr, name="cast_wout_e")],
        [cast_into_slot(w_ffn_gate, chip_arr, l=0, name="cast_wg0"),
         cast_into_slot(w_ffn_up, chip_arr, l=0, name="cast_wu0"),
         cast_into_slot(w_ffn_down, chip_arr, l=0, name="cast_wd0")],
        [cast_into_slot(w_in_odd[0], chip_arr, name="cast_win_o"),
         cast_into_slot(w_out_odd[0], chip_arr, name="cast_wout_o")],
        [cast_into_slot(w_ffn_gate, chip_arr, l=1, name="cast_wg1"),
         cast_into_slot(w_ffn_up, chip_arr, l=1, name="cast_wu1"),
         cast_into_slot(w_ffn_down, chip_arr, l=1, name="cast_wd1")]]
    smalls = [own_slot(conv_a[0]), own_slot(sgu_norm), own_slot(conv_ffn[0]), own_slot(conv_ffn[1])]
    sems, groups, smalls, token = gather_start(groups, smalls)

    def arrive(gi, after, with_smalls=False):
        kw = dict(smalls=smalls, small_send=sems[-2], small_recv=sems[-1]) if with_smalls else {}
        bufs, small_out = gather_wait(groups[gi], sems[2 * gi], sems[2 * gi + 1], after, name=f"gather_wait{gi}", **kw)
        return gather_forward(bufs, name=f"gather_forward{gi}"), small_out

    cb = b_conv_ffn.reshape(-1, N_CHIPS, 1, f_shard)
    wp_bf = w_pool[0].astype(BF16)
    wp_t_bf = jnp.transpose(w_pool[0], (0, 2, 1)).astype(BF16)
    ws = w_spatial[0]
    bs = b_spatial[0][:, :, None]

    xn0, rstd0 = rms_fwd(h0, norm_mix[0:1], name="l0_rms", pin=token)
    (win_e, wout_e), (ca_g, sn_g, cw0, cw1) = arrive(0, xn0, with_smalls=True)
    wout_e = wout_e.reshape(1, -1, d_model)
    ca_full = jnp.transpose(ca_g, (1, 0, 2)).reshape(ca_g.shape[1], -1)
    sn_full = sn_g.reshape(1, -1)
    proj0 = mm_nn(xn0, win_e, name="l0_in")
    mix0 = even_fwd(proj0, ca_full, wp_bf, pool_scale, name="l0_mix")
    h1 = mm_acc(mix0[None], wout_e, h0, name="l0_out")
    hn0, rstdf0 = rms_fwd(h1, norm_ffn[0:1], name="ffn0_rms")
    (wg0, wu0, wd0), _ = arrive(1, hn0)
    h2, ffn0 = _ffn_fwd(h1, hn0, rstdf0, wg0, wu0, wd0, cw0, cb[0], 0)
    xn1, rstd1 = rms_fwd(h2, norm_mix[1:2], name="l1_rms")
    (win_o, wout_o), _ = arrive(2, xn1)
    wout_o = wout_o.reshape(1, -1, d_model)
    p1 = mm_nn(xn1, win_o, name="l1_in")
    mix1, rstd_v = sgu_fwd(p1, sn_full, ws, bs, name="l1_mix")
    h3 = mm_acc(mix1[None], wout_o, h2, name="l1_out")
    hn1, rstdf1 = rms_fwd(h3, norm_ffn[1:2], name="ffn1_rms")
    (wg1, wu1, wd1), _ = arrive(3, hn1)
    h4, ffn1 = _ffn_fwd(h3, hn1, rstdf1, wg1, wu1, wd1, cw1, cb[1], 1)

    dh4, dh4_bf, loss_row, d_final = final_loss(h4, target, final_norm[None], name="loss")
    loss = lax.psum(loss_row[0, 0], ("x", "y", "c"))

    dh3, dh3_bf, big3, (dcw1, dcb1, dnf1) = _ffn_bwd(
        dh4, dh4_bf, h3, norm_ffn[1:2], ffn1, wg1, wu1, wd1, cw1, cb[1], 1, None)
    red3 = _reduce_to_chip(big3, place, tag="g3")

    def as_blocks(pair):
        return tuple(a.reshape(N_CHIPS, -1, d_model) for a in pair)

    dwout_o = as_blocks(mm_tn(mix1[None], dh3_bf[None], name="l1_dwout", pin=red3[-1]))
    dmix1 = mm_nt_each(dh3_bf, wout_o, name="l1_dmix")[0]
    dp1, dsn, dws, dbs = sgu_bwd(p1, dmix1, rstd_v, sn_full, ws, bs, name="l1_mix_bwd")
    dwin_o = mm_tn(xn1[None], dp1, name="l1_dwin")
    dxn1 = mm_nt_sum([(dp1, win_o)], name="l1_dxn")
    dh2, dh2_bf, dnm1 = rms_bwd(dxn1, h2, norm_mix[1:2], rstd1, dh3, name="l1_rms_bwd")
    red2 = _reduce_to_chip([dwin_o, dwout_o], place, tag="g2")

    dh1, dh1_bf, big1, (dcw0, dcb0, dnf0) = _ffn_bwd(
        dh2, dh2_bf, h1, norm_ffn[0:1], ffn0, wg0, wu0, wd0, cw0, cb[0], 0, red2[-1])
    red1 = _reduce_to_chip(big1, place, tag="g1")

    dwout_e = as_blocks(mm_tn(mix0[None], dh1_bf[None], name="l0_dwout", pin=red1[-1]))
    dmix0 = mm_nt_each(dh1_bf, wout_e, name="l0_dmix")[0]
    dproj0, dca, dwp, dps = even_bwd(proj0, dmix0, ca_full, wp_bf, wp_t_bf, pool_scale, name="l0_mix_bwd")
    dwin_e = mm_tn(xn0[None], dproj0, name="l0_dwin")
    dxn0 = mm_nt_sum([(dproj0, win_e)], name="l0_dxn")
    dh0, _, dnm0 = rms_bwd(dxn0, h0, norm_mix[0:1], rstd0, dh1, name="l0_rms_bwd")
    grad_x = dh0[None]
    red0 = _reduce_to_chip([dwin_e, dwout_e], place, tag="g0")

    small_parts = dict(
        norm_mix=jnp.concatenate([dnm0, dnm1]), norm_ffn=jnp.concatenate([dnf0, dnf1]), final_norm=d_final,
        conv_a=dca, w_pool=dwp, pool_scale=dps, sgu_norm=dsn, w_spatial=dws, b_spatial=dbs,
        conv_ffn=jnp.stack([dcw0, dcw1]), b_conv_ffn=jnp.stack([dcb0, dcb1]))
    flat = jnp.concatenate([v.reshape(-1) for v in small_parts.values()])
    pad = (-flat.shape[0]) % (8 * 128)
    small = jnp.pad(flat, (0, pad)).reshape(-1, 128)
    dev = 2 * chip + core
    small_slots = lax.dynamic_update_index_in_dim(jnp.zeros((N_DEV,) + small.shape, F32), small, dev, 0)
    small_all = gather_small(small_slots, name="gather_small")
    small_sum = sum_small(small_all, name="sum_small").reshape(-1)

    full3 = _reduce_finish(red3, place, small_all, tag="g3")
    full2 = _reduce_finish(red2, place, full3[0], tag="g2")
    full1 = _reduce_finish(red1, place, full2[0], tag="g1")
    full0 = _reduce_finish(red0, place, full1[0], tag="g0")
    grads = {
        "w_in_even": full0[0][None], "w_out_even": full0[1][None],
        "w_in_odd": full2[0][None], "w_out_odd": full2[1][None],
        "w_ffn_gate": jnp.stack([full1[0], full3[0]]), "w_ffn_up": jnp.stack([full1[1], full3[1]]),
        "w_ffn_down": jnp.stack([full1[2], full3[2]])}
    off = 0
    small_red = {}
    for nm, v in small_parts.items():
        small_red[nm] = small_sum[off:off + v.size].reshape(v.shape)
        off += v.size
    for nm in ("norm_mix", "norm_ffn", "pool_scale"):
        grads[nm] = small_red[nm].reshape(weights[nm].shape)
    grads["final_norm"] = small_red["final_norm"].reshape(weights["final_norm"].shape)
    grads["w_pool"] = small_red["w_pool"][None]
    grads["w_spatial"] = small_red["w_spatial"][None]
    grads["b_spatial"] = small_red["b_spatial"].reshape(weights["b_spatial"].shape)
    grads["b_conv_ffn"] = small_red["b_conv_ffn"].reshape(weights["b_conv_ffn"].shape)
    grads["conv_a"] = lax.dynamic_slice_in_dim(small_red["conv_a"], chip * conv_a.shape[-1], conv_a.shape[-1], 1)[None]
    grads["sgu_norm"] = lax.dynamic_slice_in_dim(small_red["sgu_norm"], chip * sgu_norm.shape[-1], sgu_norm.shape[-1], 1)
    grads["conv_ffn"] = lax.dynamic_index_in_dim(small_red["conv_ffn"], chip, 1, keepdims=False)

    deltas, new_m, new_v = {}, {}, {}
    for nm in order:
        w = weights[nm]
        w2 = w[None] if w.ndim == 1 else w
        shp = w2.shape
        d, nm_, nv_ = adamw(w2, grads[nm].reshape(shp), m_in[nm].reshape(shp), v_in[nm].reshape(shp),
                            name=f"adamw_{nm}")
        deltas[nm], new_m[nm], new_v[nm] = d.reshape(w.shape), nm_.reshape(w.shape), nv_.reshape(w.shape)

    return (loss, grad_x, *[grads[n] for n in order], *[deltas[n] for n in order],
            *[new_m[n] for n in order], *[new_v[n] for n in order])
```

```python
import jax
import jax.numpy as jnp
from jax import lax
from jax.experimental import pallas as pl
from jax.experimental.pallas import tpu as pltpu

F32 = jnp.float32
BF16 = jnp.bfloat16
MESH = pl.DeviceIdType.MESH
ANY = pl.BlockSpec(memory_space=pl.ANY)

EPS = 1e-6
POOL_WINDOWS = (2, 4, 8, 16)
GROUP = 128
CHUNK = 128
N_CHIPS = 4
N_DEV = 8
ROW_TILE = 512
HALO = 16
VMEM_LIMIT = 56 * 1024 * 1024

ADAM_LR = 0.001
ADAM_B1 = 0.9
ADAM_B2 = 0.999
ADAM_EPS = 1e-08
ADAM_WD = 0.01
ADAM_STEP = 10


def _params(*sem):
    return pltpu.CompilerParams(dimension_semantics=sem, vmem_limit_bytes=VMEM_LIMIT)


def _layer_spec(block, l, idx):
    if l is None:
        return pl.BlockSpec(block, idx)
    return pl.BlockSpec((None,) + block, lambda *g: (l,) + idx(*g))


def mm_nn(a, b, *, l=None, name):
    s, k = a.shape
    j_n, n = b.shape[-3], b.shape[-1]
    tm = min(ROW_TILE, s)

    def body(a_ref, b_ref, o_ref):
        o_ref[...] = jnp.dot(a_ref[...], b_ref[...], preferred_element_type=F32)

    return pl.pallas_call(
        body, name=name, grid=(j_n, s // tm),
        in_specs=[pl.BlockSpec((tm, k), lambda j, i: (i, 0)),
                  _layer_spec((None, k, n), l, lambda j, i: (j, 0, 0))],
        out_specs=pl.BlockSpec((None, tm, n), lambda j, i: (j, i, 0)),
        out_shape=jax.ShapeDtypeStruct((j_n, s, n), F32),
        compiler_params=_params("parallel", "parallel"))(a, b)


def mm_acc(a, b, res, *, l=None, name):
    j_n, s, kj = a.shape
    n = b.shape[-1]
    tm = min(ROW_TILE, s)

    def body(a_ref, b_ref, r_ref, o_ref):
        acc = r_ref[...]
        for j in range(j_n):
            acc = acc + jnp.dot(a_ref[j], b_ref[j], preferred_element_type=F32)
        o_ref[...] = acc

    return pl.pallas_call(
        body, name=name, grid=(s // tm,),
        in_specs=[pl.BlockSpec((j_n, tm, kj), lambda i: (0, i, 0)),
                  _layer_spec((j_n, kj, n), l, lambda i: (0, 0, 0)),
                  pl.BlockSpec((tm, n), lambda i: (i, 0))],
        out_specs=pl.BlockSpec((tm, n), lambda i: (i, 0)),
        out_shape=jax.ShapeDtypeStruct((s, n), F32),
        compiler_params=_params("parallel"))(a, b, res)


_NT = (((1,), (1,)), ((), ()))
_TN = (((0,), (0,)), ((), ()))


def mm_nt_sum(pairs, *, l=None, name):
    j_n, s, nj = pairs[0][0].shape
    k = pairs[0][1].shape[-2]
    tm = min(ROW_TILE, s)
    n_p = len(pairs)

    def body(*refs):
        o_ref = refs[-1]
        acc = jnp.zeros((tm, k), F32)
        for p in range(n_p):
            dy_ref, w_ref = refs[2 * p], refs[2 * p + 1]
            for j in range(j_n):
                acc = acc + lax.dot_general(dy_ref[j], w_ref[j], _NT, preferred_element_type=F32)
        o_ref[...] = acc

    in_specs, args = [], []
    for dy, w in pairs:
        in_specs += [pl.BlockSpec((j_n, tm, nj), lambda i: (0, i, 0)),
                     _layer_spec((j_n, k, nj), l, lambda i: (0, 0, 0))]
        args += [dy, w]
    return pl.pallas_call(
        body, name=name, grid=(s // tm,), in_specs=in_specs,
        out_specs=pl.BlockSpec((tm, k), lambda i: (i, 0)),
        out_shape=jax.ShapeDtypeStruct((s, k), F32),
        compiler_params=_params("parallel"))(*args)


def mm_nt_each(a, b, *, l=None, name, pin=None):
    s, k = a.shape
    j_n, nj = b.shape[-3], b.shape[-2]
    tm = min(ROW_TILE, s)

    def body(a_ref, b_ref, *rest):
        rest[-1][...] = lax.dot_general(a_ref[...], b_ref[...], _NT, preferred_element_type=F32)

    return pl.pallas_call(
        body, name=name, grid=(j_n, s // tm),
        in_specs=[pl.BlockSpec((tm, k), lambda j, i: (i, 0)),
                  _layer_spec((None, nj, k), l, lambda j, i: (j, 0, 0))] + ([ANY] if pin is not None else []),
        out_specs=pl.BlockSpec((None, tm, nj), lambda j, i: (j, i, 0)),
        out_shape=jax.ShapeDtypeStruct((j_n, s, nj), F32),
        compiler_params=_params("parallel", "parallel"))(a, b, *([pin] if pin is not None else []))


def mm_tn(a, dy, *, name, pin=None):
    ja, s, k = a.shape
    jd, _, n = dy.shape
    j_n = max(ja, jd)
    tk = min(ROW_TILE, s)
    n_steps = s // tk

    def body(a_ref, d_ref, *rest):
        o_ref, ob_ref = rest[-2:]

        @pl.when(pl.program_id(1) == 0)
        def _():
            o_ref[...] = jnp.zeros_like(o_ref)
        o_ref[...] += lax.dot_general(a_ref[...], d_ref[...], _TN, preferred_element_type=F32)

        @pl.when(pl.program_id(1) == n_steps - 1)
        def _():
            ob_ref[...] = o_ref[...].astype(BF16)

    out_spec = pl.BlockSpec((None, k, n), lambda j, i: (j, 0, 0))
    return pl.pallas_call(
        body, name=name, grid=(j_n, n_steps),
        in_specs=[pl.BlockSpec((None, tk, k), (lambda j, i: (j, i, 0)) if ja > 1 else (lambda j, i: (0, i, 0))),
                  pl.BlockSpec((None, tk, n), (lambda j, i: (j, i, 0)) if jd > 1 else (lambda j, i: (0, i, 0)))]
        + ([ANY] if pin is not None else []),
        out_specs=[out_spec, out_spec],
        out_shape=[jax.ShapeDtypeStruct((j_n, k, n), F32), jax.ShapeDtypeStruct((j_n, k, n), BF16)],
        compiler_params=_params("parallel", "arbitrary"))(a, dy, *([pin] if pin is not None else []))


def _back(x, k):
    return pltpu.roll(x, k, 0)


def _fwd(x, k):
    return pltpu.roll(x, x.shape[0] - k, 0)


def _causal_conv(x, w_ref):
    return w_ref[0:1, :] * _back(x, 2) + w_ref[1:2, :] * _back(x, 1) + w_ref[2:3, :] * x


def _causal_conv_t(dy, w_ref):
    return w_ref[2:3, :] * dy + w_ref[1:2, :] * _fwd(dy, 1) + w_ref[0:1, :] * _fwd(dy, 2)


def _gelu(x):
    return 0.5 * x * (1.0 + lax.erf(x * 0.7071067811865476))


def _gelu_grad(x):
    return 0.5 * (1.0 + lax.erf(x * 0.7071067811865476)) + x * jnp.exp(-0.5 * x * x) * 0.3989422804014327


def _colsum(x):
    return jnp.sum(x, axis=0, keepdims=True)


def rms_fwd(h, gain, *, name, pin=None):
    s, d = h.shape
    ts = min(ROW_TILE, s)

    def body(h_ref, g_ref, *rest):
        o_ref, r_ref = rest[-2:]
        x = h_ref[...]
        rstd = lax.rsqrt(jnp.mean(x * x, axis=-1, keepdims=True) + EPS)
        o_ref[...] = (x * rstd * g_ref[...]).astype(BF16)
        r_ref[...] = rstd

    return pl.pallas_call(
        body, name=name, grid=(s // ts,),
        in_specs=[pl.BlockSpec((ts, d), lambda i: (i, 0)), pl.BlockSpec((1, d), lambda i: (0, 0))]
        + ([ANY] if pin is not None else []),
        out_specs=[pl.BlockSpec((ts, d), lambda i: (i, 0)), pl.BlockSpec((ts, 1), lambda i: (i, 0))],
        out_shape=[jax.ShapeDtypeStruct((s, d), BF16), jax.ShapeDtypeStruct((s, 1), F32)],
        compiler_params=_params("parallel"))(h, gain, *([pin] if pin is not None else []))


def rms_bwd(dxn, h, gain, rstd, dres, *, name):
    s, d = h.shape
    ts = min(ROW_TILE, s)

    def body(dx_ref, h_ref, g_ref, r_ref, dr_ref, o_ref, ob_ref, dg_ref):
        @pl.when(pl.program_id(0) == 0)
        def _():
            dg_ref[...] = jnp.zeros_like(dg_ref)
        rstd_v = r_ref[...]
        hhat = h_ref[...] * rstd_v
        dx = dx_ref[...]
        dg_ref[...] += _colsum(dx * hhat)
        dxg = dx * g_ref[...]
        dh = dr_ref[...] + rstd_v * (dxg - hhat * jnp.mean(dxg * hhat, axis=-1, keepdims=True))
        o_ref[...] = dh
        ob_ref[...] = dh.astype(BF16)

    row = pl.BlockSpec((ts, d), lambda i: (i, 0))
    vec = pl.BlockSpec((1, d), lambda i: (0, 0))
    return pl.pallas_call(
        body, name=name, grid=(s // ts,),
        in_specs=[row, row, vec, pl.BlockSpec((ts, 1), lambda i: (i, 0)), row],
        out_specs=[row, row, vec],
        out_shape=[jax.ShapeDtypeStruct((s, d), F32), jax.ShapeDtypeStruct((s, d), BF16),
                   jax.ShapeDtypeStruct((1, d), F32)],
        compiler_params=_params("arbitrary"))(dxn, h, gain, rstd, dres)


def final_loss(h, target, gain, *, name):
    s, d = h.shape
    ts = min(ROW_TILE, s)

    def body(h_ref, t_ref, g_ref, o_ref, ob_ref, l_ref, dg_ref):
        @pl.when(pl.program_id(0) == 0)
        def _():
            l_ref[...] = jnp.zeros_like(l_ref)
            dg_ref[...] = jnp.zeros_like(dg_ref)
        x = h_ref[...]
        rstd = lax.rsqrt(jnp.mean(x * x, axis=-1, keepdims=True) + EPS)
        hhat = x * rstd
        err = hhat * g_ref[...] - t_ref[...]
        l_ref[...] += 0.5 * jnp.sum(jnp.mean(err * err, axis=-1, keepdims=True), axis=0, keepdims=True)
        dy = err * (1.0 / d)
        dg_ref[...] += _colsum(dy * hhat)
        dyg = dy * g_ref[...]
        dh = rstd * (dyg - hhat * jnp.mean(dyg * hhat, axis=-1, keepdims=True))
        o_ref[...] = dh
        ob_ref[...] = dh.astype(BF16)

    row = pl.BlockSpec((ts, d), lambda i: (i, 0))
    vec = pl.BlockSpec((1, d), lambda i: (0, 0))
    return pl.pallas_call(
        body, name=name, grid=(s // ts,),
        in_specs=[row, row, vec],
        out_specs=[row, row, pl.BlockSpec((1, 128), lambda i: (0, 0)), vec],
        out_shape=[jax.ShapeDtypeStruct((s, d), F32), jax.ShapeDtypeStruct((s, d), BF16),
                   jax.ShapeDtypeStruct((1, 128), F32), jax.ShapeDtypeStruct((1, d), F32)],
        compiler_params=_params("arbitrary"))(h, target, gain)


def _halo_specs(n_lead, ts, width, n_tiles):
    hb = ts // HALO
    prev = pl.BlockSpec((n_lead, HALO, width), lambda i: (0, jnp.maximum(i * hb - 1, 0), 0))
    nxt = pl.BlockSpec((n_lead, HALO, width), lambda i: (0, jnp.minimum((i + 1) * hb, n_tiles * hb - 1), 0))
    return prev, nxt


def _pool_fwd(z_ext, g, pos):
    w = POOL_WINDOWS[g]
    zg = z_ext[:, g * GROUP:(g + 1) * GROUP]
    acc = zg
    sh = 1
    while sh < w:
        acc = acc + _back(acc, sh)
        sh *= 2
    return acc[HALO:] / jnp.minimum(pos, float(w)) - zg[HALO:]


def even_fwd(proj, conv_a, w_pool, pool_scale, *, name):
    _, s, w = proj.shape
    ts = min(ROW_TILE, s)
    n_t = s // ts
    prev, _ = _halo_specs(4, ts, w, n_t)

    def body(p_ref, ph_ref, ca_ref, wp_ref, ps_ref, o_ref):
        i = pl.program_id(0)
        keep = jnp.where(i > 0, 1.0, 0.0)
        cv_ext = jnp.concatenate([ph_ref[1] * ph_ref[2] * keep, p_ref[1] * p_ref[2]], axis=0)
        o_ref[:, 0:w] = (p_ref[0] * _causal_conv(cv_ext, ca_ref)[HALO:]).astype(BF16)
        z_ext = jnp.concatenate([ph_ref[3] * keep, p_ref[3]], axis=0)
        pos = (i * ts + lax.broadcasted_iota(jnp.int32, (ts, 1), 0) + 1).astype(F32)
        for g in range(len(POOL_WINDOWS)):
            pooled = _pool_fwd(z_ext, g, pos)
            mixed = jnp.dot(pooled.astype(BF16), wp_ref[g], preferred_element_type=F32)
            cols = slice(g * GROUP, (g + 1) * GROUP)
            o_ref[:, w + g * GROUP:w + (g + 1) * GROUP] = (mixed * ps_ref[:, cols]).astype(BF16)

    return pl.pallas_call(
        body, name=name, grid=(n_t,),
        in_specs=[pl.BlockSpec((4, ts, w), lambda i: (0, i, 0)), prev,
                  pl.BlockSpec((3, w), lambda i: (0, 0)),
                  pl.BlockSpec((4, GROUP, GROUP), lambda i: (0, 0, 0)),
                  pl.BlockSpec((1, w), lambda i: (0, 0))],
        out_specs=pl.BlockSpec((ts, 2 * w), lambda i: (i, 0)),
        out_shape=jax.ShapeDtypeStruct((s, 2 * w), BF16),
        compiler_params=_params("parallel"))(proj, proj, conv_a, w_pool, pool_scale)


def even_bwd(proj, dmix, conv_a, w_pool, w_pool_t, pool_scale, *, name):
    _, s, w = proj.shape
    ts = min(ROW_TILE, s)
    n_t = s // ts
    prev, nxt = _halo_specs(4, ts, w, n_t)
    hb = ts // HALO
    dm_next = pl.BlockSpec((HALO, 2 * w), lambda i: (jnp.minimum((i + 1) * hb, n_t * hb - 1), 0))
    n_ext = ts + HALO

    def body(p_ref, pp_ref, pn_ref, dm_ref, dmn_ref, ca_ref, wp_ref, wpt_ref, ps_ref,
             dp_ref, dca_ref, dwp_ref, dps_ref):
        i = pl.program_id(0)

        @pl.when(i == 0)
        def _():
            dca_ref[...] = jnp.zeros_like(dca_ref)
            dwp_ref[...] = jnp.zeros_like(dwp_ref)
            dps_ref[...] = jnp.zeros_like(dps_ref)

        keep_p = jnp.where(i > 0, 1.0, 0.0)
        keep_n = jnp.where(i < n_t - 1, 1.0, 0.0)
        a_b, a_c, a_v = p_ref[0], p_ref[1], p_ref[2]
        cv_ext = jnp.concatenate([pp_ref[1] * pp_ref[2] * keep_p, a_c * a_v], axis=0)
        dy_a = dm_ref[:, 0:w]
        dp_ref[0] = (dy_a * _causal_conv(cv_ext, ca_ref)[HALO:]).astype(BF16)
        dcc = dy_a * a_b
        dca_ref[2:3, :] += _colsum(dcc * cv_ext[HALO:])
        dca_ref[1:2, :] += _colsum(dcc * _back(cv_ext, 1)[HALO:])
        dca_ref[0:1, :] += _colsum(dcc * _back(cv_ext, 2)[HALO:])
        dcc_ext = jnp.concatenate([dcc, dmn_ref[:, 0:w] * pn_ref[0] * keep_n], axis=0)
        dcv = _causal_conv_t(dcc_ext, ca_ref)[:ts]
        dp_ref[1] = (dcv * a_v).astype(BF16)
        dp_ref[2] = (dcv * a_c).astype(BF16)
        z_ext = jnp.concatenate([pp_ref[3] * keep_p, p_ref[3]], axis=0)
        pos = (i * ts + lax.broadcasted_iota(jnp.int32, (ts, 1), 0) + 1).astype(F32)
        pos_ext = (i * ts + lax.broadcasted_iota(jnp.int32, (n_ext, 1), 0) + 1).astype(F32)
        for g, win in enumerate(POOL_WINDOWS):
            cols = slice(g * GROUP, (g + 1) * GROUP)
            ycols = slice(w + g * GROUP, w + (g + 1) * GROUP)
            pooled = _pool_fwd(z_ext, g, pos).astype(BF16)
            mixed = jnp.dot(pooled, wp_ref[g], preferred_element_type=F32)
            dy_b = dm_ref[:, ycols]
            dps_ref[:, cols] += _colsum(dy_b * mixed)
            dmixed_ext = jnp.concatenate([dy_b, dmn_ref[:, ycols] * keep_n], axis=0) * ps_ref[:, cols]
            dmixed_ext = dmixed_ext.astype(BF16)
            dwp_ref[g] += lax.dot_general(pooled, dmixed_ext[:ts], _TN, preferred_element_type=F32)
            dpooled = jnp.dot(dmixed_ext, wpt_ref[g], preferred_element_type=F32)
            acc = dpooled / jnp.minimum(pos_ext, float(win))
            sh = 1
            while sh < win:
                acc = acc + _fwd(acc, sh)
                sh *= 2
            dp_ref[3, :, cols] = (acc[:ts] - dpooled[:ts]).astype(BF16)

    tile4 = pl.BlockSpec((4, ts, w), lambda i: (0, i, 0))
    const = lambda shape: pl.BlockSpec(shape, lambda i: (0,) * len(shape))
    return pl.pallas_call(
        body, name=name, grid=(n_t,),
        in_specs=[tile4, prev, nxt, pl.BlockSpec((ts, 2 * w), lambda i: (i, 0)), dm_next,
                  const((3, w)), const((4, GROUP, GROUP)), const((4, GROUP, GROUP)), const((1, w))],
        out_specs=[tile4, const((3, w)), const((4, GROUP, GROUP)), const((1, w))],
        out_shape=[jax.ShapeDtypeStruct((4, s, w), BF16), jax.ShapeDtypeStruct((3, w), F32),
                   jax.ShapeDtypeStruct((4, GROUP, GROUP), F32), jax.ShapeDtypeStruct((1, w), F32)],
        compiler_params=_params("arbitrary"))(proj, proj, proj, dmix, dmix, conv_a, w_pool, w_pool_t, pool_scale)


def _ffn_halo(ts, f, n_t, nxt):
    hb = ts // HALO
    if nxt:
        return pl.BlockSpec((None, HALO, f), lambda j, i: (j, jnp.minimum((i + 1) * hb, n_t * hb - 1), 0))
    return pl.BlockSpec((None, HALO, f), lambda j, i: (j, jnp.maximum(i * hb - 1, 0), 0))


def ffn_act_fwd(g, up, cw, cb, *, name):
    j_n, s, f = g.shape
    ts = min(ROW_TILE, s)
    n_t = s // ts

    def body(g_ref, gp_ref, u_ref, cw_ref, cb_ref, o_ref):
        keep = jnp.where(pl.program_id(1) > 0, 1.0, 0.0)
        g_ext = jnp.concatenate([gp_ref[...] * keep, g_ref[...]], axis=0)
        gc = _causal_conv(g_ext, cw_ref)[HALO:] + cb_ref[...]
        o_ref[...] = (gc * jax.nn.sigmoid(gc) * u_ref[...]).astype(BF16)

    tile = pl.BlockSpec((None, ts, f), lambda j, i: (j, i, 0))
    return pl.pallas_call(
        body, name=name, grid=(j_n, n_t),
        in_specs=[tile, _ffn_halo(ts, f, n_t, False), tile,
                  pl.BlockSpec((None, 3, f), lambda j, i: (j, 0, 0)),
                  pl.BlockSpec((None, 1, f), lambda j, i: (j, 0, 0))],
        out_specs=tile,
        out_shape=jax.ShapeDtypeStruct((j_n, s, f), BF16),
        compiler_params=_params("parallel", "parallel"))(g, g, up, cw, cb)


def ffn_act_bwd(g, up, dact, cw, cb, *, name):
    j_n, s, f = g.shape
    ts = min(ROW_TILE, s)
    n_t = s // ts

    def body(g_ref, gp_ref, gn_ref, u_ref, un_ref, d_ref, dn_ref, cw_ref, cb_ref,
             dg_ref, du_ref, dcw_ref, dcb_ref):
        i = pl.program_id(1)

        @pl.when(i == 0)
        def _():
            dcw_ref[...] = jnp.zeros_like(dcw_ref)
            dcb_ref[...] = jnp.zeros_like(dcb_ref)

        keep_p = jnp.where(i > 0, 1.0, 0.0)
        keep_n = jnp.where(i < n_t - 1, 1.0, 0.0)
        g_ext = jnp.concatenate([gp_ref[...] * keep_p, g_ref[...], gn_ref[...]], axis=0)
        gc = _causal_conv(g_ext, cw_ref)[HALO:] + cb_ref[...]
        sig = jax.nn.sigmoid(gc)
        dact_ext = jnp.concatenate([d_ref[...], dn_ref[...] * keep_n], axis=0)
        du_ref[...] = (dact_ext * gc * sig)[:ts].astype(BF16)
        up_ext = jnp.concatenate([u_ref[...], un_ref[...]], axis=0)
        dgc = dact_ext * up_ext * (sig * (1.0 + gc * (1.0 - sig)))
        dg_ref[...] = _causal_conv_t(dgc, cw_ref)[:ts].astype(BF16)
        dgc_t = dgc[:ts]
        dcb_ref[...] += _colsum(dgc_t)
        dcw_ref[2:3, :] += _colsum(dgc_t * g_ext[HALO:HALO + ts])
        dcw_ref[1:2, :] += _colsum(dgc_t * _back(g_ext, 1)[HALO:HALO + ts])
        dcw_ref[0:1, :] += _colsum(dgc_t * _back(g_ext, 2)[HALO:HALO + ts])

    tile = pl.BlockSpec((None, ts, f), lambda j, i: (j, i, 0))
    prev, nxt = _ffn_halo(ts, f, n_t, False), _ffn_halo(ts, f, n_t, True)
    return pl.pallas_call(
        body, name=name, grid=(j_n, n_t),
        in_specs=[tile, prev, nxt, tile, nxt, tile, nxt,
                  pl.BlockSpec((None, 3, f), lambda j, i: (j, 0, 0)),
                  pl.BlockSpec((None, 1, f), lambda j, i: (j, 0, 0))],
        out_specs=[tile, tile, pl.BlockSpec((None, 3, f), lambda j, i: (j, 0, 0)),
                   pl.BlockSpec((None, 1, f), lambda j, i: (j, 0, 0))],
        out_shape=[jax.ShapeDtypeStruct((j_n, s, f), BF16), jax.ShapeDtypeStruct((j_n, s, f), BF16),
                   jax.ShapeDtypeStruct((j_n, 3, f), F32), jax.ShapeDtypeStruct((j_n, 1, f), F32)],
        compiler_params=_params("parallel", "arbitrary"))(g, g, g, up, up, dact, dact, cw, cb)


def _sgu_gate(vn_bf, ws_ref, bs_ref, h, rows):
    tri = lax.broadcasted_iota(jnp.int32, (CHUNK, CHUNK), 0) >= lax.broadcasted_iota(jnp.int32, (CHUNK, CHUNK), 1)
    ws = jnp.where(tri, ws_ref[h], 0.0).astype(BF16)
    cols = slice((h % 4) * GROUP, (h % 4 + 1) * GROUP)
    return ws, jnp.dot(ws, vn_bf[h // 4][rows, cols], preferred_element_type=F32) + bs_ref[h]


def sgu_fwd(p, sgu_norm, w_spatial, b_spatial, *, name):
    _, s, w = p.shape
    ts = min(ROW_TILE, s)
    n_heads = w_spatial.shape[0]

    def body(p_ref, n_ref, ws_ref, bs_ref, o_ref, r_ref):
        v = [_gelu(p_ref[2]), _gelu(p_ref[3])]
        ms = (jnp.sum(v[0] * v[0], axis=-1, keepdims=True) + jnp.sum(v[1] * v[1], axis=-1, keepdims=True)) / (2 * w)
        rstd = lax.rsqrt(ms + EPS)
        r_ref[...] = rstd
        vn = [(v[k] * rstd * n_ref[:, k * w:(k + 1) * w]).astype(BF16) for k in range(2)]
        for h in range(n_heads):
            cols = slice((h % 4) * GROUP, (h % 4 + 1) * GROUP)
            for c in range(ts // CHUNK):
                rows = slice(c * CHUNK, (c + 1) * CHUNK)
                _, gate = _sgu_gate(vn, ws_ref, bs_ref, h, rows)
                u = _gelu(p_ref[h // 4, rows, cols])
                o_ref[rows, h * GROUP:(h + 1) * GROUP] = (u * gate).astype(BF16)

    const = lambda shape: pl.BlockSpec(shape, lambda i: (0,) * len(shape))
    return pl.pallas_call(
        body, name=name, grid=(s // ts,),
        in_specs=[pl.BlockSpec((4, ts, w), lambda i: (0, i, 0)), const((1, 2 * w)),
                  const((n_heads, CHUNK, CHUNK)), const((n_heads, CHUNK, 1))],
        out_specs=[pl.BlockSpec((ts, 2 * w), lambda i: (i, 0)), pl.BlockSpec((ts, 1), lambda i: (i, 0))],
        out_shape=[jax.ShapeDtypeStruct((s, 2 * w), BF16), jax.ShapeDtypeStruct((s, 1), F32)],
        compiler_params=_params("parallel"))(p, sgu_norm, w_spatial, b_spatial)


def sgu_bwd(p, dmix, rstd, sgu_norm, w_spatial, b_spatial, *, name):
    _, s, w = p.shape
    ts = min(ROW_TILE, s)
    n_heads = w_spatial.shape[0]

    def body(p_ref, dm_ref, r_ref, n_ref, ws_ref, bs_ref, dp_ref, dn_ref, dws_ref, dbs_ref, dvn_ref):
        @pl.when(pl.program_id(0) == 0)
        def _():
            dn_ref[...] = jnp.zeros_like(dn_ref)
            dws_ref[...] = jnp.zeros_like(dws_ref)
            dbs_ref[...] = jnp.zeros_like(dbs_ref)

        rstd_v = r_ref[...]
        vhat = [_gelu(p_ref[2 + k]) * rstd_v for k in range(2)]
        vn = [(vhat[k] * n_ref[:, k * w:(k + 1) * w]).astype(BF16) for k in range(2)]
        tri = lax.broadcasted_iota(jnp.int32, (CHUNK, CHUNK), 0) >= lax.broadcasted_iota(jnp.int32, (CHUNK, CHUNK), 1)
        for h in range(n_heads):
            cols = slice((h % 4) * GROUP, (h % 4 + 1) * GROUP)
            ocols = slice(h * GROUP, (h + 1) * GROUP)
            for c in range(ts // CHUNK):
                rows = slice(c * CHUNK, (c + 1) * CHUNK)
                ws, gate = _sgu_gate(vn, ws_ref, bs_ref, h, rows)
                pu = p_ref[h // 4, rows, cols]
                dm = dm_ref[rows, ocols]
                dp_ref[h // 4, rows, cols] = (dm * gate * _gelu_grad(pu)).astype(BF16)
                dgate = dm * _gelu(pu)
                dbs_ref[h] += jnp.sum(dgate, axis=-1, keepdims=True)
                dgate_bf = dgate.astype(BF16)
                dws = lax.dot_general(dgate_bf, vn[h // 4][rows, cols], _NT, preferred_element_type=F32)
                dws_ref[h] += jnp.where(tri, dws, 0.0)
                dvn_ref[rows, ocols] = lax.dot_general(ws, dgate_bf, _TN, preferred_element_type=F32)
        for k in range(2):
            kc = slice(k * w, (k + 1) * w)
            dvn = dvn_ref[:, kc]
            dn_ref[:, kc] += _colsum(dvn * vhat[k])
        dvh = [dvn_ref[:, k * w:(k + 1) * w] * n_ref[:, k * w:(k + 1) * w] for k in range(2)]
        dot = (jnp.sum(dvh[0] * vhat[0], axis=-1, keepdims=True)
               + jnp.sum(dvh[1] * vhat[1], axis=-1, keepdims=True)) / (2 * w)
        for k in range(2):
            dv = rstd_v * (dvh[k] - vhat[k] * dot)
            dp_ref[2 + k] = (dv * _gelu_grad(p_ref[2 + k])).astype(BF16)

    const = lambda shape: pl.BlockSpec(shape, lambda i: (0,) * len(shape))
    tile4 = pl.BlockSpec((4, ts, w), lambda i: (0, i, 0))
    return pl.pallas_call(
        body, name=name, grid=(s // ts,),
        in_specs=[tile4, pl.BlockSpec((ts, 2 * w), lambda i: (i, 0)), pl.BlockSpec((ts, 1), lambda i: (i, 0)),
                  const((1, 2 * w)), const((n_heads, CHUNK, CHUNK)), const((n_heads, CHUNK, 1))],
        out_specs=[tile4, const((1, 2 * w)), const((n_heads, CHUNK, CHUNK)), const((n_heads, CHUNK, 1))],
        out_shape=[jax.ShapeDtypeStruct((4, s, w), BF16), jax.ShapeDtypeStruct((1, 2 * w), F32),
                   jax.ShapeDtypeStruct((n_heads, CHUNK, CHUNK), F32),
                   jax.ShapeDtypeStruct((n_heads, CHUNK, 1), F32)],
        scratch_shapes=[pltpu.VMEM((ts, 2 * w), F32)],
        compiler_params=_params("arbitrary"))(p, dmix, rstd, sgu_norm, w_spatial, b_spatial)


def _row_tile(rows):
    if rows <= ROW_TILE:
        return rows
    for t in (512, 384, 352, 256, 128, 64, 32, 16, 8):
        if rows % t == 0:
            return t
    return rows


def adamw(w, g, m, v, *, name):
    shape = w.shape
    cols = shape[-1]
    rows = w.size // cols
    w2, g2, m2, v2 = (a.reshape(rows, cols) for a in (w, g, m, v))
    tr = _row_tile(rows)
    bc1 = 1.0 - ADAM_B1 ** ADAM_STEP
    bc2 = 1.0 - ADAM_B2 ** ADAM_STEP

    def body(w_ref, g_ref, m_ref, v_ref, d_ref, nm_ref, nv_ref):
        grad = g_ref[...]
        m_new = ADAM_B1 * m_ref[...] + (1.0 - ADAM_B1) * grad
        v_new = ADAM_B2 * v_ref[...] + (1.0 - ADAM_B2) * (grad * grad)
        nm_ref[...] = m_new
        nv_ref[...] = v_new
        d_ref[...] = -ADAM_LR * ((m_new / bc1) / (jnp.sqrt(v_new / bc2) + ADAM_EPS) + ADAM_WD * w_ref[...])

    spec = pl.BlockSpec((tr, cols), lambda i: (i, 0))
    outs = pl.pallas_call(
        body, name=name, grid=(rows // tr,),
        in_specs=[spec] * 4, out_specs=[spec] * 3,
        out_shape=[jax.ShapeDtypeStruct((rows, cols), F32)] * 3,
        compiler_params=_params("parallel"))(w2, g2, m2, v2)
    return tuple(o.reshape(shape) for o in outs)


def _place():
    return lax.axis_index("x"), lax.axis_index("y"), lax.axis_index("c")


def _other_chips(x, y):
    return [(1 - x, y), (x, 1 - y), (1 - x, 1 - y)]


HBM = pl.BlockSpec(memory_space=pltpu.HBM)
SEM = pl.BlockSpec(memory_space=pltpu.SEMAPHORE)
DATAFLOW = pltpu.SideEffectType.DATAFLOW_SIDE_EFFECTING


def _in_hbm(a):
    return pltpu.with_memory_space_constraint(a, pltpu.HBM)


def cast_into_slot(w, chip, *, l=None, name):
    rows, cols = w.shape[-2:]
    tr = _row_tile(rows)

    def body(chip_ref, w_ref, o_ref):
        o_ref[...] = w_ref[...].astype(BF16)

    in_spec = (pl.BlockSpec((tr, cols), lambda i, chip_ref: (i, 0)) if l is None
               else pl.BlockSpec((None, tr, cols), lambda i, chip_ref: (l, i, 0)))
    return pl.pallas_call(
        body, name=name,
        grid_spec=pltpu.PrefetchScalarGridSpec(
            num_scalar_prefetch=1, grid=(rows // tr,), in_specs=[in_spec],
            out_specs=pl.BlockSpec((None, tr, cols), lambda i, chip_ref: (chip_ref[0], i, 0))),
        out_shape=jax.ShapeDtypeStruct((N_CHIPS, rows, cols), BF16),
        compiler_params=_params("parallel"))(chip, w)


def _half(ref, slot, c):
    half = ref.shape[1] // 2
    return ref.at[slot, pl.ds(c * half, half), :]


def gather_start(groups, smalls):
    flat = [b for g in groups for b in g]
    n_b, n_s, n_g = len(flat), len(smalls), len(groups)

    def body(*refs):
        bufs, small_refs = refs[:n_b], refs[n_b:n_b + n_s]
        sems = refs[n_b + n_s:n_b + n_s + 2 * n_g + 2]
        token = refs[-1]
        x, y, c = _place()
        me = 2 * x + y
        chips = _other_chips(x, y)
        for si in range(n_s):
            piece = small_refs[si].at[me]
            for k, (px, py) in enumerate(chips):
                pltpu.make_async_remote_copy(
                    src_ref=piece, dst_ref=piece,
                    send_sem=sems[2 * n_g].at[3 * si + k], recv_sem=sems[2 * n_g + 1].at[3 * si + k],
                    device_id=(px, py, c), device_id_type=MESH).start()
        t = 0
        for gi, group in enumerate(groups):
            for ti in range(len(group)):
                piece = _half(bufs[t], me, c)
                t += 1
                for k, (px, py) in enumerate(chips):
                    pltpu.make_async_remote_copy(
                        src_ref=piece, dst_ref=piece,
                        send_sem=sems[2 * gi].at[3 * ti + k], recv_sem=sems[2 * gi + 1].at[3 * ti + k],
                        device_id=(px, py, c), device_id_type=MESH).start()
        token[...] = jnp.zeros_like(token)

    sem_shapes = []
    for group in groups:
        sem_shapes += [pltpu.SemaphoreType.DMA((3 * len(group),))] * 2
    sem_shapes += [pltpu.SemaphoreType.DMA((3 * n_s,))] * 2
    arrays = flat + list(smalls)
    n_sem = len(sem_shapes)
    res = pl.pallas_call(
        body, name="gather_start",
        out_shape=tuple(sem_shapes) + tuple(pltpu.HBM(a.shape, a.dtype) for a in arrays)
        + (jax.ShapeDtypeStruct((8, 128), F32),),
        in_specs=[HBM] * len(arrays),
        out_specs=tuple([SEM] * n_sem + [HBM] * len(arrays) + [pl.BlockSpec(memory_space=pltpu.VMEM)]),
        input_output_aliases={i: n_sem + i for i in range(len(arrays))},
        compiler_params=pltpu.CompilerParams(has_side_effects=DATAFLOW))(*[_in_hbm(a) for a in arrays])
    sems, thru, token = res[:n_sem], res[n_sem:-1], res[-1]
    out_groups, t = [], 0
    for group in groups:
        out_groups.append(list(thru[t:t + len(group)]))
        t += len(group)
    return sems, out_groups, list(thru[n_b:]), token


def gather_wait(bufs, send, recv, after, *, name, smalls=(), small_send=None, small_recv=None):
    n_b, n_s = len(bufs), len(smalls)
    arrays = list(bufs) + list(smalls)
    sem_ops = [send, recv] + ([small_send, small_recv] if n_s else [])

    def body(*refs):
        buf_refs, small_refs = refs[:n_b], refs[n_b:n_b + n_s]
        sems = refs[n_b + n_s:n_b + n_s + len(sem_ops)]
        x, y, c = _place()
        me = 2 * x + y
        chips = _other_chips(x, y)
        for ti in range(n_b):
            for k, (px, py) in enumerate(chips):
                done = pltpu.make_async_remote_copy(
                    src_ref=_half(buf_refs[ti], me, c), dst_ref=_half(buf_refs[ti], 2 * px + py, c),
                    send_sem=sems[0].at[3 * ti + k], recv_sem=sems[1].at[3 * ti + k],
                    device_id=(px, py, c), device_id_type=MESH)
                done.wait_send()
                done.wait_recv()
        for si in range(n_s):
            for k, (px, py) in enumerate(chips):
                done = pltpu.make_async_remote_copy(
                    src_ref=small_refs[si].at[me], dst_ref=small_refs[si].at[2 * px + py],
                    send_sem=sems[2].at[3 * si + k], recv_sem=sems[3].at[3 * si + k],
                    device_id=(px, py, c), device_id_type=MESH)
                done.wait_send()
                done.wait_recv()

    res = pl.pallas_call(
        body, name=name,
        out_shape=tuple(pltpu.HBM(a.shape, a.dtype) for a in arrays),
        in_specs=[HBM] * len(arrays) + [SEM] * len(sem_ops) + [ANY],
        out_specs=tuple([HBM] * len(arrays)),
        input_output_aliases={i: i for i in range(len(arrays))},
        compiler_params=pltpu.CompilerParams(has_side_effects=DATAFLOW))(*arrays, *sem_ops, after)
    return list(res[:n_b]), list(res[n_b:])


def gather_forward(bufs, *, name):
    n = len(bufs)

    def body(*refs):
        ins, outs = refs[:n], refs[n:2 * n]
        send_sems, recv_sems = refs[2 * n:]
        x, y, c = _place()
        chips = _other_chips(x, y)
        for t in range(n):
            for k, (px, py) in enumerate(chips):
                pltpu.make_async_remote_copy(
                    src_ref=_half(ins[t], 2 * px + py, c), dst_ref=_half(outs[t], 2 * px + py, c),
                    send_sem=send_sems.at[3 * t + k], recv_sem=recv_sems.at[3 * t + k],
                    device_id=(x, y, 1 - c), device_id_type=MESH).start()
        for t in range(n):
            for k, (px, py) in enumerate(chips):
                done = pltpu.make_async_remote_copy(
                    src_ref=_half(ins[t], 2 * px + py, c), dst_ref=_half(outs[t], 2 * px + py, 1 - c),
                    send_sem=send_sems.at[3 * t + k], recv_sem=recv_sems.at[3 * t + k],
                    device_id=(x, y, 1 - c), device_id_type=MESH)
                done.wait_send()
                done.wait_recv()

    return pl.pallas_call(
        body, name=name, in_specs=[ANY] * n, out_specs=[ANY] * n,
        out_shape=[jax.ShapeDtypeStruct(a.shape, a.dtype) for a in bufs],
        input_output_aliases={i: i for i in range(n)},
        scratch_shapes=[pltpu.SemaphoreType.DMA((3 * n,)), pltpu.SemaphoreType.DMA((3 * n,))],
        compiler_params=pltpu.CompilerParams(has_side_effects=True))(*bufs)


def reduce_stage_a(grads_bf, *, name):
    n = len(grads_bf)

    def body(*refs):
        srcs, outs = refs[:n], refs[n:2 * n]
        send_sems, recv_sems = refs[2 * n:]
        x, y, c = _place()
        for t in range(n):
            half = srcs[t].shape[1] // 2
            pltpu.make_async_remote_copy(
                src_ref=srcs[t].at[:, pl.ds((1 - c) * half, half), :], dst_ref=outs[t],
                send_sem=send_sems.at[t], recv_sem=recv_sems.at[t],
                device_id=(x, y, 1 - c), device_id_type=MESH).start()
        for t in range(n):
            half = srcs[t].shape[1] // 2
            done = pltpu.make_async_remote_copy(
                src_ref=srcs[t].at[:, pl.ds((1 - c) * half, half), :], dst_ref=outs[t],
                send_sem=send_sems.at[t], recv_sem=recv_sems.at[t],
                device_id=(x, y, 1 - c), device_id_type=MESH)
            done.wait_send()
            done.wait_recv()

    out_shape = [jax.ShapeDtypeStruct((a.shape[0], a.shape[1] // 2, a.shape[2]), a.dtype) for a in grads_bf]
    return pl.pallas_call(
        body, name=name, in_specs=[ANY] * n, out_specs=[ANY] * n, out_shape=out_shape,
        scratch_shapes=[pltpu.SemaphoreType.DMA((n,)), pltpu.SemaphoreType.DMA((n,))],
        compiler_params=pltpu.CompilerParams(has_side_effects=True))(*grads_bf)


def sum_stage_a(grad, recv, place, *, name):
    j_n, half, cols = recv.shape

    def body(place_ref, g_ref, r_ref, o_ref, ob_ref):
        acc = g_ref[...] + r_ref[...].astype(F32)
        o_ref[...] = acc
        ob_ref[...] = acc.astype(ob_ref.dtype)

    blk = (None, half, cols)
    return pl.pallas_call(
        body, name=name,
        grid_spec=pltpu.PrefetchScalarGridSpec(
            num_scalar_prefetch=1, grid=(j_n,),
            in_specs=[pl.BlockSpec(blk, lambda j, place_ref: (j, place_ref[1], 0)),
                      pl.BlockSpec(blk, lambda j, place_ref: (j, 0, 0))],
            out_specs=[pl.BlockSpec(blk, lambda j, place_ref: (j, 0, 0))] * 2),
        out_shape=[jax.ShapeDtypeStruct(recv.shape, F32), jax.ShapeDtypeStruct(recv.shape, recv.dtype)],
        compiler_params=_params("parallel"))(place, grad, recv)


def reduce_b_start(parts_bf, *, name):
    n = len(parts_bf)
    lands = [lax.empty((3,) + a.shape[1:], a.dtype) for a in parts_bf]

    def body(*refs):
        srcs, land = refs[:n], refs[n:2 * n]
        send_sems, recv_sems = refs[2 * n], refs[2 * n + 1]
        token = refs[-1]
        x, y, c = _place()
        chips = _other_chips(x, y)
        for t in range(n):
            for k, (px, py) in enumerate(chips):
                pltpu.make_async_remote_copy(
                    src_ref=srcs[t].at[2 * px + py], dst_ref=land[t].at[k],
                    send_sem=send_sems.at[3 * t + k], recv_sem=recv_sems.at[3 * t + k],
                    device_id=(px, py, c), device_id_type=MESH).start()
        token[...] = jnp.zeros_like(token)

    arrays = list(parts_bf) + lands
    res = pl.pallas_call(
        body, name=name,
        out_shape=(pltpu.SemaphoreType.DMA((3 * n,)), pltpu.SemaphoreType.DMA((3 * n,)))
        + tuple(pltpu.HBM(a.shape, a.dtype) for a in arrays) + (jax.ShapeDtypeStruct((8, 128), F32),),
        in_specs=[HBM] * (2 * n),
        out_specs=tuple([SEM, SEM] + [HBM] * (2 * n) + [pl.BlockSpec(memory_space=pltpu.VMEM)]),
        input_output_aliases={i: 2 + i for i in range(2 * n)},
        compiler_params=pltpu.CompilerParams(has_side_effects=DATAFLOW))(*[_in_hbm(a) for a in arrays])
    return res[0], res[1], list(res[2:2 + n]), list(res[2 + n:2 + 2 * n]), res[-1]


def reduce_b_wait(srcs, lands, send, recv, after, *, name):
    n = len(srcs)

    def body(*refs):
        src_refs, land = refs[:n], refs[n:2 * n]
        send_sems, recv_sems = refs[2 * n], refs[2 * n + 1]
        x, y, c = _place()
        chips = _other_chips(x, y)
        for t in range(n):
            for k, (px, py) in enumerate(chips):
                done = pltpu.make_async_remote_copy(
                    src_ref=src_refs[t].at[2 * px + py], dst_ref=land[t].at[k],
                    send_sem=send_sems.at[3 * t + k], recv_sem=recv_sems.at[3 * t + k],
                    device_id=(px, py, c), device_id_type=MESH)
                done.wait_send()
                done.wait_recv()

    arrays = list(srcs) + list(lands)
    res = pl.pallas_call(
        body, name=name,
        out_shape=tuple(pltpu.HBM(a.shape, a.dtype) for a in arrays),
        in_specs=[HBM] * (2 * n) + [SEM, SEM, ANY],
        out_specs=tuple([HBM] * (2 * n)),
        input_output_aliases={i: i for i in range(2 * n)},
        compiler_params=pltpu.CompilerParams(has_side_effects=DATAFLOW))(*arrays, send, recv, after)
    return list(res[n:])


def sum_stage_b(part, recv, place, *, name):
    _, half, cols = part.shape

    def body(place_ref, p_ref, r_ref, o_ref):
        acc = p_ref[...]
        for k in range(3):
            acc = acc + r_ref[k].astype(F32)
        o_ref[...] = acc

    return pl.pallas_call(
        body, name=name,
        grid_spec=pltpu.PrefetchScalarGridSpec(
            num_scalar_prefetch=1, grid=(1,),
            in_specs=[pl.BlockSpec((None, half, cols), lambda i, place_ref: (place_ref[0], 0, 0)),
                      pl.BlockSpec((3, half, cols), lambda i, place_ref: (0, 0, 0))],
            out_specs=pl.BlockSpec((half, cols), lambda i, place_ref: (place_ref[1], 0))),
        out_shape=jax.ShapeDtypeStruct((2 * half, cols), F32),
        compiler_params=_params("arbitrary"))(place, part, recv)


def reduce_stage_c(fulls, *, name):
    n = len(fulls)

    def body(*refs):
        ins, outs = refs[:n], refs[n:2 * n]
        send_sems, recv_sems = refs[2 * n:]
        x, y, c = _place()
        for t in range(n):
            half = ins[t].shape[0] // 2
            pltpu.make_async_remote_copy(
                src_ref=ins[t].at[pl.ds(c * half, half), :], dst_ref=outs[t].at[pl.ds(c * half, half), :],
                send_sem=send_sems.at[t], recv_sem=recv_sems.at[t],
                device_id=(x, y, 1 - c), device_id_type=MESH).start()
        for t in range(n):
            half = ins[t].shape[0] // 2
            done = pltpu.make_async_remote_copy(
                src_ref=ins[t].at[pl.ds(c * half, half), :], dst_ref=outs[t].at[pl.ds((1 - c) * half, half), :],
                send_sem=send_sems.at[t], recv_sem=recv_sems.at[t],
                device_id=(x, y, 1 - c), device_id_type=MESH)
            done.wait_send()
            done.wait_recv()

    return pl.pallas_call(
        body, name=name, in_specs=[ANY] * n, out_specs=[ANY] * n,
        out_shape=[jax.ShapeDtypeStruct(a.shape, a.dtype) for a in fulls],
        input_output_aliases={i: i for i in range(n)},
        scratch_shapes=[pltpu.SemaphoreType.DMA((n,)), pltpu.SemaphoreType.DMA((n,))],
        compiler_params=pltpu.CompilerParams(has_side_effects=True))(*fulls)


def gather_chip_blocks(slots, *, name):
    def body(in_ref, out_ref, send_sems, recv_sems):
        x, y, c = _place()
        me = 2 * x + y
        chips = _other_chips(x, y)
        for k, (px, py) in enumerate(chips):
            pltpu.make_async_remote_copy(
                src_ref=in_ref.at[me], dst_ref=out_ref.at[me],
                send_sem=send_sems.at[k], recv_sem=recv_sems.at[k],
                device_id=(px, py, c), device_id_type=MESH).start()
        for k, (px, py) in enumerate(chips):
            done = pltpu.make_async_remote_copy(
                src_ref=in_ref.at[me], dst_ref=out_ref.at[2 * px + py],
                send_sem=send_sems.at[k], recv_sem=recv_sems.at[k],
                device_id=(px, py, c), device_id_type=MESH)
            done.wait_send()
            done.wait_recv()

    return pl.pallas_call(
        body, name=name, in_specs=[ANY], out_specs=ANY,
        out_shape=jax.ShapeDtypeStruct(slots.shape, slots.dtype),
        input_output_aliases={0: 0},
        scratch_shapes=[pltpu.SemaphoreType.DMA((3,)), pltpu.SemaphoreType.DMA((3,))],
        compiler_params=pltpu.CompilerParams(has_side_effects=True))(slots)


def _ffn_fwd(h, hn, rstd, wg, wu, wd, cw, cb, l):
    g = mm_nn(hn, wg, name=f"ffn{l}_gate")
    up = mm_nn(hn, wu, name=f"ffn{l}_up")
    act = ffn_act_fwd(g, up, cw, cb, name=f"ffn{l}_act")
    out = mm_acc(act, wd, h, name=f"ffn{l}_down")
    return out, (hn, rstd, g, up, act)


def _ffn_bwd(dh, dh_bf, h, gain, saved, wg, wu, wd, cw, cb, l, pin):
    hn, rstd, g, up, act = saved
    dwd = mm_tn(act, dh_bf[None], name=f"ffn{l}_dwd")
    dact = mm_nt_each(dh_bf, wd, name=f"ffn{l}_dact", pin=pin)
    dg, dup, dcw, dcb = ffn_act_bwd(g, up, dact, cw, cb, name=f"ffn{l}_act_bwd")
    dwg = mm_tn(hn[None], dg, name=f"ffn{l}_dwg")
    dwu = mm_tn(hn[None], dup, name=f"ffn{l}_dwu")
    dhn = mm_nt_sum([(dg, wg), (dup, wu)], name=f"ffn{l}_dhn")
    dh_in, dh_in_bf, dgain = rms_bwd(dhn, h, gain, rstd, dh, name=f"ffn{l}_rms_bwd")
    return dh_in, dh_in_bf, (dwg, dwu, dwd), (dcw, dcb, dgain)


def _reduce_to_chip(grads, place, *, tag):
    recv_a = reduce_stage_a([g[1] for g in grads], name=f"reduce_a_{tag}")
    parts = [sum_stage_a(g[0], r, place, name=f"sum_a_{tag}{i}") for i, (g, r) in enumerate(zip(grads, recv_a))]
    send, recv, srcs, lands, token = reduce_b_start([p[1] for p in parts], name=f"reduce_b_start_{tag}")
    return [p[0] for p in parts], send, recv, srcs, lands, token


def _reduce_finish(state, place, after, *, tag):
    parts, send, recv, srcs, lands, _ = state
    recv_b = reduce_b_wait(srcs, lands, send, recv, after, name=f"reduce_b_wait_{tag}")
    halves = [sum_stage_b(p, r, place, name=f"sum_b_{tag}{i}") for i, (p, r) in enumerate(zip(parts, recv_b))]
    return reduce_stage_c(halves, name=f"reduce_c_{tag}")


def kernel(x, norm_mix, norm_ffn, final_norm, w_in_even, conv_a, w_pool, pool_scale, w_out_even, w_in_odd, sgu_norm, w_spatial, b_spatial, w_out_odd, w_ffn_gate, w_ffn_up, conv_ffn, b_conv_ffn, w_ffn_down, loss_target, m_norm_mix, m_norm_ffn, m_final_norm, m_w_in_even, m_conv_a, m_w_pool, m_pool_scale, m_w_out_even, m_w_in_odd, m_sgu_norm, m_w_spatial, m_b_spatial, m_w_out_odd, m_w_ffn_gate, m_w_ffn_up, m_conv_ffn, m_b_conv_ffn, m_w_ffn_down, v_norm_mix, v_norm_ffn, v_final_norm, v_w_in_even, v_conv_a, v_w_pool, v_pool_scale, v_w_out_even, v_w_in_odd, v_sgu_norm, v_w_spatial, v_b_spatial, v_w_out_odd, v_w_ffn_gate, v_w_ffn_up, v_conv_ffn, v_b_conv_ffn, v_w_ffn_down):
    weights = dict(norm_mix=norm_mix, norm_ffn=norm_ffn, final_norm=final_norm, w_in_even=w_in_even,
                   conv_a=conv_a, w_pool=w_pool, pool_scale=pool_scale, w_out_even=w_out_even,
                   w_in_odd=w_in_odd, sgu_norm=sgu_norm, w_spatial=w_spatial, b_spatial=b_spatial,
                   w_out_odd=w_out_odd, w_ffn_gate=w_ffn_gate, w_ffn_up=w_ffn_up, conv_ffn=conv_ffn,
                   b_conv_ffn=b_conv_ffn, w_ffn_down=w_ffn_down)
    m_in = dict(norm_mix=m_norm_mix, norm_ffn=m_norm_ffn, final_norm=m_final_norm, w_in_even=m_w_in_even,
                conv_a=m_conv_a, w_pool=m_w_pool, pool_scale=m_pool_scale, w_out_even=m_w_out_even,
                w_in_odd=m_w_in_odd, sgu_norm=m_sgu_norm, w_spatial=m_w_spatial, b_spatial=m_b_spatial,
                w_out_odd=m_w_out_odd, w_ffn_gate=m_w_ffn_gate, w_ffn_up=m_w_ffn_up, conv_ffn=m_conv_ffn,
                b_conv_ffn=m_b_conv_ffn, w_ffn_down=m_w_ffn_down)
    v_in = dict(norm_mix=v_norm_mix, norm_ffn=v_norm_ffn, final_norm=v_final_norm, w_in_even=v_w_in_even,
                conv_a=v_conv_a, w_pool=v_w_pool, pool_scale=v_pool_scale, w_out_even=v_w_out_even,
                w_in_odd=v_w_in_odd, sgu_norm=v_sgu_norm, w_spatial=v_w_spatial, b_spatial=v_b_spatial,
                w_out_odd=v_w_out_odd, w_ffn_gate=v_w_ffn_gate, w_ffn_up=v_w_ffn_up, conv_ffn=v_conv_ffn,
                b_conv_ffn=v_b_conv_ffn, w_ffn_down=v_w_ffn_down)
    order = list(weights)

    chip = 2 * lax.axis_index("x") + lax.axis_index("y")
    core = lax.axis_index("c")
    place = jnp.stack([chip, core]).astype(jnp.int32)
    chip_arr = place[:1]

    h0 = x[0]
    target = loss_target[0]
    d_model = h0.shape[1]
    f_shard = w_ffn_gate.shape[-1]

    def own_slot(v):
        return lax.dynamic_update_index_in_dim(jnp.zeros((N_CHIPS,) + v.shape, v.dtype), v, chip, 0)

    groups = [
        [cast_into_slot(w_in_even[0], chip_arr, name="cast_win_e"),
         cast_into_slot(w_out_even[0], chip_arr, name="cast_wout_e")],
        [cast_into_slot(w_ffn_gate, chip_arr, l=0, name="cast_wg0"),
         cast_into_slot(w_ffn_up, chip_arr, l=0, name="cast_wu0"),
         cast_into_slot(w_ffn_down, chip_arr, l=0, name="cast_wd0")],
        [cast_into_slot(w_in_odd[0], chip_arr, name="cast_win_o"),
         cast_into_slot(w_out_odd[0], chip_arr, name="cast_wout_o")],
        [cast_into_slot(w_ffn_gate, chip_arr, l=1, name="cast_wg1"),
         cast_into_slot(w_ffn_up, chip_arr, l=1, name="cast_wu1"),
         cast_into_slot(w_ffn_down, chip_arr, l=1, name="cast_wd1")]]
    smalls = [own_slot(conv_a[0]), own_slot(sgu_norm), own_slot(conv_ffn[0]), own_slot(conv_ffn[1])]
    sems, groups, smalls, token = gather_start(groups, smalls)

    def arrive(gi, after, with_smalls=False):
        kw = dict(smalls=smalls, small_send=sems[-2], small_recv=sems[-1]) if with_smalls else {}
        bufs, small_out = gather_wait(groups[gi], sems[2 * gi], sems[2 * gi + 1], after, name=f"gather_wait{gi}", **kw)
        return gather_forward(bufs, name=f"gather_forward{gi}"), small_out

    cb = b_conv_ffn.reshape(-1, N_CHIPS, 1, f_shard)
    wp_bf = w_pool[0].astype(BF16)
    wp_t_bf = jnp.transpose(w_pool[0], (0, 2, 1)).astype(BF16)
    ws = w_spatial[0]
    bs = b_spatial[0][:, :, None]

    xn0, rstd0 = rms_fwd(h0, norm_mix[0:1], name="l0_rms", pin=token)
    (win_e, wout_e), (ca_g, sn_g, cw0, cw1) = arrive(0, xn0, with_smalls=True)
    wout_e = wout_e.reshape(1, -1, d_model)
    ca_full = jnp.transpose(ca_g, (1, 0, 2)).reshape(ca_g.shape[1], -1)
    sn_full = sn_g.reshape(1, -1)
    proj0 = mm_nn(xn0, win_e, name="l0_in")
    mix0 = even_fwd(proj0, ca_full, wp_bf, pool_scale, name="l0_mix")
    h1 = mm_acc(mix0[None], wout_e, h0, name="l0_out")
    hn0, rstdf0 = rms_fwd(h1, norm_ffn[0:1], name="ffn0_rms")
    (wg0, wu0, wd0), _ = arrive(1, hn0)
    h2, ffn0 = _ffn_fwd(h1, hn0, rstdf0, wg0, wu0, wd0, cw0, cb[0], 0)
    xn1, rstd1 = rms_fwd(h2, norm_mix[1:2], name="l1_rms")
    (win_o, wout_o), _ = arrive(2, xn1)
    wout_o = wout_o.reshape(1, -1, d_model)
    p1 = mm_nn(xn1, win_o, name="l1_in")
    mix1, rstd_v = sgu_fwd(p1, sn_full, ws, bs, name="l1_mix")
    h3 = mm_acc(mix1[None], wout_o, h2, name="l1_out")
    hn1, rstdf1 = rms_fwd(h3, norm_ffn[1:2], name="ffn1_rms")
    (wg1, wu1, wd1), _ = arrive(3, hn1)
    h4, ffn1 = _ffn_fwd(h3, hn1, rstdf1, wg1, wu1, wd1, cw1, cb[1], 1)

    dh4, dh4_bf, loss_row, d_final = final_loss(h4, target, final_norm[None], name="loss")
    loss = lax.psum(loss_row[0, 0], ("x", "y", "c"))

    dh3, dh3_bf, big3, (dcw1, dcb1, dnf1) = _ffn_bwd(
        dh4, dh4_bf, h3, norm_ffn[1:2], ffn1, wg1, wu1, wd1, cw1, cb[1], 1, None)
    red3 = _reduce_to_chip(big3, place, tag="g3")

    def as_blocks(pair):
        return tuple(a.reshape(N_CHIPS, -1, d_model) for a in pair)

    dwout_o = as_blocks(mm_tn(mix1[None], dh3_bf[None], name="l1_dwout"))
    dmix1 = mm_nt_each(dh3_bf, wout_o, name="l1_dmix", pin=red3[-1])[0]
    dp1, dsn, dws, dbs = sgu_bwd(p1, dmix1, rstd_v, sn_full, ws, bs, name="l1_mix_bwd")
    dwin_o = mm_tn(xn1[None], dp1, name="l1_dwin")
    dxn1 = mm_nt_sum([(dp1, win_o)], name="l1_dxn")
    dh2, dh2_bf, dnm1 = rms_bwd(dxn1, h2, norm_mix[1:2], rstd1, dh3, name="l1_rms_bwd")
    red2 = _reduce_to_chip([dwin_o, dwout_o], place, tag="g2")

    dh1, dh1_bf, big1, (dcw0, dcb0, dnf0) = _ffn_bwd(
        dh2, dh2_bf, h1, norm_ffn[0:1], ffn0, wg0, wu0, wd0, cw0, cb[0], 0, red2[-1])
    red1 = _reduce_to_chip(big1, place, tag="g1")

    dwout_e = as_blocks(mm_tn(mix0[None], dh1_bf[None], name="l0_dwout"))
    dmix0 = mm_nt_each(dh1_bf, wout_e, name="l0_dmix", pin=red1[-1])[0]
    dproj0, dca, dwp, dps = even_bwd(proj0, dmix0, ca_full, wp_bf, wp_t_bf, pool_scale, name="l0_mix_bwd")
    dwin_e = mm_tn(xn0[None], dproj0, name="l0_dwin")
    dxn0 = mm_nt_sum([(dproj0, win_e)], name="l0_dxn")
    dh0, _, dnm0 = rms_bwd(dxn0, h0, norm_mix[0:1], rstd0, dh1, name="l0_rms_bwd")
    grad_x = dh0[None]

    small_parts = dict(
        norm_mix=jnp.concatenate([dnm0, dnm1]), norm_ffn=jnp.concatenate([dnf0, dnf1]), final_norm=d_final,
        conv_a=dca, w_pool=dwp, pool_scale=dps, sgu_norm=dsn, w_spatial=dws, b_spatial=dbs,
        conv_ffn=jnp.stack([dcw0, dcw1]), b_conv_ffn=jnp.stack([dcb0, dcb1]))
    flat = jnp.concatenate([v.reshape(-1) for v in small_parts.values()])
    pad = (-flat.shape[0]) % (N_CHIPS * 32 * 128)
    small = jnp.pad(flat, (0, pad)).reshape(N_CHIPS, -1, 128)
    red0 = _reduce_to_chip([dwin_e, dwout_e, (small, small)], place, tag="g0")

    full3 = _reduce_finish(red3, place, red0[-1], tag="g3")
    full2 = _reduce_finish(red2, place, full3[0], tag="g2")
    full1 = _reduce_finish(red1, place, full2[0], tag="g1")
    full0 = _reduce_finish(red0, place, full1[0], tag="g0")
    small_slots = lax.dynamic_update_index_in_dim(jnp.zeros(small.shape, F32), full0[2], chip, 0)
    small_sum = gather_chip_blocks(small_slots, name="gather_small").reshape(-1)
    grads = {
        "w_in_even": full0[0][None], "w_out_even": full0[1][None],
        "w_in_odd": full2[0][None], "w_out_odd": full2[1][None],
        "w_ffn_gate": jnp.stack([full1[0], full3[0]]), "w_ffn_up": jnp.stack([full1[1], full3[1]]),
        "w_ffn_down": jnp.stack([full1[2], full3[2]])}
    off = 0
    small_red = {}
    for nm, v in small_parts.items():
        small_red[nm] = small_sum[off:off + v.size].reshape(v.shape)
        off += v.size
    for nm in ("norm_mix", "norm_ffn", "pool_scale"):
        grads[nm] = small_red[nm].reshape(weights[nm].shape)
    grads["final_norm"] = small_red["final_norm"].reshape(weights["final_norm"].shape)
    grads["w_pool"] = small_red["w_pool"][None]
    grads["w_spatial"] = small_red["w_spatial"][None]
    grads["b_spatial"] = small_red["b_spatial"].reshape(weights["b_spatial"].shape)
    grads["b_conv_ffn"] = small_red["b_conv_ffn"].reshape(weights["b_conv_ffn"].shape)
    grads["conv_a"] = lax.dynamic_slice_in_dim(small_red["conv_a"], chip * conv_a.shape[-1], conv_a.shape[-1], 1)[None]
    grads["sgu_norm"] = lax.dynamic_slice_in_dim(small_red["sgu_norm"], chip * sgu_norm.shape[-1], sgu_norm.shape[-1], 1)
    grads["conv_ffn"] = lax.dynamic_index_in_dim(small_red["conv_ffn"], chip, 1, keepdims=False)

    deltas, new_m, new_v = {}, {}, {}
    for nm in order:
        w = weights[nm]
        w2 = w[None] if w.ndim == 1 else w
        shp = w2.shape
        d, nm_, nv_ = adamw(w2, grads[nm].reshape(shp), m_in[nm].reshape(shp), v_in[nm].reshape(shp),
                            name=f"adamw_{nm}")
        deltas[nm], new_m[nm], new_v[nm] = d.reshape(w.shape), nm_.reshape(w.shape), nv_.reshape(w.shape)

    return (loss, grad_x, *[grads[n] for n in order], *[deltas[n] for n in order],
            *[new_m[n] for n in order], *[new_v[n] for n in order])
```

```python
import jax
import jax.numpy as jnp
from jax import lax
from jax.experimental import pallas as pl
from jax.experimental.pallas import tpu as pltpu

F32 = jnp.float32
BF16 = jnp.bfloat16
MESH = pl.DeviceIdType.MESH
ANY = pl.BlockSpec(memory_space=pl.ANY)

EPS = 1e-6
POOL_WINDOWS = (2, 4, 8, 16)
GROUP = 128
CHUNK = 128
N_CHIPS = 4
N_DEV = 8
ROW_TILE = 512
HALO = 16
VMEM_LIMIT = 56 * 1024 * 1024

ADAM_LR = 0.001
ADAM_B1 = 0.9
ADAM_B2 = 0.999
ADAM_EPS = 1e-08
ADAM_WD = 0.01
ADAM_STEP = 10


def _params(*sem):
    return pltpu.CompilerParams(dimension_semantics=sem, vmem_limit_bytes=VMEM_LIMIT)


def _layer_spec(block, l, idx):
    if l is None:
        return pl.BlockSpec(block, idx)
    return pl.BlockSpec((None,) + block, lambda *g: (l,) + idx(*g))


def mm_nn(a, b, *, l=None, name):
    s, k = a.shape
    j_n, n = b.shape[-3], b.shape[-1]
    tm = min(ROW_TILE, s)

    def body(a_ref, b_ref, o_ref):
        o_ref[...] = jnp.dot(a_ref[...], b_ref[...], preferred_element_type=F32)

    return pl.pallas_call(
        body, name=name, grid=(j_n, s // tm),
        in_specs=[pl.BlockSpec((tm, k), lambda j, i: (i, 0)),
                  _layer_spec((None, k, n), l, lambda j, i: (j, 0, 0))],
        out_specs=pl.BlockSpec((None, tm, n), lambda j, i: (j, i, 0)),
        out_shape=jax.ShapeDtypeStruct((j_n, s, n), F32),
        compiler_params=_params("parallel", "parallel"))(a, b)


def mm_acc(a, b, res, *, l=None, name):
    j_n, s, kj = a.shape
    n = b.shape[-1]
    tm = min(ROW_TILE, s)

    def body(a_ref, b_ref, r_ref, o_ref):
        acc = r_ref[...]
        for j in range(j_n):
            acc = acc + jnp.dot(a_ref[j], b_ref[j], preferred_element_type=F32)
        o_ref[...] = acc

    return pl.pallas_call(
        body, name=name, grid=(s // tm,),
        in_specs=[pl.BlockSpec((j_n, tm, kj), lambda i: (0, i, 0)),
                  _layer_spec((j_n, kj, n), l, lambda i: (0, 0, 0)),
                  pl.BlockSpec((tm, n), lambda i: (i, 0))],
        out_specs=pl.BlockSpec((tm, n), lambda i: (i, 0)),
        out_shape=jax.ShapeDtypeStruct((s, n), F32),
        compiler_params=_params("parallel"))(a, b, res)


_NT = (((1,), (1,)), ((), ()))
_TN = (((0,), (0,)), ((), ()))


def mm_nt_sum(pairs, *, l=None, name):
    j_n, s, nj = pairs[0][0].shape
    k = pairs[0][1].shape[-2]
    tm = min(ROW_TILE, s)
    n_p = len(pairs)

    def body(*refs):
        o_ref = refs[-1]
        acc = jnp.zeros((tm, k), F32)
        for p in range(n_p):
            dy_ref, w_ref = refs[2 * p], refs[2 * p + 1]
            for j in range(j_n):
                acc = acc + lax.dot_general(dy_ref[j], w_ref[j], _NT, preferred_element_type=F32)
        o_ref[...] = acc

    in_specs, args = [], []
    for dy, w in pairs:
        in_specs += [pl.BlockSpec((j_n, tm, nj), lambda i: (0, i, 0)),
                     _layer_spec((j_n, k, nj), l, lambda i: (0, 0, 0))]
        args += [dy, w]
    return pl.pallas_call(
        body, name=name, grid=(s // tm,), in_specs=in_specs,
        out_specs=pl.BlockSpec((tm, k), lambda i: (i, 0)),
        out_shape=jax.ShapeDtypeStruct((s, k), F32),
        compiler_params=_params("parallel"))(*args)


def mm_nt_each(a, b, *, l=None, name, pin=None):
    s, k = a.shape
    j_n, nj = b.shape[-3], b.shape[-2]
    tm = min(ROW_TILE, s)

    def body(a_ref, b_ref, *rest):
        rest[-1][...] = lax.dot_general(a_ref[...], b_ref[...], _NT, preferred_element_type=F32)

    return pl.pallas_call(
        body, name=name, grid=(j_n, s // tm),
        in_specs=[pl.BlockSpec((tm, k), lambda j, i: (i, 0)),
                  _layer_spec((None, nj, k), l, lambda j, i: (j, 0, 0))] + ([ANY] if pin is not None else []),
        out_specs=pl.BlockSpec((None, tm, nj), lambda j, i: (j, i, 0)),
        out_shape=jax.ShapeDtypeStruct((j_n, s, nj), F32),
        compiler_params=_params("parallel", "parallel"))(a, b, *([pin] if pin is not None else []))


def mm_tn(a, dy, *, name, pin=None):
    ja, s, k = a.shape
    jd, _, n = dy.shape
    j_n = max(ja, jd)
    tk = min(ROW_TILE, s)

    def body(a_ref, d_ref, *rest):
        o_ref = rest[-1]

        @pl.when(pl.program_id(1) == 0)
        def _():
            o_ref[...] = jnp.zeros_like(o_ref)
        o_ref[...] += lax.dot_general(a_ref[...], d_ref[...], _TN, preferred_element_type=F32)

    return pl.pallas_call(
        body, name=name, grid=(j_n, s // tk),
        in_specs=[pl.BlockSpec((None, tk, k), (lambda j, i: (j, i, 0)) if ja > 1 else (lambda j, i: (0, i, 0))),
                  pl.BlockSpec((None, tk, n), (lambda j, i: (j, i, 0)) if jd > 1 else (lambda j, i: (0, i, 0)))]
        + ([ANY] if pin is not None else []),
        out_specs=pl.BlockSpec((None, k, n), lambda j, i: (j, 0, 0)),
        out_shape=jax.ShapeDtypeStruct((j_n, k, n), F32),
        compiler_params=_params("parallel", "arbitrary"))(a, dy, *([pin] if pin is not None else []))


def _back(x, k):
    return pltpu.roll(x, k, 0)


def _fwd(x, k):
    return pltpu.roll(x, x.shape[0] - k, 0)


def _causal_conv(x, w_ref):
    return w_ref[0:1, :] * _back(x, 2) + w_ref[1:2, :] * _back(x, 1) + w_ref[2:3, :] * x


def _causal_conv_t(dy, w_ref):
    return w_ref[2:3, :] * dy + w_ref[1:2, :] * _fwd(dy, 1) + w_ref[0:1, :] * _fwd(dy, 2)


def _gelu(x):
    return 0.5 * x * (1.0 + lax.erf(x * 0.7071067811865476))


def _gelu_grad(x):
    return 0.5 * (1.0 + lax.erf(x * 0.7071067811865476)) + x * jnp.exp(-0.5 * x * x) * 0.3989422804014327


def _colsum(x):
    return jnp.sum(x, axis=0, keepdims=True)


def rms_fwd(h, gain, *, name, pin=None):
    s, d = h.shape
    ts = min(ROW_TILE, s)

    def body(h_ref, g_ref, *rest):
        o_ref, r_ref = rest[-2:]
        x = h_ref[...]
        rstd = lax.rsqrt(jnp.mean(x * x, axis=-1, keepdims=True) + EPS)
        o_ref[...] = (x * rstd * g_ref[...]).astype(BF16)
        r_ref[...] = rstd

    return pl.pallas_call(
        body, name=name, grid=(s // ts,),
        in_specs=[pl.BlockSpec((ts, d), lambda i: (i, 0)), pl.BlockSpec((1, d), lambda i: (0, 0))]
        + ([ANY] if pin is not None else []),
        out_specs=[pl.BlockSpec((ts, d), lambda i: (i, 0)), pl.BlockSpec((ts, 1), lambda i: (i, 0))],
        out_shape=[jax.ShapeDtypeStruct((s, d), BF16), jax.ShapeDtypeStruct((s, 1), F32)],
        compiler_params=_params("parallel"))(h, gain, *([pin] if pin is not None else []))


def rms_bwd(dxn, h, gain, rstd, dres, *, name):
    s, d = h.shape
    ts = min(ROW_TILE, s)

    def body(dx_ref, h_ref, g_ref, r_ref, dr_ref, o_ref, ob_ref, dg_ref):
        @pl.when(pl.program_id(0) == 0)
        def _():
            dg_ref[...] = jnp.zeros_like(dg_ref)
        rstd_v = r_ref[...]
        hhat = h_ref[...] * rstd_v
        dx = dx_ref[...]
        dg_ref[...] += _colsum(dx * hhat)
        dxg = dx * g_ref[...]
        dh = dr_ref[...] + rstd_v * (dxg - hhat * jnp.mean(dxg * hhat, axis=-1, keepdims=True))
        o_ref[...] = dh
        ob_ref[...] = dh.astype(BF16)

    row = pl.BlockSpec((ts, d), lambda i: (i, 0))
    vec = pl.BlockSpec((1, d), lambda i: (0, 0))
    return pl.pallas_call(
        body, name=name, grid=(s // ts,),
        in_specs=[row, row, vec, pl.BlockSpec((ts, 1), lambda i: (i, 0)), row],
        out_specs=[row, row, vec],
        out_shape=[jax.ShapeDtypeStruct((s, d), F32), jax.ShapeDtypeStruct((s, d), BF16),
                   jax.ShapeDtypeStruct((1, d), F32)],
        compiler_params=_params("arbitrary"))(dxn, h, gain, rstd, dres)


def final_loss(h, target, gain, *, name):
    s, d = h.shape
    ts = min(ROW_TILE, s)

    def body(h_ref, t_ref, g_ref, o_ref, ob_ref, l_ref, dg_ref):
        @pl.when(pl.program_id(0) == 0)
        def _():
            l_ref[...] = jnp.zeros_like(l_ref)
            dg_ref[...] = jnp.zeros_like(dg_ref)
        x = h_ref[...]
        rstd = lax.rsqrt(jnp.mean(x * x, axis=-1, keepdims=True) + EPS)
        hhat = x * rstd
        err = hhat * g_ref[...] - t_ref[...]
        l_ref[...] += 0.5 * jnp.sum(jnp.mean(err * err, axis=-1, keepdims=True), axis=0, keepdims=True)
        dy = err * (1.0 / d)
        dg_ref[...] += _colsum(dy * hhat)
        dyg = dy * g_ref[...]
        dh = rstd * (dyg - hhat * jnp.mean(dyg * hhat, axis=-1, keepdims=True))
        o_ref[...] = dh
        ob_ref[...] = dh.astype(BF16)

    row = pl.BlockSpec((ts, d), lambda i: (i, 0))
    vec = pl.BlockSpec((1, d), lambda i: (0, 0))
    return pl.pallas_call(
        body, name=name, grid=(s // ts,),
        in_specs=[row, row, vec],
        out_specs=[row, row, pl.BlockSpec((1, 128), lambda i: (0, 0)), vec],
        out_shape=[jax.ShapeDtypeStruct((s, d), F32), jax.ShapeDtypeStruct((s, d), BF16),
                   jax.ShapeDtypeStruct((1, 128), F32), jax.ShapeDtypeStruct((1, d), F32)],
        compiler_params=_params("arbitrary"))(h, target, gain)


def _halo_specs(n_lead, ts, width, n_tiles):
    hb = ts // HALO
    prev = pl.BlockSpec((n_lead, HALO, width), lambda i: (0, jnp.maximum(i * hb - 1, 0), 0))
    nxt = pl.BlockSpec((n_lead, HALO, width), lambda i: (0, jnp.minimum((i + 1) * hb, n_tiles * hb - 1), 0))
    return prev, nxt


def _pool_fwd(z_ext, g, pos):
    w = POOL_WINDOWS[g]
    zg = z_ext[:, g * GROUP:(g + 1) * GROUP]
    acc = zg
    sh = 1
    while sh < w:
        acc = acc + _back(acc, sh)
        sh *= 2
    return acc[HALO:] / jnp.minimum(pos, float(w)) - zg[HALO:]


def even_fwd(proj, conv_a, w_pool, pool_scale, *, name):
    _, s, w = proj.shape
    ts = min(ROW_TILE, s)
    n_t = s // ts
    prev, _ = _halo_specs(4, ts, w, n_t)

    def body(p_ref, ph_ref, ca_ref, wp_ref, ps_ref, o_ref):
        i = pl.program_id(0)
        keep = jnp.where(i > 0, 1.0, 0.0)
        cv_ext = jnp.concatenate([ph_ref[1] * ph_ref[2] * keep, p_ref[1] * p_ref[2]], axis=0)
        o_ref[:, 0:w] = (p_ref[0] * _causal_conv(cv_ext, ca_ref)[HALO:]).astype(BF16)
        z_ext = jnp.concatenate([ph_ref[3] * keep, p_ref[3]], axis=0)
        pos = (i * ts + lax.broadcasted_iota(jnp.int32, (ts, 1), 0) + 1).astype(F32)
        for g in range(len(POOL_WINDOWS)):
            pooled = _pool_fwd(z_ext, g, pos)
            mixed = jnp.dot(pooled.astype(BF16), wp_ref[g], preferred_element_type=F32)
            cols = slice(g * GROUP, (g + 1) * GROUP)
            o_ref[:, w + g * GROUP:w + (g + 1) * GROUP] = (mixed * ps_ref[:, cols]).astype(BF16)

    return pl.pallas_call(
        body, name=name, grid=(n_t,),
        in_specs=[pl.BlockSpec((4, ts, w), lambda i: (0, i, 0)), prev,
                  pl.BlockSpec((3, w), lambda i: (0, 0)),
                  pl.BlockSpec((4, GROUP, GROUP), lambda i: (0, 0, 0)),
                  pl.BlockSpec((1, w), lambda i: (0, 0))],
        out_specs=pl.BlockSpec((ts, 2 * w), lambda i: (i, 0)),
        out_shape=jax.ShapeDtypeStruct((s, 2 * w), BF16),
        compiler_params=_params("parallel"))(proj, proj, conv_a, w_pool, pool_scale)


def even_bwd(proj, dmix, conv_a, w_pool, w_pool_t, pool_scale, *, name):
    _, s, w = proj.shape
    ts = min(ROW_TILE, s)
    n_t = s // ts
    prev, nxt = _halo_specs(4, ts, w, n_t)
    hb = ts // HALO
    dm_next = pl.BlockSpec((HALO, 2 * w), lambda i: (jnp.minimum((i + 1) * hb, n_t * hb - 1), 0))
    n_ext = ts + HALO

    def body(p_ref, pp_ref, pn_ref, dm_ref, dmn_ref, ca_ref, wp_ref, wpt_ref, ps_ref,
             dp_ref, dca_ref, dwp_ref, dps_ref):
        i = pl.program_id(0)

        @pl.when(i == 0)
        def _():
            dca_ref[...] = jnp.zeros_like(dca_ref)
            dwp_ref[...] = jnp.zeros_like(dwp_ref)
            dps_ref[...] = jnp.zeros_like(dps_ref)

        keep_p = jnp.where(i > 0, 1.0, 0.0)
        keep_n = jnp.where(i < n_t - 1, 1.0, 0.0)
        a_b, a_c, a_v = p_ref[0], p_ref[1], p_ref[2]
        cv_ext = jnp.concatenate([pp_ref[1] * pp_ref[2] * keep_p, a_c * a_v], axis=0)
        dy_a = dm_ref[:, 0:w]
        dp_ref[0] = (dy_a * _causal_conv(cv_ext, ca_ref)[HALO:]).astype(BF16)
        dcc = dy_a * a_b
        dca_ref[2:3, :] += _colsum(dcc * cv_ext[HALO:])
        dca_ref[1:2, :] += _colsum(dcc * _back(cv_ext, 1)[HALO:])
        dca_ref[0:1, :] += _colsum(dcc * _back(cv_ext, 2)[HALO:])
        dcc_ext = jnp.concatenate([dcc, dmn_ref[:, 0:w] * pn_ref[0] * keep_n], axis=0)
        dcv = _causal_conv_t(dcc_ext, ca_ref)[:ts]
        dp_ref[1] = (dcv * a_v).astype(BF16)
        dp_ref[2] = (dcv * a_c).astype(BF16)
        z_ext = jnp.concatenate([pp_ref[3] * keep_p, p_ref[3]], axis=0)
        pos = (i * ts + lax.broadcasted_iota(jnp.int32, (ts, 1), 0) + 1).astype(F32)
        pos_ext = (i * ts + lax.broadcasted_iota(jnp.int32, (n_ext, 1), 0) + 1).astype(F32)
        for g, win in enumerate(POOL_WINDOWS):
            cols = slice(g * GROUP, (g + 1) * GROUP)
            ycols = slice(w + g * GROUP, w + (g + 1) * GROUP)
            pooled = _pool_fwd(z_ext, g, pos).astype(BF16)
            mixed = jnp.dot(pooled, wp_ref[g], preferred_element_type=F32)
            dy_b = dm_ref[:, ycols]
            dps_ref[:, cols] += _colsum(dy_b * mixed)
            dmixed_ext = jnp.concatenate([dy_b, dmn_ref[:, ycols] * keep_n], axis=0) * ps_ref[:, cols]
            dmixed_ext = dmixed_ext.astype(BF16)
            dwp_ref[g] += lax.dot_general(pooled, dmixed_ext[:ts], _TN, preferred_element_type=F32)
            dpooled = jnp.dot(dmixed_ext, wpt_ref[g], preferred_element_type=F32)
            acc = dpooled / jnp.minimum(pos_ext, float(win))
            sh = 1
            while sh < win:
                acc = acc + _fwd(acc, sh)
                sh *= 2
            dp_ref[3, :, cols] = (acc[:ts] - dpooled[:ts]).astype(BF16)

    tile4 = pl.BlockSpec((4, ts, w), lambda i: (0, i, 0))
    const = lambda shape: pl.BlockSpec(shape, lambda i: (0,) * len(shape))
    return pl.pallas_call(
        body, name=name, grid=(n_t,),
        in_specs=[tile4, prev, nxt, pl.BlockSpec((ts, 2 * w), lambda i: (i, 0)), dm_next,
                  const((3, w)), const((4, GROUP, GROUP)), const((4, GROUP, GROUP)), const((1, w))],
        out_specs=[tile4, const((3, w)), const((4, GROUP, GROUP)), const((1, w))],
        out_shape=[jax.ShapeDtypeStruct((4, s, w), BF16), jax.ShapeDtypeStruct((3, w), F32),
                   jax.ShapeDtypeStruct((4, GROUP, GROUP), F32), jax.ShapeDtypeStruct((1, w), F32)],
        compiler_params=_params("arbitrary"))(proj, proj, proj, dmix, dmix, conv_a, w_pool, w_pool_t, pool_scale)


def _ffn_halo(ts, f, n_t, nxt):
    hb = ts // HALO
    if nxt:
        return pl.BlockSpec((None, HALO, f), lambda j, i: (j, jnp.minimum((i + 1) * hb, n_t * hb - 1), 0))
    return pl.BlockSpec((None, HALO, f), lambda j, i: (j, jnp.maximum(i * hb - 1, 0), 0))


def ffn_act_fwd(g, up, cw, cb, *, name):
    j_n, s, f = g.shape
    ts = min(ROW_TILE, s)
    n_t = s // ts

    def body(g_ref, gp_ref, u_ref, cw_ref, cb_ref, o_ref):
        keep = jnp.where(pl.program_id(1) > 0, 1.0, 0.0)
        g_ext = jnp.concatenate([gp_ref[...] * keep, g_ref[...]], axis=0)
        gc = _causal_conv(g_ext, cw_ref)[HALO:] + cb_ref[...]
        o_ref[...] = (gc * jax.nn.sigmoid(gc) * u_ref[...]).astype(BF16)

    tile = pl.BlockSpec((None, ts, f), lambda j, i: (j, i, 0))
    return pl.pallas_call(
        body, name=name, grid=(j_n, n_t),
        in_specs=[tile, _ffn_halo(ts, f, n_t, False), tile,
                  pl.BlockSpec((None, 3, f), lambda j, i: (j, 0, 0)),
                  pl.BlockSpec((None, 1, f), lambda j, i: (j, 0, 0))],
        out_specs=tile,
        out_shape=jax.ShapeDtypeStruct((j_n, s, f), BF16),
        compiler_params=_params("parallel", "parallel"))(g, g, up, cw, cb)


def ffn_act_bwd(g, up, dact, cw, cb, *, name):
    j_n, s, f = g.shape
    ts = min(ROW_TILE, s)
    n_t = s // ts

    def body(g_ref, gp_ref, gn_ref, u_ref, un_ref, d_ref, dn_ref, cw_ref, cb_ref,
             dg_ref, du_ref, dcw_ref, dcb_ref):
        i = pl.program_id(1)

        @pl.when(i == 0)
        def _():
            dcw_ref[...] = jnp.zeros_like(dcw_ref)
            dcb_ref[...] = jnp.zeros_like(dcb_ref)

        keep_p = jnp.where(i > 0, 1.0, 0.0)
        keep_n = jnp.where(i < n_t - 1, 1.0, 0.0)
        g_ext = jnp.concatenate([gp_ref[...] * keep_p, g_ref[...], gn_ref[...]], axis=0)
        gc = _causal_conv(g_ext, cw_ref)[HALO:] + cb_ref[...]
        sig = jax.nn.sigmoid(gc)
        dact_ext = jnp.concatenate([d_ref[...], dn_ref[...] * keep_n], axis=0)
        du_ref[...] = (dact_ext * gc * sig)[:ts].astype(BF16)
        up_ext = jnp.concatenate([u_ref[...], un_ref[...]], axis=0)
        dgc = dact_ext * up_ext * (sig * (1.0 + gc * (1.0 - sig)))
        dg_ref[...] = _causal_conv_t(dgc, cw_ref)[:ts].astype(BF16)
        dgc_t = dgc[:ts]
        dcb_ref[...] += _colsum(dgc_t)
        dcw_ref[2:3, :] += _colsum(dgc_t * g_ext[HALO:HALO + ts])
        dcw_ref[1:2, :] += _colsum(dgc_t * _back(g_ext, 1)[HALO:HALO + ts])
        dcw_ref[0:1, :] += _colsum(dgc_t * _back(g_ext, 2)[HALO:HALO + ts])

    tile = pl.BlockSpec((None, ts, f), lambda j, i: (j, i, 0))
    prev, nxt = _ffn_halo(ts, f, n_t, False), _ffn_halo(ts, f, n_t, True)
    return pl.pallas_call(
        body, name=name, grid=(j_n, n_t),
        in_specs=[tile, prev, nxt, tile, nxt, tile, nxt,
                  pl.BlockSpec((None, 3, f), lambda j, i: (j, 0, 0)),
                  pl.BlockSpec((None, 1, f), lambda j, i: (j, 0, 0))],
        out_specs=[tile, tile, pl.BlockSpec((None, 3, f), lambda j, i: (j, 0, 0)),
                   pl.BlockSpec((None, 1, f), lambda j, i: (j, 0, 0))],
        out_shape=[jax.ShapeDtypeStruct((j_n, s, f), BF16), jax.ShapeDtypeStruct((j_n, s, f), BF16),
                   jax.ShapeDtypeStruct((j_n, 3, f), F32), jax.ShapeDtypeStruct((j_n, 1, f), F32)],
        compiler_params=_params("parallel", "arbitrary"))(g, g, g, up, up, dact, dact, cw, cb)


def ffn_in_fwd(hn, wg, wu, cw, cb, *, name):
    s, d = hn.shape
    j_n, _, f = wg.shape
    tm = min(ROW_TILE, s)
    hb = tm // HALO

    def body(x_ref, xp_ref, wg_ref, wu_ref, cw_ref, cb_ref, g_ref, u_ref, a_ref):
        i, j = pl.program_id(0), pl.program_id(1)
        x_ext = jnp.concatenate([xp_ref[...], x_ref[...]], axis=0)
        g_ext = jnp.dot(x_ext, wg_ref[j], preferred_element_type=F32).astype(BF16)
        up = jnp.dot(x_ref[...], wu_ref[j], preferred_element_type=F32).astype(BF16)
        g_ref[...] = g_ext[HALO:]
        u_ref[...] = up
        keep = jnp.where(i > 0, 1.0, 0.0)
        g32 = jnp.concatenate([g_ext[:HALO].astype(F32) * keep, g_ext[HALO:].astype(F32)], axis=0)
        gc = _causal_conv(g32, cw_ref)[HALO:] + cb_ref[...]
        a_ref[...] = (gc * jax.nn.sigmoid(gc) * up.astype(F32)).astype(BF16)

    whole = pl.BlockSpec((j_n, d, f), lambda i, j: (0, 0, 0))
    tile = pl.BlockSpec((None, tm, f), lambda i, j: (j, i, 0))
    shape = jax.ShapeDtypeStruct((j_n, s, f), BF16)
    return pl.pallas_call(
        body, name=name, grid=(s // tm, j_n),
        in_specs=[pl.BlockSpec((tm, d), lambda i, j: (i, 0)),
                  pl.BlockSpec((HALO, d), lambda i, j: (jnp.maximum(i * hb - 1, 0), 0)),
                  whole, whole,
                  pl.BlockSpec((None, 3, f), lambda i, j: (j, 0, 0)),
                  pl.BlockSpec((None, 1, f), lambda i, j: (j, 0, 0))],
        out_specs=[tile, tile, tile], out_shape=[shape, shape, shape],
        compiler_params=_params("parallel", "parallel"))(hn, hn, wg, wu, cw, cb)


def ffn_bwd_a(dh_bf, hn, g, up, wd, cw, cb, *, name, pin=None):
    s, d = hn.shape
    j_n, _, f = g.shape
    tm = min(ROW_TILE, s)
    n_t = s // tm
    hb = tm // HALO

    def body(dh_ref, dhn_ref, x_ref, g_ref, gp_ref, gn_ref, u_ref, un_ref, wd_ref, cw_ref, cb_ref, *rest):
        dg_ref, du_ref, dwg_ref, dwu_ref, dwd_ref, dcw_ref, dcb_ref = rest[-7:]
        i = pl.program_id(1)

        @pl.when(i == 0)
        def _():
            for r in (dwg_ref, dwu_ref, dwd_ref, dcw_ref, dcb_ref):
                r[...] = jnp.zeros_like(r)

        keep_p = jnp.where(i > 0, 1.0, 0.0)
        keep_n = jnp.where(i < n_t - 1, 1.0, 0.0)
        dh = dh_ref[...]
        dact = lax.dot_general(jnp.concatenate([dh, dhn_ref[...]], axis=0), wd_ref[...], _NT,
                               preferred_element_type=F32)
        dact = jnp.concatenate([dact[:tm], dact[tm:] * keep_n], axis=0)
        g_ext = jnp.concatenate([gp_ref[...].astype(F32) * keep_p, g_ref[...].astype(F32),
                                 gn_ref[...].astype(F32)], axis=0)
        gc = _causal_conv(g_ext, cw_ref)[HALO:] + cb_ref[...]
        sig = jax.nn.sigmoid(gc)
        silu = gc * sig
        up_ext = jnp.concatenate([u_ref[...], un_ref[...]], axis=0).astype(F32)
        act = (silu * up_ext)[:tm].astype(BF16)
        dwd_ref[...] += lax.dot_general(act, dh, _TN, preferred_element_type=F32)
        dup = (dact * silu)[:tm].astype(BF16)
        du_ref[...] = dup
        dgc = dact * up_ext * (sig * (1.0 + gc * (1.0 - sig)))
        dg = _causal_conv_t(dgc, cw_ref)[:tm].astype(BF16)
        dg_ref[...] = dg
        x = x_ref[...]
        dwg_ref[...] += lax.dot_general(x, dg, _TN, preferred_element_type=F32)
        dwu_ref[...] += lax.dot_general(x, dup, _TN, preferred_element_type=F32)
        dgc_t = dgc[:tm]
        dcb_ref[...] += _colsum(dgc_t)
        dcw_ref[2:3, :] += _colsum(dgc_t * g_ext[HALO:HALO + tm])
        dcw_ref[1:2, :] += _colsum(dgc_t * _back(g_ext, 1)[HALO:HALO + tm])
        dcw_ref[0:1, :] += _colsum(dgc_t * _back(g_ext, 2)[HALO:HALO + tm])

    rows = pl.BlockSpec((tm, d), lambda j, i: (i, 0))
    rows_next = pl.BlockSpec((HALO, d), lambda j, i: (jnp.minimum((i + 1) * hb, n_t * hb - 1), 0))
    tile = pl.BlockSpec((None, tm, f), lambda j, i: (j, i, 0))
    prev, nxt = _ffn_halo(tm, f, n_t, False), _ffn_halo(tm, f, n_t, True)
    per_j = lambda r, c: pl.BlockSpec((None, r, c), lambda j, i: (j, 0, 0))
    return pl.pallas_call(
        body, name=name, grid=(j_n, n_t),
        in_specs=[rows, rows_next, rows, tile, prev, nxt, tile, nxt, per_j(f, d), per_j(3, f), per_j(1, f)]
        + ([ANY] if pin is not None else []),
        out_specs=[tile, tile, per_j(d, f), per_j(d, f), per_j(f, d), per_j(3, f), per_j(1, f)],
        out_shape=[jax.ShapeDtypeStruct((j_n, s, f), BF16), jax.ShapeDtypeStruct((j_n, s, f), BF16),
                   jax.ShapeDtypeStruct((j_n, d, f), F32), jax.ShapeDtypeStruct((j_n, d, f), F32),
                   jax.ShapeDtypeStruct((j_n, f, d), F32), jax.ShapeDtypeStruct((j_n, 3, f), F32),
                   jax.ShapeDtypeStruct((j_n, 1, f), F32)],
        compiler_params=_params("parallel", "arbitrary"))(
            dh_bf, dh_bf, hn, g, g, g, up, up, wd, cw, cb, *([pin] if pin is not None else []))


def ffn_bwd_b(dg, dup, wg, wu, h, gain, rstd, dres, *, name):
    j_n, s, f = dg.shape
    d = h.shape[1]
    tm = min(ROW_TILE // 2, s)

    def body(dg_ref, du_ref, wg_ref, wu_ref, h_ref, g_ref, r_ref, dr_ref, o_ref, ob_ref, dgain_ref):
        @pl.when(pl.program_id(0) == 0)
        def _():
            dgain_ref[...] = jnp.zeros_like(dgain_ref)
        dx = jnp.zeros((tm, d), F32)
        for j in range(j_n):
            dx = dx + lax.dot_general(dg_ref[j], wg_ref[j], _NT, preferred_element_type=F32)
            dx = dx + lax.dot_general(du_ref[j], wu_ref[j], _NT, preferred_element_type=F32)
        rstd_v = r_ref[...]
        hhat = h_ref[...] * rstd_v
        dgain_ref[...] += _colsum(dx * hhat)
        dxg = dx * g_ref[...]
        dh = dr_ref[...] + rstd_v * (dxg - hhat * jnp.mean(dxg * hhat, axis=-1, keepdims=True))
        o_ref[...] = dh
        ob_ref[...] = dh.astype(BF16)

    tile4 = pl.BlockSpec((j_n, tm, f), lambda i: (0, i, 0))
    whole = pl.BlockSpec((j_n, d, f), lambda i: (0, 0, 0))
    row = pl.BlockSpec((tm, d), lambda i: (i, 0))
    vec = pl.BlockSpec((1, d), lambda i: (0, 0))
    return pl.pallas_call(
        body, name=name, grid=(s // tm,),
        in_specs=[tile4, tile4, whole, whole, row, vec, pl.BlockSpec((tm, 1), lambda i: (i, 0)), row],
        out_specs=[row, row, vec],
        out_shape=[jax.ShapeDtypeStruct((s, d), F32), jax.ShapeDtypeStruct((s, d), BF16),
                   jax.ShapeDtypeStruct((1, d), F32)],
        compiler_params=_params("arbitrary"))(dg, dup, wg, wu, h, gain, rstd, dres)


def _sgu_gate(vn_bf, ws_ref, bs_ref, h, rows):
    tri = lax.broadcasted_iota(jnp.int32, (CHUNK, CHUNK), 0) >= lax.broadcasted_iota(jnp.int32, (CHUNK, CHUNK), 1)
    ws = jnp.where(tri, ws_ref[h], 0.0).astype(BF16)
    cols = slice((h % 4) * GROUP, (h % 4 + 1) * GROUP)
    return ws, jnp.dot(ws, vn_bf[h // 4][rows, cols], preferred_element_type=F32) + bs_ref[h]


def sgu_fwd(p, sgu_norm, w_spatial, b_spatial, *, name):
    _, s, w = p.shape
    ts = min(ROW_TILE, s)
    n_heads = w_spatial.shape[0]

    def body(p_ref, n_ref, ws_ref, bs_ref, o_ref, r_ref):
        v = [_gelu(p_ref[2]), _gelu(p_ref[3])]
        ms = (jnp.sum(v[0] * v[0], axis=-1, keepdims=True) + jnp.sum(v[1] * v[1], axis=-1, keepdims=True)) / (2 * w)
        rstd = lax.rsqrt(ms + EPS)
        r_ref[...] = rstd
        vn = [(v[k] * rstd * n_ref[:, k * w:(k + 1) * w]).astype(BF16) for k in range(2)]
        for h in range(n_heads):
            cols = slice((h % 4) * GROUP, (h % 4 + 1) * GROUP)
            for c in range(ts // CHUNK):
                rows = slice(c * CHUNK, (c + 1) * CHUNK)
                _, gate = _sgu_gate(vn, ws_ref, bs_ref, h, rows)
                u = _gelu(p_ref[h // 4, rows, cols])
                o_ref[rows, h * GROUP:(h + 1) * GROUP] = (u * gate).astype(BF16)

    const = lambda shape: pl.BlockSpec(shape, lambda i: (0,) * len(shape))
    return pl.pallas_call(
        body, name=name, grid=(s // ts,),
        in_specs=[pl.BlockSpec((4, ts, w), lambda i: (0, i, 0)), const((1, 2 * w)),
                  const((n_heads, CHUNK, CHUNK)), const((n_heads, CHUNK, 1))],
        out_specs=[pl.BlockSpec((ts, 2 * w), lambda i: (i, 0)), pl.BlockSpec((ts, 1), lambda i: (i, 0))],
        out_shape=[jax.ShapeDtypeStruct((s, 2 * w), BF16), jax.ShapeDtypeStruct((s, 1), F32)],
        compiler_params=_params("parallel"))(p, sgu_norm, w_spatial, b_spatial)


def sgu_bwd(p, dmix, rstd, sgu_norm, w_spatial, b_spatial, *, name):
    _, s, w = p.shape
    ts = min(ROW_TILE, s)
    n_heads = w_spatial.shape[0]

    def body(p_ref, dm_ref, r_ref, n_ref, ws_ref, bs_ref, dp_ref, dn_ref, dws_ref, dbs_ref, dvn_ref):
        @pl.when(pl.program_id(0) == 0)
        def _():
            dn_ref[...] = jnp.zeros_like(dn_ref)
            dws_ref[...] = jnp.zeros_like(dws_ref)
            dbs_ref[...] = jnp.zeros_like(dbs_ref)

        rstd_v = r_ref[...]
        vhat = [_gelu(p_ref[2 + k]) * rstd_v for k in range(2)]
        vn = [(vhat[k] * n_ref[:, k * w:(k + 1) * w]).astype(BF16) for k in range(2)]
        tri = lax.broadcasted_iota(jnp.int32, (CHUNK, CHUNK), 0) >= lax.broadcasted_iota(jnp.int32, (CHUNK, CHUNK), 1)
        for h in range(n_heads):
            cols = slice((h % 4) * GROUP, (h % 4 + 1) * GROUP)
            ocols = slice(h * GROUP, (h + 1) * GROUP)
            for c in range(ts // CHUNK):
                rows = slice(c * CHUNK, (c + 1) * CHUNK)
                ws, gate = _sgu_gate(vn, ws_ref, bs_ref, h, rows)
                pu = p_ref[h // 4, rows, cols]
                dm = dm_ref[rows, ocols]
                dp_ref[h // 4, rows, cols] = (dm * gate * _gelu_grad(pu)).astype(BF16)
                dgate = dm * _gelu(pu)
                dbs_ref[h] += jnp.sum(dgate, axis=-1, keepdims=True)
                dgate_bf = dgate.astype(BF16)
                dws = lax.dot_general(dgate_bf, vn[h // 4][rows, cols], _NT, preferred_element_type=F32)
                dws_ref[h] += jnp.where(tri, dws, 0.0)
                dvn_ref[rows, ocols] = lax.dot_general(ws, dgate_bf, _TN, preferred_element_type=F32)
        for k in range(2):
            kc = slice(k * w, (k + 1) * w)
            dvn = dvn_ref[:, kc]
            dn_ref[:, kc] += _colsum(dvn * vhat[k])
        dvh = [dvn_ref[:, k * w:(k + 1) * w] * n_ref[:, k * w:(k + 1) * w] for k in range(2)]
        dot = (jnp.sum(dvh[0] * vhat[0], axis=-1, keepdims=True)
               + jnp.sum(dvh[1] * vhat[1], axis=-1, keepdims=True)) / (2 * w)
        for k in range(2):
            dv = rstd_v * (dvh[k] - vhat[k] * dot)
            dp_ref[2 + k] = (dv * _gelu_grad(p_ref[2 + k])).astype(BF16)

    const = lambda shape: pl.BlockSpec(shape, lambda i: (0,) * len(shape))
    tile4 = pl.BlockSpec((4, ts, w), lambda i: (0, i, 0))
    return pl.pallas_call(
        body, name=name, grid=(s // ts,),
        in_specs=[tile4, pl.BlockSpec((ts, 2 * w), lambda i: (i, 0)), pl.BlockSpec((ts, 1), lambda i: (i, 0)),
                  const((1, 2 * w)), const((n_heads, CHUNK, CHUNK)), const((n_heads, CHUNK, 1))],
        out_specs=[tile4, const((1, 2 * w)), const((n_heads, CHUNK, CHUNK)), const((n_heads, CHUNK, 1))],
        out_shape=[jax.ShapeDtypeStruct((4, s, w), BF16), jax.ShapeDtypeStruct((1, 2 * w), F32),
                   jax.ShapeDtypeStruct((n_heads, CHUNK, CHUNK), F32),
                   jax.ShapeDtypeStruct((n_heads, CHUNK, 1), F32)],
        scratch_shapes=[pltpu.VMEM((ts, 2 * w), F32)],
        compiler_params=_params("arbitrary"))(p, dmix, rstd, sgu_norm, w_spatial, b_spatial)


def _row_tile(rows):
    if rows <= ROW_TILE:
        return rows
    for t in (512, 384, 352, 256, 128, 64, 32, 16, 8):
        if rows % t == 0:
            return t
    return rows


def adamw(w, g, m, v, *, name):
    shape = w.shape
    cols = shape[-1]
    rows = w.size // cols
    w2, g2, m2, v2 = (a.reshape(rows, cols) for a in (w, g, m, v))
    tr = _row_tile(rows)
    bc1 = 1.0 - ADAM_B1 ** ADAM_STEP
    bc2 = 1.0 - ADAM_B2 ** ADAM_STEP

    def body(w_ref, g_ref, m_ref, v_ref, d_ref, nm_ref, nv_ref):
        grad = g_ref[...]
        m_new = ADAM_B1 * m_ref[...] + (1.0 - ADAM_B1) * grad
        v_new = ADAM_B2 * v_ref[...] + (1.0 - ADAM_B2) * (grad * grad)
        nm_ref[...] = m_new
        nv_ref[...] = v_new
        d_ref[...] = -ADAM_LR * ((m_new / bc1) / (jnp.sqrt(v_new / bc2) + ADAM_EPS) + ADAM_WD * w_ref[...])

    spec = pl.BlockSpec((tr, cols), lambda i: (i, 0))
    outs = pl.pallas_call(
        body, name=name, grid=(rows // tr,),
        in_specs=[spec] * 4, out_specs=[spec] * 3,
        out_shape=[jax.ShapeDtypeStruct((rows, cols), F32)] * 3,
        compiler_params=_params("parallel"))(w2, g2, m2, v2)
    return tuple(o.reshape(shape) for o in outs)


def _place():
    return lax.axis_index("x"), lax.axis_index("y"), lax.axis_index("c")


def _other_chips(x, y):
    return [(1 - x, y), (x, 1 - y), (1 - x, 1 - y)]


HBM = pl.BlockSpec(memory_space=pltpu.HBM)
SEM = pl.BlockSpec(memory_space=pltpu.SEMAPHORE)
DATAFLOW = pltpu.SideEffectType.DATAFLOW_SIDE_EFFECTING


def _in_hbm(a):
    return pltpu.with_memory_space_constraint(a, pltpu.HBM)


def cast_into_slot(w, chip, *, l=None, name):
    rows, cols = w.shape[-2:]
    tr = _row_tile(rows)

    def body(chip_ref, w_ref, o_ref):
        o_ref[...] = w_ref[...].astype(BF16)

    in_spec = (pl.BlockSpec((tr, cols), lambda i, chip_ref: (i, 0)) if l is None
               else pl.BlockSpec((None, tr, cols), lambda i, chip_ref: (l, i, 0)))
    return pl.pallas_call(
        body, name=name,
        grid_spec=pltpu.PrefetchScalarGridSpec(
            num_scalar_prefetch=1, grid=(rows // tr,), in_specs=[in_spec],
            out_specs=pl.BlockSpec((None, tr, cols), lambda i, chip_ref: (chip_ref[0], i, 0))),
        out_shape=jax.ShapeDtypeStruct((N_CHIPS, rows, cols), BF16),
        compiler_params=_params("parallel"))(chip, w)


def _half(ref, slot, c):
    half = ref.shape[1] // 2
    return ref.at[slot, pl.ds(c * half, half), :]


def gather_start(groups, smalls):
    flat = [b for g in groups for b in g]
    n_b, n_s, n_g = len(flat), len(smalls), len(groups)

    def body(*refs):
        bufs, small_refs = refs[:n_b], refs[n_b:n_b + n_s]
        sems = refs[n_b + n_s:n_b + n_s + 2 * n_g + 2]
        token = refs[-1]
        x, y, c = _place()
        me = 2 * x + y
        chips = _other_chips(x, y)
        for si in range(n_s):
            piece = small_refs[si].at[me]
            for k, (px, py) in enumerate(chips):
                pltpu.make_async_remote_copy(
                    src_ref=piece, dst_ref=piece,
                    send_sem=sems[2 * n_g].at[3 * si + k], recv_sem=sems[2 * n_g + 1].at[3 * si + k],
                    device_id=(px, py, c), device_id_type=MESH).start()
        t = 0
        for gi, group in enumerate(groups):
            for ti in range(len(group)):
                piece = _half(bufs[t], me, c)
                t += 1
                for k, (px, py) in enumerate(chips):
                    pltpu.make_async_remote_copy(
                        src_ref=piece, dst_ref=piece,
                        send_sem=sems[2 * gi].at[3 * ti + k], recv_sem=sems[2 * gi + 1].at[3 * ti + k],
                        device_id=(px, py, c), device_id_type=MESH).start()
        token[...] = jnp.zeros_like(token)

    sem_shapes = []
    for group in groups:
        sem_shapes += [pltpu.SemaphoreType.DMA((3 * len(group),))] * 2
    sem_shapes += [pltpu.SemaphoreType.DMA((3 * n_s,))] * 2
    arrays = flat + list(smalls)
    n_sem = len(sem_shapes)
    res = pl.pallas_call(
        body, name="gather_start",
        out_shape=tuple(sem_shapes) + tuple(pltpu.HBM(a.shape, a.dtype) for a in arrays)
        + (jax.ShapeDtypeStruct((8, 128), F32),),
        in_specs=[HBM] * len(arrays),
        out_specs=tuple([SEM] * n_sem + [HBM] * len(arrays) + [pl.BlockSpec(memory_space=pltpu.VMEM)]),
        input_output_aliases={i: n_sem + i for i in range(len(arrays))},
        compiler_params=pltpu.CompilerParams(has_side_effects=DATAFLOW))(*[_in_hbm(a) for a in arrays])
    sems, thru, token = res[:n_sem], res[n_sem:-1], res[-1]
    out_groups, t = [], 0
    for group in groups:
        out_groups.append(list(thru[t:t + len(group)]))
        t += len(group)
    return sems, out_groups, list(thru[n_b:]), token


def gather_wait(bufs, send, recv, after, *, name, smalls=(), small_send=None, small_recv=None):
    n_b, n_s = len(bufs), len(smalls)
    arrays = list(bufs) + list(smalls)
    sem_ops = [send, recv] + ([small_send, small_recv] if n_s else [])

    def body(*refs):
        buf_refs, small_refs = refs[:n_b], refs[n_b:n_b + n_s]
        sems = refs[n_b + n_s:n_b + n_s + len(sem_ops)]
        x, y, c = _place()
        me = 2 * x + y
        chips = _other_chips(x, y)
        for ti in range(n_b):
            for k, (px, py) in enumerate(chips):
                done = pltpu.make_async_remote_copy(
                    src_ref=_half(buf_refs[ti], me, c), dst_ref=_half(buf_refs[ti], 2 * px + py, c),
                    send_sem=sems[0].at[3 * ti + k], recv_sem=sems[1].at[3 * ti + k],
                    device_id=(px, py, c), device_id_type=MESH)
                done.wait_send()
                done.wait_recv()
        for si in range(n_s):
            for k, (px, py) in enumerate(chips):
                done = pltpu.make_async_remote_copy(
                    src_ref=small_refs[si].at[me], dst_ref=small_refs[si].at[2 * px + py],
                    send_sem=sems[2].at[3 * si + k], recv_sem=sems[3].at[3 * si + k],
                    device_id=(px, py, c), device_id_type=MESH)
                done.wait_send()
                done.wait_recv()

    res = pl.pallas_call(
        body, name=name,
        out_shape=tuple(pltpu.HBM(a.shape, a.dtype) for a in arrays),
        in_specs=[HBM] * len(arrays) + [SEM] * len(sem_ops) + [ANY],
        out_specs=tuple([HBM] * len(arrays)),
        input_output_aliases={i: i for i in range(len(arrays))},
        compiler_params=pltpu.CompilerParams(has_side_effects=DATAFLOW))(*arrays, *sem_ops, after)
    return list(res[:n_b]), list(res[n_b:])


def gather_forward(bufs, *, name):
    n = len(bufs)

    def body(*refs):
        ins, outs = refs[:n], refs[n:2 * n]
        send_sems, recv_sems = refs[2 * n:]
        x, y, c = _place()
        chips = _other_chips(x, y)
        for t in range(n):
            for k, (px, py) in enumerate(chips):
                pltpu.make_async_remote_copy(
                    src_ref=_half(ins[t], 2 * px + py, c), dst_ref=_half(outs[t], 2 * px + py, c),
                    send_sem=send_sems.at[3 * t + k], recv_sem=recv_sems.at[3 * t + k],
                    device_id=(x, y, 1 - c), device_id_type=MESH).start()
        for t in range(n):
            for k, (px, py) in enumerate(chips):
                done = pltpu.make_async_remote_copy(
                    src_ref=_half(ins[t], 2 * px + py, c), dst_ref=_half(outs[t], 2 * px + py, 1 - c),
                    send_sem=send_sems.at[3 * t + k], recv_sem=recv_sems.at[3 * t + k],
                    device_id=(x, y, 1 - c), device_id_type=MESH)
                done.wait_send()
                done.wait_recv()

    return pl.pallas_call(
        body, name=name, in_specs=[ANY] * n, out_specs=[ANY] * n,
        out_shape=[jax.ShapeDtypeStruct(a.shape, a.dtype) for a in bufs],
        input_output_aliases={i: i for i in range(n)},
        scratch_shapes=[pltpu.SemaphoreType.DMA((3 * n,)), pltpu.SemaphoreType.DMA((3 * n,))],
        compiler_params=pltpu.CompilerParams(has_side_effects=True))(*bufs)


def reduce_stage_a(grads_bf, *, name):
    n = len(grads_bf)

    def body(*refs):
        srcs, outs = refs[:n], refs[n:2 * n]
        send_sems, recv_sems = refs[2 * n:]
        x, y, c = _place()
        for t in range(n):
            half = srcs[t].shape[1] // 2
            pltpu.make_async_remote_copy(
                src_ref=srcs[t].at[:, pl.ds((1 - c) * half, half), :], dst_ref=outs[t],
                send_sem=send_sems.at[t], recv_sem=recv_sems.at[t],
                device_id=(x, y, 1 - c), device_id_type=MESH).start()
        for t in range(n):
            half = srcs[t].shape[1] // 2
            done = pltpu.make_async_remote_copy(
                src_ref=srcs[t].at[:, pl.ds((1 - c) * half, half), :], dst_ref=outs[t],
                send_sem=send_sems.at[t], recv_sem=recv_sems.at[t],
                device_id=(x, y, 1 - c), device_id_type=MESH)
            done.wait_send()
            done.wait_recv()

    out_shape = [jax.ShapeDtypeStruct((a.shape[0], a.shape[1] // 2, a.shape[2]), a.dtype) for a in grads_bf]
    return pl.pallas_call(
        body, name=name, in_specs=[ANY] * n, out_specs=[ANY] * n, out_shape=out_shape,
        scratch_shapes=[pltpu.SemaphoreType.DMA((n,)), pltpu.SemaphoreType.DMA((n,))],
        compiler_params=pltpu.CompilerParams(has_side_effects=True))(*grads_bf)


def sum_stage_a(grad, recv, place, wire, *, name):
    j_n, half, cols = recv.shape

    def body(place_ref, g_ref, r_ref, o_ref, ob_ref):
        acc = g_ref[...] + r_ref[...]
        o_ref[...] = acc
        ob_ref[...] = acc.astype(wire)

    blk = (None, half, cols)
    return pl.pallas_call(
        body, name=name,
        grid_spec=pltpu.PrefetchScalarGridSpec(
            num_scalar_prefetch=1, grid=(j_n,),
            in_specs=[pl.BlockSpec(blk, lambda j, place_ref: (j, place_ref[1], 0)),
                      pl.BlockSpec(blk, lambda j, place_ref: (j, 0, 0))],
            out_specs=[pl.BlockSpec(blk, lambda j, place_ref: (j, 0, 0))] * 2),
        out_shape=[jax.ShapeDtypeStruct(recv.shape, F32), jax.ShapeDtypeStruct(recv.shape, wire)],
        compiler_params=_params("parallel"))(place, grad, recv)


def reduce_b_start(parts_bf, *, name):
    n = len(parts_bf)
    lands = [lax.empty((3,) + a.shape[1:], a.dtype) for a in parts_bf]

    def body(*refs):
        srcs, land = refs[:n], refs[n:2 * n]
        send_sems, recv_sems = refs[2 * n], refs[2 * n + 1]
        token = refs[-1]
        x, y, c = _place()
        chips = _other_chips(x, y)
        for t in range(n):
            for k, (px, py) in enumerate(chips):
                pltpu.make_async_remote_copy(
                    src_ref=srcs[t].at[2 * px + py], dst_ref=land[t].at[k],
                    send_sem=send_sems.at[3 * t + k], recv_sem=recv_sems.at[3 * t + k],
                    device_id=(px, py, c), device_id_type=MESH).start()
        token[...] = jnp.zeros_like(token)

    arrays = list(parts_bf) + lands
    res = pl.pallas_call(
        body, name=name,
        out_shape=(pltpu.SemaphoreType.DMA((3 * n,)), pltpu.SemaphoreType.DMA((3 * n,)))
        + tuple(pltpu.HBM(a.shape, a.dtype) for a in arrays) + (jax.ShapeDtypeStruct((8, 128), F32),),
        in_specs=[HBM] * (2 * n),
        out_specs=tuple([SEM, SEM] + [HBM] * (2 * n) + [pl.BlockSpec(memory_space=pltpu.VMEM)]),
        input_output_aliases={i: 2 + i for i in range(2 * n)},
        compiler_params=pltpu.CompilerParams(has_side_effects=DATAFLOW))(*[_in_hbm(a) for a in arrays])
    return res[0], res[1], list(res[2:2 + n]), list(res[2 + n:2 + 2 * n]), res[-1]


def reduce_b_wait(srcs, lands, send, recv, after, *, name):
    n = len(srcs)

    def body(*refs):
        src_refs, land = refs[:n], refs[n:2 * n]
        send_sems, recv_sems = refs[2 * n], refs[2 * n + 1]
        x, y, c = _place()
        chips = _other_chips(x, y)
        for t in range(n):
            for k, (px, py) in enumerate(chips):
                done = pltpu.make_async_remote_copy(
                    src_ref=src_refs[t].at[2 * px + py], dst_ref=land[t].at[k],
                    send_sem=send_sems.at[3 * t + k], recv_sem=recv_sems.at[3 * t + k],
                    device_id=(px, py, c), device_id_type=MESH)
                done.wait_send()
                done.wait_recv()

    arrays = list(srcs) + list(lands)
    res = pl.pallas_call(
        body, name=name,
        out_shape=tuple(pltpu.HBM(a.shape, a.dtype) for a in arrays),
        in_specs=[HBM] * (2 * n) + [SEM, SEM, ANY],
        out_specs=tuple([HBM] * (2 * n)),
        input_output_aliases={i: i for i in range(2 * n)},
        compiler_params=pltpu.CompilerParams(has_side_effects=DATAFLOW))(*arrays, send, recv, after)
    return list(res[n:])


def sum_stage_b(part, recv, place, *, name):
    _, half, cols = part.shape

    def body(place_ref, p_ref, r_ref, o_ref):
        acc = p_ref[...]
        for k in range(3):
            acc = acc + r_ref[k].astype(F32)
        o_ref[...] = acc

    return pl.pallas_call(
        body, name=name,
        grid_spec=pltpu.PrefetchScalarGridSpec(
            num_scalar_prefetch=1, grid=(1,),
            in_specs=[pl.BlockSpec((None, half, cols), lambda i, place_ref: (place_ref[0], 0, 0)),
                      pl.BlockSpec((3, half, cols), lambda i, place_ref: (0, 0, 0))],
            out_specs=pl.BlockSpec((half, cols), lambda i, place_ref: (place_ref[1], 0))),
        out_shape=jax.ShapeDtypeStruct((2 * half, cols), F32),
        compiler_params=_params("arbitrary"))(place, part, recv)


def reduce_stage_c(fulls, *, name):
    n = len(fulls)

    def body(*refs):
        ins, outs = refs[:n], refs[n:2 * n]
        send_sems, recv_sems = refs[2 * n:]
        x, y, c = _place()
        for t in range(n):
            half = ins[t].shape[0] // 2
            pltpu.make_async_remote_copy(
                src_ref=ins[t].at[pl.ds(c * half, half), :], dst_ref=outs[t].at[pl.ds(c * half, half), :],
                send_sem=send_sems.at[t], recv_sem=recv_sems.at[t],
                device_id=(x, y, 1 - c), device_id_type=MESH).start()
        for t in range(n):
            half = ins[t].shape[0] // 2
            done = pltpu.make_async_remote_copy(
                src_ref=ins[t].at[pl.ds(c * half, half), :], dst_ref=outs[t].at[pl.ds((1 - c) * half, half), :],
                send_sem=send_sems.at[t], recv_sem=recv_sems.at[t],
                device_id=(x, y, 1 - c), device_id_type=MESH)
            done.wait_send()
            done.wait_recv()

    return pl.pallas_call(
        body, name=name, in_specs=[ANY] * n, out_specs=[ANY] * n,
        out_shape=[jax.ShapeDtypeStruct(a.shape, a.dtype) for a in fulls],
        input_output_aliases={i: i for i in range(n)},
        scratch_shapes=[pltpu.SemaphoreType.DMA((n,)), pltpu.SemaphoreType.DMA((n,))],
        compiler_params=pltpu.CompilerParams(has_side_effects=True))(*fulls)


def gather_chip_blocks(slots, *, name):
    def body(in_ref, out_ref, send_sems, recv_sems):
        x, y, c = _place()
        me = 2 * x + y
        chips = _other_chips(x, y)
        for k, (px, py) in enumerate(chips):
            pltpu.make_async_remote_copy(
                src_ref=in_ref.at[me], dst_ref=out_ref.at[me],
                send_sem=send_sems.at[k], recv_sem=recv_sems.at[k],
                device_id=(px, py, c), device_id_type=MESH).start()
        for k, (px, py) in enumerate(chips):
            done = pltpu.make_async_remote_copy(
                src_ref=in_ref.at[me], dst_ref=out_ref.at[2 * px + py],
                send_sem=send_sems.at[k], recv_sem=recv_sems.at[k],
                device_id=(px, py, c), device_id_type=MESH)
            done.wait_send()
            done.wait_recv()

    return pl.pallas_call(
        body, name=name, in_specs=[ANY], out_specs=ANY,
        out_shape=jax.ShapeDtypeStruct(slots.shape, slots.dtype),
        input_output_aliases={0: 0},
        scratch_shapes=[pltpu.SemaphoreType.DMA((3,)), pltpu.SemaphoreType.DMA((3,))],
        compiler_params=pltpu.CompilerParams(has_side_effects=True))(slots)


def _ffn_fwd(h, hn, rstd, wg, wu, wd, cw, cb, l):
    g, up, act = ffn_in_fwd(hn, wg, wu, cw, cb, name=f"ffn{l}_in")
    out = mm_acc(act, wd, h, name=f"ffn{l}_down")
    return out, (hn, rstd, g, up)


def _ffn_bwd(dh, dh_bf, h, gain, saved, wg, wu, wd, cw, cb, l, pin):
    hn, rstd, g, up = saved
    dg, dup, dwg, dwu, dwd, dcw, dcb = ffn_bwd_a(dh_bf, hn, g, up, wd, cw, cb, name=f"ffn{l}_bwd_a", pin=pin)
    dh_in, dh_in_bf, dgain = ffn_bwd_b(dg, dup, wg, wu, h, gain, rstd, dh, name=f"ffn{l}_bwd_b")
    return dh_in, dh_in_bf, (dwg, dwu, dwd), (dcw, dcb, dgain)


def _reduce_to_chip(grads, place, *, tag):
    recv_a = reduce_stage_a([g[0] for g in grads], name=f"reduce_a_{tag}")
    parts = [sum_stage_a(g[0], r, place, g[1], name=f"sum_a_{tag}{i}")
             for i, (g, r) in enumerate(zip(grads, recv_a))]
    send, recv, srcs, lands, token = reduce_b_start([p[1] for p in parts], name=f"reduce_b_start_{tag}")
    return [p[0] for p in parts], send, recv, srcs, lands, token


def _reduce_finish(state, place, after, *, tag):
    parts, send, recv, srcs, lands, _ = state
    recv_b = reduce_b_wait(srcs, lands, send, recv, after, name=f"reduce_b_wait_{tag}")
    halves = [sum_stage_b(p, r, place, name=f"sum_b_{tag}{i}") for i, (p, r) in enumerate(zip(parts, recv_b))]
    return reduce_stage_c(halves, name=f"reduce_c_{tag}")


def kernel(x, norm_mix, norm_ffn, final_norm, w_in_even, conv_a, w_pool, pool_scale, w_out_even, w_in_odd, sgu_norm, w_spatial, b_spatial, w_out_odd, w_ffn_gate, w_ffn_up, conv_ffn, b_conv_ffn, w_ffn_down, loss_target, m_norm_mix, m_norm_ffn, m_final_norm, m_w_in_even, m_conv_a, m_w_pool, m_pool_scale, m_w_out_even, m_w_in_odd, m_sgu_norm, m_w_spatial, m_b_spatial, m_w_out_odd, m_w_ffn_gate, m_w_ffn_up, m_conv_ffn, m_b_conv_ffn, m_w_ffn_down, v_norm_mix, v_norm_ffn, v_final_norm, v_w_in_even, v_conv_a, v_w_pool, v_pool_scale, v_w_out_even, v_w_in_odd, v_sgu_norm, v_w_spatial, v_b_spatial, v_w_out_odd, v_w_ffn_gate, v_w_ffn_up, v_conv_ffn, v_b_conv_ffn, v_w_ffn_down):
    weights = dict(norm_mix=norm_mix, norm_ffn=norm_ffn, final_norm=final_norm, w_in_even=w_in_even,
                   conv_a=conv_a, w_pool=w_pool, pool_scale=pool_scale, w_out_even=w_out_even,
                   w_in_odd=w_in_odd, sgu_norm=sgu_norm, w_spatial=w_spatial, b_spatial=b_spatial,
                   w_out_odd=w_out_odd, w_ffn_gate=w_ffn_gate, w_ffn_up=w_ffn_up, conv_ffn=conv_ffn,
                   b_conv_ffn=b_conv_ffn, w_ffn_down=w_ffn_down)
    m_in = dict(norm_mix=m_norm_mix, norm_ffn=m_norm_ffn, final_norm=m_final_norm, w_in_even=m_w_in_even,
                conv_a=m_conv_a, w_pool=m_w_pool, pool_scale=m_pool_scale, w_out_even=m_w_out_even,
                w_in_odd=m_w_in_odd, sgu_norm=m_sgu_norm, w_spatial=m_w_spatial, b_spatial=m_b_spatial,
                w_out_odd=m_w_out_odd, w_ffn_gate=m_w_ffn_gate, w_ffn_up=m_w_ffn_up, conv_ffn=m_conv_ffn,
                b_conv_ffn=m_b_conv_ffn, w_ffn_down=m_w_ffn_down)
    v_in = dict(norm_mix=v_norm_mix, norm_ffn=v_norm_ffn, final_norm=v_final_norm, w_in_even=v_w_in_even,
                conv_a=v_conv_a, w_pool=v_w_pool, pool_scale=v_pool_scale, w_out_even=v_w_out_even,
                w_in_odd=v_w_in_odd, sgu_norm=v_sgu_norm, w_spatial=v_w_spatial, b_spatial=v_b_spatial,
                w_out_odd=v_w_out_odd, w_ffn_gate=v_w_ffn_gate, w_ffn_up=v_w_ffn_up, conv_ffn=v_conv_ffn,
                b_conv_ffn=v_b_conv_ffn, w_ffn_down=v_w_ffn_down)
    order = list(weights)

    chip = 2 * lax.axis_index("x") + lax.axis_index("y")
    core = lax.axis_index("c")
    place = jnp.stack([chip, core]).astype(jnp.int32)
    chip_arr = place[:1]

    h0 = x[0]
    target = loss_target[0]
    d_model = h0.shape[1]
    f_shard = w_ffn_gate.shape[-1]

    def own_slot(v):
        return lax.dynamic_update_index_in_dim(jnp.zeros((N_CHIPS,) + v.shape, v.dtype), v, chip, 0)

    groups = [
        [cast_into_slot(w_in_even[0], chip_arr, name="cast_win_e"),
         cast_into_slot(w_out_even[0], chip_arr, name="cast_wout_e")],
        [cast_into_slot(w_ffn_gate, chip_arr, l=0, name="cast_wg0"),
         cast_into_slot(w_ffn_up, chip_arr, l=0, name="cast_wu0"),
         cast_into_slot(w_ffn_down, chip_arr, l=0, name="cast_wd0")],
        [cast_into_slot(w_in_odd[0], chip_arr, name="cast_win_o"),
         cast_into_slot(w_out_odd[0], chip_arr, name="cast_wout_o")],
        [cast_into_slot(w_ffn_gate, chip_arr, l=1, name="cast_wg1"),
         cast_into_slot(w_ffn_up, chip_arr, l=1, name="cast_wu1"),
         cast_into_slot(w_ffn_down, chip_arr, l=1, name="cast_wd1")]]
    smalls = [own_slot(conv_a[0]), own_slot(sgu_norm), own_slot(conv_ffn[0]), own_slot(conv_ffn[1])]
    sems, groups, smalls, token = gather_start(groups, smalls)

    def arrive(gi, after, with_smalls=False):
        kw = dict(smalls=smalls, small_send=sems[-2], small_recv=sems[-1]) if with_smalls else {}
        bufs, small_out = gather_wait(groups[gi], sems[2 * gi], sems[2 * gi + 1], after, name=f"gather_wait{gi}", **kw)
        return gather_forward(bufs, name=f"gather_forward{gi}"), small_out

    cb = b_conv_ffn.reshape(-1, N_CHIPS, 1, f_shard)
    wp_bf = w_pool[0].astype(BF16)
    wp_t_bf = jnp.transpose(w_pool[0], (0, 2, 1)).astype(BF16)
    ws = w_spatial[0]
    bs = b_spatial[0][:, :, None]

    xn0, rstd0 = rms_fwd(h0, norm_mix[0:1], name="l0_rms", pin=token)
    (win_e, wout_e), (ca_g, sn_g, cw0, cw1) = arrive(0, xn0, with_smalls=True)
    wout_e = wout_e.reshape(1, -1, d_model)
    ca_full = jnp.transpose(ca_g, (1, 0, 2)).reshape(ca_g.shape[1], -1)
    sn_full = sn_g.reshape(1, -1)
    proj0 = mm_nn(xn0, win_e, name="l0_in")
    mix0 = even_fwd(proj0, ca_full, wp_bf, pool_scale, name="l0_mix")
    h1 = mm_acc(mix0[None], wout_e, h0, name="l0_out")
    hn0, rstdf0 = rms_fwd(h1, norm_ffn[0:1], name="ffn0_rms")
    (wg0, wu0, wd0), _ = arrive(1, hn0)
    h2, ffn0 = _ffn_fwd(h1, hn0, rstdf0, wg0, wu0, wd0, cw0, cb[0], 0)
    xn1, rstd1 = rms_fwd(h2, norm_mix[1:2], name="l1_rms")
    (win_o, wout_o), _ = arrive(2, xn1)
    wout_o = wout_o.reshape(1, -1, d_model)
    p1 = mm_nn(xn1, win_o, name="l1_in")
    mix1, rstd_v = sgu_fwd(p1, sn_full, ws, bs, name="l1_mix")
    h3 = mm_acc(mix1[None], wout_o, h2, name="l1_out")
    hn1, rstdf1 = rms_fwd(h3, norm_ffn[1:2], name="ffn1_rms")
    (wg1, wu1, wd1), _ = arrive(3, hn1)
    h4, ffn1 = _ffn_fwd(h3, hn1, rstdf1, wg1, wu1, wd1, cw1, cb[1], 1)

    dh4, dh4_bf, loss_row, d_final = final_loss(h4, target, final_norm[None], name="loss")
    loss = lax.psum(loss_row[0, 0], ("x", "y", "c"))

    dh3, dh3_bf, big3, (dcw1, dcb1, dnf1) = _ffn_bwd(
        dh4, dh4_bf, h3, norm_ffn[1:2], ffn1, wg1, wu1, wd1, cw1, cb[1], 1, None)
    red3 = _reduce_to_chip([(g, BF16) for g in big3], place, tag="g3")

    def as_blocks(a):
        return a.reshape(N_CHIPS, -1, d_model)

    dwout_o = as_blocks(mm_tn(mix1[None], dh3_bf[None], name="l1_dwout"))
    dmix1 = mm_nt_each(dh3_bf, wout_o, name="l1_dmix", pin=red3[-1])[0]
    dp1, dsn, dws, dbs = sgu_bwd(p1, dmix1, rstd_v, sn_full, ws, bs, name="l1_mix_bwd")
    dwin_o = mm_tn(xn1[None], dp1, name="l1_dwin")
    dxn1 = mm_nt_sum([(dp1, win_o)], name="l1_dxn")
    dh2, dh2_bf, dnm1 = rms_bwd(dxn1, h2, norm_mix[1:2], rstd1, dh3, name="l1_rms_bwd")
    red2 = _reduce_to_chip([(dwin_o, BF16), (dwout_o, BF16)], place, tag="g2")

    dh1, dh1_bf, big1, (dcw0, dcb0, dnf0) = _ffn_bwd(
        dh2, dh2_bf, h1, norm_ffn[0:1], ffn0, wg0, wu0, wd0, cw0, cb[0], 0, red2[-1])
    red1 = _reduce_to_chip([(g, BF16) for g in big1], place, tag="g1")

    dwout_e = as_blocks(mm_tn(mix0[None], dh1_bf[None], name="l0_dwout"))
    dmix0 = mm_nt_each(dh1_bf, wout_e, name="l0_dmix", pin=red1[-1])[0]
    dproj0, dca, dwp, dps = even_bwd(proj0, dmix0, ca_full, wp_bf, wp_t_bf, pool_scale, name="l0_mix_bwd")
    dwin_e = mm_tn(xn0[None], dproj0, name="l0_dwin")
    dxn0 = mm_nt_sum([(dproj0, win_e)], name="l0_dxn")
    dh0, _, dnm0 = rms_bwd(dxn0, h0, norm_mix[0:1], rstd0, dh1, name="l0_rms_bwd")
    grad_x = dh0[None]

    small_parts = dict(
        norm_mix=jnp.concatenate([dnm0, dnm1]), norm_ffn=jnp.concatenate([dnf0, dnf1]), final_norm=d_final,
        conv_a=dca, w_pool=dwp, pool_scale=dps, sgu_norm=dsn, w_spatial=dws, b_spatial=dbs,
        conv_ffn=jnp.stack([dcw0, dcw1]), b_conv_ffn=jnp.stack([dcb0, dcb1]))
    flat = jnp.concatenate([v.reshape(-1) for v in small_parts.values()])
    pad = (-flat.shape[0]) % (N_CHIPS * 32 * 128)
    small = jnp.pad(flat, (0, pad)).reshape(N_CHIPS, -1, 128)
    red0 = _reduce_to_chip([(dwin_e, BF16), (dwout_e, BF16), (small, F32)], place, tag="g0")

    full3 = _reduce_finish(red3, place, red0[-1], tag="g3")
    full2 = _reduce_finish(red2, place, full3[0], tag="g2")
    full1 = _reduce_finish(red1, place, full2[0], tag="g1")
    full0 = _reduce_finish(red0, place, full1[0], tag="g0")
    small_slots = lax.dynamic_update_index_in_dim(jnp.zeros(small.shape, F32), full0[2], chip, 0)
    small_sum = gather_chip_blocks(small_slots, name="gather_small").reshape(-1)
    grads = {
        "w_in_even": full0[0][None], "w_out_even": full0[1][None],
        "w_in_odd": full2[0][None], "w_out_odd": full2[1][None],
        "w_ffn_gate": jnp.stack([full1[0], full3[0]]), "w_ffn_up": jnp.stack([full1[1], full3[1]]),
        "w_ffn_down": jnp.stack([full1[2], full3[2]])}
    off = 0
    small_red = {}
    for nm, v in small_parts.items():
        small_red[nm] = small_sum[off:off + v.size].reshape(v.shape)
        off += v.size
    for nm in ("norm_mix", "norm_ffn", "pool_scale"):
        grads[nm] = small_red[nm].reshape(weights[nm].shape)
    grads["final_norm"] = small_red["final_norm"].reshape(weights["final_norm"].shape)
    grads["w_pool"] = small_red["w_pool"][None]
    grads["w_spatial"] = small_red["w_spatial"][None]
    grads["b_spatial"] = small_red["b_spatial"].reshape(weights["b_spatial"].shape)
    grads["b_conv_ffn"] = small_red["b_conv_ffn"].reshape(weights["b_conv_ffn"].shape)
    grads["conv_a"] = lax.dynamic_slice_in_dim(small_red["conv_a"], chip * conv_a.shape[-1], conv_a.shape[-1], 1)[None]
    grads["sgu_norm"] = lax.dynamic_slice_in_dim(small_red["sgu_norm"], chip * sgu_norm.shape[-1], sgu_norm.shape[-1], 1)
    grads["conv_ffn"] = lax.dynamic_index_in_dim(small_red["conv_ffn"], chip, 1, keepdims=False)

    deltas, new_m, new_v = {}, {}, {}
    for nm in order:
        w = weights[nm]
        w2 = w[None] if w.ndim == 1 else w
        shp = w2.shape
        d, nm_, nv_ = adamw(w2, grads[nm].reshape(shp), m_in[nm].reshape(shp), v_in[nm].reshape(shp),
                            name=f"adamw_{nm}")
        deltas[nm], new_m[nm], new_v[nm] = d.reshape(w.shape), nm_.reshape(w.shape), nv_.reshape(w.shape)

    return (loss, grad_x, *[grads[n] for n in order], *[deltas[n] for n in order],
            *[new_m[n] for n in order], *[new_v[n] for n in order])
```

```python
import jax
import jax.numpy as jnp
from jax import lax
from jax.experimental import pallas as pl
from jax.experimental.pallas import tpu as pltpu

F32 = jnp.float32
BF16 = jnp.bfloat16
MESH = pl.DeviceIdType.MESH
ANY = pl.BlockSpec(memory_space=pl.ANY)

EPS = 1e-6
POOL_WINDOWS = (2, 4, 8, 16)
GROUP = 128
CHUNK = 128
N_CHIPS = 4
N_DEV = 8
ROW_TILE = 512
HALO = 16
VMEM_LIMIT = 56 * 1024 * 1024

ADAM_LR = 0.001
ADAM_B1 = 0.9
ADAM_B2 = 0.999
ADAM_EPS = 1e-08
ADAM_WD = 0.01
ADAM_STEP = 10


def _params(*sem):
    return pltpu.CompilerParams(dimension_semantics=sem, vmem_limit_bytes=VMEM_LIMIT)


def _layer_spec(block, l, idx):
    if l is None:
        return pl.BlockSpec(block, idx)
    return pl.BlockSpec((None,) + block, lambda *g: (l,) + idx(*g))


def mm_nn(a, b, *, l=None, name):
    s, k = a.shape
    j_n, n = b.shape[-3], b.shape[-1]
    tm = min(ROW_TILE, s)

    def body(a_ref, b_ref, o_ref):
        o_ref[...] = jnp.dot(a_ref[...], b_ref[...], preferred_element_type=F32)

    return pl.pallas_call(
        body, name=name, grid=(j_n, s // tm),
        in_specs=[pl.BlockSpec((tm, k), lambda j, i: (i, 0)),
                  _layer_spec((None, k, n), l, lambda j, i: (j, 0, 0))],
        out_specs=pl.BlockSpec((None, tm, n), lambda j, i: (j, i, 0)),
        out_shape=jax.ShapeDtypeStruct((j_n, s, n), F32),
        compiler_params=_params("parallel", "parallel"))(a, b)


def mm_acc(a, b, res, *, l=None, name):
    j_n, s, kj = a.shape
    n = b.shape[-1]
    tm = min(ROW_TILE, s)

    def body(a_ref, b_ref, r_ref, o_ref):
        acc = r_ref[...]
        for j in range(j_n):
            acc = acc + jnp.dot(a_ref[j], b_ref[j], preferred_element_type=F32)
        o_ref[...] = acc

    return pl.pallas_call(
        body, name=name, grid=(s // tm,),
        in_specs=[pl.BlockSpec((j_n, tm, kj), lambda i: (0, i, 0)),
                  _layer_spec((j_n, kj, n), l, lambda i: (0, 0, 0)),
                  pl.BlockSpec((tm, n), lambda i: (i, 0))],
        out_specs=pl.BlockSpec((tm, n), lambda i: (i, 0)),
        out_shape=jax.ShapeDtypeStruct((s, n), F32),
        compiler_params=_params("parallel"))(a, b, res)


_NT = (((1,), (1,)), ((), ()))
_TN = (((0,), (0,)), ((), ()))


def mm_nt_sum(pairs, *, l=None, name):
    j_n, s, nj = pairs[0][0].shape
    k = pairs[0][1].shape[-2]
    tm = min(ROW_TILE, s)
    n_p = len(pairs)

    def body(*refs):
        o_ref = refs[-1]
        acc = jnp.zeros((tm, k), F32)
        for p in range(n_p):
            dy_ref, w_ref = refs[2 * p], refs[2 * p + 1]
            for j in range(j_n):
                acc = acc + lax.dot_general(dy_ref[j], w_ref[j], _NT, preferred_element_type=F32)
        o_ref[...] = acc

    in_specs, args = [], []
    for dy, w in pairs:
        in_specs += [pl.BlockSpec((j_n, tm, nj), lambda i: (0, i, 0)),
                     _layer_spec((j_n, k, nj), l, lambda i: (0, 0, 0))]
        args += [dy, w]
    return pl.pallas_call(
        body, name=name, grid=(s // tm,), in_specs=in_specs,
        out_specs=pl.BlockSpec((tm, k), lambda i: (i, 0)),
        out_shape=jax.ShapeDtypeStruct((s, k), F32),
        compiler_params=_params("parallel"))(*args)


def mm_nt_each(a, b, *, l=None, name, pin=None):
    s, k = a.shape
    j_n, nj = b.shape[-3], b.shape[-2]
    tm = min(ROW_TILE, s)

    def body(a_ref, b_ref, *rest):
        rest[-1][...] = lax.dot_general(a_ref[...], b_ref[...], _NT, preferred_element_type=F32)

    return pl.pallas_call(
        body, name=name, grid=(j_n, s // tm),
        in_specs=[pl.BlockSpec((tm, k), lambda j, i: (i, 0)),
                  _layer_spec((None, nj, k), l, lambda j, i: (j, 0, 0))] + ([ANY] if pin is not None else []),
        out_specs=pl.BlockSpec((None, tm, nj), lambda j, i: (j, i, 0)),
        out_shape=jax.ShapeDtypeStruct((j_n, s, nj), F32),
        compiler_params=_params("parallel", "parallel"))(a, b, *([pin] if pin is not None else []))


def mm_tn(a, dy, *, name, pin=None):
    ja, s, k = a.shape
    jd, _, n = dy.shape
    j_n = max(ja, jd)
    tk = min(ROW_TILE, s)

    def body(a_ref, d_ref, *rest):
        o_ref = rest[-1]

        @pl.when(pl.program_id(1) == 0)
        def _():
            o_ref[...] = jnp.zeros_like(o_ref)
        o_ref[...] += lax.dot_general(a_ref[...], d_ref[...], _TN, preferred_element_type=F32)

    return pl.pallas_call(
        body, name=name, grid=(j_n, s // tk),
        in_specs=[pl.BlockSpec((None, tk, k), (lambda j, i: (j, i, 0)) if ja > 1 else (lambda j, i: (0, i, 0))),
                  pl.BlockSpec((None, tk, n), (lambda j, i: (j, i, 0)) if jd > 1 else (lambda j, i: (0, i, 0)))]
        + ([ANY] if pin is not None else []),
        out_specs=pl.BlockSpec((None, k, n), lambda j, i: (j, 0, 0)),
        out_shape=jax.ShapeDtypeStruct((j_n, k, n), F32),
        compiler_params=_params("parallel", "arbitrary"))(a, dy, *([pin] if pin is not None else []))


def _back(x, k):
    return pltpu.roll(x, k, 0)


def _fwd(x, k):
    return pltpu.roll(x, x.shape[0] - k, 0)


def _causal_conv(x, w_ref):
    return w_ref[0:1, :] * _back(x, 2) + w_ref[1:2, :] * _back(x, 1) + w_ref[2:3, :] * x


def _causal_conv_t(dy, w_ref):
    return w_ref[2:3, :] * dy + w_ref[1:2, :] * _fwd(dy, 1) + w_ref[0:1, :] * _fwd(dy, 2)


def _gelu(x):
    return 0.5 * x * (1.0 + lax.erf(x * 0.7071067811865476))


def _gelu_grad(x):
    return 0.5 * (1.0 + lax.erf(x * 0.7071067811865476)) + x * jnp.exp(-0.5 * x * x) * 0.3989422804014327


def _colsum(x):
    return jnp.sum(x, axis=0, keepdims=True)


def rms_fwd(h, gain, *, name, pin=None):
    s, d = h.shape
    ts = min(ROW_TILE, s)

    def body(h_ref, g_ref, *rest):
        o_ref, r_ref = rest[-2:]
        x = h_ref[...]
        rstd = lax.rsqrt(jnp.mean(x * x, axis=-1, keepdims=True) + EPS)
        o_ref[...] = (x * rstd * g_ref[...]).astype(BF16)
        r_ref[...] = rstd

    return pl.pallas_call(
        body, name=name, grid=(s // ts,),
        in_specs=[pl.BlockSpec((ts, d), lambda i: (i, 0)), pl.BlockSpec((1, d), lambda i: (0, 0))]
        + ([ANY] if pin is not None else []),
        out_specs=[pl.BlockSpec((ts, d), lambda i: (i, 0)), pl.BlockSpec((ts, 1), lambda i: (i, 0))],
        out_shape=[jax.ShapeDtypeStruct((s, d), BF16), jax.ShapeDtypeStruct((s, 1), F32)],
        compiler_params=_params("parallel"))(h, gain, *([pin] if pin is not None else []))


def rms_bwd(dxn, h, gain, rstd, dres, *, name):
    s, d = h.shape
    ts = min(ROW_TILE, s)

    def body(dx_ref, h_ref, g_ref, r_ref, dr_ref, o_ref, ob_ref, dg_ref):
        @pl.when(pl.program_id(0) == 0)
        def _():
            dg_ref[...] = jnp.zeros_like(dg_ref)
        rstd_v = r_ref[...]
        hhat = h_ref[...] * rstd_v
        dx = dx_ref[...]
        dg_ref[...] += _colsum(dx * hhat)
        dxg = dx * g_ref[...]
        dh = dr_ref[...] + rstd_v * (dxg - hhat * jnp.mean(dxg * hhat, axis=-1, keepdims=True))
        o_ref[...] = dh
        ob_ref[...] = dh.astype(BF16)

    row = pl.BlockSpec((ts, d), lambda i: (i, 0))
    vec = pl.BlockSpec((1, d), lambda i: (0, 0))
    return pl.pallas_call(
        body, name=name, grid=(s // ts,),
        in_specs=[row, row, vec, pl.BlockSpec((ts, 1), lambda i: (i, 0)), row],
        out_specs=[row, row, vec],
        out_shape=[jax.ShapeDtypeStruct((s, d), F32), jax.ShapeDtypeStruct((s, d), BF16),
                   jax.ShapeDtypeStruct((1, d), F32)],
        compiler_params=_params("arbitrary"))(dxn, h, gain, rstd, dres)


def final_loss(h, target, gain, *, name):
    s, d = h.shape
    ts = min(ROW_TILE, s)

    def body(h_ref, t_ref, g_ref, o_ref, ob_ref, l_ref, dg_ref):
        @pl.when(pl.program_id(0) == 0)
        def _():
            l_ref[...] = jnp.zeros_like(l_ref)
            dg_ref[...] = jnp.zeros_like(dg_ref)
        x = h_ref[...]
        rstd = lax.rsqrt(jnp.mean(x * x, axis=-1, keepdims=True) + EPS)
        hhat = x * rstd
        err = hhat * g_ref[...] - t_ref[...]
        l_ref[...] += 0.5 * jnp.sum(jnp.mean(err * err, axis=-1, keepdims=True), axis=0, keepdims=True)
        dy = err * (1.0 / d)
        dg_ref[...] += _colsum(dy * hhat)
        dyg = dy * g_ref[...]
        dh = rstd * (dyg - hhat * jnp.mean(dyg * hhat, axis=-1, keepdims=True))
        o_ref[...] = dh
        ob_ref[...] = dh.astype(BF16)

    row = pl.BlockSpec((ts, d), lambda i: (i, 0))
    vec = pl.BlockSpec((1, d), lambda i: (0, 0))
    return pl.pallas_call(
        body, name=name, grid=(s // ts,),
        in_specs=[row, row, vec],
        out_specs=[row, row, pl.BlockSpec((1, 128), lambda i: (0, 0)), vec],
        out_shape=[jax.ShapeDtypeStruct((s, d), F32), jax.ShapeDtypeStruct((s, d), BF16),
                   jax.ShapeDtypeStruct((1, 128), F32), jax.ShapeDtypeStruct((1, d), F32)],
        compiler_params=_params("arbitrary"))(h, target, gain)


def _halo_specs(n_lead, ts, width, n_tiles):
    hb = ts // HALO
    prev = pl.BlockSpec((n_lead, HALO, width), lambda i: (0, jnp.maximum(i * hb - 1, 0), 0))
    nxt = pl.BlockSpec((n_lead, HALO, width), lambda i: (0, jnp.minimum((i + 1) * hb, n_tiles * hb - 1), 0))
    return prev, nxt


def _pool_fwd(z_ext, g, pos):
    w = POOL_WINDOWS[g]
    zg = z_ext[:, g * GROUP:(g + 1) * GROUP]
    acc = zg
    sh = 1
    while sh < w:
        acc = acc + _back(acc, sh)
        sh *= 2
    return acc[HALO:] / jnp.minimum(pos, float(w)) - zg[HALO:]


def even_layer_fwd(h, gain, win, conv_a, w_pool, pool_scale, wout, *, name, pin=None):
    s, d = h.shape
    w = win.shape[-1]
    ts = min(ROW_TILE, s)
    hb = ts // HALO

    def body(h_ref, hp_ref, gain_ref, win_ref, ca_ref, wp_ref, ps_ref, wout_ref, *rest):
        o_ref, xn_ref, r_ref, p_ref, m_ref = rest[-5:]
        i = pl.program_id(0)
        keep = jnp.where(i > 0, 1.0, 0.0)
        h_ext = jnp.concatenate([hp_ref[...], h_ref[...]], axis=0)
        rstd = lax.rsqrt(jnp.mean(h_ext * h_ext, axis=-1, keepdims=True) + EPS)
        xn_ext = (h_ext * rstd * gain_ref[...]).astype(BF16)
        xn_ref[...] = xn_ext[HALO:]
        r_ref[...] = rstd[HALO:]
        p32 = []
        for k in range(4):
            pk = jnp.dot(xn_ext, win_ref[k], preferred_element_type=F32).astype(BF16)
            p_ref[k] = pk[HALO:]
            pk = pk.astype(F32)
            p32.append(jnp.concatenate([pk[:HALO] * keep, pk[HALO:]], axis=0))
        m_ref[:, 0:w] = (p32[0][HALO:] * _causal_conv(p32[1] * p32[2], ca_ref)[HALO:]).astype(BF16)
        pos = (i * ts + lax.broadcasted_iota(jnp.int32, (ts, 1), 0) + 1).astype(F32)
        for g in range(len(POOL_WINDOWS)):
            pooled = _pool_fwd(p32[3], g, pos)
            mixed = jnp.dot(pooled.astype(BF16), wp_ref[g], preferred_element_type=F32)
            cols = slice(g * GROUP, (g + 1) * GROUP)
            m_ref[:, w + g * GROUP:w + (g + 1) * GROUP] = (mixed * ps_ref[:, cols]).astype(BF16)
        o_ref[...] = h_ref[...] + jnp.dot(m_ref[...], wout_ref[...], preferred_element_type=F32)

    const = lambda shape: pl.BlockSpec(shape, lambda i: (0,) * len(shape))
    row = pl.BlockSpec((ts, d), lambda i: (i, 0))
    return pl.pallas_call(
        body, name=name, grid=(s // ts,),
        in_specs=[row, pl.BlockSpec((HALO, d), lambda i: (jnp.maximum(i * hb - 1, 0), 0)), const((1, d)),
                  const((4, d, w)), const((3, w)), const((4, GROUP, GROUP)), const((1, w)), const((2 * w, d))]
        + ([ANY] if pin is not None else []),
        out_specs=[row, row, pl.BlockSpec((ts, 1), lambda i: (i, 0)),
                   pl.BlockSpec((4, ts, w), lambda i: (0, i, 0)), pl.BlockSpec((ts, 2 * w), lambda i: (i, 0))],
        out_shape=[jax.ShapeDtypeStruct((s, d), F32), jax.ShapeDtypeStruct((s, d), BF16),
                   jax.ShapeDtypeStruct((s, 1), F32), jax.ShapeDtypeStruct((4, s, w), BF16),
                   jax.ShapeDtypeStruct((s, 2 * w), BF16)],
        compiler_params=_params("parallel"))(
            h, h, gain, win, conv_a, w_pool, pool_scale, wout, *([pin] if pin is not None else []))


def even_bwd(proj, dh_bf, wout, conv_a, w_pool, w_pool_t, pool_scale, *, name, pin=None):
    _, s, w = proj.shape
    d = dh_bf.shape[1]
    ts = min(ROW_TILE, s)
    n_t = s // ts
    prev, nxt = _halo_specs(4, ts, w, n_t)
    hb = ts // HALO
    n_ext = ts + HALO

    def body(p_ref, pp_ref, pn_ref, dh_ref, dhn_ref, wout_ref, ca_ref, wp_ref, wpt_ref, ps_ref, *rest):
        dp_ref, dca_ref, dwp_ref, dps_ref = rest[-4:]
        i = pl.program_id(0)

        @pl.when(i == 0)
        def _():
            dca_ref[...] = jnp.zeros_like(dca_ref)
            dwp_ref[...] = jnp.zeros_like(dwp_ref)
            dps_ref[...] = jnp.zeros_like(dps_ref)

        keep_p = jnp.where(i > 0, 1.0, 0.0)
        keep_n = jnp.where(i < n_t - 1, 1.0, 0.0)
        dmix = lax.dot_general(jnp.concatenate([dh_ref[...], dhn_ref[...]], axis=0), wout_ref[...], _NT,
                               preferred_element_type=F32)
        a_b, a_c, a_v = (p_ref[k].astype(F32) for k in range(3))
        cv_ext = jnp.concatenate([pp_ref[1].astype(F32) * pp_ref[2].astype(F32) * keep_p, a_c * a_v], axis=0)
        dy_a = dmix[:ts, 0:w]
        dp_ref[0] = (dy_a * _causal_conv(cv_ext, ca_ref)[HALO:]).astype(BF16)
        dcc = dy_a * a_b
        dca_ref[2:3, :] += _colsum(dcc * cv_ext[HALO:])
        dca_ref[1:2, :] += _colsum(dcc * _back(cv_ext, 1)[HALO:])
        dca_ref[0:1, :] += _colsum(dcc * _back(cv_ext, 2)[HALO:])
        dcc_ext = jnp.concatenate([dcc, dmix[ts:, 0:w] * pn_ref[0].astype(F32) * keep_n], axis=0)
        dcv = _causal_conv_t(dcc_ext, ca_ref)[:ts]
        dp_ref[1] = (dcv * a_v).astype(BF16)
        dp_ref[2] = (dcv * a_c).astype(BF16)
        z_ext = jnp.concatenate([pp_ref[3].astype(F32) * keep_p, p_ref[3].astype(F32)], axis=0)
        pos = (i * ts + lax.broadcasted_iota(jnp.int32, (ts, 1), 0) + 1).astype(F32)
        pos_ext = (i * ts + lax.broadcasted_iota(jnp.int32, (n_ext, 1), 0) + 1).astype(F32)
        for g, win in enumerate(POOL_WINDOWS):
            cols = slice(g * GROUP, (g + 1) * GROUP)
            ycols = slice(w + g * GROUP, w + (g + 1) * GROUP)
            pooled = _pool_fwd(z_ext, g, pos).astype(BF16)
            mixed = jnp.dot(pooled, wp_ref[g], preferred_element_type=F32)
            dy_b = dmix[:ts, ycols]
            dps_ref[:, cols] += _colsum(dy_b * mixed)
            dmixed_ext = jnp.concatenate([dy_b, dmix[ts:, ycols] * keep_n], axis=0) * ps_ref[:, cols]
            dmixed_ext = dmixed_ext.astype(BF16)
            dwp_ref[g] += lax.dot_general(pooled, dmixed_ext[:ts], _TN, preferred_element_type=F32)
            dpooled = jnp.dot(dmixed_ext, wpt_ref[g], preferred_element_type=F32)
            acc = dpooled / jnp.minimum(pos_ext, float(win))
            sh = 1
            while sh < win:
                acc = acc + _fwd(acc, sh)
                sh *= 2
            dp_ref[3, :, cols] = (acc[:ts] - dpooled[:ts]).astype(BF16)

    tile4 = pl.BlockSpec((4, ts, w), lambda i: (0, i, 0))
    const = lambda shape: pl.BlockSpec(shape, lambda i: (0,) * len(shape))
    return pl.pallas_call(
        body, name=name, grid=(n_t,),
        in_specs=[tile4, prev, nxt, pl.BlockSpec((ts, d), lambda i: (i, 0)),
                  pl.BlockSpec((HALO, d), lambda i: (jnp.minimum((i + 1) * hb, n_t * hb - 1), 0)),
                  const((2 * w, d)), const((3, w)), const((4, GROUP, GROUP)), const((4, GROUP, GROUP)), const((1, w))]
        + ([ANY] if pin is not None else []),
        out_specs=[tile4, const((3, w)), const((4, GROUP, GROUP)), const((1, w))],
        out_shape=[jax.ShapeDtypeStruct((4, s, w), BF16), jax.ShapeDtypeStruct((3, w), F32),
                   jax.ShapeDtypeStruct((4, GROUP, GROUP), F32), jax.ShapeDtypeStruct((1, w), F32)],
        compiler_params=_params("arbitrary"))(
            proj, proj, proj, dh_bf, dh_bf, wout, conv_a, w_pool, w_pool_t, pool_scale,
            *([pin] if pin is not None else []))


def _ffn_halo(ts, f, n_t, nxt):
    hb = ts // HALO
    if nxt:
        return pl.BlockSpec((None, HALO, f), lambda j, i: (j, jnp.minimum((i + 1) * hb, n_t * hb - 1), 0))
    return pl.BlockSpec((None, HALO, f), lambda j, i: (j, jnp.maximum(i * hb - 1, 0), 0))


def ffn_act_fwd(g, up, cw, cb, *, name):
    j_n, s, f = g.shape
    ts = min(ROW_TILE, s)
    n_t = s // ts

    def body(g_ref, gp_ref, u_ref, cw_ref, cb_ref, o_ref):
        keep = jnp.where(pl.program_id(1) > 0, 1.0, 0.0)
        g_ext = jnp.concatenate([gp_ref[...] * keep, g_ref[...]], axis=0)
        gc = _causal_conv(g_ext, cw_ref)[HALO:] + cb_ref[...]
        o_ref[...] = (gc * jax.nn.sigmoid(gc) * u_ref[...]).astype(BF16)

    tile = pl.BlockSpec((None, ts, f), lambda j, i: (j, i, 0))
    return pl.pallas_call(
        body, name=name, grid=(j_n, n_t),
        in_specs=[tile, _ffn_halo(ts, f, n_t, False), tile,
                  pl.BlockSpec((None, 3, f), lambda j, i: (j, 0, 0)),
                  pl.BlockSpec((None, 1, f), lambda j, i: (j, 0, 0))],
        out_specs=tile,
        out_shape=jax.ShapeDtypeStruct((j_n, s, f), BF16),
        compiler_params=_params("parallel", "parallel"))(g, g, up, cw, cb)


def ffn_act_bwd(g, up, dact, cw, cb, *, name):
    j_n, s, f = g.shape
    ts = min(ROW_TILE, s)
    n_t = s // ts

    def body(g_ref, gp_ref, gn_ref, u_ref, un_ref, d_ref, dn_ref, cw_ref, cb_ref,
             dg_ref, du_ref, dcw_ref, dcb_ref):
        i = pl.program_id(1)

        @pl.when(i == 0)
        def _():
            dcw_ref[...] = jnp.zeros_like(dcw_ref)
            dcb_ref[...] = jnp.zeros_like(dcb_ref)

        keep_p = jnp.where(i > 0, 1.0, 0.0)
        keep_n = jnp.where(i < n_t - 1, 1.0, 0.0)
        g_ext = jnp.concatenate([gp_ref[...] * keep_p, g_ref[...], gn_ref[...]], axis=0)
        gc = _causal_conv(g_ext, cw_ref)[HALO:] + cb_ref[...]
        sig = jax.nn.sigmoid(gc)
        dact_ext = jnp.concatenate([d_ref[...], dn_ref[...] * keep_n], axis=0)
        du_ref[...] = (dact_ext * gc * sig)[:ts].astype(BF16)
        up_ext = jnp.concatenate([u_ref[...], un_ref[...]], axis=0)
        dgc = dact_ext * up_ext * (sig * (1.0 + gc * (1.0 - sig)))
        dg_ref[...] = _causal_conv_t(dgc, cw_ref)[:ts].astype(BF16)
        dgc_t = dgc[:ts]
        dcb_ref[...] += _colsum(dgc_t)
        dcw_ref[2:3, :] += _colsum(dgc_t * g_ext[HALO:HALO + ts])
        dcw_ref[1:2, :] += _colsum(dgc_t * _back(g_ext, 1)[HALO:HALO + ts])
        dcw_ref[0:1, :] += _colsum(dgc_t * _back(g_ext, 2)[HALO:HALO + ts])

    tile = pl.BlockSpec((None, ts, f), lambda j, i: (j, i, 0))
    prev, nxt = _ffn_halo(ts, f, n_t, False), _ffn_halo(ts, f, n_t, True)
    return pl.pallas_call(
        body, name=name, grid=(j_n, n_t),
        in_specs=[tile, prev, nxt, tile, nxt, tile, nxt,
                  pl.BlockSpec((None, 3, f), lambda j, i: (j, 0, 0)),
                  pl.BlockSpec((None, 1, f), lambda j, i: (j, 0, 0))],
        out_specs=[tile, tile, pl.BlockSpec((None, 3, f), lambda j, i: (j, 0, 0)),
                   pl.BlockSpec((None, 1, f), lambda j, i: (j, 0, 0))],
        out_shape=[jax.ShapeDtypeStruct((j_n, s, f), BF16), jax.ShapeDtypeStruct((j_n, s, f), BF16),
                   jax.ShapeDtypeStruct((j_n, 3, f), F32), jax.ShapeDtypeStruct((j_n, 1, f), F32)],
        compiler_params=_params("parallel", "arbitrary"))(g, g, g, up, up, dact, dact, cw, cb)


def ffn_in_fwd(hn, wg, wu, cw, cb, *, name):
    s, d = hn.shape
    j_n, _, f = wg.shape
    tm = min(ROW_TILE, s)
    hb = tm // HALO

    def body(x_ref, xp_ref, wg_ref, wu_ref, cw_ref, cb_ref, g_ref, u_ref, a_ref):
        i, j = pl.program_id(0), pl.program_id(1)
        x_ext = jnp.concatenate([xp_ref[...], x_ref[...]], axis=0)
        g_ext = jnp.dot(x_ext, wg_ref[j], preferred_element_type=F32).astype(BF16)
        up = jnp.dot(x_ref[...], wu_ref[j], preferred_element_type=F32).astype(BF16)
        g_ref[...] = g_ext[HALO:]
        u_ref[...] = up
        keep = jnp.where(i > 0, 1.0, 0.0)
        g32 = jnp.concatenate([g_ext[:HALO].astype(F32) * keep, g_ext[HALO:].astype(F32)], axis=0)
        gc = _causal_conv(g32, cw_ref)[HALO:] + cb_ref[...]
        a_ref[...] = (gc * jax.nn.sigmoid(gc) * up.astype(F32)).astype(BF16)

    whole = pl.BlockSpec((j_n, d, f), lambda i, j: (0, 0, 0))
    tile = pl.BlockSpec((None, tm, f), lambda i, j: (j, i, 0))
    shape = jax.ShapeDtypeStruct((j_n, s, f), BF16)
    return pl.pallas_call(
        body, name=name, grid=(s // tm, j_n),
        in_specs=[pl.BlockSpec((tm, d), lambda i, j: (i, 0)),
                  pl.BlockSpec((HALO, d), lambda i, j: (jnp.maximum(i * hb - 1, 0), 0)),
                  whole, whole,
                  pl.BlockSpec((None, 3, f), lambda i, j: (j, 0, 0)),
                  pl.BlockSpec((None, 1, f), lambda i, j: (j, 0, 0))],
        out_specs=[tile, tile, tile], out_shape=[shape, shape, shape],
        compiler_params=_params("parallel", "parallel"))(hn, hn, wg, wu, cw, cb)


def ffn_bwd_a(dh_bf, hn, g, up, wd, cw, cb, *, name, pin=None):
    s, d = hn.shape
    j_n, _, f = g.shape
    tm = min(ROW_TILE, s)
    n_t = s // tm
    hb = tm // HALO

    def body(dh_ref, dhn_ref, x_ref, g_ref, gp_ref, gn_ref, u_ref, un_ref, wd_ref, cw_ref, cb_ref, *rest):
        dg_ref, du_ref, dwg_ref, dwu_ref, dwd_ref, dcw_ref, dcb_ref = rest[-7:]
        i = pl.program_id(1)

        @pl.when(i == 0)
        def _():
            for r in (dwg_ref, dwu_ref, dwd_ref, dcw_ref, dcb_ref):
                r[...] = jnp.zeros_like(r)

        keep_p = jnp.where(i > 0, 1.0, 0.0)
        keep_n = jnp.where(i < n_t - 1, 1.0, 0.0)
        dh = dh_ref[...]
        dact = lax.dot_general(jnp.concatenate([dh, dhn_ref[...]], axis=0), wd_ref[...], _NT,
                               preferred_element_type=F32)
        dact = jnp.concatenate([dact[:tm], dact[tm:] * keep_n], axis=0)
        g_ext = jnp.concatenate([gp_ref[...].astype(F32) * keep_p, g_ref[...].astype(F32),
                                 gn_ref[...].astype(F32)], axis=0)
        gc = _causal_conv(g_ext, cw_ref)[HALO:] + cb_ref[...]
        sig = jax.nn.sigmoid(gc)
        silu = gc * sig
        up_ext = jnp.concatenate([u_ref[...], un_ref[...]], axis=0).astype(F32)
        act = (silu * up_ext)[:tm].astype(BF16)
        dwd_ref[...] += lax.dot_general(act, dh, _TN, preferred_element_type=F32)
        dup = (dact * silu)[:tm].astype(BF16)
        du_ref[...] = dup
        dgc = dact * up_ext * (sig * (1.0 + gc * (1.0 - sig)))
        dg = _causal_conv_t(dgc, cw_ref)[:tm].astype(BF16)
        dg_ref[...] = dg
        x = x_ref[...]
        dwg_ref[...] += lax.dot_general(x, dg, _TN, preferred_element_type=F32)
        dwu_ref[...] += lax.dot_general(x, dup, _TN, preferred_element_type=F32)
        dgc_t = dgc[:tm]
        dcb_ref[...] += _colsum(dgc_t)
        dcw_ref[2:3, :] += _colsum(dgc_t * g_ext[HALO:HALO + tm])
        dcw_ref[1:2, :] += _colsum(dgc_t * _back(g_ext, 1)[HALO:HALO + tm])
        dcw_ref[0:1, :] += _colsum(dgc_t * _back(g_ext, 2)[HALO:HALO + tm])

    rows = pl.BlockSpec((tm, d), lambda j, i: (i, 0))
    rows_next = pl.BlockSpec((HALO, d), lambda j, i: (jnp.minimum((i + 1) * hb, n_t * hb - 1), 0))
    tile = pl.BlockSpec((None, tm, f), lambda j, i: (j, i, 0))
    prev, nxt = _ffn_halo(tm, f, n_t, False), _ffn_halo(tm, f, n_t, True)
    per_j = lambda r, c: pl.BlockSpec((None, r, c), lambda j, i: (j, 0, 0))
    return pl.pallas_call(
        body, name=name, grid=(j_n, n_t),
        in_specs=[rows, rows_next, rows, tile, prev, nxt, tile, nxt, per_j(f, d), per_j(3, f), per_j(1, f)]
        + ([ANY] if pin is not None else []),
        out_specs=[tile, tile, per_j(d, f), per_j(d, f), per_j(f, d), per_j(3, f), per_j(1, f)],
        out_shape=[jax.ShapeDtypeStruct((j_n, s, f), BF16), jax.ShapeDtypeStruct((j_n, s, f), BF16),
                   jax.ShapeDtypeStruct((j_n, d, f), F32), jax.ShapeDtypeStruct((j_n, d, f), F32),
                   jax.ShapeDtypeStruct((j_n, f, d), F32), jax.ShapeDtypeStruct((j_n, 3, f), F32),
                   jax.ShapeDtypeStruct((j_n, 1, f), F32)],
        compiler_params=_params("parallel", "arbitrary"))(
            dh_bf, dh_bf, hn, g, g, g, up, up, wd, cw, cb, *([pin] if pin is not None else []))


def dx_rms_bwd(pairs, h, gain, rstd, dres, *, name):
    j_n, s, f = pairs[0][0].shape
    d = h.shape[1]
    tm = min(ROW_TILE // 2, s)
    n_p = len(pairs)

    def body(*refs):
        dy_refs, w_refs = refs[:n_p], refs[n_p:2 * n_p]
        h_ref, g_ref, r_ref, dr_ref, o_ref, ob_ref, dgain_ref = refs[2 * n_p:]

        @pl.when(pl.program_id(0) == 0)
        def _():
            dgain_ref[...] = jnp.zeros_like(dgain_ref)
        dx = jnp.zeros((tm, d), F32)
        for j in range(j_n):
            for p in range(n_p):
                dx = dx + lax.dot_general(dy_refs[p][j], w_refs[p][j], _NT, preferred_element_type=F32)
        rstd_v = r_ref[...]
        hhat = h_ref[...] * rstd_v
        dgain_ref[...] += _colsum(dx * hhat)
        dxg = dx * g_ref[...]
        dh = dr_ref[...] + rstd_v * (dxg - hhat * jnp.mean(dxg * hhat, axis=-1, keepdims=True))
        o_ref[...] = dh
        ob_ref[...] = dh.astype(BF16)

    tile4 = pl.BlockSpec((j_n, tm, f), lambda i: (0, i, 0))
    whole = pl.BlockSpec((j_n, d, f), lambda i: (0, 0, 0))
    row = pl.BlockSpec((tm, d), lambda i: (i, 0))
    vec = pl.BlockSpec((1, d), lambda i: (0, 0))
    return pl.pallas_call(
        body, name=name, grid=(s // tm,),
        in_specs=[tile4] * n_p + [whole] * n_p + [row, vec, pl.BlockSpec((tm, 1), lambda i: (i, 0)), row],
        out_specs=[row, row, vec],
        out_shape=[jax.ShapeDtypeStruct((s, d), F32), jax.ShapeDtypeStruct((s, d), BF16),
                   jax.ShapeDtypeStruct((1, d), F32)],
        compiler_params=_params("arbitrary"))(*[p[0] for p in pairs], *[p[1] for p in pairs], h, gain, rstd, dres)


def _sgu_gate(vn_bf, ws_ref, bs_ref, h, rows):
    tri = lax.broadcasted_iota(jnp.int32, (CHUNK, CHUNK), 0) >= lax.broadcasted_iota(jnp.int32, (CHUNK, CHUNK), 1)
    ws = jnp.where(tri, ws_ref[h], 0.0).astype(BF16)
    cols = slice((h % 4) * GROUP, (h % 4 + 1) * GROUP)
    return ws, jnp.dot(ws, vn_bf[h // 4][rows, cols], preferred_element_type=F32) + bs_ref[h]


def odd_layer_fwd(h, gain, win, sgu_norm, w_spatial, b_spatial, wout, *, name):
    s, d = h.shape
    w = win.shape[-1]
    ts = min(ROW_TILE, s)
    n_heads = w_spatial.shape[0]

    def body(h_ref, gain_ref, win_ref, n_ref, ws_ref, bs_ref, wout_ref, o_ref, xn_ref, r_ref, p_ref, m_ref, rv_ref):
        x = h_ref[...]
        rstd_x = lax.rsqrt(jnp.mean(x * x, axis=-1, keepdims=True) + EPS)
        xn = (x * rstd_x * gain_ref[...]).astype(BF16)
        xn_ref[...] = xn
        r_ref[...] = rstd_x
        for k in range(4):
            p_ref[k] = jnp.dot(xn, win_ref[k], preferred_element_type=F32).astype(BF16)
        v = [_gelu(p_ref[2].astype(F32)), _gelu(p_ref[3].astype(F32))]
        ms = (jnp.sum(v[0] * v[0], axis=-1, keepdims=True) + jnp.sum(v[1] * v[1], axis=-1, keepdims=True)) / (2 * w)
        rstd = lax.rsqrt(ms + EPS)
        rv_ref[...] = rstd
        vn = [(v[k] * rstd * n_ref[:, k * w:(k + 1) * w]).astype(BF16) for k in range(2)]
        for hd in range(n_heads):
            cols = slice((hd % 4) * GROUP, (hd % 4 + 1) * GROUP)
            for c in range(ts // CHUNK):
                rows = slice(c * CHUNK, (c + 1) * CHUNK)
                _, gate = _sgu_gate(vn, ws_ref, bs_ref, hd, rows)
                u = _gelu(p_ref[hd // 4, rows, cols].astype(F32))
                m_ref[rows, hd * GROUP:(hd + 1) * GROUP] = (u * gate).astype(BF16)
        o_ref[...] = x + jnp.dot(m_ref[...], wout_ref[...], preferred_element_type=F32)

    const = lambda shape: pl.BlockSpec(shape, lambda i: (0,) * len(shape))
    row = pl.BlockSpec((ts, d), lambda i: (i, 0))
    col1 = pl.BlockSpec((ts, 1), lambda i: (i, 0))
    return pl.pallas_call(
        body, name=name, grid=(s // ts,),
        in_specs=[row, const((1, d)), const((4, d, w)), const((1, 2 * w)),
                  const((n_heads, CHUNK, CHUNK)), const((n_heads, CHUNK, 1)), const((2 * w, d))],
        out_specs=[row, row, col1, pl.BlockSpec((4, ts, w), lambda i: (0, i, 0)),
                   pl.BlockSpec((ts, 2 * w), lambda i: (i, 0)), col1],
        out_shape=[jax.ShapeDtypeStruct((s, d), F32), jax.ShapeDtypeStruct((s, d), BF16),
                   jax.ShapeDtypeStruct((s, 1), F32), jax.ShapeDtypeStruct((4, s, w), BF16),
                   jax.ShapeDtypeStruct((s, 2 * w), BF16), jax.ShapeDtypeStruct((s, 1), F32)],
        compiler_params=_params("parallel"))(h, gain, win, sgu_norm, w_spatial, b_spatial, wout)


def sgu_bwd(p, dh_bf, wout, rstd, sgu_norm, w_spatial, b_spatial, *, name, pin=None):
    _, s, w = p.shape
    d = dh_bf.shape[1]
    ts = min(ROW_TILE, s)
    n_heads = w_spatial.shape[0]

    def body(p_ref, dh_ref, wout_ref, r_ref, n_ref, ws_ref, bs_ref, *rest):
        dp_ref, dn_ref, dws_ref, dbs_ref, dvn_ref, dm_ref = rest[-6:]

        @pl.when(pl.program_id(0) == 0)
        def _():
            dn_ref[...] = jnp.zeros_like(dn_ref)
            dws_ref[...] = jnp.zeros_like(dws_ref)
            dbs_ref[...] = jnp.zeros_like(dbs_ref)

        dm_ref[...] = lax.dot_general(dh_ref[...], wout_ref[...], _NT, preferred_element_type=F32)
        rstd_v = r_ref[...]
        vhat = [_gelu(p_ref[2 + k].astype(F32)) * rstd_v for k in range(2)]
        vn = [(vhat[k] * n_ref[:, k * w:(k + 1) * w]).astype(BF16) for k in range(2)]
        tri = lax.broadcasted_iota(jnp.int32, (CHUNK, CHUNK), 0) >= lax.broadcasted_iota(jnp.int32, (CHUNK, CHUNK), 1)
        for h in range(n_heads):
            cols = slice((h % 4) * GROUP, (h % 4 + 1) * GROUP)
            ocols = slice(h * GROUP, (h + 1) * GROUP)
            for c in range(ts // CHUNK):
                rows = slice(c * CHUNK, (c + 1) * CHUNK)
                ws, gate = _sgu_gate(vn, ws_ref, bs_ref, h, rows)
                pu = p_ref[h // 4, rows, cols].astype(F32)
                dm = dm_ref[rows, ocols]
                dp_ref[h // 4, rows, cols] = (dm * gate * _gelu_grad(pu)).astype(BF16)
                dgate = dm * _gelu(pu)
                dbs_ref[h] += jnp.sum(dgate, axis=-1, keepdims=True)
                dgate_bf = dgate.astype(BF16)
                dws = lax.dot_general(dgate_bf, vn[h // 4][rows, cols], _NT, preferred_element_type=F32)
                dws_ref[h] += jnp.where(tri, dws, 0.0)
                dvn_ref[rows, ocols] = lax.dot_general(ws, dgate_bf, _TN, preferred_element_type=F32)
        for k in range(2):
            kc = slice(k * w, (k + 1) * w)
            dvn = dvn_ref[:, kc]
            dn_ref[:, kc] += _colsum(dvn * vhat[k])
        dvh = [dvn_ref[:, k * w:(k + 1) * w] * n_ref[:, k * w:(k + 1) * w] for k in range(2)]
        dot = (jnp.sum(dvh[0] * vhat[0], axis=-1, keepdims=True)
               + jnp.sum(dvh[1] * vhat[1], axis=-1, keepdims=True)) / (2 * w)
        for k in range(2):
            dv = rstd_v * (dvh[k] - vhat[k] * dot)
            dp_ref[2 + k] = (dv * _gelu_grad(p_ref[2 + k].astype(F32))).astype(BF16)

    const = lambda shape: pl.BlockSpec(shape, lambda i: (0,) * len(shape))
    tile4 = pl.BlockSpec((4, ts, w), lambda i: (0, i, 0))
    return pl.pallas_call(
        body, name=name, grid=(s // ts,),
        in_specs=[tile4, pl.BlockSpec((ts, d), lambda i: (i, 0)), const((2 * w, d)),
                  pl.BlockSpec((ts, 1), lambda i: (i, 0)),
                  const((1, 2 * w)), const((n_heads, CHUNK, CHUNK)), const((n_heads, CHUNK, 1))]
        + ([ANY] if pin is not None else []),
        out_specs=[tile4, const((1, 2 * w)), const((n_heads, CHUNK, CHUNK)), const((n_heads, CHUNK, 1))],
        out_shape=[jax.ShapeDtypeStruct((4, s, w), BF16), jax.ShapeDtypeStruct((1, 2 * w), F32),
                   jax.ShapeDtypeStruct((n_heads, CHUNK, CHUNK), F32),
                   jax.ShapeDtypeStruct((n_heads, CHUNK, 1), F32)],
        scratch_shapes=[pltpu.VMEM((ts, 2 * w), F32), pltpu.VMEM((ts, 2 * w), F32)],
        compiler_params=_params("arbitrary"))(
            p, dh_bf, wout, rstd, sgu_norm, w_spatial, b_spatial, *([pin] if pin is not None else []))


def _row_tile(rows):
    if rows <= ROW_TILE:
        return rows
    for t in (512, 384, 352, 256, 128, 64, 32, 16, 8):
        if rows % t == 0:
            return t
    return rows


def adamw(w, g, m, v, *, name):
    shape = w.shape
    cols = shape[-1]
    rows = w.size // cols
    w2, g2, m2, v2 = (a.reshape(rows, cols) for a in (w, g, m, v))
    tr = _row_tile(rows)
    bc1 = 1.0 - ADAM_B1 ** ADAM_STEP
    bc2 = 1.0 - ADAM_B2 ** ADAM_STEP

    def body(w_ref, g_ref, m_ref, v_ref, d_ref, nm_ref, nv_ref):
        grad = g_ref[...]
        m_new = ADAM_B1 * m_ref[...] + (1.0 - ADAM_B1) * grad
        v_new = ADAM_B2 * v_ref[...] + (1.0 - ADAM_B2) * (grad * grad)
        nm_ref[...] = m_new
        nv_ref[...] = v_new
        d_ref[...] = -ADAM_LR * ((m_new / bc1) / (jnp.sqrt(v_new / bc2) + ADAM_EPS) + ADAM_WD * w_ref[...])

    spec = pl.BlockSpec((tr, cols), lambda i: (i, 0))
    outs = pl.pallas_call(
        body, name=name, grid=(rows // tr,),
        in_specs=[spec] * 4, out_specs=[spec] * 3,
        out_shape=[jax.ShapeDtypeStruct((rows, cols), F32)] * 3,
        compiler_params=_params("parallel"))(w2, g2, m2, v2)
    return tuple(o.reshape(shape) for o in outs)


def _place():
    return lax.axis_index("x"), lax.axis_index("y"), lax.axis_index("c")


def _other_chips(x, y):
    return [(1 - x, y), (x, 1 - y), (1 - x, 1 - y)]


HBM = pl.BlockSpec(memory_space=pltpu.HBM)
SEM = pl.BlockSpec(memory_space=pltpu.SEMAPHORE)
DATAFLOW = pltpu.SideEffectType.DATAFLOW_SIDE_EFFECTING


def _in_hbm(a):
    return pltpu.with_memory_space_constraint(a, pltpu.HBM)


def cast_into_slot(w, chip, *, l=None, name):
    rows, cols = w.shape[-2:]
    tr = _row_tile(rows)

    def body(chip_ref, w_ref, o_ref):
        o_ref[...] = w_ref[...].astype(BF16)

    in_spec = (pl.BlockSpec((tr, cols), lambda i, chip_ref: (i, 0)) if l is None
               else pl.BlockSpec((None, tr, cols), lambda i, chip_ref: (l, i, 0)))
    return pl.pallas_call(
        body, name=name,
        grid_spec=pltpu.PrefetchScalarGridSpec(
            num_scalar_prefetch=1, grid=(rows // tr,), in_specs=[in_spec],
            out_specs=pl.BlockSpec((None, tr, cols), lambda i, chip_ref: (chip_ref[0], i, 0))),
        out_shape=jax.ShapeDtypeStruct((N_CHIPS, rows, cols), BF16),
        compiler_params=_params("parallel"))(chip, w)


def _half(ref, slot, c):
    half = ref.shape[1] // 2
    return ref.at[slot, pl.ds(c * half, half), :]


def gather_start(groups, smalls):
    flat = [b for g in groups for b in g]
    n_b, n_s, n_g = len(flat), len(smalls), len(groups)

    def body(*refs):
        bufs, small_refs = refs[:n_b], refs[n_b:n_b + n_s]
        sems = refs[n_b + n_s:n_b + n_s + 2 * n_g + 2]
        token = refs[-1]
        x, y, c = _place()
        me = 2 * x + y
        chips = _other_chips(x, y)
        for si in range(n_s):
            piece = small_refs[si].at[me]
            for k, (px, py) in enumerate(chips):
                pltpu.make_async_remote_copy(
                    src_ref=piece, dst_ref=piece,
                    send_sem=sems[2 * n_g].at[3 * si + k], recv_sem=sems[2 * n_g + 1].at[3 * si + k],
                    device_id=(px, py, c), device_id_type=MESH).start()
        t = 0
        for gi, group in enumerate(groups):
            for ti in range(len(group)):
                piece = _half(bufs[t], me, c)
                t += 1
                for k, (px, py) in enumerate(chips):
                    pltpu.make_async_remote_copy(
                        src_ref=piece, dst_ref=piece,
                        send_sem=sems[2 * gi].at[3 * ti + k], recv_sem=sems[2 * gi + 1].at[3 * ti + k],
                        device_id=(px, py, c), device_id_type=MESH).start()
        token[...] = jnp.zeros_like(token)

    sem_shapes = []
    for group in groups:
        sem_shapes += [pltpu.SemaphoreType.DMA((3 * len(group),))] * 2
    sem_shapes += [pltpu.SemaphoreType.DMA((3 * n_s,))] * 2
    arrays = flat + list(smalls)
    n_sem = len(sem_shapes)
    res = pl.pallas_call(
        body, name="gather_start",
        out_shape=tuple(sem_shapes) + tuple(pltpu.HBM(a.shape, a.dtype) for a in arrays)
        + (jax.ShapeDtypeStruct((8, 128), F32),),
        in_specs=[HBM] * len(arrays),
        out_specs=tuple([SEM] * n_sem + [HBM] * len(arrays) + [pl.BlockSpec(memory_space=pltpu.VMEM)]),
        input_output_aliases={i: n_sem + i for i in range(len(arrays))},
        compiler_params=pltpu.CompilerParams(has_side_effects=DATAFLOW))(*[_in_hbm(a) for a in arrays])
    sems, thru, token = res[:n_sem], res[n_sem:-1], res[-1]
    out_groups, t = [], 0
    for group in groups:
        out_groups.append(list(thru[t:t + len(group)]))
        t += len(group)
    return sems, out_groups, list(thru[n_b:]), token


def gather_wait(bufs, send, recv, after, *, name, smalls=(), small_send=None, small_recv=None):
    n_b, n_s = len(bufs), len(smalls)
    arrays = list(bufs) + list(smalls)
    sem_ops = [send, recv] + ([small_send, small_recv] if n_s else [])

    def body(*refs):
        buf_refs, small_refs = refs[:n_b], refs[n_b:n_b + n_s]
        sems = refs[n_b + n_s:n_b + n_s + len(sem_ops)]
        x, y, c = _place()
        me = 2 * x + y
        chips = _other_chips(x, y)
        for ti in range(n_b):
            for k, (px, py) in enumerate(chips):
                done = pltpu.make_async_remote_copy(
                    src_ref=_half(buf_refs[ti], me, c), dst_ref=_half(buf_refs[ti], 2 * px + py, c),
                    send_sem=sems[0].at[3 * ti + k], recv_sem=sems[1].at[3 * ti + k],
                    device_id=(px, py, c), device_id_type=MESH)
                done.wait_send()
                done.wait_recv()
        for si in range(n_s):
            for k, (px, py) in enumerate(chips):
                done = pltpu.make_async_remote_copy(
                    src_ref=small_refs[si].at[me], dst_ref=small_refs[si].at[2 * px + py],
                    send_sem=sems[2].at[3 * si + k], recv_sem=sems[3].at[3 * si + k],
                    device_id=(px, py, c), device_id_type=MESH)
                done.wait_send()
                done.wait_recv()

    res = pl.pallas_call(
        body, name=name,
        out_shape=tuple(pltpu.HBM(a.shape, a.dtype) for a in arrays),
        in_specs=[HBM] * len(arrays) + [SEM] * len(sem_ops) + [ANY],
        out_specs=tuple([HBM] * len(arrays)),
        input_output_aliases={i: i for i in range(len(arrays))},
        compiler_params=pltpu.CompilerParams(has_side_effects=DATAFLOW))(*arrays, *sem_ops, after)
    return list(res[:n_b]), list(res[n_b:])


def gather_forward(bufs, *, name):
    n = len(bufs)

    def body(*refs):
        ins, outs = refs[:n], refs[n:2 * n]
        send_sems, recv_sems = refs[2 * n:]
        x, y, c = _place()
        chips = _other_chips(x, y)
        for t in range(n):
            for k, (px, py) in enumerate(chips):
                pltpu.make_async_remote_copy(
                    src_ref=_half(ins[t], 2 * px + py, c), dst_ref=_half(outs[t], 2 * px + py, c),
                    send_sem=send_sems.at[3 * t + k], recv_sem=recv_sems.at[3 * t + k],
                    device_id=(x, y, 1 - c), device_id_type=MESH).start()
        for t in range(n):
            for k, (px, py) in enumerate(chips):
                done = pltpu.make_async_remote_copy(
                    src_ref=_half(ins[t], 2 * px + py, c), dst_ref=_half(outs[t], 2 * px + py, 1 - c),
                    send_sem=send_sems.at[3 * t + k], recv_sem=recv_sems.at[3 * t + k],
                    device_id=(x, y, 1 - c), device_id_type=MESH)
                done.wait_send()
                done.wait_recv()

    return pl.pallas_call(
        body, name=name, in_specs=[ANY] * n, out_specs=[ANY] * n,
        out_shape=[jax.ShapeDtypeStruct(a.shape, a.dtype) for a in bufs],
        input_output_aliases={i: i for i in range(n)},
        scratch_shapes=[pltpu.SemaphoreType.DMA((3 * n,)), pltpu.SemaphoreType.DMA((3 * n,))],
        compiler_params=pltpu.CompilerParams(has_side_effects=True))(*bufs)


def reduce_stage_a(grads_bf, *, name):
    n = len(grads_bf)

    def body(*refs):
        srcs, outs = refs[:n], refs[n:2 * n]
        send_sems, recv_sems = refs[2 * n:]
        x, y, c = _place()
        for t in range(n):
            half = srcs[t].shape[1] // 2
            pltpu.make_async_remote_copy(
                src_ref=srcs[t].at[:, pl.ds((1 - c) * half, half), :], dst_ref=outs[t],
                send_sem=send_sems.at[t], recv_sem=recv_sems.at[t],
                device_id=(x, y, 1 - c), device_id_type=MESH).start()
        for t in range(n):
            half = srcs[t].shape[1] // 2
            done = pltpu.make_async_remote_copy(
                src_ref=srcs[t].at[:, pl.ds((1 - c) * half, half), :], dst_ref=outs[t],
                send_sem=send_sems.at[t], recv_sem=recv_sems.at[t],
                device_id=(x, y, 1 - c), device_id_type=MESH)
            done.wait_send()
            done.wait_recv()

    out_shape = [jax.ShapeDtypeStruct((a.shape[0], a.shape[1] // 2, a.shape[2]), a.dtype) for a in grads_bf]
    return pl.pallas_call(
        body, name=name, in_specs=[ANY] * n, out_specs=[ANY] * n, out_shape=out_shape,
        scratch_shapes=[pltpu.SemaphoreType.DMA((n,)), pltpu.SemaphoreType.DMA((n,))],
        compiler_params=pltpu.CompilerParams(has_side_effects=True))(*grads_bf)


def sum_stage_a(grad, recv, place, wire, *, name):
    j_n, half, cols = recv.shape

    def body(place_ref, g_ref, r_ref, o_ref, ob_ref):
        acc = g_ref[...] + r_ref[...]
        o_ref[...] = acc
        ob_ref[...] = acc.astype(wire)

    blk = (None, half, cols)
    return pl.pallas_call(
        body, name=name,
        grid_spec=pltpu.PrefetchScalarGridSpec(
            num_scalar_prefetch=1, grid=(j_n,),
            in_specs=[pl.BlockSpec(blk, lambda j, place_ref: (j, place_ref[1], 0)),
                      pl.BlockSpec(blk, lambda j, place_ref: (j, 0, 0))],
            out_specs=[pl.BlockSpec(blk, lambda j, place_ref: (j, 0, 0))] * 2),
        out_shape=[jax.ShapeDtypeStruct(recv.shape, F32), jax.ShapeDtypeStruct(recv.shape, wire)],
        compiler_params=_params("parallel"))(place, grad, recv)


def reduce_b_start(parts_bf, *, name):
    n = len(parts_bf)
    lands = [lax.empty((3,) + a.shape[1:], a.dtype) for a in parts_bf]

    def body(*refs):
        srcs, land = refs[:n], refs[n:2 * n]
        send_sems, recv_sems = refs[2 * n], refs[2 * n + 1]
        token = refs[-1]
        x, y, c = _place()
        chips = _other_chips(x, y)
        for t in range(n):
            for k, (px, py) in enumerate(chips):
                pltpu.make_async_remote_copy(
                    src_ref=srcs[t].at[2 * px + py], dst_ref=land[t].at[k],
                    send_sem=send_sems.at[3 * t + k], recv_sem=recv_sems.at[3 * t + k],
                    device_id=(px, py, c), device_id_type=MESH).start()
        token[...] = jnp.zeros_like(token)

    arrays = list(parts_bf) + lands
    res = pl.pallas_call(
        body, name=name,
        out_shape=(pltpu.SemaphoreType.DMA((3 * n,)), pltpu.SemaphoreType.DMA((3 * n,)))
        + tuple(pltpu.HBM(a.shape, a.dtype) for a in arrays) + (jax.ShapeDtypeStruct((8, 128), F32),),
        in_specs=[HBM] * (2 * n),
        out_specs=tuple([SEM, SEM] + [HBM] * (2 * n) + [pl.BlockSpec(memory_space=pltpu.VMEM)]),
        input_output_aliases={i: 2 + i for i in range(2 * n)},
        compiler_params=pltpu.CompilerParams(has_side_effects=DATAFLOW))(*[_in_hbm(a) for a in arrays])
    return res[0], res[1], list(res[2:2 + n]), list(res[2 + n:2 + 2 * n]), res[-1]


def reduce_b_wait(srcs, lands, send, recv, after, *, name):
    n = len(srcs)

    def body(*refs):
        src_refs, land = refs[:n], refs[n:2 * n]
        send_sems, recv_sems = refs[2 * n], refs[2 * n + 1]
        x, y, c = _place()
        chips = _other_chips(x, y)
        for t in range(n):
            for k, (px, py) in enumerate(chips):
                done = pltpu.make_async_remote_copy(
                    src_ref=src_refs[t].at[2 * px + py], dst_ref=land[t].at[k],
                    send_sem=send_sems.at[3 * t + k], recv_sem=recv_sems.at[3 * t + k],
                    device_id=(px, py, c), device_id_type=MESH)
                done.wait_send()
                done.wait_recv()

    arrays = list(srcs) + list(lands)
    res = pl.pallas_call(
        body, name=name,
        out_shape=tuple(pltpu.HBM(a.shape, a.dtype) for a in arrays),
        in_specs=[HBM] * (2 * n) + [SEM, SEM, ANY],
        out_specs=tuple([HBM] * (2 * n)),
        input_output_aliases={i: i for i in range(2 * n)},
        compiler_params=pltpu.CompilerParams(has_side_effects=DATAFLOW))(*arrays, send, recv, after)
    return list(res[n:])


def sum_stage_b(part, recv, place, *, name):
    _, half, cols = part.shape

    def body(place_ref, p_ref, r_ref, o_ref):
        acc = p_ref[...]
        for k in range(3):
            acc = acc + r_ref[k].astype(F32)
        o_ref[...] = acc

    return pl.pallas_call(
        body, name=name,
        grid_spec=pltpu.PrefetchScalarGridSpec(
            num_scalar_prefetch=1, grid=(1,),
            in_specs=[pl.BlockSpec((None, half, cols), lambda i, place_ref: (place_ref[0], 0, 0)),
                      pl.BlockSpec((3, half, cols), lambda i, place_ref: (0, 0, 0))],
            out_specs=pl.BlockSpec((half, cols), lambda i, place_ref: (place_ref[1], 0))),
        out_shape=jax.ShapeDtypeStruct((2 * half, cols), F32),
        compiler_params=_params("arbitrary"))(place, part, recv)


def reduce_stage_c(fulls, *, name):
    n = len(fulls)

    def body(*refs):
        ins, outs = refs[:n], refs[n:2 * n]
        send_sems, recv_sems = refs[2 * n:]
        x, y, c = _place()
        for t in range(n):
            half = ins[t].shape[0] // 2
            pltpu.make_async_remote_copy(
                src_ref=ins[t].at[pl.ds(c * half, half), :], dst_ref=outs[t].at[pl.ds(c * half, half), :],
                send_sem=send_sems.at[t], recv_sem=recv_sems.at[t],
                device_id=(x, y, 1 - c), device_id_type=MESH).start()
        for t in range(n):
            half = ins[t].shape[0] // 2
            done = pltpu.make_async_remote_copy(
                src_ref=ins[t].at[pl.ds(c * half, half), :], dst_ref=outs[t].at[pl.ds((1 - c) * half, half), :],
                send_sem=send_sems.at[t], recv_sem=recv_sems.at[t],
                device_id=(x, y, 1 - c), device_id_type=MESH)
            done.wait_send()
            done.wait_recv()

    return pl.pallas_call(
        body, name=name, in_specs=[ANY] * n, out_specs=[ANY] * n,
        out_shape=[jax.ShapeDtypeStruct(a.shape, a.dtype) for a in fulls],
        input_output_aliases={i: i for i in range(n)},
        scratch_shapes=[pltpu.SemaphoreType.DMA((n,)), pltpu.SemaphoreType.DMA((n,))],
        compiler_params=pltpu.CompilerParams(has_side_effects=True))(*fulls)


def gather_chip_blocks(slots, *, name):
    def body(in_ref, out_ref, send_sems, recv_sems):
        x, y, c = _place()
        me = 2 * x + y
        chips = _other_chips(x, y)
        for k, (px, py) in enumerate(chips):
            pltpu.make_async_remote_copy(
                src_ref=in_ref.at[me], dst_ref=out_ref.at[me],
                send_sem=send_sems.at[k], recv_sem=recv_sems.at[k],
                device_id=(px, py, c), device_id_type=MESH).start()
        for k, (px, py) in enumerate(chips):
            done = pltpu.make_async_remote_copy(
                src_ref=in_ref.at[me], dst_ref=out_ref.at[2 * px + py],
                send_sem=send_sems.at[k], recv_sem=recv_sems.at[k],
                device_id=(px, py, c), device_id_type=MESH)
            done.wait_send()
            done.wait_recv()

    return pl.pallas_call(
        body, name=name, in_specs=[ANY], out_specs=ANY,
        out_shape=jax.ShapeDtypeStruct(slots.shape, slots.dtype),
        input_output_aliases={0: 0},
        scratch_shapes=[pltpu.SemaphoreType.DMA((3,)), pltpu.SemaphoreType.DMA((3,))],
        compiler_params=pltpu.CompilerParams(has_side_effects=True))(slots)


def _ffn_fwd(h, hn, rstd, wg, wu, wd, cw, cb, l):
    g, up, act = ffn_in_fwd(hn, wg, wu, cw, cb, name=f"ffn{l}_in")
    out = mm_acc(act, wd, h, name=f"ffn{l}_down")
    return out, (hn, rstd, g, up)


def _ffn_bwd(dh, dh_bf, h, gain, saved, wg, wu, wd, cw, cb, l, pin):
    hn, rstd, g, up = saved
    dg, dup, dwg, dwu, dwd, dcw, dcb = ffn_bwd_a(dh_bf, hn, g, up, wd, cw, cb, name=f"ffn{l}_bwd_a", pin=pin)
    dh_in, dh_in_bf, dgain = dx_rms_bwd([(dg, wg), (dup, wu)], h, gain, rstd, dh, name=f"ffn{l}_bwd_b")
    return dh_in, dh_in_bf, (dwg, dwu, dwd), (dcw, dcb, dgain)


def _reduce_to_chip(grads, place, *, tag):
    recv_a = reduce_stage_a([g[0] for g in grads], name=f"reduce_a_{tag}")
    parts = [sum_stage_a(g[0], r, place, g[1], name=f"sum_a_{tag}{i}")
             for i, (g, r) in enumerate(zip(grads, recv_a))]
    send, recv, srcs, lands, token = reduce_b_start([p[1] for p in parts], name=f"reduce_b_start_{tag}")
    return [p[0] for p in parts], send, recv, srcs, lands, token


def _reduce_finish(state, place, after, *, tag):
    parts, send, recv, srcs, lands, _ = state
    recv_b = reduce_b_wait(srcs, lands, send, recv, after, name=f"reduce_b_wait_{tag}")
    halves = [sum_stage_b(p, r, place, name=f"sum_b_{tag}{i}") for i, (p, r) in enumerate(zip(parts, recv_b))]
    return reduce_stage_c(halves, name=f"reduce_c_{tag}")


def kernel(x, norm_mix, norm_ffn, final_norm, w_in_even, conv_a, w_pool, pool_scale, w_out_even, w_in_odd, sgu_norm, w_spatial, b_spatial, w_out_odd, w_ffn_gate, w_ffn_up, conv_ffn, b_conv_ffn, w_ffn_down, loss_target, m_norm_mix, m_norm_ffn, m_final_norm, m_w_in_even, m_conv_a, m_w_pool, m_pool_scale, m_w_out_even, m_w_in_odd, m_sgu_norm, m_w_spatial, m_b_spatial, m_w_out_odd, m_w_ffn_gate, m_w_ffn_up, m_conv_ffn, m_b_conv_ffn, m_w_ffn_down, v_norm_mix, v_norm_ffn, v_final_norm, v_w_in_even, v_conv_a, v_w_pool, v_pool_scale, v_w_out_even, v_w_in_odd, v_sgu_norm, v_w_spatial, v_b_spatial, v_w_out_odd, v_w_ffn_gate, v_w_ffn_up, v_conv_ffn, v_b_conv_ffn, v_w_ffn_down):
    weights = dict(norm_mix=norm_mix, norm_ffn=norm_ffn, final_norm=final_norm, w_in_even=w_in_even,
                   conv_a=conv_a, w_pool=w_pool, pool_scale=pool_scale, w_out_even=w_out_even,
                   w_in_odd=w_in_odd, sgu_norm=sgu_norm, w_spatial=w_spatial, b_spatial=b_spatial,
                   w_out_odd=w_out_odd, w_ffn_gate=w_ffn_gate, w_ffn_up=w_ffn_up, conv_ffn=conv_ffn,
                   b_conv_ffn=b_conv_ffn, w_ffn_down=w_ffn_down)
    m_in = dict(norm_mix=m_norm_mix, norm_ffn=m_norm_ffn, final_norm=m_final_norm, w_in_even=m_w_in_even,
                conv_a=m_conv_a, w_pool=m_w_pool, pool_scale=m_pool_scale, w_out_even=m_w_out_even,
                w_in_odd=m_w_in_odd, sgu_norm=m_sgu_norm, w_spatial=m_w_spatial, b_spatial=m_b_spatial,
                w_out_odd=m_w_out_odd, w_ffn_gate=m_w_ffn_gate, w_ffn_up=m_w_ffn_up, conv_ffn=m_conv_ffn,
                b_conv_ffn=m_b_conv_ffn, w_ffn_down=m_w_ffn_down)
    v_in = dict(norm_mix=v_norm_mix, norm_ffn=v_norm_ffn, final_norm=v_final_norm, w_in_even=v_w_in_even,
                conv_a=v_conv_a, w_pool=v_w_pool, pool_scale=v_pool_scale, w_out_even=v_w_out_even,
                w_in_odd=v_w_in_odd, sgu_norm=v_sgu_norm, w_spatial=v_w_spatial, b_spatial=v_b_spatial,
                w_out_odd=v_w_out_odd, w_ffn_gate=v_w_ffn_gate, w_ffn_up=v_w_ffn_up, conv_ffn=v_conv_ffn,
                b_conv_ffn=v_b_conv_ffn, w_ffn_down=v_w_ffn_down)
    order = list(weights)

    chip = 2 * lax.axis_index("x") + lax.axis_index("y")
    core = lax.axis_index("c")
    place = jnp.stack([chip, core]).astype(jnp.int32)
    chip_arr = place[:1]

    h0 = x[0]
    target = loss_target[0]
    d_model = h0.shape[1]
    f_shard = w_ffn_gate.shape[-1]

    def own_slot(v):
        return lax.dynamic_update_index_in_dim(jnp.zeros((N_CHIPS,) + v.shape, v.dtype), v, chip, 0)

    groups = [
        [cast_into_slot(w_in_even[0], chip_arr, name="cast_win_e"),
         cast_into_slot(w_out_even[0], chip_arr, name="cast_wout_e")],
        [cast_into_slot(w_ffn_gate, chip_arr, l=0, name="cast_wg0"),
         cast_into_slot(w_ffn_up, chip_arr, l=0, name="cast_wu0"),
         cast_into_slot(w_ffn_down, chip_arr, l=0, name="cast_wd0")],
        [cast_into_slot(w_in_odd[0], chip_arr, name="cast_win_o"),
         cast_into_slot(w_out_odd[0], chip_arr, name="cast_wout_o")],
        [cast_into_slot(w_ffn_gate, chip_arr, l=1, name="cast_wg1"),
         cast_into_slot(w_ffn_up, chip_arr, l=1, name="cast_wu1"),
         cast_into_slot(w_ffn_down, chip_arr, l=1, name="cast_wd1")]]
    smalls = [own_slot(conv_a[0]), own_slot(sgu_norm), own_slot(conv_ffn[0]), own_slot(conv_ffn[1])]
    sems, groups, smalls, token = gather_start(groups, smalls)

    def arrive(gi, after, with_smalls=False):
        kw = dict(smalls=smalls, small_send=sems[-2], small_recv=sems[-1]) if with_smalls else {}
        bufs, small_out = gather_wait(groups[gi], sems[2 * gi], sems[2 * gi + 1], after, name=f"gather_wait{gi}", **kw)
        return gather_forward(bufs, name=f"gather_forward{gi}"), small_out

    cb = b_conv_ffn.reshape(-1, N_CHIPS, 1, f_shard)
    wp_bf = w_pool[0].astype(BF16)
    wp_t_bf = jnp.transpose(w_pool[0], (0, 2, 1)).astype(BF16)
    ws = w_spatial[0]
    bs = b_spatial[0][:, :, None]

    (win_e, wout_e), (ca_g, sn_g, cw0, cw1) = arrive(0, token, with_smalls=True)
    wout_e = wout_e.reshape(-1, d_model)
    ca_full = jnp.transpose(ca_g, (1, 0, 2)).reshape(ca_g.shape[1], -1)
    sn_full = sn_g.reshape(1, -1)
    h1, xn0, rstd0, proj0, mix0 = even_layer_fwd(h0, norm_mix[0:1], win_e, ca_full, wp_bf, pool_scale, wout_e,
                                                 name="l0_fwd")
    hn0, rstdf0 = rms_fwd(h1, norm_ffn[0:1], name="ffn0_rms")
    (wg0, wu0, wd0), _ = arrive(1, hn0)
    h2, ffn0 = _ffn_fwd(h1, hn0, rstdf0, wg0, wu0, wd0, cw0, cb[0], 0)
    (win_o, wout_o), _ = arrive(2, h2)
    wout_o = wout_o.reshape(-1, d_model)
    h3, xn1, rstd1, p1, mix1, rstd_v = odd_layer_fwd(h2, norm_mix[1:2], win_o, sn_full, ws, bs, wout_o, name="l1_fwd")
    hn1, rstdf1 = rms_fwd(h3, norm_ffn[1:2], name="ffn1_rms")
    (wg1, wu1, wd1), _ = arrive(3, hn1)
    h4, ffn1 = _ffn_fwd(h3, hn1, rstdf1, wg1, wu1, wd1, cw1, cb[1], 1)

    dh4, dh4_bf, loss_row, d_final = final_loss(h4, target, final_norm[None], name="loss")
    loss = lax.psum(loss_row[0, 0], ("x", "y", "c"))

    dh3, dh3_bf, big3, (dcw1, dcb1, dnf1) = _ffn_bwd(
        dh4, dh4_bf, h3, norm_ffn[1:2], ffn1, wg1, wu1, wd1, cw1, cb[1], 1, None)
    red3 = _reduce_to_chip([(g, BF16) for g in big3], place, tag="g3")

    def as_blocks(a):
        return a.reshape(N_CHIPS, -1, d_model)

    dwout_o = as_blocks(mm_tn(mix1[None], dh3_bf[None], name="l1_dwout"))
    dp1, dsn, dws, dbs = sgu_bwd(p1, dh3_bf, wout_o, rstd_v, sn_full, ws, bs, name="l1_mix_bwd", pin=red3[-1])
    dwin_o = mm_tn(xn1[None], dp1, name="l1_dwin")
    dh2, dh2_bf, dnm1 = dx_rms_bwd([(dp1, win_o)], h2, norm_mix[1:2], rstd1, dh3, name="l1_dx")
    red2 = _reduce_to_chip([(dwin_o, BF16), (dwout_o, BF16)], place, tag="g2")

    dh1, dh1_bf, big1, (dcw0, dcb0, dnf0) = _ffn_bwd(
        dh2, dh2_bf, h1, norm_ffn[0:1], ffn0, wg0, wu0, wd0, cw0, cb[0], 0, red2[-1])
    red1 = _reduce_to_chip([(g, BF16) for g in big1], place, tag="g1")

    dwout_e = as_blocks(mm_tn(mix0[None], dh1_bf[None], name="l0_dwout"))
    dproj0, dca, dwp, dps = even_bwd(proj0, dh1_bf, wout_e, ca_full, wp_bf, wp_t_bf, pool_scale,
                                     name="l0_mix_bwd", pin=red1[-1])
    dwin_e = mm_tn(xn0[None], dproj0, name="l0_dwin")
    dh0, _, dnm0 = dx_rms_bwd([(dproj0, win_e)], h0, norm_mix[0:1], rstd0, dh1, name="l0_dx")
    grad_x = dh0[None]

    small_parts = dict(
        norm_mix=jnp.concatenate([dnm0, dnm1]), norm_ffn=jnp.concatenate([dnf0, dnf1]), final_norm=d_final,
        conv_a=dca, w_pool=dwp, pool_scale=dps, sgu_norm=dsn, w_spatial=dws, b_spatial=dbs,
        conv_ffn=jnp.stack([dcw0, dcw1]), b_conv_ffn=jnp.stack([dcb0, dcb1]))
    flat = jnp.concatenate([v.reshape(-1) for v in small_parts.values()])
    pad = (-flat.shape[0]) % (N_CHIPS * 32 * 128)
    small = jnp.pad(flat, (0, pad)).reshape(N_CHIPS, -1, 128)
    red0 = _reduce_to_chip([(dwin_e, BF16), (dwout_e, BF16), (small, F32)], place, tag="g0")

    full3 = _reduce_finish(red3, place, red0[-1], tag="g3")
    full2 = _reduce_finish(red2, place, full3[0], tag="g2")
    full1 = _reduce_finish(red1, place, full2[0], tag="g1")
    full0 = _reduce_finish(red0, place, full1[0], tag="g0")
    small_slots = lax.dynamic_update_index_in_dim(jnp.zeros(small.shape, F32), full0[2], chip, 0)
    small_sum = gather_chip_blocks(small_slots, name="gather_small").reshape(-1)
    grads = {
        "w_in_even": full0[0][None], "w_out_even": full0[1][None],
        "w_in_odd": full2[0][None], "w_out_odd": full2[1][None],
        "w_ffn_gate": jnp.stack([full1[0], full3[0]]), "w_ffn_up": jnp.stack([full1[1], full3[1]]),
        "w_ffn_down": jnp.stack([full1[2], full3[2]])}
    off = 0
    small_red = {}
    for nm, v in small_parts.items():
        small_red[nm] = small_sum[off:off + v.size].reshape(v.shape)
        off += v.size
    for nm in ("norm_mix", "norm_ffn", "pool_scale"):
        grads[nm] = small_red[nm].reshape(weights[nm].shape)
    grads["final_norm"] = small_red["final_norm"].reshape(weights["final_norm"].shape)
    grads["w_pool"] = small_red["w_pool"][None]
    grads["w_spatial"] = small_red["w_spatial"][None]
    grads["b_spatial"] = small_red["b_spatial"].reshape(weights["b_spatial"].shape)
    grads["b_conv_ffn"] = small_red["b_conv_ffn"].reshape(weights["b_conv_ffn"].shape)
    grads["conv_a"] = lax.dynamic_slice_in_dim(small_red["conv_a"], chip * conv_a.shape[-1], conv_a.shape[-1], 1)[None]
    grads["sgu_norm"] = lax.dynamic_slice_in_dim(small_red["sgu_norm"], chip * sgu_norm.shape[-1], sgu_norm.shape[-1], 1)
    grads["conv_ffn"] = lax.dynamic_index_in_dim(small_red["conv_ffn"], chip, 1, keepdims=False)

    deltas, new_m, new_v = {}, {}, {}
    for nm in order:
        w = weights[nm]
        w2 = w[None] if w.ndim == 1 else w
        shp = w2.shape
        d, nm_, nv_ = adamw(w2, grads[nm].reshape(shp), m_in[nm].reshape(shp), v_in[nm].reshape(shp),
                            name=f"adamw_{nm}")
        deltas[nm], new_m[nm], new_v[nm] = d.reshape(w.shape), nm_.reshape(w.shape), nv_.reshape(w.shape)

    return (loss, grad_x, *[grads[n] for n in order], *[deltas[n] for n in order],
            *[new_m[n] for n in order], *[new_v[n] for n in order])
```

```python
import jax
import jax.numpy as jnp
from jax import lax
from jax.experimental import pallas as pl
from jax.experimental.pallas import tpu as pltpu

F32 = jnp.float32
BF16 = jnp.bfloat16
MESH = pl.DeviceIdType.MESH
ANY = pl.BlockSpec(memory_space=pl.ANY)

EPS = 1e-6
POOL_WINDOWS = (2, 4, 8, 16)
GROUP = 128
CHUNK = 128
N_CHIPS = 4
N_DEV = 8
ROW_TILE = 512
HALO = 16
VMEM_LIMIT = 56 * 1024 * 1024

ADAM_LR = 0.001
ADAM_B1 = 0.9
ADAM_B2 = 0.999
ADAM_EPS = 1e-08
ADAM_WD = 0.01
ADAM_STEP = 10


def _params(*sem):
    return pltpu.CompilerParams(dimension_semantics=sem, vmem_limit_bytes=VMEM_LIMIT)


def _layer_spec(block, l, idx):
    if l is None:
        return pl.BlockSpec(block, idx)
    return pl.BlockSpec((None,) + block, lambda *g: (l,) + idx(*g))


def mm_nn(a, b, *, l=None, name):
    s, k = a.shape
    j_n, n = b.shape[-3], b.shape[-1]
    tm = min(ROW_TILE, s)

    def body(a_ref, b_ref, o_ref):
        o_ref[...] = jnp.dot(a_ref[...], b_ref[...], preferred_element_type=F32)

    return pl.pallas_call(
        body, name=name, grid=(j_n, s // tm),
        in_specs=[pl.BlockSpec((tm, k), lambda j, i: (i, 0)),
                  _layer_spec((None, k, n), l, lambda j, i: (j, 0, 0))],
        out_specs=pl.BlockSpec((None, tm, n), lambda j, i: (j, i, 0)),
        out_shape=jax.ShapeDtypeStruct((j_n, s, n), F32),
        compiler_params=_params("parallel", "parallel"))(a, b)


def mm_acc(a, b, res, *, l=None, name):
    j_n, s, kj = a.shape
    n = b.shape[-1]
    tm = min(ROW_TILE, s)

    def body(a_ref, b_ref, r_ref, o_ref):
        acc = r_ref[...]
        for j in range(j_n):
            acc = acc + jnp.dot(a_ref[j], b_ref[j], preferred_element_type=F32)
        o_ref[...] = acc

    return pl.pallas_call(
        body, name=name, grid=(s // tm,),
        in_specs=[pl.BlockSpec((j_n, tm, kj), lambda i: (0, i, 0)),
                  _layer_spec((j_n, kj, n), l, lambda i: (0, 0, 0)),
                  pl.BlockSpec((tm, n), lambda i: (i, 0))],
        out_specs=pl.BlockSpec((tm, n), lambda i: (i, 0)),
        out_shape=jax.ShapeDtypeStruct((s, n), F32),
        compiler_params=_params("parallel"))(a, b, res)


_NT = (((1,), (1,)), ((), ()))
_TN = (((0,), (0,)), ((), ()))


def mm_nt_sum(pairs, *, l=None, name):
    j_n, s, nj = pairs[0][0].shape
    k = pairs[0][1].shape[-2]
    tm = min(ROW_TILE, s)
    n_p = len(pairs)

    def body(*refs):
        o_ref = refs[-1]
        acc = jnp.zeros((tm, k), F32)
        for p in range(n_p):
            dy_ref, w_ref = refs[2 * p], refs[2 * p + 1]
            for j in range(j_n):
                acc = acc + lax.dot_general(dy_ref[j], w_ref[j], _NT, preferred_element_type=F32)
        o_ref[...] = acc

    in_specs, args = [], []
    for dy, w in pairs:
        in_specs += [pl.BlockSpec((j_n, tm, nj), lambda i: (0, i, 0)),
                     _layer_spec((j_n, k, nj), l, lambda i: (0, 0, 0))]
        args += [dy, w]
    return pl.pallas_call(
        body, name=name, grid=(s // tm,), in_specs=in_specs,
        out_specs=pl.BlockSpec((tm, k), lambda i: (i, 0)),
        out_shape=jax.ShapeDtypeStruct((s, k), F32),
        compiler_params=_params("parallel"))(*args)


def mm_nt_each(a, b, *, l=None, name, pin=None):
    s, k = a.shape
    j_n, nj = b.shape[-3], b.shape[-2]
    tm = min(ROW_TILE, s)

    def body(a_ref, b_ref, *rest):
        rest[-1][...] = lax.dot_general(a_ref[...], b_ref[...], _NT, preferred_element_type=F32)

    return pl.pallas_call(
        body, name=name, grid=(j_n, s // tm),
        in_specs=[pl.BlockSpec((tm, k), lambda j, i: (i, 0)),
                  _layer_spec((None, nj, k), l, lambda j, i: (j, 0, 0))] + ([ANY] if pin is not None else []),
        out_specs=pl.BlockSpec((None, tm, nj), lambda j, i: (j, i, 0)),
        out_shape=jax.ShapeDtypeStruct((j_n, s, nj), F32),
        compiler_params=_params("parallel", "parallel"))(a, b, *([pin] if pin is not None else []))


def mm_tn(a, dy, *, name, pin=None):
    ja, s, k = a.shape
    jd, _, n = dy.shape
    j_n = max(ja, jd)
    tk = min(ROW_TILE, s)

    def body(a_ref, d_ref, *rest):
        o_ref = rest[-1]

        @pl.when(pl.program_id(1) == 0)
        def _():
            o_ref[...] = jnp.zeros_like(o_ref)
        o_ref[...] += lax.dot_general(a_ref[...], d_ref[...], _TN, preferred_element_type=F32)

    return pl.pallas_call(
        body, name=name, grid=(j_n, s // tk),
        in_specs=[pl.BlockSpec((None, tk, k), (lambda j, i: (j, i, 0)) if ja > 1 else (lambda j, i: (0, i, 0))),
                  pl.BlockSpec((None, tk, n), (lambda j, i: (j, i, 0)) if jd > 1 else (lambda j, i: (0, i, 0)))]
        + ([ANY] if pin is not None else []),
        out_specs=pl.BlockSpec((None, k, n), lambda j, i: (j, 0, 0)),
        out_shape=jax.ShapeDtypeStruct((j_n, k, n), F32),
        compiler_params=_params("parallel", "arbitrary"))(a, dy, *([pin] if pin is not None else []))


def mm_tn_shared(a, dy, *, name):
    s, k = a.shape
    j_n, _, n = dy.shape
    tk = min(ROW_TILE, s)

    def body(a_ref, d_ref, o_ref):
        @pl.when(pl.program_id(0) == 0)
        def _():
            o_ref[...] = jnp.zeros_like(o_ref)
        a_t = a_ref[...]
        for j in range(j_n):
            o_ref[j] += lax.dot_general(a_t, d_ref[j], _TN, preferred_element_type=F32)

    return pl.pallas_call(
        body, name=name, grid=(s // tk,),
        in_specs=[pl.BlockSpec((tk, k), lambda i: (i, 0)), pl.BlockSpec((j_n, tk, n), lambda i: (0, i, 0))],
        out_specs=pl.BlockSpec((j_n, k, n), lambda i: (0, 0, 0)),
        out_shape=jax.ShapeDtypeStruct((j_n, k, n), F32),
        compiler_params=_params("arbitrary"))(a, dy)


def _back(x, k):
    return pltpu.roll(x, k, 0)


def _fwd(x, k):
    return pltpu.roll(x, x.shape[0] - k, 0)


def _causal_conv(x, w_ref):
    return w_ref[0:1, :] * _back(x, 2) + w_ref[1:2, :] * _back(x, 1) + w_ref[2:3, :] * x


def _causal_conv_t(dy, w_ref):
    return w_ref[2:3, :] * dy + w_ref[1:2, :] * _fwd(dy, 1) + w_ref[0:1, :] * _fwd(dy, 2)


def _gelu(x):
    return 0.5 * x * (1.0 + lax.erf(x * 0.7071067811865476))


def _gelu_grad(x):
    return 0.5 * (1.0 + lax.erf(x * 0.7071067811865476)) + x * jnp.exp(-0.5 * x * x) * 0.3989422804014327


def _colsum(x):
    return jnp.sum(x, axis=0, keepdims=True)


def rms_fwd(h, gain, *, name, pin=None):
    s, d = h.shape
    ts = min(ROW_TILE, s)

    def body(h_ref, g_ref, *rest):
        o_ref, r_ref = rest[-2:]
        x = h_ref[...]
        rstd = lax.rsqrt(jnp.mean(x * x, axis=-1, keepdims=True) + EPS)
        o_ref[...] = (x * rstd * g_ref[...]).astype(BF16)
        r_ref[...] = rstd

    return pl.pallas_call(
        body, name=name, grid=(s // ts,),
        in_specs=[pl.BlockSpec((ts, d), lambda i: (i, 0)), pl.BlockSpec((1, d), lambda i: (0, 0))]
        + ([ANY] if pin is not None else []),
        out_specs=[pl.BlockSpec((ts, d), lambda i: (i, 0)), pl.BlockSpec((ts, 1), lambda i: (i, 0))],
        out_shape=[jax.ShapeDtypeStruct((s, d), BF16), jax.ShapeDtypeStruct((s, 1), F32)],
        compiler_params=_params("parallel"))(h, gain, *([pin] if pin is not None else []))


def rms_bwd(dxn, h, gain, rstd, dres, *, name):
    s, d = h.shape
    ts = min(ROW_TILE, s)

    def body(dx_ref, h_ref, g_ref, r_ref, dr_ref, o_ref, ob_ref, dg_ref):
        @pl.when(pl.program_id(0) == 0)
        def _():
            dg_ref[...] = jnp.zeros_like(dg_ref)
        rstd_v = r_ref[...]
        hhat = h_ref[...] * rstd_v
        dx = dx_ref[...]
        dg_ref[...] += _colsum(dx * hhat)
        dxg = dx * g_ref[...]
        dh = dr_ref[...] + rstd_v * (dxg - hhat * jnp.mean(dxg * hhat, axis=-1, keepdims=True))
        o_ref[...] = dh
        ob_ref[...] = dh.astype(BF16)

    row = pl.BlockSpec((ts, d), lambda i: (i, 0))
    vec = pl.BlockSpec((1, d), lambda i: (0, 0))
    return pl.pallas_call(
        body, name=name, grid=(s // ts,),
        in_specs=[row, row, vec, pl.BlockSpec((ts, 1), lambda i: (i, 0)), row],
        out_specs=[row, row, vec],
        out_shape=[jax.ShapeDtypeStruct((s, d), F32), jax.ShapeDtypeStruct((s, d), BF16),
                   jax.ShapeDtypeStruct((1, d), F32)],
        compiler_params=_params("arbitrary"))(dxn, h, gain, rstd, dres)


def final_loss(h, target, gain, *, name):
    s, d = h.shape
    ts = min(ROW_TILE, s)

    def body(h_ref, t_ref, g_ref, o_ref, ob_ref, l_ref, dg_ref):
        @pl.when(pl.program_id(0) == 0)
        def _():
            l_ref[...] = jnp.zeros_like(l_ref)
            dg_ref[...] = jnp.zeros_like(dg_ref)
        x = h_ref[...]
        rstd = lax.rsqrt(jnp.mean(x * x, axis=-1, keepdims=True) + EPS)
        hhat = x * rstd
        err = hhat * g_ref[...] - t_ref[...]
        l_ref[...] += 0.5 * jnp.sum(jnp.mean(err * err, axis=-1, keepdims=True), axis=0, keepdims=True)
        dy = err * (1.0 / d)
        dg_ref[...] += _colsum(dy * hhat)
        dyg = dy * g_ref[...]
        dh = rstd * (dyg - hhat * jnp.mean(dyg * hhat, axis=-1, keepdims=True))
        o_ref[...] = dh
        ob_ref[...] = dh.astype(BF16)

    row = pl.BlockSpec((ts, d), lambda i: (i, 0))
    vec = pl.BlockSpec((1, d), lambda i: (0, 0))
    return pl.pallas_call(
        body, name=name, grid=(s // ts,),
        in_specs=[row, row, vec],
        out_specs=[row, row, pl.BlockSpec((1, 128), lambda i: (0, 0)), vec],
        out_shape=[jax.ShapeDtypeStruct((s, d), F32), jax.ShapeDtypeStruct((s, d), BF16),
                   jax.ShapeDtypeStruct((1, 128), F32), jax.ShapeDtypeStruct((1, d), F32)],
        compiler_params=_params("arbitrary"))(h, target, gain)


def _halo_specs(n_lead, ts, width, n_tiles):
    hb = ts // HALO
    prev = pl.BlockSpec((n_lead, HALO, width), lambda i: (0, jnp.maximum(i * hb - 1, 0), 0))
    nxt = pl.BlockSpec((n_lead, HALO, width), lambda i: (0, jnp.minimum((i + 1) * hb, n_tiles * hb - 1), 0))
    return prev, nxt


def _pool_fwd(z_ext, g, pos):
    w = POOL_WINDOWS[g]
    zg = z_ext[:, g * GROUP:(g + 1) * GROUP]
    acc = zg
    sh = 1
    while sh < w:
        acc = acc + _back(acc, sh)
        sh *= 2
    return acc[HALO:] / jnp.minimum(pos, float(w)) - zg[HALO:]


def even_layer_fwd(h, gain, win, conv_a, w_pool, pool_scale, wout, *, name, pin=None):
    s, d = h.shape
    w = win.shape[-1]
    ts = min(ROW_TILE, s)
    hb = ts // HALO

    def body(h_ref, hp_ref, gain_ref, win_ref, ca_ref, wp_ref, ps_ref, wout_ref, *rest):
        o_ref, xn_ref, r_ref, p_ref, m_ref = rest[-5:]
        i = pl.program_id(0)
        keep = jnp.where(i > 0, 1.0, 0.0)
        h_ext = jnp.concatenate([hp_ref[...], h_ref[...]], axis=0)
        rstd = lax.rsqrt(jnp.mean(h_ext * h_ext, axis=-1, keepdims=True) + EPS)
        xn_ext = (h_ext * rstd * gain_ref[...]).astype(BF16)
        xn_ref[...] = xn_ext[HALO:]
        r_ref[...] = rstd[HALO:]
        p32 = []
        for k in range(4):
            pk = jnp.dot(xn_ext, win_ref[k], preferred_element_type=F32).astype(BF16)
            p_ref[k] = pk[HALO:]
            pk = pk.astype(F32)
            p32.append(jnp.concatenate([pk[:HALO] * keep, pk[HALO:]], axis=0))
        m_ref[:, 0:w] = (p32[0][HALO:] * _causal_conv(p32[1] * p32[2], ca_ref)[HALO:]).astype(BF16)
        pos = (i * ts + lax.broadcasted_iota(jnp.int32, (ts, 1), 0) + 1).astype(F32)
        for g in range(len(POOL_WINDOWS)):
            pooled = _pool_fwd(p32[3], g, pos)
            mixed = jnp.dot(pooled.astype(BF16), wp_ref[g], preferred_element_type=F32)
            cols = slice(g * GROUP, (g + 1) * GROUP)
            m_ref[:, w + g * GROUP:w + (g + 1) * GROUP] = (mixed * ps_ref[:, cols]).astype(BF16)
        o_ref[...] = h_ref[...] + jnp.dot(m_ref[...], wout_ref[...], preferred_element_type=F32)

    const = lambda shape: pl.BlockSpec(shape, lambda i: (0,) * len(shape))
    row = pl.BlockSpec((ts, d), lambda i: (i, 0))
    return pl.pallas_call(
        body, name=name, grid=(s // ts,),
        in_specs=[row, pl.BlockSpec((HALO, d), lambda i: (jnp.maximum(i * hb - 1, 0), 0)), const((1, d)),
                  const((4, d, w)), const((3, w)), const((4, GROUP, GROUP)), const((1, w)), const((2 * w, d))]
        + ([ANY] if pin is not None else []),
        out_specs=[row, row, pl.BlockSpec((ts, 1), lambda i: (i, 0)),
                   pl.BlockSpec((4, ts, w), lambda i: (0, i, 0)), pl.BlockSpec((ts, 2 * w), lambda i: (i, 0))],
        out_shape=[jax.ShapeDtypeStruct((s, d), F32), jax.ShapeDtypeStruct((s, d), BF16),
                   jax.ShapeDtypeStruct((s, 1), F32), jax.ShapeDtypeStruct((4, s, w), BF16),
                   jax.ShapeDtypeStruct((s, 2 * w), BF16)],
        compiler_params=_params("parallel"))(
            h, h, gain, win, conv_a, w_pool, pool_scale, wout, *([pin] if pin is not None else []))


def even_bwd(proj, dh_bf, wout, conv_a, w_pool, w_pool_t, pool_scale, *, name, pin=None):
    _, s, w = proj.shape
    d = dh_bf.shape[1]
    ts = min(ROW_TILE, s)
    n_t = s // ts
    prev, nxt = _halo_specs(4, ts, w, n_t)
    hb = ts // HALO
    n_ext = ts + HALO

    def body(p_ref, pp_ref, pn_ref, dh_ref, dhn_ref, wout_ref, ca_ref, wp_ref, wpt_ref, ps_ref, *rest):
        dp_ref, dca_ref, dwp_ref, dps_ref = rest[-4:]
        i = pl.program_id(0)

        @pl.when(i == 0)
        def _():
            dca_ref[...] = jnp.zeros_like(dca_ref)
            dwp_ref[...] = jnp.zeros_like(dwp_ref)
            dps_ref[...] = jnp.zeros_like(dps_ref)

        keep_p = jnp.where(i > 0, 1.0, 0.0)
        keep_n = jnp.where(i < n_t - 1, 1.0, 0.0)
        dmix = lax.dot_general(jnp.concatenate([dh_ref[...], dhn_ref[...]], axis=0), wout_ref[...], _NT,
                               preferred_element_type=F32)
        a_b, a_c, a_v = (p_ref[k].astype(F32) for k in range(3))
        cv_ext = jnp.concatenate([pp_ref[1].astype(F32) * pp_ref[2].astype(F32) * keep_p, a_c * a_v], axis=0)
        dy_a = dmix[:ts, 0:w]
        dp_ref[0] = (dy_a * _causal_conv(cv_ext, ca_ref)[HALO:]).astype(BF16)
        dcc = dy_a * a_b
        dca_ref[2:3, :] += _colsum(dcc * cv_ext[HALO:])
        dca_ref[1:2, :] += _colsum(dcc * _back(cv_ext, 1)[HALO:])
        dca_ref[0:1, :] += _colsum(dcc * _back(cv_ext, 2)[HALO:])
        dcc_ext = jnp.concatenate([dcc, dmix[ts:, 0:w] * pn_ref[0].astype(F32) * keep_n], axis=0)
        dcv = _causal_conv_t(dcc_ext, ca_ref)[:ts]
        dp_ref[1] = (dcv * a_v).astype(BF16)
        dp_ref[2] = (dcv * a_c).astype(BF16)
        z_ext = jnp.concatenate([pp_ref[3].astype(F32) * keep_p, p_ref[3].astype(F32)], axis=0)
        pos = (i * ts + lax.broadcasted_iota(jnp.int32, (ts, 1), 0) + 1).astype(F32)
        pos_ext = (i * ts + lax.broadcasted_iota(jnp.int32, (n_ext, 1), 0) + 1).astype(F32)
        for g, win in enumerate(POOL_WINDOWS):
            cols = slice(g * GROUP, (g + 1) * GROUP)
            ycols = slice(w + g * GROUP, w + (g + 1) * GROUP)
            pooled = _pool_fwd(z_ext, g, pos).astype(BF16)
            mixed = jnp.dot(pooled, wp_ref[g], preferred_element_type=F32)
            dy_b = dmix[:ts, ycols]
            dps_ref[:, cols] += _colsum(dy_b * mixed)
            dmixed_ext = jnp.concatenate([dy_b, dmix[ts:, ycols] * keep_n], axis=0) * ps_ref[:, cols]
            dmixed_ext = dmixed_ext.astype(BF16)
            dwp_ref[g] += lax.dot_general(pooled, dmixed_ext[:ts], _TN, preferred_element_type=F32)
            dpooled = jnp.dot(dmixed_ext, wpt_ref[g], preferred_element_type=F32)
            acc = dpooled / jnp.minimum(pos_ext, float(win))
            sh = 1
            while sh < win:
                acc = acc + _fwd(acc, sh)
                sh *= 2
            dp_ref[3, :, cols] = (acc[:ts] - dpooled[:ts]).astype(BF16)

    tile4 = pl.BlockSpec((4, ts, w), lambda i: (0, i, 0))
    const = lambda shape: pl.BlockSpec(shape, lambda i: (0,) * len(shape))
    return pl.pallas_call(
        body, name=name, grid=(n_t,),
        in_specs=[tile4, prev, nxt, pl.BlockSpec((ts, d), lambda i: (i, 0)),
                  pl.BlockSpec((HALO, d), lambda i: (jnp.minimum((i + 1) * hb, n_t * hb - 1), 0)),
                  const((2 * w, d)), const((3, w)), const((4, GROUP, GROUP)), const((4, GROUP, GROUP)), const((1, w))]
        + ([ANY] if pin is not None else []),
        out_specs=[tile4, const((3, w)), const((4, GROUP, GROUP)), const((1, w))],
        out_shape=[jax.ShapeDtypeStruct((4, s, w), BF16), jax.ShapeDtypeStruct((3, w), F32),
                   jax.ShapeDtypeStruct((4, GROUP, GROUP), F32), jax.ShapeDtypeStruct((1, w), F32)],
        compiler_params=_params("arbitrary"))(
            proj, proj, proj, dh_bf, dh_bf, wout, conv_a, w_pool, w_pool_t, pool_scale,
            *([pin] if pin is not None else []))


def _ffn_halo(ts, f, n_t, nxt):
    hb = ts // HALO
    if nxt:
        return pl.BlockSpec((None, HALO, f), lambda j, i: (j, jnp.minimum((i + 1) * hb, n_t * hb - 1), 0))
    return pl.BlockSpec((None, HALO, f), lambda j, i: (j, jnp.maximum(i * hb - 1, 0), 0))


def ffn_act_fwd(g, up, cw, cb, *, name):
    j_n, s, f = g.shape
    ts = min(ROW_TILE, s)
    n_t = s // ts

    def body(g_ref, gp_ref, u_ref, cw_ref, cb_ref, o_ref):
        keep = jnp.where(pl.program_id(1) > 0, 1.0, 0.0)
        g_ext = jnp.concatenate([gp_ref[...] * keep, g_ref[...]], axis=0)
        gc = _causal_conv(g_ext, cw_ref)[HALO:] + cb_ref[...]
        o_ref[...] = (gc * jax.nn.sigmoid(gc) * u_ref[...]).astype(BF16)

    tile = pl.BlockSpec((None, ts, f), lambda j, i: (j, i, 0))
    return pl.pallas_call(
        body, name=name, grid=(j_n, n_t),
        in_specs=[tile, _ffn_halo(ts, f, n_t, False), tile,
                  pl.BlockSpec((None, 3, f), lambda j, i: (j, 0, 0)),
                  pl.BlockSpec((None, 1, f), lambda j, i: (j, 0, 0))],
        out_specs=tile,
        out_shape=jax.ShapeDtypeStruct((j_n, s, f), BF16),
        compiler_params=_params("parallel", "parallel"))(g, g, up, cw, cb)


def ffn_act_bwd(g, up, dact, cw, cb, *, name):
    j_n, s, f = g.shape
    ts = min(ROW_TILE, s)
    n_t = s // ts

    def body(g_ref, gp_ref, gn_ref, u_ref, un_ref, d_ref, dn_ref, cw_ref, cb_ref,
             dg_ref, du_ref, dcw_ref, dcb_ref):
        i = pl.program_id(1)

        @pl.when(i == 0)
        def _():
            dcw_ref[...] = jnp.zeros_like(dcw_ref)
            dcb_ref[...] = jnp.zeros_like(dcb_ref)

        keep_p = jnp.where(i > 0, 1.0, 0.0)
        keep_n = jnp.where(i < n_t - 1, 1.0, 0.0)
        g_ext = jnp.concatenate([gp_ref[...] * keep_p, g_ref[...], gn_ref[...]], axis=0)
        gc = _causal_conv(g_ext, cw_ref)[HALO:] + cb_ref[...]
        sig = jax.nn.sigmoid(gc)
        dact_ext = jnp.concatenate([d_ref[...], dn_ref[...] * keep_n], axis=0)
        du_ref[...] = (dact_ext * gc * sig)[:ts].astype(BF16)
        up_ext = jnp.concatenate([u_ref[...], un_ref[...]], axis=0)
        dgc = dact_ext * up_ext * (sig * (1.0 + gc * (1.0 - sig)))
        dg_ref[...] = _causal_conv_t(dgc, cw_ref)[:ts].astype(BF16)
        dgc_t = dgc[:ts]
        dcb_ref[...] += _colsum(dgc_t)
        dcw_ref[2:3, :] += _colsum(dgc_t * g_ext[HALO:HALO + ts])
        dcw_ref[1:2, :] += _colsum(dgc_t * _back(g_ext, 1)[HALO:HALO + ts])
        dcw_ref[0:1, :] += _colsum(dgc_t * _back(g_ext, 2)[HALO:HALO + ts])

    tile = pl.BlockSpec((None, ts, f), lambda j, i: (j, i, 0))
    prev, nxt = _ffn_halo(ts, f, n_t, False), _ffn_halo(ts, f, n_t, True)
    return pl.pallas_call(
        body, name=name, grid=(j_n, n_t),
        in_specs=[tile, prev, nxt, tile, nxt, tile, nxt,
                  pl.BlockSpec((None, 3, f), lambda j, i: (j, 0, 0)),
                  pl.BlockSpec((None, 1, f), lambda j, i: (j, 0, 0))],
        out_specs=[tile, tile, pl.BlockSpec((None, 3, f), lambda j, i: (j, 0, 0)),
                   pl.BlockSpec((None, 1, f), lambda j, i: (j, 0, 0))],
        out_shape=[jax.ShapeDtypeStruct((j_n, s, f), BF16), jax.ShapeDtypeStruct((j_n, s, f), BF16),
                   jax.ShapeDtypeStruct((j_n, 3, f), F32), jax.ShapeDtypeStruct((j_n, 1, f), F32)],
        compiler_params=_params("parallel", "arbitrary"))(g, g, g, up, up, dact, dact, cw, cb)


def ffn_in_fwd(hn, wg, wu, cw, cb, *, name):
    s, d = hn.shape
    j_n, _, f = wg.shape
    tm = min(ROW_TILE, s)
    hb = tm // HALO

    def body(x_ref, xp_ref, wg_ref, wu_ref, cw_ref, cb_ref, g_ref, u_ref, a_ref):
        i, j = pl.program_id(0), pl.program_id(1)
        x_ext = jnp.concatenate([xp_ref[...], x_ref[...]], axis=0)
        g_ext = jnp.dot(x_ext, wg_ref[j], preferred_element_type=F32).astype(BF16)
        up = jnp.dot(x_ref[...], wu_ref[j], preferred_element_type=F32).astype(BF16)
        g_ref[...] = g_ext[HALO:]
        u_ref[...] = up
        keep = jnp.where(i > 0, 1.0, 0.0)
        g32 = jnp.concatenate([g_ext[:HALO].astype(F32) * keep, g_ext[HALO:].astype(F32)], axis=0)
        gc = _causal_conv(g32, cw_ref)[HALO:] + cb_ref[...]
        a_ref[...] = (gc * jax.nn.sigmoid(gc) * up.astype(F32)).astype(BF16)

    whole = pl.BlockSpec((j_n, d, f), lambda i, j: (0, 0, 0))
    tile = pl.BlockSpec((None, tm, f), lambda i, j: (j, i, 0))
    shape = jax.ShapeDtypeStruct((j_n, s, f), BF16)
    return pl.pallas_call(
        body, name=name, grid=(s // tm, j_n),
        in_specs=[pl.BlockSpec((tm, d), lambda i, j: (i, 0)),
                  pl.BlockSpec((HALO, d), lambda i, j: (jnp.maximum(i * hb - 1, 0), 0)),
                  whole, whole,
                  pl.BlockSpec((None, 3, f), lambda i, j: (j, 0, 0)),
                  pl.BlockSpec((None, 1, f), lambda i, j: (j, 0, 0))],
        out_specs=[tile, tile, tile], out_shape=[shape, shape, shape],
        compiler_params=_params("parallel", "parallel"))(hn, hn, wg, wu, cw, cb)


def ffn_bwd_a(dh_bf, hn, g, up, wd, cw, cb, *, name, pin=None):
    s, d = hn.shape
    j_n, _, f = g.shape
    tm = min(ROW_TILE, s)
    n_t = s // tm
    hb = tm // HALO

    def body(dh_ref, dhn_ref, x_ref, g_ref, gp_ref, gn_ref, u_ref, un_ref, wd_ref, cw_ref, cb_ref, *rest):
        dg_ref, du_ref, dwg_ref, dwu_ref, dwd_ref, dcw_ref, dcb_ref = rest[-7:]
        i = pl.program_id(1)

        @pl.when(i == 0)
        def _():
            for r in (dwg_ref, dwu_ref, dwd_ref, dcw_ref, dcb_ref):
                r[...] = jnp.zeros_like(r)

        keep_p = jnp.where(i > 0, 1.0, 0.0)
        keep_n = jnp.where(i < n_t - 1, 1.0, 0.0)
        dh = dh_ref[...]
        dact = lax.dot_general(jnp.concatenate([dh, dhn_ref[...]], axis=0), wd_ref[...], _NT,
                               preferred_element_type=F32)
        dact = jnp.concatenate([dact[:tm], dact[tm:] * keep_n], axis=0)
        g_ext = jnp.concatenate([gp_ref[...].astype(F32) * keep_p, g_ref[...].astype(F32),
                                 gn_ref[...].astype(F32)], axis=0)
        gc = _causal_conv(g_ext, cw_ref)[HALO:] + cb_ref[...]
        sig = jax.nn.sigmoid(gc)
        silu = gc * sig
        up_ext = jnp.concatenate([u_ref[...], un_ref[...]], axis=0).astype(F32)
        act = (silu * up_ext)[:tm].astype(BF16)
        dwd_ref[...] += lax.dot_general(act, dh, _TN, preferred_element_type=F32)
        dup = (dact * silu)[:tm].astype(BF16)
        du_ref[...] = dup
        dgc = dact * up_ext * (sig * (1.0 + gc * (1.0 - sig)))
        dg = _causal_conv_t(dgc, cw_ref)[:tm].astype(BF16)
        dg_ref[...] = dg
        x = x_ref[...]
        dwg_ref[...] += lax.dot_general(x, dg, _TN, preferred_element_type=F32)
        dwu_ref[...] += lax.dot_general(x, dup, _TN, preferred_element_type=F32)
        dgc_t = dgc[:tm]
        dcb_ref[...] += _colsum(dgc_t)
        dcw_ref[2:3, :] += _colsum(dgc_t * g_ext[HALO:HALO + tm])
        dcw_ref[1:2, :] += _colsum(dgc_t * _back(g_ext, 1)[HALO:HALO + tm])
        dcw_ref[0:1, :] += _colsum(dgc_t * _back(g_ext, 2)[HALO:HALO + tm])

    rows = pl.BlockSpec((tm, d), lambda j, i: (i, 0))
    rows_next = pl.BlockSpec((HALO, d), lambda j, i: (jnp.minimum((i + 1) * hb, n_t * hb - 1), 0))
    tile = pl.BlockSpec((None, tm, f), lambda j, i: (j, i, 0))
    prev, nxt = _ffn_halo(tm, f, n_t, False), _ffn_halo(tm, f, n_t, True)
    per_j = lambda r, c: pl.BlockSpec((None, r, c), lambda j, i: (j, 0, 0))
    return pl.pallas_call(
        body, name=name, grid=(j_n, n_t),
        in_specs=[rows, rows_next, rows, tile, prev, nxt, tile, nxt, per_j(f, d), per_j(3, f), per_j(1, f)]
        + ([ANY] if pin is not None else []),
        out_specs=[tile, tile, per_j(d, f), per_j(d, f), per_j(f, d), per_j(3, f), per_j(1, f)],
        out_shape=[jax.ShapeDtypeStruct((j_n, s, f), BF16), jax.ShapeDtypeStruct((j_n, s, f), BF16),
                   jax.ShapeDtypeStruct((j_n, d, f), F32), jax.ShapeDtypeStruct((j_n, d, f), F32),
                   jax.ShapeDtypeStruct((j_n, f, d), F32), jax.ShapeDtypeStruct((j_n, 3, f), F32),
                   jax.ShapeDtypeStruct((j_n, 1, f), F32)],
        compiler_params=_params("parallel", "arbitrary"))(
            dh_bf, dh_bf, hn, g, g, g, up, up, wd, cw, cb, *([pin] if pin is not None else []))


def dx_rms_bwd(pairs, h, gain, rstd, dres, *, name, pin=None):
    j_n, s, f = pairs[0][0].shape
    d = h.shape[1]
    tm = min(ROW_TILE // 2, s)
    n_p = len(pairs)

    def body(*refs):
        dy_refs, w_refs = refs[:n_p], refs[n_p:2 * n_p]
        h_ref, g_ref, r_ref, dr_ref = refs[2 * n_p:2 * n_p + 4]
        o_ref, ob_ref, dgain_ref = refs[-3:]

        @pl.when(pl.program_id(0) == 0)
        def _():
            dgain_ref[...] = jnp.zeros_like(dgain_ref)
        dx = jnp.zeros((tm, d), F32)
        for j in range(j_n):
            for p in range(n_p):
                dx = dx + lax.dot_general(dy_refs[p][j], w_refs[p][j], _NT, preferred_element_type=F32)
        rstd_v = r_ref[...]
        hhat = h_ref[...] * rstd_v
        dgain_ref[...] += _colsum(dx * hhat)
        dxg = dx * g_ref[...]
        dh = dr_ref[...] + rstd_v * (dxg - hhat * jnp.mean(dxg * hhat, axis=-1, keepdims=True))
        o_ref[...] = dh
        ob_ref[...] = dh.astype(BF16)

    tile4 = pl.BlockSpec((j_n, tm, f), lambda i: (0, i, 0))
    whole = pl.BlockSpec((j_n, d, f), lambda i: (0, 0, 0))
    row = pl.BlockSpec((tm, d), lambda i: (i, 0))
    vec = pl.BlockSpec((1, d), lambda i: (0, 0))
    return pl.pallas_call(
        body, name=name, grid=(s // tm,),
        in_specs=[tile4] * n_p + [whole] * n_p + [row, vec, pl.BlockSpec((tm, 1), lambda i: (i, 0)), row]
        + ([ANY] if pin is not None else []),
        out_specs=[row, row, vec],
        out_shape=[jax.ShapeDtypeStruct((s, d), F32), jax.ShapeDtypeStruct((s, d), BF16),
                   jax.ShapeDtypeStruct((1, d), F32)],
        compiler_params=_params("arbitrary"))(
            *[p[0] for p in pairs], *[p[1] for p in pairs], h, gain, rstd, dres, *([pin] if pin is not None else []))


def _sgu_gate(vn_bf, ws_ref, bs_ref, h, rows):
    tri = lax.broadcasted_iota(jnp.int32, (CHUNK, CHUNK), 0) >= lax.broadcasted_iota(jnp.int32, (CHUNK, CHUNK), 1)
    ws = jnp.where(tri, ws_ref[h], 0.0).astype(BF16)
    cols = slice((h % 4) * GROUP, (h % 4 + 1) * GROUP)
    return ws, jnp.dot(ws, vn_bf[h // 4][rows, cols], preferred_element_type=F32) + bs_ref[h]


def odd_layer_fwd(h, gain, win, sgu_norm, w_spatial, b_spatial, wout, *, name):
    s, d = h.shape
    w = win.shape[-1]
    ts = min(ROW_TILE, s)
    n_heads = w_spatial.shape[0]

    def body(h_ref, gain_ref, win_ref, n_ref, ws_ref, bs_ref, wout_ref, o_ref, xn_ref, r_ref, p_ref, m_ref, rv_ref):
        x = h_ref[...]
        rstd_x = lax.rsqrt(jnp.mean(x * x, axis=-1, keepdims=True) + EPS)
        xn = (x * rstd_x * gain_ref[...]).astype(BF16)
        xn_ref[...] = xn
        r_ref[...] = rstd_x
        for k in range(4):
            p_ref[k] = jnp.dot(xn, win_ref[k], preferred_element_type=F32).astype(BF16)
        v = [_gelu(p_ref[2].astype(F32)), _gelu(p_ref[3].astype(F32))]
        ms = (jnp.sum(v[0] * v[0], axis=-1, keepdims=True) + jnp.sum(v[1] * v[1], axis=-1, keepdims=True)) / (2 * w)
        rstd = lax.rsqrt(ms + EPS)
        rv_ref[...] = rstd
        vn = [(v[k] * rstd * n_ref[:, k * w:(k + 1) * w]).astype(BF16) for k in range(2)]
        for hd in range(n_heads):
            cols = slice((hd % 4) * GROUP, (hd % 4 + 1) * GROUP)
            for c in range(ts // CHUNK):
                rows = slice(c * CHUNK, (c + 1) * CHUNK)
                _, gate = _sgu_gate(vn, ws_ref, bs_ref, hd, rows)
                u = _gelu(p_ref[hd // 4, rows, cols].astype(F32))
                m_ref[rows, hd * GROUP:(hd + 1) * GROUP] = (u * gate).astype(BF16)
        o_ref[...] = x + jnp.dot(m_ref[...], wout_ref[...], preferred_element_type=F32)

    const = lambda shape: pl.BlockSpec(shape, lambda i: (0,) * len(shape))
    row = pl.BlockSpec((ts, d), lambda i: (i, 0))
    col1 = pl.BlockSpec((ts, 1), lambda i: (i, 0))
    return pl.pallas_call(
        body, name=name, grid=(s // ts,),
        in_specs=[row, const((1, d)), const((4, d, w)), const((1, 2 * w)),
                  const((n_heads, CHUNK, CHUNK)), const((n_heads, CHUNK, 1)), const((2 * w, d))],
        out_specs=[row, row, col1, pl.BlockSpec((4, ts, w), lambda i: (0, i, 0)),
                   pl.BlockSpec((ts, 2 * w), lambda i: (i, 0)), col1],
        out_shape=[jax.ShapeDtypeStruct((s, d), F32), jax.ShapeDtypeStruct((s, d), BF16),
                   jax.ShapeDtypeStruct((s, 1), F32), jax.ShapeDtypeStruct((4, s, w), BF16),
                   jax.ShapeDtypeStruct((s, 2 * w), BF16), jax.ShapeDtypeStruct((s, 1), F32)],
        compiler_params=_params("parallel"))(h, gain, win, sgu_norm, w_spatial, b_spatial, wout)


def sgu_bwd(p, dh_bf, wout, rstd, sgu_norm, w_spatial, b_spatial, *, name, pin=None):
    _, s, w = p.shape
    d = dh_bf.shape[1]
    ts = min(ROW_TILE, s)
    n_heads = w_spatial.shape[0]

    def body(p_ref, dh_ref, wout_ref, r_ref, n_ref, ws_ref, bs_ref, *rest):
        dp_ref, dn_ref, dws_ref, dbs_ref, dvn_ref, dm_ref = rest[-6:]

        @pl.when(pl.program_id(0) == 0)
        def _():
            dn_ref[...] = jnp.zeros_like(dn_ref)
            dws_ref[...] = jnp.zeros_like(dws_ref)
            dbs_ref[...] = jnp.zeros_like(dbs_ref)

        dm_ref[...] = lax.dot_general(dh_ref[...], wout_ref[...], _NT, preferred_element_type=F32)
        rstd_v = r_ref[...]
        vhat = [_gelu(p_ref[2 + k].astype(F32)) * rstd_v for k in range(2)]
        vn = [(vhat[k] * n_ref[:, k * w:(k + 1) * w]).astype(BF16) for k in range(2)]
        tri = lax.broadcasted_iota(jnp.int32, (CHUNK, CHUNK), 0) >= lax.broadcasted_iota(jnp.int32, (CHUNK, CHUNK), 1)
        for h in range(n_heads):
            cols = slice((h % 4) * GROUP, (h % 4 + 1) * GROUP)
            ocols = slice(h * GROUP, (h + 1) * GROUP)
            for c in range(ts // CHUNK):
                rows = slice(c * CHUNK, (c + 1) * CHUNK)
                ws, gate = _sgu_gate(vn, ws_ref, bs_ref, h, rows)
                pu = p_ref[h // 4, rows, cols].astype(F32)
                dm = dm_ref[rows, ocols]
                dp_ref[h // 4, rows, cols] = (dm * gate * _gelu_grad(pu)).astype(BF16)
                dgate = dm * _gelu(pu)
                dbs_ref[h] += jnp.sum(dgate, axis=-1, keepdims=True)
                dgate_bf = dgate.astype(BF16)
                dws = lax.dot_general(dgate_bf, vn[h // 4][rows, cols], _NT, preferred_element_type=F32)
                dws_ref[h] += jnp.where(tri, dws, 0.0)
                dvn_ref[rows, ocols] = lax.dot_general(ws, dgate_bf, _TN, preferred_element_type=F32)
        for k in range(2):
            kc = slice(k * w, (k + 1) * w)
            dvn = dvn_ref[:, kc]
            dn_ref[:, kc] += _colsum(dvn * vhat[k])
        dvh = [dvn_ref[:, k * w:(k + 1) * w] * n_ref[:, k * w:(k + 1) * w] for k in range(2)]
        dot = (jnp.sum(dvh[0] * vhat[0], axis=-1, keepdims=True)
               + jnp.sum(dvh[1] * vhat[1], axis=-1, keepdims=True)) / (2 * w)
        for k in range(2):
            dv = rstd_v * (dvh[k] - vhat[k] * dot)
            dp_ref[2 + k] = (dv * _gelu_grad(p_ref[2 + k].astype(F32))).astype(BF16)

    const = lambda shape: pl.BlockSpec(shape, lambda i: (0,) * len(shape))
    tile4 = pl.BlockSpec((4, ts, w), lambda i: (0, i, 0))
    return pl.pallas_call(
        body, name=name, grid=(s // ts,),
        in_specs=[tile4, pl.BlockSpec((ts, d), lambda i: (i, 0)), const((2 * w, d)),
                  pl.BlockSpec((ts, 1), lambda i: (i, 0)),
                  const((1, 2 * w)), const((n_heads, CHUNK, CHUNK)), const((n_heads, CHUNK, 1))]
        + ([ANY] if pin is not None else []),
        out_specs=[tile4, const((1, 2 * w)), const((n_heads, CHUNK, CHUNK)), const((n_heads, CHUNK, 1))],
        out_shape=[jax.ShapeDtypeStruct((4, s, w), BF16), jax.ShapeDtypeStruct((1, 2 * w), F32),
                   jax.ShapeDtypeStruct((n_heads, CHUNK, CHUNK), F32),
                   jax.ShapeDtypeStruct((n_heads, CHUNK, 1), F32)],
        scratch_shapes=[pltpu.VMEM((ts, 2 * w), F32), pltpu.VMEM((ts, 2 * w), F32)],
        compiler_params=_params("arbitrary"))(
            p, dh_bf, wout, rstd, sgu_norm, w_spatial, b_spatial, *([pin] if pin is not None else []))


def _row_tile(rows):
    if rows <= ROW_TILE:
        return rows
    for t in (512, 384, 352, 256, 128, 64, 32, 16, 8):
        if rows % t == 0:
            return t
    return rows


def adamw(w, g, m, v, *, name):
    shape = w.shape
    cols = shape[-1]
    rows = w.size // cols
    w2, g2, m2, v2 = (a.reshape(rows, cols) for a in (w, g, m, v))
    tr = _row_tile(rows)
    bc1 = 1.0 - ADAM_B1 ** ADAM_STEP
    bc2 = 1.0 - ADAM_B2 ** ADAM_STEP

    def body(w_ref, g_ref, m_ref, v_ref, d_ref, nm_ref, nv_ref):
        grad = g_ref[...]
        m_new = ADAM_B1 * m_ref[...] + (1.0 - ADAM_B1) * grad
        v_new = ADAM_B2 * v_ref[...] + (1.0 - ADAM_B2) * (grad * grad)
        nm_ref[...] = m_new
        nv_ref[...] = v_new
        d_ref[...] = -ADAM_LR * ((m_new / bc1) / (jnp.sqrt(v_new / bc2) + ADAM_EPS) + ADAM_WD * w_ref[...])

    spec = pl.BlockSpec((tr, cols), lambda i: (i, 0))
    outs = pl.pallas_call(
        body, name=name, grid=(rows // tr,),
        in_specs=[spec] * 4, out_specs=[spec] * 3,
        out_shape=[jax.ShapeDtypeStruct((rows, cols), F32)] * 3,
        compiler_params=_params("parallel"))(w2, g2, m2, v2)
    return tuple(o.reshape(shape) for o in outs)


def adamw_layers(w, grads, m, v, *, name):
    n_l, rows, cols = w.shape
    tr = _row_tile(rows)
    bc1 = 1.0 - ADAM_B1 ** ADAM_STEP
    bc2 = 1.0 - ADAM_B2 ** ADAM_STEP
    outs = None
    for l in range(n_l):
        def body(w_ref, g_ref, m_ref, v_ref, *rest):
            go_ref, d_ref, nm_ref, nv_ref = rest[-4:]
            grad = g_ref[...]
            m_new = ADAM_B1 * m_ref[...] + (1.0 - ADAM_B1) * grad
            v_new = ADAM_B2 * v_ref[...] + (1.0 - ADAM_B2) * (grad * grad)
            go_ref[...] = grad
            nm_ref[...] = m_new
            nv_ref[...] = v_new
            d_ref[...] = -ADAM_LR * ((m_new / bc1) / (jnp.sqrt(v_new / bc2) + ADAM_EPS) + ADAM_WD * w_ref[...])

        layer = pl.BlockSpec((None, tr, cols), lambda i, l=l: (l, i, 0))
        prev = list(outs) if outs is not None else []
        outs = pl.pallas_call(
            body, name=f"{name}{l}", grid=(rows // tr,),
            in_specs=[layer, pl.BlockSpec((tr, cols), lambda i: (i, 0)), layer, layer] + [ANY] * len(prev),
            out_specs=[layer] * 4,
            out_shape=[jax.ShapeDtypeStruct(w.shape, F32)] * 4,
            input_output_aliases={4 + k: k for k in range(len(prev))},
            compiler_params=_params("parallel"))(w, grads[l], m, v, *prev)
    return tuple(outs)


def _place():
    return lax.axis_index("x"), lax.axis_index("y"), lax.axis_index("c")


def _other_chips(x, y):
    return [(1 - x, y), (x, 1 - y), (1 - x, 1 - y)]


HBM = pl.BlockSpec(memory_space=pltpu.HBM)
SEM = pl.BlockSpec(memory_space=pltpu.SEMAPHORE)
DATAFLOW = pltpu.SideEffectType.DATAFLOW_SIDE_EFFECTING


def _in_hbm(a):
    return pltpu.with_memory_space_constraint(a, pltpu.HBM)


def cast_into_slot(w, chip, *, l=None, name):
    rows, cols = w.shape[-2:]
    tr = _row_tile(rows)

    def body(chip_ref, w_ref, o_ref):
        o_ref[...] = w_ref[...].astype(BF16)

    in_spec = (pl.BlockSpec((tr, cols), lambda i, chip_ref: (i, 0)) if l is None
               else pl.BlockSpec((None, tr, cols), lambda i, chip_ref: (l, i, 0)))
    return pl.pallas_call(
        body, name=name,
        grid_spec=pltpu.PrefetchScalarGridSpec(
            num_scalar_prefetch=1, grid=(rows // tr,), in_specs=[in_spec],
            out_specs=pl.BlockSpec((None, tr, cols), lambda i, chip_ref: (chip_ref[0], i, 0))),
        out_shape=jax.ShapeDtypeStruct((N_CHIPS, rows, cols), BF16),
        compiler_params=_params("parallel"))(chip, w)


def _half(ref, slot, c):
    half = ref.shape[1] // 2
    return ref.at[slot, pl.ds(c * half, half), :]


def gather_start(groups, smalls):
    flat = [b for g in groups for b in g]
    n_b, n_s, n_g = len(flat), len(smalls), len(groups)

    def body(*refs):
        bufs, small_refs = refs[:n_b], refs[n_b:n_b + n_s]
        sems = refs[n_b + n_s:n_b + n_s + 2 * n_g + 2]
        token = refs[-1]
        x, y, c = _place()
        me = 2 * x + y
        chips = _other_chips(x, y)
        for si in range(n_s):
            piece = small_refs[si].at[me]
            for k, (px, py) in enumerate(chips):
                pltpu.make_async_remote_copy(
                    src_ref=piece, dst_ref=piece,
                    send_sem=sems[2 * n_g].at[3 * si + k], recv_sem=sems[2 * n_g + 1].at[3 * si + k],
                    device_id=(px, py, c), device_id_type=MESH).start()
        t = 0
        for gi, group in enumerate(groups):
            for ti in range(len(group)):
                piece = _half(bufs[t], me, c)
                t += 1
                for k, (px, py) in enumerate(chips):
                    pltpu.make_async_remote_copy(
                        src_ref=piece, dst_ref=piece,
                        send_sem=sems[2 * gi].at[3 * ti + k], recv_sem=sems[2 * gi + 1].at[3 * ti + k],
                        device_id=(px, py, c), device_id_type=MESH).start()
        token[...] = jnp.zeros_like(token)

    sem_shapes = []
    for group in groups:
        sem_shapes += [pltpu.SemaphoreType.DMA((3 * len(group),))] * 2
    sem_shapes += [pltpu.SemaphoreType.DMA((3 * n_s,))] * 2
    arrays = flat + list(smalls)
    n_sem = len(sem_shapes)
    res = pl.pallas_call(
        body, name="gather_start",
        out_shape=tuple(sem_shapes) + tuple(pltpu.HBM(a.shape, a.dtype) for a in arrays)
        + (jax.ShapeDtypeStruct((8, 128), F32),),
        in_specs=[HBM] * len(arrays),
        out_specs=tuple([SEM] * n_sem + [HBM] * len(arrays) + [pl.BlockSpec(memory_space=pltpu.VMEM)]),
        input_output_aliases={i: n_sem + i for i in range(len(arrays))},
        compiler_params=pltpu.CompilerParams(has_side_effects=DATAFLOW))(*[_in_hbm(a) for a in arrays])
    sems, thru, token = res[:n_sem], res[n_sem:-1], res[-1]
    out_groups, t = [], 0
    for group in groups:
        out_groups.append(list(thru[t:t + len(group)]))
        t += len(group)
    return sems, out_groups, list(thru[n_b:]), token


def gather_wait(bufs, send, recv, after, *, name, smalls=(), small_send=None, small_recv=None):
    n_b, n_s = len(bufs), len(smalls)
    arrays = list(bufs) + list(smalls)
    sem_ops = [send, recv] + ([small_send, small_recv] if n_s else [])

    def body(*refs):
        buf_refs, small_refs = refs[:n_b], refs[n_b:n_b + n_s]
        sems = refs[n_b + n_s:n_b + n_s + len(sem_ops)]
        x, y, c = _place()
        me = 2 * x + y
        chips = _other_chips(x, y)
        for ti in range(n_b):
            for k, (px, py) in enumerate(chips):
                done = pltpu.make_async_remote_copy(
                    src_ref=_half(buf_refs[ti], me, c), dst_ref=_half(buf_refs[ti], 2 * px + py, c),
                    send_sem=sems[0].at[3 * ti + k], recv_sem=sems[1].at[3 * ti + k],
                    device_id=(px, py, c), device_id_type=MESH)
                done.wait_send()
                done.wait_recv()
        for si in range(n_s):
            for k, (px, py) in enumerate(chips):
                done = pltpu.make_async_remote_copy(
                    src_ref=small_refs[si].at[me], dst_ref=small_refs[si].at[2 * px + py],
                    send_sem=sems[2].at[3 * si + k], recv_sem=sems[3].at[3 * si + k],
                    device_id=(px, py, c), device_id_type=MESH)
                done.wait_send()
                done.wait_recv()

    res = pl.pallas_call(
        body, name=name,
        out_shape=tuple(pltpu.HBM(a.shape, a.dtype) for a in arrays),
        in_specs=[HBM] * len(arrays) + [SEM] * len(sem_ops) + [ANY],
        out_specs=tuple([HBM] * len(arrays)),
        input_output_aliases={i: i for i in range(len(arrays))},
        compiler_params=pltpu.CompilerParams(has_side_effects=DATAFLOW))(*arrays, *sem_ops, after)
    return list(res[:n_b]), list(res[n_b:])


def gather_forward(bufs, *, name):
    n = len(bufs)

    def body(*refs):
        ins, outs = refs[:n], refs[n:2 * n]
        send_sems, recv_sems = refs[2 * n:]
        x, y, c = _place()
        chips = _other_chips(x, y)
        for t in range(n):
            for k, (px, py) in enumerate(chips):
                pltpu.make_async_remote_copy(
                    src_ref=_half(ins[t], 2 * px + py, c), dst_ref=_half(outs[t], 2 * px + py, c),
                    send_sem=send_sems.at[3 * t + k], recv_sem=recv_sems.at[3 * t + k],
                    device_id=(x, y, 1 - c), device_id_type=MESH).start()
        for t in range(n):
            for k, (px, py) in enumerate(chips):
                done = pltpu.make_async_remote_copy(
                    src_ref=_half(ins[t], 2 * px + py, c), dst_ref=_half(outs[t], 2 * px + py, 1 - c),
                    send_sem=send_sems.at[3 * t + k], recv_sem=recv_sems.at[3 * t + k],
                    device_id=(x, y, 1 - c), device_id_type=MESH)
                done.wait_send()
                done.wait_recv()

    return pl.pallas_call(
        body, name=name, in_specs=[ANY] * n, out_specs=[ANY] * n,
        out_shape=[jax.ShapeDtypeStruct(a.shape, a.dtype) for a in bufs],
        input_output_aliases={i: i for i in range(n)},
        scratch_shapes=[pltpu.SemaphoreType.DMA((3 * n,)), pltpu.SemaphoreType.DMA((3 * n,))],
        compiler_params=pltpu.CompilerParams(has_side_effects=True))(*bufs)


def sum_stage_a(grad, recv, place, wire, *, name):
    j_n, half, cols = recv.shape

    def body(place_ref, g_ref, r_ref, o_ref, ob_ref):
        acc = g_ref[...] + r_ref[...]
        o_ref[...] = acc
        ob_ref[...] = acc.astype(wire)

    blk = (None, half, cols)
    return pl.pallas_call(
        body, name=name,
        grid_spec=pltpu.PrefetchScalarGridSpec(
            num_scalar_prefetch=1, grid=(j_n,),
            in_specs=[pl.BlockSpec(blk, lambda j, place_ref: (j, place_ref[1], 0)),
                      pl.BlockSpec(blk, lambda j, place_ref: (j, 0, 0))],
            out_specs=[pl.BlockSpec(blk, lambda j, place_ref: (j, 0, 0))] * 2),
        out_shape=[jax.ShapeDtypeStruct(recv.shape, F32), jax.ShapeDtypeStruct(recv.shape, wire)],
        compiler_params=_params("parallel"))(place, grad, recv)


def _stage_a_copies(srcs, lands, x, y, c):
    out = []
    for src, land in zip(srcs, lands):
        half = src.shape[1] // 2
        out.append((src.at[:, pl.ds((1 - c) * half, half), :], land, (x, y, 1 - c)))
    return out


def _stage_b_copies(srcs, lands, x, y, c):
    out = []
    for src, land in zip(srcs, lands):
        for k, (px, py) in enumerate(_other_chips(x, y)):
            out.append((src.at[2 * px + py], land.at[k], (px, py, c)))
    return out


def split_start(srcs, lands, copies, *, name):
    n = len(srcs)
    n_c = len(copies(srcs, lands, 0, 0, 0))

    def body(*refs):
        src_refs, land_refs = refs[:n], refs[n:2 * n]
        send_sems, recv_sems = refs[2 * n], refs[2 * n + 1]
        token = refs[-1]
        x, y, c = _place()
        for k, (src, dst, target) in enumerate(copies(src_refs, land_refs, x, y, c)):
            pltpu.make_async_remote_copy(src_ref=src, dst_ref=dst, send_sem=send_sems.at[k], recv_sem=recv_sems.at[k],
                                         device_id=target, device_id_type=MESH).start()
        token[...] = jnp.zeros_like(token)

    arrays = list(srcs) + list(lands)
    res = pl.pallas_call(
        body, name=name,
        out_shape=(pltpu.SemaphoreType.DMA((n_c,)), pltpu.SemaphoreType.DMA((n_c,)))
        + tuple(pltpu.HBM(a.shape, a.dtype) for a in arrays) + (jax.ShapeDtypeStruct((8, 128), F32),),
        in_specs=[HBM] * (2 * n),
        out_specs=tuple([SEM, SEM] + [HBM] * (2 * n) + [pl.BlockSpec(memory_space=pltpu.VMEM)]),
        input_output_aliases={i: 2 + i for i in range(2 * n)},
        compiler_params=pltpu.CompilerParams(has_side_effects=DATAFLOW))(*[_in_hbm(a) for a in arrays])
    return res[0], res[1], list(res[2:2 + n]), list(res[2 + n:2 + 2 * n]), res[-1]


def split_wait(srcs, lands, send, recv, copies, after, *, name):
    n = len(srcs)

    def body(*refs):
        src_refs, land_refs = refs[:n], refs[n:2 * n]
        send_sems, recv_sems = refs[2 * n], refs[2 * n + 1]
        x, y, c = _place()
        for k, (src, dst, target) in enumerate(copies(src_refs, land_refs, x, y, c)):
            done = pltpu.make_async_remote_copy(src_ref=src, dst_ref=dst, send_sem=send_sems.at[k],
                                                recv_sem=recv_sems.at[k], device_id=target, device_id_type=MESH)
            done.wait_send()
            done.wait_recv()

    arrays = list(srcs) + list(lands)
    res = pl.pallas_call(
        body, name=name,
        out_shape=tuple(pltpu.HBM(a.shape, a.dtype) for a in arrays),
        in_specs=[HBM] * (2 * n) + [SEM, SEM, ANY],
        out_specs=tuple([HBM] * (2 * n)),
        input_output_aliases={i: i for i in range(2 * n)},
        compiler_params=pltpu.CompilerParams(has_side_effects=DATAFLOW))(*arrays, send, recv, after)
    return list(res[:n]), list(res[n:])


def sum_stage_b(part, recv, place, *, name):
    _, half, cols = part.shape

    def body(place_ref, p_ref, r_ref, o_ref):
        acc = p_ref[...]
        for k in range(3):
            acc = acc + r_ref[k].astype(F32)
        o_ref[...] = acc

    return pl.pallas_call(
        body, name=name,
        grid_spec=pltpu.PrefetchScalarGridSpec(
            num_scalar_prefetch=1, grid=(1,),
            in_specs=[pl.BlockSpec((None, half, cols), lambda i, place_ref: (place_ref[0], 0, 0)),
                      pl.BlockSpec((3, half, cols), lambda i, place_ref: (0, 0, 0))],
            out_specs=pl.BlockSpec((half, cols), lambda i, place_ref: (place_ref[1], 0))),
        out_shape=jax.ShapeDtypeStruct((2 * half, cols), F32),
        compiler_params=_params("arbitrary"))(place, part, recv)


def reduce_stage_c(fulls, *, name):
    n = len(fulls)

    def body(*refs):
        ins, outs = refs[:n], refs[n:2 * n]
        send_sems, recv_sems = refs[2 * n:]
        x, y, c = _place()
        for t in range(n):
            half = ins[t].shape[0] // 2
            pltpu.make_async_remote_copy(
                src_ref=ins[t].at[pl.ds(c * half, half), :], dst_ref=outs[t].at[pl.ds(c * half, half), :],
                send_sem=send_sems.at[t], recv_sem=recv_sems.at[t],
                device_id=(x, y, 1 - c), device_id_type=MESH).start()
        for t in range(n):
            half = ins[t].shape[0] // 2
            done = pltpu.make_async_remote_copy(
                src_ref=ins[t].at[pl.ds(c * half, half), :], dst_ref=outs[t].at[pl.ds((1 - c) * half, half), :],
                send_sem=send_sems.at[t], recv_sem=recv_sems.at[t],
                device_id=(x, y, 1 - c), device_id_type=MESH)
            done.wait_send()
            done.wait_recv()

    return pl.pallas_call(
        body, name=name, in_specs=[ANY] * n, out_specs=[ANY] * n,
        out_shape=[jax.ShapeDtypeStruct(a.shape, a.dtype) for a in fulls],
        input_output_aliases={i: i for i in range(n)},
        scratch_shapes=[pltpu.SemaphoreType.DMA((n,)), pltpu.SemaphoreType.DMA((n,))],
        compiler_params=pltpu.CompilerParams(has_side_effects=True))(*fulls)


def gather_chip_blocks(slots, *, name):
    def body(in_ref, out_ref, send_sems, recv_sems):
        x, y, c = _place()
        me = 2 * x + y
        chips = _other_chips(x, y)
        for k, (px, py) in enumerate(chips):
            pltpu.make_async_remote_copy(
                src_ref=in_ref.at[me], dst_ref=out_ref.at[me],
                send_sem=send_sems.at[k], recv_sem=recv_sems.at[k],
                device_id=(px, py, c), device_id_type=MESH).start()
        for k, (px, py) in enumerate(chips):
            done = pltpu.make_async_remote_copy(
                src_ref=in_ref.at[me], dst_ref=out_ref.at[2 * px + py],
                send_sem=send_sems.at[k], recv_sem=recv_sems.at[k],
                device_id=(px, py, c), device_id_type=MESH)
            done.wait_send()
            done.wait_recv()

    return pl.pallas_call(
        body, name=name, in_specs=[ANY], out_specs=ANY,
        out_shape=jax.ShapeDtypeStruct(slots.shape, slots.dtype),
        input_output_aliases={0: 0},
        scratch_shapes=[pltpu.SemaphoreType.DMA((3,)), pltpu.SemaphoreType.DMA((3,))],
        compiler_params=pltpu.CompilerParams(has_side_effects=True))(slots)


def _ffn_bwd(dh, dh_bf, h, gain, saved, wg, wu, wd, cw, cb, place, l, pin):
    hn, rstd, g, up = saved
    dg, dup, dwg, dwu, dwd, dcw, dcb = ffn_bwd_a(dh_bf, hn, g, up, wd, cw, cb, name=f"ffn{l}_bwd_a", pin=pin)
    red = _reduce_a_begin([(dwg, BF16), (dwu, BF16), (dwd, BF16)], tag=f"f{l}")
    dh_in, dh_in_bf, dgain = dx_rms_bwd([(dg, wg), (dup, wu)], h, gain, rstd, dh, name=f"ffn{l}_bwd_b",
                                        pin=red[-1])
    red = _reduce_b_begin(red, place, dh_in_bf, tag=f"f{l}")
    return dh_in, dh_in_bf, (dcw, dcb, dgain), red


def _reduce_a_begin(grads, *, tag):
    lands = [lax.empty((g.shape[0], g.shape[1] // 2, g.shape[2]), F32) for g, _ in grads]
    send, recv, srcs, lands, token = split_start([g for g, _ in grads], lands, _stage_a_copies,
                                                 name=f"reduce_a_start_{tag}")
    return [w for _, w in grads], send, recv, srcs, lands, token


def _reduce_b_begin(state, place, after, *, tag):
    wires, send, recv, srcs, lands, _ = state
    grads, recv_a = split_wait(srcs, lands, send, recv, _stage_a_copies, after, name=f"reduce_a_wait_{tag}")
    parts = [sum_stage_a(g, r, place, w, name=f"sum_a_{tag}{i}") for i, (g, r, w) in enumerate(zip(grads, recv_a, wires))]
    lands_b = [lax.empty((3,) + p[1].shape[1:], p[1].dtype) for p in parts]
    send, recv, srcs, lands, token = split_start([p[1] for p in parts], lands_b, _stage_b_copies,
                                                 name=f"reduce_b_start_{tag}")
    return [p[0] for p in parts], send, recv, srcs, lands, token


def _reduce_finish(state, place, after, *, tag):
    parts, send, recv, srcs, lands, _ = state
    _, recv_b = split_wait(srcs, lands, send, recv, _stage_b_copies, after, name=f"reduce_b_wait_{tag}")
    halves = [sum_stage_b(p, r, place, name=f"sum_b_{tag}{i}") for i, (p, r) in enumerate(zip(parts, recv_b))]
    return reduce_stage_c(halves, name=f"reduce_c_{tag}")


def kernel(x, norm_mix, norm_ffn, final_norm, w_in_even, conv_a, w_pool, pool_scale, w_out_even, w_in_odd, sgu_norm, w_spatial, b_spatial, w_out_odd, w_ffn_gate, w_ffn_up, conv_ffn, b_conv_ffn, w_ffn_down, loss_target, m_norm_mix, m_norm_ffn, m_final_norm, m_w_in_even, m_conv_a, m_w_pool, m_pool_scale, m_w_out_even, m_w_in_odd, m_sgu_norm, m_w_spatial, m_b_spatial, m_w_out_odd, m_w_ffn_gate, m_w_ffn_up, m_conv_ffn, m_b_conv_ffn, m_w_ffn_down, v_norm_mix, v_norm_ffn, v_final_norm, v_w_in_even, v_conv_a, v_w_pool, v_pool_scale, v_w_out_even, v_w_in_odd, v_sgu_norm, v_w_spatial, v_b_spatial, v_w_out_odd, v_w_ffn_gate, v_w_ffn_up, v_conv_ffn, v_b_conv_ffn, v_w_ffn_down):
    weights = dict(norm_mix=norm_mix, norm_ffn=norm_ffn, final_norm=final_norm, w_in_even=w_in_even,
                   conv_a=conv_a, w_pool=w_pool, pool_scale=pool_scale, w_out_even=w_out_even,
                   w_in_odd=w_in_odd, sgu_norm=sgu_norm, w_spatial=w_spatial, b_spatial=b_spatial,
                   w_out_odd=w_out_odd, w_ffn_gate=w_ffn_gate, w_ffn_up=w_ffn_up, conv_ffn=conv_ffn,
                   b_conv_ffn=b_conv_ffn, w_ffn_down=w_ffn_down)
    m_in = dict(norm_mix=m_norm_mix, norm_ffn=m_norm_ffn, final_norm=m_final_norm, w_in_even=m_w_in_even,
                conv_a=m_conv_a, w_pool=m_w_pool, pool_scale=m_pool_scale, w_out_even=m_w_out_even,
                w_in_odd=m_w_in_odd, sgu_norm=m_sgu_norm, w_spatial=m_w_spatial, b_spatial=m_b_spatial,
                w_out_odd=m_w_out_odd, w_ffn_gate=m_w_ffn_gate, w_ffn_up=m_w_ffn_up, conv_ffn=m_conv_ffn,
                b_conv_ffn=m_b_conv_ffn, w_ffn_down=m_w_ffn_down)
    v_in = dict(norm_mix=v_norm_mix, norm_ffn=v_norm_ffn, final_norm=v_final_norm, w_in_even=v_w_in_even,
                conv_a=v_conv_a, w_pool=v_w_pool, pool_scale=v_pool_scale, w_out_even=v_w_out_even,
                w_in_odd=v_w_in_odd, sgu_norm=v_sgu_norm, w_spatial=v_w_spatial, b_spatial=v_b_spatial,
                w_out_odd=v_w_out_odd, w_ffn_gate=v_w_ffn_gate, w_ffn_up=v_w_ffn_up, conv_ffn=v_conv_ffn,
                b_conv_ffn=v_b_conv_ffn, w_ffn_down=v_w_ffn_down)
    order = list(weights)

    chip = 2 * lax.axis_index("x") + lax.axis_index("y")
    core = lax.axis_index("c")
    place = jnp.stack([chip, core]).astype(jnp.int32)
    chip_arr = place[:1]

    h0 = x[0]
    target = loss_target[0]
    d_model = h0.shape[1]
    f_shard = w_ffn_gate.shape[-1]

    def own_slot(v):
        return lax.dynamic_update_index_in_dim(jnp.zeros((N_CHIPS,) + v.shape, v.dtype), v, chip, 0)

    groups = [
        [cast_into_slot(w_in_even[0], chip_arr, name="cast_win_e"),
         cast_into_slot(w_out_even[0], chip_arr, name="cast_wout_e")],
        [cast_into_slot(w_ffn_gate, chip_arr, l=0, name="cast_wg0"),
         cast_into_slot(w_ffn_up, chip_arr, l=0, name="cast_wu0")],
        [cast_into_slot(w_ffn_down, chip_arr, l=0, name="cast_wd0")],
        [cast_into_slot(w_in_odd[0], chip_arr, name="cast_win_o"),
         cast_into_slot(w_out_odd[0], chip_arr, name="cast_wout_o")],
        [cast_into_slot(w_ffn_gate, chip_arr, l=1, name="cast_wg1"),
         cast_into_slot(w_ffn_up, chip_arr, l=1, name="cast_wu1")],
        [cast_into_slot(w_ffn_down, chip_arr, l=1, name="cast_wd1")]]
    smalls = [own_slot(conv_a[0]), own_slot(sgu_norm), own_slot(conv_ffn[0]), own_slot(conv_ffn[1])]
    sems, groups, smalls, token = gather_start(groups, smalls)

    def arrive(gi, after, with_smalls=False):
        kw = dict(smalls=smalls, small_send=sems[-2], small_recv=sems[-1]) if with_smalls else {}
        bufs, small_out = gather_wait(groups[gi], sems[2 * gi], sems[2 * gi + 1], after, name=f"gather_wait{gi}", **kw)
        return gather_forward(bufs, name=f"gather_forward{gi}"), small_out

    cb = b_conv_ffn.reshape(-1, N_CHIPS, 1, f_shard)
    wp_bf = w_pool[0].astype(BF16)
    wp_t_bf = jnp.transpose(w_pool[0], (0, 2, 1)).astype(BF16)
    ws = w_spatial[0]
    bs = b_spatial[0][:, :, None]

    (win_e, wout_e), (ca_g, sn_g, cw0, cw1) = arrive(0, token, with_smalls=True)
    wout_e = wout_e.reshape(-1, d_model)
    ca_full = jnp.transpose(ca_g, (1, 0, 2)).reshape(ca_g.shape[1], -1)
    sn_full = sn_g.reshape(1, -1)
    h1, xn0, rstd0, proj0, mix0 = even_layer_fwd(h0, norm_mix[0:1], win_e, ca_full, wp_bf, pool_scale, wout_e,
                                                 name="l0_fwd")
    hn0, rstdf0 = rms_fwd(h1, norm_ffn[0:1], name="ffn0_rms")
    (wg0, wu0), _ = arrive(1, hn0)
    g0, up0, act0 = ffn_in_fwd(hn0, wg0, wu0, cw0, cb[0], name="ffn0_in")
    (wd0,), _ = arrive(2, act0)
    h2 = mm_acc(act0, wd0, h1, name="ffn0_down")
    ffn0 = (hn0, rstdf0, g0, up0)
    (win_o, wout_o), _ = arrive(3, h2)
    wout_o = wout_o.reshape(-1, d_model)
    h3, xn1, rstd1, p1, mix1, rstd_v = odd_layer_fwd(h2, norm_mix[1:2], win_o, sn_full, ws, bs, wout_o, name="l1_fwd")
    hn1, rstdf1 = rms_fwd(h3, norm_ffn[1:2], name="ffn1_rms")
    (wg1, wu1), _ = arrive(4, hn1)
    g1, up1, act1 = ffn_in_fwd(hn1, wg1, wu1, cw1, cb[1], name="ffn1_in")
    (wd1,), _ = arrive(5, act1)
    h4 = mm_acc(act1, wd1, h3, name="ffn1_down")
    ffn1 = (hn1, rstdf1, g1, up1)

    dh4, dh4_bf, loss_row, d_final = final_loss(h4, target, final_norm[None], name="loss")
    loss = lax.psum(loss_row[0, 0], ("x", "y", "c"))

    dh3, dh3_bf, (dcw1, dcb1, dnf1), red3 = _ffn_bwd(
        dh4, dh4_bf, h3, norm_ffn[1:2], ffn1, wg1, wu1, wd1, cw1, cb[1], place, 1, None)

    def as_blocks(a):
        return a.reshape(N_CHIPS, -1, d_model)

    dp1, dsn, dws, dbs = sgu_bwd(p1, dh3_bf, wout_o, rstd_v, sn_full, ws, bs, name="l1_mix_bwd", pin=red3[-1])
    dwout_o = as_blocks(mm_tn(mix1[None], dh3_bf[None], name="l1_dwout"))
    dwin_o = mm_tn_shared(xn1, dp1, name="l1_dwin")
    red2 = _reduce_a_begin([(dwin_o, BF16), (dwout_o, BF16)], tag="m1")
    dh2, dh2_bf, dnm1 = dx_rms_bwd([(dp1, win_o)], h2, norm_mix[1:2], rstd1, dh3, name="l1_dx", pin=red2[-1])
    red2 = _reduce_b_begin(red2, place, dh2_bf, tag="m1")

    dh1, dh1_bf, (dcw0, dcb0, dnf0), red1 = _ffn_bwd(
        dh2, dh2_bf, h1, norm_ffn[0:1], ffn0, wg0, wu0, wd0, cw0, cb[0], place, 0, red2[-1])

    dproj0, dca, dwp, dps = even_bwd(proj0, dh1_bf, wout_e, ca_full, wp_bf, wp_t_bf, pool_scale,
                                     name="l0_mix_bwd", pin=red1[-1])
    dwout_e = as_blocks(mm_tn(mix0[None], dh1_bf[None], name="l0_dwout"))
    dwin_e = mm_tn_shared(xn0, dproj0, name="l0_dwin")
    dh0, _, dnm0 = dx_rms_bwd([(dproj0, win_e)], h0, norm_mix[0:1], rstd0, dh1, name="l0_dx")
    grad_x = dh0[None]

    small_parts = dict(
        norm_mix=jnp.concatenate([dnm0, dnm1]), norm_ffn=jnp.concatenate([dnf0, dnf1]), final_norm=d_final,
        conv_a=dca, w_pool=dwp, pool_scale=dps, sgu_norm=dsn, w_spatial=dws, b_spatial=dbs,
        conv_ffn=jnp.stack([dcw0, dcw1]), b_conv_ffn=jnp.stack([dcb0, dcb1]))
    flat = jnp.concatenate([v.reshape(-1) for v in small_parts.values()])
    pad = (-flat.shape[0]) % (N_CHIPS * 32 * 128)
    small = jnp.pad(flat, (0, pad)).reshape(N_CHIPS, -1, 128)
    red0 = _reduce_a_begin([(dwin_e, BF16), (dwout_e, BF16), (small, F32)], tag="m0")

    full3 = _reduce_finish(red3, place, red0[-1], tag="f1")
    red0 = _reduce_b_begin(red0, place, full3[0], tag="m0")
    full2 = _reduce_finish(red2, place, red0[-1], tag="m1")
    full1 = _reduce_finish(red1, place, full2[0], tag="f0")
    full0 = _reduce_finish(red0, place, full1[0], tag="m0")
    small_slots = lax.dynamic_update_index_in_dim(jnp.zeros(small.shape, F32), full0[2], chip, 0)
    small_sum = gather_chip_blocks(small_slots, name="gather_small").reshape(-1)
    grads = {
        "w_in_even": full0[0][None], "w_out_even": full0[1][None],
        "w_in_odd": full2[0][None], "w_out_odd": full2[1][None],
        }
    layered = {"w_ffn_gate": [full1[0], full3[0]], "w_ffn_up": [full1[1], full3[1]],
               "w_ffn_down": [full1[2], full3[2]]}
    off = 0
    small_red = {}
    for nm, v in small_parts.items():
        small_red[nm] = small_sum[off:off + v.size].reshape(v.shape)
        off += v.size
    for nm in ("norm_mix", "norm_ffn", "pool_scale"):
        grads[nm] = small_red[nm].reshape(weights[nm].shape)
    grads["final_norm"] = small_red["final_norm"].reshape(weights["final_norm"].shape)
    grads["w_pool"] = small_red["w_pool"][None]
    grads["w_spatial"] = small_red["w_spatial"][None]
    grads["b_spatial"] = small_red["b_spatial"].reshape(weights["b_spatial"].shape)
    grads["b_conv_ffn"] = small_red["b_conv_ffn"].reshape(weights["b_conv_ffn"].shape)
    grads["conv_a"] = lax.dynamic_slice_in_dim(small_red["conv_a"], chip * conv_a.shape[-1], conv_a.shape[-1], 1)[None]
    grads["sgu_norm"] = lax.dynamic_slice_in_dim(small_red["sgu_norm"], chip * sgu_norm.shape[-1], sgu_norm.shape[-1], 1)
    grads["conv_ffn"] = lax.dynamic_index_in_dim(small_red["conv_ffn"], chip, 1, keepdims=False)

    deltas, new_m, new_v = {}, {}, {}
    for nm, per_layer in layered.items():
        grads[nm], deltas[nm], new_m[nm], new_v[nm] = adamw_layers(
            weights[nm], per_layer, m_in[nm], v_in[nm], name=f"adamw_{nm}")
    for nm in order:
        if nm in layered:
            continue
        w = weights[nm]
        w2 = w[None] if w.ndim == 1 else w
        shp = w2.shape
        d, nm_, nv_ = adamw(w2, grads[nm].reshape(shp), m_in[nm].reshape(shp), v_in[nm].reshape(shp),
                            name=f"adamw_{nm}")
        deltas[nm], new_m[nm], new_v[nm] = d.reshape(w.shape), nm_.reshape(w.shape), nv_.reshape(w.shape)

    return (loss, grad_x, *[grads[n] for n in order], *[deltas[n] for n in order],
            *[new_m[n] for n in order], *[new_v[n] for n in order])
```

```python
import jax
import jax.numpy as jnp
from jax import lax
from jax.experimental import pallas as pl
from jax.experimental.pallas import tpu as pltpu

F32 = jnp.float32
BF16 = jnp.bfloat16
MESH = pl.DeviceIdType.MESH
ANY = pl.BlockSpec(memory_space=pl.ANY)

EPS = 1e-6
POOL_WINDOWS = (2, 4, 8, 16)
GROUP = 128
CHUNK = 128
N_CHIPS = 4
N_DEV = 8
ROW_TILE = 512
HALO = 16
VMEM_LIMIT = 56 * 1024 * 1024

ADAM_LR = 0.001
ADAM_B1 = 0.9
ADAM_B2 = 0.999
ADAM_EPS = 1e-08
ADAM_WD = 0.01
ADAM_STEP = 10


def _params(*sem):
    return pltpu.CompilerParams(dimension_semantics=sem, vmem_limit_bytes=VMEM_LIMIT)


def _layer_spec(block, l, idx):
    if l is None:
        return pl.BlockSpec(block, idx)
    return pl.BlockSpec((None,) + block, lambda *g: (l,) + idx(*g))


def mm_nn(a, b, *, l=None, name):
    s, k = a.shape
    j_n, n = b.shape[-3], b.shape[-1]
    tm = min(ROW_TILE, s)

    def body(a_ref, b_ref, o_ref):
        o_ref[...] = jnp.dot(a_ref[...], b_ref[...], preferred_element_type=F32)

    return pl.pallas_call(
        body, name=name, grid=(j_n, s // tm),
        in_specs=[pl.BlockSpec((tm, k), lambda j, i: (i, 0)),
                  _layer_spec((None, k, n), l, lambda j, i: (j, 0, 0))],
        out_specs=pl.BlockSpec((None, tm, n), lambda j, i: (j, i, 0)),
        out_shape=jax.ShapeDtypeStruct((j_n, s, n), F32),
        compiler_params=_params("parallel", "parallel"))(a, b)


def mm_acc(a, b, res, *, l=None, name):
    j_n, s, kj = a.shape
    n = b.shape[-1]
    tm = min(ROW_TILE, s)

    def body(a_ref, b_ref, r_ref, o_ref):
        acc = r_ref[...]
        for j in range(j_n):
            acc = acc + jnp.dot(a_ref[j], b_ref[j], preferred_element_type=F32)
        o_ref[...] = acc

    return pl.pallas_call(
        body, name=name, grid=(s // tm,),
        in_specs=[pl.BlockSpec((j_n, tm, kj), lambda i: (0, i, 0)),
                  _layer_spec((j_n, kj, n), l, lambda i: (0, 0, 0)),
                  pl.BlockSpec((tm, n), lambda i: (i, 0))],
        out_specs=pl.BlockSpec((tm, n), lambda i: (i, 0)),
        out_shape=jax.ShapeDtypeStruct((s, n), F32),
        compiler_params=_params("parallel"))(a, b, res)


_NT = (((1,), (1,)), ((), ()))
_TN = (((0,), (0,)), ((), ()))


def mm_nt_sum(pairs, *, l=None, name):
    j_n, s, nj = pairs[0][0].shape
    k = pairs[0][1].shape[-2]
    tm = min(ROW_TILE, s)
    n_p = len(pairs)

    def body(*refs):
        o_ref = refs[-1]
        acc = jnp.zeros((tm, k), F32)
        for p in range(n_p):
            dy_ref, w_ref = refs[2 * p], refs[2 * p + 1]
            for j in range(j_n):
                acc = acc + lax.dot_general(dy_ref[j], w_ref[j], _NT, preferred_element_type=F32)
        o_ref[...] = acc

    in_specs, args = [], []
    for dy, w in pairs:
        in_specs += [pl.BlockSpec((j_n, tm, nj), lambda i: (0, i, 0)),
                     _layer_spec((j_n, k, nj), l, lambda i: (0, 0, 0))]
        args += [dy, w]
    return pl.pallas_call(
        body, name=name, grid=(s // tm,), in_specs=in_specs,
        out_specs=pl.BlockSpec((tm, k), lambda i: (i, 0)),
        out_shape=jax.ShapeDtypeStruct((s, k), F32),
        compiler_params=_params("parallel"))(*args)


def mm_nt_each(a, b, *, l=None, name, pin=None):
    s, k = a.shape
    j_n, nj = b.shape[-3], b.shape[-2]
    tm = min(ROW_TILE, s)

    def body(a_ref, b_ref, *rest):
        rest[-1][...] = lax.dot_general(a_ref[...], b_ref[...], _NT, preferred_element_type=F32)

    return pl.pallas_call(
        body, name=name, grid=(j_n, s // tm),
        in_specs=[pl.BlockSpec((tm, k), lambda j, i: (i, 0)),
                  _layer_spec((None, nj, k), l, lambda j, i: (j, 0, 0))] + ([ANY] if pin is not None else []),
        out_specs=pl.BlockSpec((None, tm, nj), lambda j, i: (j, i, 0)),
        out_shape=jax.ShapeDtypeStruct((j_n, s, nj), F32),
        compiler_params=_params("parallel", "parallel"))(a, b, *([pin] if pin is not None else []))


def mm_tn(a, dy, *, name, pin=None):
    ja, s, k = a.shape
    jd, _, n = dy.shape
    j_n = max(ja, jd)
    tk = min(ROW_TILE, s)

    def body(a_ref, d_ref, *rest):
        o_ref = rest[-1]

        @pl.when(pl.program_id(1) == 0)
        def _():
            o_ref[...] = jnp.zeros_like(o_ref)
        o_ref[...] += lax.dot_general(a_ref[...], d_ref[...], _TN, preferred_element_type=F32)

    return pl.pallas_call(
        body, name=name, grid=(j_n, s // tk),
        in_specs=[pl.BlockSpec((None, tk, k), (lambda j, i: (j, i, 0)) if ja > 1 else (lambda j, i: (0, i, 0))),
                  pl.BlockSpec((None, tk, n), (lambda j, i: (j, i, 0)) if jd > 1 else (lambda j, i: (0, i, 0)))]
        + ([ANY] if pin is not None else []),
        out_specs=pl.BlockSpec((None, k, n), lambda j, i: (j, 0, 0)),
        out_shape=jax.ShapeDtypeStruct((j_n, k, n), F32),
        compiler_params=_params("parallel", "arbitrary"))(a, dy, *([pin] if pin is not None else []))


def mm_tn_shared(a, dy, *, name):
    s, k = a.shape
    j_n, _, n = dy.shape
    tk = min(ROW_TILE, s)

    def body(a_ref, d_ref, o_ref):
        @pl.when(pl.program_id(0) == 0)
        def _():
            o_ref[...] = jnp.zeros_like(o_ref)
        a_t = a_ref[...]
        for j in range(j_n):
            o_ref[j] += lax.dot_general(a_t, d_ref[j], _TN, preferred_element_type=F32)

    return pl.pallas_call(
        body, name=name, grid=(s // tk,),
        in_specs=[pl.BlockSpec((tk, k), lambda i: (i, 0)), pl.BlockSpec((j_n, tk, n), lambda i: (0, i, 0))],
        out_specs=pl.BlockSpec((j_n, k, n), lambda i: (0, 0, 0)),
        out_shape=jax.ShapeDtypeStruct((j_n, k, n), F32),
        compiler_params=_params("arbitrary"))(a, dy)


def _back(x, k):
    return pltpu.roll(x, k, 0)


def _fwd(x, k):
    return pltpu.roll(x, x.shape[0] - k, 0)


def _causal_conv(x, w_ref):
    return w_ref[0:1, :] * _back(x, 2) + w_ref[1:2, :] * _back(x, 1) + w_ref[2:3, :] * x


def _causal_conv_t(dy, w_ref):
    return w_ref[2:3, :] * dy + w_ref[1:2, :] * _fwd(dy, 1) + w_ref[0:1, :] * _fwd(dy, 2)


def _gelu(x):
    return 0.5 * x * (1.0 + lax.erf(x * 0.7071067811865476))


def _gelu_grad(x):
    return 0.5 * (1.0 + lax.erf(x * 0.7071067811865476)) + x * jnp.exp(-0.5 * x * x) * 0.3989422804014327


def _colsum(x):
    return jnp.sum(x, axis=0, keepdims=True)


def rms_fwd(h, gain, *, name, pin=None):
    s, d = h.shape
    ts = min(ROW_TILE, s)

    def body(h_ref, g_ref, *rest):
        o_ref, r_ref = rest[-2:]
        x = h_ref[...]
        rstd = lax.rsqrt(jnp.mean(x * x, axis=-1, keepdims=True) + EPS)
        o_ref[...] = (x * rstd * g_ref[...]).astype(BF16)
        r_ref[...] = rstd

    return pl.pallas_call(
        body, name=name, grid=(s // ts,),
        in_specs=[pl.BlockSpec((ts, d), lambda i: (i, 0)), pl.BlockSpec((1, d), lambda i: (0, 0))]
        + ([ANY] if pin is not None else []),
        out_specs=[pl.BlockSpec((ts, d), lambda i: (i, 0)), pl.BlockSpec((ts, 1), lambda i: (i, 0))],
        out_shape=[jax.ShapeDtypeStruct((s, d), BF16), jax.ShapeDtypeStruct((s, 1), F32)],
        compiler_params=_params("parallel"))(h, gain, *([pin] if pin is not None else []))


def rms_bwd(dxn, h, gain, rstd, dres, *, name):
    s, d = h.shape
    ts = min(ROW_TILE, s)

    def body(dx_ref, h_ref, g_ref, r_ref, dr_ref, o_ref, ob_ref, dg_ref):
        @pl.when(pl.program_id(0) == 0)
        def _():
            dg_ref[...] = jnp.zeros_like(dg_ref)
        rstd_v = r_ref[...]
        hhat = h_ref[...] * rstd_v
        dx = dx_ref[...]
        dg_ref[...] += _colsum(dx * hhat)
        dxg = dx * g_ref[...]
        dh = dr_ref[...] + rstd_v * (dxg - hhat * jnp.mean(dxg * hhat, axis=-1, keepdims=True))
        o_ref[...] = dh
        ob_ref[...] = dh.astype(BF16)

    row = pl.BlockSpec((ts, d), lambda i: (i, 0))
    vec = pl.BlockSpec((1, d), lambda i: (0, 0))
    return pl.pallas_call(
        body, name=name, grid=(s // ts,),
        in_specs=[row, row, vec, pl.BlockSpec((ts, 1), lambda i: (i, 0)), row],
        out_specs=[row, row, vec],
        out_shape=[jax.ShapeDtypeStruct((s, d), F32), jax.ShapeDtypeStruct((s, d), BF16),
                   jax.ShapeDtypeStruct((1, d), F32)],
        compiler_params=_params("arbitrary"))(dxn, h, gain, rstd, dres)


def final_loss(h, target, gain, *, name):
    s, d = h.shape
    ts = min(ROW_TILE, s)

    def body(h_ref, t_ref, g_ref, o_ref, ob_ref, l_ref, dg_ref):
        @pl.when(pl.program_id(0) == 0)
        def _():
            l_ref[...] = jnp.zeros_like(l_ref)
            dg_ref[...] = jnp.zeros_like(dg_ref)
        x = h_ref[...]
        rstd = lax.rsqrt(jnp.mean(x * x, axis=-1, keepdims=True) + EPS)
        hhat = x * rstd
        err = hhat * g_ref[...] - t_ref[...]
        l_ref[...] += 0.5 * jnp.sum(jnp.mean(err * err, axis=-1, keepdims=True), axis=0, keepdims=True)
        dy = err * (1.0 / d)
        dg_ref[...] += _colsum(dy * hhat)
        dyg = dy * g_ref[...]
        dh = rstd * (dyg - hhat * jnp.mean(dyg * hhat, axis=-1, keepdims=True))
        o_ref[...] = dh
        ob_ref[...] = dh.astype(BF16)

    row = pl.BlockSpec((ts, d), lambda i: (i, 0))
    vec = pl.BlockSpec((1, d), lambda i: (0, 0))
    return pl.pallas_call(
        body, name=name, grid=(s // ts,),
        in_specs=[row, row, vec],
        out_specs=[row, row, pl.BlockSpec((1, 128), lambda i: (0, 0)), vec],
        out_shape=[jax.ShapeDtypeStruct((s, d), F32), jax.ShapeDtypeStruct((s, d), BF16),
                   jax.ShapeDtypeStruct((1, 128), F32), jax.ShapeDtypeStruct((1, d), F32)],
        compiler_params=_params("arbitrary"))(h, target, gain)


def _halo_specs(n_lead, ts, width, n_tiles):
    hb = ts // HALO
    prev = pl.BlockSpec((n_lead, HALO, width), lambda i: (0, jnp.maximum(i * hb - 1, 0), 0))
    nxt = pl.BlockSpec((n_lead, HALO, width), lambda i: (0, jnp.minimum((i + 1) * hb, n_tiles * hb - 1), 0))
    return prev, nxt


def _pool_fwd(z_ext, g, pos):
    w = POOL_WINDOWS[g]
    zg = z_ext[:, g * GROUP:(g + 1) * GROUP]
    acc = zg
    sh = 1
    while sh < w:
        acc = acc + _back(acc, sh)
        sh *= 2
    return acc[HALO:] / jnp.minimum(pos, float(w)) - zg[HALO:]


def even_layer_fwd(h, gain, win, conv_a, w_pool, pool_scale, wout, *, name, pin=None):
    s, d = h.shape
    w = win.shape[-1]
    ts = min(ROW_TILE, s)
    hb = ts // HALO

    def body(h_ref, hp_ref, gain_ref, win_ref, ca_ref, wp_ref, ps_ref, wout_ref, *rest):
        o_ref, xn_ref, r_ref, p_ref, m_ref = rest[-5:]
        i = pl.program_id(0)
        keep = jnp.where(i > 0, 1.0, 0.0)
        h_ext = jnp.concatenate([hp_ref[...], h_ref[...]], axis=0)
        rstd = lax.rsqrt(jnp.mean(h_ext * h_ext, axis=-1, keepdims=True) + EPS)
        xn_ext = (h_ext * rstd * gain_ref[...]).astype(BF16)
        xn_ref[...] = xn_ext[HALO:]
        r_ref[...] = rstd[HALO:]
        p32 = []
        for k in range(4):
            pk = jnp.dot(xn_ext, win_ref[k], preferred_element_type=F32).astype(BF16)
            p_ref[k] = pk[HALO:]
            pk = pk.astype(F32)
            p32.append(jnp.concatenate([pk[:HALO] * keep, pk[HALO:]], axis=0))
        m_ref[:, 0:w] = (p32[0][HALO:] * _causal_conv(p32[1] * p32[2], ca_ref)[HALO:]).astype(BF16)
        pos = (i * ts + lax.broadcasted_iota(jnp.int32, (ts, 1), 0) + 1).astype(F32)
        for g in range(len(POOL_WINDOWS)):
            pooled = _pool_fwd(p32[3], g, pos)
            mixed = jnp.dot(pooled.astype(BF16), wp_ref[g], preferred_element_type=F32)
            cols = slice(g * GROUP, (g + 1) * GROUP)
            m_ref[:, w + g * GROUP:w + (g + 1) * GROUP] = (mixed * ps_ref[:, cols]).astype(BF16)
        o_ref[...] = h_ref[...] + jnp.dot(m_ref[...], wout_ref[...], preferred_element_type=F32)

    const = lambda shape: pl.BlockSpec(shape, lambda i: (0,) * len(shape))
    row = pl.BlockSpec((ts, d), lambda i: (i, 0))
    return pl.pallas_call(
        body, name=name, grid=(s // ts,),
        in_specs=[row, pl.BlockSpec((HALO, d), lambda i: (jnp.maximum(i * hb - 1, 0), 0)), const((1, d)),
                  const((4, d, w)), const((3, w)), const((4, GROUP, GROUP)), const((1, w)), const((2 * w, d))]
        + ([ANY] if pin is not None else []),
        out_specs=[row, row, pl.BlockSpec((ts, 1), lambda i: (i, 0)),
                   pl.BlockSpec((4, ts, w), lambda i: (0, i, 0)), pl.BlockSpec((ts, 2 * w), lambda i: (i, 0))],
        out_shape=[jax.ShapeDtypeStruct((s, d), F32), jax.ShapeDtypeStruct((s, d), BF16),
                   jax.ShapeDtypeStruct((s, 1), F32), jax.ShapeDtypeStruct((4, s, w), BF16),
                   jax.ShapeDtypeStruct((s, 2 * w), BF16)],
        compiler_params=_params("parallel"))(
            h, h, gain, win, conv_a, w_pool, pool_scale, wout, *([pin] if pin is not None else []))


def even_bwd(proj, dh_bf, wout, conv_a, w_pool, w_pool_t, pool_scale, *, name, pin=None):
    _, s, w = proj.shape
    d = dh_bf.shape[1]
    ts = min(ROW_TILE, s)
    n_t = s // ts
    prev, nxt = _halo_specs(4, ts, w, n_t)
    hb = ts // HALO
    n_ext = ts + HALO

    def body(p_ref, pp_ref, pn_ref, dh_ref, dhn_ref, wout_ref, ca_ref, wp_ref, wpt_ref, ps_ref, *rest):
        dp_ref, dca_ref, dwp_ref, dps_ref = rest[-4:]
        i = pl.program_id(0)

        @pl.when(i == 0)
        def _():
            dca_ref[...] = jnp.zeros_like(dca_ref)
            dwp_ref[...] = jnp.zeros_like(dwp_ref)
            dps_ref[...] = jnp.zeros_like(dps_ref)

        keep_p = jnp.where(i > 0, 1.0, 0.0)
        keep_n = jnp.where(i < n_t - 1, 1.0, 0.0)
        dmix = lax.dot_general(jnp.concatenate([dh_ref[...], dhn_ref[...]], axis=0), wout_ref[...], _NT,
                               preferred_element_type=F32)
        a_b, a_c, a_v = (p_ref[k].astype(F32) for k in range(3))
        cv_ext = jnp.concatenate([pp_ref[1].astype(F32) * pp_ref[2].astype(F32) * keep_p, a_c * a_v], axis=0)
        dy_a = dmix[:ts, 0:w]
        dp_ref[0] = (dy_a * _causal_conv(cv_ext, ca_ref)[HALO:]).astype(BF16)
        dcc = dy_a * a_b
        dca_ref[2:3, :] += _colsum(dcc * cv_ext[HALO:])
        dca_ref[1:2, :] += _colsum(dcc * _back(cv_ext, 1)[HALO:])
        dca_ref[0:1, :] += _colsum(dcc * _back(cv_ext, 2)[HALO:])
        dcc_ext = jnp.concatenate([dcc, dmix[ts:, 0:w] * pn_ref[0].astype(F32) * keep_n], axis=0)
        dcv = _causal_conv_t(dcc_ext, ca_ref)[:ts]
        dp_ref[1] = (dcv * a_v).astype(BF16)
        dp_ref[2] = (dcv * a_c).astype(BF16)
        z_ext = jnp.concatenate([pp_ref[3].astype(F32) * keep_p, p_ref[3].astype(F32)], axis=0)
        pos = (i * ts + lax.broadcasted_iota(jnp.int32, (ts, 1), 0) + 1).astype(F32)
        pos_ext = (i * ts + lax.broadcasted_iota(jnp.int32, (n_ext, 1), 0) + 1).astype(F32)
        for g, win in enumerate(POOL_WINDOWS):
            cols = slice(g * GROUP, (g + 1) * GROUP)
            ycols = slice(w + g * GROUP, w + (g + 1) * GROUP)
            pooled = _pool_fwd(z_ext, g, pos).astype(BF16)
            mixed = jnp.dot(pooled, wp_ref[g], preferred_element_type=F32)
            dy_b = dmix[:ts, ycols]
            dps_ref[:, cols] += _colsum(dy_b * mixed)
            dmixed_ext = jnp.concatenate([dy_b, dmix[ts:, ycols] * keep_n], axis=0) * ps_ref[:, cols]
            dmixed_ext = dmixed_ext.astype(BF16)
            dwp_ref[g] += lax.dot_general(pooled, dmixed_ext[:ts], _TN, preferred_element_type=F32)
            dpooled = jnp.dot(dmixed_ext, wpt_ref[g], preferred_element_type=F32)
            acc = dpooled / jnp.minimum(pos_ext, float(win))
            sh = 1
            while sh < win:
                acc = acc + _fwd(acc, sh)
                sh *= 2
            dp_ref[3, :, cols] = (acc[:ts] - dpooled[:ts]).astype(BF16)

    tile4 = pl.BlockSpec((4, ts, w), lambda i: (0, i, 0))
    const = lambda shape: pl.BlockSpec(shape, lambda i: (0,) * len(shape))
    return pl.pallas_call(
        body, name=name, grid=(n_t,),
        in_specs=[tile4, prev, nxt, pl.BlockSpec((ts, d), lambda i: (i, 0)),
                  pl.BlockSpec((HALO, d), lambda i: (jnp.minimum((i + 1) * hb, n_t * hb - 1), 0)),
                  const((2 * w, d)), const((3, w)), const((4, GROUP, GROUP)), const((4, GROUP, GROUP)), const((1, w))]
        + ([ANY] if pin is not None else []),
        out_specs=[tile4, const((3, w)), const((4, GROUP, GROUP)), const((1, w))],
        out_shape=[jax.ShapeDtypeStruct((4, s, w), BF16), jax.ShapeDtypeStruct((3, w), F32),
                   jax.ShapeDtypeStruct((4, GROUP, GROUP), F32), jax.ShapeDtypeStruct((1, w), F32)],
        compiler_params=_params("arbitrary"))(
            proj, proj, proj, dh_bf, dh_bf, wout, conv_a, w_pool, w_pool_t, pool_scale,
            *([pin] if pin is not None else []))


def _ffn_halo(ts, f, n_t, nxt):
    hb = ts // HALO
    if nxt:
        return pl.BlockSpec((None, HALO, f), lambda j, i: (j, jnp.minimum((i + 1) * hb, n_t * hb - 1), 0))
    return pl.BlockSpec((None, HALO, f), lambda j, i: (j, jnp.maximum(i * hb - 1, 0), 0))


def ffn_act_fwd(g, up, cw, cb, *, name):
    j_n, s, f = g.shape
    ts = min(ROW_TILE, s)
    n_t = s // ts

    def body(g_ref, gp_ref, u_ref, cw_ref, cb_ref, o_ref):
        keep = jnp.where(pl.program_id(1) > 0, 1.0, 0.0)
        g_ext = jnp.concatenate([gp_ref[...] * keep, g_ref[...]], axis=0)
        gc = _causal_conv(g_ext, cw_ref)[HALO:] + cb_ref[...]
        o_ref[...] = (gc * jax.nn.sigmoid(gc) * u_ref[...]).astype(BF16)

    tile = pl.BlockSpec((None, ts, f), lambda j, i: (j, i, 0))
    return pl.pallas_call(
        body, name=name, grid=(j_n, n_t),
        in_specs=[tile, _ffn_halo(ts, f, n_t, False), tile,
                  pl.BlockSpec((None, 3, f), lambda j, i: (j, 0, 0)),
                  pl.BlockSpec((None, 1, f), lambda j, i: (j, 0, 0))],
        out_specs=tile,
        out_shape=jax.ShapeDtypeStruct((j_n, s, f), BF16),
        compiler_params=_params("parallel", "parallel"))(g, g, up, cw, cb)


def ffn_act_bwd(g, up, dact, cw, cb, *, name):
    j_n, s, f = g.shape
    ts = min(ROW_TILE, s)
    n_t = s // ts

    def body(g_ref, gp_ref, gn_ref, u_ref, un_ref, d_ref, dn_ref, cw_ref, cb_ref,
             dg_ref, du_ref, dcw_ref, dcb_ref):
        i = pl.program_id(1)

        @pl.when(i == 0)
        def _():
            dcw_ref[...] = jnp.zeros_like(dcw_ref)
            dcb_ref[...] = jnp.zeros_like(dcb_ref)

        keep_p = jnp.where(i > 0, 1.0, 0.0)
        keep_n = jnp.where(i < n_t - 1, 1.0, 0.0)
        g_ext = jnp.concatenate([gp_ref[...] * keep_p, g_ref[...], gn_ref[...]], axis=0)
        gc = _causal_conv(g_ext, cw_ref)[HALO:] + cb_ref[...]
        sig = jax.nn.sigmoid(gc)
        dact_ext = jnp.concatenate([d_ref[...], dn_ref[...] * keep_n], axis=0)
        du_ref[...] = (dact_ext * gc * sig)[:ts].astype(BF16)
        up_ext = jnp.concatenate([u_ref[...], un_ref[...]], axis=0)
        dgc = dact_ext * up_ext * (sig * (1.0 + gc * (1.0 - sig)))
        dg_ref[...] = _causal_conv_t(dgc, cw_ref)[:ts].astype(BF16)
        dgc_t = dgc[:ts]
        dcb_ref[...] += _colsum(dgc_t)
        dcw_ref[2:3, :] += _colsum(dgc_t * g_ext[HALO:HALO + ts])
        dcw_ref[1:2, :] += _colsum(dgc_t * _back(g_ext, 1)[HALO:HALO + ts])
        dcw_ref[0:1, :] += _colsum(dgc_t * _back(g_ext, 2)[HALO:HALO + ts])

    tile = pl.BlockSpec((None, ts, f), lambda j, i: (j, i, 0))
    prev, nxt = _ffn_halo(ts, f, n_t, False), _ffn_halo(ts, f, n_t, True)
    return pl.pallas_call(
        body, name=name, grid=(j_n, n_t),
        in_specs=[tile, prev, nxt, tile, nxt, tile, nxt,
                  pl.BlockSpec((None, 3, f), lambda j, i: (j, 0, 0)),
                  pl.BlockSpec((None, 1, f), lambda j, i: (j, 0, 0))],
        out_specs=[tile, tile, pl.BlockSpec((None, 3, f), lambda j, i: (j, 0, 0)),
                   pl.BlockSpec((None, 1, f), lambda j, i: (j, 0, 0))],
        out_shape=[jax.ShapeDtypeStruct((j_n, s, f), BF16), jax.ShapeDtypeStruct((j_n, s, f), BF16),
                   jax.ShapeDtypeStruct((j_n, 3, f), F32), jax.ShapeDtypeStruct((j_n, 1, f), F32)],
        compiler_params=_params("parallel", "arbitrary"))(g, g, g, up, up, dact, dact, cw, cb)


def ffn_in_fwd(hn, wg, wu, cw, cb, *, name):
    s, d = hn.shape
    j_n, f, _ = wg.shape
    tm = min(ROW_TILE, s)
    hb = tm // HALO

    def body(x_ref, xp_ref, wg_ref, wu_ref, cw_ref, cb_ref, g_ref, u_ref, a_ref):
        i, j = pl.program_id(0), pl.program_id(1)
        x_ext = jnp.concatenate([xp_ref[...], x_ref[...]], axis=0)
        g_ext = lax.dot_general(x_ext, wg_ref[j], _NT, preferred_element_type=F32).astype(BF16)
        up = lax.dot_general(x_ref[...], wu_ref[j], _NT, preferred_element_type=F32).astype(BF16)
        g_ref[...] = g_ext[HALO:]
        u_ref[...] = up
        keep = jnp.where(i > 0, 1.0, 0.0)
        g32 = jnp.concatenate([g_ext[:HALO].astype(F32) * keep, g_ext[HALO:].astype(F32)], axis=0)
        gc = _causal_conv(g32, cw_ref)[HALO:] + cb_ref[...]
        a_ref[...] = (gc * jax.nn.sigmoid(gc) * up.astype(F32)).astype(BF16)

    whole = pl.BlockSpec((j_n, f, d), lambda i, j: (0, 0, 0))
    tile = pl.BlockSpec((None, tm, f), lambda i, j: (j, i, 0))
    shape = jax.ShapeDtypeStruct((j_n, s, f), BF16)
    return pl.pallas_call(
        body, name=name, grid=(s // tm, j_n),
        in_specs=[pl.BlockSpec((tm, d), lambda i, j: (i, 0)),
                  pl.BlockSpec((HALO, d), lambda i, j: (jnp.maximum(i * hb - 1, 0), 0)),
                  whole, whole,
                  pl.BlockSpec((None, 3, f), lambda i, j: (j, 0, 0)),
                  pl.BlockSpec((None, 1, f), lambda i, j: (j, 0, 0))],
        out_specs=[tile, tile, tile], out_shape=[shape, shape, shape],
        compiler_params=_params("parallel", "parallel"))(hn, hn, wg, wu, cw, cb)


def ffn_bwd_a(dh_bf, hn, g, up, wd, cw, cb, *, name, pin=None):
    s, d = hn.shape
    j_n, _, f = g.shape
    tm = min(ROW_TILE, s)
    n_t = s // tm
    hb = tm // HALO

    def body(dh_ref, dhn_ref, x_ref, g_ref, gp_ref, gn_ref, u_ref, un_ref, wd_ref, cw_ref, cb_ref, *rest):
        dg_ref, du_ref, dwg_ref, dwu_ref, dwd_ref, dcw_ref, dcb_ref = rest[-7:]
        i = pl.program_id(1)

        @pl.when(i == 0)
        def _():
            for r in (dwg_ref, dwu_ref, dwd_ref, dcw_ref, dcb_ref):
                r[...] = jnp.zeros_like(r)

        keep_p = jnp.where(i > 0, 1.0, 0.0)
        keep_n = jnp.where(i < n_t - 1, 1.0, 0.0)
        dh = dh_ref[...]
        dact = lax.dot_general(jnp.concatenate([dh, dhn_ref[...]], axis=0), wd_ref[...], _NT,
                               preferred_element_type=F32)
        dact = jnp.concatenate([dact[:tm], dact[tm:] * keep_n], axis=0)
        g_ext = jnp.concatenate([gp_ref[...].astype(F32) * keep_p, g_ref[...].astype(F32),
                                 gn_ref[...].astype(F32)], axis=0)
        gc = _causal_conv(g_ext, cw_ref)[HALO:] + cb_ref[...]
        sig = jax.nn.sigmoid(gc)
        silu = gc * sig
        up_ext = jnp.concatenate([u_ref[...], un_ref[...]], axis=0).astype(F32)
        act = (silu * up_ext)[:tm].astype(BF16)
        dwd_ref[...] += lax.dot_general(act, dh, _TN, preferred_element_type=F32)
        dup = (dact * silu)[:tm].astype(BF16)
        du_ref[...] = dup
        dgc = dact * up_ext * (sig * (1.0 + gc * (1.0 - sig)))
        dg = _causal_conv_t(dgc, cw_ref)[:tm].astype(BF16)
        dg_ref[...] = dg
        x = x_ref[...]
        dwg_ref[...] += lax.dot_general(dg, x, _TN, preferred_element_type=F32)
        dwu_ref[...] += lax.dot_general(dup, x, _TN, preferred_element_type=F32)
        dgc_t = dgc[:tm]
        dcb_ref[...] += _colsum(dgc_t)
        dcw_ref[2:3, :] += _colsum(dgc_t * g_ext[HALO:HALO + tm])
        dcw_ref[1:2, :] += _colsum(dgc_t * _back(g_ext, 1)[HALO:HALO + tm])
        dcw_ref[0:1, :] += _colsum(dgc_t * _back(g_ext, 2)[HALO:HALO + tm])

    rows = pl.BlockSpec((tm, d), lambda j, i: (i, 0))
    rows_next = pl.BlockSpec((HALO, d), lambda j, i: (jnp.minimum((i + 1) * hb, n_t * hb - 1), 0))
    tile = pl.BlockSpec((None, tm, f), lambda j, i: (j, i, 0))
    prev, nxt = _ffn_halo(tm, f, n_t, False), _ffn_halo(tm, f, n_t, True)
    per_j = lambda r, c: pl.BlockSpec((None, r, c), lambda j, i: (j, 0, 0))
    return pl.pallas_call(
        body, name=name, grid=(j_n, n_t),
        in_specs=[rows, rows_next, rows, tile, prev, nxt, tile, nxt, per_j(f, d), per_j(3, f), per_j(1, f)]
        + ([ANY] if pin is not None else []),
        out_specs=[tile, tile, per_j(f, d), per_j(f, d), per_j(f, d), per_j(3, f), per_j(1, f)],
        out_shape=[jax.ShapeDtypeStruct((j_n, s, f), BF16), jax.ShapeDtypeStruct((j_n, s, f), BF16),
                   jax.ShapeDtypeStruct((j_n, f, d), F32), jax.ShapeDtypeStruct((j_n, f, d), F32),
                   jax.ShapeDtypeStruct((j_n, f, d), F32), jax.ShapeDtypeStruct((j_n, 3, f), F32),
                   jax.ShapeDtypeStruct((j_n, 1, f), F32)],
        compiler_params=_params("parallel", "arbitrary"))(
            dh_bf, dh_bf, hn, g, g, g, up, up, wd, cw, cb, *([pin] if pin is not None else []))


def dx_rms_bwd(pairs, h, gain, rstd, dres, *, name, pin=None):
    j_n, s, f = pairs[0][0].shape
    d = h.shape[1]
    tm = min(ROW_TILE // 2, s)
    n_p = len(pairs)
    dims = [_NT if w.shape[1:] == (d, f) else (((1,), (0,)), ((), ())) for _, w in pairs]

    def body(*refs):
        dy_refs, w_refs = refs[:n_p], refs[n_p:2 * n_p]
        h_ref, g_ref, r_ref, dr_ref = refs[2 * n_p:2 * n_p + 4]
        o_ref, ob_ref, dgain_ref = refs[-3:]

        @pl.when(pl.program_id(0) == 0)
        def _():
            dgain_ref[...] = jnp.zeros_like(dgain_ref)
        dx = jnp.zeros((tm, d), F32)
        for j in range(j_n):
            for p in range(n_p):
                dx = dx + lax.dot_general(dy_refs[p][j], w_refs[p][j], dims[p], preferred_element_type=F32)
        rstd_v = r_ref[...]
        hhat = h_ref[...] * rstd_v
        dgain_ref[...] += _colsum(dx * hhat)
        dxg = dx * g_ref[...]
        dh = dr_ref[...] + rstd_v * (dxg - hhat * jnp.mean(dxg * hhat, axis=-1, keepdims=True))
        o_ref[...] = dh
        ob_ref[...] = dh.astype(BF16)

    tile4 = pl.BlockSpec((j_n, tm, f), lambda i: (0, i, 0))
    whole = [pl.BlockSpec(w.shape, lambda i: (0, 0, 0)) for _, w in pairs]
    row = pl.BlockSpec((tm, d), lambda i: (i, 0))
    vec = pl.BlockSpec((1, d), lambda i: (0, 0))
    return pl.pallas_call(
        body, name=name, grid=(s // tm,),
        in_specs=[tile4] * n_p + whole + [row, vec, pl.BlockSpec((tm, 1), lambda i: (i, 0)), row]
        + ([ANY] if pin is not None else []),
        out_specs=[row, row, vec],
        out_shape=[jax.ShapeDtypeStruct((s, d), F32), jax.ShapeDtypeStruct((s, d), BF16),
                   jax.ShapeDtypeStruct((1, d), F32)],
        compiler_params=_params("arbitrary"))(
            *[p[0] for p in pairs], *[p[1] for p in pairs], h, gain, rstd, dres, *([pin] if pin is not None else []))


def _sgu_gate(vn_bf, ws_ref, bs_ref, h, rows):
    tri = lax.broadcasted_iota(jnp.int32, (CHUNK, CHUNK), 0) >= lax.broadcasted_iota(jnp.int32, (CHUNK, CHUNK), 1)
    ws = jnp.where(tri, ws_ref[h], 0.0).astype(BF16)
    cols = slice((h % 4) * GROUP, (h % 4 + 1) * GROUP)
    return ws, jnp.dot(ws, vn_bf[h // 4][rows, cols], preferred_element_type=F32) + bs_ref[h]


def odd_layer_fwd(h, gain, win, sgu_norm, w_spatial, b_spatial, wout, *, name):
    s, d = h.shape
    w = win.shape[-1]
    ts = min(ROW_TILE, s)
    n_heads = w_spatial.shape[0]

    def body(h_ref, gain_ref, win_ref, n_ref, ws_ref, bs_ref, wout_ref, o_ref, xn_ref, r_ref, p_ref, m_ref, rv_ref):
        x = h_ref[...]
        rstd_x = lax.rsqrt(jnp.mean(x * x, axis=-1, keepdims=True) + EPS)
        xn = (x * rstd_x * gain_ref[...]).astype(BF16)
        xn_ref[...] = xn
        r_ref[...] = rstd_x
        for k in range(4):
            p_ref[k] = jnp.dot(xn, win_ref[k], preferred_element_type=F32).astype(BF16)
        v = [_gelu(p_ref[2].astype(F32)), _gelu(p_ref[3].astype(F32))]
        ms = (jnp.sum(v[0] * v[0], axis=-1, keepdims=True) + jnp.sum(v[1] * v[1], axis=-1, keepdims=True)) / (2 * w)
        rstd = lax.rsqrt(ms + EPS)
        rv_ref[...] = rstd
        vn = [(v[k] * rstd * n_ref[:, k * w:(k + 1) * w]).astype(BF16) for k in range(2)]
        for hd in range(n_heads):
            cols = slice((hd % 4) * GROUP, (hd % 4 + 1) * GROUP)
            for c in range(ts // CHUNK):
                rows = slice(c * CHUNK, (c + 1) * CHUNK)
                _, gate = _sgu_gate(vn, ws_ref, bs_ref, hd, rows)
                u = _gelu(p_ref[hd // 4, rows, cols].astype(F32))
                m_ref[rows, hd * GROUP:(hd + 1) * GROUP] = (u * gate).astype(BF16)
        o_ref[...] = x + jnp.dot(m_ref[...], wout_ref[...], preferred_element_type=F32)

    const = lambda shape: pl.BlockSpec(shape, lambda i: (0,) * len(shape))
    row = pl.BlockSpec((ts, d), lambda i: (i, 0))
    col1 = pl.BlockSpec((ts, 1), lambda i: (i, 0))
    return pl.pallas_call(
        body, name=name, grid=(s // ts,),
        in_specs=[row, const((1, d)), const((4, d, w)), const((1, 2 * w)),
                  const((n_heads, CHUNK, CHUNK)), const((n_heads, CHUNK, 1)), const((2 * w, d))],
        out_specs=[row, row, col1, pl.BlockSpec((4, ts, w), lambda i: (0, i, 0)),
                   pl.BlockSpec((ts, 2 * w), lambda i: (i, 0)), col1],
        out_shape=[jax.ShapeDtypeStruct((s, d), F32), jax.ShapeDtypeStruct((s, d), BF16),
                   jax.ShapeDtypeStruct((s, 1), F32), jax.ShapeDtypeStruct((4, s, w), BF16),
                   jax.ShapeDtypeStruct((s, 2 * w), BF16), jax.ShapeDtypeStruct((s, 1), F32)],
        compiler_params=_params("parallel"))(h, gain, win, sgu_norm, w_spatial, b_spatial, wout)


def sgu_bwd(p, dh_bf, wout, rstd, sgu_norm, w_spatial, b_spatial, *, name, pin=None):
    _, s, w = p.shape
    d = dh_bf.shape[1]
    ts = min(ROW_TILE, s)
    n_heads = w_spatial.shape[0]

    def body(p_ref, dh_ref, wout_ref, r_ref, n_ref, ws_ref, bs_ref, *rest):
        dp_ref, dn_ref, dws_ref, dbs_ref, dvn_ref, dm_ref = rest[-6:]

        @pl.when(pl.program_id(0) == 0)
        def _():
            dn_ref[...] = jnp.zeros_like(dn_ref)
            dws_ref[...] = jnp.zeros_like(dws_ref)
            dbs_ref[...] = jnp.zeros_like(dbs_ref)

        dm_ref[...] = lax.dot_general(dh_ref[...], wout_ref[...], _NT, preferred_element_type=F32)
        rstd_v = r_ref[...]
        vhat = [_gelu(p_ref[2 + k].astype(F32)) * rstd_v for k in range(2)]
        vn = [(vhat[k] * n_ref[:, k * w:(k + 1) * w]).astype(BF16) for k in range(2)]
        tri = lax.broadcasted_iota(jnp.int32, (CHUNK, CHUNK), 0) >= lax.broadcasted_iota(jnp.int32, (CHUNK, CHUNK), 1)
        for h in range(n_heads):
            cols = slice((h % 4) * GROUP, (h % 4 + 1) * GROUP)
            ocols = slice(h * GROUP, (h + 1) * GROUP)
            for c in range(ts // CHUNK):
                rows = slice(c * CHUNK, (c + 1) * CHUNK)
                ws, gate = _sgu_gate(vn, ws_ref, bs_ref, h, rows)
                pu = p_ref[h // 4, rows, cols].astype(F32)
                dm = dm_ref[rows, ocols]
                dp_ref[h // 4, rows, cols] = (dm * gate * _gelu_grad(pu)).astype(BF16)
                dgate = dm * _gelu(pu)
                dbs_ref[h] += jnp.sum(dgate, axis=-1, keepdims=True)
                dgate_bf = dgate.astype(BF16)
                dws = lax.dot_general(dgate_bf, vn[h // 4][rows, cols], _NT, preferred_element_type=F32)
                dws_ref[h] += jnp.where(tri, dws, 0.0)
                dvn_ref[rows, ocols] = lax.dot_general(ws, dgate_bf, _TN, preferred_element_type=F32)
        for k in range(2):
            kc = slice(k * w, (k + 1) * w)
            dvn = dvn_ref[:, kc]
            dn_ref[:, kc] += _colsum(dvn * vhat[k])
        dvh = [dvn_ref[:, k * w:(k + 1) * w] * n_ref[:, k * w:(k + 1) * w] for k in range(2)]
        dot = (jnp.sum(dvh[0] * vhat[0], axis=-1, keepdims=True)
               + jnp.sum(dvh[1] * vhat[1], axis=-1, keepdims=True)) / (2 * w)
        for k in range(2):
            dv = rstd_v * (dvh[k] - vhat[k] * dot)
            dp_ref[2 + k] = (dv * _gelu_grad(p_ref[2 + k].astype(F32))).astype(BF16)

    const = lambda shape: pl.BlockSpec(shape, lambda i: (0,) * len(shape))
    tile4 = pl.BlockSpec((4, ts, w), lambda i: (0, i, 0))
    return pl.pallas_call(
        body, name=name, grid=(s // ts,),
        in_specs=[tile4, pl.BlockSpec((ts, d), lambda i: (i, 0)), const((2 * w, d)),
                  pl.BlockSpec((ts, 1), lambda i: (i, 0)),
                  const((1, 2 * w)), const((n_heads, CHUNK, CHUNK)), const((n_heads, CHUNK, 1))]
        + ([ANY] if pin is not None else []),
        out_specs=[tile4, const((1, 2 * w)), const((n_heads, CHUNK, CHUNK)), const((n_heads, CHUNK, 1))],
        out_shape=[jax.ShapeDtypeStruct((4, s, w), BF16), jax.ShapeDtypeStruct((1, 2 * w), F32),
                   jax.ShapeDtypeStruct((n_heads, CHUNK, CHUNK), F32),
                   jax.ShapeDtypeStruct((n_heads, CHUNK, 1), F32)],
        scratch_shapes=[pltpu.VMEM((ts, 2 * w), F32), pltpu.VMEM((ts, 2 * w), F32)],
        compiler_params=_params("arbitrary"))(
            p, dh_bf, wout, rstd, sgu_norm, w_spatial, b_spatial, *([pin] if pin is not None else []))


def _row_tile(rows):
    if rows <= ROW_TILE:
        return rows
    for t in (512, 384, 352, 256, 128, 64, 32, 16, 8):
        if rows % t == 0:
            return t
    return rows


def adamw(w, g, m, v, *, name):
    shape = w.shape
    cols = shape[-1]
    rows = w.size // cols
    w2, g2, m2, v2 = (a.reshape(rows, cols) for a in (w, g, m, v))
    tr = _row_tile(rows)
    bc1 = 1.0 - ADAM_B1 ** ADAM_STEP
    bc2 = 1.0 - ADAM_B2 ** ADAM_STEP

    def body(w_ref, g_ref, m_ref, v_ref, d_ref, nm_ref, nv_ref):
        grad = g_ref[...]
        m_new = ADAM_B1 * m_ref[...] + (1.0 - ADAM_B1) * grad
        v_new = ADAM_B2 * v_ref[...] + (1.0 - ADAM_B2) * (grad * grad)
        nm_ref[...] = m_new
        nv_ref[...] = v_new
        d_ref[...] = -ADAM_LR * ((m_new / bc1) / (jnp.sqrt(v_new / bc2) + ADAM_EPS) + ADAM_WD * w_ref[...])

    spec = pl.BlockSpec((tr, cols), lambda i: (i, 0))
    outs = pl.pallas_call(
        body, name=name, grid=(rows // tr,),
        in_specs=[spec] * 4, out_specs=[spec] * 3,
        out_shape=[jax.ShapeDtypeStruct((rows, cols), F32)] * 3,
        compiler_params=_params("parallel"))(w2, g2, m2, v2)
    return tuple(o.reshape(shape) for o in outs)


def adamw_layers(w, grads, m, v, *, name):
    n_l, rows, cols = w.shape
    tr = _row_tile(rows)
    bc1 = 1.0 - ADAM_B1 ** ADAM_STEP
    bc2 = 1.0 - ADAM_B2 ** ADAM_STEP
    outs = None
    for l in range(n_l):
        def body(w_ref, g_ref, m_ref, v_ref, *rest):
            go_ref, d_ref, nm_ref, nv_ref = rest[-4:]
            grad = g_ref[...]
            m_new = ADAM_B1 * m_ref[...] + (1.0 - ADAM_B1) * grad
            v_new = ADAM_B2 * v_ref[...] + (1.0 - ADAM_B2) * (grad * grad)
            go_ref[...] = grad
            nm_ref[...] = m_new
            nv_ref[...] = v_new
            d_ref[...] = -ADAM_LR * ((m_new / bc1) / (jnp.sqrt(v_new / bc2) + ADAM_EPS) + ADAM_WD * w_ref[...])

        layer = pl.BlockSpec((None, tr, cols), lambda i, l=l: (l, i, 0))
        prev = list(outs) if outs is not None else []
        outs = pl.pallas_call(
            body, name=f"{name}{l}", grid=(rows // tr,),
            in_specs=[layer, pl.BlockSpec((tr, cols), lambda i: (i, 0)), layer, layer] + [ANY] * len(prev),
            out_specs=[layer] * 4,
            out_shape=[jax.ShapeDtypeStruct(w.shape, F32)] * 4,
            input_output_aliases={4 + k: k for k in range(len(prev))},
            compiler_params=_params("parallel"))(w, grads[l], m, v, *prev)
    return tuple(outs)


def _place():
    return lax.axis_index("x"), lax.axis_index("y"), lax.axis_index("c")


def _other_chips(x, y):
    return [(1 - x, y), (x, 1 - y), (1 - x, 1 - y)]


HBM = pl.BlockSpec(memory_space=pltpu.HBM)
SEM = pl.BlockSpec(memory_space=pltpu.SEMAPHORE)
DATAFLOW = pltpu.SideEffectType.DATAFLOW_SIDE_EFFECTING


def _in_hbm(a):
    return pltpu.with_memory_space_constraint(a, pltpu.HBM)


def cast_into_slot(w, chip, *, l=None, name):
    rows, cols = w.shape[-2:]
    tr = _row_tile(rows)

    def body(chip_ref, w_ref, o_ref):
        o_ref[...] = w_ref[...].astype(BF16)

    in_spec = (pl.BlockSpec((tr, cols), lambda i, chip_ref: (i, 0)) if l is None
               else pl.BlockSpec((None, tr, cols), lambda i, chip_ref: (l, i, 0)))
    return pl.pallas_call(
        body, name=name,
        grid_spec=pltpu.PrefetchScalarGridSpec(
            num_scalar_prefetch=1, grid=(rows // tr,), in_specs=[in_spec],
            out_specs=pl.BlockSpec((None, tr, cols), lambda i, chip_ref: (chip_ref[0], i, 0))),
        out_shape=jax.ShapeDtypeStruct((N_CHIPS, rows, cols), BF16),
        compiler_params=_params("parallel"))(chip, w)


def _half(ref, slot, c):
    half = ref.shape[1] // 2
    return ref.at[slot, pl.ds(c * half, half), :]


def gather_start(groups, smalls):
    flat = [b for g in groups for b in g]
    n_b, n_s, n_g = len(flat), len(smalls), len(groups)

    def body(*refs):
        bufs, small_refs = refs[:n_b], refs[n_b:n_b + n_s]
        sems = refs[n_b + n_s:n_b + n_s + 2 * n_g + 2]
        token = refs[-1]
        x, y, c = _place()
        me = 2 * x + y
        chips = _other_chips(x, y)
        for si in range(n_s):
            piece = small_refs[si].at[me]
            for k, (px, py) in enumerate(chips):
                pltpu.make_async_remote_copy(
                    src_ref=piece, dst_ref=piece,
                    send_sem=sems[2 * n_g].at[3 * si + k], recv_sem=sems[2 * n_g + 1].at[3 * si + k],
                    device_id=(px, py, c), device_id_type=MESH).start()
        t = 0
        for gi, group in enumerate(groups):
            for ti in range(len(group)):
                piece = _half(bufs[t], me, c)
                t += 1
                for k, (px, py) in enumerate(chips):
                    pltpu.make_async_remote_copy(
                        src_ref=piece, dst_ref=piece,
                        send_sem=sems[2 * gi].at[3 * ti + k], recv_sem=sems[2 * gi + 1].at[3 * ti + k],
                        device_id=(px, py, c), device_id_type=MESH).start()
        token[...] = jnp.zeros_like(token)

    sem_shapes = []
    for group in groups:
        sem_shapes += [pltpu.SemaphoreType.DMA((3 * len(group),))] * 2
    sem_shapes += [pltpu.SemaphoreType.DMA((3 * n_s,))] * 2
    arrays = flat + list(smalls)
    n_sem = len(sem_shapes)
    res = pl.pallas_call(
        body, name="gather_start",
        out_shape=tuple(sem_shapes) + tuple(pltpu.HBM(a.shape, a.dtype) for a in arrays)
        + (jax.ShapeDtypeStruct((8, 128), F32),),
        in_specs=[HBM] * len(arrays),
        out_specs=tuple([SEM] * n_sem + [HBM] * len(arrays) + [pl.BlockSpec(memory_space=pltpu.VMEM)]),
        input_output_aliases={i: n_sem + i for i in range(len(arrays))},
        compiler_params=pltpu.CompilerParams(has_side_effects=DATAFLOW))(*[_in_hbm(a) for a in arrays])
    sems, thru, token = res[:n_sem], res[n_sem:-1], res[-1]
    out_groups, t = [], 0
    for group in groups:
        out_groups.append(list(thru[t:t + len(group)]))
        t += len(group)
    return sems, out_groups, list(thru[n_b:]), token


def gather_wait(bufs, send, recv, after, *, name, smalls=(), small_send=None, small_recv=None):
    n_b, n_s = len(bufs), len(smalls)
    arrays = list(bufs) + list(smalls)
    sem_ops = [send, recv] + ([small_send, small_recv] if n_s else [])

    def body(*refs):
        buf_refs, small_refs = refs[:n_b], refs[n_b:n_b + n_s]
        sems = refs[n_b + n_s:n_b + n_s + len(sem_ops)]
        x, y, c = _place()
        me = 2 * x + y
        chips = _other_chips(x, y)
        for ti in range(n_b):
            for k, (px, py) in enumerate(chips):
                done = pltpu.make_async_remote_copy(
                    src_ref=_half(buf_refs[ti], me, c), dst_ref=_half(buf_refs[ti], 2 * px + py, c),
                    send_sem=sems[0].at[3 * ti + k], recv_sem=sems[1].at[3 * ti + k],
                    device_id=(px, py, c), device_id_type=MESH)
                done.wait_send()
                done.wait_recv()
        for si in range(n_s):
            for k, (px, py) in enumerate(chips):
                done = pltpu.make_async_remote_copy(
                    src_ref=small_refs[si].at[me], dst_ref=small_refs[si].at[2 * px + py],
                    send_sem=sems[2].at[3 * si + k], recv_sem=sems[3].at[3 * si + k],
                    device_id=(px, py, c), device_id_type=MESH)
                done.wait_send()
                done.wait_recv()

    res = pl.pallas_call(
        body, name=name,
        out_shape=tuple(pltpu.HBM(a.shape, a.dtype) for a in arrays),
        in_specs=[HBM] * len(arrays) + [SEM] * len(sem_ops) + [ANY],
        out_specs=tuple([HBM] * len(arrays)),
        input_output_aliases={i: i for i in range(len(arrays))},
        compiler_params=pltpu.CompilerParams(has_side_effects=DATAFLOW))(*arrays, *sem_ops, after)
    return list(res[:n_b]), list(res[n_b:])


def gather_forward(bufs, *, name):
    n = len(bufs)

    def body(*refs):
        ins, outs = refs[:n], refs[n:2 * n]
        send_sems, recv_sems = refs[2 * n:]
        x, y, c = _place()
        chips = _other_chips(x, y)
        for t in range(n):
            for k, (px, py) in enumerate(chips):
                pltpu.make_async_remote_copy(
                    src_ref=_half(ins[t], 2 * px + py, c), dst_ref=_half(outs[t], 2 * px + py, c),
                    send_sem=send_sems.at[3 * t + k], recv_sem=recv_sems.at[3 * t + k],
                    device_id=(x, y, 1 - c), device_id_type=MESH).start()
        for t in range(n):
            for k, (px, py) in enumerate(chips):
                done = pltpu.make_async_remote_copy(
                    src_ref=_half(ins[t], 2 * px + py, c), dst_ref=_half(outs[t], 2 * px + py, 1 - c),
                    send_sem=send_sems.at[3 * t + k], recv_sem=recv_sems.at[3 * t + k],
                    device_id=(x, y, 1 - c), device_id_type=MESH)
                done.wait_send()
                done.wait_recv()

    return pl.pallas_call(
        body, name=name, in_specs=[ANY] * n, out_specs=[ANY] * n,
        out_shape=[jax.ShapeDtypeStruct(a.shape, a.dtype) for a in bufs],
        input_output_aliases={i: i for i in range(n)},
        scratch_shapes=[pltpu.SemaphoreType.DMA((3 * n,)), pltpu.SemaphoreType.DMA((3 * n,))],
        compiler_params=pltpu.CompilerParams(has_side_effects=True))(*bufs)


def sum_stage_a(grad, recv, place, wire, *, name):
    j_n, half, cols = recv.shape

    def body(place_ref, g_ref, r_ref, o_ref, ob_ref):
        acc = g_ref[...] + r_ref[...]
        o_ref[...] = acc
        ob_ref[...] = acc.astype(wire)

    blk = (None, half, cols)
    return pl.pallas_call(
        body, name=name,
        grid_spec=pltpu.PrefetchScalarGridSpec(
            num_scalar_prefetch=1, grid=(j_n,),
            in_specs=[pl.BlockSpec(blk, lambda j, place_ref: (j, place_ref[1], 0)),
                      pl.BlockSpec(blk, lambda j, place_ref: (j, 0, 0))],
            out_specs=[pl.BlockSpec(blk, lambda j, place_ref: (j, 0, 0))] * 2),
        out_shape=[jax.ShapeDtypeStruct(recv.shape, F32), jax.ShapeDtypeStruct(recv.shape, wire)],
        compiler_params=_params("parallel"))(place, grad, recv)


def _stage_a_copies(srcs, lands, x, y, c):
    out = []
    for src, land in zip(srcs, lands):
        half = src.shape[1] // 2
        out.append((src.at[:, pl.ds((1 - c) * half, half), :], land, (x, y, 1 - c)))
    return out


def _stage_b_copies(srcs, lands, x, y, c):
    out = []
    for src, land in zip(srcs, lands):
        for k, (px, py) in enumerate(_other_chips(x, y)):
            out.append((src.at[2 * px + py], land.at[k], (px, py, c)))
    return out


def split_start(srcs, lands, copies, *, name):
    n = len(srcs)
    n_c = len(copies(srcs, lands, 0, 0, 0))

    def body(*refs):
        src_refs, land_refs = refs[:n], refs[n:2 * n]
        send_sems, recv_sems = refs[2 * n], refs[2 * n + 1]
        token = refs[-1]
        x, y, c = _place()
        for k, (src, dst, target) in enumerate(copies(src_refs, land_refs, x, y, c)):
            pltpu.make_async_remote_copy(src_ref=src, dst_ref=dst, send_sem=send_sems.at[k], recv_sem=recv_sems.at[k],
                                         device_id=target, device_id_type=MESH).start()
        token[...] = jnp.zeros_like(token)

    arrays = list(srcs) + list(lands)
    res = pl.pallas_call(
        body, name=name,
        out_shape=(pltpu.SemaphoreType.DMA((n_c,)), pltpu.SemaphoreType.DMA((n_c,)))
        + tuple(pltpu.HBM(a.shape, a.dtype) for a in arrays) + (jax.ShapeDtypeStruct((8, 128), F32),),
        in_specs=[HBM] * (2 * n),
        out_specs=tuple([SEM, SEM] + [HBM] * (2 * n) + [pl.BlockSpec(memory_space=pltpu.VMEM)]),
        input_output_aliases={i: 2 + i for i in range(2 * n)},
        compiler_params=pltpu.CompilerParams(has_side_effects=DATAFLOW))(*[_in_hbm(a) for a in arrays])
    return res[0], res[1], list(res[2:2 + n]), list(res[2 + n:2 + 2 * n]), res[-1]


def split_wait(srcs, lands, send, recv, copies, after, *, name):
    n = len(srcs)

    def body(*refs):
        src_refs, land_refs = refs[:n], refs[n:2 * n]
        send_sems, recv_sems = refs[2 * n], refs[2 * n + 1]
        x, y, c = _place()
        for k, (src, dst, target) in enumerate(copies(src_refs, land_refs, x, y, c)):
            done = pltpu.make_async_remote_copy(src_ref=src, dst_ref=dst, send_sem=send_sems.at[k],
                                                recv_sem=recv_sems.at[k], device_id=target, device_id_type=MESH)
            done.wait_send()
            done.wait_recv()

    arrays = list(srcs) + list(lands)
    res = pl.pallas_call(
        body, name=name,
        out_shape=tuple(pltpu.HBM(a.shape, a.dtype) for a in arrays),
        in_specs=[HBM] * (2 * n) + [SEM, SEM, ANY],
        out_specs=tuple([HBM] * (2 * n)),
        input_output_aliases={i: i for i in range(2 * n)},
        compiler_params=pltpu.CompilerParams(has_side_effects=DATAFLOW))(*arrays, send, recv, after)
    return list(res[:n]), list(res[n:])


def sum_stage_b(part, recv, place, *, name):
    _, half, cols = part.shape

    def body(place_ref, p_ref, r_ref, o_ref):
        acc = p_ref[...]
        for k in range(3):
            acc = acc + r_ref[k].astype(F32)
        o_ref[...] = acc

    return pl.pallas_call(
        body, name=name,
        grid_spec=pltpu.PrefetchScalarGridSpec(
            num_scalar_prefetch=1, grid=(1,),
            in_specs=[pl.BlockSpec((None, half, cols), lambda i, place_ref: (place_ref[0], 0, 0)),
                      pl.BlockSpec((3, half, cols), lambda i, place_ref: (0, 0, 0))],
            out_specs=pl.BlockSpec((half, cols), lambda i, place_ref: (place_ref[1], 0))),
        out_shape=jax.ShapeDtypeStruct((2 * half, cols), F32),
        compiler_params=_params("arbitrary"))(place, part, recv)


def reduce_stage_c(fulls, *, name):
    n = len(fulls)

    def body(*refs):
        ins, outs = refs[:n], refs[n:2 * n]
        send_sems, recv_sems = refs[2 * n:]
        x, y, c = _place()
        for t in range(n):
            half = ins[t].shape[0] // 2
            pltpu.make_async_remote_copy(
                src_ref=ins[t].at[pl.ds(c * half, half), :], dst_ref=outs[t].at[pl.ds(c * half, half), :],
                send_sem=send_sems.at[t], recv_sem=recv_sems.at[t],
                device_id=(x, y, 1 - c), device_id_type=MESH).start()
        for t in range(n):
            half = ins[t].shape[0] // 2
            done = pltpu.make_async_remote_copy(
                src_ref=ins[t].at[pl.ds(c * half, half), :], dst_ref=outs[t].at[pl.ds((1 - c) * half, half), :],
                send_sem=send_sems.at[t], recv_sem=recv_sems.at[t],
                device_id=(x, y, 1 - c), device_id_type=MESH)
            done.wait_send()
            done.wait_recv()

    return pl.pallas_call(
        body, name=name, in_specs=[ANY] * n, out_specs=[ANY] * n,
        out_shape=[jax.ShapeDtypeStruct(a.shape, a.dtype) for a in fulls],
        input_output_aliases={i: i for i in range(n)},
        scratch_shapes=[pltpu.SemaphoreType.DMA((n,)), pltpu.SemaphoreType.DMA((n,))],
        compiler_params=pltpu.CompilerParams(has_side_effects=True))(*fulls)


def gather_chip_blocks(slots, *, name):
    def body(in_ref, out_ref, send_sems, recv_sems):
        x, y, c = _place()
        me = 2 * x + y
        chips = _other_chips(x, y)
        for k, (px, py) in enumerate(chips):
            pltpu.make_async_remote_copy(
                src_ref=in_ref.at[me], dst_ref=out_ref.at[me],
                send_sem=send_sems.at[k], recv_sem=recv_sems.at[k],
                device_id=(px, py, c), device_id_type=MESH).start()
        for k, (px, py) in enumerate(chips):
            done = pltpu.make_async_remote_copy(
                src_ref=in_ref.at[me], dst_ref=out_ref.at[2 * px + py],
                send_sem=send_sems.at[k], recv_sem=recv_sems.at[k],
                device_id=(px, py, c), device_id_type=MESH)
            done.wait_send()
            done.wait_recv()

    return pl.pallas_call(
        body, name=name, in_specs=[ANY], out_specs=ANY,
        out_shape=jax.ShapeDtypeStruct(slots.shape, slots.dtype),
        input_output_aliases={0: 0},
        scratch_shapes=[pltpu.SemaphoreType.DMA((3,)), pltpu.SemaphoreType.DMA((3,))],
        compiler_params=pltpu.CompilerParams(has_side_effects=True))(slots)


def _ffn_bwd(dh, dh_bf, h, gain, saved, wg, wu, wd, cw, cb, place, l, pin):
    hn, rstd, g, up = saved
    dg, dup, dwg, dwu, dwd, dcw, dcb = ffn_bwd_a(dh_bf, hn, g, up, wd, cw, cb, name=f"ffn{l}_bwd_a", pin=pin)
    red = _reduce_a_begin([(dwg, BF16), (dwu, BF16), (dwd, BF16)], tag=f"f{l}")
    dh_in, dh_in_bf, dgain = dx_rms_bwd([(dg, wg), (dup, wu)], h, gain, rstd, dh, name=f"ffn{l}_bwd_b",
                                        pin=red[-1])
    red = _reduce_b_begin(red, place, dh_in_bf, tag=f"f{l}")
    return dh_in, dh_in_bf, (dcw, dcb, dgain), red


def _reduce_a_begin(grads, *, tag):
    lands = [lax.empty((g.shape[0], g.shape[1] // 2, g.shape[2]), F32) for g, _ in grads]
    send, recv, srcs, lands, token = split_start([g for g, _ in grads], lands, _stage_a_copies,
                                                 name=f"reduce_a_start_{tag}")
    return [w for _, w in grads], send, recv, srcs, lands, token


def _reduce_b_begin(state, place, after, *, tag):
    wires, send, recv, srcs, lands, _ = state
    grads, recv_a = split_wait(srcs, lands, send, recv, _stage_a_copies, after, name=f"reduce_a_wait_{tag}")
    parts = [sum_stage_a(g, r, place, w, name=f"sum_a_{tag}{i}") for i, (g, r, w) in enumerate(zip(grads, recv_a, wires))]
    lands_b = [lax.empty((3,) + p[1].shape[1:], p[1].dtype) for p in parts]
    send, recv, srcs, lands, token = split_start([p[1] for p in parts], lands_b, _stage_b_copies,
                                                 name=f"reduce_b_start_{tag}")
    return [p[0] for p in parts], send, recv, srcs, lands, token


def _reduce_finish(state, place, after, *, tag):
    parts, send, recv, srcs, lands, _ = state
    _, recv_b = split_wait(srcs, lands, send, recv, _stage_b_copies, after, name=f"reduce_b_wait_{tag}")
    halves = [sum_stage_b(p, r, place, name=f"sum_b_{tag}{i}") for i, (p, r) in enumerate(zip(parts, recv_b))]
    return reduce_stage_c(halves, name=f"reduce_c_{tag}")


def kernel(x, norm_mix, norm_ffn, final_norm, w_in_even, conv_a, w_pool, pool_scale, w_out_even, w_in_odd, sgu_norm, w_spatial, b_spatial, w_out_odd, w_ffn_gate, w_ffn_up, conv_ffn, b_conv_ffn, w_ffn_down, loss_target, m_norm_mix, m_norm_ffn, m_final_norm, m_w_in_even, m_conv_a, m_w_pool, m_pool_scale, m_w_out_even, m_w_in_odd, m_sgu_norm, m_w_spatial, m_b_spatial, m_w_out_odd, m_w_ffn_gate, m_w_ffn_up, m_conv_ffn, m_b_conv_ffn, m_w_ffn_down, v_norm_mix, v_norm_ffn, v_final_norm, v_w_in_even, v_conv_a, v_w_pool, v_pool_scale, v_w_out_even, v_w_in_odd, v_sgu_norm, v_w_spatial, v_b_spatial, v_w_out_odd, v_w_ffn_gate, v_w_ffn_up, v_conv_ffn, v_b_conv_ffn, v_w_ffn_down):
    weights = dict(norm_mix=norm_mix, norm_ffn=norm_ffn, final_norm=final_norm, w_in_even=w_in_even,
                   conv_a=conv_a, w_pool=w_pool, pool_scale=pool_scale, w_out_even=w_out_even,
                   w_in_odd=w_in_odd, sgu_norm=sgu_norm, w_spatial=w_spatial, b_spatial=b_spatial,
                   w_out_odd=w_out_odd, w_ffn_gate=w_ffn_gate, w_ffn_up=w_ffn_up, conv_ffn=conv_ffn,
                   b_conv_ffn=b_conv_ffn, w_ffn_down=w_ffn_down)
    m_in = dict(norm_mix=m_norm_mix, norm_ffn=m_norm_ffn, final_norm=m_final_norm, w_in_even=m_w_in_even,
                conv_a=m_conv_a, w_pool=m_w_pool, pool_scale=m_pool_scale, w_out_even=m_w_out_even,
                w_in_odd=m_w_in_odd, sgu_norm=m_sgu_norm, w_spatial=m_w_spatial, b_spatial=m_b_spatial,
                w_out_odd=m_w_out_odd, w_ffn_gate=m_w_ffn_gate, w_ffn_up=m_w_ffn_up, conv_ffn=m_conv_ffn,
                b_conv_ffn=m_b_conv_ffn, w_ffn_down=m_w_ffn_down)
    v_in = dict(norm_mix=v_norm_mix, norm_ffn=v_norm_ffn, final_norm=v_final_norm, w_in_even=v_w_in_even,
                conv_a=v_conv_a, w_pool=v_w_pool, pool_scale=v_pool_scale, w_out_even=v_w_out_even,
                w_in_odd=v_w_in_odd, sgu_norm=v_sgu_norm, w_spatial=v_w_spatial, b_spatial=v_b_spatial,
                w_out_odd=v_w_out_odd, w_ffn_gate=v_w_ffn_gate, w_ffn_up=v_w_ffn_up, conv_ffn=v_conv_ffn,
                b_conv_ffn=v_b_conv_ffn, w_ffn_down=v_w_ffn_down)
    order = list(weights)

    chip = 2 * lax.axis_index("x") + lax.axis_index("y")
    core = lax.axis_index("c")
    place = jnp.stack([chip, core]).astype(jnp.int32)
    chip_arr = place[:1]

    h0 = x[0]
    target = loss_target[0]
    d_model = h0.shape[1]
    f_shard = w_ffn_gate.shape[-1]

    def turned(a):
        return jnp.transpose(a, (0, 2, 1))

    def own_slot(v):
        return lax.dynamic_update_index_in_dim(jnp.zeros((N_CHIPS,) + v.shape, v.dtype), v, chip, 0)

    groups = [
        [cast_into_slot(w_in_even[0], chip_arr, name="cast_win_e"),
         cast_into_slot(w_out_even[0], chip_arr, name="cast_wout_e")],
        [cast_into_slot(turned(w_ffn_gate), chip_arr, l=0, name="cast_wg0"),
         cast_into_slot(turned(w_ffn_up), chip_arr, l=0, name="cast_wu0")],
        [cast_into_slot(w_ffn_down, chip_arr, l=0, name="cast_wd0")],
        [cast_into_slot(w_in_odd[0], chip_arr, name="cast_win_o"),
         cast_into_slot(w_out_odd[0], chip_arr, name="cast_wout_o")],
        [cast_into_slot(turned(w_ffn_gate), chip_arr, l=1, name="cast_wg1"),
         cast_into_slot(turned(w_ffn_up), chip_arr, l=1, name="cast_wu1")],
        [cast_into_slot(w_ffn_down, chip_arr, l=1, name="cast_wd1")]]
    smalls = [own_slot(conv_a[0]), own_slot(sgu_norm), own_slot(conv_ffn[0]), own_slot(conv_ffn[1])]
    sems, groups, smalls, token = gather_start(groups, smalls)

    def arrive(gi, after, with_smalls=False):
        kw = dict(smalls=smalls, small_send=sems[-2], small_recv=sems[-1]) if with_smalls else {}
        bufs, small_out = gather_wait(groups[gi], sems[2 * gi], sems[2 * gi + 1], after, name=f"gather_wait{gi}", **kw)
        return gather_forward(bufs, name=f"gather_forward{gi}"), small_out

    cb = b_conv_ffn.reshape(-1, N_CHIPS, 1, f_shard)
    wp_bf = w_pool[0].astype(BF16)
    wp_t_bf = jnp.transpose(w_pool[0], (0, 2, 1)).astype(BF16)
    ws = w_spatial[0]
    bs = b_spatial[0][:, :, None]

    (win_e, wout_e), (ca_g, sn_g, cw0, cw1) = arrive(0, token, with_smalls=True)
    wout_e = wout_e.reshape(-1, d_model)
    ca_full = jnp.transpose(ca_g, (1, 0, 2)).reshape(ca_g.shape[1], -1)
    sn_full = sn_g.reshape(1, -1)
    h1, xn0, rstd0, proj0, mix0 = even_layer_fwd(h0, norm_mix[0:1], win_e, ca_full, wp_bf, pool_scale, wout_e,
                                                 name="l0_fwd")
    hn0, rstdf0 = rms_fwd(h1, norm_ffn[0:1], name="ffn0_rms")
    (wg0, wu0), _ = arrive(1, hn0)
    g0, up0, act0 = ffn_in_fwd(hn0, wg0, wu0, cw0, cb[0], name="ffn0_in")
    (wd0,), _ = arrive(2, act0)
    h2 = mm_acc(act0, wd0, h1, name="ffn0_down")
    ffn0 = (hn0, rstdf0, g0, up0)
    (win_o, wout_o), _ = arrive(3, h2)
    wout_o = wout_o.reshape(-1, d_model)
    h3, xn1, rstd1, p1, mix1, rstd_v = odd_layer_fwd(h2, norm_mix[1:2], win_o, sn_full, ws, bs, wout_o, name="l1_fwd")
    hn1, rstdf1 = rms_fwd(h3, norm_ffn[1:2], name="ffn1_rms")
    (wg1, wu1), _ = arrive(4, hn1)
    g1, up1, act1 = ffn_in_fwd(hn1, wg1, wu1, cw1, cb[1], name="ffn1_in")
    (wd1,), _ = arrive(5, act1)
    h4 = mm_acc(act1, wd1, h3, name="ffn1_down")
    ffn1 = (hn1, rstdf1, g1, up1)

    dh4, dh4_bf, loss_row, d_final = final_loss(h4, target, final_norm[None], name="loss")
    loss = lax.psum(loss_row[0, 0], ("x", "y", "c"))

    dh3, dh3_bf, (dcw1, dcb1, dnf1), red3 = _ffn_bwd(
        dh4, dh4_bf, h3, norm_ffn[1:2], ffn1, wg1, wu1, wd1, cw1, cb[1], place, 1, None)

    def as_blocks(a):
        return a.reshape(N_CHIPS, -1, d_model)

    dp1, dsn, dws, dbs = sgu_bwd(p1, dh3_bf, wout_o, rstd_v, sn_full, ws, bs, name="l1_mix_bwd", pin=red3[-1])
    dwout_o = as_blocks(mm_tn(mix1[None], dh3_bf[None], name="l1_dwout"))
    dwin_o = mm_tn_shared(xn1, dp1, name="l1_dwin")
    red2 = _reduce_a_begin([(dwin_o, BF16), (dwout_o, BF16)], tag="m1")
    dh2, dh2_bf, dnm1 = dx_rms_bwd([(dp1, win_o)], h2, norm_mix[1:2], rstd1, dh3, name="l1_dx", pin=red2[-1])
    red2 = _reduce_b_begin(red2, place, dh2_bf, tag="m1")

    dh1, dh1_bf, (dcw0, dcb0, dnf0), red1 = _ffn_bwd(
        dh2, dh2_bf, h1, norm_ffn[0:1], ffn0, wg0, wu0, wd0, cw0, cb[0], place, 0, red2[-1])

    dproj0, dca, dwp, dps = even_bwd(proj0, dh1_bf, wout_e, ca_full, wp_bf, wp_t_bf, pool_scale,
                                     name="l0_mix_bwd", pin=red1[-1])
    dwout_e = as_blocks(mm_tn(mix0[None], dh1_bf[None], name="l0_dwout"))
    dwin_e = mm_tn_shared(xn0, dproj0, name="l0_dwin")
    dh0, _, dnm0 = dx_rms_bwd([(dproj0, win_e)], h0, norm_mix[0:1], rstd0, dh1, name="l0_dx")
    grad_x = dh0[None]

    small_parts = dict(
        norm_mix=jnp.concatenate([dnm0, dnm1]), norm_ffn=jnp.concatenate([dnf0, dnf1]), final_norm=d_final,
        conv_a=dca, w_pool=dwp, pool_scale=dps, sgu_norm=dsn, w_spatial=dws, b_spatial=dbs,
        conv_ffn=jnp.stack([dcw0, dcw1]), b_conv_ffn=jnp.stack([dcb0, dcb1]))
    flat = jnp.concatenate([v.reshape(-1) for v in small_parts.values()])
    pad = (-flat.shape[0]) % (N_CHIPS * 32 * 128)
    small = jnp.pad(flat, (0, pad)).reshape(N_CHIPS, -1, 128)
    red0 = _reduce_a_begin([(dwin_e, BF16), (dwout_e, BF16), (small, F32)], tag="m0")

    full3 = _reduce_finish(red3, place, red0[-1], tag="f1")
    red0 = _reduce_b_begin(red0, place, full3[0], tag="m0")
    full2 = _reduce_finish(red2, place, red0[-1], tag="m1")
    full1 = _reduce_finish(red1, place, full2[0], tag="f0")
    full0 = _reduce_finish(red0, place, full1[0], tag="m0")
    small_slots = lax.dynamic_update_index_in_dim(jnp.zeros(small.shape, F32), full0[2], chip, 0)
    small_sum = gather_chip_blocks(small_slots, name="gather_small").reshape(-1)
    grads = {
        "w_in_even": full0[0][None], "w_out_even": full0[1][None],
        "w_in_odd": full2[0][None], "w_out_odd": full2[1][None],
        }
    layered = {"w_ffn_gate": [full1[0], full3[0]], "w_ffn_up": [full1[1], full3[1]],
               "w_ffn_down": [full1[2], full3[2]]}
    off = 0
    small_red = {}
    for nm, v in small_parts.items():
        small_red[nm] = small_sum[off:off + v.size].reshape(v.shape)
        off += v.size
    for nm in ("norm_mix", "norm_ffn", "pool_scale"):
        grads[nm] = small_red[nm].reshape(weights[nm].shape)
    grads["final_norm"] = small_red["final_norm"].reshape(weights["final_norm"].shape)
    grads["w_pool"] = small_red["w_pool"][None]
    grads["w_spatial"] = small_red["w_spatial"][None]
    grads["b_spatial"] = small_red["b_spatial"].reshape(weights["b_spatial"].shape)
    grads["b_conv_ffn"] = small_red["b_conv_ffn"].reshape(weights["b_conv_ffn"].shape)
    grads["conv_a"] = lax.dynamic_slice_in_dim(small_red["conv_a"], chip * conv_a.shape[-1], conv_a.shape[-1], 1)[None]
    grads["sgu_norm"] = lax.dynamic_slice_in_dim(small_red["sgu_norm"], chip * sgu_norm.shape[-1], sgu_norm.shape[-1], 1)
    grads["conv_ffn"] = lax.dynamic_index_in_dim(small_red["conv_ffn"], chip, 1, keepdims=False)

    deltas, new_m, new_v = {}, {}, {}
    for nm, per_layer in layered.items():
        if nm == "w_ffn_down":
            grads[nm], deltas[nm], new_m[nm], new_v[nm] = adamw_layers(
                weights[nm], per_layer, m_in[nm], v_in[nm], name=f"adamw_{nm}")
        else:
            outs = adamw_layers(turned(weights[nm]), per_layer, turned(m_in[nm]), turned(v_in[nm]),
                                name=f"adamw_{nm}")
            grads[nm], deltas[nm], new_m[nm], new_v[nm] = (turned(o) for o in outs)
    for nm in order:
        if nm in layered:
            continue
        w = weights[nm]
        w2 = w[None] if w.ndim == 1 else w
        shp = w2.shape
        d, nm_, nv_ = adamw(w2, grads[nm].reshape(shp), m_in[nm].reshape(shp), v_in[nm].reshape(shp),
                            name=f"adamw_{nm}")
        deltas[nm], new_m[nm], new_v[nm] = d.reshape(w.shape), nm_.reshape(w.shape), nv_.reshape(w.shape)

    return (loss, grad_x, *[grads[n] for n in order], *[deltas[n] for n in order],
            *[new_m[n] for n in order], *[new_v[n] for n in order])
```

```python
import jax
import jax.numpy as jnp
from jax import lax
from jax.experimental import pallas as pl
from jax.experimental.pallas import tpu as pltpu

F32 = jnp.float32
BF16 = jnp.bfloat16
MESH = pl.DeviceIdType.MESH
ANY = pl.BlockSpec(memory_space=pl.ANY)

EPS = 1e-6
POOL_WINDOWS = (2, 4, 8, 16)
GROUP = 128
CHUNK = 128
N_CHIPS = 4
N_DEV = 8
ROW_TILE = 512
HALO = 16
VMEM_LIMIT = 56 * 1024 * 1024

ADAM_LR = 0.001
ADAM_B1 = 0.9
ADAM_B2 = 0.999
ADAM_EPS = 1e-08
ADAM_WD = 0.01
ADAM_STEP = 10


def _params(*sem):
    return pltpu.CompilerParams(dimension_semantics=sem, vmem_limit_bytes=VMEM_LIMIT)


def _layer_spec(block, l, idx):
    if l is None:
        return pl.BlockSpec(block, idx)
    return pl.BlockSpec((None,) + block, lambda *g: (l,) + idx(*g))


def mm_nn(a, b, *, l=None, name):
    s, k = a.shape
    j_n, n = b.shape[-3], b.shape[-1]
    tm = min(ROW_TILE, s)

    def body(a_ref, b_ref, o_ref):
        o_ref[...] = jnp.dot(a_ref[...], b_ref[...], preferred_element_type=F32)

    return pl.pallas_call(
        body, name=name, grid=(j_n, s // tm),
        in_specs=[pl.BlockSpec((tm, k), lambda j, i: (i, 0)),
                  _layer_spec((None, k, n), l, lambda j, i: (j, 0, 0))],
        out_specs=pl.BlockSpec((None, tm, n), lambda j, i: (j, i, 0)),
        out_shape=jax.ShapeDtypeStruct((j_n, s, n), F32),
        compiler_params=_params("parallel", "parallel"))(a, b)


def mm_acc(a, b, res, *, l=None, name):
    j_n, s, kj = a.shape
    n = b.shape[-1]
    tm = min(ROW_TILE, s)

    def body(a_ref, b_ref, r_ref, o_ref):
        acc = r_ref[...]
        for j in range(j_n):
            acc = acc + jnp.dot(a_ref[j], b_ref[j], preferred_element_type=F32)
        o_ref[...] = acc

    return pl.pallas_call(
        body, name=name, grid=(s // tm,),
        in_specs=[pl.BlockSpec((j_n, tm, kj), lambda i: (0, i, 0)),
                  _layer_spec((j_n, kj, n), l, lambda i: (0, 0, 0)),
                  pl.BlockSpec((tm, n), lambda i: (i, 0))],
        out_specs=pl.BlockSpec((tm, n), lambda i: (i, 0)),
        out_shape=jax.ShapeDtypeStruct((s, n), F32),
        compiler_params=_params("parallel"))(a, b, res)


_NT = (((1,), (1,)), ((), ()))
_TN = (((0,), (0,)), ((), ()))


def mm_nt_sum(pairs, *, l=None, name):
    j_n, s, nj = pairs[0][0].shape
    k = pairs[0][1].shape[-2]
    tm = min(ROW_TILE, s)
    n_p = len(pairs)

    def body(*refs):
        o_ref = refs[-1]
        acc = jnp.zeros((tm, k), F32)
        for p in range(n_p):
            dy_ref, w_ref = refs[2 * p], refs[2 * p + 1]
            for j in range(j_n):
                acc = acc + lax.dot_general(dy_ref[j], w_ref[j], _NT, preferred_element_type=F32)
        o_ref[...] = acc

    in_specs, args = [], []
    for dy, w in pairs:
        in_specs += [pl.BlockSpec((j_n, tm, nj), lambda i: (0, i, 0)),
                     _layer_spec((j_n, k, nj), l, lambda i: (0, 0, 0))]
        args += [dy, w]
    return pl.pallas_call(
        body, name=name, grid=(s // tm,), in_specs=in_specs,
        out_specs=pl.BlockSpec((tm, k), lambda i: (i, 0)),
        out_shape=jax.ShapeDtypeStruct((s, k), F32),
        compiler_params=_params("parallel"))(*args)


def mm_nt_each(a, b, *, l=None, name, pin=None):
    s, k = a.shape
    j_n, nj = b.shape[-3], b.shape[-2]
    tm = min(ROW_TILE, s)

    def body(a_ref, b_ref, *rest):
        rest[-1][...] = lax.dot_general(a_ref[...], b_ref[...], _NT, preferred_element_type=F32)

    return pl.pallas_call(
        body, name=name, grid=(j_n, s // tm),
        in_specs=[pl.BlockSpec((tm, k), lambda j, i: (i, 0)),
                  _layer_spec((None, nj, k), l, lambda j, i: (j, 0, 0))] + ([ANY] if pin is not None else []),
        out_specs=pl.BlockSpec((None, tm, nj), lambda j, i: (j, i, 0)),
        out_shape=jax.ShapeDtypeStruct((j_n, s, nj), F32),
        compiler_params=_params("parallel", "parallel"))(a, b, *([pin] if pin is not None else []))


def mm_tn(a, dy, *, name, pin=None):
    ja, s, k = a.shape
    jd, _, n = dy.shape
    j_n = max(ja, jd)
    tk = min(ROW_TILE, s)

    def body(a_ref, d_ref, *rest):
        o_ref = rest[-1]

        @pl.when(pl.program_id(1) == 0)
        def _():
            o_ref[...] = jnp.zeros_like(o_ref)
        o_ref[...] += lax.dot_general(a_ref[...], d_ref[...], _TN, preferred_element_type=F32)

    return pl.pallas_call(
        body, name=name, grid=(j_n, s // tk),
        in_specs=[pl.BlockSpec((None, tk, k), (lambda j, i: (j, i, 0)) if ja > 1 else (lambda j, i: (0, i, 0))),
                  pl.BlockSpec((None, tk, n), (lambda j, i: (j, i, 0)) if jd > 1 else (lambda j, i: (0, i, 0)))]
        + ([ANY] if pin is not None else []),
        out_specs=pl.BlockSpec((None, k, n), lambda j, i: (j, 0, 0)),
        out_shape=jax.ShapeDtypeStruct((j_n, k, n), F32),
        compiler_params=_params("parallel", "arbitrary"))(a, dy, *([pin] if pin is not None else []))


def mm_tn_shared(a, dy, *, name):
    s, k = a.shape
    j_n, _, n = dy.shape
    tk = min(ROW_TILE, s)

    def body(a_ref, d_ref, o_ref):
        @pl.when(pl.program_id(0) == 0)
        def _():
            o_ref[...] = jnp.zeros_like(o_ref)
        a_t = a_ref[...]
        for j in range(j_n):
            o_ref[j] += lax.dot_general(a_t, d_ref[j], _TN, preferred_element_type=F32)

    return pl.pallas_call(
        body, name=name, grid=(s // tk,),
        in_specs=[pl.BlockSpec((tk, k), lambda i: (i, 0)), pl.BlockSpec((j_n, tk, n), lambda i: (0, i, 0))],
        out_specs=pl.BlockSpec((j_n, k, n), lambda i: (0, 0, 0)),
        out_shape=jax.ShapeDtypeStruct((j_n, k, n), F32),
        compiler_params=_params("arbitrary"))(a, dy)


def _back(x, k):
    return pltpu.roll(x, k, 0)


def _fwd(x, k):
    return pltpu.roll(x, x.shape[0] - k, 0)


def _causal_conv(x, w_ref):
    return w_ref[0:1, :] * _back(x, 2) + w_ref[1:2, :] * _back(x, 1) + w_ref[2:3, :] * x


def _causal_conv_t(dy, w_ref):
    return w_ref[2:3, :] * dy + w_ref[1:2, :] * _fwd(dy, 1) + w_ref[0:1, :] * _fwd(dy, 2)


def _gelu(x):
    return 0.5 * x * (1.0 + lax.erf(x * 0.7071067811865476))


def _gelu_grad(x):
    return 0.5 * (1.0 + lax.erf(x * 0.7071067811865476)) + x * jnp.exp(-0.5 * x * x) * 0.3989422804014327


def _colsum(x):
    return jnp.sum(x, axis=0, keepdims=True)


def rms_fwd(h, gain, *, name, pin=None):
    s, d = h.shape
    ts = min(ROW_TILE, s)

    def body(h_ref, g_ref, *rest):
        o_ref, r_ref = rest[-2:]
        x = h_ref[...]
        rstd = lax.rsqrt(jnp.mean(x * x, axis=-1, keepdims=True) + EPS)
        o_ref[...] = (x * rstd * g_ref[...]).astype(BF16)
        r_ref[...] = rstd

    return pl.pallas_call(
        body, name=name, grid=(s // ts,),
        in_specs=[pl.BlockSpec((ts, d), lambda i: (i, 0)), pl.BlockSpec((1, d), lambda i: (0, 0))]
        + ([ANY] if pin is not None else []),
        out_specs=[pl.BlockSpec((ts, d), lambda i: (i, 0)), pl.BlockSpec((ts, 1), lambda i: (i, 0))],
        out_shape=[jax.ShapeDtypeStruct((s, d), BF16), jax.ShapeDtypeStruct((s, 1), F32)],
        compiler_params=_params("parallel"))(h, gain, *([pin] if pin is not None else []))


def rms_bwd(dxn, h, gain, rstd, dres, *, name):
    s, d = h.shape
    ts = min(ROW_TILE, s)

    def body(dx_ref, h_ref, g_ref, r_ref, dr_ref, o_ref, ob_ref, dg_ref):
        @pl.when(pl.program_id(0) == 0)
        def _():
            dg_ref[...] = jnp.zeros_like(dg_ref)
        rstd_v = r_ref[...]
        hhat = h_ref[...] * rstd_v
        dx = dx_ref[...]
        dg_ref[...] += _colsum(dx * hhat)
        dxg = dx * g_ref[...]
        dh = dr_ref[...] + rstd_v * (dxg - hhat * jnp.mean(dxg * hhat, axis=-1, keepdims=True))
        o_ref[...] = dh
        ob_ref[...] = dh.astype(BF16)

    row = pl.BlockSpec((ts, d), lambda i: (i, 0))
    vec = pl.BlockSpec((1, d), lambda i: (0, 0))
    return pl.pallas_call(
        body, name=name, grid=(s // ts,),
        in_specs=[row, row, vec, pl.BlockSpec((ts, 1), lambda i: (i, 0)), row],
        out_specs=[row, row, vec],
        out_shape=[jax.ShapeDtypeStruct((s, d), F32), jax.ShapeDtypeStruct((s, d), BF16),
                   jax.ShapeDtypeStruct((1, d), F32)],
        compiler_params=_params("arbitrary"))(dxn, h, gain, rstd, dres)


def final_loss(h, target, gain, *, name):
    s, d = h.shape
    ts = min(ROW_TILE, s)

    def body(h_ref, t_ref, g_ref, o_ref, ob_ref, l_ref, dg_ref):
        @pl.when(pl.program_id(0) == 0)
        def _():
            l_ref[...] = jnp.zeros_like(l_ref)
            dg_ref[...] = jnp.zeros_like(dg_ref)
        x = h_ref[...]
        rstd = lax.rsqrt(jnp.mean(x * x, axis=-1, keepdims=True) + EPS)
        hhat = x * rstd
        err = hhat * g_ref[...] - t_ref[...]
        l_ref[...] += 0.5 * jnp.sum(jnp.mean(err * err, axis=-1, keepdims=True), axis=0, keepdims=True)
        dy = err * (1.0 / d)
        dg_ref[...] += _colsum(dy * hhat)
        dyg = dy * g_ref[...]
        dh = rstd * (dyg - hhat * jnp.mean(dyg * hhat, axis=-1, keepdims=True))
        o_ref[...] = dh
        ob_ref[...] = dh.astype(BF16)

    row = pl.BlockSpec((ts, d), lambda i: (i, 0))
    vec = pl.BlockSpec((1, d), lambda i: (0, 0))
    return pl.pallas_call(
        body, name=name, grid=(s // ts,),
        in_specs=[row, row, vec],
        out_specs=[row, row, pl.BlockSpec((1, 128), lambda i: (0, 0)), vec],
        out_shape=[jax.ShapeDtypeStruct((s, d), F32), jax.ShapeDtypeStruct((s, d), BF16),
                   jax.ShapeDtypeStruct((1, 128), F32), jax.ShapeDtypeStruct((1, d), F32)],
        compiler_params=_params("arbitrary"))(h, target, gain)


def _halo_specs(n_lead, ts, width, n_tiles):
    hb = ts // HALO
    prev = pl.BlockSpec((n_lead, HALO, width), lambda i: (0, jnp.maximum(i * hb - 1, 0), 0))
    nxt = pl.BlockSpec((n_lead, HALO, width), lambda i: (0, jnp.minimum((i + 1) * hb, n_tiles * hb - 1), 0))
    return prev, nxt


def _pool_fwd(z_ext, g, pos):
    w = POOL_WINDOWS[g]
    zg = z_ext[:, g * GROUP:(g + 1) * GROUP]
    acc = zg
    sh = 1
    while sh < w:
        acc = acc + _back(acc, sh)
        sh *= 2
    return acc[HALO:] / jnp.minimum(pos, float(w)) - zg[HALO:]


def even_layer_fwd(h, gain, win, conv_a, w_pool, pool_scale, wout, *, name, pin=None):
    s, d = h.shape
    w = win.shape[-1]
    ts = min(ROW_TILE, s)
    hb = ts // HALO

    def body(h_ref, hp_ref, gain_ref, win_ref, ca_ref, wp_ref, ps_ref, wout_ref, *rest):
        o_ref, xn_ref, r_ref, p_ref, m_ref = rest[-5:]
        i = pl.program_id(0)
        keep = jnp.where(i > 0, 1.0, 0.0)
        h_ext = jnp.concatenate([hp_ref[...], h_ref[...]], axis=0)
        rstd = lax.rsqrt(jnp.mean(h_ext * h_ext, axis=-1, keepdims=True) + EPS)
        xn_ext = (h_ext * rstd * gain_ref[...]).astype(BF16)
        xn_ref[...] = xn_ext[HALO:]
        r_ref[...] = rstd[HALO:]
        p32 = []
        for k in range(4):
            pk = jnp.dot(xn_ext, win_ref[k], preferred_element_type=F32).astype(BF16)
            p_ref[k] = pk[HALO:]
            pk = pk.astype(F32)
            p32.append(jnp.concatenate([pk[:HALO] * keep, pk[HALO:]], axis=0))
        m_ref[:, 0:w] = (p32[0][HALO:] * _causal_conv(p32[1] * p32[2], ca_ref)[HALO:]).astype(BF16)
        pos = (i * ts + lax.broadcasted_iota(jnp.int32, (ts, 1), 0) + 1).astype(F32)
        for g in range(len(POOL_WINDOWS)):
            pooled = _pool_fwd(p32[3], g, pos)
            mixed = jnp.dot(pooled.astype(BF16), wp_ref[g], preferred_element_type=F32)
            cols = slice(g * GROUP, (g + 1) * GROUP)
            m_ref[:, w + g * GROUP:w + (g + 1) * GROUP] = (mixed * ps_ref[:, cols]).astype(BF16)
        o_ref[...] = h_ref[...] + jnp.dot(m_ref[...], wout_ref[...], preferred_element_type=F32)

    const = lambda shape: pl.BlockSpec(shape, lambda i: (0,) * len(shape))
    row = pl.BlockSpec((ts, d), lambda i: (i, 0))
    return pl.pallas_call(
        body, name=name, grid=(s // ts,),
        in_specs=[row, pl.BlockSpec((HALO, d), lambda i: (jnp.maximum(i * hb - 1, 0), 0)), const((1, d)),
                  const((4, d, w)), const((3, w)), const((4, GROUP, GROUP)), const((1, w)), const((2 * w, d))]
        + ([ANY] if pin is not None else []),
        out_specs=[row, row, pl.BlockSpec((ts, 1), lambda i: (i, 0)),
                   pl.BlockSpec((4, ts, w), lambda i: (0, i, 0)), pl.BlockSpec((ts, 2 * w), lambda i: (i, 0))],
        out_shape=[jax.ShapeDtypeStruct((s, d), F32), jax.ShapeDtypeStruct((s, d), BF16),
                   jax.ShapeDtypeStruct((s, 1), F32), jax.ShapeDtypeStruct((4, s, w), BF16),
                   jax.ShapeDtypeStruct((s, 2 * w), BF16)],
        compiler_params=_params("parallel"))(
            h, h, gain, win, conv_a, w_pool, pool_scale, wout, *([pin] if pin is not None else []))


def even_bwd(proj, dh_bf, wout, conv_a, w_pool, w_pool_t, pool_scale, *, name, pin=None):
    _, s, w = proj.shape
    d = dh_bf.shape[1]
    ts = min(ROW_TILE, s)
    n_t = s // ts
    prev, nxt = _halo_specs(4, ts, w, n_t)
    hb = ts // HALO
    n_ext = ts + HALO

    def body(p_ref, pp_ref, pn_ref, dh_ref, dhn_ref, wout_ref, ca_ref, wp_ref, wpt_ref, ps_ref, *rest):
        dp_ref, dca_ref, dwp_ref, dps_ref = rest[-4:]
        i = pl.program_id(0)

        @pl.when(i == 0)
        def _():
            dca_ref[...] = jnp.zeros_like(dca_ref)
            dwp_ref[...] = jnp.zeros_like(dwp_ref)
            dps_ref[...] = jnp.zeros_like(dps_ref)

        keep_p = jnp.where(i > 0, 1.0, 0.0)
        keep_n = jnp.where(i < n_t - 1, 1.0, 0.0)
        dmix = lax.dot_general(jnp.concatenate([dh_ref[...], dhn_ref[...]], axis=0), wout_ref[...], _NT,
                               preferred_element_type=F32)
        a_b, a_c, a_v = (p_ref[k].astype(F32) for k in range(3))
        cv_ext = jnp.concatenate([pp_ref[1].astype(F32) * pp_ref[2].astype(F32) * keep_p, a_c * a_v], axis=0)
        dy_a = dmix[:ts, 0:w]
        dp_ref[0] = (dy_a * _causal_conv(cv_ext, ca_ref)[HALO:]).astype(BF16)
        dcc = dy_a * a_b
        dca_ref[2:3, :] += _colsum(dcc * cv_ext[HALO:])
        dca_ref[1:2, :] += _colsum(dcc * _back(cv_ext, 1)[HALO:])
        dca_ref[0:1, :] += _colsum(dcc * _back(cv_ext, 2)[HALO:])
        dcc_ext = jnp.concatenate([dcc, dmix[ts:, 0:w] * pn_ref[0].astype(F32) * keep_n], axis=0)
        dcv = _causal_conv_t(dcc_ext, ca_ref)[:ts]
        dp_ref[1] = (dcv * a_v).astype(BF16)
        dp_ref[2] = (dcv * a_c).astype(BF16)
        z_ext = jnp.concatenate([pp_ref[3].astype(F32) * keep_p, p_ref[3].astype(F32)], axis=0)
        pos = (i * ts + lax.broadcasted_iota(jnp.int32, (ts, 1), 0) + 1).astype(F32)
        pos_ext = (i * ts + lax.broadcasted_iota(jnp.int32, (n_ext, 1), 0) + 1).astype(F32)
        for g, win in enumerate(POOL_WINDOWS):
            cols = slice(g * GROUP, (g + 1) * GROUP)
            ycols = slice(w + g * GROUP, w + (g + 1) * GROUP)
            pooled = _pool_fwd(z_ext, g, pos).astype(BF16)
            mixed = jnp.dot(pooled, wp_ref[g], preferred_element_type=F32)
            dy_b = dmix[:ts, ycols]
            dps_ref[:, cols] += _colsum(dy_b * mixed)
            dmixed_ext = jnp.concatenate([dy_b, dmix[ts:, ycols] * keep_n], axis=0) * ps_ref[:, cols]
            dmixed_ext = dmixed_ext.astype(BF16)
            dwp_ref[g] += lax.dot_general(pooled, dmixed_ext[:ts], _TN, preferred_element_type=F32)
            dpooled = jnp.dot(dmixed_ext, wpt_ref[g], preferred_element_type=F32)
            acc = dpooled / jnp.minimum(pos_ext, float(win))
            sh = 1
            while sh < win:
                acc = acc + _fwd(acc, sh)
                sh *= 2
            dp_ref[3, :, cols] = (acc[:ts] - dpooled[:ts]).astype(BF16)

    tile4 = pl.BlockSpec((4, ts, w), lambda i: (0, i, 0))
    const = lambda shape: pl.BlockSpec(shape, lambda i: (0,) * len(shape))
    return pl.pallas_call(
        body, name=name, grid=(n_t,),
        in_specs=[tile4, prev, nxt, pl.BlockSpec((ts, d), lambda i: (i, 0)),
                  pl.BlockSpec((HALO, d), lambda i: (jnp.minimum((i + 1) * hb, n_t * hb - 1), 0)),
                  const((2 * w, d)), const((3, w)), const((4, GROUP, GROUP)), const((4, GROUP, GROUP)), const((1, w))]
        + ([ANY] if pin is not None else []),
        out_specs=[tile4, const((3, w)), const((4, GROUP, GROUP)), const((1, w))],
        out_shape=[jax.ShapeDtypeStruct((4, s, w), BF16), jax.ShapeDtypeStruct((3, w), F32),
                   jax.ShapeDtypeStruct((4, GROUP, GROUP), F32), jax.ShapeDtypeStruct((1, w), F32)],
        compiler_params=_params("arbitrary"))(
            proj, proj, proj, dh_bf, dh_bf, wout, conv_a, w_pool, w_pool_t, pool_scale,
            *([pin] if pin is not None else []))


def _ffn_halo(ts, f, n_t, nxt):
    hb = ts // HALO
    if nxt:
        return pl.BlockSpec((None, HALO, f), lambda j, i: (j, jnp.minimum((i + 1) * hb, n_t * hb - 1), 0))
    return pl.BlockSpec((None, HALO, f), lambda j, i: (j, jnp.maximum(i * hb - 1, 0), 0))


def ffn_act_fwd(g, up, cw, cb, *, name):
    j_n, s, f = g.shape
    ts = min(ROW_TILE, s)
    n_t = s // ts

    def body(g_ref, gp_ref, u_ref, cw_ref, cb_ref, o_ref):
        keep = jnp.where(pl.program_id(1) > 0, 1.0, 0.0)
        g_ext = jnp.concatenate([gp_ref[...] * keep, g_ref[...]], axis=0)
        gc = _causal_conv(g_ext, cw_ref)[HALO:] + cb_ref[...]
        o_ref[...] = (gc * jax.nn.sigmoid(gc) * u_ref[...]).astype(BF16)

    tile = pl.BlockSpec((None, ts, f), lambda j, i: (j, i, 0))
    return pl.pallas_call(
        body, name=name, grid=(j_n, n_t),
        in_specs=[tile, _ffn_halo(ts, f, n_t, False), tile,
                  pl.BlockSpec((None, 3, f), lambda j, i: (j, 0, 0)),
                  pl.BlockSpec((None, 1, f), lambda j, i: (j, 0, 0))],
        out_specs=tile,
        out_shape=jax.ShapeDtypeStruct((j_n, s, f), BF16),
        compiler_params=_params("parallel", "parallel"))(g, g, up, cw, cb)


def ffn_act_bwd(g, up, dact, cw, cb, *, name):
    j_n, s, f = g.shape
    ts = min(ROW_TILE, s)
    n_t = s // ts

    def body(g_ref, gp_ref, gn_ref, u_ref, un_ref, d_ref, dn_ref, cw_ref, cb_ref,
             dg_ref, du_ref, dcw_ref, dcb_ref):
        i = pl.program_id(1)

        @pl.when(i == 0)
        def _():
            dcw_ref[...] = jnp.zeros_like(dcw_ref)
            dcb_ref[...] = jnp.zeros_like(dcb_ref)

        keep_p = jnp.where(i > 0, 1.0, 0.0)
        keep_n = jnp.where(i < n_t - 1, 1.0, 0.0)
        g_ext = jnp.concatenate([gp_ref[...] * keep_p, g_ref[...], gn_ref[...]], axis=0)
        gc = _causal_conv(g_ext, cw_ref)[HALO:] + cb_ref[...]
        sig = jax.nn.sigmoid(gc)
        dact_ext = jnp.concatenate([d_ref[...], dn_ref[...] * keep_n], axis=0)
        du_ref[...] = (dact_ext * gc * sig)[:ts].astype(BF16)
        up_ext = jnp.concatenate([u_ref[...], un_ref[...]], axis=0)
        dgc = dact_ext * up_ext * (sig * (1.0 + gc * (1.0 - sig)))
        dg_ref[...] = _causal_conv_t(dgc, cw_ref)[:ts].astype(BF16)
        dgc_t = dgc[:ts]
        dcb_ref[...] += _colsum(dgc_t)
        dcw_ref[2:3, :] += _colsum(dgc_t * g_ext[HALO:HALO + ts])
        dcw_ref[1:2, :] += _colsum(dgc_t * _back(g_ext, 1)[HALO:HALO + ts])
        dcw_ref[0:1, :] += _colsum(dgc_t * _back(g_ext, 2)[HALO:HALO + ts])

    tile = pl.BlockSpec((None, ts, f), lambda j, i: (j, i, 0))
    prev, nxt = _ffn_halo(ts, f, n_t, False), _ffn_halo(ts, f, n_t, True)
    return pl.pallas_call(
        body, name=name, grid=(j_n, n_t),
        in_specs=[tile, prev, nxt, tile, nxt, tile, nxt,
                  pl.BlockSpec((None, 3, f), lambda j, i: (j, 0, 0)),
                  pl.BlockSpec((None, 1, f), lambda j, i: (j, 0, 0))],
        out_specs=[tile, tile, pl.BlockSpec((None, 3, f), lambda j, i: (j, 0, 0)),
                   pl.BlockSpec((None, 1, f), lambda j, i: (j, 0, 0))],
        out_shape=[jax.ShapeDtypeStruct((j_n, s, f), BF16), jax.ShapeDtypeStruct((j_n, s, f), BF16),
                   jax.ShapeDtypeStruct((j_n, 3, f), F32), jax.ShapeDtypeStruct((j_n, 1, f), F32)],
        compiler_params=_params("parallel", "arbitrary"))(g, g, g, up, up, dact, dact, cw, cb)


def ffn_in_fwd(hn, wg, wu, cw, cb, *, name):
    s, d = hn.shape
    j_n, f, _ = wg.shape
    tm = min(ROW_TILE, s)
    hb = tm // HALO

    def body(x_ref, xp_ref, wg_ref, wu_ref, cw_ref, cb_ref, g_ref, u_ref, a_ref):
        i, j = pl.program_id(0), pl.program_id(1)
        x_ext = jnp.concatenate([xp_ref[...], x_ref[...]], axis=0)
        g_ext = lax.dot_general(x_ext, wg_ref[j], _NT, preferred_element_type=F32).astype(BF16)
        up = lax.dot_general(x_ref[...], wu_ref[j], _NT, preferred_element_type=F32).astype(BF16)
        g_ref[...] = g_ext[HALO:]
        u_ref[...] = up
        keep = jnp.where(i > 0, 1.0, 0.0)
        g32 = jnp.concatenate([g_ext[:HALO].astype(F32) * keep, g_ext[HALO:].astype(F32)], axis=0)
        gc = _causal_conv(g32, cw_ref)[HALO:] + cb_ref[...]
        a_ref[...] = (gc * jax.nn.sigmoid(gc) * up.astype(F32)).astype(BF16)

    whole = pl.BlockSpec((j_n, f, d), lambda i, j: (0, 0, 0))
    tile = pl.BlockSpec((None, tm, f), lambda i, j: (j, i, 0))
    shape = jax.ShapeDtypeStruct((j_n, s, f), BF16)
    return pl.pallas_call(
        body, name=name, grid=(s // tm, j_n),
        in_specs=[pl.BlockSpec((tm, d), lambda i, j: (i, 0)),
                  pl.BlockSpec((HALO, d), lambda i, j: (jnp.maximum(i * hb - 1, 0), 0)),
                  whole, whole,
                  pl.BlockSpec((None, 3, f), lambda i, j: (j, 0, 0)),
                  pl.BlockSpec((None, 1, f), lambda i, j: (j, 0, 0))],
        out_specs=[tile, tile, tile], out_shape=[shape, shape, shape],
        compiler_params=_params("parallel", "parallel"))(hn, hn, wg, wu, cw, cb)


def ffn_fwd(h, gain, wg, wu, wd, cw, cb, *, name):
    s, d = h.shape
    j_n, f, _ = wg.shape
    tm = min(ROW_TILE, s)
    hb = tm // HALO

    def body(h_ref, hp_ref, gain_ref, wg_ref, wu_ref, wd_ref, cw_ref, cb_ref, o_ref, xn_ref, r_ref, g_ref, u_ref, x_s):
        i, j = pl.program_id(0), pl.program_id(1)

        @pl.when(j == 0)
        def _():
            h_ext = jnp.concatenate([hp_ref[...], h_ref[...]], axis=0)
            rstd = lax.rsqrt(jnp.mean(h_ext * h_ext, axis=-1, keepdims=True) + EPS)
            x_s[...] = (h_ext * rstd * gain_ref[...]).astype(BF16)
            xn_ref[...] = x_s[HALO:, :]
            r_ref[...] = rstd[HALO:]
            o_ref[...] = h_ref[...]

        g_ext = lax.dot_general(x_s[...], wg_ref[j], _NT, preferred_element_type=F32).astype(BF16)
        up = lax.dot_general(x_s[HALO:, :], wu_ref[j], _NT, preferred_element_type=F32).astype(BF16)
        g_ref[...] = g_ext[HALO:]
        u_ref[...] = up
        keep = jnp.where(i > 0, 1.0, 0.0)
        g32 = jnp.concatenate([g_ext[:HALO].astype(F32) * keep, g_ext[HALO:].astype(F32)], axis=0)
        gc = _causal_conv(g32, cw_ref)[HALO:] + cb_ref[...]
        act = (gc * jax.nn.sigmoid(gc) * up.astype(F32)).astype(BF16)
        o_ref[...] += jnp.dot(act, wd_ref[j], preferred_element_type=F32)

    whole = pl.BlockSpec((j_n, f, d), lambda i, j: (0, 0, 0), pipeline_mode=pl.Buffered(1))
    row = pl.BlockSpec((tm, d), lambda i, j: (i, 0))
    tile = pl.BlockSpec((None, tm, f), lambda i, j: (j, i, 0))
    return pl.pallas_call(
        body, name=name, grid=(s // tm, j_n),
        in_specs=[row, pl.BlockSpec((HALO, d), lambda i, j: (jnp.maximum(i * hb - 1, 0), 0)),
                  pl.BlockSpec((1, d), lambda i, j: (0, 0)), whole, whole, whole,
                  pl.BlockSpec((None, 3, f), lambda i, j: (j, 0, 0)),
                  pl.BlockSpec((None, 1, f), lambda i, j: (j, 0, 0))],
        out_specs=[row, row, pl.BlockSpec((tm, 1), lambda i, j: (i, 0)), tile, tile],
        out_shape=[jax.ShapeDtypeStruct((s, d), F32), jax.ShapeDtypeStruct((s, d), BF16),
                   jax.ShapeDtypeStruct((s, 1), F32), jax.ShapeDtypeStruct((j_n, s, f), BF16),
                   jax.ShapeDtypeStruct((j_n, s, f), BF16)],
        scratch_shapes=[pltpu.VMEM((HALO + tm, d), BF16)],
        compiler_params=_params("parallel", "arbitrary"))(h, h, gain, wg, wu, wd, cw, cb)


def ffn_bwd_a(dh_bf, hn, g, up, wd, cw, cb, *, name, pin=None):
    s, d = hn.shape
    j_n, _, f = g.shape
    tm = min(ROW_TILE, s)
    n_t = s // tm
    hb = tm // HALO

    def body(dh_ref, dhn_ref, x_ref, g_ref, gp_ref, gn_ref, u_ref, un_ref, wd_ref, cw_ref, cb_ref, *rest):
        dg_ref, du_ref, dwg_ref, dwu_ref, dwd_ref, dcw_ref, dcb_ref = rest[-7:]
        i = pl.program_id(1)

        @pl.when(i == 0)
        def _():
            for r in (dwg_ref, dwu_ref, dwd_ref, dcw_ref, dcb_ref):
                r[...] = jnp.zeros_like(r)

        keep_p = jnp.where(i > 0, 1.0, 0.0)
        keep_n = jnp.where(i < n_t - 1, 1.0, 0.0)
        dh = dh_ref[...]
        dact = lax.dot_general(jnp.concatenate([dh, dhn_ref[...]], axis=0), wd_ref[...], _NT,
                               preferred_element_type=F32)
        dact = jnp.concatenate([dact[:tm], dact[tm:] * keep_n], axis=0)
        g_ext = jnp.concatenate([gp_ref[...].astype(F32) * keep_p, g_ref[...].astype(F32),
                                 gn_ref[...].astype(F32)], axis=0)
        gc = _causal_conv(g_ext, cw_ref)[HALO:] + cb_ref[...]
        sig = jax.nn.sigmoid(gc)
        silu = gc * sig
        up_ext = jnp.concatenate([u_ref[...], un_ref[...]], axis=0).astype(F32)
        act = (silu * up_ext)[:tm].astype(BF16)
        dwd_ref[...] += lax.dot_general(act, dh, _TN, preferred_element_type=F32)
        dup = (dact * silu)[:tm].astype(BF16)
        du_ref[...] = dup
        dgc = dact * up_ext * (sig * (1.0 + gc * (1.0 - sig)))
        dg = _causal_conv_t(dgc, cw_ref)[:tm].astype(BF16)
        dg_ref[...] = dg
        x = x_ref[...]
        dwg_ref[...] += lax.dot_general(dg, x, _TN, preferred_element_type=F32)
        dwu_ref[...] += lax.dot_general(dup, x, _TN, preferred_element_type=F32)
        dgc_t = dgc[:tm]
        dcb_ref[...] += _colsum(dgc_t)
        dcw_ref[2:3, :] += _colsum(dgc_t * g_ext[HALO:HALO + tm])
        dcw_ref[1:2, :] += _colsum(dgc_t * _back(g_ext, 1)[HALO:HALO + tm])
        dcw_ref[0:1, :] += _colsum(dgc_t * _back(g_ext, 2)[HALO:HALO + tm])

    rows = pl.BlockSpec((tm, d), lambda j, i: (i, 0))
    rows_next = pl.BlockSpec((HALO, d), lambda j, i: (jnp.minimum((i + 1) * hb, n_t * hb - 1), 0))
    tile = pl.BlockSpec((None, tm, f), lambda j, i: (j, i, 0))
    prev, nxt = _ffn_halo(tm, f, n_t, False), _ffn_halo(tm, f, n_t, True)
    per_j = lambda r, c: pl.BlockSpec((None, r, c), lambda j, i: (j, 0, 0))
    return pl.pallas_call(
        body, name=name, grid=(j_n, n_t),
        in_specs=[rows, rows_next, rows, tile, prev, nxt, tile, nxt, per_j(f, d), per_j(3, f), per_j(1, f)]
        + ([ANY] if pin is not None else []),
        out_specs=[tile, tile, per_j(f, d), per_j(f, d), per_j(f, d), per_j(3, f), per_j(1, f)],
        out_shape=[jax.ShapeDtypeStruct((j_n, s, f), BF16), jax.ShapeDtypeStruct((j_n, s, f), BF16),
                   jax.ShapeDtypeStruct((j_n, f, d), F32), jax.ShapeDtypeStruct((j_n, f, d), F32),
                   jax.ShapeDtypeStruct((j_n, f, d), F32), jax.ShapeDtypeStruct((j_n, 3, f), F32),
                   jax.ShapeDtypeStruct((j_n, 1, f), F32)],
        compiler_params=_params("parallel", "arbitrary"))(
            dh_bf, dh_bf, hn, g, g, g, up, up, wd, cw, cb, *([pin] if pin is not None else []))


def dx_rms_bwd(pairs, h, gain, rstd, dres, *, name, pin=None):
    j_n, s, f = pairs[0][0].shape
    d = h.shape[1]
    tm = min(ROW_TILE, s)
    n_p = len(pairs)
    dims = [_NT if w.shape[1:] == (d, f) else (((1,), (0,)), ((), ())) for _, w in pairs]

    def body(*refs):
        dy_refs, w_refs = refs[:n_p], refs[n_p:2 * n_p]
        h_ref, g_ref, r_ref, dr_ref = refs[2 * n_p:2 * n_p + 4]
        o_ref, ob_ref, dgain_ref = refs[-3:]

        @pl.when(pl.program_id(0) == 0)
        def _():
            dgain_ref[...] = jnp.zeros_like(dgain_ref)
        dx = jnp.zeros((tm, d), F32)
        for j in range(j_n):
            for p in range(n_p):
                dx = dx + lax.dot_general(dy_refs[p][j], w_refs[p][j], dims[p], preferred_element_type=F32)
        rstd_v = r_ref[...]
        hhat = h_ref[...] * rstd_v
        dgain_ref[...] += _colsum(dx * hhat)
        dxg = dx * g_ref[...]
        dh = dr_ref[...] + rstd_v * (dxg - hhat * jnp.mean(dxg * hhat, axis=-1, keepdims=True))
        o_ref[...] = dh
        ob_ref[...] = dh.astype(BF16)

    tile4 = pl.BlockSpec((j_n, tm, f), lambda i: (0, i, 0))
    whole = [pl.BlockSpec(w.shape, lambda i: (0, 0, 0), pipeline_mode=pl.Buffered(1)) for _, w in pairs]
    row = pl.BlockSpec((tm, d), lambda i: (i, 0))
    vec = pl.BlockSpec((1, d), lambda i: (0, 0))
    return pl.pallas_call(
        body, name=name, grid=(s // tm,),
        in_specs=[tile4] * n_p + whole + [row, vec, pl.BlockSpec((tm, 1), lambda i: (i, 0)), row]
        + ([ANY] if pin is not None else []),
        out_specs=[row, row, vec],
        out_shape=[jax.ShapeDtypeStruct((s, d), F32), jax.ShapeDtypeStruct((s, d), BF16),
                   jax.ShapeDtypeStruct((1, d), F32)],
        compiler_params=_params("arbitrary"))(
            *[p[0] for p in pairs], *[p[1] for p in pairs], h, gain, rstd, dres, *([pin] if pin is not None else []))


def _sgu_gate(vn_bf, ws_ref, bs_ref, h, rows):
    tri = lax.broadcasted_iota(jnp.int32, (CHUNK, CHUNK), 0) >= lax.broadcasted_iota(jnp.int32, (CHUNK, CHUNK), 1)
    ws = jnp.where(tri, ws_ref[h], 0.0).astype(BF16)
    cols = slice((h % 4) * GROUP, (h % 4 + 1) * GROUP)
    return ws, jnp.dot(ws, vn_bf[h // 4][rows, cols], preferred_element_type=F32) + bs_ref[h]


def odd_layer_fwd(h, gain, win, sgu_norm, w_spatial, b_spatial, wout, *, name):
    s, d = h.shape
    w = win.shape[-1]
    ts = min(ROW_TILE, s)
    n_heads = w_spatial.shape[0]

    def body(h_ref, gain_ref, win_ref, n_ref, ws_ref, bs_ref, wout_ref, o_ref, xn_ref, r_ref, p_ref, m_ref, rv_ref):
        x = h_ref[...]
        rstd_x = lax.rsqrt(jnp.mean(x * x, axis=-1, keepdims=True) + EPS)
        xn = (x * rstd_x * gain_ref[...]).astype(BF16)
        xn_ref[...] = xn
        r_ref[...] = rstd_x
        for k in range(4):
            p_ref[k] = jnp.dot(xn, win_ref[k], preferred_element_type=F32).astype(BF16)
        v = [_gelu(p_ref[2].astype(F32)), _gelu(p_ref[3].astype(F32))]
        ms = (jnp.sum(v[0] * v[0], axis=-1, keepdims=True) + jnp.sum(v[1] * v[1], axis=-1, keepdims=True)) / (2 * w)
        rstd = lax.rsqrt(ms + EPS)
        rv_ref[...] = rstd
        vn = [(v[k] * rstd * n_ref[:, k * w:(k + 1) * w]).astype(BF16) for k in range(2)]
        for hd in range(n_heads):
            cols = slice((hd % 4) * GROUP, (hd % 4 + 1) * GROUP)
            for c in range(ts // CHUNK):
                rows = slice(c * CHUNK, (c + 1) * CHUNK)
                _, gate = _sgu_gate(vn, ws_ref, bs_ref, hd, rows)
                u = _gelu(p_ref[hd // 4, rows, cols].astype(F32))
                m_ref[rows, hd * GROUP:(hd + 1) * GROUP] = (u * gate).astype(BF16)
        o_ref[...] = x + jnp.dot(m_ref[...], wout_ref[...], preferred_element_type=F32)

    const = lambda shape: pl.BlockSpec(shape, lambda i: (0,) * len(shape))
    row = pl.BlockSpec((ts, d), lambda i: (i, 0))
    col1 = pl.BlockSpec((ts, 1), lambda i: (i, 0))
    return pl.pallas_call(
        body, name=name, grid=(s // ts,),
        in_specs=[row, const((1, d)), const((4, d, w)), const((1, 2 * w)),
                  const((n_heads, CHUNK, CHUNK)), const((n_heads, CHUNK, 1)), const((2 * w, d))],
        out_specs=[row, row, col1, pl.BlockSpec((4, ts, w), lambda i: (0, i, 0)),
                   pl.BlockSpec((ts, 2 * w), lambda i: (i, 0)), col1],
        out_shape=[jax.ShapeDtypeStruct((s, d), F32), jax.ShapeDtypeStruct((s, d), BF16),
                   jax.ShapeDtypeStruct((s, 1), F32), jax.ShapeDtypeStruct((4, s, w), BF16),
                   jax.ShapeDtypeStruct((s, 2 * w), BF16), jax.ShapeDtypeStruct((s, 1), F32)],
        compiler_params=_params("parallel"))(h, gain, win, sgu_norm, w_spatial, b_spatial, wout)


def sgu_bwd(p, dh_bf, wout, rstd, sgu_norm, w_spatial, b_spatial, *, name, pin=None):
    _, s, w = p.shape
    d = dh_bf.shape[1]
    ts = min(ROW_TILE, s)
    n_heads = w_spatial.shape[0]

    def body(p_ref, dh_ref, wout_ref, r_ref, n_ref, ws_ref, bs_ref, *rest):
        dp_ref, dn_ref, dws_ref, dbs_ref, dvn_ref, dm_ref = rest[-6:]

        @pl.when(pl.program_id(0) == 0)
        def _():
            dn_ref[...] = jnp.zeros_like(dn_ref)
            dws_ref[...] = jnp.zeros_like(dws_ref)
            dbs_ref[...] = jnp.zeros_like(dbs_ref)

        dm_ref[...] = lax.dot_general(dh_ref[...], wout_ref[...], _NT, preferred_element_type=F32)
        rstd_v = r_ref[...]
        vhat = [_gelu(p_ref[2 + k].astype(F32)) * rstd_v for k in range(2)]
        vn = [(vhat[k] * n_ref[:, k * w:(k + 1) * w]).astype(BF16) for k in range(2)]
        tri = lax.broadcasted_iota(jnp.int32, (CHUNK, CHUNK), 0) >= lax.broadcasted_iota(jnp.int32, (CHUNK, CHUNK), 1)
        for h in range(n_heads):
            cols = slice((h % 4) * GROUP, (h % 4 + 1) * GROUP)
            ocols = slice(h * GROUP, (h + 1) * GROUP)
            for c in range(ts // CHUNK):
                rows = slice(c * CHUNK, (c + 1) * CHUNK)
                ws, gate = _sgu_gate(vn, ws_ref, bs_ref, h, rows)
                pu = p_ref[h // 4, rows, cols].astype(F32)
                dm = dm_ref[rows, ocols]
                dp_ref[h // 4, rows, cols] = (dm * gate * _gelu_grad(pu)).astype(BF16)
                dgate = dm * _gelu(pu)
                dbs_ref[h] += jnp.sum(dgate, axis=-1, keepdims=True)
                dgate_bf = dgate.astype(BF16)
                dws = lax.dot_general(dgate_bf, vn[h // 4][rows, cols], _NT, preferred_element_type=F32)
                dws_ref[h] += jnp.where(tri, dws, 0.0)
                dvn_ref[rows, ocols] = lax.dot_general(ws, dgate_bf, _TN, preferred_element_type=F32)
        for k in range(2):
            kc = slice(k * w, (k + 1) * w)
            dvn = dvn_ref[:, kc]
            dn_ref[:, kc] += _colsum(dvn * vhat[k])
        dvh = [dvn_ref[:, k * w:(k + 1) * w] * n_ref[:, k * w:(k + 1) * w] for k in range(2)]
        dot = (jnp.sum(dvh[0] * vhat[0], axis=-1, keepdims=True)
               + jnp.sum(dvh[1] * vhat[1], axis=-1, keepdims=True)) / (2 * w)
        for k in range(2):
            dv = rstd_v * (dvh[k] - vhat[k] * dot)
            dp_ref[2 + k] = (dv * _gelu_grad(p_ref[2 + k].astype(F32))).astype(BF16)

    const = lambda shape: pl.BlockSpec(shape, lambda i: (0,) * len(shape))
    tile4 = pl.BlockSpec((4, ts, w), lambda i: (0, i, 0))
    return pl.pallas_call(
        body, name=name, grid=(s // ts,),
        in_specs=[tile4, pl.BlockSpec((ts, d), lambda i: (i, 0)), const((2 * w, d)),
                  pl.BlockSpec((ts, 1), lambda i: (i, 0)),
                  const((1, 2 * w)), const((n_heads, CHUNK, CHUNK)), const((n_heads, CHUNK, 1))]
        + ([ANY] if pin is not None else []),
        out_specs=[tile4, const((1, 2 * w)), const((n_heads, CHUNK, CHUNK)), const((n_heads, CHUNK, 1))],
        out_shape=[jax.ShapeDtypeStruct((4, s, w), BF16), jax.ShapeDtypeStruct((1, 2 * w), F32),
                   jax.ShapeDtypeStruct((n_heads, CHUNK, CHUNK), F32),
                   jax.ShapeDtypeStruct((n_heads, CHUNK, 1), F32)],
        scratch_shapes=[pltpu.VMEM((ts, 2 * w), F32), pltpu.VMEM((ts, 2 * w), F32)],
        compiler_params=_params("arbitrary"))(
            p, dh_bf, wout, rstd, sgu_norm, w_spatial, b_spatial, *([pin] if pin is not None else []))


def _row_tile(rows):
    if rows <= ROW_TILE:
        return rows
    for t in (512, 384, 352, 256, 128, 64, 32, 16, 8):
        if rows % t == 0:
            return t
    return rows


def adamw(w, g, m, v, *, name):
    shape = w.shape
    cols = shape[-1]
    rows = w.size // cols
    w2, g2, m2, v2 = (a.reshape(rows, cols) for a in (w, g, m, v))
    tr = _row_tile(rows)
    bc1 = 1.0 - ADAM_B1 ** ADAM_STEP
    bc2 = 1.0 - ADAM_B2 ** ADAM_STEP

    def body(w_ref, g_ref, m_ref, v_ref, d_ref, nm_ref, nv_ref):
        grad = g_ref[...]
        m_new = ADAM_B1 * m_ref[...] + (1.0 - ADAM_B1) * grad
        v_new = ADAM_B2 * v_ref[...] + (1.0 - ADAM_B2) * (grad * grad)
        nm_ref[...] = m_new
        nv_ref[...] = v_new
        d_ref[...] = -ADAM_LR * ((m_new / bc1) / (jnp.sqrt(v_new / bc2) + ADAM_EPS) + ADAM_WD * w_ref[...])

    spec = pl.BlockSpec((tr, cols), lambda i: (i, 0))
    outs = pl.pallas_call(
        body, name=name, grid=(rows // tr,),
        in_specs=[spec] * 4, out_specs=[spec] * 3,
        out_shape=[jax.ShapeDtypeStruct((rows, cols), F32)] * 3,
        compiler_params=_params("parallel"))(w2, g2, m2, v2)
    return tuple(o.reshape(shape) for o in outs)


def adamw_layers(w, grads, m, v, *, name):
    n_l, rows, cols = w.shape
    tr = _row_tile(rows)
    bc1 = 1.0 - ADAM_B1 ** ADAM_STEP
    bc2 = 1.0 - ADAM_B2 ** ADAM_STEP
    outs = None
    for l in range(n_l):
        def body(w_ref, g_ref, m_ref, v_ref, *rest):
            go_ref, d_ref, nm_ref, nv_ref = rest[-4:]
            grad = g_ref[...]
            m_new = ADAM_B1 * m_ref[...] + (1.0 - ADAM_B1) * grad
            v_new = ADAM_B2 * v_ref[...] + (1.0 - ADAM_B2) * (grad * grad)
            go_ref[...] = grad
            nm_ref[...] = m_new
            nv_ref[...] = v_new
            d_ref[...] = -ADAM_LR * ((m_new / bc1) / (jnp.sqrt(v_new / bc2) + ADAM_EPS) + ADAM_WD * w_ref[...])

        layer = pl.BlockSpec((None, tr, cols), lambda i, l=l: (l, i, 0))
        prev = list(outs) if outs is not None else []
        outs = pl.pallas_call(
            body, name=f"{name}{l}", grid=(rows // tr,),
            in_specs=[layer, pl.BlockSpec((tr, cols), lambda i: (i, 0)), layer, layer] + [ANY] * len(prev),
            out_specs=[layer] * 4,
            out_shape=[jax.ShapeDtypeStruct(w.shape, F32)] * 4,
            input_output_aliases={4 + k: k for k in range(len(prev))},
            compiler_params=_params("parallel"))(w, grads[l], m, v, *prev)
    return tuple(outs)


def _place():
    return lax.axis_index("x"), lax.axis_index("y"), lax.axis_index("c")


def _other_chips(x, y):
    return [(1 - x, y), (x, 1 - y), (1 - x, 1 - y)]


HBM = pl.BlockSpec(memory_space=pltpu.HBM)
SEM = pl.BlockSpec(memory_space=pltpu.SEMAPHORE)
DATAFLOW = pltpu.SideEffectType.DATAFLOW_SIDE_EFFECTING


def _in_hbm(a):
    return pltpu.with_memory_space_constraint(a, pltpu.HBM)


def cast_into_slot(w, chip, *, l=None, name):
    rows, cols = w.shape[-2:]
    tr = _row_tile(rows)

    def body(chip_ref, w_ref, o_ref):
        o_ref[...] = w_ref[...].astype(BF16)

    in_spec = (pl.BlockSpec((tr, cols), lambda i, chip_ref: (i, 0)) if l is None
               else pl.BlockSpec((None, tr, cols), lambda i, chip_ref: (l, i, 0)))
    return pl.pallas_call(
        body, name=name,
        grid_spec=pltpu.PrefetchScalarGridSpec(
            num_scalar_prefetch=1, grid=(rows // tr,), in_specs=[in_spec],
            out_specs=pl.BlockSpec((None, tr, cols), lambda i, chip_ref: (chip_ref[0], i, 0))),
        out_shape=jax.ShapeDtypeStruct((N_CHIPS, rows, cols), BF16),
        compiler_params=_params("parallel"))(chip, w)


def _half(ref, slot, c):
    half = ref.shape[1] // 2
    return ref.at[slot, pl.ds(c * half, half), :]


def gather_start(groups, smalls):
    flat = [b for g in groups for b in g]
    n_b, n_s, n_g = len(flat), len(smalls), len(groups)

    def body(*refs):
        bufs, small_refs = refs[:n_b], refs[n_b:n_b + n_s]
        sems = refs[n_b + n_s:n_b + n_s + 2 * n_g + 2]
        token = refs[-1]
        x, y, c = _place()
        me = 2 * x + y
        chips = _other_chips(x, y)
        for si in range(n_s):
            piece = small_refs[si].at[me]
            for k, (px, py) in enumerate(chips):
                pltpu.make_async_remote_copy(
                    src_ref=piece, dst_ref=piece,
                    send_sem=sems[2 * n_g].at[3 * si + k], recv_sem=sems[2 * n_g + 1].at[3 * si + k],
                    device_id=(px, py, c), device_id_type=MESH).start()
        t = 0
        for gi, group in enumerate(groups):
            for ti in range(len(group)):
                piece = _half(bufs[t], me, c)
                t += 1
                for k, (px, py) in enumerate(chips):
                    pltpu.make_async_remote_copy(
                        src_ref=piece, dst_ref=piece,
                        send_sem=sems[2 * gi].at[3 * ti + k], recv_sem=sems[2 * gi + 1].at[3 * ti + k],
                        device_id=(px, py, c), device_id_type=MESH).start()
        token[...] = jnp.zeros_like(token)

    sem_shapes = []
    for group in groups:
        sem_shapes += [pltpu.SemaphoreType.DMA((3 * len(group),))] * 2
    sem_shapes += [pltpu.SemaphoreType.DMA((3 * n_s,))] * 2
    arrays = flat + list(smalls)
    n_sem = len(sem_shapes)
    res = pl.pallas_call(
        body, name="gather_start",
        out_shape=tuple(sem_shapes) + tuple(pltpu.HBM(a.shape, a.dtype) for a in arrays)
        + (jax.ShapeDtypeStruct((8, 128), F32),),
        in_specs=[HBM] * len(arrays),
        out_specs=tuple([SEM] * n_sem + [HBM] * len(arrays) + [pl.BlockSpec(memory_space=pltpu.VMEM)]),
        input_output_aliases={i: n_sem + i for i in range(len(arrays))},
        compiler_params=pltpu.CompilerParams(has_side_effects=DATAFLOW))(*[_in_hbm(a) for a in arrays])
    sems, thru, token = res[:n_sem], res[n_sem:-1], res[-1]
    out_groups, t = [], 0
    for group in groups:
        out_groups.append(list(thru[t:t + len(group)]))
        t += len(group)
    return sems, out_groups, list(thru[n_b:]), token


def gather_wait(bufs, send, recv, after, *, name, smalls=(), small_send=None, small_recv=None):
    n_b, n_s = len(bufs), len(smalls)
    arrays = list(bufs) + list(smalls)
    sem_ops = [send, recv] + ([small_send, small_recv] if n_s else [])

    def body(*refs):
        buf_refs, small_refs = refs[:n_b], refs[n_b:n_b + n_s]
        sems = refs[n_b + n_s:n_b + n_s + len(sem_ops)]
        x, y, c = _place()
        me = 2 * x + y
        chips = _other_chips(x, y)
        for ti in range(n_b):
            for k, (px, py) in enumerate(chips):
                done = pltpu.make_async_remote_copy(
                    src_ref=_half(buf_refs[ti], me, c), dst_ref=_half(buf_refs[ti], 2 * px + py, c),
                    send_sem=sems[0].at[3 * ti + k], recv_sem=sems[1].at[3 * ti + k],
                    device_id=(px, py, c), device_id_type=MESH)
                done.wait_send()
                done.wait_recv()
        for si in range(n_s):
            for k, (px, py) in enumerate(chips):
                done = pltpu.make_async_remote_copy(
                    src_ref=small_refs[si].at[me], dst_ref=small_refs[si].at[2 * px + py],
                    send_sem=sems[2].at[3 * si + k], recv_sem=sems[3].at[3 * si + k],
                    device_id=(px, py, c), device_id_type=MESH)
                done.wait_send()
                done.wait_recv()

    res = pl.pallas_call(
        body, name=name,
        out_shape=tuple(pltpu.HBM(a.shape, a.dtype) for a in arrays),
        in_specs=[HBM] * len(arrays) + [SEM] * len(sem_ops) + [ANY],
        out_specs=tuple([HBM] * len(arrays)),
        input_output_aliases={i: i for i in range(len(arrays))},
        compiler_params=pltpu.CompilerParams(has_side_effects=DATAFLOW))(*arrays, *sem_ops, after)
    return list(res[:n_b]), list(res[n_b:])


def gather_forward(bufs, *, name):
    n = len(bufs)

    def body(*refs):
        ins, outs = refs[:n], refs[n:2 * n]
        send_sems, recv_sems = refs[2 * n:]
        x, y, c = _place()
        chips = _other_chips(x, y)
        for t in range(n):
            for k, (px, py) in enumerate(chips):
                pltpu.make_async_remote_copy(
                    src_ref=_half(ins[t], 2 * px + py, c), dst_ref=_half(outs[t], 2 * px + py, c),
                    send_sem=send_sems.at[3 * t + k], recv_sem=recv_sems.at[3 * t + k],
                    device_id=(x, y, 1 - c), device_id_type=MESH).start()
        for t in range(n):
            for k, (px, py) in enumerate(chips):
                done = pltpu.make_async_remote_copy(
                    src_ref=_half(ins[t], 2 * px + py, c), dst_ref=_half(outs[t], 2 * px + py, 1 - c),
                    send_sem=send_sems.at[3 * t + k], recv_sem=recv_sems.at[3 * t + k],
                    device_id=(x, y, 1 - c), device_id_type=MESH)
                done.wait_send()
                done.wait_recv()

    return pl.pallas_call(
        body, name=name, in_specs=[ANY] * n, out_specs=[ANY] * n,
        out_shape=[jax.ShapeDtypeStruct(a.shape, a.dtype) for a in bufs],
        input_output_aliases={i: i for i in range(n)},
        scratch_shapes=[pltpu.SemaphoreType.DMA((3 * n,)), pltpu.SemaphoreType.DMA((3 * n,))],
        compiler_params=pltpu.CompilerParams(has_side_effects=True))(*bufs)


def sum_stage_a(grad, recv, place, wire, *, name):
    j_n, half, cols = recv.shape

    def body(place_ref, g_ref, r_ref, o_ref, ob_ref):
        acc = g_ref[...] + r_ref[...]
        o_ref[...] = acc
        ob_ref[...] = acc.astype(wire)

    blk = (None, half, cols)
    return pl.pallas_call(
        body, name=name,
        grid_spec=pltpu.PrefetchScalarGridSpec(
            num_scalar_prefetch=1, grid=(j_n,),
            in_specs=[pl.BlockSpec(blk, lambda j, place_ref: (j, place_ref[1], 0)),
                      pl.BlockSpec(blk, lambda j, place_ref: (j, 0, 0))],
            out_specs=[pl.BlockSpec(blk, lambda j, place_ref: (j, 0, 0))] * 2),
        out_shape=[jax.ShapeDtypeStruct(recv.shape, F32), jax.ShapeDtypeStruct(recv.shape, wire)],
        compiler_params=_params("parallel"))(place, grad, recv)


def _stage_a_copies(srcs, lands, x, y, c):
    out = []
    for src, land in zip(srcs, lands):
        half = src.shape[1] // 2
        out.append((src.at[:, pl.ds((1 - c) * half, half), :], land, (x, y, 1 - c)))
    return out


def _stage_b_copies(srcs, lands, x, y, c):
    out = []
    for src, land in zip(srcs, lands):
        for k, (px, py) in enumerate(_other_chips(x, y)):
            out.append((src.at[2 * px + py], land.at[k], (px, py, c)))
    return out


def split_start(srcs, lands, copies, *, name):
    n = len(srcs)
    n_c = len(copies(srcs, lands, 0, 0, 0))

    def body(*refs):
        src_refs, land_refs = refs[:n], refs[n:2 * n]
        send_sems, recv_sems = refs[2 * n], refs[2 * n + 1]
        token = refs[-1]
        x, y, c = _place()
        for k, (src, dst, target) in enumerate(copies(src_refs, land_refs, x, y, c)):
            pltpu.make_async_remote_copy(src_ref=src, dst_ref=dst, send_sem=send_sems.at[k], recv_sem=recv_sems.at[k],
                                         device_id=target, device_id_type=MESH).start()
        token[...] = jnp.zeros_like(token)

    arrays = list(srcs) + list(lands)
    res = pl.pallas_call(
        body, name=name,
        out_shape=(pltpu.SemaphoreType.DMA((n_c,)), pltpu.SemaphoreType.DMA((n_c,)))
        + tuple(pltpu.HBM(a.shape, a.dtype) for a in arrays) + (jax.ShapeDtypeStruct((8, 128), F32),),
        in_specs=[HBM] * (2 * n),
        out_specs=tuple([SEM, SEM] + [HBM] * (2 * n) + [pl.BlockSpec(memory_space=pltpu.VMEM)]),
        input_output_aliases={i: 2 + i for i in range(2 * n)},
        compiler_params=pltpu.CompilerParams(has_side_effects=DATAFLOW))(*[_in_hbm(a) for a in arrays])
    return res[0], res[1], list(res[2:2 + n]), list(res[2 + n:2 + 2 * n]), res[-1]


def split_wait(srcs, lands, send, recv, copies, after, *, name):
    n = len(srcs)

    def body(*refs):
        src_refs, land_refs = refs[:n], refs[n:2 * n]
        send_sems, recv_sems = refs[2 * n], refs[2 * n + 1]
        x, y, c = _place()
        for k, (src, dst, target) in enumerate(copies(src_refs, land_refs, x, y, c)):
            done = pltpu.make_async_remote_copy(src_ref=src, dst_ref=dst, send_sem=send_sems.at[k],
                                                recv_sem=recv_sems.at[k], device_id=target, device_id_type=MESH)
            done.wait_send()
            done.wait_recv()

    arrays = list(srcs) + list(lands)
    res = pl.pallas_call(
        body, name=name,
        out_shape=tuple(pltpu.HBM(a.shape, a.dtype) for a in arrays),
        in_specs=[HBM] * (2 * n) + [SEM, SEM, ANY],
        out_specs=tuple([HBM] * (2 * n)),
        input_output_aliases={i: i for i in range(2 * n)},
        compiler_params=pltpu.CompilerParams(has_side_effects=DATAFLOW))(*arrays, send, recv, after)
    return list(res[:n]), list(res[n:])


def sum_stage_b(part, recv, place, *, name):
    _, half, cols = part.shape

    def body(place_ref, p_ref, r_ref, o_ref):
        acc = p_ref[...]
        for k in range(3):
            acc = acc + r_ref[k].astype(F32)
        o_ref[...] = acc

    return pl.pallas_call(
        body, name=name,
        grid_spec=pltpu.PrefetchScalarGridSpec(
            num_scalar_prefetch=1, grid=(1,),
            in_specs=[pl.BlockSpec((None, half, cols), lambda i, place_ref: (place_ref[0], 0, 0)),
                      pl.BlockSpec((3, half, cols), lambda i, place_ref: (0, 0, 0))],
            out_specs=pl.BlockSpec((half, cols), lambda i, place_ref: (place_ref[1], 0))),
        out_shape=jax.ShapeDtypeStruct((2 * half, cols), F32),
        compiler_params=_params("arbitrary"))(place, part, recv)


def reduce_stage_c(fulls, *, name):
    n = len(fulls)

    def body(*refs):
        ins, outs = refs[:n], refs[n:2 * n]
        send_sems, recv_sems = refs[2 * n:]
        x, y, c = _place()
        for t in range(n):
            half = ins[t].shape[0] // 2
            pltpu.make_async_remote_copy(
                src_ref=ins[t].at[pl.ds(c * half, half), :], dst_ref=outs[t].at[pl.ds(c * half, half), :],
                send_sem=send_sems.at[t], recv_sem=recv_sems.at[t],
                device_id=(x, y, 1 - c), device_id_type=MESH).start()
        for t in range(n):
            half = ins[t].shape[0] // 2
            done = pltpu.make_async_remote_copy(
                src_ref=ins[t].at[pl.ds(c * half, half), :], dst_ref=outs[t].at[pl.ds((1 - c) * half, half), :],
                send_sem=send_sems.at[t], recv_sem=recv_sems.at[t],
                device_id=(x, y, 1 - c), device_id_type=MESH)
            done.wait_send()
            done.wait_recv()

    return pl.pallas_call(
        body, name=name, in_specs=[ANY] * n, out_specs=[ANY] * n,
        out_shape=[jax.ShapeDtypeStruct(a.shape, a.dtype) for a in fulls],
        input_output_aliases={i: i for i in range(n)},
        scratch_shapes=[pltpu.SemaphoreType.DMA((n,)), pltpu.SemaphoreType.DMA((n,))],
        compiler_params=pltpu.CompilerParams(has_side_effects=True))(*fulls)


def gather_chip_blocks(slots, *, name):
    def body(in_ref, out_ref, send_sems, recv_sems):
        x, y, c = _place()
        me = 2 * x + y
        chips = _other_chips(x, y)
        for k, (px, py) in enumerate(chips):
            pltpu.make_async_remote_copy(
                src_ref=in_ref.at[me], dst_ref=out_ref.at[me],
                send_sem=send_sems.at[k], recv_sem=recv_sems.at[k],
                device_id=(px, py, c), device_id_type=MESH).start()
        for k, (px, py) in enumerate(chips):
            done = pltpu.make_async_remote_copy(
                src_ref=in_ref.at[me], dst_ref=out_ref.at[2 * px + py],
                send_sem=send_sems.at[k], recv_sem=recv_sems.at[k],
                device_id=(px, py, c), device_id_type=MESH)
            done.wait_send()
            done.wait_recv()

    return pl.pallas_call(
        body, name=name, in_specs=[ANY], out_specs=ANY,
        out_shape=jax.ShapeDtypeStruct(slots.shape, slots.dtype),
        input_output_aliases={0: 0},
        scratch_shapes=[pltpu.SemaphoreType.DMA((3,)), pltpu.SemaphoreType.DMA((3,))],
        compiler_params=pltpu.CompilerParams(has_side_effects=True))(slots)


def _ffn_bwd(dh, dh_bf, h, gain, saved, wg, wu, wd, cw, cb, place, l, pin):
    hn, rstd, g, up = saved
    dg, dup, dwg, dwu, dwd, dcw, dcb = ffn_bwd_a(dh_bf, hn, g, up, wd, cw, cb, name=f"ffn{l}_bwd_a", pin=pin)
    red = _reduce_a_begin([(dwg, BF16), (dwu, BF16), (dwd, BF16)], tag=f"f{l}")
    dh_in, dh_in_bf, dgain = dx_rms_bwd([(dg, wg), (dup, wu)], h, gain, rstd, dh, name=f"ffn{l}_bwd_b",
                                        pin=red[-1])
    red = _reduce_b_begin(red, place, dh_in_bf, tag=f"f{l}")
    return dh_in, dh_in_bf, (dcw, dcb, dgain), red


def _reduce_a_begin(grads, *, tag):
    lands = [lax.empty((g.shape[0], g.shape[1] // 2, g.shape[2]), F32) for g, _ in grads]
    send, recv, srcs, lands, token = split_start([g for g, _ in grads], lands, _stage_a_copies,
                                                 name=f"reduce_a_start_{tag}")
    return [w for _, w in grads], send, recv, srcs, lands, token


def _reduce_b_begin(state, place, after, *, tag):
    wires, send, recv, srcs, lands, _ = state
    grads, recv_a = split_wait(srcs, lands, send, recv, _stage_a_copies, after, name=f"reduce_a_wait_{tag}")
    parts = [sum_stage_a(g, r, place, w, name=f"sum_a_{tag}{i}") for i, (g, r, w) in enumerate(zip(grads, recv_a, wires))]
    lands_b = [lax.empty((3,) + p[1].shape[1:], p[1].dtype) for p in parts]
    send, recv, srcs, lands, token = split_start([p[1] for p in parts], lands_b, _stage_b_copies,
                                                 name=f"reduce_b_start_{tag}")
    return [p[0] for p in parts], send, recv, srcs, lands, token


def _reduce_finish(state, place, after, *, tag):
    parts, send, recv, srcs, lands, _ = state
    _, recv_b = split_wait(srcs, lands, send, recv, _stage_b_copies, after, name=f"reduce_b_wait_{tag}")
    halves = [sum_stage_b(p, r, place, name=f"sum_b_{tag}{i}") for i, (p, r) in enumerate(zip(parts, recv_b))]
    return reduce_stage_c(halves, name=f"reduce_c_{tag}")


def kernel(x, norm_mix, norm_ffn, final_norm, w_in_even, conv_a, w_pool, pool_scale, w_out_even, w_in_odd, sgu_norm, w_spatial, b_spatial, w_out_odd, w_ffn_gate, w_ffn_up, conv_ffn, b_conv_ffn, w_ffn_down, loss_target, m_norm_mix, m_norm_ffn, m_final_norm, m_w_in_even, m_conv_a, m_w_pool, m_pool_scale, m_w_out_even, m_w_in_odd, m_sgu_norm, m_w_spatial, m_b_spatial, m_w_out_odd, m_w_ffn_gate, m_w_ffn_up, m_conv_ffn, m_b_conv_ffn, m_w_ffn_down, v_norm_mix, v_norm_ffn, v_final_norm, v_w_in_even, v_conv_a, v_w_pool, v_pool_scale, v_w_out_even, v_w_in_odd, v_sgu_norm, v_w_spatial, v_b_spatial, v_w_out_odd, v_w_ffn_gate, v_w_ffn_up, v_conv_ffn, v_b_conv_ffn, v_w_ffn_down):
    weights = dict(norm_mix=norm_mix, norm_ffn=norm_ffn, final_norm=final_norm, w_in_even=w_in_even,
                   conv_a=conv_a, w_pool=w_pool, pool_scale=pool_scale, w_out_even=w_out_even,
                   w_in_odd=w_in_odd, sgu_norm=sgu_norm, w_spatial=w_spatial, b_spatial=b_spatial,
                   w_out_odd=w_out_odd, w_ffn_gate=w_ffn_gate, w_ffn_up=w_ffn_up, conv_ffn=conv_ffn,
                   b_conv_ffn=b_conv_ffn, w_ffn_down=w_ffn_down)
    m_in = dict(norm_mix=m_norm_mix, norm_ffn=m_norm_ffn, final_norm=m_final_norm, w_in_even=m_w_in_even,
                conv_a=m_conv_a, w_pool=m_w_pool, pool_scale=m_pool_scale, w_out_even=m_w_out_even,
                w_in_odd=m_w_in_odd, sgu_norm=m_sgu_norm, w_spatial=m_w_spatial, b_spatial=m_b_spatial,
                w_out_odd=m_w_out_odd, w_ffn_gate=m_w_ffn_gate, w_ffn_up=m_w_ffn_up, conv_ffn=m_conv_ffn,
                b_conv_ffn=m_b_conv_ffn, w_ffn_down=m_w_ffn_down)
    v_in = dict(norm_mix=v_norm_mix, norm_ffn=v_norm_ffn, final_norm=v_final_norm, w_in_even=v_w_in_even,
                conv_a=v_conv_a, w_pool=v_w_pool, pool_scale=v_pool_scale, w_out_even=v_w_out_even,
                w_in_odd=v_w_in_odd, sgu_norm=v_sgu_norm, w_spatial=v_w_spatial, b_spatial=v_b_spatial,
                w_out_odd=v_w_out_odd, w_ffn_gate=v_w_ffn_gate, w_ffn_up=v_w_ffn_up, conv_ffn=v_conv_ffn,
                b_conv_ffn=v_b_conv_ffn, w_ffn_down=v_w_ffn_down)
    order = list(weights)

    chip = 2 * lax.axis_index("x") + lax.axis_index("y")
    core = lax.axis_index("c")
    place = jnp.stack([chip, core]).astype(jnp.int32)
    chip_arr = place[:1]

    h0 = x[0]
    target = loss_target[0]
    d_model = h0.shape[1]
    f_shard = w_ffn_gate.shape[-1]

    def turned(a):
        return jnp.transpose(a, (0, 2, 1))

    def own_slot(v):
        return lax.dynamic_update_index_in_dim(jnp.zeros((N_CHIPS,) + v.shape, v.dtype), v, chip, 0)

    groups = [
        [cast_into_slot(w_in_even[0], chip_arr, name="cast_win_e"),
         cast_into_slot(w_out_even[0], chip_arr, name="cast_wout_e")],
        [cast_into_slot(turned(w_ffn_gate), chip_arr, l=0, name="cast_wg0"),
         cast_into_slot(turned(w_ffn_up), chip_arr, l=0, name="cast_wu0")],
        [cast_into_slot(w_ffn_down, chip_arr, l=0, name="cast_wd0")],
        [cast_into_slot(w_in_odd[0], chip_arr, name="cast_win_o"),
         cast_into_slot(w_out_odd[0], chip_arr, name="cast_wout_o")],
        [cast_into_slot(turned(w_ffn_gate), chip_arr, l=1, name="cast_wg1"),
         cast_into_slot(turned(w_ffn_up), chip_arr, l=1, name="cast_wu1")],
        [cast_into_slot(w_ffn_down, chip_arr, l=1, name="cast_wd1")]]
    smalls = [own_slot(conv_a[0]), own_slot(sgu_norm), own_slot(conv_ffn[0]), own_slot(conv_ffn[1])]
    sems, groups, smalls, token = gather_start(groups, smalls)

    def arrive(gi, after, with_smalls=False):
        kw = dict(smalls=smalls, small_send=sems[-2], small_recv=sems[-1]) if with_smalls else {}
        bufs, small_out = gather_wait(groups[gi], sems[2 * gi], sems[2 * gi + 1], after, name=f"gather_wait{gi}", **kw)
        return gather_forward(bufs, name=f"gather_forward{gi}"), small_out

    cb = b_conv_ffn.reshape(-1, N_CHIPS, 1, f_shard)
    wp_bf = w_pool[0].astype(BF16)
    wp_t_bf = jnp.transpose(w_pool[0], (0, 2, 1)).astype(BF16)
    ws = w_spatial[0]
    bs = b_spatial[0][:, :, None]

    (win_e, wout_e), (ca_g, sn_g, cw0, cw1) = arrive(0, token, with_smalls=True)
    wout_e = wout_e.reshape(-1, d_model)
    ca_full = jnp.transpose(ca_g, (1, 0, 2)).reshape(ca_g.shape[1], -1)
    sn_full = sn_g.reshape(1, -1)
    h1, xn0, rstd0, proj0, mix0 = even_layer_fwd(h0, norm_mix[0:1], win_e, ca_full, wp_bf, pool_scale, wout_e,
                                                 name="l0_fwd")
    hn0, rstdf0 = rms_fwd(h1, norm_ffn[0:1], name="ffn0_rms")
    (wg0, wu0), _ = arrive(1, hn0)
    g0, up0, act0 = ffn_in_fwd(hn0, wg0, wu0, cw0, cb[0], name="ffn0_in")
    (wd0,), _ = arrive(2, act0)
    h2 = mm_acc(act0, wd0, h1, name="ffn0_down")
    ffn0 = (hn0, rstdf0, g0, up0)
    (win_o, wout_o), _ = arrive(3, h2)
    wout_o = wout_o.reshape(-1, d_model)
    h3, xn1, rstd1, p1, mix1, rstd_v = odd_layer_fwd(h2, norm_mix[1:2], win_o, sn_full, ws, bs, wout_o, name="l1_fwd")
    (wg1, wu1), _ = arrive(4, h3)
    (wd1,), _ = arrive(5, wg1)
    h4, hn1, rstdf1, g1, up1 = ffn_fwd(h3, norm_ffn[1:2], wg1, wu1, wd1, cw1, cb[1], name="ffn1_fwd")
    ffn1 = (hn1, rstdf1, g1, up1)

    dh4, dh4_bf, loss_row, d_final = final_loss(h4, target, final_norm[None], name="loss")
    loss = lax.psum(loss_row[0, 0], ("x", "y", "c"))

    dh3, dh3_bf, (dcw1, dcb1, dnf1), red3 = _ffn_bwd(
        dh4, dh4_bf, h3, norm_ffn[1:2], ffn1, wg1, wu1, wd1, cw1, cb[1], place, 1, None)

    def as_blocks(a):
        return a.reshape(N_CHIPS, -1, d_model)

    dp1, dsn, dws, dbs = sgu_bwd(p1, dh3_bf, wout_o, rstd_v, sn_full, ws, bs, name="l1_mix_bwd", pin=red3[-1])
    dwout_o = as_blocks(mm_tn(mix1[None], dh3_bf[None], name="l1_dwout"))
    dwin_o = mm_tn_shared(xn1, dp1, name="l1_dwin")
    red2 = _reduce_a_begin([(dwin_o, BF16), (dwout_o, BF16)], tag="m1")
    dh2, dh2_bf, dnm1 = dx_rms_bwd([(dp1, win_o)], h2, norm_mix[1:2], rstd1, dh3, name="l1_dx", pin=red2[-1])
    red2 = _reduce_b_begin(red2, place, dh2_bf, tag="m1")

    dh1, dh1_bf, (dcw0, dcb0, dnf0), red1 = _ffn_bwd(
        dh2, dh2_bf, h1, norm_ffn[0:1], ffn0, wg0, wu0, wd0, cw0, cb[0], place, 0, red2[-1])

    dproj0, dca, dwp, dps = even_bwd(proj0, dh1_bf, wout_e, ca_full, wp_bf, wp_t_bf, pool_scale,
                                     name="l0_mix_bwd", pin=red1[-1])
    dwout_e = as_blocks(mm_tn(mix0[None], dh1_bf[None], name="l0_dwout"))
    dwin_e = mm_tn_shared(xn0, dproj0, name="l0_dwin")
    dh0, _, dnm0 = dx_rms_bwd([(dproj0, win_e)], h0, norm_mix[0:1], rstd0, dh1, name="l0_dx")
    grad_x = dh0[None]

    small_parts = dict(
        norm_mix=jnp.concatenate([dnm0, dnm1]), norm_ffn=jnp.concatenate([dnf0, dnf1]), final_norm=d_final,
        conv_a=dca, w_pool=dwp, pool_scale=dps, sgu_norm=dsn, w_spatial=dws, b_spatial=dbs,
        conv_ffn=jnp.stack([dcw0, dcw1]), b_conv_ffn=jnp.stack([dcb0, dcb1]))
    flat = jnp.concatenate([v.reshape(-1) for v in small_parts.values()])
    pad = (-flat.shape[0]) % (N_CHIPS * 32 * 128)
    small = jnp.pad(flat, (0, pad)).reshape(N_CHIPS, -1, 128)
    red0 = _reduce_a_begin([(dwin_e, BF16), (dwout_e, BF16), (small, F32)], tag="m0")

    full3 = _reduce_finish(red3, place, red0[-1], tag="f1")
    red0 = _reduce_b_begin(red0, place, full3[0], tag="m0")
    full2 = _reduce_finish(red2, place, red0[-1], tag="m1")
    full1 = _reduce_finish(red1, place, full2[0], tag="f0")
    full0 = _reduce_finish(red0, place, full1[0], tag="m0")
    small_slots = lax.dynamic_update_index_in_dim(jnp.zeros(small.shape, F32), full0[2], chip, 0)
    small_sum = gather_chip_blocks(small_slots, name="gather_small").reshape(-1)
    grads = {
        "w_in_even": full0[0][None], "w_out_even": full0[1][None],
        "w_in_odd": full2[0][None], "w_out_odd": full2[1][None],
        }
    layered = {"w_ffn_gate": [full1[0], full3[0]], "w_ffn_up": [full1[1], full3[1]],
               "w_ffn_down": [full1[2], full3[2]]}
    off = 0
    small_red = {}
    for nm, v in small_parts.items():
        small_red[nm] = small_sum[off:off + v.size].reshape(v.shape)
        off += v.size
    for nm in ("norm_mix", "norm_ffn", "pool_scale"):
        grads[nm] = small_red[nm].reshape(weights[nm].shape)
    grads["final_norm"] = small_red["final_norm"].reshape(weights["final_norm"].shape)
    grads["w_pool"] = small_red["w_pool"][None]
    grads["w_spatial"] = small_red["w_spatial"][None]
    grads["b_spatial"] = small_red["b_spatial"].reshape(weights["b_spatial"].shape)
    grads["b_conv_ffn"] = small_red["b_conv_ffn"].reshape(weights["b_conv_ffn"].shape)
    grads["conv_a"] = lax.dynamic_slice_in_dim(small_red["conv_a"], chip * conv_a.shape[-1], conv_a.shape[-1], 1)[None]
    grads["sgu_norm"] = lax.dynamic_slice_in_dim(small_red["sgu_norm"], chip * sgu_norm.shape[-1], sgu_norm.shape[-1], 1)
    grads["conv_ffn"] = lax.dynamic_index_in_dim(small_red["conv_ffn"], chip, 1, keepdims=False)

    deltas, new_m, new_v = {}, {}, {}
    for nm, per_layer in layered.items():
        if nm == "w_ffn_down":
            grads[nm], deltas[nm], new_m[nm], new_v[nm] = adamw_layers(
                weights[nm], per_layer, m_in[nm], v_in[nm], name=f"adamw_{nm}")
        else:
            outs = adamw_layers(turned(weights[nm]), per_layer, turned(m_in[nm]), turned(v_in[nm]),
                                name=f"adamw_{nm}")
            grads[nm], deltas[nm], new_m[nm], new_v[nm] = (turned(o) for o in outs)
    for nm in order:
        if nm in layered:
            continue
        w = weights[nm]
        w2 = w[None] if w.ndim == 1 else w
        shp = w2.shape
        d, nm_, nv_ = adamw(w2, grads[nm].reshape(shp), m_in[nm].reshape(shp), v_in[nm].reshape(shp),
                            name=f"adamw_{nm}")
        deltas[nm], new_m[nm], new_v[nm] = d.reshape(w.shape), nm_.reshape(w.shape), nv_.reshape(w.shape)

    return (loss, grad_x, *[grads[n] for n in order], *[deltas[n] for n in order],
            *[new_m[n] for n in order], *[new_v[n] for n in order])
```

```python
import jax
import jax.numpy as jnp
from jax import lax
from jax.experimental import pallas as pl
from jax.experimental.pallas import tpu as pltpu

F32 = jnp.float32
BF16 = jnp.bfloat16
MESH = pl.DeviceIdType.MESH
ANY = pl.BlockSpec(memory_space=pl.ANY)

EPS = 1e-6
POOL_WINDOWS = (2, 4, 8, 16)
GROUP = 128
CHUNK = 128
N_CHIPS = 4
N_DEV = 8
ROW_TILE = 512
HALO = 16
VMEM_LIMIT = 56 * 1024 * 1024

ADAM_LR = 0.001
ADAM_B1 = 0.9
ADAM_B2 = 0.999
ADAM_EPS = 1e-08
ADAM_WD = 0.01
ADAM_STEP = 10


def _params(*sem):
    return pltpu.CompilerParams(dimension_semantics=sem, vmem_limit_bytes=VMEM_LIMIT)


def _layer_spec(block, l, idx):
    if l is None:
        return pl.BlockSpec(block, idx)
    return pl.BlockSpec((None,) + block, lambda *g: (l,) + idx(*g))


def mm_nn(a, b, *, l=None, name):
    s, k = a.shape
    j_n, n = b.shape[-3], b.shape[-1]
    tm = min(ROW_TILE, s)

    def body(a_ref, b_ref, o_ref):
        o_ref[...] = jnp.dot(a_ref[...], b_ref[...], preferred_element_type=F32)

    return pl.pallas_call(
        body, name=name, grid=(j_n, s // tm),
        in_specs=[pl.BlockSpec((tm, k), lambda j, i: (i, 0)),
                  _layer_spec((None, k, n), l, lambda j, i: (j, 0, 0))],
        out_specs=pl.BlockSpec((None, tm, n), lambda j, i: (j, i, 0)),
        out_shape=jax.ShapeDtypeStruct((j_n, s, n), F32),
        compiler_params=_params("parallel", "parallel"))(a, b)


def mm_acc(a, b, res, *, l=None, name):
    j_n, s, kj = a.shape
    n = b.shape[-1]
    tm = min(ROW_TILE, s)

    def body(a_ref, b_ref, r_ref, o_ref):
        acc = r_ref[...]
        for j in range(j_n):
            acc = acc + jnp.dot(a_ref[j], b_ref[j], preferred_element_type=F32)
        o_ref[...] = acc

    return pl.pallas_call(
        body, name=name, grid=(s // tm,),
        in_specs=[pl.BlockSpec((j_n, tm, kj), lambda i: (0, i, 0)),
                  _layer_spec((j_n, kj, n), l, lambda i: (0, 0, 0)),
                  pl.BlockSpec((tm, n), lambda i: (i, 0))],
        out_specs=pl.BlockSpec((tm, n), lambda i: (i, 0)),
        out_shape=jax.ShapeDtypeStruct((s, n), F32),
        compiler_params=_params("parallel"))(a, b, res)


_NT = (((1,), (1,)), ((), ()))
_TN = (((0,), (0,)), ((), ()))


def mm_nt_sum(pairs, *, l=None, name):
    j_n, s, nj = pairs[0][0].shape
    k = pairs[0][1].shape[-2]
    tm = min(ROW_TILE, s)
    n_p = len(pairs)

    def body(*refs):
        o_ref = refs[-1]
        acc = jnp.zeros((tm, k), F32)
        for p in range(n_p):
            dy_ref, w_ref = refs[2 * p], refs[2 * p + 1]
            for j in range(j_n):
                acc = acc + lax.dot_general(dy_ref[j], w_ref[j], _NT, preferred_element_type=F32)
        o_ref[...] = acc

    in_specs, args = [], []
    for dy, w in pairs:
        in_specs += [pl.BlockSpec((j_n, tm, nj), lambda i: (0, i, 0)),
                     _layer_spec((j_n, k, nj), l, lambda i: (0, 0, 0))]
        args += [dy, w]
    return pl.pallas_call(
        body, name=name, grid=(s // tm,), in_specs=in_specs,
        out_specs=pl.BlockSpec((tm, k), lambda i: (i, 0)),
        out_shape=jax.ShapeDtypeStruct((s, k), F32),
        compiler_params=_params("parallel"))(*args)


def mm_nt_each(a, b, *, l=None, name, pin=None):
    s, k = a.shape
    j_n, nj = b.shape[-3], b.shape[-2]
    tm = min(ROW_TILE, s)

    def body(a_ref, b_ref, *rest):
        rest[-1][...] = lax.dot_general(a_ref[...], b_ref[...], _NT, preferred_element_type=F32)

    return pl.pallas_call(
        body, name=name, grid=(j_n, s // tm),
        in_specs=[pl.BlockSpec((tm, k), lambda j, i: (i, 0)),
                  _layer_spec((None, nj, k), l, lambda j, i: (j, 0, 0))] + ([ANY] if pin is not None else []),
        out_specs=pl.BlockSpec((None, tm, nj), lambda j, i: (j, i, 0)),
        out_shape=jax.ShapeDtypeStruct((j_n, s, nj), F32),
        compiler_params=_params("parallel", "parallel"))(a, b, *([pin] if pin is not None else []))


def mm_tn(a, dy, *, name, pin=None):
    ja, s, k = a.shape
    jd, _, n = dy.shape
    j_n = max(ja, jd)
    tk = min(ROW_TILE, s)

    def body(a_ref, d_ref, *rest):
        o_ref = rest[-1]

        @pl.when(pl.program_id(1) == 0)
        def _():
            o_ref[...] = jnp.zeros_like(o_ref)
        o_ref[...] += lax.dot_general(a_ref[...], d_ref[...], _TN, preferred_element_type=F32)

    return pl.pallas_call(
        body, name=name, grid=(j_n, s // tk),
        in_specs=[pl.BlockSpec((None, tk, k), (lambda j, i: (j, i, 0)) if ja > 1 else (lambda j, i: (0, i, 0))),
                  pl.BlockSpec((None, tk, n), (lambda j, i: (j, i, 0)) if jd > 1 else (lambda j, i: (0, i, 0)))]
        + ([ANY] if pin is not None else []),
        out_specs=pl.BlockSpec((None, k, n), lambda j, i: (j, 0, 0)),
        out_shape=jax.ShapeDtypeStruct((j_n, k, n), F32),
        compiler_params=_params("parallel", "arbitrary"))(a, dy, *([pin] if pin is not None else []))


def mm_tn_shared(a, dy, *, name):
    s, k = a.shape
    j_n, _, n = dy.shape
    tk = min(ROW_TILE, s)

    def body(a_ref, d_ref, o_ref):
        @pl.when(pl.program_id(0) == 0)
        def _():
            o_ref[...] = jnp.zeros_like(o_ref)
        a_t = a_ref[...]
        for j in range(j_n):
            o_ref[j] += lax.dot_general(a_t, d_ref[j], _TN, preferred_element_type=F32)

    return pl.pallas_call(
        body, name=name, grid=(s // tk,),
        in_specs=[pl.BlockSpec((tk, k), lambda i: (i, 0)), pl.BlockSpec((j_n, tk, n), lambda i: (0, i, 0))],
        out_specs=pl.BlockSpec((j_n, k, n), lambda i: (0, 0, 0)),
        out_shape=jax.ShapeDtypeStruct((j_n, k, n), F32),
        compiler_params=_params("arbitrary"))(a, dy)


def _back(x, k):
    return pltpu.roll(x, k, 0)


def _fwd(x, k):
    return pltpu.roll(x, x.shape[0] - k, 0)


def _causal_conv(x, w_ref):
    return w_ref[0:1, :] * _back(x, 2) + w_ref[1:2, :] * _back(x, 1) + w_ref[2:3, :] * x


def _causal_conv_t(dy, w_ref):
    return w_ref[2:3, :] * dy + w_ref[1:2, :] * _fwd(dy, 1) + w_ref[0:1, :] * _fwd(dy, 2)


def _gelu(x):
    return 0.5 * x * (1.0 + lax.erf(x * 0.7071067811865476))


def _gelu_grad(x):
    return 0.5 * (1.0 + lax.erf(x * 0.7071067811865476)) + x * jnp.exp(-0.5 * x * x) * 0.3989422804014327


def _colsum(x):
    return jnp.sum(x, axis=0, keepdims=True)


def rms_fwd(h, gain, *, name, pin=None):
    s, d = h.shape
    ts = min(ROW_TILE, s)

    def body(h_ref, g_ref, *rest):
        o_ref, r_ref = rest[-2:]
        x = h_ref[...]
        rstd = lax.rsqrt(jnp.mean(x * x, axis=-1, keepdims=True) + EPS)
        o_ref[...] = (x * rstd * g_ref[...]).astype(BF16)
        r_ref[...] = rstd

    return pl.pallas_call(
        body, name=name, grid=(s // ts,),
        in_specs=[pl.BlockSpec((ts, d), lambda i: (i, 0)), pl.BlockSpec((1, d), lambda i: (0, 0))]
        + ([ANY] if pin is not None else []),
        out_specs=[pl.BlockSpec((ts, d), lambda i: (i, 0)), pl.BlockSpec((ts, 1), lambda i: (i, 0))],
        out_shape=[jax.ShapeDtypeStruct((s, d), BF16), jax.ShapeDtypeStruct((s, 1), F32)],
        compiler_params=_params("parallel"))(h, gain, *([pin] if pin is not None else []))


def rms_bwd(dxn, h, gain, rstd, dres, *, name):
    s, d = h.shape
    ts = min(ROW_TILE, s)

    def body(dx_ref, h_ref, g_ref, r_ref, dr_ref, o_ref, ob_ref, dg_ref):
        @pl.when(pl.program_id(0) == 0)
        def _():
            dg_ref[...] = jnp.zeros_like(dg_ref)
        rstd_v = r_ref[...]
        hhat = h_ref[...] * rstd_v
        dx = dx_ref[...]
        dg_ref[...] += _colsum(dx * hhat)
        dxg = dx * g_ref[...]
        dh = dr_ref[...] + rstd_v * (dxg - hhat * jnp.mean(dxg * hhat, axis=-1, keepdims=True))
        o_ref[...] = dh
        ob_ref[...] = dh.astype(BF16)

    row = pl.BlockSpec((ts, d), lambda i: (i, 0))
    vec = pl.BlockSpec((1, d), lambda i: (0, 0))
    return pl.pallas_call(
        body, name=name, grid=(s // ts,),
        in_specs=[row, row, vec, pl.BlockSpec((ts, 1), lambda i: (i, 0)), row],
        out_specs=[row, row, vec],
        out_shape=[jax.ShapeDtypeStruct((s, d), F32), jax.ShapeDtypeStruct((s, d), BF16),
                   jax.ShapeDtypeStruct((1, d), F32)],
        compiler_params=_params("arbitrary"))(dxn, h, gain, rstd, dres)


def final_loss(h, target, gain, *, name):
    s, d = h.shape
    ts = min(ROW_TILE, s)

    def body(h_ref, t_ref, g_ref, o_ref, ob_ref, l_ref, dg_ref):
        @pl.when(pl.program_id(0) == 0)
        def _():
            l_ref[...] = jnp.zeros_like(l_ref)
            dg_ref[...] = jnp.zeros_like(dg_ref)
        x = h_ref[...]
        rstd = lax.rsqrt(jnp.mean(x * x, axis=-1, keepdims=True) + EPS)
        hhat = x * rstd
        err = hhat * g_ref[...] - t_ref[...]
        l_ref[...] += 0.5 * jnp.sum(jnp.mean(err * err, axis=-1, keepdims=True), axis=0, keepdims=True)
        dy = err * (1.0 / d)
        dg_ref[...] += _colsum(dy * hhat)
        dyg = dy * g_ref[...]
        dh = rstd * (dyg - hhat * jnp.mean(dyg * hhat, axis=-1, keepdims=True))
        o_ref[...] = dh
        ob_ref[...] = dh.astype(BF16)

    row = pl.BlockSpec((ts, d), lambda i: (i, 0))
    vec = pl.BlockSpec((1, d), lambda i: (0, 0))
    return pl.pallas_call(
        body, name=name, grid=(s // ts,),
        in_specs=[row, row, vec],
        out_specs=[row, row, pl.BlockSpec((1, 128), lambda i: (0, 0)), vec],
        out_shape=[jax.ShapeDtypeStruct((s, d), F32), jax.ShapeDtypeStruct((s, d), BF16),
                   jax.ShapeDtypeStruct((1, 128), F32), jax.ShapeDtypeStruct((1, d), F32)],
        compiler_params=_params("arbitrary"))(h, target, gain)


def _halo_specs(n_lead, ts, width, n_tiles):
    hb = ts // HALO
    prev = pl.BlockSpec((n_lead, HALO, width), lambda i: (0, jnp.maximum(i * hb - 1, 0), 0))
    nxt = pl.BlockSpec((n_lead, HALO, width), lambda i: (0, jnp.minimum((i + 1) * hb, n_tiles * hb - 1), 0))
    return prev, nxt


def _pool_fwd(z_ext, g, pos):
    w = POOL_WINDOWS[g]
    zg = z_ext[:, g * GROUP:(g + 1) * GROUP]
    acc = zg
    sh = 1
    while sh < w:
        acc = acc + _back(acc, sh)
        sh *= 2
    return acc[HALO:] / jnp.minimum(pos, float(w)) - zg[HALO:]


def even_layer_fwd(h, gain, win, conv_a, w_pool, pool_scale, wout, *, name, pin=None):
    s, d = h.shape
    w = win.shape[-1]
    ts = min(ROW_TILE, s)
    hb = ts // HALO

    def body(h_ref, hp_ref, gain_ref, win_ref, ca_ref, wp_ref, ps_ref, wout_ref, *rest):
        o_ref, xn_ref, r_ref, p_ref, m_ref = rest[-5:]
        i = pl.program_id(0)
        keep = jnp.where(i > 0, 1.0, 0.0)
        h_ext = jnp.concatenate([hp_ref[...], h_ref[...]], axis=0)
        rstd = lax.rsqrt(jnp.mean(h_ext * h_ext, axis=-1, keepdims=True) + EPS)
        xn_ext = (h_ext * rstd * gain_ref[...]).astype(BF16)
        xn_ref[...] = xn_ext[HALO:]
        r_ref[...] = rstd[HALO:]
        p32 = []
        for k in range(4):
            pk = jnp.dot(xn_ext, win_ref[k], preferred_element_type=F32).astype(BF16)
            p_ref[k] = pk[HALO:]
            pk = pk.astype(F32)
            p32.append(jnp.concatenate([pk[:HALO] * keep, pk[HALO:]], axis=0))
        m_ref[:, 0:w] = (p32[0][HALO:] * _causal_conv(p32[1] * p32[2], ca_ref)[HALO:]).astype(BF16)
        pos = (i * ts + lax.broadcasted_iota(jnp.int32, (ts, 1), 0) + 1).astype(F32)
        for g in range(len(POOL_WINDOWS)):
            pooled = _pool_fwd(p32[3], g, pos)
            mixed = jnp.dot(pooled.astype(BF16), wp_ref[g], preferred_element_type=F32)
            cols = slice(g * GROUP, (g + 1) * GROUP)
            m_ref[:, w + g * GROUP:w + (g + 1) * GROUP] = (mixed * ps_ref[:, cols]).astype(BF16)
        o_ref[...] = h_ref[...] + jnp.dot(m_ref[...], wout_ref[...], preferred_element_type=F32)

    const = lambda shape: pl.BlockSpec(shape, lambda i: (0,) * len(shape))
    row = pl.BlockSpec((ts, d), lambda i: (i, 0))
    return pl.pallas_call(
        body, name=name, grid=(s // ts,),
        in_specs=[row, pl.BlockSpec((HALO, d), lambda i: (jnp.maximum(i * hb - 1, 0), 0)), const((1, d)),
                  const((4, d, w)), const((3, w)), const((4, GROUP, GROUP)), const((1, w)), const((2 * w, d))]
        + ([ANY] if pin is not None else []),
        out_specs=[row, row, pl.BlockSpec((ts, 1), lambda i: (i, 0)),
                   pl.BlockSpec((4, ts, w), lambda i: (0, i, 0)), pl.BlockSpec((ts, 2 * w), lambda i: (i, 0))],
        out_shape=[jax.ShapeDtypeStruct((s, d), F32), jax.ShapeDtypeStruct((s, d), BF16),
                   jax.ShapeDtypeStruct((s, 1), F32), jax.ShapeDtypeStruct((4, s, w), BF16),
                   jax.ShapeDtypeStruct((s, 2 * w), BF16)],
        compiler_params=_params("parallel"))(
            h, h, gain, win, conv_a, w_pool, pool_scale, wout, *([pin] if pin is not None else []))


def even_bwd(proj, dh_bf, wout, conv_a, w_pool, w_pool_t, pool_scale, *, name, pin=None):
    _, s, w = proj.shape
    d = dh_bf.shape[1]
    ts = min(ROW_TILE, s)
    n_t = s // ts
    prev, nxt = _halo_specs(4, ts, w, n_t)
    hb = ts // HALO
    n_ext = ts + HALO

    def body(p_ref, pp_ref, pn_ref, dh_ref, dhn_ref, wout_ref, ca_ref, wp_ref, wpt_ref, ps_ref, *rest):
        dp_ref, dca_ref, dwp_ref, dps_ref = rest[-4:]
        i = pl.program_id(0)

        @pl.when(i == 0)
        def _():
            dca_ref[...] = jnp.zeros_like(dca_ref)
            dwp_ref[...] = jnp.zeros_like(dwp_ref)
            dps_ref[...] = jnp.zeros_like(dps_ref)

        keep_p = jnp.where(i > 0, 1.0, 0.0)
        keep_n = jnp.where(i < n_t - 1, 1.0, 0.0)
        dmix = lax.dot_general(jnp.concatenate([dh_ref[...], dhn_ref[...]], axis=0), wout_ref[...], _NT,
                               preferred_element_type=F32)
        a_b, a_c, a_v = (p_ref[k].astype(F32) for k in range(3))
        cv_ext = jnp.concatenate([pp_ref[1].astype(F32) * pp_ref[2].astype(F32) * keep_p, a_c * a_v], axis=0)
        dy_a = dmix[:ts, 0:w]
        dp_ref[0] = (dy_a * _causal_conv(cv_ext, ca_ref)[HALO:]).astype(BF16)
        dcc = dy_a * a_b
        dca_ref[2:3, :] += _colsum(dcc * cv_ext[HALO:])
        dca_ref[1:2, :] += _colsum(dcc * _back(cv_ext, 1)[HALO:])
        dca_ref[0:1, :] += _colsum(dcc * _back(cv_ext, 2)[HALO:])
        dcc_ext = jnp.concatenate([dcc, dmix[ts:, 0:w] * pn_ref[0].astype(F32) * keep_n], axis=0)
        dcv = _causal_conv_t(dcc_ext, ca_ref)[:ts]
        dp_ref[1] = (dcv * a_v).astype(BF16)
        dp_ref[2] = (dcv * a_c).astype(BF16)
        z_ext = jnp.concatenate([pp_ref[3].astype(F32) * keep_p, p_ref[3].astype(F32)], axis=0)
        pos = (i * ts + lax.broadcasted_iota(jnp.int32, (ts, 1), 0) + 1).astype(F32)
        pos_ext = (i * ts + lax.broadcasted_iota(jnp.int32, (n_ext, 1), 0) + 1).astype(F32)
        for g, win in enumerate(POOL_WINDOWS):
            cols = slice(g * GROUP, (g + 1) * GROUP)
            ycols = slice(w + g * GROUP, w + (g + 1) * GROUP)
            pooled = _pool_fwd(z_ext, g, pos).astype(BF16)
            mixed = jnp.dot(pooled, wp_ref[g], preferred_element_type=F32)
            dy_b = dmix[:ts, ycols]
            dps_ref[:, cols] += _colsum(dy_b * mixed)
            dmixed_ext = jnp.concatenate([dy_b, dmix[ts:, ycols] * keep_n], axis=0) * ps_ref[:, cols]
            dmixed_ext = dmixed_ext.astype(BF16)
            dwp_ref[g] += lax.dot_general(pooled, dmixed_ext[:ts], _TN, preferred_element_type=F32)
            dpooled = jnp.dot(dmixed_ext, wpt_ref[g], preferred_element_type=F32)
            acc = dpooled / jnp.minimum(pos_ext, float(win))
            sh = 1
            while sh < win:
                acc = acc + _fwd(acc, sh)
                sh *= 2
            dp_ref[3, :, cols] = (acc[:ts] - dpooled[:ts]).astype(BF16)

    tile4 = pl.BlockSpec((4, ts, w), lambda i: (0, i, 0))
    const = lambda shape: pl.BlockSpec(shape, lambda i: (0,) * len(shape))
    return pl.pallas_call(
        body, name=name, grid=(n_t,),
        in_specs=[tile4, prev, nxt, pl.BlockSpec((ts, d), lambda i: (i, 0)),
                  pl.BlockSpec((HALO, d), lambda i: (jnp.minimum((i + 1) * hb, n_t * hb - 1), 0)),
                  const((2 * w, d)), const((3, w)), const((4, GROUP, GROUP)), const((4, GROUP, GROUP)), const((1, w))]
        + ([ANY] if pin is not None else []),
        out_specs=[tile4, const((3, w)), const((4, GROUP, GROUP)), const((1, w))],
        out_shape=[jax.ShapeDtypeStruct((4, s, w), BF16), jax.ShapeDtypeStruct((3, w), F32),
                   jax.ShapeDtypeStruct((4, GROUP, GROUP), F32), jax.ShapeDtypeStruct((1, w), F32)],
        compiler_params=_params("arbitrary"))(
            proj, proj, proj, dh_bf, dh_bf, wout, conv_a, w_pool, w_pool_t, pool_scale,
            *([pin] if pin is not None else []))


def _ffn_halo(ts, f, n_t, nxt):
    hb = ts // HALO
    if nxt:
        return pl.BlockSpec((None, HALO, f), lambda j, i: (j, jnp.minimum((i + 1) * hb, n_t * hb - 1), 0))
    return pl.BlockSpec((None, HALO, f), lambda j, i: (j, jnp.maximum(i * hb - 1, 0), 0))


def ffn_act_fwd(g, up, cw, cb, *, name):
    j_n, s, f = g.shape
    ts = min(ROW_TILE, s)
    n_t = s // ts

    def body(g_ref, gp_ref, u_ref, cw_ref, cb_ref, o_ref):
        keep = jnp.where(pl.program_id(1) > 0, 1.0, 0.0)
        g_ext = jnp.concatenate([gp_ref[...] * keep, g_ref[...]], axis=0)
        gc = _causal_conv(g_ext, cw_ref)[HALO:] + cb_ref[...]
        o_ref[...] = (gc * jax.nn.sigmoid(gc) * u_ref[...]).astype(BF16)

    tile = pl.BlockSpec((None, ts, f), lambda j, i: (j, i, 0))
    return pl.pallas_call(
        body, name=name, grid=(j_n, n_t),
        in_specs=[tile, _ffn_halo(ts, f, n_t, False), tile,
                  pl.BlockSpec((None, 3, f), lambda j, i: (j, 0, 0)),
                  pl.BlockSpec((None, 1, f), lambda j, i: (j, 0, 0))],
        out_specs=tile,
        out_shape=jax.ShapeDtypeStruct((j_n, s, f), BF16),
        compiler_params=_params("parallel", "parallel"))(g, g, up, cw, cb)


def ffn_act_bwd(g, up, dact, cw, cb, *, name):
    j_n, s, f = g.shape
    ts = min(ROW_TILE, s)
    n_t = s // ts

    def body(g_ref, gp_ref, gn_ref, u_ref, un_ref, d_ref, dn_ref, cw_ref, cb_ref,
             dg_ref, du_ref, dcw_ref, dcb_ref):
        i = pl.program_id(1)

        @pl.when(i == 0)
        def _():
            dcw_ref[...] = jnp.zeros_like(dcw_ref)
            dcb_ref[...] = jnp.zeros_like(dcb_ref)

        keep_p = jnp.where(i > 0, 1.0, 0.0)
        keep_n = jnp.where(i < n_t - 1, 1.0, 0.0)
        g_ext = jnp.concatenate([gp_ref[...] * keep_p, g_ref[...], gn_ref[...]], axis=0)
        gc = _causal_conv(g_ext, cw_ref)[HALO:] + cb_ref[...]
        sig = jax.nn.sigmoid(gc)
        dact_ext = jnp.concatenate([d_ref[...], dn_ref[...] * keep_n], axis=0)
        du_ref[...] = (dact_ext * gc * sig)[:ts].astype(BF16)
        up_ext = jnp.concatenate([u_ref[...], un_ref[...]], axis=0)
        dgc = dact_ext * up_ext * (sig * (1.0 + gc * (1.0 - sig)))
        dg_ref[...] = _causal_conv_t(dgc, cw_ref)[:ts].astype(BF16)
        dgc_t = dgc[:ts]
        dcb_ref[...] += _colsum(dgc_t)
        dcw_ref[2:3, :] += _colsum(dgc_t * g_ext[HALO:HALO + ts])
        dcw_ref[1:2, :] += _colsum(dgc_t * _back(g_ext, 1)[HALO:HALO + ts])
        dcw_ref[0:1, :] += _colsum(dgc_t * _back(g_ext, 2)[HALO:HALO + ts])

    tile = pl.BlockSpec((None, ts, f), lambda j, i: (j, i, 0))
    prev, nxt = _ffn_halo(ts, f, n_t, False), _ffn_halo(ts, f, n_t, True)
    return pl.pallas_call(
        body, name=name, grid=(j_n, n_t),
        in_specs=[tile, prev, nxt, tile, nxt, tile, nxt,
                  pl.BlockSpec((None, 3, f), lambda j, i: (j, 0, 0)),
                  pl.BlockSpec((None, 1, f), lambda j, i: (j, 0, 0))],
        out_specs=[tile, tile, pl.BlockSpec((None, 3, f), lambda j, i: (j, 0, 0)),
                   pl.BlockSpec((None, 1, f), lambda j, i: (j, 0, 0))],
        out_shape=[jax.ShapeDtypeStruct((j_n, s, f), BF16), jax.ShapeDtypeStruct((j_n, s, f), BF16),
                   jax.ShapeDtypeStruct((j_n, 3, f), F32), jax.ShapeDtypeStruct((j_n, 1, f), F32)],
        compiler_params=_params("parallel", "arbitrary"))(g, g, g, up, up, dact, dact, cw, cb)


def ffn_in_fwd(hn, wg, wu, cw, cb, *, name):
    s, d = hn.shape
    j_n, f, _ = wg.shape
    tm = min(ROW_TILE, s)
    hb = tm // HALO

    def body(x_ref, xp_ref, wg_ref, wu_ref, cw_ref, cb_ref, g_ref, u_ref, gc_ref, a_ref):
        i, j = pl.program_id(0), pl.program_id(1)
        x_ext = jnp.concatenate([xp_ref[...], x_ref[...]], axis=0)
        g_ext = lax.dot_general(x_ext, wg_ref[j], _NT, preferred_element_type=F32).astype(BF16)
        up = lax.dot_general(x_ref[...], wu_ref[j], _NT, preferred_element_type=F32).astype(BF16)
        g_ref[...] = g_ext[HALO:]
        u_ref[...] = up
        a_ref[...] = _ffn_act(g_ext, up, cw_ref, cb_ref, gc_ref, i)

    whole = pl.BlockSpec((j_n, f, d), lambda i, j: (0, 0, 0))
    tile = pl.BlockSpec((None, tm, f), lambda i, j: (j, i, 0))
    shape = jax.ShapeDtypeStruct((j_n, s, f), BF16)
    return pl.pallas_call(
        body, name=name, grid=(s // tm, j_n),
        in_specs=[pl.BlockSpec((tm, d), lambda i, j: (i, 0)),
                  pl.BlockSpec((HALO, d), lambda i, j: (jnp.maximum(i * hb - 1, 0), 0)),
                  whole, whole,
                  pl.BlockSpec((None, 3, f), lambda i, j: (j, 0, 0)),
                  pl.BlockSpec((None, 1, f), lambda i, j: (j, 0, 0))],
        out_specs=[tile] * 4, out_shape=[shape] * 4,
        compiler_params=_params("parallel", "parallel"))(hn, hn, wg, wu, cw, cb)


def _ffn_act(g_ext, up, cw_ref, cb_ref, gc_ref, i):
    keep = jnp.where(i > 0, 1.0, 0.0)
    g32 = jnp.concatenate([g_ext[:HALO].astype(F32) * keep, g_ext[HALO:].astype(F32)], axis=0)
    gc = (_causal_conv(g32, cw_ref)[HALO:] + cb_ref[...]).astype(BF16)
    gc_ref[...] = gc
    gc = gc.astype(F32)
    return (gc * jax.nn.sigmoid(gc) * up.astype(F32)).astype(BF16)


def ffn_fwd(h, gain, wg, wu, wd, cw, cb, *, name):
    s, d = h.shape
    j_n, f, _ = wg.shape
    tm = min(ROW_TILE, s)
    hb = tm // HALO

    def body(h_ref, hp_ref, gain_ref, wg_ref, wu_ref, wd_ref, cw_ref, cb_ref,
             o_ref, xn_ref, r_ref, g_ref, u_ref, gc_ref, x_s):
        i, j = pl.program_id(0), pl.program_id(1)

        @pl.when(j == 0)
        def _():
            h_ext = jnp.concatenate([hp_ref[...], h_ref[...]], axis=0)
            rstd = lax.rsqrt(jnp.mean(h_ext * h_ext, axis=-1, keepdims=True) + EPS)
            x_s[...] = (h_ext * rstd * gain_ref[...]).astype(BF16)
            xn_ref[...] = x_s[HALO:, :]
            r_ref[...] = rstd[HALO:]
            o_ref[...] = h_ref[...]

        g_ext = lax.dot_general(x_s[...], wg_ref[j], _NT, preferred_element_type=F32).astype(BF16)
        up = lax.dot_general(x_s[HALO:, :], wu_ref[j], _NT, preferred_element_type=F32).astype(BF16)
        g_ref[...] = g_ext[HALO:]
        u_ref[...] = up
        act = _ffn_act(g_ext, up, cw_ref, cb_ref, gc_ref, i)
        o_ref[...] += jnp.dot(act, wd_ref[j], preferred_element_type=F32)

    whole = pl.BlockSpec((j_n, f, d), lambda i, j: (0, 0, 0), pipeline_mode=pl.Buffered(1))
    row = pl.BlockSpec((tm, d), lambda i, j: (i, 0))
    tile = pl.BlockSpec((None, tm, f), lambda i, j: (j, i, 0))
    return pl.pallas_call(
        body, name=name, grid=(s // tm, j_n),
        in_specs=[row, pl.BlockSpec((HALO, d), lambda i, j: (jnp.maximum(i * hb - 1, 0), 0)),
                  pl.BlockSpec((1, d), lambda i, j: (0, 0)), whole, whole, whole,
                  pl.BlockSpec((None, 3, f), lambda i, j: (j, 0, 0)),
                  pl.BlockSpec((None, 1, f), lambda i, j: (j, 0, 0))],
        out_specs=[row, row, pl.BlockSpec((tm, 1), lambda i, j: (i, 0)), tile, tile, tile],
        out_shape=[jax.ShapeDtypeStruct((s, d), F32), jax.ShapeDtypeStruct((s, d), BF16),
                   jax.ShapeDtypeStruct((s, 1), F32)] + [jax.ShapeDtypeStruct((j_n, s, f), BF16)] * 3,
        scratch_shapes=[pltpu.VMEM((HALO + tm, d), BF16)],
        compiler_params=_params("parallel", "arbitrary"))(h, h, gain, wg, wu, wd, cw, cb)


def ffn_bwd_a(dh_bf, hn, g, up, gc, wd, cw, *, name, pin=None):
    s, d = hn.shape
    j_n, _, f = g.shape
    tm = min(ROW_TILE, s)
    n_t = s // tm
    hb = tm // HALO

    def body(dh_ref, dhn_ref, x_ref, g_ref, gc_ref, gcn_ref, u_ref, un_ref, wd_ref, cw_ref, *rest):
        dg_ref, du_ref, dwg_ref, dwu_ref, dwd_ref, dcw_ref, dcb_ref = rest[-7:]
        i = pl.program_id(1)

        @pl.when(i == 0)
        def _():
            for r in (dwg_ref, dwu_ref, dwd_ref, dcw_ref, dcb_ref):
                r[...] = jnp.zeros_like(r)

        keep_n = jnp.where(i < n_t - 1, 1.0, 0.0)
        dh = dh_ref[...]
        dact = lax.dot_general(jnp.concatenate([dh, dhn_ref[...]], axis=0), wd_ref[...], _NT,
                               preferred_element_type=F32)
        dact = jnp.concatenate([dact[:tm], dact[tm:] * keep_n], axis=0)
        gc_ext = jnp.concatenate([gc_ref[...], gcn_ref[...]], axis=0).astype(F32)
        sig = jax.nn.sigmoid(gc_ext)
        silu = gc_ext * sig
        up_ext = jnp.concatenate([u_ref[...], un_ref[...]], axis=0).astype(F32)
        act = (silu * up_ext)[:tm].astype(BF16)
        dwd_ref[...] += lax.dot_general(act, dh, _TN, preferred_element_type=F32)
        dup = (dact * silu)[:tm].astype(BF16)
        du_ref[...] = dup
        dgc = dact * up_ext * (sig + silu * (1.0 - sig))
        dgc_1, dgc_2 = _fwd(dgc, 1), _fwd(dgc, 2)
        dg = (cw_ref[2:3, :] * dgc + cw_ref[1:2, :] * dgc_1 + cw_ref[0:1, :] * dgc_2)[:tm].astype(BF16)
        dg_ref[...] = dg
        x = x_ref[...]
        dwg_ref[...] += lax.dot_general(dg, x, _TN, preferred_element_type=F32)
        dwu_ref[...] += lax.dot_general(dup, x, _TN, preferred_element_type=F32)
        g32 = g_ref[...].astype(F32)
        dcb_ref[...] += _colsum(dgc[:tm])
        dcw_ref[2:3, :] += _colsum(dgc[:tm] * g32)
        dcw_ref[1:2, :] += _colsum(dgc_1[:tm] * g32)
        dcw_ref[0:1, :] += _colsum(dgc_2[:tm] * g32)

    rows = pl.BlockSpec((tm, d), lambda j, i: (i, 0))
    rows_next = pl.BlockSpec((HALO, d), lambda j, i: (jnp.minimum((i + 1) * hb, n_t * hb - 1), 0))
    tile = pl.BlockSpec((None, tm, f), lambda j, i: (j, i, 0))
    nxt = _ffn_halo(tm, f, n_t, True)
    per_j = lambda r, c: pl.BlockSpec((None, r, c), lambda j, i: (j, 0, 0))
    return pl.pallas_call(
        body, name=name, grid=(j_n, n_t),
        in_specs=[rows, rows_next, rows, tile, tile, nxt, tile, nxt, per_j(f, d), per_j(3, f)]
        + ([ANY] if pin is not None else []),
        out_specs=[tile, tile, per_j(f, d), per_j(f, d), per_j(f, d), per_j(3, f), per_j(1, f)],
        out_shape=[jax.ShapeDtypeStruct((j_n, s, f), BF16), jax.ShapeDtypeStruct((j_n, s, f), BF16),
                   jax.ShapeDtypeStruct((j_n, f, d), F32), jax.ShapeDtypeStruct((j_n, f, d), F32),
                   jax.ShapeDtypeStruct((j_n, f, d), F32), jax.ShapeDtypeStruct((j_n, 3, f), F32),
                   jax.ShapeDtypeStruct((j_n, 1, f), F32)],
        compiler_params=_params("parallel", "arbitrary"))(
            dh_bf, dh_bf, hn, g, gc, gc, up, up, wd, cw, *([pin] if pin is not None else []))


def dx_rms_bwd(pairs, h, gain, rstd, dres, *, name, pin=None):
    j_n, s, f = pairs[0][0].shape
    d = h.shape[1]
    tm = min(ROW_TILE, s)
    n_p = len(pairs)
    dims = [_NT if w.shape[1:] == (d, f) else (((1,), (0,)), ((), ())) for _, w in pairs]

    def body(*refs):
        dy_refs, w_refs = refs[:n_p], refs[n_p:2 * n_p]
        h_ref, g_ref, r_ref, dr_ref = refs[2 * n_p:2 * n_p + 4]
        o_ref, ob_ref, dgain_ref = refs[-3:]

        @pl.when(pl.program_id(0) == 0)
        def _():
            dgain_ref[...] = jnp.zeros_like(dgain_ref)
        dx = jnp.zeros((tm, d), F32)
        for j in range(j_n):
            for p in range(n_p):
                dx = dx + lax.dot_general(dy_refs[p][j], w_refs[p][j], dims[p], preferred_element_type=F32)
        rstd_v = r_ref[...]
        hhat = h_ref[...] * rstd_v
        dgain_ref[...] += _colsum(dx * hhat)
        dxg = dx * g_ref[...]
        dh = dr_ref[...] + rstd_v * (dxg - hhat * jnp.mean(dxg * hhat, axis=-1, keepdims=True))
        o_ref[...] = dh
        ob_ref[...] = dh.astype(BF16)

    tile4 = pl.BlockSpec((j_n, tm, f), lambda i: (0, i, 0))
    whole = [pl.BlockSpec(w.shape, lambda i: (0, 0, 0), pipeline_mode=pl.Buffered(1)) for _, w in pairs]
    row = pl.BlockSpec((tm, d), lambda i: (i, 0))
    vec = pl.BlockSpec((1, d), lambda i: (0, 0))
    return pl.pallas_call(
        body, name=name, grid=(s // tm,),
        in_specs=[tile4] * n_p + whole + [row, vec, pl.BlockSpec((tm, 1), lambda i: (i, 0)), row]
        + ([ANY] if pin is not None else []),
        out_specs=[row, row, vec],
        out_shape=[jax.ShapeDtypeStruct((s, d), F32), jax.ShapeDtypeStruct((s, d), BF16),
                   jax.ShapeDtypeStruct((1, d), F32)],
        compiler_params=_params("arbitrary"))(
            *[p[0] for p in pairs], *[p[1] for p in pairs], h, gain, rstd, dres, *([pin] if pin is not None else []))


def _sgu_gate(vn_bf, ws_ref, bs_ref, h, rows):
    tri = lax.broadcasted_iota(jnp.int32, (CHUNK, CHUNK), 0) >= lax.broadcasted_iota(jnp.int32, (CHUNK, CHUNK), 1)
    ws = jnp.where(tri, ws_ref[h], 0.0).astype(BF16)
    cols = slice((h % 4) * GROUP, (h % 4 + 1) * GROUP)
    return ws, jnp.dot(ws, vn_bf[h // 4][rows, cols], preferred_element_type=F32) + bs_ref[h]


def odd_layer_fwd(h, gain, win, sgu_norm, w_spatial, b_spatial, wout, *, name):
    s, d = h.shape
    w = win.shape[-1]
    ts = min(ROW_TILE, s)
    n_heads = w_spatial.shape[0]

    def body(h_ref, gain_ref, win_ref, n_ref, ws_ref, bs_ref, wout_ref, o_ref, xn_ref, r_ref, p_ref, m_ref, rv_ref):
        x = h_ref[...]
        rstd_x = lax.rsqrt(jnp.mean(x * x, axis=-1, keepdims=True) + EPS)
        xn = (x * rstd_x * gain_ref[...]).astype(BF16)
        xn_ref[...] = xn
        r_ref[...] = rstd_x
        for k in range(4):
            p_ref[k] = jnp.dot(xn, win_ref[k], preferred_element_type=F32).astype(BF16)
        v = [_gelu(p_ref[2].astype(F32)), _gelu(p_ref[3].astype(F32))]
        ms = (jnp.sum(v[0] * v[0], axis=-1, keepdims=True) + jnp.sum(v[1] * v[1], axis=-1, keepdims=True)) / (2 * w)
        rstd = lax.rsqrt(ms + EPS)
        rv_ref[...] = rstd
        vn = [(v[k] * rstd * n_ref[:, k * w:(k + 1) * w]).astype(BF16) for k in range(2)]
        for hd in range(n_heads):
            cols = slice((hd % 4) * GROUP, (hd % 4 + 1) * GROUP)
            for c in range(ts // CHUNK):
                rows = slice(c * CHUNK, (c + 1) * CHUNK)
                _, gate = _sgu_gate(vn, ws_ref, bs_ref, hd, rows)
                u = _gelu(p_ref[hd // 4, rows, cols].astype(F32))
                m_ref[rows, hd * GROUP:(hd + 1) * GROUP] = (u * gate).astype(BF16)
        o_ref[...] = x + jnp.dot(m_ref[...], wout_ref[...], preferred_element_type=F32)

    const = lambda shape: pl.BlockSpec(shape, lambda i: (0,) * len(shape))
    row = pl.BlockSpec((ts, d), lambda i: (i, 0))
    col1 = pl.BlockSpec((ts, 1), lambda i: (i, 0))
    return pl.pallas_call(
        body, name=name, grid=(s // ts,),
        in_specs=[row, const((1, d)), const((4, d, w)), const((1, 2 * w)),
                  const((n_heads, CHUNK, CHUNK)), const((n_heads, CHUNK, 1)), const((2 * w, d))],
        out_specs=[row, row, col1, pl.BlockSpec((4, ts, w), lambda i: (0, i, 0)),
                   pl.BlockSpec((ts, 2 * w), lambda i: (i, 0)), col1],
        out_shape=[jax.ShapeDtypeStruct((s, d), F32), jax.ShapeDtypeStruct((s, d), BF16),
                   jax.ShapeDtypeStruct((s, 1), F32), jax.ShapeDtypeStruct((4, s, w), BF16),
                   jax.ShapeDtypeStruct((s, 2 * w), BF16), jax.ShapeDtypeStruct((s, 1), F32)],
        compiler_params=_params("parallel"))(h, gain, win, sgu_norm, w_spatial, b_spatial, wout)


def sgu_bwd(p, dh_bf, wout, rstd, sgu_norm, w_spatial, b_spatial, *, name, pin=None):
    _, s, w = p.shape
    d = dh_bf.shape[1]
    ts = min(ROW_TILE, s)
    n_heads = w_spatial.shape[0]

    def body(p_ref, dh_ref, wout_ref, r_ref, n_ref, ws_ref, bs_ref, *rest):
        dp_ref, dn_ref, dws_ref, dbs_ref, dvn_ref, dm_ref = rest[-6:]

        @pl.when(pl.program_id(0) == 0)
        def _():
            dn_ref[...] = jnp.zeros_like(dn_ref)
            dws_ref[...] = jnp.zeros_like(dws_ref)
            dbs_ref[...] = jnp.zeros_like(dbs_ref)

        dm_ref[...] = lax.dot_general(dh_ref[...], wout_ref[...], _NT, preferred_element_type=F32)
        rstd_v = r_ref[...]
        vhat = [_gelu(p_ref[2 + k].astype(F32)) * rstd_v for k in range(2)]
        vn = [(vhat[k] * n_ref[:, k * w:(k + 1) * w]).astype(BF16) for k in range(2)]
        tri = lax.broadcasted_iota(jnp.int32, (CHUNK, CHUNK), 0) >= lax.broadcasted_iota(jnp.int32, (CHUNK, CHUNK), 1)
        for h in range(n_heads):
            cols = slice((h % 4) * GROUP, (h % 4 + 1) * GROUP)
            ocols = slice(h * GROUP, (h + 1) * GROUP)
            for c in range(ts // CHUNK):
                rows = slice(c * CHUNK, (c + 1) * CHUNK)
                ws, gate = _sgu_gate(vn, ws_ref, bs_ref, h, rows)
                pu = p_ref[h // 4, rows, cols].astype(F32)
                dm = dm_ref[rows, ocols]
                dp_ref[h // 4, rows, cols] = (dm * gate * _gelu_grad(pu)).astype(BF16)
                dgate = dm * _gelu(pu)
                dbs_ref[h] += jnp.sum(dgate, axis=-1, keepdims=True)
                dgate_bf = dgate.astype(BF16)
                dws = lax.dot_general(dgate_bf, vn[h // 4][rows, cols], _NT, preferred_element_type=F32)
                dws_ref[h] += jnp.where(tri, dws, 0.0)
                dvn_ref[rows, ocols] = lax.dot_general(ws, dgate_bf, _TN, preferred_element_type=F32)
        for k in range(2):
            kc = slice(k * w, (k + 1) * w)
            dvn = dvn_ref[:, kc]
            dn_ref[:, kc] += _colsum(dvn * vhat[k])
        dvh = [dvn_ref[:, k * w:(k + 1) * w] * n_ref[:, k * w:(k + 1) * w] for k in range(2)]
        dot = (jnp.sum(dvh[0] * vhat[0], axis=-1, keepdims=True)
               + jnp.sum(dvh[1] * vhat[1], axis=-1, keepdims=True)) / (2 * w)
        for k in range(2):
            dv = rstd_v * (dvh[k] - vhat[k] * dot)
            dp_ref[2 + k] = (dv * _gelu_grad(p_ref[2 + k].astype(F32))).astype(BF16)

    const = lambda shape: pl.BlockSpec(shape, lambda i: (0,) * len(shape))
    tile4 = pl.BlockSpec((4, ts, w), lambda i: (0, i, 0))
    return pl.pallas_call(
        body, name=name, grid=(s // ts,),
        in_specs=[tile4, pl.BlockSpec((ts, d), lambda i: (i, 0)), const((2 * w, d)),
                  pl.BlockSpec((ts, 1), lambda i: (i, 0)),
                  const((1, 2 * w)), const((n_heads, CHUNK, CHUNK)), const((n_heads, CHUNK, 1))]
        + ([ANY] if pin is not None else []),
        out_specs=[tile4, const((1, 2 * w)), const((n_heads, CHUNK, CHUNK)), const((n_heads, CHUNK, 1))],
        out_shape=[jax.ShapeDtypeStruct((4, s, w), BF16), jax.ShapeDtypeStruct((1, 2 * w), F32),
                   jax.ShapeDtypeStruct((n_heads, CHUNK, CHUNK), F32),
                   jax.ShapeDtypeStruct((n_heads, CHUNK, 1), F32)],
        scratch_shapes=[pltpu.VMEM((ts, 2 * w), F32), pltpu.VMEM((ts, 2 * w), F32)],
        compiler_params=_params("arbitrary"))(
            p, dh_bf, wout, rstd, sgu_norm, w_spatial, b_spatial, *([pin] if pin is not None else []))


def _row_tile(rows):
    if rows <= ROW_TILE:
        return rows
    for t in (512, 384, 352, 256, 128, 64, 32, 16, 8):
        if rows % t == 0:
            return t
    return rows


def adamw(w, g, m, v, *, name):
    shape = w.shape
    cols = shape[-1]
    rows = w.size // cols
    w2, g2, m2, v2 = (a.reshape(rows, cols) for a in (w, g, m, v))
    tr = _row_tile(rows)
    bc1 = 1.0 - ADAM_B1 ** ADAM_STEP
    bc2 = 1.0 - ADAM_B2 ** ADAM_STEP

    def body(w_ref, g_ref, m_ref, v_ref, d_ref, nm_ref, nv_ref):
        grad = g_ref[...]
        m_new = ADAM_B1 * m_ref[...] + (1.0 - ADAM_B1) * grad
        v_new = ADAM_B2 * v_ref[...] + (1.0 - ADAM_B2) * (grad * grad)
        nm_ref[...] = m_new
        nv_ref[...] = v_new
        d_ref[...] = -ADAM_LR * ((m_new / bc1) / (jnp.sqrt(v_new / bc2) + ADAM_EPS) + ADAM_WD * w_ref[...])

    spec = pl.BlockSpec((tr, cols), lambda i: (i, 0))
    outs = pl.pallas_call(
        body, name=name, grid=(rows // tr,),
        in_specs=[spec] * 4, out_specs=[spec] * 3,
        out_shape=[jax.ShapeDtypeStruct((rows, cols), F32)] * 3,
        compiler_params=_params("parallel"))(w2, g2, m2, v2)
    return tuple(o.reshape(shape) for o in outs)


def adamw_layers(w, grads, m, v, *, name):
    n_l, rows, cols = w.shape
    tr = _row_tile(rows)
    bc1 = 1.0 - ADAM_B1 ** ADAM_STEP
    bc2 = 1.0 - ADAM_B2 ** ADAM_STEP
    outs = None
    for l in range(n_l):
        def body(w_ref, g_ref, m_ref, v_ref, *rest):
            go_ref, d_ref, nm_ref, nv_ref = rest[-4:]
            grad = g_ref[...]
            m_new = ADAM_B1 * m_ref[...] + (1.0 - ADAM_B1) * grad
            v_new = ADAM_B2 * v_ref[...] + (1.0 - ADAM_B2) * (grad * grad)
            go_ref[...] = grad
            nm_ref[...] = m_new
            nv_ref[...] = v_new
            d_ref[...] = -ADAM_LR * ((m_new / bc1) / (jnp.sqrt(v_new / bc2) + ADAM_EPS) + ADAM_WD * w_ref[...])

        layer = pl.BlockSpec((None, tr, cols), lambda i, l=l: (l, i, 0))
        prev = list(outs) if outs is not None else []
        outs = pl.pallas_call(
            body, name=f"{name}{l}", grid=(rows // tr,),
            in_specs=[layer, pl.BlockSpec((tr, cols), lambda i: (i, 0)), layer, layer] + [ANY] * len(prev),
            out_specs=[layer] * 4,
            out_shape=[jax.ShapeDtypeStruct(w.shape, F32)] * 4,
            input_output_aliases={4 + k: k for k in range(len(prev))},
            compiler_params=_params("parallel"))(w, grads[l], m, v, *prev)
    return tuple(outs)


def _place():
    return lax.axis_index("x"), lax.axis_index("y"), lax.axis_index("c")


def _other_chips(x, y):
    return [(1 - x, y), (x, 1 - y), (1 - x, 1 - y)]


HBM = pl.BlockSpec(memory_space=pltpu.HBM)
SEM = pl.BlockSpec(memory_space=pltpu.SEMAPHORE)
DATAFLOW = pltpu.SideEffectType.DATAFLOW_SIDE_EFFECTING


def _in_hbm(a):
    return pltpu.with_memory_space_constraint(a, pltpu.HBM)


def cast_into_slot(w, chip, *, l=None, name):
    rows, cols = w.shape[-2:]
    tr = _row_tile(rows)

    def body(chip_ref, w_ref, o_ref):
        o_ref[...] = w_ref[...].astype(BF16)

    in_spec = (pl.BlockSpec((tr, cols), lambda i, chip_ref: (i, 0)) if l is None
               else pl.BlockSpec((None, tr, cols), lambda i, chip_ref: (l, i, 0)))
    return pl.pallas_call(
        body, name=name,
        grid_spec=pltpu.PrefetchScalarGridSpec(
            num_scalar_prefetch=1, grid=(rows // tr,), in_specs=[in_spec],
            out_specs=pl.BlockSpec((None, tr, cols), lambda i, chip_ref: (chip_ref[0], i, 0))),
        out_shape=jax.ShapeDtypeStruct((N_CHIPS, rows, cols), BF16),
        compiler_params=_params("parallel"))(chip, w)


def _half(ref, slot, c):
    half = ref.shape[1] // 2
    return ref.at[slot, pl.ds(c * half, half), :]


def gather_start(groups, smalls):
    flat = [b for g in groups for b in g]
    n_b, n_s, n_g = len(flat), len(smalls), len(groups)

    def body(*refs):
        bufs, small_refs = refs[:n_b], refs[n_b:n_b + n_s]
        sems = refs[n_b + n_s:n_b + n_s + 2 * n_g + 2]
        token = refs[-1]
        x, y, c = _place()
        me = 2 * x + y
        chips = _other_chips(x, y)
        for si in range(n_s):
            piece = small_refs[si].at[me]
            for k, (px, py) in enumerate(chips):
                pltpu.make_async_remote_copy(
                    src_ref=piece, dst_ref=piece,
                    send_sem=sems[2 * n_g].at[3 * si + k], recv_sem=sems[2 * n_g + 1].at[3 * si + k],
                    device_id=(px, py, c), device_id_type=MESH).start()
        t = 0
        for gi, group in enumerate(groups):
            for ti in range(len(group)):
                piece = _half(bufs[t], me, c)
                t += 1
                for k, (px, py) in enumerate(chips):
                    pltpu.make_async_remote_copy(
                        src_ref=piece, dst_ref=piece,
                        send_sem=sems[2 * gi].at[3 * ti + k], recv_sem=sems[2 * gi + 1].at[3 * ti + k],
                        device_id=(px, py, c), device_id_type=MESH).start()
        token[...] = jnp.zeros_like(token)

    sem_shapes = []
    for group in groups:
        sem_shapes += [pltpu.SemaphoreType.DMA((3 * len(group),))] * 2
    sem_shapes += [pltpu.SemaphoreType.DMA((3 * n_s,))] * 2
    arrays = flat + list(smalls)
    n_sem = len(sem_shapes)
    res = pl.pallas_call(
        body, name="gather_start",
        out_shape=tuple(sem_shapes) + tuple(pltpu.HBM(a.shape, a.dtype) for a in arrays)
        + (jax.ShapeDtypeStruct((8, 128), F32),),
        in_specs=[HBM] * len(arrays),
        out_specs=tuple([SEM] * n_sem + [HBM] * len(arrays) + [pl.BlockSpec(memory_space=pltpu.VMEM)]),
        input_output_aliases={i: n_sem + i for i in range(len(arrays))},
        compiler_params=pltpu.CompilerParams(has_side_effects=DATAFLOW))(*[_in_hbm(a) for a in arrays])
    sems, thru, token = res[:n_sem], res[n_sem:-1], res[-1]
    out_groups, t = [], 0
    for group in groups:
        out_groups.append(list(thru[t:t + len(group)]))
        t += len(group)
    return sems, out_groups, list(thru[n_b:]), token


def gather_wait(bufs, send, recv, after, *, name, smalls=(), small_send=None, small_recv=None):
    n_b, n_s = len(bufs), len(smalls)
    arrays = list(bufs) + list(smalls)
    sem_ops = [send, recv] + ([small_send, small_recv] if n_s else [])

    def body(*refs):
        buf_refs, small_refs = refs[:n_b], refs[n_b:n_b + n_s]
        sems = refs[n_b + n_s:n_b + n_s + len(sem_ops)]
        x, y, c = _place()
        me = 2 * x + y
        chips = _other_chips(x, y)
        for ti in range(n_b):
            for k, (px, py) in enumerate(chips):
                done = pltpu.make_async_remote_copy(
                    src_ref=_half(buf_refs[ti], me, c), dst_ref=_half(buf_refs[ti], 2 * px + py, c),
                    send_sem=sems[0].at[3 * ti + k], recv_sem=sems[1].at[3 * ti + k],
                    device_id=(px, py, c), device_id_type=MESH)
                done.wait_send()
                done.wait_recv()
        for si in range(n_s):
            for k, (px, py) in enumerate(chips):
                done = pltpu.make_async_remote_copy(
                    src_ref=small_refs[si].at[me], dst_ref=small_refs[si].at[2 * px + py],
                    send_sem=sems[2].at[3 * si + k], recv_sem=sems[3].at[3 * si + k],
                    device_id=(px, py, c), device_id_type=MESH)
                done.wait_send()
                done.wait_recv()

    res = pl.pallas_call(
        body, name=name,
        out_shape=tuple(pltpu.HBM(a.shape, a.dtype) for a in arrays),
        in_specs=[HBM] * len(arrays) + [SEM] * len(sem_ops) + [ANY],
        out_specs=tuple([HBM] * len(arrays)),
        input_output_aliases={i: i for i in range(len(arrays))},
        compiler_params=pltpu.CompilerParams(has_side_effects=DATAFLOW))(*arrays, *sem_ops, after)
    return list(res[:n_b]), list(res[n_b:])


def gather_forward(bufs, *, name):
    n = len(bufs)

    def body(*refs):
        ins, outs = refs[:n], refs[n:2 * n]
        send_sems, recv_sems = refs[2 * n:]
        x, y, c = _place()
        chips = _other_chips(x, y)
        for t in range(n):
            for k, (px, py) in enumerate(chips):
                pltpu.make_async_remote_copy(
                    src_ref=_half(ins[t], 2 * px + py, c), dst_ref=_half(outs[t], 2 * px + py, c),
                    send_sem=send_sems.at[3 * t + k], recv_sem=recv_sems.at[3 * t + k],
                    device_id=(x, y, 1 - c), device_id_type=MESH).start()
        for t in range(n):
            for k, (px, py) in enumerate(chips):
                done = pltpu.make_async_remote_copy(
                    src_ref=_half(ins[t], 2 * px + py, c), dst_ref=_half(outs[t], 2 * px + py, 1 - c),
                    send_sem=send_sems.at[3 * t + k], recv_sem=recv_sems.at[3 * t + k],
                    device_id=(x, y, 1 - c), device_id_type=MESH)
                done.wait_send()
                done.wait_recv()

    return pl.pallas_call(
        body, name=name, in_specs=[ANY] * n, out_specs=[ANY] * n,
        out_shape=[jax.ShapeDtypeStruct(a.shape, a.dtype) for a in bufs],
        input_output_aliases={i: i for i in range(n)},
        scratch_shapes=[pltpu.SemaphoreType.DMA((3 * n,)), pltpu.SemaphoreType.DMA((3 * n,))],
        compiler_params=pltpu.CompilerParams(has_side_effects=True))(*bufs)


def sum_stage_a(grad, recv, place, wire, *, name):
    j_n, half, cols = recv.shape

    def body(place_ref, g_ref, r_ref, o_ref, ob_ref):
        acc = g_ref[...] + r_ref[...]
        ob_ref[...] = acc.astype(wire)

        @pl.when(pl.program_id(0) == place_ref[0])
        def _():
            o_ref[...] = acc

    blk = (None, half, cols)
    return pl.pallas_call(
        body, name=name,
        grid_spec=pltpu.PrefetchScalarGridSpec(
            num_scalar_prefetch=1, grid=(j_n,),
            in_specs=[pl.BlockSpec(blk, lambda j, place_ref: (j, place_ref[1], 0)),
                      pl.BlockSpec(blk, lambda j, place_ref: (j, 0, 0))],
            out_specs=[pl.BlockSpec((half, cols), lambda j, place_ref: (0, 0)),
                       pl.BlockSpec(blk, lambda j, place_ref: (j, 0, 0))]),
        out_shape=[jax.ShapeDtypeStruct((half, cols), F32), jax.ShapeDtypeStruct(recv.shape, wire)],
        compiler_params=_params("arbitrary"))(place, grad, recv)


def _stage_a_copies(srcs, lands, x, y, c):
    out = []
    for src, land in zip(srcs, lands):
        half = src.shape[1] // 2
        out.append((src.at[:, pl.ds((1 - c) * half, half), :], land, (x, y, 1 - c)))
    return out


def _stage_b_copies(srcs, lands, x, y, c):
    out = []
    for src, land in zip(srcs, lands):
        for k, (px, py) in enumerate(_other_chips(x, y)):
            out.append((src.at[2 * px + py], land.at[k], (px, py, c)))
    return out


def split_start(srcs, lands, copies, *, name):
    n = len(srcs)
    n_c = len(copies(srcs, lands, 0, 0, 0))

    def body(*refs):
        src_refs, land_refs = refs[:n], refs[n:2 * n]
        send_sems, recv_sems = refs[2 * n], refs[2 * n + 1]
        token = refs[-1]
        x, y, c = _place()
        for k, (src, dst, target) in enumerate(copies(src_refs, land_refs, x, y, c)):
            pltpu.make_async_remote_copy(src_ref=src, dst_ref=dst, send_sem=send_sems.at[k], recv_sem=recv_sems.at[k],
                                         device_id=target, device_id_type=MESH).start()
        token[...] = jnp.zeros_like(token)

    arrays = list(srcs) + list(lands)
    res = pl.pallas_call(
        body, name=name,
        out_shape=(pltpu.SemaphoreType.DMA((n_c,)), pltpu.SemaphoreType.DMA((n_c,)))
        + tuple(pltpu.HBM(a.shape, a.dtype) for a in arrays) + (jax.ShapeDtypeStruct((8, 128), F32),),
        in_specs=[HBM] * (2 * n),
        out_specs=tuple([SEM, SEM] + [HBM] * (2 * n) + [pl.BlockSpec(memory_space=pltpu.VMEM)]),
        input_output_aliases={i: 2 + i for i in range(2 * n)},
        compiler_params=pltpu.CompilerParams(has_side_effects=DATAFLOW))(*[_in_hbm(a) for a in arrays])
    return res[0], res[1], list(res[2:2 + n]), list(res[2 + n:2 + 2 * n]), res[-1]


def split_wait(srcs, lands, send, recv, copies, after, *, name):
    n = len(srcs)

    def body(*refs):
        src_refs, land_refs = refs[:n], refs[n:2 * n]
        send_sems, recv_sems = refs[2 * n], refs[2 * n + 1]
        x, y, c = _place()
        for k, (src, dst, target) in enumerate(copies(src_refs, land_refs, x, y, c)):
            done = pltpu.make_async_remote_copy(src_ref=src, dst_ref=dst, send_sem=send_sems.at[k],
                                                recv_sem=recv_sems.at[k], device_id=target, device_id_type=MESH)
            done.wait_send()
            done.wait_recv()

    arrays = list(srcs) + list(lands)
    res = pl.pallas_call(
        body, name=name,
        out_shape=tuple(pltpu.HBM(a.shape, a.dtype) for a in arrays),
        in_specs=[HBM] * (2 * n) + [SEM, SEM, ANY],
        out_specs=tuple([HBM] * (2 * n)),
        input_output_aliases={i: i for i in range(2 * n)},
        compiler_params=pltpu.CompilerParams(has_side_effects=DATAFLOW))(*arrays, send, recv, after)
    return list(res[:n]), list(res[n:])


def sum_stage_b(part, recv, place, *, name):
    half, cols = part.shape

    def body(place_ref, p_ref, r_ref, o_ref):
        acc = p_ref[...]
        for k in range(3):
            acc = acc + r_ref[k].astype(F32)
        o_ref[...] = acc

    return pl.pallas_call(
        body, name=name,
        grid_spec=pltpu.PrefetchScalarGridSpec(
            num_scalar_prefetch=1, grid=(1,),
            in_specs=[pl.BlockSpec((half, cols), lambda i, place_ref: (0, 0)),
                      pl.BlockSpec((3, half, cols), lambda i, place_ref: (0, 0, 0))],
            out_specs=pl.BlockSpec((half, cols), lambda i, place_ref: (place_ref[1], 0))),
        out_shape=jax.ShapeDtypeStruct((2 * half, cols), F32),
        compiler_params=_params("arbitrary"))(place, part, recv)


def reduce_stage_c(fulls, *, name):
    n = len(fulls)

    def body(*refs):
        ins, outs = refs[:n], refs[n:2 * n]
        send_sems, recv_sems = refs[2 * n:]
        x, y, c = _place()
        for t in range(n):
            half = ins[t].shape[0] // 2
            pltpu.make_async_remote_copy(
                src_ref=ins[t].at[pl.ds(c * half, half), :], dst_ref=outs[t].at[pl.ds(c * half, half), :],
                send_sem=send_sems.at[t], recv_sem=recv_sems.at[t],
                device_id=(x, y, 1 - c), device_id_type=MESH).start()
        for t in range(n):
            half = ins[t].shape[0] // 2
            done = pltpu.make_async_remote_copy(
                src_ref=ins[t].at[pl.ds(c * half, half), :], dst_ref=outs[t].at[pl.ds((1 - c) * half, half), :],
                send_sem=send_sems.at[t], recv_sem=recv_sems.at[t],
                device_id=(x, y, 1 - c), device_id_type=MESH)
            done.wait_send()
            done.wait_recv()

    return pl.pallas_call(
        body, name=name, in_specs=[ANY] * n, out_specs=[ANY] * n,
        out_shape=[jax.ShapeDtypeStruct(a.shape, a.dtype) for a in fulls],
        input_output_aliases={i: i for i in range(n)},
        scratch_shapes=[pltpu.SemaphoreType.DMA((n,)), pltpu.SemaphoreType.DMA((n,))],
        compiler_params=pltpu.CompilerParams(has_side_effects=True))(*fulls)


def gather_chip_blocks(slots, *, name):
    def body(in_ref, out_ref, send_sems, recv_sems):
        x, y, c = _place()
        me = 2 * x + y
        chips = _other_chips(x, y)
        for k, (px, py) in enumerate(chips):
            pltpu.make_async_remote_copy(
                src_ref=in_ref.at[me], dst_ref=out_ref.at[me],
                send_sem=send_sems.at[k], recv_sem=recv_sems.at[k],
                device_id=(px, py, c), device_id_type=MESH).start()
        for k, (px, py) in enumerate(chips):
            done = pltpu.make_async_remote_copy(
                src_ref=in_ref.at[me], dst_ref=out_ref.at[2 * px + py],
                send_sem=send_sems.at[k], recv_sem=recv_sems.at[k],
                device_id=(px, py, c), device_id_type=MESH)
            done.wait_send()
            done.wait_recv()

    return pl.pallas_call(
        body, name=name, in_specs=[ANY], out_specs=ANY,
        out_shape=jax.ShapeDtypeStruct(slots.shape, slots.dtype),
        input_output_aliases={0: 0},
        scratch_shapes=[pltpu.SemaphoreType.DMA((3,)), pltpu.SemaphoreType.DMA((3,))],
        compiler_params=pltpu.CompilerParams(has_side_effects=True))(slots)


def _ffn_bwd(dh, dh_bf, h, gain, saved, wg, wu, wd, cw, cb, place, l, pin):
    hn, rstd, g, up, gc = saved
    dg, dup, dwg, dwu, dwd, dcw, dcb = ffn_bwd_a(dh_bf, hn, g, up, gc, wd, cw, name=f"ffn{l}_bwd_a", pin=pin)
    red = _reduce_a_begin([(dwg, BF16), (dwu, BF16), (dwd, BF16)], tag=f"f{l}")
    dh_in, dh_in_bf, dgain = dx_rms_bwd([(dg, wg), (dup, wu)], h, gain, rstd, dh, name=f"ffn{l}_bwd_b",
                                        pin=red[-1])
    red = _reduce_b_begin(red, place, dh_in_bf, tag=f"f{l}")
    return dh_in, dh_in_bf, (dcw, dcb, dgain), red


def _reduce_a_begin(grads, *, tag):
    lands = [lax.empty((g.shape[0], g.shape[1] // 2, g.shape[2]), F32) for g, _ in grads]
    send, recv, srcs, lands, token = split_start([g for g, _ in grads], lands, _stage_a_copies,
                                                 name=f"reduce_a_start_{tag}")
    return [w for _, w in grads], send, recv, srcs, lands, token


def _reduce_b_begin(state, place, after, *, tag):
    wires, send, recv, srcs, lands, _ = state
    grads, recv_a = split_wait(srcs, lands, send, recv, _stage_a_copies, after, name=f"reduce_a_wait_{tag}")
    parts = [sum_stage_a(g, r, place, w, name=f"sum_a_{tag}{i}") for i, (g, r, w) in enumerate(zip(grads, recv_a, wires))]
    lands_b = [lax.empty((3,) + p[1].shape[1:], p[1].dtype) for p in parts]
    send, recv, srcs, lands, token = split_start([p[1] for p in parts], lands_b, _stage_b_copies,
                                                 name=f"reduce_b_start_{tag}")
    return [p[0] for p in parts], send, recv, srcs, lands, token


def _reduce_finish(state, place, after, *, tag):
    parts, send, recv, srcs, lands, _ = state
    _, recv_b = split_wait(srcs, lands, send, recv, _stage_b_copies, after, name=f"reduce_b_wait_{tag}")
    halves = [sum_stage_b(p, r, place, name=f"sum_b_{tag}{i}") for i, (p, r) in enumerate(zip(parts, recv_b))]
    return reduce_stage_c(halves, name=f"reduce_c_{tag}")


def kernel(x, norm_mix, norm_ffn, final_norm, w_in_even, conv_a, w_pool, pool_scale, w_out_even, w_in_odd, sgu_norm, w_spatial, b_spatial, w_out_odd, w_ffn_gate, w_ffn_up, conv_ffn, b_conv_ffn, w_ffn_down, loss_target, m_norm_mix, m_norm_ffn, m_final_norm, m_w_in_even, m_conv_a, m_w_pool, m_pool_scale, m_w_out_even, m_w_in_odd, m_sgu_norm, m_w_spatial, m_b_spatial, m_w_out_odd, m_w_ffn_gate, m_w_ffn_up, m_conv_ffn, m_b_conv_ffn, m_w_ffn_down, v_norm_mix, v_norm_ffn, v_final_norm, v_w_in_even, v_conv_a, v_w_pool, v_pool_scale, v_w_out_even, v_w_in_odd, v_sgu_norm, v_w_spatial, v_b_spatial, v_w_out_odd, v_w_ffn_gate, v_w_ffn_up, v_conv_ffn, v_b_conv_ffn, v_w_ffn_down):
    weights = dict(norm_mix=norm_mix, norm_ffn=norm_ffn, final_norm=final_norm, w_in_even=w_in_even,
                   conv_a=conv_a, w_pool=w_pool, pool_scale=pool_scale, w_out_even=w_out_even,
                   w_in_odd=w_in_odd, sgu_norm=sgu_norm, w_spatial=w_spatial, b_spatial=b_spatial,
                   w_out_odd=w_out_odd, w_ffn_gate=w_ffn_gate, w_ffn_up=w_ffn_up, conv_ffn=conv_ffn,
                   b_conv_ffn=b_conv_ffn, w_ffn_down=w_ffn_down)
    m_in = dict(norm_mix=m_norm_mix, norm_ffn=m_norm_ffn, final_norm=m_final_norm, w_in_even=m_w_in_even,
                conv_a=m_conv_a, w_pool=m_w_pool, pool_scale=m_pool_scale, w_out_even=m_w_out_even,
                w_in_odd=m_w_in_odd, sgu_norm=m_sgu_norm, w_spatial=m_w_spatial, b_spatial=m_b_spatial,
                w_out_odd=m_w_out_odd, w_ffn_gate=m_w_ffn_gate, w_ffn_up=m_w_ffn_up, conv_ffn=m_conv_ffn,
                b_conv_ffn=m_b_conv_ffn, w_ffn_down=m_w_ffn_down)
    v_in = dict(norm_mix=v_norm_mix, norm_ffn=v_norm_ffn, final_norm=v_final_norm, w_in_even=v_w_in_even,
                conv_a=v_conv_a, w_pool=v_w_pool, pool_scale=v_pool_scale, w_out_even=v_w_out_even,
                w_in_odd=v_w_in_odd, sgu_norm=v_sgu_norm, w_spatial=v_w_spatial, b_spatial=v_b_spatial,
                w_out_odd=v_w_out_odd, w_ffn_gate=v_w_ffn_gate, w_ffn_up=v_w_ffn_up, conv_ffn=v_conv_ffn,
                b_conv_ffn=v_b_conv_ffn, w_ffn_down=v_w_ffn_down)
    order = list(weights)

    chip = 2 * lax.axis_index("x") + lax.axis_index("y")
    core = lax.axis_index("c")
    place = jnp.stack([chip, core]).astype(jnp.int32)
    chip_arr = place[:1]

    h0 = x[0]
    target = loss_target[0]
    d_model = h0.shape[1]
    f_shard = w_ffn_gate.shape[-1]

    def turned(a):
        return jnp.transpose(a, (0, 2, 1))

    def own_slot(v):
        return lax.dynamic_update_index_in_dim(jnp.zeros((N_CHIPS,) + v.shape, v.dtype), v, chip, 0)

    groups = [
        [cast_into_slot(w_in_even[0], chip_arr, name="cast_win_e"),
         cast_into_slot(w_out_even[0], chip_arr, name="cast_wout_e")],
        [cast_into_slot(turned(w_ffn_gate), chip_arr, l=0, name="cast_wg0"),
         cast_into_slot(turned(w_ffn_up), chip_arr, l=0, name="cast_wu0")],
        [cast_into_slot(w_ffn_down, chip_arr, l=0, name="cast_wd0")],
        [cast_into_slot(w_in_odd[0], chip_arr, name="cast_win_o"),
         cast_into_slot(w_out_odd[0], chip_arr, name="cast_wout_o")],
        [cast_into_slot(turned(w_ffn_gate), chip_arr, l=1, name="cast_wg1"),
         cast_into_slot(turned(w_ffn_up), chip_arr, l=1, name="cast_wu1")],
        [cast_into_slot(w_ffn_down, chip_arr, l=1, name="cast_wd1")]]
    smalls = [own_slot(conv_a[0]), own_slot(sgu_norm), own_slot(conv_ffn[0]), own_slot(conv_ffn[1])]
    sems, groups, smalls, token = gather_start(groups, smalls)

    def arrive(gi, after, with_smalls=False):
        kw = dict(smalls=smalls, small_send=sems[-2], small_recv=sems[-1]) if with_smalls else {}
        bufs, small_out = gather_wait(groups[gi], sems[2 * gi], sems[2 * gi + 1], after, name=f"gather_wait{gi}", **kw)
        return gather_forward(bufs, name=f"gather_forward{gi}"), small_out

    cb = b_conv_ffn.reshape(-1, N_CHIPS, 1, f_shard)
    wp_bf = w_pool[0].astype(BF16)
    wp_t_bf = jnp.transpose(w_pool[0], (0, 2, 1)).astype(BF16)
    ws = w_spatial[0]
    bs = b_spatial[0][:, :, None]

    (win_e, wout_e), (ca_g, sn_g, cw0, cw1) = arrive(0, token, with_smalls=True)
    wout_e = wout_e.reshape(-1, d_model)
    ca_full = jnp.transpose(ca_g, (1, 0, 2)).reshape(ca_g.shape[1], -1)
    sn_full = sn_g.reshape(1, -1)
    h1, xn0, rstd0, proj0, mix0 = even_layer_fwd(h0, norm_mix[0:1], win_e, ca_full, wp_bf, pool_scale, wout_e,
                                                 name="l0_fwd")
    hn0, rstdf0 = rms_fwd(h1, norm_ffn[0:1], name="ffn0_rms")
    (wg0, wu0), _ = arrive(1, hn0)
    g0, up0, gc0, act0 = ffn_in_fwd(hn0, wg0, wu0, cw0, cb[0], name="ffn0_in")
    (wd0,), _ = arrive(2, act0)
    h2 = mm_acc(act0, wd0, h1, name="ffn0_down")
    ffn0 = (hn0, rstdf0, g0, up0, gc0)
    (win_o, wout_o), _ = arrive(3, h2)
    wout_o = wout_o.reshape(-1, d_model)
    h3, xn1, rstd1, p1, mix1, rstd_v = odd_layer_fwd(h2, norm_mix[1:2], win_o, sn_full, ws, bs, wout_o, name="l1_fwd")
    (wg1, wu1), _ = arrive(4, h3)
    (wd1,), _ = arrive(5, wg1)
    h4, hn1, rstdf1, g1, up1, gc1 = ffn_fwd(h3, norm_ffn[1:2], wg1, wu1, wd1, cw1, cb[1], name="ffn1_fwd")
    ffn1 = (hn1, rstdf1, g1, up1, gc1)

    dh4, dh4_bf, loss_row, d_final = final_loss(h4, target, final_norm[None], name="loss")
    loss = lax.psum(loss_row[0, 0], ("x", "y", "c"))

    dh3, dh3_bf, (dcw1, dcb1, dnf1), red3 = _ffn_bwd(
        dh4, dh4_bf, h3, norm_ffn[1:2], ffn1, wg1, wu1, wd1, cw1, cb[1], place, 1, None)

    def as_blocks(a):
        return a.reshape(N_CHIPS, -1, d_model)

    dp1, dsn, dws, dbs = sgu_bwd(p1, dh3_bf, wout_o, rstd_v, sn_full, ws, bs, name="l1_mix_bwd", pin=red3[-1])
    dwout_o = as_blocks(mm_tn(mix1[None], dh3_bf[None], name="l1_dwout"))
    dwin_o = mm_tn_shared(xn1, dp1, name="l1_dwin")
    red2 = _reduce_a_begin([(dwin_o, BF16), (dwout_o, BF16)], tag="m1")
    dh2, dh2_bf, dnm1 = dx_rms_bwd([(dp1, win_o)], h2, norm_mix[1:2], rstd1, dh3, name="l1_dx", pin=red2[-1])
    red2 = _reduce_b_begin(red2, place, dh2_bf, tag="m1")

    dh1, dh1_bf, (dcw0, dcb0, dnf0), red1 = _ffn_bwd(
        dh2, dh2_bf, h1, norm_ffn[0:1], ffn0, wg0, wu0, wd0, cw0, cb[0], place, 0, red2[-1])

    dproj0, dca, dwp, dps = even_bwd(proj0, dh1_bf, wout_e, ca_full, wp_bf, wp_t_bf, pool_scale,
                                     name="l0_mix_bwd", pin=red1[-1])
    dwout_e = as_blocks(mm_tn(mix0[None], dh1_bf[None], name="l0_dwout"))
    dwin_e = mm_tn_shared(xn0, dproj0, name="l0_dwin")
    dh0, _, dnm0 = dx_rms_bwd([(dproj0, win_e)], h0, norm_mix[0:1], rstd0, dh1, name="l0_dx")
    grad_x = dh0[None]

    small_parts = dict(
        norm_mix=jnp.concatenate([dnm0, dnm1]), norm_ffn=jnp.concatenate([dnf0, dnf1]), final_norm=d_final,
        conv_a=dca, w_pool=dwp, pool_scale=dps, sgu_norm=dsn, w_spatial=dws, b_spatial=dbs,
        conv_ffn=jnp.stack([dcw0, dcw1]), b_conv_ffn=jnp.stack([dcb0, dcb1]))
    flat = jnp.concatenate([v.reshape(-1) for v in small_parts.values()])
    pad = (-flat.shape[0]) % (N_CHIPS * 32 * 128)
    small = jnp.pad(flat, (0, pad)).reshape(N_CHIPS, -1, 128)
    red0 = _reduce_a_begin([(dwin_e, BF16), (dwout_e, BF16), (small, F32)], tag="m0")

    full3 = _reduce_finish(red3, place, red0[-1], tag="f1")
    red0 = _reduce_b_begin(red0, place, full3[0], tag="m0")
    full2 = _reduce_finish(red2, place, red0[-1], tag="m1")
    full1 = _reduce_finish(red1, place, full2[0], tag="f0")
    full0 = _reduce_finish(red0, place, full1[0], tag="m0")
    small_slots = lax.dynamic_update_index_in_dim(jnp.zeros(small.shape, F32), full0[2], chip, 0)
    small_sum = gather_chip_blocks(small_slots, name="gather_small").reshape(-1)
    grads = {
        "w_in_even": full0[0][None], "w_out_even": full0[1][None],
        "w_in_odd": full2[0][None], "w_out_odd": full2[1][None],
        }
    layered = {"w_ffn_gate": [full1[0], full3[0]], "w_ffn_up": [full1[1], full3[1]],
               "w_ffn_down": [full1[2], full3[2]]}
    off = 0
    small_red = {}
    for nm, v in small_parts.items():
        small_red[nm] = small_sum[off:off + v.size].reshape(v.shape)
        off += v.size
    for nm in ("norm_mix", "norm_ffn", "pool_scale"):
        grads[nm] = small_red[nm].reshape(weights[nm].shape)
    grads["final_norm"] = small_red["final_norm"].reshape(weights["final_norm"].shape)
    grads["w_pool"] = small_red["w_pool"][None]
    grads["w_spatial"] = small_red["w_spatial"][None]
    grads["b_spatial"] = small_red["b_spatial"].reshape(weights["b_spatial"].shape)
    grads["b_conv_ffn"] = small_red["b_conv_ffn"].reshape(weights["b_conv_ffn"].shape)
    grads["conv_a"] = lax.dynamic_slice_in_dim(small_red["conv_a"], chip * conv_a.shape[-1], conv_a.shape[-1], 1)[None]
    grads["sgu_norm"] = lax.dynamic_slice_in_dim(small_red["sgu_norm"], chip * sgu_norm.shape[-1], sgu_norm.shape[-1], 1)
    grads["conv_ffn"] = lax.dynamic_index_in_dim(small_red["conv_ffn"], chip, 1, keepdims=False)

    deltas, new_m, new_v = {}, {}, {}
    for nm, per_layer in layered.items():
        if nm == "w_ffn_down":
            grads[nm], deltas[nm], new_m[nm], new_v[nm] = adamw_layers(
                weights[nm], per_layer, m_in[nm], v_in[nm], name=f"adamw_{nm}")
        else:
            outs = adamw_layers(turned(weights[nm]), per_layer, turned(m_in[nm]), turned(v_in[nm]),
                                name=f"adamw_{nm}")
            grads[nm], deltas[nm], new_m[nm], new_v[nm] = (turned(o) for o in outs)
    for nm in order:
        if nm in layered:
            continue
        w = weights[nm]
        w2 = w[None] if w.ndim == 1 else w
        shp = w2.shape
        d, nm_, nv_ = adamw(w2, grads[nm].reshape(shp), m_in[nm].reshape(shp), v_in[nm].reshape(shp),
                            name=f"adamw_{nm}")
        deltas[nm], new_m[nm], new_v[nm] = d.reshape(w.shape), nm_.reshape(w.shape), nv_.reshape(w.shape)

    return (loss, grad_x, *[grads[n] for n in order], *[deltas[n] for n in order],
            *[new_m[n] for n in order], *[new_v[n] for n in order])
```

```python
import jax
import jax.numpy as jnp
from jax import lax
from jax.experimental import pallas as pl
from jax.experimental.pallas import tpu as pltpu

F32 = jnp.float32
BF16 = jnp.bfloat16
MESH = pl.DeviceIdType.MESH
ANY = pl.BlockSpec(memory_space=pl.ANY)

EPS = 1e-6
POOL_WINDOWS = (2, 4, 8, 16)
GROUP = 128
CHUNK = 128
N_CHIPS = 4
N_DEV = 8
ROW_TILE = 512
HALO = 16
VMEM_LIMIT = 56 * 1024 * 1024

ADAM_LR = 0.001
ADAM_B1 = 0.9
ADAM_B2 = 0.999
ADAM_EPS = 1e-08
ADAM_WD = 0.01
ADAM_STEP = 10


def _params(*sem):
    return pltpu.CompilerParams(dimension_semantics=sem, vmem_limit_bytes=VMEM_LIMIT)


def _layer_spec(block, l, idx):
    if l is None:
        return pl.BlockSpec(block, idx)
    return pl.BlockSpec((None,) + block, lambda *g: (l,) + idx(*g))


def mm_nn(a, b, *, l=None, name):
    s, k = a.shape
    j_n, n = b.shape[-3], b.shape[-1]
    tm = min(ROW_TILE, s)

    def body(a_ref, b_ref, o_ref):
        o_ref[...] = jnp.dot(a_ref[...], b_ref[...], preferred_element_type=F32)

    return pl.pallas_call(
        body, name=name, grid=(j_n, s // tm),
        in_specs=[pl.BlockSpec((tm, k), lambda j, i: (i, 0)),
                  _layer_spec((None, k, n), l, lambda j, i: (j, 0, 0))],
        out_specs=pl.BlockSpec((None, tm, n), lambda j, i: (j, i, 0)),
        out_shape=jax.ShapeDtypeStruct((j_n, s, n), F32),
        compiler_params=_params("parallel", "parallel"))(a, b)


def mm_acc(a, b, res, *, l=None, name):
    j_n, s, kj = a.shape
    n = b.shape[-1]
    tm = min(ROW_TILE, s)

    def body(a_ref, b_ref, r_ref, o_ref):
        acc = r_ref[...]
        for j in range(j_n):
            acc = acc + jnp.dot(a_ref[j], b_ref[j], preferred_element_type=F32)
        o_ref[...] = acc

    return pl.pallas_call(
        body, name=name, grid=(s // tm,),
        in_specs=[pl.BlockSpec((j_n, tm, kj), lambda i: (0, i, 0)),
                  _layer_spec((j_n, kj, n), l, lambda i: (0, 0, 0)),
                  pl.BlockSpec((tm, n), lambda i: (i, 0))],
        out_specs=pl.BlockSpec((tm, n), lambda i: (i, 0)),
        out_shape=jax.ShapeDtypeStruct((s, n), F32),
        compiler_params=_params("parallel"))(a, b, res)


_NT = (((1,), (1,)), ((), ()))
_TN = (((0,), (0,)), ((), ()))


def mm_nt_sum(pairs, *, l=None, name):
    j_n, s, nj = pairs[0][0].shape
    k = pairs[0][1].shape[-2]
    tm = min(ROW_TILE, s)
    n_p = len(pairs)

    def body(*refs):
        o_ref = refs[-1]
        acc = jnp.zeros((tm, k), F32)
        for p in range(n_p):
            dy_ref, w_ref = refs[2 * p], refs[2 * p + 1]
            for j in range(j_n):
                acc = acc + lax.dot_general(dy_ref[j], w_ref[j], _NT, preferred_element_type=F32)
        o_ref[...] = acc

    in_specs, args = [], []
    for dy, w in pairs:
        in_specs += [pl.BlockSpec((j_n, tm, nj), lambda i: (0, i, 0)),
                     _layer_spec((j_n, k, nj), l, lambda i: (0, 0, 0))]
        args += [dy, w]
    return pl.pallas_call(
        body, name=name, grid=(s // tm,), in_specs=in_specs,
        out_specs=pl.BlockSpec((tm, k), lambda i: (i, 0)),
        out_shape=jax.ShapeDtypeStruct((s, k), F32),
        compiler_params=_params("parallel"))(*args)


def mm_nt_each(a, b, *, l=None, name, pin=None):
    s, k = a.shape
    j_n, nj = b.shape[-3], b.shape[-2]
    tm = min(ROW_TILE, s)

    def body(a_ref, b_ref, *rest):
        rest[-1][...] = lax.dot_general(a_ref[...], b_ref[...], _NT, preferred_element_type=F32)

    return pl.pallas_call(
        body, name=name, grid=(j_n, s // tm),
        in_specs=[pl.BlockSpec((tm, k), lambda j, i: (i, 0)),
                  _layer_spec((None, nj, k), l, lambda j, i: (j, 0, 0))] + ([ANY] if pin is not None else []),
        out_specs=pl.BlockSpec((None, tm, nj), lambda j, i: (j, i, 0)),
        out_shape=jax.ShapeDtypeStruct((j_n, s, nj), F32),
        compiler_params=_params("parallel", "parallel"))(a, b, *([pin] if pin is not None else []))


def mm_tn(a, dy, *, name, pin=None):
    ja, s, k = a.shape
    jd, _, n = dy.shape
    j_n = max(ja, jd)
    tk = min(ROW_TILE, s)

    def body(a_ref, d_ref, *rest):
        o_ref = rest[-1]

        @pl.when(pl.program_id(1) == 0)
        def _():
            o_ref[...] = jnp.zeros_like(o_ref)
        o_ref[...] += lax.dot_general(a_ref[...], d_ref[...], _TN, preferred_element_type=F32)

    return pl.pallas_call(
        body, name=name, grid=(j_n, s // tk),
        in_specs=[pl.BlockSpec((None, tk, k), (lambda j, i: (j, i, 0)) if ja > 1 else (lambda j, i: (0, i, 0))),
                  pl.BlockSpec((None, tk, n), (lambda j, i: (j, i, 0)) if jd > 1 else (lambda j, i: (0, i, 0)))]
        + ([ANY] if pin is not None else []),
        out_specs=pl.BlockSpec((None, k, n), lambda j, i: (j, 0, 0)),
        out_shape=jax.ShapeDtypeStruct((j_n, k, n), F32),
        compiler_params=_params("parallel", "arbitrary"))(a, dy, *([pin] if pin is not None else []))


def mm_tn_shared(a, dy, *, name):
    s, k = a.shape
    j_n, _, n = dy.shape
    tk = min(ROW_TILE, s)

    def body(a_ref, d_ref, o_ref):
        @pl.when(pl.program_id(0) == 0)
        def _():
            o_ref[...] = jnp.zeros_like(o_ref)
        a_t = a_ref[...]
        for j in range(j_n):
            o_ref[j] += lax.dot_general(a_t, d_ref[j], _TN, preferred_element_type=F32)

    return pl.pallas_call(
        body, name=name, grid=(s // tk,),
        in_specs=[pl.BlockSpec((tk, k), lambda i: (i, 0)), pl.BlockSpec((j_n, tk, n), lambda i: (0, i, 0))],
        out_specs=pl.BlockSpec((j_n, k, n), lambda i: (0, 0, 0)),
        out_shape=jax.ShapeDtypeStruct((j_n, k, n), F32),
        compiler_params=_params("arbitrary"))(a, dy)


def _back(x, k):
    return pltpu.roll(x, k, 0)


def _fwd(x, k):
    return pltpu.roll(x, x.shape[0] - k, 0)


def _causal_conv(x, w_ref):
    return w_ref[0:1, :] * _back(x, 2) + w_ref[1:2, :] * _back(x, 1) + w_ref[2:3, :] * x


def _causal_conv_t(dy, w_ref):
    return w_ref[2:3, :] * dy + w_ref[1:2, :] * _fwd(dy, 1) + w_ref[0:1, :] * _fwd(dy, 2)


def _gelu(x):
    return 0.5 * x * (1.0 + lax.erf(x * 0.7071067811865476))


def _gelu_grad(x):
    return 0.5 * (1.0 + lax.erf(x * 0.7071067811865476)) + x * jnp.exp(-0.5 * x * x) * 0.3989422804014327


def _colsum(x):
    return jnp.sum(x, axis=0, keepdims=True)


def rms_fwd(h, gain, *, name, pin=None):
    s, d = h.shape
    ts = min(ROW_TILE, s)

    def body(h_ref, g_ref, *rest):
        o_ref, r_ref = rest[-2:]
        x = h_ref[...]
        rstd = lax.rsqrt(jnp.mean(x * x, axis=-1, keepdims=True) + EPS)
        o_ref[...] = (x * rstd * g_ref[...]).astype(BF16)
        r_ref[...] = rstd

    return pl.pallas_call(
        body, name=name, grid=(s // ts,),
        in_specs=[pl.BlockSpec((ts, d), lambda i: (i, 0)), pl.BlockSpec((1, d), lambda i: (0, 0))]
        + ([ANY] if pin is not None else []),
        out_specs=[pl.BlockSpec((ts, d), lambda i: (i, 0)), pl.BlockSpec((ts, 1), lambda i: (i, 0))],
        out_shape=[jax.ShapeDtypeStruct((s, d), BF16), jax.ShapeDtypeStruct((s, 1), F32)],
        compiler_params=_params("parallel"))(h, gain, *([pin] if pin is not None else []))


def rms_bwd(dxn, h, gain, rstd, dres, *, name):
    s, d = h.shape
    ts = min(ROW_TILE, s)

    def body(dx_ref, h_ref, g_ref, r_ref, dr_ref, o_ref, ob_ref, dg_ref):
        @pl.when(pl.program_id(0) == 0)
        def _():
            dg_ref[...] = jnp.zeros_like(dg_ref)
        rstd_v = r_ref[...]
        hhat = h_ref[...] * rstd_v
        dx = dx_ref[...]
        dg_ref[...] += _colsum(dx * hhat)
        dxg = dx * g_ref[...]
        dh = dr_ref[...] + rstd_v * (dxg - hhat * jnp.mean(dxg * hhat, axis=-1, keepdims=True))
        o_ref[...] = dh
        ob_ref[...] = dh.astype(BF16)

    row = pl.BlockSpec((ts, d), lambda i: (i, 0))
    vec = pl.BlockSpec((1, d), lambda i: (0, 0))
    return pl.pallas_call(
        body, name=name, grid=(s // ts,),
        in_specs=[row, row, vec, pl.BlockSpec((ts, 1), lambda i: (i, 0)), row],
        out_specs=[row, row, vec],
        out_shape=[jax.ShapeDtypeStruct((s, d), F32), jax.ShapeDtypeStruct((s, d), BF16),
                   jax.ShapeDtypeStruct((1, d), F32)],
        compiler_params=_params("arbitrary"))(dxn, h, gain, rstd, dres)


def final_loss(h, target, gain, *, name):
    s, d = h.shape
    ts = min(ROW_TILE, s)

    def body(h_ref, t_ref, g_ref, o_ref, ob_ref, l_ref, dg_ref):
        @pl.when(pl.program_id(0) == 0)
        def _():
            l_ref[...] = jnp.zeros_like(l_ref)
            dg_ref[...] = jnp.zeros_like(dg_ref)
        x = h_ref[...]
        rstd = lax.rsqrt(jnp.mean(x * x, axis=-1, keepdims=True) + EPS)
        hhat = x * rstd
        err = hhat * g_ref[...] - t_ref[...]
        l_ref[...] += 0.5 * jnp.sum(jnp.mean(err * err, axis=-1, keepdims=True), axis=0, keepdims=True)
        dy = err * (1.0 / d)
        dg_ref[...] += _colsum(dy * hhat)
        dyg = dy * g_ref[...]
        dh = rstd * (dyg - hhat * jnp.mean(dyg * hhat, axis=-1, keepdims=True))
        o_ref[...] = dh
        ob_ref[...] = dh.astype(BF16)

    row = pl.BlockSpec((ts, d), lambda i: (i, 0))
    vec = pl.BlockSpec((1, d), lambda i: (0, 0))
    return pl.pallas_call(
        body, name=name, grid=(s // ts,),
        in_specs=[row, row, vec],
        out_specs=[row, row, pl.BlockSpec((1, 128), lambda i: (0, 0)), vec],
        out_shape=[jax.ShapeDtypeStruct((s, d), F32), jax.ShapeDtypeStruct((s, d), BF16),
                   jax.ShapeDtypeStruct((1, 128), F32), jax.ShapeDtypeStruct((1, d), F32)],
        compiler_params=_params("arbitrary"))(h, target, gain)


def _halo_specs(n_lead, ts, width, n_tiles):
    hb = ts // HALO
    prev = pl.BlockSpec((n_lead, HALO, width), lambda i: (0, jnp.maximum(i * hb - 1, 0), 0))
    nxt = pl.BlockSpec((n_lead, HALO, width), lambda i: (0, jnp.minimum((i + 1) * hb, n_tiles * hb - 1), 0))
    return prev, nxt


def _pool_fwd(z_ext, g, pos):
    w = POOL_WINDOWS[g]
    zg = z_ext[:, g * GROUP:(g + 1) * GROUP]
    acc = zg
    sh = 1
    while sh < w:
        acc = acc + _back(acc, sh)
        sh *= 2
    return acc[HALO:] / jnp.minimum(pos, float(w)) - zg[HALO:]


def even_layer_fwd(h, gain, win, conv_a, w_pool, pool_scale, wout, *, name, pin=None):
    s, d = h.shape
    w = win.shape[-1]
    ts = min(ROW_TILE, s)
    hb = ts // HALO

    def body(h_ref, hp_ref, gain_ref, win_ref, ca_ref, wp_ref, ps_ref, wout_ref, *rest):
        o_ref, xn_ref, r_ref, p_ref, m_ref = rest[-5:]
        i = pl.program_id(0)
        keep = jnp.where(i > 0, 1.0, 0.0)
        h_ext = jnp.concatenate([hp_ref[...], h_ref[...]], axis=0)
        rstd = lax.rsqrt(jnp.mean(h_ext * h_ext, axis=-1, keepdims=True) + EPS)
        xn_ext = (h_ext * rstd * gain_ref[...]).astype(BF16)
        xn_ref[...] = xn_ext[HALO:]
        r_ref[...] = rstd[HALO:]
        p32 = []
        for k in range(4):
            pk = jnp.dot(xn_ext, win_ref[k], preferred_element_type=F32).astype(BF16)
            p_ref[k] = pk[HALO:]
            pk = pk.astype(F32)
            p32.append(jnp.concatenate([pk[:HALO] * keep, pk[HALO:]], axis=0))
        m_ref[:, 0:w] = (p32[0][HALO:] * _causal_conv(p32[1] * p32[2], ca_ref)[HALO:]).astype(BF16)
        pos = (i * ts + lax.broadcasted_iota(jnp.int32, (ts, 1), 0) + 1).astype(F32)
        for g in range(len(POOL_WINDOWS)):
            pooled = _pool_fwd(p32[3], g, pos)
            mixed = jnp.dot(pooled.astype(BF16), wp_ref[g], preferred_element_type=F32)
            cols = slice(g * GROUP, (g + 1) * GROUP)
            m_ref[:, w + g * GROUP:w + (g + 1) * GROUP] = (mixed * ps_ref[:, cols]).astype(BF16)
        o_ref[...] = h_ref[...] + jnp.dot(m_ref[...], wout_ref[...], preferred_element_type=F32)

    const = lambda shape: pl.BlockSpec(shape, lambda i: (0,) * len(shape))
    row = pl.BlockSpec((ts, d), lambda i: (i, 0))
    return pl.pallas_call(
        body, name=name, grid=(s // ts,),
        in_specs=[row, pl.BlockSpec((HALO, d), lambda i: (jnp.maximum(i * hb - 1, 0), 0)), const((1, d)),
                  const((4, d, w)), const((3, w)), const((4, GROUP, GROUP)), const((1, w)), const((2 * w, d))]
        + ([ANY] if pin is not None else []),
        out_specs=[row, row, pl.BlockSpec((ts, 1), lambda i: (i, 0)),
                   pl.BlockSpec((4, ts, w), lambda i: (0, i, 0)), pl.BlockSpec((ts, 2 * w), lambda i: (i, 0))],
        out_shape=[jax.ShapeDtypeStruct((s, d), F32), jax.ShapeDtypeStruct((s, d), BF16),
                   jax.ShapeDtypeStruct((s, 1), F32), jax.ShapeDtypeStruct((4, s, w), BF16),
                   jax.ShapeDtypeStruct((s, 2 * w), BF16)],
        compiler_params=_params("parallel"))(
            h, h, gain, win, conv_a, w_pool, pool_scale, wout, *([pin] if pin is not None else []))


def even_bwd(proj, dh_bf, wout, conv_a, w_pool, w_pool_t, pool_scale, *, name, pin=None):
    _, s, w = proj.shape
    d = dh_bf.shape[1]
    ts = min(ROW_TILE, s)
    n_t = s // ts
    prev, nxt = _halo_specs(4, ts, w, n_t)
    hb = ts // HALO
    n_ext = ts + HALO

    def body(p_ref, pp_ref, pn_ref, dh_ref, dhn_ref, wout_ref, ca_ref, wp_ref, wpt_ref, ps_ref, *rest):
        dp_ref, dca_ref, dwp_ref, dps_ref = rest[-4:]
        i = pl.program_id(0)

        @pl.when(i == 0)
        def _():
            dca_ref[...] = jnp.zeros_like(dca_ref)
            dwp_ref[...] = jnp.zeros_like(dwp_ref)
            dps_ref[...] = jnp.zeros_like(dps_ref)

        keep_p = jnp.where(i > 0, 1.0, 0.0)
        keep_n = jnp.where(i < n_t - 1, 1.0, 0.0)
        dmix = lax.dot_general(jnp.concatenate([dh_ref[...], dhn_ref[...]], axis=0), wout_ref[...], _NT,
                               preferred_element_type=F32)
        a_b, a_c, a_v = (p_ref[k].astype(F32) for k in range(3))
        cv_ext = jnp.concatenate([pp_ref[1].astype(F32) * pp_ref[2].astype(F32) * keep_p, a_c * a_v], axis=0)
        dy_a = dmix[:ts, 0:w]
        dp_ref[0] = (dy_a * _causal_conv(cv_ext, ca_ref)[HALO:]).astype(BF16)
        dcc = dy_a * a_b
        dca_ref[2:3, :] += _colsum(dcc * cv_ext[HALO:])
        dca_ref[1:2, :] += _colsum(dcc * _back(cv_ext, 1)[HALO:])
        dca_ref[0:1, :] += _colsum(dcc * _back(cv_ext, 2)[HALO:])
        dcc_ext = jnp.concatenate([dcc, dmix[ts:, 0:w] * pn_ref[0].astype(F32) * keep_n], axis=0)
        dcv = _causal_conv_t(dcc_ext, ca_ref)[:ts]
        dp_ref[1] = (dcv * a_v).astype(BF16)
        dp_ref[2] = (dcv * a_c).astype(BF16)
        z_ext = jnp.concatenate([pp_ref[3].astype(F32) * keep_p, p_ref[3].astype(F32)], axis=0)
        pos = (i * ts + lax.broadcasted_iota(jnp.int32, (ts, 1), 0) + 1).astype(F32)
        pos_ext = (i * ts + lax.broadcasted_iota(jnp.int32, (n_ext, 1), 0) + 1).astype(F32)
        for g, win in enumerate(POOL_WINDOWS):
            cols = slice(g * GROUP, (g + 1) * GROUP)
            ycols = slice(w + g * GROUP, w + (g + 1) * GROUP)
            pooled = _pool_fwd(z_ext, g, pos).astype(BF16)
            mixed = jnp.dot(pooled, wp_ref[g], preferred_element_type=F32)
            dy_b = dmix[:ts, ycols]
            dps_ref[:, cols] += _colsum(dy_b * mixed)
            dmixed_ext = jnp.concatenate([dy_b, dmix[ts:, ycols] * keep_n], axis=0) * ps_ref[:, cols]
            dmixed_ext = dmixed_ext.astype(BF16)
            dwp_ref[g] += lax.dot_general(pooled, dmixed_ext[:ts], _TN, preferred_element_type=F32)
            dpooled = jnp.dot(dmixed_ext, wpt_ref[g], preferred_element_type=F32)
            acc = dpooled / jnp.minimum(pos_ext, float(win))
            sh = 1
            while sh < win:
                acc = acc + _fwd(acc, sh)
                sh *= 2
            dp_ref[3, :, cols] = (acc[:ts] - dpooled[:ts]).astype(BF16)

    tile4 = pl.BlockSpec((4, ts, w), lambda i: (0, i, 0))
    const = lambda shape: pl.BlockSpec(shape, lambda i: (0,) * len(shape))
    return pl.pallas_call(
        body, name=name, grid=(n_t,),
        in_specs=[tile4, prev, nxt, pl.BlockSpec((ts, d), lambda i: (i, 0)),
                  pl.BlockSpec((HALO, d), lambda i: (jnp.minimum((i + 1) * hb, n_t * hb - 1), 0)),
                  const((2 * w, d)), const((3, w)), const((4, GROUP, GROUP)), const((4, GROUP, GROUP)), const((1, w))]
        + ([ANY] if pin is not None else []),
        out_specs=[tile4, const((3, w)), const((4, GROUP, GROUP)), const((1, w))],
        out_shape=[jax.ShapeDtypeStruct((4, s, w), BF16), jax.ShapeDtypeStruct((3, w), F32),
                   jax.ShapeDtypeStruct((4, GROUP, GROUP), F32), jax.ShapeDtypeStruct((1, w), F32)],
        compiler_params=_params("arbitrary"))(
            proj, proj, proj, dh_bf, dh_bf, wout, conv_a, w_pool, w_pool_t, pool_scale,
            *([pin] if pin is not None else []))


def _ffn_halo(ts, f, n_t, nxt):
    hb = ts // HALO
    if nxt:
        return pl.BlockSpec((None, HALO, f), lambda j, i: (j, jnp.minimum((i + 1) * hb, n_t * hb - 1), 0))
    return pl.BlockSpec((None, HALO, f), lambda j, i: (j, jnp.maximum(i * hb - 1, 0), 0))


def ffn_act_fwd(g, up, cw, cb, *, name):
    j_n, s, f = g.shape
    ts = min(ROW_TILE, s)
    n_t = s // ts

    def body(g_ref, gp_ref, u_ref, cw_ref, cb_ref, o_ref):
        keep = jnp.where(pl.program_id(1) > 0, 1.0, 0.0)
        g_ext = jnp.concatenate([gp_ref[...] * keep, g_ref[...]], axis=0)
        gc = _causal_conv(g_ext, cw_ref)[HALO:] + cb_ref[...]
        o_ref[...] = (gc * jax.nn.sigmoid(gc) * u_ref[...]).astype(BF16)

    tile = pl.BlockSpec((None, ts, f), lambda j, i: (j, i, 0))
    return pl.pallas_call(
        body, name=name, grid=(j_n, n_t),
        in_specs=[tile, _ffn_halo(ts, f, n_t, False), tile,
                  pl.BlockSpec((None, 3, f), lambda j, i: (j, 0, 0)),
                  pl.BlockSpec((None, 1, f), lambda j, i: (j, 0, 0))],
        out_specs=tile,
        out_shape=jax.ShapeDtypeStruct((j_n, s, f), BF16),
        compiler_params=_params("parallel", "parallel"))(g, g, up, cw, cb)


def ffn_act_bwd(g, up, dact, cw, cb, *, name):
    j_n, s, f = g.shape
    ts = min(ROW_TILE, s)
    n_t = s // ts

    def body(g_ref, gp_ref, gn_ref, u_ref, un_ref, d_ref, dn_ref, cw_ref, cb_ref,
             dg_ref, du_ref, dcw_ref, dcb_ref):
        i = pl.program_id(1)

        @pl.when(i == 0)
        def _():
            dcw_ref[...] = jnp.zeros_like(dcw_ref)
            dcb_ref[...] = jnp.zeros_like(dcb_ref)

        keep_p = jnp.where(i > 0, 1.0, 0.0)
        keep_n = jnp.where(i < n_t - 1, 1.0, 0.0)
        g_ext = jnp.concatenate([gp_ref[...] * keep_p, g_ref[...], gn_ref[...]], axis=0)
        gc = _causal_conv(g_ext, cw_ref)[HALO:] + cb_ref[...]
        sig = jax.nn.sigmoid(gc)
        dact_ext = jnp.concatenate([d_ref[...], dn_ref[...] * keep_n], axis=0)
        du_ref[...] = (dact_ext * gc * sig)[:ts].astype(BF16)
        up_ext = jnp.concatenate([u_ref[...], un_ref[...]], axis=0)
        dgc = dact_ext * up_ext * (sig * (1.0 + gc * (1.0 - sig)))
        dg_ref[...] = _causal_conv_t(dgc, cw_ref)[:ts].astype(BF16)
        dgc_t = dgc[:ts]
        dcb_ref[...] += _colsum(dgc_t)
        dcw_ref[2:3, :] += _colsum(dgc_t * g_ext[HALO:HALO + ts])
        dcw_ref[1:2, :] += _colsum(dgc_t * _back(g_ext, 1)[HALO:HALO + ts])
        dcw_ref[0:1, :] += _colsum(dgc_t * _back(g_ext, 2)[HALO:HALO + ts])

    tile = pl.BlockSpec((None, ts, f), lambda j, i: (j, i, 0))
    prev, nxt = _ffn_halo(ts, f, n_t, False), _ffn_halo(ts, f, n_t, True)
    return pl.pallas_call(
        body, name=name, grid=(j_n, n_t),
        in_specs=[tile, prev, nxt, tile, nxt, tile, nxt,
                  pl.BlockSpec((None, 3, f), lambda j, i: (j, 0, 0)),
                  pl.BlockSpec((None, 1, f), lambda j, i: (j, 0, 0))],
        out_specs=[tile, tile, pl.BlockSpec((None, 3, f), lambda j, i: (j, 0, 0)),
                   pl.BlockSpec((None, 1, f), lambda j, i: (j, 0, 0))],
        out_shape=[jax.ShapeDtypeStruct((j_n, s, f), BF16), jax.ShapeDtypeStruct((j_n, s, f), BF16),
                   jax.ShapeDtypeStruct((j_n, 3, f), F32), jax.ShapeDtypeStruct((j_n, 1, f), F32)],
        compiler_params=_params("parallel", "arbitrary"))(g, g, g, up, up, dact, dact, cw, cb)


def ffn_in_fwd(hn, wg, wu, cw, cb, *, name, pin=None):
    s, d = hn.shape
    j_n, f, _ = wg.shape
    tm = min(ROW_TILE, s)
    hb = tm // HALO

    def body(x_ref, xp_ref, wg_ref, wu_ref, cw_ref, cb_ref, *rest):
        g_ref, u_ref, gc_ref, a_ref = rest[-4:]
        i, j = pl.program_id(0), pl.program_id(1)
        x_ext = jnp.concatenate([xp_ref[...], x_ref[...]], axis=0)
        g_ext = lax.dot_general(x_ext, wg_ref[j], _NT, preferred_element_type=F32).astype(BF16)
        up = lax.dot_general(x_ref[...], wu_ref[j], _NT, preferred_element_type=F32).astype(BF16)
        g_ref[...] = g_ext[HALO:]
        u_ref[...] = up
        a_ref[...] = _ffn_act(g_ext, up, cw_ref, cb_ref, gc_ref, i)

    whole = pl.BlockSpec((j_n, f, d), lambda i, j: (0, 0, 0))
    tile = pl.BlockSpec((None, tm, f), lambda i, j: (j, i, 0))
    shape = jax.ShapeDtypeStruct((j_n, s, f), BF16)
    return pl.pallas_call(
        body, name=name, grid=(s // tm, j_n),
        in_specs=[pl.BlockSpec((tm, d), lambda i, j: (i, 0)),
                  pl.BlockSpec((HALO, d), lambda i, j: (jnp.maximum(i * hb - 1, 0), 0)),
                  whole, whole,
                  pl.BlockSpec((None, 3, f), lambda i, j: (j, 0, 0)),
                  pl.BlockSpec((None, 1, f), lambda i, j: (j, 0, 0))] + ([ANY] if pin is not None else []),
        out_specs=[tile] * 4, out_shape=[shape] * 4,
        compiler_params=_params("parallel", "parallel"))(hn, hn, wg, wu, cw, cb, *([pin] if pin is not None else []))


def _ffn_act(g_ext, up, cw_ref, cb_ref, gc_ref, i):
    keep = jnp.where(i > 0, 1.0, 0.0)
    g32 = jnp.concatenate([g_ext[:HALO].astype(F32) * keep, g_ext[HALO:].astype(F32)], axis=0)
    gc = (_causal_conv(g32, cw_ref)[HALO:] + cb_ref[...]).astype(BF16)
    gc_ref[...] = gc
    gc = gc.astype(F32)
    return (gc * jax.nn.sigmoid(gc) * up.astype(F32)).astype(BF16)


def ffn_fwd(h, gain, wg, wu, wd, cw, cb, *, name):
    s, d = h.shape
    j_n, f, _ = wg.shape
    tm = min(ROW_TILE, s)
    hb = tm // HALO

    def body(h_ref, hp_ref, gain_ref, wg_ref, wu_ref, wd_ref, cw_ref, cb_ref,
             o_ref, xn_ref, r_ref, g_ref, u_ref, gc_ref, x_s):
        i, j = pl.program_id(0), pl.program_id(1)

        @pl.when(j == 0)
        def _():
            h_ext = jnp.concatenate([hp_ref[...], h_ref[...]], axis=0)
            rstd = lax.rsqrt(jnp.mean(h_ext * h_ext, axis=-1, keepdims=True) + EPS)
            x_s[...] = (h_ext * rstd * gain_ref[...]).astype(BF16)
            xn_ref[...] = x_s[HALO:, :]
            r_ref[...] = rstd[HALO:]
            o_ref[...] = h_ref[...]

        g_ext = lax.dot_general(x_s[...], wg_ref[j], _NT, preferred_element_type=F32).astype(BF16)
        up = lax.dot_general(x_s[HALO:, :], wu_ref[j], _NT, preferred_element_type=F32).astype(BF16)
        g_ref[...] = g_ext[HALO:]
        u_ref[...] = up
        act = _ffn_act(g_ext, up, cw_ref, cb_ref, gc_ref, i)
        o_ref[...] += jnp.dot(act, wd_ref[j], preferred_element_type=F32)

    whole = pl.BlockSpec((j_n, f, d), lambda i, j: (0, 0, 0), pipeline_mode=pl.Buffered(1))
    row = pl.BlockSpec((tm, d), lambda i, j: (i, 0))
    tile = pl.BlockSpec((None, tm, f), lambda i, j: (j, i, 0))
    return pl.pallas_call(
        body, name=name, grid=(s // tm, j_n),
        in_specs=[row, pl.BlockSpec((HALO, d), lambda i, j: (jnp.maximum(i * hb - 1, 0), 0)),
                  pl.BlockSpec((1, d), lambda i, j: (0, 0)), whole, whole, whole,
                  pl.BlockSpec((None, 3, f), lambda i, j: (j, 0, 0)),
                  pl.BlockSpec((None, 1, f), lambda i, j: (j, 0, 0))],
        out_specs=[row, row, pl.BlockSpec((tm, 1), lambda i, j: (i, 0)), tile, tile, tile],
        out_shape=[jax.ShapeDtypeStruct((s, d), F32), jax.ShapeDtypeStruct((s, d), BF16),
                   jax.ShapeDtypeStruct((s, 1), F32)] + [jax.ShapeDtypeStruct((j_n, s, f), BF16)] * 3,
        scratch_shapes=[pltpu.VMEM((HALO + tm, d), BF16)],
        compiler_params=_params("parallel", "arbitrary"))(h, h, gain, wg, wu, wd, cw, cb)


def ffn_bwd_a(dh_bf, hn, g, up, gc, wd, cw, *, name, pin=None):
    s, d = hn.shape
    j_n, _, f = g.shape
    tm = min(ROW_TILE, s)
    n_t = s // tm
    hb = tm // HALO

    def body(dh_ref, dhn_ref, x_ref, g_ref, gc_ref, gcn_ref, u_ref, un_ref, wd_ref, cw_ref, *rest):
        dg_ref, du_ref, dwg_ref, dwu_ref, dwd_ref, dcw_ref, dcb_ref = rest[-7:]
        i = pl.program_id(1)

        @pl.when(i == 0)
        def _():
            for r in (dwg_ref, dwu_ref, dwd_ref, dcw_ref, dcb_ref):
                r[...] = jnp.zeros_like(r)

        keep_n = jnp.where(i < n_t - 1, 1.0, 0.0)
        dh = dh_ref[...]
        dact = lax.dot_general(jnp.concatenate([dh, dhn_ref[...]], axis=0), wd_ref[...], _NT,
                               preferred_element_type=F32)
        dact = jnp.concatenate([dact[:tm], dact[tm:] * keep_n], axis=0)
        gc_ext = jnp.concatenate([gc_ref[...], gcn_ref[...]], axis=0).astype(F32)
        sig = jax.nn.sigmoid(gc_ext)
        silu = gc_ext * sig
        up_ext = jnp.concatenate([u_ref[...], un_ref[...]], axis=0).astype(F32)
        act = (silu * up_ext)[:tm].astype(BF16)
        dwd_ref[...] += lax.dot_general(act, dh, _TN, preferred_element_type=F32)
        dup = (dact * silu)[:tm].astype(BF16)
        du_ref[...] = dup
        dgc = dact * up_ext * (sig + silu * (1.0 - sig))
        dgc_1, dgc_2 = _fwd(dgc, 1), _fwd(dgc, 2)
        dg = (cw_ref[2:3, :] * dgc + cw_ref[1:2, :] * dgc_1 + cw_ref[0:1, :] * dgc_2)[:tm].astype(BF16)
        dg_ref[...] = dg
        x = x_ref[...]
        dwg_ref[...] += lax.dot_general(dg, x, _TN, preferred_element_type=F32)
        dwu_ref[...] += lax.dot_general(dup, x, _TN, preferred_element_type=F32)
        g32 = g_ref[...].astype(F32)
        dcb_ref[...] += _colsum(dgc[:tm])
        dcw_ref[2:3, :] += _colsum(dgc[:tm] * g32)
        dcw_ref[1:2, :] += _colsum(dgc_1[:tm] * g32)
        dcw_ref[0:1, :] += _colsum(dgc_2[:tm] * g32)

    rows = pl.BlockSpec((tm, d), lambda j, i: (i, 0))
    rows_next = pl.BlockSpec((HALO, d), lambda j, i: (jnp.minimum((i + 1) * hb, n_t * hb - 1), 0))
    tile = pl.BlockSpec((None, tm, f), lambda j, i: (j, i, 0))
    nxt = _ffn_halo(tm, f, n_t, True)
    per_j = lambda r, c: pl.BlockSpec((None, r, c), lambda j, i: (j, 0, 0))
    return pl.pallas_call(
        body, name=name, grid=(j_n, n_t),
        in_specs=[rows, rows_next, rows, tile, tile, nxt, tile, nxt, per_j(f, d), per_j(3, f)]
        + ([ANY] if pin is not None else []),
        out_specs=[tile, tile, per_j(f, d), per_j(f, d), per_j(f, d), per_j(3, f), per_j(1, f)],
        out_shape=[jax.ShapeDtypeStruct((j_n, s, f), BF16), jax.ShapeDtypeStruct((j_n, s, f), BF16),
                   jax.ShapeDtypeStruct((j_n, f, d), F32), jax.ShapeDtypeStruct((j_n, f, d), F32),
                   jax.ShapeDtypeStruct((j_n, f, d), F32), jax.ShapeDtypeStruct((j_n, 3, f), F32),
                   jax.ShapeDtypeStruct((j_n, 1, f), F32)],
        compiler_params=_params("parallel", "arbitrary"))(
            dh_bf, dh_bf, hn, g, gc, gc, up, up, wd, cw, *([pin] if pin is not None else []))


def dx_rms_bwd(pairs, h, gain, rstd, dres, *, name, pin=None):
    j_n, s, f = pairs[0][0].shape
    d = h.shape[1]
    tm = min(ROW_TILE, s)
    n_p = len(pairs)
    dims = [_NT if w.shape[1:] == (d, f) else (((1,), (0,)), ((), ())) for _, w in pairs]

    def body(*refs):
        dy_refs, w_refs = refs[:n_p], refs[n_p:2 * n_p]
        h_ref, g_ref, r_ref, dr_ref = refs[2 * n_p:2 * n_p + 4]
        o_ref, ob_ref, dgain_ref = refs[-3:]

        @pl.when(pl.program_id(0) == 0)
        def _():
            dgain_ref[...] = jnp.zeros_like(dgain_ref)
        dx = jnp.zeros((tm, d), F32)
        for j in range(j_n):
            for p in range(n_p):
                dx = dx + lax.dot_general(dy_refs[p][j], w_refs[p][j], dims[p], preferred_element_type=F32)
        rstd_v = r_ref[...]
        hhat = h_ref[...] * rstd_v
        dgain_ref[...] += _colsum(dx * hhat)
        dxg = dx * g_ref[...]
        dh = dr_ref[...] + rstd_v * (dxg - hhat * jnp.mean(dxg * hhat, axis=-1, keepdims=True))
        o_ref[...] = dh
        ob_ref[...] = dh.astype(BF16)

    tile4 = pl.BlockSpec((j_n, tm, f), lambda i: (0, i, 0))
    whole = [pl.BlockSpec(w.shape, lambda i: (0, 0, 0), pipeline_mode=pl.Buffered(1)) for _, w in pairs]
    row = pl.BlockSpec((tm, d), lambda i: (i, 0))
    vec = pl.BlockSpec((1, d), lambda i: (0, 0))
    return pl.pallas_call(
        body, name=name, grid=(s // tm,),
        in_specs=[tile4] * n_p + whole + [row, vec, pl.BlockSpec((tm, 1), lambda i: (i, 0)), row]
        + ([ANY] if pin is not None else []),
        out_specs=[row, row, vec],
        out_shape=[jax.ShapeDtypeStruct((s, d), F32), jax.ShapeDtypeStruct((s, d), BF16),
                   jax.ShapeDtypeStruct((1, d), F32)],
        compiler_params=_params("arbitrary"))(
            *[p[0] for p in pairs], *[p[1] for p in pairs], h, gain, rstd, dres, *([pin] if pin is not None else []))


def _sgu_gate(vn_bf, ws_ref, bs_ref, h, rows):
    tri = lax.broadcasted_iota(jnp.int32, (CHUNK, CHUNK), 0) >= lax.broadcasted_iota(jnp.int32, (CHUNK, CHUNK), 1)
    ws = jnp.where(tri, ws_ref[h], 0.0).astype(BF16)
    cols = slice((h % 4) * GROUP, (h % 4 + 1) * GROUP)
    return ws, jnp.dot(ws, vn_bf[h // 4][rows, cols], preferred_element_type=F32) + bs_ref[h]


def odd_layer_fwd(h, gain, win, sgu_norm, w_spatial, b_spatial, wout, *, name):
    s, d = h.shape
    w = win.shape[-1]
    ts = min(ROW_TILE, s)
    n_heads = w_spatial.shape[0]

    def body(h_ref, gain_ref, win_ref, n_ref, ws_ref, bs_ref, wout_ref, o_ref, xn_ref, r_ref, p_ref, m_ref, rv_ref):
        x = h_ref[...]
        rstd_x = lax.rsqrt(jnp.mean(x * x, axis=-1, keepdims=True) + EPS)
        xn = (x * rstd_x * gain_ref[...]).astype(BF16)
        xn_ref[...] = xn
        r_ref[...] = rstd_x
        for k in range(4):
            p_ref[k] = jnp.dot(xn, win_ref[k], preferred_element_type=F32).astype(BF16)
        v = [_gelu(p_ref[2].astype(F32)), _gelu(p_ref[3].astype(F32))]
        ms = (jnp.sum(v[0] * v[0], axis=-1, keepdims=True) + jnp.sum(v[1] * v[1], axis=-1, keepdims=True)) / (2 * w)
        rstd = lax.rsqrt(ms + EPS)
        rv_ref[...] = rstd
        vn = [(v[k] * rstd * n_ref[:, k * w:(k + 1) * w]).astype(BF16) for k in range(2)]
        for hd in range(n_heads):
            cols = slice((hd % 4) * GROUP, (hd % 4 + 1) * GROUP)
            for c in range(ts // CHUNK):
                rows = slice(c * CHUNK, (c + 1) * CHUNK)
                _, gate = _sgu_gate(vn, ws_ref, bs_ref, hd, rows)
                u = _gelu(p_ref[hd // 4, rows, cols].astype(F32))
                m_ref[rows, hd * GROUP:(hd + 1) * GROUP] = (u * gate).astype(BF16)
        o_ref[...] = x + jnp.dot(m_ref[...], wout_ref[...], preferred_element_type=F32)

    const = lambda shape: pl.BlockSpec(shape, lambda i: (0,) * len(shape))
    row = pl.BlockSpec((ts, d), lambda i: (i, 0))
    col1 = pl.BlockSpec((ts, 1), lambda i: (i, 0))
    return pl.pallas_call(
        body, name=name, grid=(s // ts,),
        in_specs=[row, const((1, d)), const((4, d, w)), const((1, 2 * w)),
                  const((n_heads, CHUNK, CHUNK)), const((n_heads, CHUNK, 1)), const((2 * w, d))],
        out_specs=[row, row, col1, pl.BlockSpec((4, ts, w), lambda i: (0, i, 0)),
                   pl.BlockSpec((ts, 2 * w), lambda i: (i, 0)), col1],
        out_shape=[jax.ShapeDtypeStruct((s, d), F32), jax.ShapeDtypeStruct((s, d), BF16),
                   jax.ShapeDtypeStruct((s, 1), F32), jax.ShapeDtypeStruct((4, s, w), BF16),
                   jax.ShapeDtypeStruct((s, 2 * w), BF16), jax.ShapeDtypeStruct((s, 1), F32)],
        compiler_params=_params("parallel"))(h, gain, win, sgu_norm, w_spatial, b_spatial, wout)


def sgu_bwd(p, dh_bf, wout, rstd, sgu_norm, w_spatial, b_spatial, *, name, pin=None):
    _, s, w = p.shape
    d = dh_bf.shape[1]
    ts = min(ROW_TILE, s)
    n_heads = w_spatial.shape[0]

    def body(p_ref, dh_ref, wout_ref, r_ref, n_ref, ws_ref, bs_ref, *rest):
        dp_ref, dn_ref, dws_ref, dbs_ref, dvn_ref, dm_ref = rest[-6:]

        @pl.when(pl.program_id(0) == 0)
        def _():
            dn_ref[...] = jnp.zeros_like(dn_ref)
            dws_ref[...] = jnp.zeros_like(dws_ref)
            dbs_ref[...] = jnp.zeros_like(dbs_ref)

        dm_ref[...] = lax.dot_general(dh_ref[...], wout_ref[...], _NT, preferred_element_type=F32)
        rstd_v = r_ref[...]
        vhat = [_gelu(p_ref[2 + k].astype(F32)) * rstd_v for k in range(2)]
        vn = [(vhat[k] * n_ref[:, k * w:(k + 1) * w]).astype(BF16) for k in range(2)]
        tri = lax.broadcasted_iota(jnp.int32, (CHUNK, CHUNK), 0) >= lax.broadcasted_iota(jnp.int32, (CHUNK, CHUNK), 1)
        for h in range(n_heads):
            cols = slice((h % 4) * GROUP, (h % 4 + 1) * GROUP)
            ocols = slice(h * GROUP, (h + 1) * GROUP)
            for c in range(ts // CHUNK):
                rows = slice(c * CHUNK, (c + 1) * CHUNK)
                ws, gate = _sgu_gate(vn, ws_ref, bs_ref, h, rows)
                pu = p_ref[h // 4, rows, cols].astype(F32)
                dm = dm_ref[rows, ocols]
                dp_ref[h // 4, rows, cols] = (dm * gate * _gelu_grad(pu)).astype(BF16)
                dgate = dm * _gelu(pu)
                dbs_ref[h] += jnp.sum(dgate, axis=-1, keepdims=True)
                dgate_bf = dgate.astype(BF16)
                dws = lax.dot_general(dgate_bf, vn[h // 4][rows, cols], _NT, preferred_element_type=F32)
                dws_ref[h] += jnp.where(tri, dws, 0.0)
                dvn_ref[rows, ocols] = lax.dot_general(ws, dgate_bf, _TN, preferred_element_type=F32)
        for k in range(2):
            kc = slice(k * w, (k + 1) * w)
            dvn = dvn_ref[:, kc]
            dn_ref[:, kc] += _colsum(dvn * vhat[k])
        dvh = [dvn_ref[:, k * w:(k + 1) * w] * n_ref[:, k * w:(k + 1) * w] for k in range(2)]
        dot = (jnp.sum(dvh[0] * vhat[0], axis=-1, keepdims=True)
               + jnp.sum(dvh[1] * vhat[1], axis=-1, keepdims=True)) / (2 * w)
        for k in range(2):
            dv = rstd_v * (dvh[k] - vhat[k] * dot)
            dp_ref[2 + k] = (dv * _gelu_grad(p_ref[2 + k].astype(F32))).astype(BF16)

    const = lambda shape: pl.BlockSpec(shape, lambda i: (0,) * len(shape))
    tile4 = pl.BlockSpec((4, ts, w), lambda i: (0, i, 0))
    return pl.pallas_call(
        body, name=name, grid=(s // ts,),
        in_specs=[tile4, pl.BlockSpec((ts, d), lambda i: (i, 0)), const((2 * w, d)),
                  pl.BlockSpec((ts, 1), lambda i: (i, 0)),
                  const((1, 2 * w)), const((n_heads, CHUNK, CHUNK)), const((n_heads, CHUNK, 1))]
        + ([ANY] if pin is not None else []),
        out_specs=[tile4, const((1, 2 * w)), const((n_heads, CHUNK, CHUNK)), const((n_heads, CHUNK, 1))],
        out_shape=[jax.ShapeDtypeStruct((4, s, w), BF16), jax.ShapeDtypeStruct((1, 2 * w), F32),
                   jax.ShapeDtypeStruct((n_heads, CHUNK, CHUNK), F32),
                   jax.ShapeDtypeStruct((n_heads, CHUNK, 1), F32)],
        scratch_shapes=[pltpu.VMEM((ts, 2 * w), F32), pltpu.VMEM((ts, 2 * w), F32)],
        compiler_params=_params("arbitrary"))(
            p, dh_bf, wout, rstd, sgu_norm, w_spatial, b_spatial, *([pin] if pin is not None else []))


def _row_tile(rows):
    if rows <= ROW_TILE:
        return rows
    for t in (512, 384, 352, 256, 128, 64, 32, 16, 8):
        if rows % t == 0:
            return t
    return rows


def adamw(w, g, m, v, *, name):
    shape = w.shape
    cols = shape[-1]
    rows = w.size // cols
    w2, g2, m2, v2 = (a.reshape(rows, cols) for a in (w, g, m, v))
    tr = _row_tile(rows)
    bc1 = 1.0 - ADAM_B1 ** ADAM_STEP
    bc2 = 1.0 - ADAM_B2 ** ADAM_STEP

    def body(w_ref, g_ref, m_ref, v_ref, d_ref, nm_ref, nv_ref):
        grad = g_ref[...]
        m_new = ADAM_B1 * m_ref[...] + (1.0 - ADAM_B1) * grad
        v_new = ADAM_B2 * v_ref[...] + (1.0 - ADAM_B2) * (grad * grad)
        nm_ref[...] = m_new
        nv_ref[...] = v_new
        d_ref[...] = -ADAM_LR * ((m_new / bc1) / (jnp.sqrt(v_new / bc2) + ADAM_EPS) + ADAM_WD * w_ref[...])

    spec = pl.BlockSpec((tr, cols), lambda i: (i, 0))
    outs = pl.pallas_call(
        body, name=name, grid=(rows // tr,),
        in_specs=[spec] * 4, out_specs=[spec] * 3,
        out_shape=[jax.ShapeDtypeStruct((rows, cols), F32)] * 3,
        compiler_params=_params("parallel"))(w2, g2, m2, v2)
    return tuple(o.reshape(shape) for o in outs)


def adamw_layers(w, grads, m, v, *, name):
    n_l, rows, cols = w.shape
    tr = _row_tile(rows)
    bc1 = 1.0 - ADAM_B1 ** ADAM_STEP
    bc2 = 1.0 - ADAM_B2 ** ADAM_STEP
    outs = None
    for l in range(n_l):
        def body(w_ref, g_ref, m_ref, v_ref, *rest):
            go_ref, d_ref, nm_ref, nv_ref = rest[-4:]
            grad = g_ref[...]
            m_new = ADAM_B1 * m_ref[...] + (1.0 - ADAM_B1) * grad
            v_new = ADAM_B2 * v_ref[...] + (1.0 - ADAM_B2) * (grad * grad)
            go_ref[...] = grad
            nm_ref[...] = m_new
            nv_ref[...] = v_new
            d_ref[...] = -ADAM_LR * ((m_new / bc1) / (jnp.sqrt(v_new / bc2) + ADAM_EPS) + ADAM_WD * w_ref[...])

        layer = pl.BlockSpec((None, tr, cols), lambda i, l=l: (l, i, 0))
        prev = list(outs) if outs is not None else []
        outs = pl.pallas_call(
            body, name=f"{name}{l}", grid=(rows // tr,),
            in_specs=[layer, pl.BlockSpec((tr, cols), lambda i: (i, 0)), layer, layer] + [ANY] * len(prev),
            out_specs=[layer] * 4,
            out_shape=[jax.ShapeDtypeStruct(w.shape, F32)] * 4,
            input_output_aliases={4 + k: k for k in range(len(prev))},
            compiler_params=_params("parallel"))(w, grads[l], m, v, *prev)
    return tuple(outs)


def _place():
    return lax.axis_index("x"), lax.axis_index("y"), lax.axis_index("c")


def _other_chips(x, y):
    return [(1 - x, y), (x, 1 - y), (1 - x, 1 - y)]


HBM = pl.BlockSpec(memory_space=pltpu.HBM)
SEM = pl.BlockSpec(memory_space=pltpu.SEMAPHORE)
DATAFLOW = pltpu.SideEffectType.DATAFLOW_SIDE_EFFECTING


def _in_hbm(a):
    return pltpu.with_memory_space_constraint(a, pltpu.HBM)


def cast_into_slot(w, chip, *, l=None, name):
    rows, cols = w.shape[-2:]
    tr = _row_tile(rows)

    def body(chip_ref, w_ref, o_ref):
        o_ref[...] = w_ref[...].astype(BF16)

    in_spec = (pl.BlockSpec((tr, cols), lambda i, chip_ref: (i, 0)) if l is None
               else pl.BlockSpec((None, tr, cols), lambda i, chip_ref: (l, i, 0)))
    return pl.pallas_call(
        body, name=name,
        grid_spec=pltpu.PrefetchScalarGridSpec(
            num_scalar_prefetch=1, grid=(rows // tr,), in_specs=[in_spec],
            out_specs=pl.BlockSpec((None, tr, cols), lambda i, chip_ref: (chip_ref[0], i, 0))),
        out_shape=jax.ShapeDtypeStruct((N_CHIPS, rows, cols), BF16),
        compiler_params=_params("parallel"))(chip, w)


def _half(ref, slot, c):
    half = ref.shape[1] // 2
    return ref.at[slot, pl.ds(c * half, half), :]


def gather_start(groups, smalls):
    flat = [b for g in groups for b in g]
    n_b, n_s, n_g = len(flat), len(smalls), len(groups)

    def body(*refs):
        bufs, small_refs = refs[:n_b], refs[n_b:n_b + n_s]
        sems = refs[n_b + n_s:n_b + n_s + 2 * n_g + 2]
        token = refs[-1]
        x, y, c = _place()
        me = 2 * x + y
        chips = _other_chips(x, y)
        for si in range(n_s):
            piece = small_refs[si].at[me]
            for k, (px, py) in enumerate(chips):
                pltpu.make_async_remote_copy(
                    src_ref=piece, dst_ref=piece,
                    send_sem=sems[2 * n_g].at[3 * si + k], recv_sem=sems[2 * n_g + 1].at[3 * si + k],
                    device_id=(px, py, c), device_id_type=MESH).start()
        t = 0
        for gi, group in enumerate(groups):
            for ti in range(len(group)):
                piece = _half(bufs[t], me, c)
                t += 1
                for k, (px, py) in enumerate(chips):
                    pltpu.make_async_remote_copy(
                        src_ref=piece, dst_ref=piece,
                        send_sem=sems[2 * gi].at[3 * ti + k], recv_sem=sems[2 * gi + 1].at[3 * ti + k],
                        device_id=(px, py, c), device_id_type=MESH).start()
        token[...] = jnp.zeros_like(token)

    sem_shapes = []
    for group in groups:
        sem_shapes += [pltpu.SemaphoreType.DMA((3 * len(group),))] * 2
    sem_shapes += [pltpu.SemaphoreType.DMA((3 * n_s,))] * 2
    arrays = flat + list(smalls)
    n_sem = len(sem_shapes)
    res = pl.pallas_call(
        body, name="gather_start",
        out_shape=tuple(sem_shapes) + tuple(pltpu.HBM(a.shape, a.dtype) for a in arrays)
        + (jax.ShapeDtypeStruct((8, 128), F32),),
        in_specs=[HBM] * len(arrays),
        out_specs=tuple([SEM] * n_sem + [HBM] * len(arrays) + [pl.BlockSpec(memory_space=pltpu.VMEM)]),
        input_output_aliases={i: n_sem + i for i in range(len(arrays))},
        compiler_params=pltpu.CompilerParams(has_side_effects=DATAFLOW))(*[_in_hbm(a) for a in arrays])
    sems, thru, token = res[:n_sem], res[n_sem:-1], res[-1]
    out_groups, t = [], 0
    for group in groups:
        out_groups.append(list(thru[t:t + len(group)]))
        t += len(group)
    return sems, out_groups, list(thru[n_b:]), token


def gather_wait(bufs, send, recv, after, *, name, smalls=(), small_send=None, small_recv=None):
    n_b, n_s = len(bufs), len(smalls)
    arrays = list(bufs) + list(smalls)
    sem_ops = [send, recv] + ([small_send, small_recv] if n_s else [])

    def body(*refs):
        buf_refs, small_refs = refs[:n_b], refs[n_b:n_b + n_s]
        sems = refs[n_b + n_s:n_b + n_s + len(sem_ops)]
        x, y, c = _place()
        me = 2 * x + y
        chips = _other_chips(x, y)
        for ti in range(n_b):
            for k, (px, py) in enumerate(chips):
                done = pltpu.make_async_remote_copy(
                    src_ref=_half(buf_refs[ti], me, c), dst_ref=_half(buf_refs[ti], 2 * px + py, c),
                    send_sem=sems[0].at[3 * ti + k], recv_sem=sems[1].at[3 * ti + k],
                    device_id=(px, py, c), device_id_type=MESH)
                done.wait_send()
                done.wait_recv()
        for si in range(n_s):
            for k, (px, py) in enumerate(chips):
                done = pltpu.make_async_remote_copy(
                    src_ref=small_refs[si].at[me], dst_ref=small_refs[si].at[2 * px + py],
                    send_sem=sems[2].at[3 * si + k], recv_sem=sems[3].at[3 * si + k],
                    device_id=(px, py, c), device_id_type=MESH)
                done.wait_send()
                done.wait_recv()

    res = pl.pallas_call(
        body, name=name,
        out_shape=tuple(pltpu.HBM(a.shape, a.dtype) for a in arrays),
        in_specs=[HBM] * len(arrays) + [SEM] * len(sem_ops) + [ANY],
        out_specs=tuple([HBM] * len(arrays)),
        input_output_aliases={i: i for i in range(len(arrays))},
        compiler_params=pltpu.CompilerParams(has_side_effects=DATAFLOW))(*arrays, *sem_ops, after)
    return list(res[:n_b]), list(res[n_b:])


def gather_forward(bufs, *, name):
    n = len(bufs)

    def body(*refs):
        ins, outs = refs[:n], refs[n:2 * n]
        send_sems, recv_sems = refs[2 * n:]
        x, y, c = _place()
        chips = _other_chips(x, y)
        for t in range(n):
            for k, (px, py) in enumerate(chips):
                pltpu.make_async_remote_copy(
                    src_ref=_half(ins[t], 2 * px + py, c), dst_ref=_half(outs[t], 2 * px + py, c),
                    send_sem=send_sems.at[3 * t + k], recv_sem=recv_sems.at[3 * t + k],
                    device_id=(x, y, 1 - c), device_id_type=MESH).start()
        for t in range(n):
            for k, (px, py) in enumerate(chips):
                done = pltpu.make_async_remote_copy(
                    src_ref=_half(ins[t], 2 * px + py, c), dst_ref=_half(outs[t], 2 * px + py, 1 - c),
                    send_sem=send_sems.at[3 * t + k], recv_sem=recv_sems.at[3 * t + k],
                    device_id=(x, y, 1 - c), device_id_type=MESH)
                done.wait_send()
                done.wait_recv()

    return pl.pallas_call(
        body, name=name, in_specs=[ANY] * n, out_specs=[ANY] * n,
        out_shape=[jax.ShapeDtypeStruct(a.shape, a.dtype) for a in bufs],
        input_output_aliases={i: i for i in range(n)},
        scratch_shapes=[pltpu.SemaphoreType.DMA((3 * n,)), pltpu.SemaphoreType.DMA((3 * n,))],
        compiler_params=pltpu.CompilerParams(has_side_effects=True))(*bufs)


def sum_stage_a(grad, recv, place, wire, *, name):
    j_n, half, cols = recv.shape

    def body(place_ref, g_ref, r_ref, o_ref, ob_ref):
        acc = g_ref[...] + r_ref[...]
        ob_ref[...] = acc.astype(wire)

        @pl.when(pl.program_id(0) == place_ref[0])
        def _():
            o_ref[...] = acc

    blk = (None, half, cols)
    return pl.pallas_call(
        body, name=name,
        grid_spec=pltpu.PrefetchScalarGridSpec(
            num_scalar_prefetch=1, grid=(j_n,),
            in_specs=[pl.BlockSpec(blk, lambda j, place_ref: (j, place_ref[1], 0)),
                      pl.BlockSpec(blk, lambda j, place_ref: (j, 0, 0))],
            out_specs=[pl.BlockSpec((half, cols), lambda j, place_ref: (0, 0)),
                       pl.BlockSpec(blk, lambda j, place_ref: (j, 0, 0))]),
        out_shape=[jax.ShapeDtypeStruct((half, cols), F32), jax.ShapeDtypeStruct(recv.shape, wire)],
        compiler_params=_params("arbitrary"))(place, grad, recv)


def _stage_a_copies(srcs, lands, x, y, c):
    out = []
    for src, land in zip(srcs, lands):
        half = src.shape[1] // 2
        out.append((src.at[:, pl.ds((1 - c) * half, half), :], land, (x, y, 1 - c)))
    return out


def _stage_b_copies(srcs, lands, x, y, c):
    out = []
    for src, land in zip(srcs, lands):
        for k, (px, py) in enumerate(_other_chips(x, y)):
            out.append((src.at[2 * px + py], land.at[k], (px, py, c)))
    return out


def _forward_copies(bufs, _, x, y, c):
    out = []
    for buf in bufs:
        for px, py in _other_chips(x, y):
            out.append((_half(buf, 2 * px + py, c), _half(buf, 2 * px + py, c), (x, y, 1 - c)))
    return out


def _stage_c_copies(fulls, _, x, y, c):
    out = []
    for full in fulls:
        half = full.shape[0] // 2
        mine = full.at[pl.ds(c * half, half), :]
        out.append((mine, mine, (x, y, 1 - c)))
    return out


def _chip_block_copies(bufs, _, x, y, c):
    me = 2 * x + y
    return [(buf.at[me], buf.at[me], (px, py, c)) for buf in bufs for px, py in _other_chips(x, y)]


def split_start(srcs, lands, copies, *, name):
    n, n_all = len(srcs), len(srcs) + len(lands)
    n_c = len(copies(srcs, lands, 0, 0, 0))

    def body(*refs):
        src_refs, land_refs = refs[:n], refs[n:n_all]
        send_sems, recv_sems = refs[n_all], refs[n_all + 1]
        token = refs[-1]
        x, y, c = _place()
        for k, (src, dst, target) in enumerate(copies(src_refs, land_refs, x, y, c)):
            pltpu.make_async_remote_copy(src_ref=src, dst_ref=dst, send_sem=send_sems.at[k], recv_sem=recv_sems.at[k],
                                         device_id=target, device_id_type=MESH).start()
        token[...] = jnp.zeros_like(token)

    arrays = list(srcs) + list(lands)
    res = pl.pallas_call(
        body, name=name,
        out_shape=(pltpu.SemaphoreType.DMA((n_c,)), pltpu.SemaphoreType.DMA((n_c,)))
        + tuple(pltpu.HBM(a.shape, a.dtype) for a in arrays) + (jax.ShapeDtypeStruct((8, 128), F32),),
        in_specs=[HBM] * n_all,
        out_specs=tuple([SEM, SEM] + [HBM] * n_all + [pl.BlockSpec(memory_space=pltpu.VMEM)]),
        input_output_aliases={i: 2 + i for i in range(n_all)},
        compiler_params=pltpu.CompilerParams(has_side_effects=DATAFLOW))(*[_in_hbm(a) for a in arrays])
    return res[0], res[1], list(res[2:2 + n]), list(res[2 + n:2 + n_all]), res[-1]


def split_wait(srcs, lands, send, recv, copies, after, *, name):
    n, n_all = len(srcs), len(srcs) + len(lands)

    def body(*refs):
        src_refs, land_refs = refs[:n], refs[n:n_all]
        send_sems, recv_sems = refs[n_all], refs[n_all + 1]
        x, y, c = _place()
        for k, (src, dst, target) in enumerate(copies(src_refs, land_refs, x, y, c)):
            done = pltpu.make_async_remote_copy(src_ref=src, dst_ref=dst, send_sem=send_sems.at[k],
                                                recv_sem=recv_sems.at[k], device_id=target, device_id_type=MESH)
            done.wait_send()
            done.wait_recv()

    arrays = list(srcs) + list(lands)
    res = pl.pallas_call(
        body, name=name,
        out_shape=tuple(pltpu.HBM(a.shape, a.dtype) for a in arrays),
        in_specs=[HBM] * n_all + [SEM, SEM, ANY],
        out_specs=tuple([HBM] * n_all),
        input_output_aliases={i: i for i in range(n_all)},
        compiler_params=pltpu.CompilerParams(has_side_effects=DATAFLOW))(*arrays, send, recv, after)
    return list(res[:n]), list(res[n:])


def sum_stage_b(part, recv, place, *, name):
    half, cols = part.shape

    def body(place_ref, p_ref, r_ref, o_ref):
        acc = p_ref[...]
        for k in range(3):
            acc = acc + r_ref[k].astype(F32)
        o_ref[...] = acc

    return pl.pallas_call(
        body, name=name,
        grid_spec=pltpu.PrefetchScalarGridSpec(
            num_scalar_prefetch=1, grid=(1,),
            in_specs=[pl.BlockSpec((half, cols), lambda i, place_ref: (0, 0)),
                      pl.BlockSpec((3, half, cols), lambda i, place_ref: (0, 0, 0))],
            out_specs=pl.BlockSpec((half, cols), lambda i, place_ref: (place_ref[1], 0))),
        out_shape=jax.ShapeDtypeStruct((2 * half, cols), F32),
        compiler_params=_params("arbitrary"))(place, part, recv)


def reduce_stage_c(fulls, *, name):
    n = len(fulls)

    def body(*refs):
        ins, outs = refs[:n], refs[n:2 * n]
        send_sems, recv_sems = refs[2 * n:]
        x, y, c = _place()
        for t in range(n):
            half = ins[t].shape[0] // 2
            pltpu.make_async_remote_copy(
                src_ref=ins[t].at[pl.ds(c * half, half), :], dst_ref=outs[t].at[pl.ds(c * half, half), :],
                send_sem=send_sems.at[t], recv_sem=recv_sems.at[t],
                device_id=(x, y, 1 - c), device_id_type=MESH).start()
        for t in range(n):
            half = ins[t].shape[0] // 2
            done = pltpu.make_async_remote_copy(
                src_ref=ins[t].at[pl.ds(c * half, half), :], dst_ref=outs[t].at[pl.ds((1 - c) * half, half), :],
                send_sem=send_sems.at[t], recv_sem=recv_sems.at[t],
                device_id=(x, y, 1 - c), device_id_type=MESH)
            done.wait_send()
            done.wait_recv()

    return pl.pallas_call(
        body, name=name, in_specs=[ANY] * n, out_specs=[ANY] * n,
        out_shape=[jax.ShapeDtypeStruct(a.shape, a.dtype) for a in fulls],
        input_output_aliases={i: i for i in range(n)},
        scratch_shapes=[pltpu.SemaphoreType.DMA((n,)), pltpu.SemaphoreType.DMA((n,))],
        compiler_params=pltpu.CompilerParams(has_side_effects=True))(*fulls)


def gather_chip_blocks(slots, *, name):
    def body(in_ref, out_ref, send_sems, recv_sems):
        x, y, c = _place()
        me = 2 * x + y
        chips = _other_chips(x, y)
        for k, (px, py) in enumerate(chips):
            pltpu.make_async_remote_copy(
                src_ref=in_ref.at[me], dst_ref=out_ref.at[me],
                send_sem=send_sems.at[k], recv_sem=recv_sems.at[k],
                device_id=(px, py, c), device_id_type=MESH).start()
        for k, (px, py) in enumerate(chips):
            done = pltpu.make_async_remote_copy(
                src_ref=in_ref.at[me], dst_ref=out_ref.at[2 * px + py],
                send_sem=send_sems.at[k], recv_sem=recv_sems.at[k],
                device_id=(px, py, c), device_id_type=MESH)
            done.wait_send()
            done.wait_recv()

    return pl.pallas_call(
        body, name=name, in_specs=[ANY], out_specs=ANY,
        out_shape=jax.ShapeDtypeStruct(slots.shape, slots.dtype),
        input_output_aliases={0: 0},
        scratch_shapes=[pltpu.SemaphoreType.DMA((3,)), pltpu.SemaphoreType.DMA((3,))],
        compiler_params=pltpu.CompilerParams(has_side_effects=True))(slots)


def _ffn_bwd(dh, dh_bf, h, gain, saved, wg, wu, wd, cw, cb, place, l, pin):
    hn, rstd, g, up, gc = saved
    dg, dup, dwg, dwu, dwd, dcw, dcb = ffn_bwd_a(dh_bf, hn, g, up, gc, wd, cw, name=f"ffn{l}_bwd_a", pin=pin)
    red = _reduce_a_begin([(dwg, BF16), (dwu, BF16), (dwd, BF16)], tag=f"f{l}")
    dh_in, dh_in_bf, dgain = dx_rms_bwd([(dg, wg), (dup, wu)], h, gain, rstd, dh, name=f"ffn{l}_bwd_b",
                                        pin=red[-1])
    red = _reduce_b_begin(red, place, dh_in_bf, tag=f"f{l}")
    return dh_in, dh_in_bf, (dcw, dcb, dgain), red


def _reduce_a_begin(grads, *, tag):
    lands = [lax.empty((g.shape[0], g.shape[1] // 2, g.shape[2]), F32) for g, _ in grads]
    send, recv, srcs, lands, token = split_start([g for g, _ in grads], lands, _stage_a_copies,
                                                 name=f"reduce_a_start_{tag}")
    return [w for _, w in grads], send, recv, srcs, lands, token


def _reduce_b_begin(state, place, after, *, tag):
    wires, send, recv, srcs, lands, _ = state
    grads, recv_a = split_wait(srcs, lands, send, recv, _stage_a_copies, after, name=f"reduce_a_wait_{tag}")
    parts = [sum_stage_a(g, r, place, w, name=f"sum_a_{tag}{i}") for i, (g, r, w) in enumerate(zip(grads, recv_a, wires))]
    lands_b = [lax.empty((3,) + p[1].shape[1:], p[1].dtype) for p in parts]
    send, recv, srcs, lands, token = split_start([p[1] for p in parts], lands_b, _stage_b_copies,
                                                 name=f"reduce_b_start_{tag}")
    return [p[0] for p in parts], send, recv, srcs, lands, token


def _reduce_c_begin(state, place, after, *, tag):
    parts, send, recv, srcs, lands, _ = state
    _, recv_b = split_wait(srcs, lands, send, recv, _stage_b_copies, after, name=f"reduce_b_wait_{tag}")
    halves = [sum_stage_b(p, r, place, name=f"sum_b_{tag}{i}") for i, (p, r) in enumerate(zip(parts, recv_b))]
    send, recv, fulls, _, token = split_start(halves, [], _stage_c_copies, name=f"reduce_c_start_{tag}")
    return send, recv, fulls, token


def _reduce_finish(state, after, *, tag):
    send, recv, fulls, _ = state
    fulls, _ = split_wait(fulls, [], send, recv, _stage_c_copies, after, name=f"reduce_c_wait_{tag}")
    return fulls


def kernel(x, norm_mix, norm_ffn, final_norm, w_in_even, conv_a, w_pool, pool_scale, w_out_even, w_in_odd, sgu_norm, w_spatial, b_spatial, w_out_odd, w_ffn_gate, w_ffn_up, conv_ffn, b_conv_ffn, w_ffn_down, loss_target, m_norm_mix, m_norm_ffn, m_final_norm, m_w_in_even, m_conv_a, m_w_pool, m_pool_scale, m_w_out_even, m_w_in_odd, m_sgu_norm, m_w_spatial, m_b_spatial, m_w_out_odd, m_w_ffn_gate, m_w_ffn_up, m_conv_ffn, m_b_conv_ffn, m_w_ffn_down, v_norm_mix, v_norm_ffn, v_final_norm, v_w_in_even, v_conv_a, v_w_pool, v_pool_scale, v_w_out_even, v_w_in_odd, v_sgu_norm, v_w_spatial, v_b_spatial, v_w_out_odd, v_w_ffn_gate, v_w_ffn_up, v_conv_ffn, v_b_conv_ffn, v_w_ffn_down):
    weights = dict(norm_mix=norm_mix, norm_ffn=norm_ffn, final_norm=final_norm, w_in_even=w_in_even,
                   conv_a=conv_a, w_pool=w_pool, pool_scale=pool_scale, w_out_even=w_out_even,
                   w_in_odd=w_in_odd, sgu_norm=sgu_norm, w_spatial=w_spatial, b_spatial=b_spatial,
                   w_out_odd=w_out_odd, w_ffn_gate=w_ffn_gate, w_ffn_up=w_ffn_up, conv_ffn=conv_ffn,
                   b_conv_ffn=b_conv_ffn, w_ffn_down=w_ffn_down)
    m_in = dict(norm_mix=m_norm_mix, norm_ffn=m_norm_ffn, final_norm=m_final_norm, w_in_even=m_w_in_even,
                conv_a=m_conv_a, w_pool=m_w_pool, pool_scale=m_pool_scale, w_out_even=m_w_out_even,
                w_in_odd=m_w_in_odd, sgu_norm=m_sgu_norm, w_spatial=m_w_spatial, b_spatial=m_b_spatial,
                w_out_odd=m_w_out_odd, w_ffn_gate=m_w_ffn_gate, w_ffn_up=m_w_ffn_up, conv_ffn=m_conv_ffn,
                b_conv_ffn=m_b_conv_ffn, w_ffn_down=m_w_ffn_down)
    v_in = dict(norm_mix=v_norm_mix, norm_ffn=v_norm_ffn, final_norm=v_final_norm, w_in_even=v_w_in_even,
                conv_a=v_conv_a, w_pool=v_w_pool, pool_scale=v_pool_scale, w_out_even=v_w_out_even,
                w_in_odd=v_w_in_odd, sgu_norm=v_sgu_norm, w_spatial=v_w_spatial, b_spatial=v_b_spatial,
                w_out_odd=v_w_out_odd, w_ffn_gate=v_w_ffn_gate, w_ffn_up=v_w_ffn_up, conv_ffn=v_conv_ffn,
                b_conv_ffn=v_b_conv_ffn, w_ffn_down=v_w_ffn_down)
    order = list(weights)

    chip = 2 * lax.axis_index("x") + lax.axis_index("y")
    core = lax.axis_index("c")
    place = jnp.stack([chip, core]).astype(jnp.int32)
    chip_arr = place[:1]

    h0 = x[0]
    target = loss_target[0]
    d_model = h0.shape[1]
    f_shard = w_ffn_gate.shape[-1]

    def turned(a):
        return jnp.transpose(a, (0, 2, 1))

    def own_slot(v):
        return lax.dynamic_update_index_in_dim(jnp.zeros((N_CHIPS,) + v.shape, v.dtype), v, chip, 0)

    groups = [
        [cast_into_slot(w_in_even[0], chip_arr, name="cast_win_e"),
         cast_into_slot(w_out_even[0], chip_arr, name="cast_wout_e")],
        [cast_into_slot(turned(w_ffn_gate), chip_arr, l=0, name="cast_wg0"),
         cast_into_slot(turned(w_ffn_up), chip_arr, l=0, name="cast_wu0")],
        [cast_into_slot(w_ffn_down, chip_arr, l=0, name="cast_wd0")],
        [cast_into_slot(w_in_odd[0], chip_arr, name="cast_win_o"),
         cast_into_slot(w_out_odd[0], chip_arr, name="cast_wout_o")],
        [cast_into_slot(turned(w_ffn_gate), chip_arr, l=1, name="cast_wg1"),
         cast_into_slot(turned(w_ffn_up), chip_arr, l=1, name="cast_wu1")],
        [cast_into_slot(w_ffn_down, chip_arr, l=1, name="cast_wd1")]]
    smalls = [own_slot(conv_a[0]), own_slot(sgu_norm), own_slot(conv_ffn[0]), own_slot(conv_ffn[1])]
    sems, groups, smalls, token = gather_start(groups, smalls)

    def arrive(gi, after, with_smalls=False):
        kw = dict(smalls=smalls, small_send=sems[-2], small_recv=sems[-1]) if with_smalls else {}
        bufs, small_out = gather_wait(groups[gi], sems[2 * gi], sems[2 * gi + 1], after, name=f"gather_wait{gi}", **kw)
        return gather_forward(bufs, name=f"gather_forward{gi}"), small_out

    def arrive_begin(gi, after):
        bufs, _ = gather_wait(groups[gi], sems[2 * gi], sems[2 * gi + 1], after, name=f"gather_wait{gi}")
        send, recv, bufs, _, tok = split_start(bufs, [], _forward_copies, name=f"gather_forward_start{gi}")
        return (send, recv, bufs), tok

    def arrive_end(state, after, gi):
        send, recv, bufs = state
        return split_wait(bufs, [], send, recv, _forward_copies, after, name=f"gather_forward_wait{gi}")[0]

    cb = b_conv_ffn.reshape(-1, N_CHIPS, 1, f_shard)
    wp_bf = w_pool[0].astype(BF16)
    wp_t_bf = jnp.transpose(w_pool[0], (0, 2, 1)).astype(BF16)
    ws = w_spatial[0]
    bs = b_spatial[0][:, :, None]

    (win_e, wout_e), (ca_g, sn_g, cw0, cw1) = arrive(0, token, with_smalls=True)
    wout_e = wout_e.reshape(-1, d_model)
    ca_full = jnp.transpose(ca_g, (1, 0, 2)).reshape(ca_g.shape[1], -1)
    sn_full = sn_g.reshape(1, -1)
    h1, xn0, rstd0, proj0, mix0 = even_layer_fwd(h0, norm_mix[0:1], win_e, ca_full, wp_bf, pool_scale, wout_e,
                                                 name="l0_fwd")
    hn0, rstdf0 = rms_fwd(h1, norm_ffn[0:1], name="ffn0_rms")
    (wg0, wu0), _ = arrive(1, hn0)
    swap2, tok2 = arrive_begin(2, wg0)
    g0, up0, gc0, act0 = ffn_in_fwd(hn0, wg0, wu0, cw0, cb[0], name="ffn0_in", pin=tok2)
    swap3, tok3 = arrive_begin(3, act0)
    (wd0,) = arrive_end(swap2, tok3, 2)
    h2 = mm_acc(act0, wd0, h1, name="ffn0_down")
    ffn0 = (hn0, rstdf0, g0, up0, gc0)
    swap4, tok4 = arrive_begin(4, h2)
    swap5, tok5 = arrive_begin(5, tok4)
    win_o, wout_o = arrive_end(swap3, tok5, 3)
    wout_o = wout_o.reshape(-1, d_model)
    h3, xn1, rstd1, p1, mix1, rstd_v = odd_layer_fwd(h2, norm_mix[1:2], win_o, sn_full, ws, bs, wout_o, name="l1_fwd")
    wg1, wu1 = arrive_end(swap4, h3, 4)
    (wd1,) = arrive_end(swap5, wg1, 5)
    h4, hn1, rstdf1, g1, up1, gc1 = ffn_fwd(h3, norm_ffn[1:2], wg1, wu1, wd1, cw1, cb[1], name="ffn1_fwd")
    ffn1 = (hn1, rstdf1, g1, up1, gc1)

    dh4, dh4_bf, loss_row, d_final = final_loss(h4, target, final_norm[None], name="loss")
    loss = lax.psum(loss_row[0, 0], ("x", "y", "c"))

    dh3, dh3_bf, (dcw1, dcb1, dnf1), red3 = _ffn_bwd(
        dh4, dh4_bf, h3, norm_ffn[1:2], ffn1, wg1, wu1, wd1, cw1, cb[1], place, 1, None)

    def as_blocks(a):
        return a.reshape(N_CHIPS, -1, d_model)

    dp1, dsn, dws, dbs = sgu_bwd(p1, dh3_bf, wout_o, rstd_v, sn_full, ws, bs, name="l1_mix_bwd", pin=red3[-1])
    dwout_o = as_blocks(mm_tn(mix1[None], dh3_bf[None], name="l1_dwout"))
    dwin_o = mm_tn_shared(xn1, dp1, name="l1_dwin")
    red2 = _reduce_a_begin([(dwin_o, BF16), (dwout_o, BF16)], tag="m1")
    dh2, dh2_bf, dnm1 = dx_rms_bwd([(dp1, win_o)], h2, norm_mix[1:2], rstd1, dh3, name="l1_dx", pin=red2[-1])
    red2 = _reduce_b_begin(red2, place, dh2_bf, tag="m1")

    dh1, dh1_bf, (dcw0, dcb0, dnf0), red1 = _ffn_bwd(
        dh2, dh2_bf, h1, norm_ffn[0:1], ffn0, wg0, wu0, wd0, cw0, cb[0], place, 0, red2[-1])

    dproj0, dca, dwp, dps = even_bwd(proj0, dh1_bf, wout_e, ca_full, wp_bf, wp_t_bf, pool_scale,
                                     name="l0_mix_bwd", pin=red1[-1])
    dwout_e = as_blocks(mm_tn(mix0[None], dh1_bf[None], name="l0_dwout"))
    dwin_e = mm_tn_shared(xn0, dproj0, name="l0_dwin")
    dh0, _, dnm0 = dx_rms_bwd([(dproj0, win_e)], h0, norm_mix[0:1], rstd0, dh1, name="l0_dx")
    grad_x = dh0[None]

    small_parts = dict(
        norm_mix=jnp.concatenate([dnm0, dnm1]), norm_ffn=jnp.concatenate([dnf0, dnf1]), final_norm=d_final,
        conv_a=dca, w_pool=dwp, pool_scale=dps, sgu_norm=dsn, w_spatial=dws, b_spatial=dbs,
        conv_ffn=jnp.stack([dcw0, dcw1]), b_conv_ffn=jnp.stack([dcb0, dcb1]))
    flat = jnp.concatenate([v.reshape(-1) for v in small_parts.values()])
    pad = (-flat.shape[0]) % (N_CHIPS * 32 * 128)
    small = jnp.pad(flat, (0, pad)).reshape(N_CHIPS, -1, 128)
    red0 = _reduce_a_begin([(dwin_e, BF16), (dwout_e, BF16), (small, F32)], tag="m0")

    swap_f1 = _reduce_c_begin(red3, place, red0[-1], tag="f1")
    red0 = _reduce_b_begin(red0, place, swap_f1[-1], tag="m0")
    swap_m1 = _reduce_c_begin(red2, place, red0[-1], tag="m1")
    swap_f0 = _reduce_c_begin(red1, place, swap_m1[-1], tag="f0")
    full3 = _reduce_finish(swap_f1, swap_f0[-1], tag="f1")
    full2 = _reduce_finish(swap_m1, full3[0], tag="m1")
    full1 = _reduce_finish(swap_f0, full2[0], tag="f0")
    swap_m0 = _reduce_c_begin(red0, place, full1[0], tag="m0")
    full0 = _reduce_finish(swap_m0, swap_m0[-1], tag="m0")
    small_slots = lax.dynamic_update_index_in_dim(jnp.zeros(small.shape, F32), full0[2], chip, 0)
    small_sum = gather_chip_blocks(small_slots, name="gather_small").reshape(-1)
    grads = {
        "w_in_even": full0[0][None], "w_out_even": full0[1][None],
        "w_in_odd": full2[0][None], "w_out_odd": full2[1][None],
        }
    layered = {"w_ffn_gate": [full1[0], full3[0]], "w_ffn_up": [full1[1], full3[1]],
               "w_ffn_down": [full1[2], full3[2]]}
    off = 0
    small_red = {}
    for nm, v in small_parts.items():
        small_red[nm] = small_sum[off:off + v.size].reshape(v.shape)
        off += v.size
    for nm in ("norm_mix", "norm_ffn", "pool_scale"):
        grads[nm] = small_red[nm].reshape(weights[nm].shape)
    grads["final_norm"] = small_red["final_norm"].reshape(weights["final_norm"].shape)
    grads["w_pool"] = small_red["w_pool"][None]
    grads["w_spatial"] = small_red["w_spatial"][None]
    grads["b_spatial"] = small_red["b_spatial"].reshape(weights["b_spatial"].shape)
    grads["b_conv_ffn"] = small_red["b_conv_ffn"].reshape(weights["b_conv_ffn"].shape)
    grads["conv_a"] = lax.dynamic_slice_in_dim(small_red["conv_a"], chip * conv_a.shape[-1], conv_a.shape[-1], 1)[None]
    grads["sgu_norm"] = lax.dynamic_slice_in_dim(small_red["sgu_norm"], chip * sgu_norm.shape[-1], sgu_norm.shape[-1], 1)
    grads["conv_ffn"] = lax.dynamic_index_in_dim(small_red["conv_ffn"], chip, 1, keepdims=False)

    deltas, new_m, new_v = {}, {}, {}
    for nm, per_layer in layered.items():
        if nm == "w_ffn_down":
            grads[nm], deltas[nm], new_m[nm], new_v[nm] = adamw_layers(
                weights[nm], per_layer, m_in[nm], v_in[nm], name=f"adamw_{nm}")
        else:
            outs = adamw_layers(turned(weights[nm]), per_layer, turned(m_in[nm]), turned(v_in[nm]),
                                name=f"adamw_{nm}")
            grads[nm], deltas[nm], new_m[nm], new_v[nm] = (turned(o) for o in outs)
    for nm in order:
        if nm in layered:
            continue
        w = weights[nm]
        w2 = w[None] if w.ndim == 1 else w
        shp = w2.shape
        d, nm_, nv_ = adamw(w2, grads[nm].reshape(shp), m_in[nm].reshape(shp), v_in[nm].reshape(shp),
                            name=f"adamw_{nm}")
        deltas[nm], new_m[nm], new_v[nm] = d.reshape(w.shape), nm_.reshape(w.shape), nv_.reshape(w.shape)

    return (loss, grad_x, *[grads[n] for n in order], *[deltas[n] for n in order],
            *[new_m[n] for n in order], *[new_v[n] for n in order])
```

```python
import jax
import jax.numpy as jnp
from jax import lax
from jax.experimental import pallas as pl
from jax.experimental.pallas import tpu as pltpu

F32 = jnp.float32
BF16 = jnp.bfloat16
MESH = pl.DeviceIdType.MESH
ANY = pl.BlockSpec(memory_space=pl.ANY)

EPS = 1e-6
POOL_WINDOWS = (2, 4, 8, 16)
GROUP = 128
CHUNK = 128
N_CHIPS = 4
N_DEV = 8
ROW_TILE = 512
HALO = 16
VMEM_LIMIT = 56 * 1024 * 1024

ADAM_LR = 0.001
ADAM_B1 = 0.9
ADAM_B2 = 0.999
ADAM_EPS = 1e-08
ADAM_WD = 0.01
ADAM_STEP = 10


def _params(*sem):
    return pltpu.CompilerParams(dimension_semantics=sem, vmem_limit_bytes=VMEM_LIMIT)


def _layer_spec(block, l, idx):
    if l is None:
        return pl.BlockSpec(block, idx)
    return pl.BlockSpec((None,) + block, lambda *g: (l,) + idx(*g))


def mm_nn(a, b, *, l=None, name):
    s, k = a.shape
    j_n, n = b.shape[-3], b.shape[-1]
    tm = min(ROW_TILE, s)

    def body(a_ref, b_ref, o_ref):
        o_ref[...] = jnp.dot(a_ref[...], b_ref[...], preferred_element_type=F32)

    return pl.pallas_call(
        body, name=name, grid=(j_n, s // tm),
        in_specs=[pl.BlockSpec((tm, k), lambda j, i: (i, 0)),
                  _layer_spec((None, k, n), l, lambda j, i: (j, 0, 0))],
        out_specs=pl.BlockSpec((None, tm, n), lambda j, i: (j, i, 0)),
        out_shape=jax.ShapeDtypeStruct((j_n, s, n), F32),
        compiler_params=_params("parallel", "parallel"))(a, b)


def mm_acc(a, b, res, *, l=None, name):
    j_n, s, kj = a.shape
    n = b.shape[-1]
    tm = min(ROW_TILE, s)

    def body(a_ref, b_ref, r_ref, o_ref):
        acc = r_ref[...]
        for j in range(j_n):
            acc = acc + jnp.dot(a_ref[j], b_ref[j], preferred_element_type=F32)
        o_ref[...] = acc

    return pl.pallas_call(
        body, name=name, grid=(s // tm,),
        in_specs=[pl.BlockSpec((j_n, tm, kj), lambda i: (0, i, 0)),
                  _layer_spec((j_n, kj, n), l, lambda i: (0, 0, 0)),
                  pl.BlockSpec((tm, n), lambda i: (i, 0))],
        out_specs=pl.BlockSpec((tm, n), lambda i: (i, 0)),
        out_shape=jax.ShapeDtypeStruct((s, n), F32),
        compiler_params=_params("parallel"))(a, b, res)


_NT = (((1,), (1,)), ((), ()))
_TN = (((0,), (0,)), ((), ()))


def mm_nt_sum(pairs, *, l=None, name):
    j_n, s, nj = pairs[0][0].shape
    k = pairs[0][1].shape[-2]
    tm = min(ROW_TILE, s)
    n_p = len(pairs)

    def body(*refs):
        o_ref = refs[-1]
        acc = jnp.zeros((tm, k), F32)
        for p in range(n_p):
            dy_ref, w_ref = refs[2 * p], refs[2 * p + 1]
            for j in range(j_n):
                acc = acc + lax.dot_general(dy_ref[j], w_ref[j], _NT, preferred_element_type=F32)
        o_ref[...] = acc

    in_specs, args = [], []
    for dy, w in pairs:
        in_specs += [pl.BlockSpec((j_n, tm, nj), lambda i: (0, i, 0)),
                     _layer_spec((j_n, k, nj), l, lambda i: (0, 0, 0))]
        args += [dy, w]
    return pl.pallas_call(
        body, name=name, grid=(s // tm,), in_specs=in_specs,
        out_specs=pl.BlockSpec((tm, k), lambda i: (i, 0)),
        out_shape=jax.ShapeDtypeStruct((s, k), F32),
        compiler_params=_params("parallel"))(*args)


def mm_nt_each(a, b, *, l=None, name, pin=None):
    s, k = a.shape
    j_n, nj = b.shape[-3], b.shape[-2]
    tm = min(ROW_TILE, s)

    def body(a_ref, b_ref, *rest):
        rest[-1][...] = lax.dot_general(a_ref[...], b_ref[...], _NT, preferred_element_type=F32)

    return pl.pallas_call(
        body, name=name, grid=(j_n, s // tm),
        in_specs=[pl.BlockSpec((tm, k), lambda j, i: (i, 0)),
                  _layer_spec((None, nj, k), l, lambda j, i: (j, 0, 0))] + ([ANY] if pin is not None else []),
        out_specs=pl.BlockSpec((None, tm, nj), lambda j, i: (j, i, 0)),
        out_shape=jax.ShapeDtypeStruct((j_n, s, nj), F32),
        compiler_params=_params("parallel", "parallel"))(a, b, *([pin] if pin is not None else []))


def mm_tn(a, dy, *, name, pin=None):
    ja, s, k = a.shape
    jd, _, n = dy.shape
    j_n = max(ja, jd)
    tk = min(ROW_TILE, s)

    def body(a_ref, d_ref, *rest):
        o_ref = rest[-1]

        @pl.when(pl.program_id(1) == 0)
        def _():
            o_ref[...] = jnp.zeros_like(o_ref)
        o_ref[...] += lax.dot_general(a_ref[...], d_ref[...], _TN, preferred_element_type=F32)

    return pl.pallas_call(
        body, name=name, grid=(j_n, s // tk),
        in_specs=[pl.BlockSpec((None, tk, k), (lambda j, i: (j, i, 0)) if ja > 1 else (lambda j, i: (0, i, 0))),
                  pl.BlockSpec((None, tk, n), (lambda j, i: (j, i, 0)) if jd > 1 else (lambda j, i: (0, i, 0)))]
        + ([ANY] if pin is not None else []),
        out_specs=pl.BlockSpec((None, k, n), lambda j, i: (j, 0, 0)),
        out_shape=jax.ShapeDtypeStruct((j_n, k, n), F32),
        compiler_params=_params("parallel", "arbitrary"))(a, dy, *([pin] if pin is not None else []))


def mm_tn_shared(a, dy, *, name):
    s, k = a.shape
    j_n, _, n = dy.shape
    tk = min(ROW_TILE, s)

    def body(a_ref, d_ref, o_ref):
        @pl.when(pl.program_id(0) == 0)
        def _():
            o_ref[...] = jnp.zeros_like(o_ref)
        a_t = a_ref[...]
        for j in range(j_n):
            o_ref[j] += lax.dot_general(a_t, d_ref[j], _TN, preferred_element_type=F32)

    return pl.pallas_call(
        body, name=name, grid=(s // tk,),
        in_specs=[pl.BlockSpec((tk, k), lambda i: (i, 0)), pl.BlockSpec((j_n, tk, n), lambda i: (0, i, 0))],
        out_specs=pl.BlockSpec((j_n, k, n), lambda i: (0, 0, 0)),
        out_shape=jax.ShapeDtypeStruct((j_n, k, n), F32),
        compiler_params=_params("arbitrary"))(a, dy)


def _back(x, k):
    return pltpu.roll(x, k, 0)


def _fwd(x, k):
    return pltpu.roll(x, x.shape[0] - k, 0)


def _causal_conv(x, w_ref):
    return w_ref[0:1, :] * _back(x, 2) + w_ref[1:2, :] * _back(x, 1) + w_ref[2:3, :] * x


def _causal_conv_t(dy, w_ref):
    return w_ref[2:3, :] * dy + w_ref[1:2, :] * _fwd(dy, 1) + w_ref[0:1, :] * _fwd(dy, 2)


def _gelu(x):
    return 0.5 * x * (1.0 + lax.erf(x * 0.7071067811865476))


def _gelu_grad(x):
    return 0.5 * (1.0 + lax.erf(x * 0.7071067811865476)) + x * jnp.exp(-0.5 * x * x) * 0.3989422804014327


def _colsum(x):
    return jnp.sum(x, axis=0, keepdims=True)


def rms_fwd(h, gain, *, name, pin=None):
    s, d = h.shape
    ts = min(ROW_TILE, s)

    def body(h_ref, g_ref, *rest):
        o_ref, r_ref = rest[-2:]
        x = h_ref[...]
        rstd = lax.rsqrt(jnp.mean(x * x, axis=-1, keepdims=True) + EPS)
        o_ref[...] = (x * rstd * g_ref[...]).astype(BF16)
        r_ref[...] = rstd

    return pl.pallas_call(
        body, name=name, grid=(s // ts,),
        in_specs=[pl.BlockSpec((ts, d), lambda i: (i, 0)), pl.BlockSpec((1, d), lambda i: (0, 0))]
        + ([ANY] if pin is not None else []),
        out_specs=[pl.BlockSpec((ts, d), lambda i: (i, 0)), pl.BlockSpec((ts, 1), lambda i: (i, 0))],
        out_shape=[jax.ShapeDtypeStruct((s, d), BF16), jax.ShapeDtypeStruct((s, 1), F32)],
        compiler_params=_params("parallel"))(h, gain, *([pin] if pin is not None else []))


def rms_bwd(dxn, h, gain, rstd, dres, *, name):
    s, d = h.shape
    ts = min(ROW_TILE, s)

    def body(dx_ref, h_ref, g_ref, r_ref, dr_ref, o_ref, ob_ref, dg_ref):
        @pl.when(pl.program_id(0) == 0)
        def _():
            dg_ref[...] = jnp.zeros_like(dg_ref)
        rstd_v = r_ref[...]
        hhat = h_ref[...] * rstd_v
        dx = dx_ref[...]
        dg_ref[...] += _colsum(dx * hhat)
        dxg = dx * g_ref[...]
        dh = dr_ref[...] + rstd_v * (dxg - hhat * jnp.mean(dxg * hhat, axis=-1, keepdims=True))
        o_ref[...] = dh
        ob_ref[...] = dh.astype(BF16)

    row = pl.BlockSpec((ts, d), lambda i: (i, 0))
    vec = pl.BlockSpec((1, d), lambda i: (0, 0))
    return pl.pallas_call(
        body, name=name, grid=(s // ts,),
        in_specs=[row, row, vec, pl.BlockSpec((ts, 1), lambda i: (i, 0)), row],
        out_specs=[row, row, vec],
        out_shape=[jax.ShapeDtypeStruct((s, d), F32), jax.ShapeDtypeStruct((s, d), BF16),
                   jax.ShapeDtypeStruct((1, d), F32)],
        compiler_params=_params("arbitrary"))(dxn, h, gain, rstd, dres)


def final_loss(h, target, gain, *, name):
    s, d = h.shape
    ts = min(ROW_TILE, s)

    def body(h_ref, t_ref, g_ref, o_ref, ob_ref, l_ref, dg_ref):
        @pl.when(pl.program_id(0) == 0)
        def _():
            l_ref[...] = jnp.zeros_like(l_ref)
            dg_ref[...] = jnp.zeros_like(dg_ref)
        x = h_ref[...]
        rstd = lax.rsqrt(jnp.mean(x * x, axis=-1, keepdims=True) + EPS)
        hhat = x * rstd
        err = hhat * g_ref[...] - t_ref[...]
        l_ref[...] += 0.5 * jnp.sum(jnp.mean(err * err, axis=-1, keepdims=True), axis=0, keepdims=True)
        dy = err * (1.0 / d)
        dg_ref[...] += _colsum(dy * hhat)
        dyg = dy * g_ref[...]
        dh = rstd * (dyg - hhat * jnp.mean(dyg * hhat, axis=-1, keepdims=True))
        o_ref[...] = dh
        ob_ref[...] = dh.astype(BF16)

    row = pl.BlockSpec((ts, d), lambda i: (i, 0))
    vec = pl.BlockSpec((1, d), lambda i: (0, 0))
    return pl.pallas_call(
        body, name=name, grid=(s // ts,),
        in_specs=[row, row, vec],
        out_specs=[row, row, pl.BlockSpec((1, 128), lambda i: (0, 0)), vec],
        out_shape=[jax.ShapeDtypeStruct((s, d), F32), jax.ShapeDtypeStruct((s, d), BF16),
                   jax.ShapeDtypeStruct((1, 128), F32), jax.ShapeDtypeStruct((1, d), F32)],
        compiler_params=_params("arbitrary"))(h, target, gain)


def _halo_specs(n_lead, ts, width, n_tiles):
    hb = ts // HALO
    prev = pl.BlockSpec((n_lead, HALO, width), lambda i: (0, jnp.maximum(i * hb - 1, 0), 0))
    nxt = pl.BlockSpec((n_lead, HALO, width), lambda i: (0, jnp.minimum((i + 1) * hb, n_tiles * hb - 1), 0))
    return prev, nxt


def _pool_fwd(z_ext, g, pos):
    w = POOL_WINDOWS[g]
    zg = z_ext[:, g * GROUP:(g + 1) * GROUP]
    acc = zg
    sh = 1
    while sh < w:
        acc = acc + _back(acc, sh)
        sh *= 2
    return acc[HALO:] / jnp.minimum(pos, float(w)) - zg[HALO:]


def even_layer_fwd(h, gain, win, conv_a, w_pool, pool_scale, wout, gain_next, *, name, pin=None):
    s, d = h.shape
    w = win.shape[-1]
    ts = min(ROW_TILE, s)
    hb = ts // HALO

    def body(h_ref, hp_ref, gain_ref, win_ref, ca_ref, wp_ref, ps_ref, wout_ref, gn_ref, *rest):
        o_ref, xn_ref, r_ref, p_ref, m_ref, hn_ref, rn_ref = rest[-7:]
        i = pl.program_id(0)
        keep = jnp.where(i > 0, 1.0, 0.0)
        h_ext = jnp.concatenate([hp_ref[...], h_ref[...]], axis=0)
        rstd = lax.rsqrt(jnp.mean(h_ext * h_ext, axis=-1, keepdims=True) + EPS)
        xn_ext = (h_ext * rstd * gain_ref[...]).astype(BF16)
        xn_ref[...] = xn_ext[HALO:]
        r_ref[...] = rstd[HALO:]
        p32 = []
        for k in range(4):
            pk = jnp.dot(xn_ext, win_ref[k], preferred_element_type=F32).astype(BF16)
            p_ref[k] = pk[HALO:]
            pk = pk.astype(F32)
            p32.append(jnp.concatenate([pk[:HALO] * keep, pk[HALO:]], axis=0))
        m_ref[:, 0:w] = (p32[0][HALO:] * _causal_conv(p32[1] * p32[2], ca_ref)[HALO:]).astype(BF16)
        pos = (i * ts + lax.broadcasted_iota(jnp.int32, (ts, 1), 0) + 1).astype(F32)
        for g in range(len(POOL_WINDOWS)):
            pooled = _pool_fwd(p32[3], g, pos)
            mixed = jnp.dot(pooled.astype(BF16), wp_ref[g], preferred_element_type=F32)
            cols = slice(g * GROUP, (g + 1) * GROUP)
            m_ref[:, w + g * GROUP:w + (g + 1) * GROUP] = (mixed * ps_ref[:, cols]).astype(BF16)
        out = h_ref[...] + jnp.dot(m_ref[...], wout_ref[...], preferred_element_type=F32)
        o_ref[...] = out
        rstd_n = lax.rsqrt(jnp.mean(out * out, axis=-1, keepdims=True) + EPS)
        hn_ref[...] = (out * rstd_n * gn_ref[...]).astype(BF16)
        rn_ref[...] = rstd_n

    const = lambda shape: pl.BlockSpec(shape, lambda i: (0,) * len(shape))
    row = pl.BlockSpec((ts, d), lambda i: (i, 0))
    col1 = pl.BlockSpec((ts, 1), lambda i: (i, 0))
    return pl.pallas_call(
        body, name=name, grid=(s // ts,),
        in_specs=[row, pl.BlockSpec((HALO, d), lambda i: (jnp.maximum(i * hb - 1, 0), 0)), const((1, d)),
                  const((4, d, w)), const((3, w)), const((4, GROUP, GROUP)), const((1, w)), const((2 * w, d)),
                  const((1, d))] + ([ANY] if pin is not None else []),
        out_specs=[row, row, col1, pl.BlockSpec((4, ts, w), lambda i: (0, i, 0)),
                   pl.BlockSpec((ts, 2 * w), lambda i: (i, 0)), row, col1],
        out_shape=[jax.ShapeDtypeStruct((s, d), F32), jax.ShapeDtypeStruct((s, d), BF16),
                   jax.ShapeDtypeStruct((s, 1), F32), jax.ShapeDtypeStruct((4, s, w), BF16),
                   jax.ShapeDtypeStruct((s, 2 * w), BF16), jax.ShapeDtypeStruct((s, d), BF16),
                   jax.ShapeDtypeStruct((s, 1), F32)],
        compiler_params=_params("parallel"))(
            h, h, gain, win, conv_a, w_pool, pool_scale, wout, gain_next, *([pin] if pin is not None else []))


def even_bwd(proj, dh_bf, wout, conv_a, w_pool, w_pool_t, pool_scale, *, name, pin=None):
    _, s, w = proj.shape
    d = dh_bf.shape[1]
    ts = min(ROW_TILE, s)
    n_t = s // ts
    prev, nxt = _halo_specs(4, ts, w, n_t)
    hb = ts // HALO
    n_ext = ts + HALO

    def body(p_ref, pp_ref, pn_ref, dh_ref, dhn_ref, wout_ref, ca_ref, wp_ref, wpt_ref, ps_ref, *rest):
        dp_ref, dca_ref, dwp_ref, dps_ref = rest[-4:]
        i = pl.program_id(0)

        @pl.when(i == 0)
        def _():
            dca_ref[...] = jnp.zeros_like(dca_ref)
            dwp_ref[...] = jnp.zeros_like(dwp_ref)
            dps_ref[...] = jnp.zeros_like(dps_ref)

        keep_p = jnp.where(i > 0, 1.0, 0.0)
        keep_n = jnp.where(i < n_t - 1, 1.0, 0.0)
        dmix = lax.dot_general(jnp.concatenate([dh_ref[...], dhn_ref[...]], axis=0), wout_ref[...], _NT,
                               preferred_element_type=F32)
        a_b, a_c, a_v = (p_ref[k].astype(F32) for k in range(3))
        cv_ext = jnp.concatenate([pp_ref[1].astype(F32) * pp_ref[2].astype(F32) * keep_p, a_c * a_v], axis=0)
        dy_a = dmix[:ts, 0:w]
        dp_ref[0] = (dy_a * _causal_conv(cv_ext, ca_ref)[HALO:]).astype(BF16)
        dcc = dy_a * a_b
        dca_ref[2:3, :] += _colsum(dcc * cv_ext[HALO:])
        dca_ref[1:2, :] += _colsum(dcc * _back(cv_ext, 1)[HALO:])
        dca_ref[0:1, :] += _colsum(dcc * _back(cv_ext, 2)[HALO:])
        dcc_ext = jnp.concatenate([dcc, dmix[ts:, 0:w] * pn_ref[0].astype(F32) * keep_n], axis=0)
        dcv = _causal_conv_t(dcc_ext, ca_ref)[:ts]
        dp_ref[1] = (dcv * a_v).astype(BF16)
        dp_ref[2] = (dcv * a_c).astype(BF16)
        z_ext = jnp.concatenate([pp_ref[3].astype(F32) * keep_p, p_ref[3].astype(F32)], axis=0)
        pos = (i * ts + lax.broadcasted_iota(jnp.int32, (ts, 1), 0) + 1).astype(F32)
        pos_ext = (i * ts + lax.broadcasted_iota(jnp.int32, (n_ext, 1), 0) + 1).astype(F32)
        for g, win in enumerate(POOL_WINDOWS):
            cols = slice(g * GROUP, (g + 1) * GROUP)
            ycols = slice(w + g * GROUP, w + (g + 1) * GROUP)
            pooled = _pool_fwd(z_ext, g, pos).astype(BF16)
            mixed = jnp.dot(pooled, wp_ref[g], preferred_element_type=F32)
            dy_b = dmix[:ts, ycols]
            dps_ref[:, cols] += _colsum(dy_b * mixed)
            dmixed_ext = jnp.concatenate([dy_b, dmix[ts:, ycols] * keep_n], axis=0) * ps_ref[:, cols]
            dmixed_ext = dmixed_ext.astype(BF16)
            dwp_ref[g] += lax.dot_general(pooled, dmixed_ext[:ts], _TN, preferred_element_type=F32)
            dpooled = jnp.dot(dmixed_ext, wpt_ref[g], preferred_element_type=F32)
            acc = dpooled / jnp.minimum(pos_ext, float(win))
            sh = 1
            while sh < win:
                acc = acc + _fwd(acc, sh)
                sh *= 2
            dp_ref[3, :, cols] = (acc[:ts] - dpooled[:ts]).astype(BF16)

    tile4 = pl.BlockSpec((4, ts, w), lambda i: (0, i, 0))
    const = lambda shape: pl.BlockSpec(shape, lambda i: (0,) * len(shape))
    return pl.pallas_call(
        body, name=name, grid=(n_t,),
        in_specs=[tile4, prev, nxt, pl.BlockSpec((ts, d), lambda i: (i, 0)),
                  pl.BlockSpec((HALO, d), lambda i: (jnp.minimum((i + 1) * hb, n_t * hb - 1), 0)),
                  const((2 * w, d)), const((3, w)), const((4, GROUP, GROUP)), const((4, GROUP, GROUP)), const((1, w))]
        + ([ANY] if pin is not None else []),
        out_specs=[tile4, const((3, w)), const((4, GROUP, GROUP)), const((1, w))],
        out_shape=[jax.ShapeDtypeStruct((4, s, w), BF16), jax.ShapeDtypeStruct((3, w), F32),
                   jax.ShapeDtypeStruct((4, GROUP, GROUP), F32), jax.ShapeDtypeStruct((1, w), F32)],
        compiler_params=_params("arbitrary"))(
            proj, proj, proj, dh_bf, dh_bf, wout, conv_a, w_pool, w_pool_t, pool_scale,
            *([pin] if pin is not None else []))


def _ffn_halo(ts, f, n_t, nxt):
    hb = ts // HALO
    if nxt:
        return pl.BlockSpec((None, HALO, f), lambda j, i: (j, jnp.minimum((i + 1) * hb, n_t * hb - 1), 0))
    return pl.BlockSpec((None, HALO, f), lambda j, i: (j, jnp.maximum(i * hb - 1, 0), 0))


def ffn_act_fwd(g, up, cw, cb, *, name):
    j_n, s, f = g.shape
    ts = min(ROW_TILE, s)
    n_t = s // ts

    def body(g_ref, gp_ref, u_ref, cw_ref, cb_ref, o_ref):
        keep = jnp.where(pl.program_id(1) > 0, 1.0, 0.0)
        g_ext = jnp.concatenate([gp_ref[...] * keep, g_ref[...]], axis=0)
        gc = _causal_conv(g_ext, cw_ref)[HALO:] + cb_ref[...]
        o_ref[...] = (gc * jax.nn.sigmoid(gc) * u_ref[...]).astype(BF16)

    tile = pl.BlockSpec((None, ts, f), lambda j, i: (j, i, 0))
    return pl.pallas_call(
        body, name=name, grid=(j_n, n_t),
        in_specs=[tile, _ffn_halo(ts, f, n_t, False), tile,
                  pl.BlockSpec((None, 3, f), lambda j, i: (j, 0, 0)),
                  pl.BlockSpec((None, 1, f), lambda j, i: (j, 0, 0))],
        out_specs=tile,
        out_shape=jax.ShapeDtypeStruct((j_n, s, f), BF16),
        compiler_params=_params("parallel", "parallel"))(g, g, up, cw, cb)


def ffn_act_bwd(g, up, dact, cw, cb, *, name):
    j_n, s, f = g.shape
    ts = min(ROW_TILE, s)
    n_t = s // ts

    def body(g_ref, gp_ref, gn_ref, u_ref, un_ref, d_ref, dn_ref, cw_ref, cb_ref,
             dg_ref, du_ref, dcw_ref, dcb_ref):
        i = pl.program_id(1)

        @pl.when(i == 0)
        def _():
            dcw_ref[...] = jnp.zeros_like(dcw_ref)
            dcb_ref[...] = jnp.zeros_like(dcb_ref)

        keep_p = jnp.where(i > 0, 1.0, 0.0)
        keep_n = jnp.where(i < n_t - 1, 1.0, 0.0)
        g_ext = jnp.concatenate([gp_ref[...] * keep_p, g_ref[...], gn_ref[...]], axis=0)
        gc = _causal_conv(g_ext, cw_ref)[HALO:] + cb_ref[...]
        sig = jax.nn.sigmoid(gc)
        dact_ext = jnp.concatenate([d_ref[...], dn_ref[...] * keep_n], axis=0)
        du_ref[...] = (dact_ext * gc * sig)[:ts].astype(BF16)
        up_ext = jnp.concatenate([u_ref[...], un_ref[...]], axis=0)
        dgc = dact_ext * up_ext * (sig * (1.0 + gc * (1.0 - sig)))
        dg_ref[...] = _causal_conv_t(dgc, cw_ref)[:ts].astype(BF16)
        dgc_t = dgc[:ts]
        dcb_ref[...] += _colsum(dgc_t)
        dcw_ref[2:3, :] += _colsum(dgc_t * g_ext[HALO:HALO + ts])
        dcw_ref[1:2, :] += _colsum(dgc_t * _back(g_ext, 1)[HALO:HALO + ts])
        dcw_ref[0:1, :] += _colsum(dgc_t * _back(g_ext, 2)[HALO:HALO + ts])

    tile = pl.BlockSpec((None, ts, f), lambda j, i: (j, i, 0))
    prev, nxt = _ffn_halo(ts, f, n_t, False), _ffn_halo(ts, f, n_t, True)
    return pl.pallas_call(
        body, name=name, grid=(j_n, n_t),
        in_specs=[tile, prev, nxt, tile, nxt, tile, nxt,
                  pl.BlockSpec((None, 3, f), lambda j, i: (j, 0, 0)),
                  pl.BlockSpec((None, 1, f), lambda j, i: (j, 0, 0))],
        out_specs=[tile, tile, pl.BlockSpec((None, 3, f), lambda j, i: (j, 0, 0)),
                   pl.BlockSpec((None, 1, f), lambda j, i: (j, 0, 0))],
        out_shape=[jax.ShapeDtypeStruct((j_n, s, f), BF16), jax.ShapeDtypeStruct((j_n, s, f), BF16),
                   jax.ShapeDtypeStruct((j_n, 3, f), F32), jax.ShapeDtypeStruct((j_n, 1, f), F32)],
        compiler_params=_params("parallel", "arbitrary"))(g, g, g, up, up, dact, dact, cw, cb)


def ffn_in_fwd(hn, wg, wu, cw, cb, *, name, pin=None):
    s, d = hn.shape
    j_n, f, _ = wg.shape
    tm = min(ROW_TILE, s)
    hb = tm // HALO

    def body(x_ref, xp_ref, wg_ref, wu_ref, cw_ref, cb_ref, *rest):
        g_ref, u_ref, gc_ref, a_ref = rest[-4:]
        i, j = pl.program_id(0), pl.program_id(1)
        x_ext = jnp.concatenate([xp_ref[...], x_ref[...]], axis=0)
        g_ext = lax.dot_general(x_ext, wg_ref[j], _NT, preferred_element_type=F32).astype(BF16)
        up = lax.dot_general(x_ref[...], wu_ref[j], _NT, preferred_element_type=F32).astype(BF16)
        g_ref[...] = g_ext[HALO:]
        u_ref[...] = up
        a_ref[...] = _ffn_act(g_ext, up, cw_ref, cb_ref, gc_ref, i)

    whole = pl.BlockSpec((j_n, f, d), lambda i, j: (0, 0, 0))
    tile = pl.BlockSpec((None, tm, f), lambda i, j: (j, i, 0))
    shape = jax.ShapeDtypeStruct((j_n, s, f), BF16)
    return pl.pallas_call(
        body, name=name, grid=(s // tm, j_n),
        in_specs=[pl.BlockSpec((tm, d), lambda i, j: (i, 0)),
                  pl.BlockSpec((HALO, d), lambda i, j: (jnp.maximum(i * hb - 1, 0), 0)),
                  whole, whole,
                  pl.BlockSpec((None, 3, f), lambda i, j: (j, 0, 0)),
                  pl.BlockSpec((None, 1, f), lambda i, j: (j, 0, 0))] + ([ANY] if pin is not None else []),
        out_specs=[tile] * 4, out_shape=[shape] * 4,
        compiler_params=_params("parallel", "parallel"))(hn, hn, wg, wu, cw, cb, *([pin] if pin is not None else []))


def _ffn_act(g_ext, up, cw_ref, cb_ref, gc_ref, i):
    keep = jnp.where(i > 0, 1.0, 0.0)
    g32 = jnp.concatenate([g_ext[:HALO].astype(F32) * keep, g_ext[HALO:].astype(F32)], axis=0)
    gc = (_causal_conv(g32, cw_ref)[HALO:] + cb_ref[...]).astype(BF16)
    gc_ref[...] = gc
    gc = gc.astype(F32)
    return (gc * jax.nn.sigmoid(gc) * up.astype(F32)).astype(BF16)


def ffn_loss_fwd(h, gain, wg, wu, wd, cw, cb, target, final_gain, *, name):
    s, d = h.shape
    j_n, f, _ = wg.shape
    tm = min(ROW_TILE, s)
    hb = tm // HALO

    def body(h_ref, hp_ref, gain_ref, wg_ref, wu_ref, wd_ref, cw_ref, cb_ref, t_ref, fg_ref,
             o_ref, ob_ref, l_ref, dfg_ref, xn_ref, r_ref, g_ref, u_ref, gc_ref, x_s, acc_s):
        i, j = pl.program_id(0), pl.program_id(1)

        @pl.when((i == 0) & (j == 0))
        def _():
            l_ref[...] = jnp.zeros_like(l_ref)
            dfg_ref[...] = jnp.zeros_like(dfg_ref)

        @pl.when(j == 0)
        def _():
            h_ext = jnp.concatenate([hp_ref[...], h_ref[...]], axis=0)
            rstd = lax.rsqrt(jnp.mean(h_ext * h_ext, axis=-1, keepdims=True) + EPS)
            x_s[...] = (h_ext * rstd * gain_ref[...]).astype(BF16)
            xn_ref[...] = x_s[HALO:, :]
            r_ref[...] = rstd[HALO:]
            acc_s[...] = h_ref[...]

        g_ext = lax.dot_general(x_s[...], wg_ref[j], _NT, preferred_element_type=F32).astype(BF16)
        up = lax.dot_general(x_s[HALO:, :], wu_ref[j], _NT, preferred_element_type=F32).astype(BF16)
        g_ref[...] = g_ext[HALO:]
        u_ref[...] = up
        act = _ffn_act(g_ext, up, cw_ref, cb_ref, gc_ref, i)
        acc_s[...] += jnp.dot(act, wd_ref[j], preferred_element_type=F32)

        @pl.when(j == j_n - 1)
        def _():
            x = acc_s[...]
            rstd = lax.rsqrt(jnp.mean(x * x, axis=-1, keepdims=True) + EPS)
            hhat = x * rstd
            err = hhat * fg_ref[...] - t_ref[...]
            l_ref[...] += 0.5 * jnp.sum(jnp.mean(err * err, axis=-1, keepdims=True), axis=0, keepdims=True)
            dy = err * (1.0 / d)
            dfg_ref[...] += _colsum(dy * hhat)
            dyg = dy * fg_ref[...]
            dh = rstd * (dyg - hhat * jnp.mean(dyg * hhat, axis=-1, keepdims=True))
            o_ref[...] = dh
            ob_ref[...] = dh.astype(BF16)

    whole = pl.BlockSpec((j_n, f, d), lambda i, j: (0, 0, 0), pipeline_mode=pl.Buffered(1))
    row = pl.BlockSpec((tm, d), lambda i, j: (i, 0))
    vec = pl.BlockSpec((1, d), lambda i, j: (0, 0))
    tile = pl.BlockSpec((None, tm, f), lambda i, j: (j, i, 0))
    return pl.pallas_call(
        body, name=name, grid=(s // tm, j_n),
        in_specs=[row, pl.BlockSpec((HALO, d), lambda i, j: (jnp.maximum(i * hb - 1, 0), 0)),
                  vec, whole, whole, whole,
                  pl.BlockSpec((None, 3, f), lambda i, j: (j, 0, 0)),
                  pl.BlockSpec((None, 1, f), lambda i, j: (j, 0, 0)), row, vec],
        out_specs=[row, row, pl.BlockSpec((1, 128), lambda i, j: (0, 0)), vec,
                   row, pl.BlockSpec((tm, 1), lambda i, j: (i, 0)), tile, tile, tile],
        out_shape=[jax.ShapeDtypeStruct((s, d), F32), jax.ShapeDtypeStruct((s, d), BF16),
                   jax.ShapeDtypeStruct((1, 128), F32), jax.ShapeDtypeStruct((1, d), F32),
                   jax.ShapeDtypeStruct((s, d), BF16), jax.ShapeDtypeStruct((s, 1), F32)]
        + [jax.ShapeDtypeStruct((j_n, s, f), BF16)] * 3,
        scratch_shapes=[pltpu.VMEM((HALO + tm, d), BF16), pltpu.VMEM((tm, d), F32)],
        compiler_params=_params("arbitrary", "arbitrary"))(h, h, gain, wg, wu, wd, cw, cb, target, final_gain)


def ffn_bwd_a(dh_bf, hn, g, up, gc, wd, cw, *, name, pin=None):
    s, d = hn.shape
    j_n, _, f = g.shape
    tm = min(ROW_TILE, s)
    n_t = s // tm
    hb = tm // HALO

    def body(dh_ref, dhn_ref, x_ref, g_ref, gc_ref, gcn_ref, u_ref, un_ref, wd_ref, cw_ref, *rest):
        dg_ref, du_ref, dwg_ref, dwu_ref, dwd_ref, dcw_ref, dcb_ref = rest[-7:]
        i = pl.program_id(1)

        @pl.when(i == 0)
        def _():
            for r in (dwg_ref, dwu_ref, dwd_ref, dcw_ref, dcb_ref):
                r[...] = jnp.zeros_like(r)

        keep_n = jnp.where(i < n_t - 1, 1.0, 0.0)
        dh = dh_ref[...]
        dact = lax.dot_general(jnp.concatenate([dh, dhn_ref[...]], axis=0), wd_ref[...], _NT,
                               preferred_element_type=F32)
        dact = jnp.concatenate([dact[:tm], dact[tm:] * keep_n], axis=0)
        gc_ext = jnp.concatenate([gc_ref[...], gcn_ref[...]], axis=0).astype(F32)
        sig = jax.nn.sigmoid(gc_ext)
        silu = gc_ext * sig
        up_ext = jnp.concatenate([u_ref[...], un_ref[...]], axis=0).astype(F32)
        act = (silu * up_ext)[:tm].astype(BF16)
        dwd_ref[...] += lax.dot_general(act, dh, _TN, preferred_element_type=F32)
        dup = (dact * silu)[:tm].astype(BF16)
        du_ref[...] = dup
        dgc = dact * up_ext * (sig + silu * (1.0 - sig))
        dgc_1, dgc_2 = _fwd(dgc, 1), _fwd(dgc, 2)
        dg = (cw_ref[2:3, :] * dgc + cw_ref[1:2, :] * dgc_1 + cw_ref[0:1, :] * dgc_2)[:tm].astype(BF16)
        dg_ref[...] = dg
        x = x_ref[...]
        dwg_ref[...] += lax.dot_general(dg, x, _TN, preferred_element_type=F32)
        dwu_ref[...] += lax.dot_general(dup, x, _TN, preferred_element_type=F32)
        g32 = g_ref[...].astype(F32)
        dcb_ref[...] += _colsum(dgc[:tm])
        dcw_ref[2:3, :] += _colsum(dgc[:tm] * g32)
        dcw_ref[1:2, :] += _colsum(dgc_1[:tm] * g32)
        dcw_ref[0:1, :] += _colsum(dgc_2[:tm] * g32)

    rows = pl.BlockSpec((tm, d), lambda j, i: (i, 0))
    rows_next = pl.BlockSpec((HALO, d), lambda j, i: (jnp.minimum((i + 1) * hb, n_t * hb - 1), 0))
    tile = pl.BlockSpec((None, tm, f), lambda j, i: (j, i, 0))
    nxt = _ffn_halo(tm, f, n_t, True)
    per_j = lambda r, c: pl.BlockSpec((None, r, c), lambda j, i: (j, 0, 0))
    return pl.pallas_call(
        body, name=name, grid=(j_n, n_t),
        in_specs=[rows, rows_next, rows, tile, tile, nxt, tile, nxt, per_j(f, d), per_j(3, f)]
        + ([ANY] if pin is not None else []),
        out_specs=[tile, tile, per_j(f, d), per_j(f, d), per_j(f, d), per_j(3, f), per_j(1, f)],
        out_shape=[jax.ShapeDtypeStruct((j_n, s, f), BF16), jax.ShapeDtypeStruct((j_n, s, f), BF16),
                   jax.ShapeDtypeStruct((j_n, f, d), F32), jax.ShapeDtypeStruct((j_n, f, d), F32),
                   jax.ShapeDtypeStruct((j_n, f, d), F32), jax.ShapeDtypeStruct((j_n, 3, f), F32),
                   jax.ShapeDtypeStruct((j_n, 1, f), F32)],
        compiler_params=_params("parallel", "arbitrary"))(
            dh_bf, dh_bf, hn, g, gc, gc, up, up, wd, cw, *([pin] if pin is not None else []))


def dx_rms_bwd(pairs, h, gain, rstd, dres, *, name, pin=None):
    j_n, s, f = pairs[0][0].shape
    d = h.shape[1]
    tm = min(ROW_TILE, s)
    n_p = len(pairs)
    dims = [_NT if w.shape[1:] == (d, f) else (((1,), (0,)), ((), ())) for _, w in pairs]

    def body(*refs):
        dy_refs, w_refs = refs[:n_p], refs[n_p:2 * n_p]
        h_ref, g_ref, r_ref, dr_ref = refs[2 * n_p:2 * n_p + 4]
        o_ref, ob_ref, dgain_ref = refs[-3:]

        @pl.when(pl.program_id(0) == 0)
        def _():
            dgain_ref[...] = jnp.zeros_like(dgain_ref)
        dx = jnp.zeros((tm, d), F32)
        for j in range(j_n):
            for p in range(n_p):
                dx = dx + lax.dot_general(dy_refs[p][j], w_refs[p][j], dims[p], preferred_element_type=F32)
        rstd_v = r_ref[...]
        hhat = h_ref[...] * rstd_v
        dgain_ref[...] += _colsum(dx * hhat)
        dxg = dx * g_ref[...]
        dh = dr_ref[...] + rstd_v * (dxg - hhat * jnp.mean(dxg * hhat, axis=-1, keepdims=True))
        o_ref[...] = dh
        ob_ref[...] = dh.astype(BF16)

    tile4 = pl.BlockSpec((j_n, tm, f), lambda i: (0, i, 0))
    whole = [pl.BlockSpec(w.shape, lambda i: (0, 0, 0), pipeline_mode=pl.Buffered(1)) for _, w in pairs]
    row = pl.BlockSpec((tm, d), lambda i: (i, 0))
    vec = pl.BlockSpec((1, d), lambda i: (0, 0))
    return pl.pallas_call(
        body, name=name, grid=(s // tm,),
        in_specs=[tile4] * n_p + whole + [row, vec, pl.BlockSpec((tm, 1), lambda i: (i, 0)), row]
        + ([ANY] if pin is not None else []),
        out_specs=[row, row, vec],
        out_shape=[jax.ShapeDtypeStruct((s, d), F32), jax.ShapeDtypeStruct((s, d), BF16),
                   jax.ShapeDtypeStruct((1, d), F32)],
        compiler_params=_params("arbitrary"))(
            *[p[0] for p in pairs], *[p[1] for p in pairs], h, gain, rstd, dres, *([pin] if pin is not None else []))


def _sgu_gate(vn_bf, ws_ref, bs_ref, h, rows):
    tri = lax.broadcasted_iota(jnp.int32, (CHUNK, CHUNK), 0) >= lax.broadcasted_iota(jnp.int32, (CHUNK, CHUNK), 1)
    ws = jnp.where(tri, ws_ref[h], 0.0).astype(BF16)
    cols = slice((h % 4) * GROUP, (h % 4 + 1) * GROUP)
    return ws, jnp.dot(ws, vn_bf[h // 4][rows, cols], preferred_element_type=F32) + bs_ref[h]


def odd_layer_fwd(h, gain, win, sgu_norm, w_spatial, b_spatial, wout, *, name):
    s, d = h.shape
    w = win.shape[-1]
    ts = min(ROW_TILE, s)
    n_heads = w_spatial.shape[0]

    def body(h_ref, gain_ref, win_ref, n_ref, ws_ref, bs_ref, wout_ref, o_ref, xn_ref, r_ref, p_ref, m_ref, rv_ref):
        x = h_ref[...]
        rstd_x = lax.rsqrt(jnp.mean(x * x, axis=-1, keepdims=True) + EPS)
        xn = (x * rstd_x * gain_ref[...]).astype(BF16)
        xn_ref[...] = xn
        r_ref[...] = rstd_x
        for k in range(4):
            p_ref[k] = jnp.dot(xn, win_ref[k], preferred_element_type=F32).astype(BF16)
        v = [_gelu(p_ref[2].astype(F32)), _gelu(p_ref[3].astype(F32))]
        ms = (jnp.sum(v[0] * v[0], axis=-1, keepdims=True) + jnp.sum(v[1] * v[1], axis=-1, keepdims=True)) / (2 * w)
        rstd = lax.rsqrt(ms + EPS)
        rv_ref[...] = rstd
        vn = [(v[k] * rstd * n_ref[:, k * w:(k + 1) * w]).astype(BF16) for k in range(2)]
        for hd in range(n_heads):
            cols = slice((hd % 4) * GROUP, (hd % 4 + 1) * GROUP)
            for c in range(ts // CHUNK):
                rows = slice(c * CHUNK, (c + 1) * CHUNK)
                _, gate = _sgu_gate(vn, ws_ref, bs_ref, hd, rows)
                u = _gelu(p_ref[hd // 4, rows, cols].astype(F32))
                m_ref[rows, hd * GROUP:(hd + 1) * GROUP] = (u * gate).astype(BF16)
        o_ref[...] = x + jnp.dot(m_ref[...], wout_ref[...], preferred_element_type=F32)

    const = lambda shape: pl.BlockSpec(shape, lambda i: (0,) * len(shape))
    row = pl.BlockSpec((ts, d), lambda i: (i, 0))
    col1 = pl.BlockSpec((ts, 1), lambda i: (i, 0))
    return pl.pallas_call(
        body, name=name, grid=(s // ts,),
        in_specs=[row, const((1, d)), const((4, d, w)), const((1, 2 * w)),
                  const((n_heads, CHUNK, CHUNK)), const((n_heads, CHUNK, 1)), const((2 * w, d))],
        out_specs=[row, row, col1, pl.BlockSpec((4, ts, w), lambda i: (0, i, 0)),
                   pl.BlockSpec((ts, 2 * w), lambda i: (i, 0)), col1],
        out_shape=[jax.ShapeDtypeStruct((s, d), F32), jax.ShapeDtypeStruct((s, d), BF16),
                   jax.ShapeDtypeStruct((s, 1), F32), jax.ShapeDtypeStruct((4, s, w), BF16),
                   jax.ShapeDtypeStruct((s, 2 * w), BF16), jax.ShapeDtypeStruct((s, 1), F32)],
        compiler_params=_params("parallel"))(h, gain, win, sgu_norm, w_spatial, b_spatial, wout)


def sgu_bwd(p, dh_bf, wout, rstd, sgu_norm, w_spatial, b_spatial, *, name, pin=None):
    _, s, w = p.shape
    d = dh_bf.shape[1]
    ts = min(ROW_TILE, s)
    n_heads = w_spatial.shape[0]

    def body(p_ref, dh_ref, wout_ref, r_ref, n_ref, ws_ref, bs_ref, *rest):
        dp_ref, dn_ref, dws_ref, dbs_ref, dvn_ref, dm_ref = rest[-6:]

        @pl.when(pl.program_id(0) == 0)
        def _():
            dn_ref[...] = jnp.zeros_like(dn_ref)
            dws_ref[...] = jnp.zeros_like(dws_ref)
            dbs_ref[...] = jnp.zeros_like(dbs_ref)

        dm_ref[...] = lax.dot_general(dh_ref[...], wout_ref[...], _NT, preferred_element_type=F32)
        rstd_v = r_ref[...]
        vhat = [_gelu(p_ref[2 + k].astype(F32)) * rstd_v for k in range(2)]
        vn = [(vhat[k] * n_ref[:, k * w:(k + 1) * w]).astype(BF16) for k in range(2)]
        tri = lax.broadcasted_iota(jnp.int32, (CHUNK, CHUNK), 0) >= lax.broadcasted_iota(jnp.int32, (CHUNK, CHUNK), 1)
        for h in range(n_heads):
            cols = slice((h % 4) * GROUP, (h % 4 + 1) * GROUP)
            ocols = slice(h * GROUP, (h + 1) * GROUP)
            for c in range(ts // CHUNK):
                rows = slice(c * CHUNK, (c + 1) * CHUNK)
                ws, gate = _sgu_gate(vn, ws_ref, bs_ref, h, rows)
                pu = p_ref[h // 4, rows, cols].astype(F32)
                dm = dm_ref[rows, ocols]
                dp_ref[h // 4, rows, cols] = (dm * gate * _gelu_grad(pu)).astype(BF16)
                dgate = dm * _gelu(pu)
                dbs_ref[h] += jnp.sum(dgate, axis=-1, keepdims=True)
                dgate_bf = dgate.astype(BF16)
                dws = lax.dot_general(dgate_bf, vn[h // 4][rows, cols], _NT, preferred_element_type=F32)
                dws_ref[h] += jnp.where(tri, dws, 0.0)
                dvn_ref[rows, ocols] = lax.dot_general(ws, dgate_bf, _TN, preferred_element_type=F32)
        for k in range(2):
            kc = slice(k * w, (k + 1) * w)
            dvn = dvn_ref[:, kc]
            dn_ref[:, kc] += _colsum(dvn * vhat[k])
        dvh = [dvn_ref[:, k * w:(k + 1) * w] * n_ref[:, k * w:(k + 1) * w] for k in range(2)]
        dot = (jnp.sum(dvh[0] * vhat[0], axis=-1, keepdims=True)
               + jnp.sum(dvh[1] * vhat[1], axis=-1, keepdims=True)) / (2 * w)
        for k in range(2):
            dv = rstd_v * (dvh[k] - vhat[k] * dot)
            dp_ref[2 + k] = (dv * _gelu_grad(p_ref[2 + k].astype(F32))).astype(BF16)

    const = lambda shape: pl.BlockSpec(shape, lambda i: (0,) * len(shape))
    tile4 = pl.BlockSpec((4, ts, w), lambda i: (0, i, 0))
    return pl.pallas_call(
        body, name=name, grid=(s // ts,),
        in_specs=[tile4, pl.BlockSpec((ts, d), lambda i: (i, 0)), const((2 * w, d)),
                  pl.BlockSpec((ts, 1), lambda i: (i, 0)),
                  const((1, 2 * w)), const((n_heads, CHUNK, CHUNK)), const((n_heads, CHUNK, 1))]
        + ([ANY] if pin is not None else []),
        out_specs=[tile4, const((1, 2 * w)), const((n_heads, CHUNK, CHUNK)), const((n_heads, CHUNK, 1))],
        out_shape=[jax.ShapeDtypeStruct((4, s, w), BF16), jax.ShapeDtypeStruct((1, 2 * w), F32),
                   jax.ShapeDtypeStruct((n_heads, CHUNK, CHUNK), F32),
                   jax.ShapeDtypeStruct((n_heads, CHUNK, 1), F32)],
        scratch_shapes=[pltpu.VMEM((ts, 2 * w), F32), pltpu.VMEM((ts, 2 * w), F32)],
        compiler_params=_params("arbitrary"))(
            p, dh_bf, wout, rstd, sgu_norm, w_spatial, b_spatial, *([pin] if pin is not None else []))


def _row_tile(rows):
    if rows <= ROW_TILE:
        return rows
    for t in (512, 384, 352, 256, 128, 64, 32, 16, 8):
        if rows % t == 0:
            return t
    return rows


def adamw(w, g, m, v, *, name):
    shape = w.shape
    cols = shape[-1]
    rows = w.size // cols
    w2, g2, m2, v2 = (a.reshape(rows, cols) for a in (w, g, m, v))
    tr = _row_tile(rows)
    bc1 = 1.0 - ADAM_B1 ** ADAM_STEP
    bc2 = 1.0 - ADAM_B2 ** ADAM_STEP

    def body(w_ref, g_ref, m_ref, v_ref, d_ref, nm_ref, nv_ref):
        grad = g_ref[...]
        m_new = ADAM_B1 * m_ref[...] + (1.0 - ADAM_B1) * grad
        v_new = ADAM_B2 * v_ref[...] + (1.0 - ADAM_B2) * (grad * grad)
        nm_ref[...] = m_new
        nv_ref[...] = v_new
        d_ref[...] = -ADAM_LR * ((m_new / bc1) / (jnp.sqrt(v_new / bc2) + ADAM_EPS) + ADAM_WD * w_ref[...])

    spec = pl.BlockSpec((tr, cols), lambda i: (i, 0))
    outs = pl.pallas_call(
        body, name=name, grid=(rows // tr,),
        in_specs=[spec] * 4, out_specs=[spec] * 3,
        out_shape=[jax.ShapeDtypeStruct((rows, cols), F32)] * 3,
        compiler_params=_params("parallel"))(w2, g2, m2, v2)
    return tuple(o.reshape(shape) for o in outs)


def adamw_layers(w, grads, m, v, *, name):
    n_l, rows, cols = w.shape
    tr = _row_tile(rows)
    bc1 = 1.0 - ADAM_B1 ** ADAM_STEP
    bc2 = 1.0 - ADAM_B2 ** ADAM_STEP
    outs = None
    for l in range(n_l):
        def body(w_ref, g_ref, m_ref, v_ref, *rest):
            go_ref, d_ref, nm_ref, nv_ref = rest[-4:]
            grad = g_ref[...]
            m_new = ADAM_B1 * m_ref[...] + (1.0 - ADAM_B1) * grad
            v_new = ADAM_B2 * v_ref[...] + (1.0 - ADAM_B2) * (grad * grad)
            go_ref[...] = grad
            nm_ref[...] = m_new
            nv_ref[...] = v_new
            d_ref[...] = -ADAM_LR * ((m_new / bc1) / (jnp.sqrt(v_new / bc2) + ADAM_EPS) + ADAM_WD * w_ref[...])

        layer = pl.BlockSpec((None, tr, cols), lambda i, l=l: (l, i, 0))
        prev = list(outs) if outs is not None else []
        outs = pl.pallas_call(
            body, name=f"{name}{l}", grid=(rows // tr,),
            in_specs=[layer, pl.BlockSpec((tr, cols), lambda i: (i, 0)), layer, layer] + [ANY] * len(prev),
            out_specs=[layer] * 4,
            out_shape=[jax.ShapeDtypeStruct(w.shape, F32)] * 4,
            input_output_aliases={4 + k: k for k in range(len(prev))},
            compiler_params=_params("parallel"))(w, grads[l], m, v, *prev)
    return tuple(outs)


def _place():
    return lax.axis_index("x"), lax.axis_index("y"), lax.axis_index("c")


def _other_chips(x, y):
    return [(1 - x, y), (x, 1 - y), (1 - x, 1 - y)]


HBM = pl.BlockSpec(memory_space=pltpu.HBM)
SEM = pl.BlockSpec(memory_space=pltpu.SEMAPHORE)
DATAFLOW = pltpu.SideEffectType.DATAFLOW_SIDE_EFFECTING


def _in_hbm(a):
    return pltpu.with_memory_space_constraint(a, pltpu.HBM)


def cast_into_slot(w, chip, *, l=None, name):
    rows, cols = w.shape[-2:]
    tr = _row_tile(rows)

    def body(chip_ref, w_ref, o_ref):
        o_ref[...] = w_ref[...].astype(BF16)

    in_spec = (pl.BlockSpec((tr, cols), lambda i, chip_ref: (i, 0)) if l is None
               else pl.BlockSpec((None, tr, cols), lambda i, chip_ref: (l, i, 0)))
    return pl.pallas_call(
        body, name=name,
        grid_spec=pltpu.PrefetchScalarGridSpec(
            num_scalar_prefetch=1, grid=(rows // tr,), in_specs=[in_spec],
            out_specs=pl.BlockSpec((None, tr, cols), lambda i, chip_ref: (chip_ref[0], i, 0))),
        out_shape=jax.ShapeDtypeStruct((N_CHIPS, rows, cols), BF16),
        compiler_params=_params("parallel"))(chip, w)


def _half(ref, slot, c):
    half = ref.shape[1] // 2
    return ref.at[slot, pl.ds(c * half, half), :]


def gather_start(groups, smalls):
    flat = [b for g in groups for b in g]
    n_b, n_s, n_g = len(flat), len(smalls), len(groups)

    def body(*refs):
        bufs, small_refs = refs[:n_b], refs[n_b:n_b + n_s]
        sems = refs[n_b + n_s:n_b + n_s + 2 * n_g + 2]
        token = refs[-1]
        x, y, c = _place()
        me = 2 * x + y
        chips = _other_chips(x, y)
        for si in range(n_s):
            piece = small_refs[si].at[me]
            for k, (px, py) in enumerate(chips):
                pltpu.make_async_remote_copy(
                    src_ref=piece, dst_ref=piece,
                    send_sem=sems[2 * n_g].at[3 * si + k], recv_sem=sems[2 * n_g + 1].at[3 * si + k],
                    device_id=(px, py, c), device_id_type=MESH).start()
        t = 0
        for gi, group in enumerate(groups):
            for ti in range(len(group)):
                piece = _half(bufs[t], me, c)
                t += 1
                for k, (px, py) in enumerate(chips):
                    pltpu.make_async_remote_copy(
                        src_ref=piece, dst_ref=piece,
                        send_sem=sems[2 * gi].at[3 * ti + k], recv_sem=sems[2 * gi + 1].at[3 * ti + k],
                        device_id=(px, py, c), device_id_type=MESH).start()
        token[...] = jnp.zeros_like(token)

    sem_shapes = []
    for group in groups:
        sem_shapes += [pltpu.SemaphoreType.DMA((3 * len(group),))] * 2
    sem_shapes += [pltpu.SemaphoreType.DMA((3 * n_s,))] * 2
    arrays = flat + list(smalls)
    n_sem = len(sem_shapes)
    res = pl.pallas_call(
        body, name="gather_start",
        out_shape=tuple(sem_shapes) + tuple(pltpu.HBM(a.shape, a.dtype) for a in arrays)
        + (jax.ShapeDtypeStruct((8, 128), F32),),
        in_specs=[HBM] * len(arrays),
        out_specs=tuple([SEM] * n_sem + [HBM] * len(arrays) + [pl.BlockSpec(memory_space=pltpu.VMEM)]),
        input_output_aliases={i: n_sem + i for i in range(len(arrays))},
        compiler_params=pltpu.CompilerParams(has_side_effects=DATAFLOW))(*[_in_hbm(a) for a in arrays])
    sems, thru, token = res[:n_sem], res[n_sem:-1], res[-1]
    out_groups, t = [], 0
    for group in groups:
        out_groups.append(list(thru[t:t + len(group)]))
        t += len(group)
    return sems, out_groups, list(thru[n_b:]), token


def gather_wait(bufs, send, recv, after, *, name, smalls=(), small_send=None, small_recv=None):
    n_b, n_s = len(bufs), len(smalls)
    arrays = list(bufs) + list(smalls)
    sem_ops = [send, recv] + ([small_send, small_recv] if n_s else [])

    def body(*refs):
        buf_refs, small_refs = refs[:n_b], refs[n_b:n_b + n_s]
        sems = refs[n_b + n_s:n_b + n_s + len(sem_ops)]
        x, y, c = _place()
        me = 2 * x + y
        chips = _other_chips(x, y)
        for ti in range(n_b):
            for k, (px, py) in enumerate(chips):
                done = pltpu.make_async_remote_copy(
                    src_ref=_half(buf_refs[ti], me, c), dst_ref=_half(buf_refs[ti], 2 * px + py, c),
                    send_sem=sems[0].at[3 * ti + k], recv_sem=sems[1].at[3 * ti + k],
                    device_id=(px, py, c), device_id_type=MESH)
                done.wait_send()
                done.wait_recv()
        for si in range(n_s):
            for k, (px, py) in enumerate(chips):
                done = pltpu.make_async_remote_copy(
                    src_ref=small_refs[si].at[me], dst_ref=small_refs[si].at[2 * px + py],
                    send_sem=sems[2].at[3 * si + k], recv_sem=sems[3].at[3 * si + k],
                    device_id=(px, py, c), device_id_type=MESH)
                done.wait_send()
                done.wait_recv()

    res = pl.pallas_call(
        body, name=name,
        out_shape=tuple(pltpu.HBM(a.shape, a.dtype) for a in arrays),
        in_specs=[HBM] * len(arrays) + [SEM] * len(sem_ops) + [ANY],
        out_specs=tuple([HBM] * len(arrays)),
        input_output_aliases={i: i for i in range(len(arrays))},
        compiler_params=pltpu.CompilerParams(has_side_effects=DATAFLOW))(*arrays, *sem_ops, after)
    return list(res[:n_b]), list(res[n_b:])


def gather_forward(bufs, *, name):
    n = len(bufs)

    def body(*refs):
        ins, outs = refs[:n], refs[n:2 * n]
        send_sems, recv_sems = refs[2 * n:]
        x, y, c = _place()
        chips = _other_chips(x, y)
        for t in range(n):
            for k, (px, py) in enumerate(chips):
                pltpu.make_async_remote_copy(
                    src_ref=_half(ins[t], 2 * px + py, c), dst_ref=_half(outs[t], 2 * px + py, c),
                    send_sem=send_sems.at[3 * t + k], recv_sem=recv_sems.at[3 * t + k],
                    device_id=(x, y, 1 - c), device_id_type=MESH).start()
        for t in range(n):
            for k, (px, py) in enumerate(chips):
                done = pltpu.make_async_remote_copy(
                    src_ref=_half(ins[t], 2 * px + py, c), dst_ref=_half(outs[t], 2 * px + py, 1 - c),
                    send_sem=send_sems.at[3 * t + k], recv_sem=recv_sems.at[3 * t + k],
                    device_id=(x, y, 1 - c), device_id_type=MESH)
                done.wait_send()
                done.wait_recv()

    return pl.pallas_call(
        body, name=name, in_specs=[ANY] * n, out_specs=[ANY] * n,
        out_shape=[jax.ShapeDtypeStruct(a.shape, a.dtype) for a in bufs],
        input_output_aliases={i: i for i in range(n)},
        scratch_shapes=[pltpu.SemaphoreType.DMA((3 * n,)), pltpu.SemaphoreType.DMA((3 * n,))],
        compiler_params=pltpu.CompilerParams(has_side_effects=True))(*bufs)


def sum_stage_a(grad, recv, place, wire, *, name):
    j_n, half, cols = recv.shape

    def body(place_ref, g_ref, r_ref, o_ref, ob_ref):
        acc = g_ref[...] + r_ref[...]
        ob_ref[...] = acc.astype(wire)

        @pl.when(pl.program_id(0) == place_ref[0])
        def _():
            o_ref[...] = acc

    blk = (None, half, cols)
    return pl.pallas_call(
        body, name=name,
        grid_spec=pltpu.PrefetchScalarGridSpec(
            num_scalar_prefetch=1, grid=(j_n,),
            in_specs=[pl.BlockSpec(blk, lambda j, place_ref: (j, place_ref[1], 0)),
                      pl.BlockSpec(blk, lambda j, place_ref: (j, 0, 0))],
            out_specs=[pl.BlockSpec((half, cols), lambda j, place_ref: (0, 0)),
                       pl.BlockSpec(blk, lambda j, place_ref: (j, 0, 0))]),
        out_shape=[jax.ShapeDtypeStruct((half, cols), F32), jax.ShapeDtypeStruct(recv.shape, wire)],
        compiler_params=_params("arbitrary"))(place, grad, recv)


def _stage_a_copies(srcs, lands, x, y, c):
    out = []
    for src, land in zip(srcs, lands):
        half = src.shape[1] // 2
        out.append((src.at[:, pl.ds((1 - c) * half, half), :], land, (x, y, 1 - c)))
    return out


def _stage_b_copies(srcs, lands, x, y, c):
    out = []
    for src, land in zip(srcs, lands):
        for k, (px, py) in enumerate(_other_chips(x, y)):
            out.append((src.at[2 * px + py], land.at[k], (px, py, c)))
    return out


def _forward_copies(bufs, _, x, y, c):
    out = []
    for buf in bufs:
        for px, py in _other_chips(x, y):
            out.append((_half(buf, 2 * px + py, c), _half(buf, 2 * px + py, c), (x, y, 1 - c)))
    return out


def _stage_c_copies(fulls, _, x, y, c):
    out = []
    for full in fulls:
        half = full.shape[0] // 2
        mine = full.at[pl.ds(c * half, half), :]
        out.append((mine, mine, (x, y, 1 - c)))
    return out


def _chip_block_copies(bufs, _, x, y, c):
    me = 2 * x + y
    return [(buf.at[me], buf.at[me], (px, py, c)) for buf in bufs for px, py in _other_chips(x, y)]


def split_start(srcs, lands, copies, *, name):
    n, n_all = len(srcs), len(srcs) + len(lands)
    n_c = len(copies(srcs, lands, 0, 0, 0))

    def body(*refs):
        src_refs, land_refs = refs[:n], refs[n:n_all]
        send_sems, recv_sems = refs[n_all], refs[n_all + 1]
        token = refs[-1]
        x, y, c = _place()
        for k, (src, dst, target) in enumerate(copies(src_refs, land_refs, x, y, c)):
            pltpu.make_async_remote_copy(src_ref=src, dst_ref=dst, send_sem=send_sems.at[k], recv_sem=recv_sems.at[k],
                                         device_id=target, device_id_type=MESH).start()
        token[...] = jnp.zeros_like(token)

    arrays = list(srcs) + list(lands)
    res = pl.pallas_call(
        body, name=name,
        out_shape=(pltpu.SemaphoreType.DMA((n_c,)), pltpu.SemaphoreType.DMA((n_c,)))
        + tuple(pltpu.HBM(a.shape, a.dtype) for a in arrays) + (jax.ShapeDtypeStruct((8, 128), F32),),
        in_specs=[HBM] * n_all,
        out_specs=tuple([SEM, SEM] + [HBM] * n_all + [pl.BlockSpec(memory_space=pltpu.VMEM)]),
        input_output_aliases={i: 2 + i for i in range(n_all)},
        compiler_params=pltpu.CompilerParams(has_side_effects=DATAFLOW))(*[_in_hbm(a) for a in arrays])
    return res[0], res[1], list(res[2:2 + n]), list(res[2 + n:2 + n_all]), res[-1]


def split_wait(srcs, lands, send, recv, copies, after, *, name):
    n, n_all = len(srcs), len(srcs) + len(lands)

    def body(*refs):
        src_refs, land_refs = refs[:n], refs[n:n_all]
        send_sems, recv_sems = refs[n_all], refs[n_all + 1]
        x, y, c = _place()
        for k, (src, dst, target) in enumerate(copies(src_refs, land_refs, x, y, c)):
            done = pltpu.make_async_remote_copy(src_ref=src, dst_ref=dst, send_sem=send_sems.at[k],
                                                recv_sem=recv_sems.at[k], device_id=target, device_id_type=MESH)
            done.wait_send()
            done.wait_recv()

    arrays = list(srcs) + list(lands)
    res = pl.pallas_call(
        body, name=name,
        out_shape=tuple(pltpu.HBM(a.shape, a.dtype) for a in arrays),
        in_specs=[HBM] * n_all + [SEM, SEM, ANY],
        out_specs=tuple([HBM] * n_all),
        input_output_aliases={i: i for i in range(n_all)},
        compiler_params=pltpu.CompilerParams(has_side_effects=DATAFLOW))(*arrays, send, recv, after)
    return list(res[:n]), list(res[n:])


def sum_stage_b(part, recv, place, *, name):
    half, cols = part.shape

    def body(place_ref, p_ref, r_ref, o_ref):
        acc = p_ref[...]
        for k in range(3):
            acc = acc + r_ref[k].astype(F32)
        o_ref[...] = acc

    return pl.pallas_call(
        body, name=name,
        grid_spec=pltpu.PrefetchScalarGridSpec(
            num_scalar_prefetch=1, grid=(1,),
            in_specs=[pl.BlockSpec((half, cols), lambda i, place_ref: (0, 0)),
                      pl.BlockSpec((3, half, cols), lambda i, place_ref: (0, 0, 0))],
            out_specs=pl.BlockSpec((half, cols), lambda i, place_ref: (place_ref[1], 0))),
        out_shape=jax.ShapeDtypeStruct((2 * half, cols), F32),
        compiler_params=_params("arbitrary"))(place, part, recv)


def reduce_stage_c(fulls, *, name):
    n = len(fulls)

    def body(*refs):
        ins, outs = refs[:n], refs[n:2 * n]
        send_sems, recv_sems = refs[2 * n:]
        x, y, c = _place()
        for t in range(n):
            half = ins[t].shape[0] // 2
            pltpu.make_async_remote_copy(
                src_ref=ins[t].at[pl.ds(c * half, half), :], dst_ref=outs[t].at[pl.ds(c * half, half), :],
                send_sem=send_sems.at[t], recv_sem=recv_sems.at[t],
                device_id=(x, y, 1 - c), device_id_type=MESH).start()
        for t in range(n):
            half = ins[t].shape[0] // 2
            done = pltpu.make_async_remote_copy(
                src_ref=ins[t].at[pl.ds(c * half, half), :], dst_ref=outs[t].at[pl.ds((1 - c) * half, half), :],
                send_sem=send_sems.at[t], recv_sem=recv_sems.at[t],
                device_id=(x, y, 1 - c), device_id_type=MESH)
            done.wait_send()
            done.wait_recv()

    return pl.pallas_call(
        body, name=name, in_specs=[ANY] * n, out_specs=[ANY] * n,
        out_shape=[jax.ShapeDtypeStruct(a.shape, a.dtype) for a in fulls],
        input_output_aliases={i: i for i in range(n)},
        scratch_shapes=[pltpu.SemaphoreType.DMA((n,)), pltpu.SemaphoreType.DMA((n,))],
        compiler_params=pltpu.CompilerParams(has_side_effects=True))(*fulls)


def gather_chip_blocks(slots, *, name):
    def body(in_ref, out_ref, send_sems, recv_sems):
        x, y, c = _place()
        me = 2 * x + y
        chips = _other_chips(x, y)
        for k, (px, py) in enumerate(chips):
            pltpu.make_async_remote_copy(
                src_ref=in_ref.at[me], dst_ref=out_ref.at[me],
                send_sem=send_sems.at[k], recv_sem=recv_sems.at[k],
                device_id=(px, py, c), device_id_type=MESH).start()
        for k, (px, py) in enumerate(chips):
            done = pltpu.make_async_remote_copy(
                src_ref=in_ref.at[me], dst_ref=out_ref.at[2 * px + py],
                send_sem=send_sems.at[k], recv_sem=recv_sems.at[k],
                device_id=(px, py, c), device_id_type=MESH)
            done.wait_send()
            done.wait_recv()

    return pl.pallas_call(
        body, name=name, in_specs=[ANY], out_specs=ANY,
        out_shape=jax.ShapeDtypeStruct(slots.shape, slots.dtype),
        input_output_aliases={0: 0},
        scratch_shapes=[pltpu.SemaphoreType.DMA((3,)), pltpu.SemaphoreType.DMA((3,))],
        compiler_params=pltpu.CompilerParams(has_side_effects=True))(slots)


def _ffn_bwd(dh, dh_bf, h, gain, saved, wg, wu, wd, cw, cb, place, l, pin):
    hn, rstd, g, up, gc = saved
    dg, dup, dwg, dwu, dwd, dcw, dcb = ffn_bwd_a(dh_bf, hn, g, up, gc, wd, cw, name=f"ffn{l}_bwd_a", pin=pin)
    red = _reduce_a_begin([(dwg, BF16), (dwu, BF16), (dwd, BF16)], tag=f"f{l}")
    dh_in, dh_in_bf, dgain = dx_rms_bwd([(dg, wg), (dup, wu)], h, gain, rstd, dh, name=f"ffn{l}_bwd_b",
                                        pin=red[-1])
    red = _reduce_b_begin(red, place, dh_in_bf, tag=f"f{l}")
    return dh_in, dh_in_bf, (dcw, dcb, dgain), red


def _reduce_a_begin(grads, *, tag):
    lands = [lax.empty((g.shape[0], g.shape[1] // 2, g.shape[2]), F32) for g, _ in grads]
    send, recv, srcs, lands, token = split_start([g for g, _ in grads], lands, _stage_a_copies,
                                                 name=f"reduce_a_start_{tag}")
    return [w for _, w in grads], send, recv, srcs, lands, token


def _reduce_b_begin(state, place, after, *, tag):
    wires, send, recv, srcs, lands, _ = state
    grads, recv_a = split_wait(srcs, lands, send, recv, _stage_a_copies, after, name=f"reduce_a_wait_{tag}")
    parts = [sum_stage_a(g, r, place, w, name=f"sum_a_{tag}{i}") for i, (g, r, w) in enumerate(zip(grads, recv_a, wires))]
    lands_b = [lax.empty((3,) + p[1].shape[1:], p[1].dtype) for p in parts]
    send, recv, srcs, lands, token = split_start([p[1] for p in parts], lands_b, _stage_b_copies,
                                                 name=f"reduce_b_start_{tag}")
    return [p[0] for p in parts], send, recv, srcs, lands, token


def _reduce_c_begin(state, place, after, *, tag):
    parts, send, recv, srcs, lands, _ = state
    _, recv_b = split_wait(srcs, lands, send, recv, _stage_b_copies, after, name=f"reduce_b_wait_{tag}")
    halves = [sum_stage_b(p, r, place, name=f"sum_b_{tag}{i}") for i, (p, r) in enumerate(zip(parts, recv_b))]
    send, recv, fulls, _, token = split_start(halves, [], _stage_c_copies, name=f"reduce_c_start_{tag}")
    return send, recv, fulls, token


def _reduce_finish(state, after, *, tag):
    send, recv, fulls, _ = state
    fulls, _ = split_wait(fulls, [], send, recv, _stage_c_copies, after, name=f"reduce_c_wait_{tag}")
    return fulls


def kernel(x, norm_mix, norm_ffn, final_norm, w_in_even, conv_a, w_pool, pool_scale, w_out_even, w_in_odd, sgu_norm, w_spatial, b_spatial, w_out_odd, w_ffn_gate, w_ffn_up, conv_ffn, b_conv_ffn, w_ffn_down, loss_target, m_norm_mix, m_norm_ffn, m_final_norm, m_w_in_even, m_conv_a, m_w_pool, m_pool_scale, m_w_out_even, m_w_in_odd, m_sgu_norm, m_w_spatial, m_b_spatial, m_w_out_odd, m_w_ffn_gate, m_w_ffn_up, m_conv_ffn, m_b_conv_ffn, m_w_ffn_down, v_norm_mix, v_norm_ffn, v_final_norm, v_w_in_even, v_conv_a, v_w_pool, v_pool_scale, v_w_out_even, v_w_in_odd, v_sgu_norm, v_w_spatial, v_b_spatial, v_w_out_odd, v_w_ffn_gate, v_w_ffn_up, v_conv_ffn, v_b_conv_ffn, v_w_ffn_down):
    weights = dict(norm_mix=norm_mix, norm_ffn=norm_ffn, final_norm=final_norm, w_in_even=w_in_even,
                   conv_a=conv_a, w_pool=w_pool, pool_scale=pool_scale, w_out_even=w_out_even,
                   w_in_odd=w_in_odd, sgu_norm=sgu_norm, w_spatial=w_spatial, b_spatial=b_spatial,
                   w_out_odd=w_out_odd, w_ffn_gate=w_ffn_gate, w_ffn_up=w_ffn_up, conv_ffn=conv_ffn,
                   b_conv_ffn=b_conv_ffn, w_ffn_down=w_ffn_down)
    m_in = dict(norm_mix=m_norm_mix, norm_ffn=m_norm_ffn, final_norm=m_final_norm, w_in_even=m_w_in_even,
                conv_a=m_conv_a, w_pool=m_w_pool, pool_scale=m_pool_scale, w_out_even=m_w_out_even,
                w_in_odd=m_w_in_odd, sgu_norm=m_sgu_norm, w_spatial=m_w_spatial, b_spatial=m_b_spatial,
                w_out_odd=m_w_out_odd, w_ffn_gate=m_w_ffn_gate, w_ffn_up=m_w_ffn_up, conv_ffn=m_conv_ffn,
                b_conv_ffn=m_b_conv_ffn, w_ffn_down=m_w_ffn_down)
    v_in = dict(norm_mix=v_norm_mix, norm_ffn=v_norm_ffn, final_norm=v_final_norm, w_in_even=v_w_in_even,
                conv_a=v_conv_a, w_pool=v_w_pool, pool_scale=v_pool_scale, w_out_even=v_w_out_even,
                w_in_odd=v_w_in_odd, sgu_norm=v_sgu_norm, w_spatial=v_w_spatial, b_spatial=v_b_spatial,
                w_out_odd=v_w_out_odd, w_ffn_gate=v_w_ffn_gate, w_ffn_up=v_w_ffn_up, conv_ffn=v_conv_ffn,
                b_conv_ffn=v_b_conv_ffn, w_ffn_down=v_w_ffn_down)
    order = list(weights)

    chip = 2 * lax.axis_index("x") + lax.axis_index("y")
    core = lax.axis_index("c")
    place = jnp.stack([chip, core]).astype(jnp.int32)
    chip_arr = place[:1]

    h0 = x[0]
    target = loss_target[0]
    d_model = h0.shape[1]
    f_shard = w_ffn_gate.shape[-1]

    def turned(a):
        return jnp.transpose(a, (0, 2, 1))

    def own_slot(v):
        return lax.dynamic_update_index_in_dim(jnp.zeros((N_CHIPS,) + v.shape, v.dtype), v, chip, 0)

    groups = [
        [cast_into_slot(w_in_even[0], chip_arr, name="cast_win_e"),
         cast_into_slot(w_out_even[0], chip_arr, name="cast_wout_e")],
        [cast_into_slot(turned(w_ffn_gate), chip_arr, l=0, name="cast_wg0"),
         cast_into_slot(turned(w_ffn_up), chip_arr, l=0, name="cast_wu0")],
        [cast_into_slot(w_ffn_down, chip_arr, l=0, name="cast_wd0")],
        [cast_into_slot(w_in_odd[0], chip_arr, name="cast_win_o"),
         cast_into_slot(w_out_odd[0], chip_arr, name="cast_wout_o")],
        [cast_into_slot(turned(w_ffn_gate), chip_arr, l=1, name="cast_wg1"),
         cast_into_slot(turned(w_ffn_up), chip_arr, l=1, name="cast_wu1")],
        [cast_into_slot(w_ffn_down, chip_arr, l=1, name="cast_wd1")]]
    smalls = [own_slot(conv_a[0]), own_slot(sgu_norm), own_slot(conv_ffn[0]), own_slot(conv_ffn[1])]
    sems, groups, smalls, token = gather_start(groups, smalls)

    def arrive(gi, after, with_smalls=False):
        kw = dict(smalls=smalls, small_send=sems[-2], small_recv=sems[-1]) if with_smalls else {}
        bufs, small_out = gather_wait(groups[gi], sems[2 * gi], sems[2 * gi + 1], after, name=f"gather_wait{gi}", **kw)
        return gather_forward(bufs, name=f"gather_forward{gi}"), small_out

    def arrive_begin(gi, after):
        bufs, _ = gather_wait(groups[gi], sems[2 * gi], sems[2 * gi + 1], after, name=f"gather_wait{gi}")
        send, recv, bufs, _, tok = split_start(bufs, [], _forward_copies, name=f"gather_forward_start{gi}")
        return (send, recv, bufs), tok

    def arrive_end(state, after, gi):
        send, recv, bufs = state
        return split_wait(bufs, [], send, recv, _forward_copies, after, name=f"gather_forward_wait{gi}")[0]

    cb = b_conv_ffn.reshape(-1, N_CHIPS, 1, f_shard)
    wp_bf = w_pool[0].astype(BF16)
    wp_t_bf = jnp.transpose(w_pool[0], (0, 2, 1)).astype(BF16)
    ws = w_spatial[0]
    bs = b_spatial[0][:, :, None]

    (win_e, wout_e), (ca_g, sn_g, cw0, cw1) = arrive(0, token, with_smalls=True)
    wout_e = wout_e.reshape(-1, d_model)
    ca_full = jnp.transpose(ca_g, (1, 0, 2)).reshape(ca_g.shape[1], -1)
    sn_full = sn_g.reshape(1, -1)
    h1, xn0, rstd0, proj0, mix0, hn0, rstdf0 = even_layer_fwd(
        h0, norm_mix[0:1], win_e, ca_full, wp_bf, pool_scale, wout_e, norm_ffn[0:1], name="l0_fwd")
    (wg0, wu0), _ = arrive(1, hn0)
    g0, up0, gc0, act0 = ffn_in_fwd(hn0, wg0, wu0, cw0, cb[0], name="ffn0_in")
    swap2, tok2 = arrive_begin(2, act0)
    swap3, tok3 = arrive_begin(3, tok2)
    (wd0,) = arrive_end(swap2, tok3, 2)
    h2 = mm_acc(act0, wd0, h1, name="ffn0_down")
    ffn0 = (hn0, rstdf0, g0, up0, gc0)
    swap4, tok4 = arrive_begin(4, h2)
    swap5, tok5 = arrive_begin(5, tok4)
    win_o, wout_o = arrive_end(swap3, tok5, 3)
    wout_o = wout_o.reshape(-1, d_model)
    h3, xn1, rstd1, p1, mix1, rstd_v = odd_layer_fwd(h2, norm_mix[1:2], win_o, sn_full, ws, bs, wout_o, name="l1_fwd")
    wg1, wu1 = arrive_end(swap4, h3, 4)
    (wd1,) = arrive_end(swap5, wg1, 5)
    dh4, dh4_bf, loss_row, d_final, hn1, rstdf1, g1, up1, gc1 = ffn_loss_fwd(
        h3, norm_ffn[1:2], wg1, wu1, wd1, cw1, cb[1], target, final_norm[None], name="ffn1_fwd_loss")
    ffn1 = (hn1, rstdf1, g1, up1, gc1)

    loss = lax.psum(loss_row[0, 0], ("x", "y", "c"))

    dh3, dh3_bf, (dcw1, dcb1, dnf1), red3 = _ffn_bwd(
        dh4, dh4_bf, h3, norm_ffn[1:2], ffn1, wg1, wu1, wd1, cw1, cb[1], place, 1, None)

    def as_blocks(a):
        return a.reshape(N_CHIPS, -1, d_model)

    dp1, dsn, dws, dbs = sgu_bwd(p1, dh3_bf, wout_o, rstd_v, sn_full, ws, bs, name="l1_mix_bwd", pin=red3[-1])
    dwout_o = as_blocks(mm_tn(mix1[None], dh3_bf[None], name="l1_dwout"))
    dwin_o = mm_tn_shared(xn1, dp1, name="l1_dwin")
    red2 = _reduce_a_begin([(dwin_o, BF16), (dwout_o, BF16)], tag="m1")
    dh2, dh2_bf, dnm1 = dx_rms_bwd([(dp1, win_o)], h2, norm_mix[1:2], rstd1, dh3, name="l1_dx", pin=red2[-1])
    red2 = _reduce_b_begin(red2, place, dh2_bf, tag="m1")

    dh1, dh1_bf, (dcw0, dcb0, dnf0), red1 = _ffn_bwd(
        dh2, dh2_bf, h1, norm_ffn[0:1], ffn0, wg0, wu0, wd0, cw0, cb[0], place, 0, red2[-1])

    dproj0, dca, dwp, dps = even_bwd(proj0, dh1_bf, wout_e, ca_full, wp_bf, wp_t_bf, pool_scale,
                                     name="l0_mix_bwd", pin=red1[-1])
    dwout_e = as_blocks(mm_tn(mix0[None], dh1_bf[None], name="l0_dwout"))
    dwin_e = mm_tn_shared(xn0, dproj0, name="l0_dwin")
    dh0, _, dnm0 = dx_rms_bwd([(dproj0, win_e)], h0, norm_mix[0:1], rstd0, dh1, name="l0_dx")
    grad_x = dh0[None]

    small_parts = dict(
        norm_mix=jnp.concatenate([dnm0, dnm1]), norm_ffn=jnp.concatenate([dnf0, dnf1]), final_norm=d_final,
        conv_a=dca, w_pool=dwp, pool_scale=dps, sgu_norm=dsn, w_spatial=dws, b_spatial=dbs,
        conv_ffn=jnp.stack([dcw0, dcw1]), b_conv_ffn=jnp.stack([dcb0, dcb1]))
    flat = jnp.concatenate([v.reshape(-1) for v in small_parts.values()])
    pad = (-flat.shape[0]) % (N_CHIPS * 32 * 128)
    small = jnp.pad(flat, (0, pad)).reshape(N_CHIPS, -1, 128)
    red0 = _reduce_a_begin([(dwin_e, BF16), (dwout_e, BF16), (small, F32)], tag="m0")

    swap_f1 = _reduce_c_begin(red3, place, red0[-1], tag="f1")
    red0 = _reduce_b_begin(red0, place, swap_f1[-1], tag="m0")
    swap_m1 = _reduce_c_begin(red2, place, red0[-1], tag="m1")
    swap_f0 = _reduce_c_begin(red1, place, swap_m1[-1], tag="f0")
    full3 = _reduce_finish(swap_f1, swap_f0[-1], tag="f1")
    full2 = _reduce_finish(swap_m1, full3[0], tag="m1")
    full1 = _reduce_finish(swap_f0, full2[0], tag="f0")
    swap_m0 = _reduce_c_begin(red0, place, full1[0], tag="m0")
    full0 = _reduce_finish(swap_m0, swap_m0[-1], tag="m0")
    small_slots = lax.dynamic_update_index_in_dim(jnp.zeros(small.shape, F32), full0[2], chip, 0)
    small_sum = gather_chip_blocks(small_slots, name="gather_small").reshape(-1)
    grads = {
        "w_in_even": full0[0][None], "w_out_even": full0[1][None],
        "w_in_odd": full2[0][None], "w_out_odd": full2[1][None],
        }
    layered = {"w_ffn_gate": [full1[0], full3[0]], "w_ffn_up": [full1[1], full3[1]],
               "w_ffn_down": [full1[2], full3[2]]}
    off = 0
    small_red = {}
    for nm, v in small_parts.items():
        small_red[nm] = small_sum[off:off + v.size].reshape(v.shape)
        off += v.size
    for nm in ("norm_mix", "norm_ffn", "pool_scale"):
        grads[nm] = small_red[nm].reshape(weights[nm].shape)
    grads["final_norm"] = small_red["final_norm"].reshape(weights["final_norm"].shape)
    grads["w_pool"] = small_red["w_pool"][None]
    grads["w_spatial"] = small_red["w_spatial"][None]
    grads["b_spatial"] = small_red["b_spatial"].reshape(weights["b_spatial"].shape)
    grads["b_conv_ffn"] = small_red["b_conv_ffn"].reshape(weights["b_conv_ffn"].shape)
    grads["conv_a"] = lax.dynamic_slice_in_dim(small_red["conv_a"], chip * conv_a.shape[-1], conv_a.shape[-1], 1)[None]
    grads["sgu_norm"] = lax.dynamic_slice_in_dim(small_red["sgu_norm"], chip * sgu_norm.shape[-1], sgu_norm.shape[-1], 1)
    grads["conv_ffn"] = lax.dynamic_index_in_dim(small_red["conv_ffn"], chip, 1, keepdims=False)

    deltas, new_m, new_v = {}, {}, {}
    for nm, per_layer in layered.items():
        if nm == "w_ffn_down":
            grads[nm], deltas[nm], new_m[nm], new_v[nm] = adamw_layers(
                weights[nm], per_layer, m_in[nm], v_in[nm], name=f"adamw_{nm}")
        else:
            outs = adamw_layers(turned(weights[nm]), per_layer, turned(m_in[nm]), turned(v_in[nm]),
                                name=f"adamw_{nm}")
            grads[nm], deltas[nm], new_m[nm], new_v[nm] = (turned(o) for o in outs)
    for nm in order:
        if nm in layered:
            continue
        w = weights[nm]
        w2 = w[None] if w.ndim == 1 else w
        shp = w2.shape
        d, nm_, nv_ = adamw(w2, grads[nm].reshape(shp), m_in[nm].reshape(shp), v_in[nm].reshape(shp),
                            name=f"adamw_{nm}")
        deltas[nm], new_m[nm], new_v[nm] = d.reshape(w.shape), nm_.reshape(w.shape), nv_.reshape(w.shape)

    return (loss, grad_x, *[grads[n] for n in order], *[deltas[n] for n in order],
            *[new_m[n] for n in order], *[new_v[n] for n in order])
```

```python
import jax
import jax.numpy as jnp
from jax import lax
from jax.experimental import pallas as pl
from jax.experimental.pallas import tpu as pltpu

F32 = jnp.float32
BF16 = jnp.bfloat16
MESH = pl.DeviceIdType.MESH
ANY = pl.BlockSpec(memory_space=pl.ANY)

EPS = 1e-6
POOL_WINDOWS = (2, 4, 8, 16)
GROUP = 128
CHUNK = 128
N_CHIPS = 4
N_DEV = 8
ROW_TILE = 512
HALO = 16
VMEM_LIMIT = 56 * 1024 * 1024

ADAM_LR = 0.001
ADAM_B1 = 0.9
ADAM_B2 = 0.999
ADAM_EPS = 1e-08
ADAM_WD = 0.01
ADAM_STEP = 10


def _params(*sem):
    return pltpu.CompilerParams(dimension_semantics=sem, vmem_limit_bytes=VMEM_LIMIT)


def _layer_spec(block, l, idx):
    if l is None:
        return pl.BlockSpec(block, idx)
    return pl.BlockSpec((None,) + block, lambda *g: (l,) + idx(*g))


def mm_nn(a, b, *, l=None, name):
    s, k = a.shape
    j_n, n = b.shape[-3], b.shape[-1]
    tm = min(ROW_TILE, s)

    def body(a_ref, b_ref, o_ref):
        o_ref[...] = jnp.dot(a_ref[...], b_ref[...], preferred_element_type=F32)

    return pl.pallas_call(
        body, name=name, grid=(j_n, s // tm),
        in_specs=[pl.BlockSpec((tm, k), lambda j, i: (i, 0)),
                  _layer_spec((None, k, n), l, lambda j, i: (j, 0, 0))],
        out_specs=pl.BlockSpec((None, tm, n), lambda j, i: (j, i, 0)),
        out_shape=jax.ShapeDtypeStruct((j_n, s, n), F32),
        compiler_params=_params("parallel", "parallel"))(a, b)


def mm_acc(a, b, res, *, l=None, name):
    j_n, s, kj = a.shape
    n = b.shape[-1]
    tm = min(ROW_TILE, s)

    def body(a_ref, b_ref, r_ref, o_ref):
        acc = r_ref[...]
        for j in range(j_n):
            acc = acc + jnp.dot(a_ref[j], b_ref[j], preferred_element_type=F32)
        o_ref[...] = acc

    return pl.pallas_call(
        body, name=name, grid=(s // tm,),
        in_specs=[pl.BlockSpec((j_n, tm, kj), lambda i: (0, i, 0)),
                  _layer_spec((j_n, kj, n), l, lambda i: (0, 0, 0)),
                  pl.BlockSpec((tm, n), lambda i: (i, 0))],
        out_specs=pl.BlockSpec((tm, n), lambda i: (i, 0)),
        out_shape=jax.ShapeDtypeStruct((s, n), F32),
        compiler_params=_params("parallel"))(a, b, res)


_NT = (((1,), (1,)), ((), ()))
_TN = (((0,), (0,)), ((), ()))


def mm_nt_sum(pairs, *, l=None, name):
    j_n, s, nj = pairs[0][0].shape
    k = pairs[0][1].shape[-2]
    tm = min(ROW_TILE, s)
    n_p = len(pairs)

    def body(*refs):
        o_ref = refs[-1]
        acc = jnp.zeros((tm, k), F32)
        for p in range(n_p):
            dy_ref, w_ref = refs[2 * p], refs[2 * p + 1]
            for j in range(j_n):
                acc = acc + lax.dot_general(dy_ref[j], w_ref[j], _NT, preferred_element_type=F32)
        o_ref[...] = acc

    in_specs, args = [], []
    for dy, w in pairs:
        in_specs += [pl.BlockSpec((j_n, tm, nj), lambda i: (0, i, 0)),
                     _layer_spec((j_n, k, nj), l, lambda i: (0, 0, 0))]
        args += [dy, w]
    return pl.pallas_call(
        body, name=name, grid=(s // tm,), in_specs=in_specs,
        out_specs=pl.BlockSpec((tm, k), lambda i: (i, 0)),
        out_shape=jax.ShapeDtypeStruct((s, k), F32),
        compiler_params=_params("parallel"))(*args)


def mm_nt_each(a, b, *, l=None, name, pin=None):
    s, k = a.shape
    j_n, nj = b.shape[-3], b.shape[-2]
    tm = min(ROW_TILE, s)

    def body(a_ref, b_ref, *rest):
        rest[-1][...] = lax.dot_general(a_ref[...], b_ref[...], _NT, preferred_element_type=F32)

    return pl.pallas_call(
        body, name=name, grid=(j_n, s // tm),
        in_specs=[pl.BlockSpec((tm, k), lambda j, i: (i, 0)),
                  _layer_spec((None, nj, k), l, lambda j, i: (j, 0, 0))] + ([ANY] if pin is not None else []),
        out_specs=pl.BlockSpec((None, tm, nj), lambda j, i: (j, i, 0)),
        out_shape=jax.ShapeDtypeStruct((j_n, s, nj), F32),
        compiler_params=_params("parallel", "parallel"))(a, b, *([pin] if pin is not None else []))


def mm_tn(a, dy, *, name, pin=None):
    ja, s, k = a.shape
    jd, _, n = dy.shape
    j_n = max(ja, jd)
    tk = min(ROW_TILE, s)

    def body(a_ref, d_ref, *rest):
        o_ref = rest[-1]

        @pl.when(pl.program_id(1) == 0)
        def _():
            o_ref[...] = jnp.zeros_like(o_ref)
        o_ref[...] += lax.dot_general(a_ref[...], d_ref[...], _TN, preferred_element_type=F32)

    return pl.pallas_call(
        body, name=name, grid=(j_n, s // tk),
        in_specs=[pl.BlockSpec((None, tk, k), (lambda j, i: (j, i, 0)) if ja > 1 else (lambda j, i: (0, i, 0))),
                  pl.BlockSpec((None, tk, n), (lambda j, i: (j, i, 0)) if jd > 1 else (lambda j, i: (0, i, 0)))]
        + ([ANY] if pin is not None else []),
        out_specs=pl.BlockSpec((None, k, n), lambda j, i: (j, 0, 0)),
        out_shape=jax.ShapeDtypeStruct((j_n, k, n), F32),
        compiler_params=_params("parallel", "arbitrary"))(a, dy, *([pin] if pin is not None else []))


def mm_tn_shared(a, dy, *, name):
    s, k = a.shape
    j_n, _, n = dy.shape
    tk = min(ROW_TILE, s)

    def body(a_ref, d_ref, o_ref):
        @pl.when(pl.program_id(0) == 0)
        def _():
            o_ref[...] = jnp.zeros_like(o_ref)
        a_t = a_ref[...]
        for j in range(j_n):
            o_ref[j] += lax.dot_general(a_t, d_ref[j], _TN, preferred_element_type=F32)

    return pl.pallas_call(
        body, name=name, grid=(s // tk,),
        in_specs=[pl.BlockSpec((tk, k), lambda i: (i, 0)), pl.BlockSpec((j_n, tk, n), lambda i: (0, i, 0))],
        out_specs=pl.BlockSpec((j_n, k, n), lambda i: (0, 0, 0)),
        out_shape=jax.ShapeDtypeStruct((j_n, k, n), F32),
        compiler_params=_params("arbitrary"))(a, dy)


def _back(x, k):
    return pltpu.roll(x, k, 0)


def _fwd(x, k):
    return pltpu.roll(x, x.shape[0] - k, 0)


def _causal_conv(x, w_ref):
    return w_ref[0:1, :] * _back(x, 2) + w_ref[1:2, :] * _back(x, 1) + w_ref[2:3, :] * x


def _causal_conv_t(dy, w_ref):
    return w_ref[2:3, :] * dy + w_ref[1:2, :] * _fwd(dy, 1) + w_ref[0:1, :] * _fwd(dy, 2)


def _gelu(x):
    return 0.5 * x * (1.0 + lax.erf(x * 0.7071067811865476))


def _gelu_grad(x):
    return 0.5 * (1.0 + lax.erf(x * 0.7071067811865476)) + x * jnp.exp(-0.5 * x * x) * 0.3989422804014327


def _colsum(x):
    return jnp.sum(x, axis=0, keepdims=True)


def rms_fwd(h, gain, *, name, pin=None):
    s, d = h.shape
    ts = min(ROW_TILE, s)

    def body(h_ref, g_ref, *rest):
        o_ref, r_ref = rest[-2:]
        x = h_ref[...]
        rstd = lax.rsqrt(jnp.mean(x * x, axis=-1, keepdims=True) + EPS)
        o_ref[...] = (x * rstd * g_ref[...]).astype(BF16)
        r_ref[...] = rstd

    return pl.pallas_call(
        body, name=name, grid=(s // ts,),
        in_specs=[pl.BlockSpec((ts, d), lambda i: (i, 0)), pl.BlockSpec((1, d), lambda i: (0, 0))]
        + ([ANY] if pin is not None else []),
        out_specs=[pl.BlockSpec((ts, d), lambda i: (i, 0)), pl.BlockSpec((ts, 1), lambda i: (i, 0))],
        out_shape=[jax.ShapeDtypeStruct((s, d), BF16), jax.ShapeDtypeStruct((s, 1), F32)],
        compiler_params=_params("parallel"))(h, gain, *([pin] if pin is not None else []))


def rms_bwd(dxn, h, gain, rstd, dres, *, name):
    s, d = h.shape
    ts = min(ROW_TILE, s)

    def body(dx_ref, h_ref, g_ref, r_ref, dr_ref, o_ref, ob_ref, dg_ref):
        @pl.when(pl.program_id(0) == 0)
        def _():
            dg_ref[...] = jnp.zeros_like(dg_ref)
        rstd_v = r_ref[...]
        hhat = h_ref[...] * rstd_v
        dx = dx_ref[...]
        dg_ref[...] += _colsum(dx * hhat)
        dxg = dx * g_ref[...]
        dh = dr_ref[...] + rstd_v * (dxg - hhat * jnp.mean(dxg * hhat, axis=-1, keepdims=True))
        o_ref[...] = dh
        ob_ref[...] = dh.astype(BF16)

    row = pl.BlockSpec((ts, d), lambda i: (i, 0))
    vec = pl.BlockSpec((1, d), lambda i: (0, 0))
    return pl.pallas_call(
        body, name=name, grid=(s // ts,),
        in_specs=[row, row, vec, pl.BlockSpec((ts, 1), lambda i: (i, 0)), row],
        out_specs=[row, row, vec],
        out_shape=[jax.ShapeDtypeStruct((s, d), F32), jax.ShapeDtypeStruct((s, d), BF16),
                   jax.ShapeDtypeStruct((1, d), F32)],
        compiler_params=_params("arbitrary"))(dxn, h, gain, rstd, dres)


def final_loss(h, target, gain, *, name):
    s, d = h.shape
    ts = min(ROW_TILE, s)

    def body(h_ref, t_ref, g_ref, o_ref, ob_ref, l_ref, dg_ref):
        @pl.when(pl.program_id(0) == 0)
        def _():
            l_ref[...] = jnp.zeros_like(l_ref)
            dg_ref[...] = jnp.zeros_like(dg_ref)
        x = h_ref[...]
        rstd = lax.rsqrt(jnp.mean(x * x, axis=-1, keepdims=True) + EPS)
        hhat = x * rstd
        err = hhat * g_ref[...] - t_ref[...]
        l_ref[...] += 0.5 * jnp.sum(jnp.mean(err * err, axis=-1, keepdims=True), axis=0, keepdims=True)
        dy = err * (1.0 / d)
        dg_ref[...] += _colsum(dy * hhat)
        dyg = dy * g_ref[...]
        dh = rstd * (dyg - hhat * jnp.mean(dyg * hhat, axis=-1, keepdims=True))
        o_ref[...] = dh
        ob_ref[...] = dh.astype(BF16)

    row = pl.BlockSpec((ts, d), lambda i: (i, 0))
    vec = pl.BlockSpec((1, d), lambda i: (0, 0))
    return pl.pallas_call(
        body, name=name, grid=(s // ts,),
        in_specs=[row, row, vec],
        out_specs=[row, row, pl.BlockSpec((1, 128), lambda i: (0, 0)), vec],
        out_shape=[jax.ShapeDtypeStruct((s, d), F32), jax.ShapeDtypeStruct((s, d), BF16),
                   jax.ShapeDtypeStruct((1, 128), F32), jax.ShapeDtypeStruct((1, d), F32)],
        compiler_params=_params("arbitrary"))(h, target, gain)


def _halo_specs(n_lead, ts, width, n_tiles):
    hb = ts // HALO
    prev = pl.BlockSpec((n_lead, HALO, width), lambda i: (0, jnp.maximum(i * hb - 1, 0), 0))
    nxt = pl.BlockSpec((n_lead, HALO, width), lambda i: (0, jnp.minimum((i + 1) * hb, n_tiles * hb - 1), 0))
    return prev, nxt


def _pool_fwd(z_ext, g, pos):
    w = POOL_WINDOWS[g]
    zg = z_ext[:, g * GROUP:(g + 1) * GROUP]
    acc = zg
    sh = 1
    while sh < w:
        acc = acc + _back(acc, sh)
        sh *= 2
    return acc[HALO:] / jnp.minimum(pos, float(w)) - zg[HALO:]


def even_layer_fwd(h, gain, win, conv_a, w_pool, pool_scale, wout, gain_next, *, name, pin=None):
    s, d = h.shape
    w = win.shape[-1]
    ts = min(ROW_TILE, s)
    hb = ts // HALO

    def body(h_ref, hp_ref, gain_ref, win_ref, ca_ref, wp_ref, ps_ref, wout_ref, gn_ref, *rest):
        o_ref, xn_ref, r_ref, p_ref, m_ref, hn_ref, rn_ref = rest[-7:]
        i = pl.program_id(0)
        keep = jnp.where(i > 0, 1.0, 0.0)
        h_ext = jnp.concatenate([hp_ref[...], h_ref[...]], axis=0)
        rstd = lax.rsqrt(jnp.mean(h_ext * h_ext, axis=-1, keepdims=True) + EPS)
        xn_ext = (h_ext * rstd * gain_ref[...]).astype(BF16)
        xn_ref[...] = xn_ext[HALO:]
        r_ref[...] = rstd[HALO:]
        p32 = []
        for k in range(4):
            pk = jnp.dot(xn_ext, win_ref[k], preferred_element_type=F32).astype(BF16)
            p_ref[k] = pk[HALO:]
            pk = pk.astype(F32)
            p32.append(jnp.concatenate([pk[:HALO] * keep, pk[HALO:]], axis=0))
        m_ref[:, 0:w] = (p32[0][HALO:] * _causal_conv(p32[1] * p32[2], ca_ref)[HALO:]).astype(BF16)
        pos = (i * ts + lax.broadcasted_iota(jnp.int32, (ts, 1), 0) + 1).astype(F32)
        for g in range(len(POOL_WINDOWS)):
            pooled = _pool_fwd(p32[3], g, pos)
            mixed = jnp.dot(pooled.astype(BF16), wp_ref[g], preferred_element_type=F32)
            cols = slice(g * GROUP, (g + 1) * GROUP)
            m_ref[:, w + g * GROUP:w + (g + 1) * GROUP] = (mixed * ps_ref[:, cols]).astype(BF16)
        out = h_ref[...] + jnp.dot(m_ref[...], wout_ref[...], preferred_element_type=F32)
        o_ref[...] = out
        rstd_n = lax.rsqrt(jnp.mean(out * out, axis=-1, keepdims=True) + EPS)
        hn_ref[...] = (out * rstd_n * gn_ref[...]).astype(BF16)
        rn_ref[...] = rstd_n

    const = lambda shape: pl.BlockSpec(shape, lambda i: (0,) * len(shape))
    row = pl.BlockSpec((ts, d), lambda i: (i, 0))
    col1 = pl.BlockSpec((ts, 1), lambda i: (i, 0))
    return pl.pallas_call(
        body, name=name, grid=(s // ts,),
        in_specs=[row, pl.BlockSpec((HALO, d), lambda i: (jnp.maximum(i * hb - 1, 0), 0)), const((1, d)),
                  const((4, d, w)), const((3, w)), const((4, GROUP, GROUP)), const((1, w)), const((2 * w, d)),
                  const((1, d))] + ([ANY] if pin is not None else []),
        out_specs=[row, row, col1, pl.BlockSpec((4, ts, w), lambda i: (0, i, 0)),
                   pl.BlockSpec((ts, 2 * w), lambda i: (i, 0)), row, col1],
        out_shape=[jax.ShapeDtypeStruct((s, d), F32), jax.ShapeDtypeStruct((s, d), BF16),
                   jax.ShapeDtypeStruct((s, 1), F32), jax.ShapeDtypeStruct((4, s, w), BF16),
                   jax.ShapeDtypeStruct((s, 2 * w), BF16), jax.ShapeDtypeStruct((s, d), BF16),
                   jax.ShapeDtypeStruct((s, 1), F32)],
        compiler_params=_params("parallel"))(
            h, h, gain, win, conv_a, w_pool, pool_scale, wout, gain_next, *([pin] if pin is not None else []))


def even_bwd(proj, dh_bf, wout, conv_a, w_pool, w_pool_t, pool_scale, *, name, pin=None):
    _, s, w = proj.shape
    d = dh_bf.shape[1]
    ts = min(ROW_TILE, s)
    n_t = s // ts
    prev, nxt = _halo_specs(4, ts, w, n_t)
    hb = ts // HALO
    n_ext = ts + HALO

    def body(p_ref, pp_ref, pn_ref, dh_ref, dhn_ref, wout_ref, ca_ref, wp_ref, wpt_ref, ps_ref, *rest):
        dp_ref, dca_ref, dwp_ref, dps_ref = rest[-4:]
        i = pl.program_id(0)

        @pl.when(i == 0)
        def _():
            dca_ref[...] = jnp.zeros_like(dca_ref)
            dwp_ref[...] = jnp.zeros_like(dwp_ref)
            dps_ref[...] = jnp.zeros_like(dps_ref)

        keep_p = jnp.where(i > 0, 1.0, 0.0)
        keep_n = jnp.where(i < n_t - 1, 1.0, 0.0)
        dmix = lax.dot_general(jnp.concatenate([dh_ref[...], dhn_ref[...]], axis=0), wout_ref[...], _NT,
                               preferred_element_type=F32)
        a_b, a_c, a_v = (p_ref[k].astype(F32) for k in range(3))
        cv_ext = jnp.concatenate([pp_ref[1].astype(F32) * pp_ref[2].astype(F32) * keep_p, a_c * a_v], axis=0)
        dy_a = dmix[:ts, 0:w]
        dp_ref[0] = (dy_a * _causal_conv(cv_ext, ca_ref)[HALO:]).astype(BF16)
        dcc = dy_a * a_b
        dca_ref[2:3, :] += _colsum(dcc * cv_ext[HALO:])
        dca_ref[1:2, :] += _colsum(dcc * _back(cv_ext, 1)[HALO:])
        dca_ref[0:1, :] += _colsum(dcc * _back(cv_ext, 2)[HALO:])
        dcc_ext = jnp.concatenate([dcc, dmix[ts:, 0:w] * pn_ref[0].astype(F32) * keep_n], axis=0)
        dcv = _causal_conv_t(dcc_ext, ca_ref)[:ts]
        dp_ref[1] = (dcv * a_v).astype(BF16)
        dp_ref[2] = (dcv * a_c).astype(BF16)
        z_ext = jnp.concatenate([pp_ref[3].astype(F32) * keep_p, p_ref[3].astype(F32)], axis=0)
        pos = (i * ts + lax.broadcasted_iota(jnp.int32, (ts, 1), 0) + 1).astype(F32)
        pos_ext = (i * ts + lax.broadcasted_iota(jnp.int32, (n_ext, 1), 0) + 1).astype(F32)
        for g, win in enumerate(POOL_WINDOWS):
            cols = slice(g * GROUP, (g + 1) * GROUP)
            ycols = slice(w + g * GROUP, w + (g + 1) * GROUP)
            pooled = _pool_fwd(z_ext, g, pos).astype(BF16)
            mixed = jnp.dot(pooled, wp_ref[g], preferred_element_type=F32)
            dy_b = dmix[:ts, ycols]
            dps_ref[:, cols] += _colsum(dy_b * mixed)
            dmixed_ext = jnp.concatenate([dy_b, dmix[ts:, ycols] * keep_n], axis=0) * ps_ref[:, cols]
            dmixed_ext = dmixed_ext.astype(BF16)
            dwp_ref[g] += lax.dot_general(pooled, dmixed_ext[:ts], _TN, preferred_element_type=F32)
            dpooled = jnp.dot(dmixed_ext, wpt_ref[g], preferred_element_type=F32)
            acc = dpooled / jnp.minimum(pos_ext, float(win))
            sh = 1
            while sh < win:
                acc = acc + _fwd(acc, sh)
                sh *= 2
            dp_ref[3, :, cols] = (acc[:ts] - dpooled[:ts]).astype(BF16)

    tile4 = pl.BlockSpec((4, ts, w), lambda i: (0, i, 0))
    const = lambda shape: pl.BlockSpec(shape, lambda i: (0,) * len(shape))
    return pl.pallas_call(
        body, name=name, grid=(n_t,),
        in_specs=[tile4, prev, nxt, pl.BlockSpec((ts, d), lambda i: (i, 0)),
                  pl.BlockSpec((HALO, d), lambda i: (jnp.minimum((i + 1) * hb, n_t * hb - 1), 0)),
                  const((2 * w, d)), const((3, w)), const((4, GROUP, GROUP)), const((4, GROUP, GROUP)), const((1, w))]
        + ([ANY] if pin is not None else []),
        out_specs=[tile4, const((3, w)), const((4, GROUP, GROUP)), const((1, w))],
        out_shape=[jax.ShapeDtypeStruct((4, s, w), BF16), jax.ShapeDtypeStruct((3, w), F32),
                   jax.ShapeDtypeStruct((4, GROUP, GROUP), F32), jax.ShapeDtypeStruct((1, w), F32)],
        compiler_params=_params("arbitrary"))(
            proj, proj, proj, dh_bf, dh_bf, wout, conv_a, w_pool, w_pool_t, pool_scale,
            *([pin] if pin is not None else []))


def _ffn_halo(ts, f, n_t, nxt):
    hb = ts // HALO
    if nxt:
        return pl.BlockSpec((None, HALO, f), lambda j, i: (j, jnp.minimum((i + 1) * hb, n_t * hb - 1), 0))
    return pl.BlockSpec((None, HALO, f), lambda j, i: (j, jnp.maximum(i * hb - 1, 0), 0))


def ffn_act_fwd(g, up, cw, cb, *, name):
    j_n, s, f = g.shape
    ts = min(ROW_TILE, s)
    n_t = s // ts

    def body(g_ref, gp_ref, u_ref, cw_ref, cb_ref, o_ref):
        keep = jnp.where(pl.program_id(1) > 0, 1.0, 0.0)
        g_ext = jnp.concatenate([gp_ref[...] * keep, g_ref[...]], axis=0)
        gc = _causal_conv(g_ext, cw_ref)[HALO:] + cb_ref[...]
        o_ref[...] = (gc * jax.nn.sigmoid(gc) * u_ref[...]).astype(BF16)

    tile = pl.BlockSpec((None, ts, f), lambda j, i: (j, i, 0))
    return pl.pallas_call(
        body, name=name, grid=(j_n, n_t),
        in_specs=[tile, _ffn_halo(ts, f, n_t, False), tile,
                  pl.BlockSpec((None, 3, f), lambda j, i: (j, 0, 0)),
                  pl.BlockSpec((None, 1, f), lambda j, i: (j, 0, 0))],
        out_specs=tile,
        out_shape=jax.ShapeDtypeStruct((j_n, s, f), BF16),
        compiler_params=_params("parallel", "parallel"))(g, g, up, cw, cb)


def ffn_act_bwd(g, up, dact, cw, cb, *, name):
    j_n, s, f = g.shape
    ts = min(ROW_TILE, s)
    n_t = s // ts

    def body(g_ref, gp_ref, gn_ref, u_ref, un_ref, d_ref, dn_ref, cw_ref, cb_ref,
             dg_ref, du_ref, dcw_ref, dcb_ref):
        i = pl.program_id(1)

        @pl.when(i == 0)
        def _():
            dcw_ref[...] = jnp.zeros_like(dcw_ref)
            dcb_ref[...] = jnp.zeros_like(dcb_ref)

        keep_p = jnp.where(i > 0, 1.0, 0.0)
        keep_n = jnp.where(i < n_t - 1, 1.0, 0.0)
        g_ext = jnp.concatenate([gp_ref[...] * keep_p, g_ref[...], gn_ref[...]], axis=0)
        gc = _causal_conv(g_ext, cw_ref)[HALO:] + cb_ref[...]
        sig = jax.nn.sigmoid(gc)
        dact_ext = jnp.concatenate([d_ref[...], dn_ref[...] * keep_n], axis=0)
        du_ref[...] = (dact_ext * gc * sig)[:ts].astype(BF16)
        up_ext = jnp.concatenate([u_ref[...], un_ref[...]], axis=0)
        dgc = dact_ext * up_ext * (sig * (1.0 + gc * (1.0 - sig)))
        dg_ref[...] = _causal_conv_t(dgc, cw_ref)[:ts].astype(BF16)
        dgc_t = dgc[:ts]
        dcb_ref[...] += _colsum(dgc_t)
        dcw_ref[2:3, :] += _colsum(dgc_t * g_ext[HALO:HALO + ts])
        dcw_ref[1:2, :] += _colsum(dgc_t * _back(g_ext, 1)[HALO:HALO + ts])
        dcw_ref[0:1, :] += _colsum(dgc_t * _back(g_ext, 2)[HALO:HALO + ts])

    tile = pl.BlockSpec((None, ts, f), lambda j, i: (j, i, 0))
    prev, nxt = _ffn_halo(ts, f, n_t, False), _ffn_halo(ts, f, n_t, True)
    return pl.pallas_call(
        body, name=name, grid=(j_n, n_t),
        in_specs=[tile, prev, nxt, tile, nxt, tile, nxt,
                  pl.BlockSpec((None, 3, f), lambda j, i: (j, 0, 0)),
                  pl.BlockSpec((None, 1, f), lambda j, i: (j, 0, 0))],
        out_specs=[tile, tile, pl.BlockSpec((None, 3, f), lambda j, i: (j, 0, 0)),
                   pl.BlockSpec((None, 1, f), lambda j, i: (j, 0, 0))],
        out_shape=[jax.ShapeDtypeStruct((j_n, s, f), BF16), jax.ShapeDtypeStruct((j_n, s, f), BF16),
                   jax.ShapeDtypeStruct((j_n, 3, f), F32), jax.ShapeDtypeStruct((j_n, 1, f), F32)],
        compiler_params=_params("parallel", "arbitrary"))(g, g, g, up, up, dact, dact, cw, cb)


def ffn_in_fwd(hn, wg, wu, cw, cb, *, name, pin=None):
    s, d = hn.shape
    j_n, f, _ = wg.shape
    tm = min(ROW_TILE, s)
    hb = tm // HALO

    def body(x_ref, xp_ref, wg_ref, wu_ref, cw_ref, cb_ref, *rest):
        g_ref, u_ref, gc_ref, a_ref = rest[-4:]
        i, j = pl.program_id(0), pl.program_id(1)
        x_ext = jnp.concatenate([xp_ref[...], x_ref[...]], axis=0)
        g_ext = lax.dot_general(x_ext, wg_ref[j], _NT, preferred_element_type=F32).astype(BF16)
        up = lax.dot_general(x_ref[...], wu_ref[j], _NT, preferred_element_type=F32).astype(BF16)
        g_ref[...] = g_ext[HALO:]
        u_ref[...] = up
        a_ref[...] = _ffn_act(g_ext, up, cw_ref, cb_ref, gc_ref, i)

    whole = pl.BlockSpec((j_n, f, d), lambda i, j: (0, 0, 0))
    tile = pl.BlockSpec((None, tm, f), lambda i, j: (j, i, 0))
    shape = jax.ShapeDtypeStruct((j_n, s, f), BF16)
    return pl.pallas_call(
        body, name=name, grid=(s // tm, j_n),
        in_specs=[pl.BlockSpec((tm, d), lambda i, j: (i, 0)),
                  pl.BlockSpec((HALO, d), lambda i, j: (jnp.maximum(i * hb - 1, 0), 0)),
                  whole, whole,
                  pl.BlockSpec((None, 3, f), lambda i, j: (j, 0, 0)),
                  pl.BlockSpec((None, 1, f), lambda i, j: (j, 0, 0))] + ([ANY] if pin is not None else []),
        out_specs=[tile] * 4, out_shape=[shape] * 4,
        compiler_params=_params("parallel", "parallel"))(hn, hn, wg, wu, cw, cb, *([pin] if pin is not None else []))


def _ffn_act(g_ext, up, cw_ref, cb_ref, gc_ref, i):
    keep = jnp.where(i > 0, 1.0, 0.0)
    g32 = jnp.concatenate([g_ext[:HALO].astype(F32) * keep, g_ext[HALO:].astype(F32)], axis=0)
    gc = (_causal_conv(g32, cw_ref)[HALO:] + cb_ref[...]).astype(BF16)
    gc_ref[...] = gc
    gc = gc.astype(F32)
    return (gc * jax.nn.sigmoid(gc) * up.astype(F32)).astype(BF16)


def ffn_loss_fwd(h, gain, wg, wu, wd, cw, cb, target, final_gain, *, name):
    s, d = h.shape
    j_n, f, _ = wg.shape
    tm = min(ROW_TILE, s)
    hb = tm // HALO

    def body(h_ref, hp_ref, gain_ref, wg_ref, wu_ref, wd_ref, cw_ref, cb_ref, t_ref, fg_ref,
             o_ref, ob_ref, l_ref, dfg_ref, xn_ref, r_ref, g_ref, u_ref, gc_ref, x_s, acc_s):
        i, j = pl.program_id(0), pl.program_id(1)

        @pl.when((i == 0) & (j == 0))
        def _():
            l_ref[...] = jnp.zeros_like(l_ref)
            dfg_ref[...] = jnp.zeros_like(dfg_ref)

        @pl.when(j == 0)
        def _():
            h_ext = jnp.concatenate([hp_ref[...], h_ref[...]], axis=0)
            rstd = lax.rsqrt(jnp.mean(h_ext * h_ext, axis=-1, keepdims=True) + EPS)
            x_s[...] = (h_ext * rstd * gain_ref[...]).astype(BF16)
            xn_ref[...] = x_s[HALO:, :]
            r_ref[...] = rstd[HALO:]
            acc_s[...] = h_ref[...]

        g_ext = lax.dot_general(x_s[...], wg_ref[j], _NT, preferred_element_type=F32).astype(BF16)
        up = lax.dot_general(x_s[HALO:, :], wu_ref[j], _NT, preferred_element_type=F32).astype(BF16)
        g_ref[...] = g_ext[HALO:]
        u_ref[...] = up
        act = _ffn_act(g_ext, up, cw_ref, cb_ref, gc_ref, i)
        acc_s[...] += jnp.dot(act, wd_ref[j], preferred_element_type=F32)

        @pl.when(j == j_n - 1)
        def _():
            x = acc_s[...]
            rstd = lax.rsqrt(jnp.mean(x * x, axis=-1, keepdims=True) + EPS)
            hhat = x * rstd
            err = hhat * fg_ref[...] - t_ref[...]
            l_ref[...] += 0.5 * jnp.sum(jnp.mean(err * err, axis=-1, keepdims=True), axis=0, keepdims=True)
            dy = err * (1.0 / d)
            dfg_ref[...] += _colsum(dy * hhat)
            dyg = dy * fg_ref[...]
            dh = rstd * (dyg - hhat * jnp.mean(dyg * hhat, axis=-1, keepdims=True))
            o_ref[...] = dh
            ob_ref[...] = dh.astype(BF16)

    whole = pl.BlockSpec((j_n, f, d), lambda i, j: (0, 0, 0), pipeline_mode=pl.Buffered(1))
    row = pl.BlockSpec((tm, d), lambda i, j: (i, 0))
    vec = pl.BlockSpec((1, d), lambda i, j: (0, 0))
    tile = pl.BlockSpec((None, tm, f), lambda i, j: (j, i, 0))
    return pl.pallas_call(
        body, name=name, grid=(s // tm, j_n),
        in_specs=[row, pl.BlockSpec((HALO, d), lambda i, j: (jnp.maximum(i * hb - 1, 0), 0)),
                  vec, whole, whole, whole,
                  pl.BlockSpec((None, 3, f), lambda i, j: (j, 0, 0)),
                  pl.BlockSpec((None, 1, f), lambda i, j: (j, 0, 0)), row, vec],
        out_specs=[row, row, pl.BlockSpec((1, 128), lambda i, j: (0, 0)), vec,
                   row, pl.BlockSpec((tm, 1), lambda i, j: (i, 0)), tile, tile, tile],
        out_shape=[jax.ShapeDtypeStruct((s, d), F32), jax.ShapeDtypeStruct((s, d), BF16),
                   jax.ShapeDtypeStruct((1, 128), F32), jax.ShapeDtypeStruct((1, d), F32),
                   jax.ShapeDtypeStruct((s, d), BF16), jax.ShapeDtypeStruct((s, 1), F32)]
        + [jax.ShapeDtypeStruct((j_n, s, f), BF16)] * 3,
        scratch_shapes=[pltpu.VMEM((HALO + tm, d), BF16), pltpu.VMEM((tm, d), F32)],
        compiler_params=_params("arbitrary", "arbitrary"))(h, h, gain, wg, wu, wd, cw, cb, target, final_gain)


def ffn_bwd_a(dh_bf, hn, g, up, gc, wd, cw, *, name, pin=None):
    s, d = hn.shape
    j_n, _, f = g.shape
    tm = min(ROW_TILE, s)
    n_t = s // tm
    hb = tm // HALO

    def body(dh_ref, dhn_ref, x_ref, g_ref, gc_ref, gcn_ref, u_ref, un_ref, wd_ref, cw_ref, *rest):
        dg_ref, du_ref, dwg_ref, dwu_ref, dwd_ref, dcw_ref, dcb_ref = rest[-7:]
        i = pl.program_id(1)

        @pl.when(i == 0)
        def _():
            for r in (dwg_ref, dwu_ref, dwd_ref, dcw_ref, dcb_ref):
                r[...] = jnp.zeros_like(r)

        keep_n = jnp.where(i < n_t - 1, 1.0, 0.0)
        dh = dh_ref[...]
        dact = lax.dot_general(jnp.concatenate([dh, dhn_ref[...]], axis=0), wd_ref[...], _NT,
                               preferred_element_type=F32)
        dact = jnp.concatenate([dact[:tm], dact[tm:] * keep_n], axis=0)
        gc_ext = jnp.concatenate([gc_ref[...], gcn_ref[...]], axis=0).astype(F32)
        sig = jax.nn.sigmoid(gc_ext)
        silu = gc_ext * sig
        up_ext = jnp.concatenate([u_ref[...], un_ref[...]], axis=0).astype(F32)
        act = (silu * up_ext)[:tm].astype(BF16)
        dwd_ref[...] += lax.dot_general(act, dh, _TN, preferred_element_type=F32)
        dup = (dact * silu)[:tm].astype(BF16)
        du_ref[...] = dup
        dgc = dact * up_ext * (sig + silu * (1.0 - sig))
        dgc_1, dgc_2 = _fwd(dgc, 1), _fwd(dgc, 2)
        dg = (cw_ref[2:3, :] * dgc + cw_ref[1:2, :] * dgc_1 + cw_ref[0:1, :] * dgc_2)[:tm].astype(BF16)
        dg_ref[...] = dg
        x = x_ref[...]
        dwg_ref[...] += lax.dot_general(dg, x, _TN, preferred_element_type=F32)
        dwu_ref[...] += lax.dot_general(dup, x, _TN, preferred_element_type=F32)
        g32 = g_ref[...].astype(F32)
        dcb_ref[...] += _colsum(dgc[:tm])
        dcw_ref[2:3, :] += _colsum(dgc[:tm] * g32)
        dcw_ref[1:2, :] += _colsum(dgc_1[:tm] * g32)
        dcw_ref[0:1, :] += _colsum(dgc_2[:tm] * g32)

    rows = pl.BlockSpec((tm, d), lambda j, i: (i, 0))
    rows_next = pl.BlockSpec((HALO, d), lambda j, i: (jnp.minimum((i + 1) * hb, n_t * hb - 1), 0))
    tile = pl.BlockSpec((None, tm, f), lambda j, i: (j, i, 0))
    nxt = _ffn_halo(tm, f, n_t, True)
    per_j = lambda r, c: pl.BlockSpec((None, r, c), lambda j, i: (j, 0, 0))
    return pl.pallas_call(
        body, name=name, grid=(j_n, n_t),
        in_specs=[rows, rows_next, rows, tile, tile, nxt, tile, nxt, per_j(f, d), per_j(3, f)]
        + ([ANY] if pin is not None else []),
        out_specs=[tile, tile, per_j(f, d), per_j(f, d), per_j(f, d), per_j(3, f), per_j(1, f)],
        out_shape=[jax.ShapeDtypeStruct((j_n, s, f), BF16), jax.ShapeDtypeStruct((j_n, s, f), BF16),
                   jax.ShapeDtypeStruct((j_n, f, d), F32), jax.ShapeDtypeStruct((j_n, f, d), F32),
                   jax.ShapeDtypeStruct((j_n, f, d), F32), jax.ShapeDtypeStruct((j_n, 3, f), F32),
                   jax.ShapeDtypeStruct((j_n, 1, f), F32)],
        compiler_params=_params("parallel", "arbitrary"))(
            dh_bf, dh_bf, hn, g, gc, gc, up, up, wd, cw, *([pin] if pin is not None else []))


def dx_rms_bwd(pairs, h, gain, rstd, dres, *, name, pin=None):
    j_n, s, f = pairs[0][0].shape
    d = h.shape[1]
    tm = min(ROW_TILE, s)
    n_p = len(pairs)
    dims = [_NT if w.shape[1:] == (d, f) else (((1,), (0,)), ((), ())) for _, w in pairs]

    def body(*refs):
        dy_refs, w_refs = refs[:n_p], refs[n_p:2 * n_p]
        h_ref, g_ref, r_ref, dr_ref = refs[2 * n_p:2 * n_p + 4]
        o_ref, ob_ref, dgain_ref = refs[-3:]

        @pl.when(pl.program_id(0) == 0)
        def _():
            dgain_ref[...] = jnp.zeros_like(dgain_ref)
        dx = jnp.zeros((tm, d), F32)
        for j in range(j_n):
            for p in range(n_p):
                dx = dx + lax.dot_general(dy_refs[p][j], w_refs[p][j], dims[p], preferred_element_type=F32)
        rstd_v = r_ref[...]
        hhat = h_ref[...] * rstd_v
        dgain_ref[...] += _colsum(dx * hhat)
        dxg = dx * g_ref[...]
        dh = dr_ref[...] + rstd_v * (dxg - hhat * jnp.mean(dxg * hhat, axis=-1, keepdims=True))
        o_ref[...] = dh
        ob_ref[...] = dh.astype(BF16)

    tile4 = pl.BlockSpec((j_n, tm, f), lambda i: (0, i, 0))
    whole = [pl.BlockSpec(w.shape, lambda i: (0, 0, 0), pipeline_mode=pl.Buffered(1)) for _, w in pairs]
    row = pl.BlockSpec((tm, d), lambda i: (i, 0))
    vec = pl.BlockSpec((1, d), lambda i: (0, 0))
    return pl.pallas_call(
        body, name=name, grid=(s // tm,),
        in_specs=[tile4] * n_p + whole + [row, vec, pl.BlockSpec((tm, 1), lambda i: (i, 0)), row]
        + ([ANY] if pin is not None else []),
        out_specs=[row, row, vec],
        out_shape=[jax.ShapeDtypeStruct((s, d), F32), jax.ShapeDtypeStruct((s, d), BF16),
                   jax.ShapeDtypeStruct((1, d), F32)],
        compiler_params=_params("arbitrary"))(
            *[p[0] for p in pairs], *[p[1] for p in pairs], h, gain, rstd, dres, *([pin] if pin is not None else []))


def _sgu_gate(vn_bf, ws_ref, bs_ref, h, rows):
    tri = lax.broadcasted_iota(jnp.int32, (CHUNK, CHUNK), 0) >= lax.broadcasted_iota(jnp.int32, (CHUNK, CHUNK), 1)
    ws = jnp.where(tri, ws_ref[h], 0.0).astype(BF16)
    cols = slice((h % 4) * GROUP, (h % 4 + 1) * GROUP)
    return ws, jnp.dot(ws, vn_bf[h // 4][rows, cols], preferred_element_type=F32) + bs_ref[h]


def odd_layer_fwd(h, gain, win, sgu_norm, w_spatial, b_spatial, wout, *, name):
    s, d = h.shape
    w = win.shape[-1]
    ts = min(ROW_TILE, s)
    n_heads = w_spatial.shape[0]

    def body(h_ref, gain_ref, win_ref, n_ref, ws_ref, bs_ref, wout_ref, o_ref, xn_ref, r_ref, p_ref, m_ref, rv_ref):
        x = h_ref[...]
        rstd_x = lax.rsqrt(jnp.mean(x * x, axis=-1, keepdims=True) + EPS)
        xn = (x * rstd_x * gain_ref[...]).astype(BF16)
        xn_ref[...] = xn
        r_ref[...] = rstd_x
        for k in range(4):
            p_ref[k] = jnp.dot(xn, win_ref[k], preferred_element_type=F32).astype(BF16)
        v = [_gelu(p_ref[2].astype(F32)), _gelu(p_ref[3].astype(F32))]
        ms = (jnp.sum(v[0] * v[0], axis=-1, keepdims=True) + jnp.sum(v[1] * v[1], axis=-1, keepdims=True)) / (2 * w)
        rstd = lax.rsqrt(ms + EPS)
        rv_ref[...] = rstd
        vn = [(v[k] * rstd * n_ref[:, k * w:(k + 1) * w]).astype(BF16) for k in range(2)]
        for hd in range(n_heads):
            cols = slice((hd % 4) * GROUP, (hd % 4 + 1) * GROUP)
            for c in range(ts // CHUNK):
                rows = slice(c * CHUNK, (c + 1) * CHUNK)
                _, gate = _sgu_gate(vn, ws_ref, bs_ref, hd, rows)
                u = _gelu(p_ref[hd // 4, rows, cols].astype(F32))
                m_ref[rows, hd * GROUP:(hd + 1) * GROUP] = (u * gate).astype(BF16)
        o_ref[...] = x + jnp.dot(m_ref[...], wout_ref[...], preferred_element_type=F32)

    const = lambda shape: pl.BlockSpec(shape, lambda i: (0,) * len(shape))
    row = pl.BlockSpec((ts, d), lambda i: (i, 0))
    col1 = pl.BlockSpec((ts, 1), lambda i: (i, 0))
    return pl.pallas_call(
        body, name=name, grid=(s // ts,),
        in_specs=[row, const((1, d)), const((4, d, w)), const((1, 2 * w)),
                  const((n_heads, CHUNK, CHUNK)), const((n_heads, CHUNK, 1)), const((2 * w, d))],
        out_specs=[row, row, col1, pl.BlockSpec((4, ts, w), lambda i: (0, i, 0)),
                   pl.BlockSpec((ts, 2 * w), lambda i: (i, 0)), col1],
        out_shape=[jax.ShapeDtypeStruct((s, d), F32), jax.ShapeDtypeStruct((s, d), BF16),
                   jax.ShapeDtypeStruct((s, 1), F32), jax.ShapeDtypeStruct((4, s, w), BF16),
                   jax.ShapeDtypeStruct((s, 2 * w), BF16), jax.ShapeDtypeStruct((s, 1), F32)],
        compiler_params=_params("parallel"))(h, gain, win, sgu_norm, w_spatial, b_spatial, wout)


def sgu_bwd(p, dh_bf, wout, rstd, sgu_norm, w_spatial, b_spatial, *, name, pin=None):
    _, s, w = p.shape
    d = dh_bf.shape[1]
    ts = min(ROW_TILE, s)
    n_heads = w_spatial.shape[0]

    def body(p_ref, dh_ref, wout_ref, r_ref, n_ref, ws_ref, bs_ref, *rest):
        dp_ref, dn_ref, dws_ref, dbs_ref, dvn_ref, dm_ref = rest[-6:]

        @pl.when(pl.program_id(0) == 0)
        def _():
            dn_ref[...] = jnp.zeros_like(dn_ref)
            dws_ref[...] = jnp.zeros_like(dws_ref)
            dbs_ref[...] = jnp.zeros_like(dbs_ref)

        dm_ref[...] = lax.dot_general(dh_ref[...], wout_ref[...], _NT, preferred_element_type=F32)
        rstd_v = r_ref[...]
        vhat = [_gelu(p_ref[2 + k].astype(F32)) * rstd_v for k in range(2)]
        vn = [(vhat[k] * n_ref[:, k * w:(k + 1) * w]).astype(BF16) for k in range(2)]
        tri = lax.broadcasted_iota(jnp.int32, (CHUNK, CHUNK), 0) >= lax.broadcasted_iota(jnp.int32, (CHUNK, CHUNK), 1)
        for h in range(n_heads):
            cols = slice((h % 4) * GROUP, (h % 4 + 1) * GROUP)
            ocols = slice(h * GROUP, (h + 1) * GROUP)
            for c in range(ts // CHUNK):
                rows = slice(c * CHUNK, (c + 1) * CHUNK)
                ws, gate = _sgu_gate(vn, ws_ref, bs_ref, h, rows)
                pu = p_ref[h // 4, rows, cols].astype(F32)
                dm = dm_ref[rows, ocols]
                dp_ref[h // 4, rows, cols] = (dm * gate * _gelu_grad(pu)).astype(BF16)
                dgate = dm * _gelu(pu)
                dbs_ref[h] += jnp.sum(dgate, axis=-1, keepdims=True)
                dgate_bf = dgate.astype(BF16)
                dws = lax.dot_general(dgate_bf, vn[h // 4][rows, cols], _NT, preferred_element_type=F32)
                dws_ref[h] += jnp.where(tri, dws, 0.0)
                dvn_ref[rows, ocols] = lax.dot_general(ws, dgate_bf, _TN, preferred_element_type=F32)
        for k in range(2):
            kc = slice(k * w, (k + 1) * w)
            dvn = dvn_ref[:, kc]
            dn_ref[:, kc] += _colsum(dvn * vhat[k])
        dvh = [dvn_ref[:, k * w:(k + 1) * w] * n_ref[:, k * w:(k + 1) * w] for k in range(2)]
        dot = (jnp.sum(dvh[0] * vhat[0], axis=-1, keepdims=True)
               + jnp.sum(dvh[1] * vhat[1], axis=-1, keepdims=True)) / (2 * w)
        for k in range(2):
            dv = rstd_v * (dvh[k] - vhat[k] * dot)
            dp_ref[2 + k] = (dv * _gelu_grad(p_ref[2 + k].astype(F32))).astype(BF16)

    const = lambda shape: pl.BlockSpec(shape, lambda i: (0,) * len(shape))
    tile4 = pl.BlockSpec((4, ts, w), lambda i: (0, i, 0))
    return pl.pallas_call(
        body, name=name, grid=(s // ts,),
        in_specs=[tile4, pl.BlockSpec((ts, d), lambda i: (i, 0)), const((2 * w, d)),
                  pl.BlockSpec((ts, 1), lambda i: (i, 0)),
                  const((1, 2 * w)), const((n_heads, CHUNK, CHUNK)), const((n_heads, CHUNK, 1))]
        + ([ANY] if pin is not None else []),
        out_specs=[tile4, const((1, 2 * w)), const((n_heads, CHUNK, CHUNK)), const((n_heads, CHUNK, 1))],
        out_shape=[jax.ShapeDtypeStruct((4, s, w), BF16), jax.ShapeDtypeStruct((1, 2 * w), F32),
                   jax.ShapeDtypeStruct((n_heads, CHUNK, CHUNK), F32),
                   jax.ShapeDtypeStruct((n_heads, CHUNK, 1), F32)],
        scratch_shapes=[pltpu.VMEM((ts, 2 * w), F32), pltpu.VMEM((ts, 2 * w), F32)],
        compiler_params=_params("arbitrary"))(
            p, dh_bf, wout, rstd, sgu_norm, w_spatial, b_spatial, *([pin] if pin is not None else []))


def _row_tile(rows):
    if rows <= ROW_TILE:
        return rows
    for t in (512, 384, 352, 256, 128, 64, 32, 16, 8):
        if rows % t == 0:
            return t
    return rows


def adamw(w, g, m, v, *, name):
    shape = w.shape
    cols = shape[-1]
    rows = w.size // cols
    w2, g2, m2, v2 = (a.reshape(rows, cols) for a in (w, g, m, v))
    tr = _row_tile(rows)
    bc1 = 1.0 - ADAM_B1 ** ADAM_STEP
    bc2 = 1.0 - ADAM_B2 ** ADAM_STEP

    def body(w_ref, g_ref, m_ref, v_ref, d_ref, nm_ref, nv_ref):
        grad = g_ref[...]
        m_new = ADAM_B1 * m_ref[...] + (1.0 - ADAM_B1) * grad
        v_new = ADAM_B2 * v_ref[...] + (1.0 - ADAM_B2) * (grad * grad)
        nm_ref[...] = m_new
        nv_ref[...] = v_new
        d_ref[...] = -ADAM_LR * ((m_new / bc1) / (jnp.sqrt(v_new / bc2) + ADAM_EPS) + ADAM_WD * w_ref[...])

    spec = pl.BlockSpec((tr, cols), lambda i: (i, 0))
    outs = pl.pallas_call(
        body, name=name, grid=(rows // tr,),
        in_specs=[spec] * 4, out_specs=[spec] * 3,
        out_shape=[jax.ShapeDtypeStruct((rows, cols), F32)] * 3,
        compiler_params=_params("parallel"))(w2, g2, m2, v2)
    return tuple(o.reshape(shape) for o in outs)


def adamw_layers(w, grads, m, v, *, name):
    n_l, rows, cols = w.shape
    tr = _row_tile(rows)
    bc1 = 1.0 - ADAM_B1 ** ADAM_STEP
    bc2 = 1.0 - ADAM_B2 ** ADAM_STEP
    outs = None
    for l in range(n_l):
        def body(w_ref, g_ref, m_ref, v_ref, *rest):
            go_ref, d_ref, nm_ref, nv_ref = rest[-4:]
            grad = g_ref[...]
            m_new = ADAM_B1 * m_ref[...] + (1.0 - ADAM_B1) * grad
            v_new = ADAM_B2 * v_ref[...] + (1.0 - ADAM_B2) * (grad * grad)
            go_ref[...] = grad
            nm_ref[...] = m_new
            nv_ref[...] = v_new
            d_ref[...] = -ADAM_LR * ((m_new / bc1) / (jnp.sqrt(v_new / bc2) + ADAM_EPS) + ADAM_WD * w_ref[...])

        layer = pl.BlockSpec((None, tr, cols), lambda i, l=l: (l, i, 0))
        prev = list(outs) if outs is not None else []
        outs = pl.pallas_call(
            body, name=f"{name}{l}", grid=(rows // tr,),
            in_specs=[layer, pl.BlockSpec((tr, cols), lambda i: (i, 0)), layer, layer] + [ANY] * len(prev),
            out_specs=[layer] * 4,
            out_shape=[jax.ShapeDtypeStruct(w.shape, F32)] * 4,
            input_output_aliases={4 + k: k for k in range(len(prev))},
            compiler_params=_params("parallel"))(w, grads[l], m, v, *prev)
    return tuple(outs)


def _place():
    return lax.axis_index("x"), lax.axis_index("y"), lax.axis_index("c")


def _other_chips(x, y):
    return [(1 - x, y), (x, 1 - y), (1 - x, 1 - y)]


HBM = pl.BlockSpec(memory_space=pltpu.HBM)
SEM = pl.BlockSpec(memory_space=pltpu.SEMAPHORE)
DATAFLOW = pltpu.SideEffectType.DATAFLOW_SIDE_EFFECTING


def _in_hbm(a):
    return pltpu.with_memory_space_constraint(a, pltpu.HBM)


def cast_into_slot(w, chip, *, l=None, name, pin=None):
    rows, cols = w.shape[-2:]
    tr = _row_tile(rows)

    def body(chip_ref, w_ref, *rest):
        rest[-1][...] = w_ref[...].astype(BF16)

    in_spec = (pl.BlockSpec((tr, cols), lambda i, chip_ref: (i, 0)) if l is None
               else pl.BlockSpec((None, tr, cols), lambda i, chip_ref: (l, i, 0)))
    return pl.pallas_call(
        body, name=name,
        grid_spec=pltpu.PrefetchScalarGridSpec(
            num_scalar_prefetch=1, grid=(rows // tr,), in_specs=[in_spec] + ([ANY] if pin is not None else []),
            out_specs=pl.BlockSpec((None, tr, cols), lambda i, chip_ref: (chip_ref[0], i, 0))),
        out_shape=jax.ShapeDtypeStruct((N_CHIPS, rows, cols), BF16),
        compiler_params=_params("parallel"))(chip, w, *([pin] if pin is not None else []))


def _half(ref, slot, c):
    half = ref.shape[1] // 2
    return ref.at[slot, pl.ds(c * half, half), :]


def gather_start(groups, smalls, *, name):
    flat = [b for g in groups for b in g]
    n_b, n_s, n_g = len(flat), len(smalls), len(groups)
    n_sem = 2 * n_g + (2 if n_s else 0)

    def body(*refs):
        bufs, small_refs = refs[:n_b], refs[n_b:n_b + n_s]
        sems = refs[n_b + n_s:n_b + n_s + n_sem]
        token = refs[-1]
        x, y, c = _place()
        me = 2 * x + y
        chips = _other_chips(x, y)
        for si in range(n_s):
            piece = small_refs[si].at[me]
            for k, (px, py) in enumerate(chips):
                pltpu.make_async_remote_copy(
                    src_ref=piece, dst_ref=piece,
                    send_sem=sems[2 * n_g].at[3 * si + k], recv_sem=sems[2 * n_g + 1].at[3 * si + k],
                    device_id=(px, py, c), device_id_type=MESH).start()
        t = 0
        for gi, group in enumerate(groups):
            for ti in range(len(group)):
                piece = _half(bufs[t], me, c)
                t += 1
                for k, (px, py) in enumerate(chips):
                    pltpu.make_async_remote_copy(
                        src_ref=piece, dst_ref=piece,
                        send_sem=sems[2 * gi].at[3 * ti + k], recv_sem=sems[2 * gi + 1].at[3 * ti + k],
                        device_id=(px, py, c), device_id_type=MESH).start()
        token[...] = jnp.zeros_like(token)

    sem_shapes = []
    for group in groups:
        sem_shapes += [pltpu.SemaphoreType.DMA((3 * len(group),))] * 2
    if n_s:
        sem_shapes += [pltpu.SemaphoreType.DMA((3 * n_s,))] * 2
    arrays = flat + list(smalls)
    res = pl.pallas_call(
        body, name=name,
        out_shape=tuple(sem_shapes) + tuple(pltpu.HBM(a.shape, a.dtype) for a in arrays)
        + (jax.ShapeDtypeStruct((8, 128), F32),),
        in_specs=[HBM] * len(arrays),
        out_specs=tuple([SEM] * n_sem + [HBM] * len(arrays) + [pl.BlockSpec(memory_space=pltpu.VMEM)]),
        input_output_aliases={i: n_sem + i for i in range(len(arrays))},
        compiler_params=pltpu.CompilerParams(has_side_effects=DATAFLOW))(*[_in_hbm(a) for a in arrays])
    sems, thru, token = res[:n_sem], res[n_sem:-1], res[-1]
    out_groups, t = [], 0
    for group in groups:
        out_groups.append(list(thru[t:t + len(group)]))
        t += len(group)
    return sems, out_groups, list(thru[n_b:]), token


def gather_wait(bufs, send, recv, after, *, name, smalls=(), small_send=None, small_recv=None):
    n_b, n_s = len(bufs), len(smalls)
    arrays = list(bufs) + list(smalls)
    sem_ops = [send, recv] + ([small_send, small_recv] if n_s else [])

    def body(*refs):
        buf_refs, small_refs = refs[:n_b], refs[n_b:n_b + n_s]
        sems = refs[n_b + n_s:n_b + n_s + len(sem_ops)]
        x, y, c = _place()
        me = 2 * x + y
        chips = _other_chips(x, y)
        for ti in range(n_b):
            for k, (px, py) in enumerate(chips):
                done = pltpu.make_async_remote_copy(
                    src_ref=_half(buf_refs[ti], me, c), dst_ref=_half(buf_refs[ti], 2 * px + py, c),
                    send_sem=sems[0].at[3 * ti + k], recv_sem=sems[1].at[3 * ti + k],
                    device_id=(px, py, c), device_id_type=MESH)
                done.wait_send()
                done.wait_recv()
        for si in range(n_s):
            for k, (px, py) in enumerate(chips):
                done = pltpu.make_async_remote_copy(
                    src_ref=small_refs[si].at[me], dst_ref=small_refs[si].at[2 * px + py],
                    send_sem=sems[2].at[3 * si + k], recv_sem=sems[3].at[3 * si + k],
                    device_id=(px, py, c), device_id_type=MESH)
                done.wait_send()
                done.wait_recv()

    res = pl.pallas_call(
        body, name=name,
        out_shape=tuple(pltpu.HBM(a.shape, a.dtype) for a in arrays),
        in_specs=[HBM] * len(arrays) + [SEM] * len(sem_ops) + [ANY],
        out_specs=tuple([HBM] * len(arrays)),
        input_output_aliases={i: i for i in range(len(arrays))},
        compiler_params=pltpu.CompilerParams(has_side_effects=DATAFLOW))(*arrays, *sem_ops, after)
    return list(res[:n_b]), list(res[n_b:])


def gather_forward(bufs, *, name):
    n = len(bufs)

    def body(*refs):
        ins, outs = refs[:n], refs[n:2 * n]
        send_sems, recv_sems = refs[2 * n:]
        x, y, c = _place()
        chips = _other_chips(x, y)
        for t in range(n):
            for k, (px, py) in enumerate(chips):
                pltpu.make_async_remote_copy(
                    src_ref=_half(ins[t], 2 * px + py, c), dst_ref=_half(outs[t], 2 * px + py, c),
                    send_sem=send_sems.at[3 * t + k], recv_sem=recv_sems.at[3 * t + k],
                    device_id=(x, y, 1 - c), device_id_type=MESH).start()
        for t in range(n):
            for k, (px, py) in enumerate(chips):
                done = pltpu.make_async_remote_copy(
                    src_ref=_half(ins[t], 2 * px + py, c), dst_ref=_half(outs[t], 2 * px + py, 1 - c),
                    send_sem=send_sems.at[3 * t + k], recv_sem=recv_sems.at[3 * t + k],
                    device_id=(x, y, 1 - c), device_id_type=MESH)
                done.wait_send()
                done.wait_recv()

    return pl.pallas_call(
        body, name=name, in_specs=[ANY] * n, out_specs=[ANY] * n,
        out_shape=[jax.ShapeDtypeStruct(a.shape, a.dtype) for a in bufs],
        input_output_aliases={i: i for i in range(n)},
        scratch_shapes=[pltpu.SemaphoreType.DMA((3 * n,)), pltpu.SemaphoreType.DMA((3 * n,))],
        compiler_params=pltpu.CompilerParams(has_side_effects=True))(*bufs)


def sum_stage_a(grad, recv, place, wire, *, name):
    j_n, half, cols = recv.shape

    def body(place_ref, g_ref, r_ref, o_ref, ob_ref):
        acc = g_ref[...] + r_ref[...]
        ob_ref[...] = acc.astype(wire)

        @pl.when(pl.program_id(0) == place_ref[0])
        def _():
            o_ref[...] = acc

    blk = (None, half, cols)
    return pl.pallas_call(
        body, name=name,
        grid_spec=pltpu.PrefetchScalarGridSpec(
            num_scalar_prefetch=1, grid=(j_n,),
            in_specs=[pl.BlockSpec(blk, lambda j, place_ref: (j, place_ref[1], 0)),
                      pl.BlockSpec(blk, lambda j, place_ref: (j, 0, 0))],
            out_specs=[pl.BlockSpec((half, cols), lambda j, place_ref: (0, 0)),
                       pl.BlockSpec(blk, lambda j, place_ref: (j, 0, 0))]),
        out_shape=[jax.ShapeDtypeStruct((half, cols), F32), jax.ShapeDtypeStruct(recv.shape, wire)],
        compiler_params=_params("arbitrary"))(place, grad, recv)


def _stage_a_copies(srcs, lands, x, y, c):
    out = []
    for src, land in zip(srcs, lands):
        half = src.shape[1] // 2
        out.append((src.at[:, pl.ds((1 - c) * half, half), :], land, (x, y, 1 - c)))
    return out


def _stage_b_copies(srcs, lands, x, y, c):
    out = []
    for src, land in zip(srcs, lands):
        for k, (px, py) in enumerate(_other_chips(x, y)):
            out.append((src.at[2 * px + py], land.at[k], (px, py, c)))
    return out


def _forward_copies(bufs, _, x, y, c):
    out = []
    for buf in bufs:
        for px, py in _other_chips(x, y):
            out.append((_half(buf, 2 * px + py, c), _half(buf, 2 * px + py, c), (x, y, 1 - c)))
    return out


def _stage_c_copies(fulls, _, x, y, c):
    out = []
    for full in fulls:
        half = full.shape[0] // 2
        mine = full.at[pl.ds(c * half, half), :]
        out.append((mine, mine, (x, y, 1 - c)))
    return out


def _chip_block_copies(bufs, _, x, y, c):
    me = 2 * x + y
    return [(buf.at[me], buf.at[me], (px, py, c)) for buf in bufs for px, py in _other_chips(x, y)]


def split_start(srcs, lands, copies, *, name):
    n, n_all = len(srcs), len(srcs) + len(lands)
    n_c = len(copies(srcs, lands, 0, 0, 0))

    def body(*refs):
        src_refs, land_refs = refs[:n], refs[n:n_all]
        send_sems, recv_sems = refs[n_all], refs[n_all + 1]
        token = refs[-1]
        x, y, c = _place()
        for k, (src, dst, target) in enumerate(copies(src_refs, land_refs, x, y, c)):
            pltpu.make_async_remote_copy(src_ref=src, dst_ref=dst, send_sem=send_sems.at[k], recv_sem=recv_sems.at[k],
                                         device_id=target, device_id_type=MESH).start()
        token[...] = jnp.zeros_like(token)

    arrays = list(srcs) + list(lands)
    res = pl.pallas_call(
        body, name=name,
        out_shape=(pltpu.SemaphoreType.DMA((n_c,)), pltpu.SemaphoreType.DMA((n_c,)))
        + tuple(pltpu.HBM(a.shape, a.dtype) for a in arrays) + (jax.ShapeDtypeStruct((8, 128), F32),),
        in_specs=[HBM] * n_all,
        out_specs=tuple([SEM, SEM] + [HBM] * n_all + [pl.BlockSpec(memory_space=pltpu.VMEM)]),
        input_output_aliases={i: 2 + i for i in range(n_all)},
        compiler_params=pltpu.CompilerParams(has_side_effects=DATAFLOW))(*[_in_hbm(a) for a in arrays])
    return res[0], res[1], list(res[2:2 + n]), list(res[2 + n:2 + n_all]), res[-1]


def split_wait(srcs, lands, send, recv, copies, after, *, name):
    n, n_all = len(srcs), len(srcs) + len(lands)

    def body(*refs):
        src_refs, land_refs = refs[:n], refs[n:n_all]
        send_sems, recv_sems = refs[n_all], refs[n_all + 1]
        x, y, c = _place()
        for k, (src, dst, target) in enumerate(copies(src_refs, land_refs, x, y, c)):
            done = pltpu.make_async_remote_copy(src_ref=src, dst_ref=dst, send_sem=send_sems.at[k],
                                                recv_sem=recv_sems.at[k], device_id=target, device_id_type=MESH)
            done.wait_send()
            done.wait_recv()

    arrays = list(srcs) + list(lands)
    res = pl.pallas_call(
        body, name=name,
        out_shape=tuple(pltpu.HBM(a.shape, a.dtype) for a in arrays),
        in_specs=[HBM] * n_all + [SEM, SEM, ANY],
        out_specs=tuple([HBM] * n_all),
        input_output_aliases={i: i for i in range(n_all)},
        compiler_params=pltpu.CompilerParams(has_side_effects=DATAFLOW))(*arrays, send, recv, after)
    return list(res[:n]), list(res[n:])


def sum_stage_b(part, recv, place, *, name):
    half, cols = part.shape

    def body(place_ref, p_ref, r_ref, o_ref):
        acc = p_ref[...]
        for k in range(3):
            acc = acc + r_ref[k].astype(F32)
        o_ref[...] = acc

    return pl.pallas_call(
        body, name=name,
        grid_spec=pltpu.PrefetchScalarGridSpec(
            num_scalar_prefetch=1, grid=(1,),
            in_specs=[pl.BlockSpec((half, cols), lambda i, place_ref: (0, 0)),
                      pl.BlockSpec((3, half, cols), lambda i, place_ref: (0, 0, 0))],
            out_specs=pl.BlockSpec((half, cols), lambda i, place_ref: (place_ref[1], 0))),
        out_shape=jax.ShapeDtypeStruct((2 * half, cols), F32),
        compiler_params=_params("arbitrary"))(place, part, recv)


def reduce_stage_c(fulls, *, name):
    n = len(fulls)

    def body(*refs):
        ins, outs = refs[:n], refs[n:2 * n]
        send_sems, recv_sems = refs[2 * n:]
        x, y, c = _place()
        for t in range(n):
            half = ins[t].shape[0] // 2
            pltpu.make_async_remote_copy(
                src_ref=ins[t].at[pl.ds(c * half, half), :], dst_ref=outs[t].at[pl.ds(c * half, half), :],
                send_sem=send_sems.at[t], recv_sem=recv_sems.at[t],
                device_id=(x, y, 1 - c), device_id_type=MESH).start()
        for t in range(n):
            half = ins[t].shape[0] // 2
            done = pltpu.make_async_remote_copy(
                src_ref=ins[t].at[pl.ds(c * half, half), :], dst_ref=outs[t].at[pl.ds((1 - c) * half, half), :],
                send_sem=send_sems.at[t], recv_sem=recv_sems.at[t],
                device_id=(x, y, 1 - c), device_id_type=MESH)
            done.wait_send()
            done.wait_recv()

    return pl.pallas_call(
        body, name=name, in_specs=[ANY] * n, out_specs=[ANY] * n,
        out_shape=[jax.ShapeDtypeStruct(a.shape, a.dtype) for a in fulls],
        input_output_aliases={i: i for i in range(n)},
        scratch_shapes=[pltpu.SemaphoreType.DMA((n,)), pltpu.SemaphoreType.DMA((n,))],
        compiler_params=pltpu.CompilerParams(has_side_effects=True))(*fulls)


def gather_chip_blocks(slots, *, name):
    def body(in_ref, out_ref, send_sems, recv_sems):
        x, y, c = _place()
        me = 2 * x + y
        chips = _other_chips(x, y)
        for k, (px, py) in enumerate(chips):
            pltpu.make_async_remote_copy(
                src_ref=in_ref.at[me], dst_ref=out_ref.at[me],
                send_sem=send_sems.at[k], recv_sem=recv_sems.at[k],
                device_id=(px, py, c), device_id_type=MESH).start()
        for k, (px, py) in enumerate(chips):
            done = pltpu.make_async_remote_copy(
                src_ref=in_ref.at[me], dst_ref=out_ref.at[2 * px + py],
                send_sem=send_sems.at[k], recv_sem=recv_sems.at[k],
                device_id=(px, py, c), device_id_type=MESH)
            done.wait_send()
            done.wait_recv()

    return pl.pallas_call(
        body, name=name, in_specs=[ANY], out_specs=ANY,
        out_shape=jax.ShapeDtypeStruct(slots.shape, slots.dtype),
        input_output_aliases={0: 0},
        scratch_shapes=[pltpu.SemaphoreType.DMA((3,)), pltpu.SemaphoreType.DMA((3,))],
        compiler_params=pltpu.CompilerParams(has_side_effects=True))(slots)


def _ffn_bwd(dh, dh_bf, h, gain, saved, wg, wu, wd, cw, cb, place, l, pin):
    hn, rstd, g, up, gc = saved
    dg, dup, dwg, dwu, dwd, dcw, dcb = ffn_bwd_a(dh_bf, hn, g, up, gc, wd, cw, name=f"ffn{l}_bwd_a", pin=pin)
    red = _reduce_a_begin([(dwg, BF16), (dwu, BF16), (dwd, BF16)], tag=f"f{l}")
    dh_in, dh_in_bf, dgain = dx_rms_bwd([(dg, wg), (dup, wu)], h, gain, rstd, dh, name=f"ffn{l}_bwd_b",
                                        pin=red[-1])
    red = _reduce_b_begin(red, place, dh_in_bf, tag=f"f{l}")
    return dh_in, dh_in_bf, (dcw, dcb, dgain), red


def _reduce_a_begin(grads, *, tag):
    lands = [lax.empty((g.shape[0], g.shape[1] // 2, g.shape[2]), F32) for g, _ in grads]
    send, recv, srcs, lands, token = split_start([g for g, _ in grads], lands, _stage_a_copies,
                                                 name=f"reduce_a_start_{tag}")
    return [w for _, w in grads], send, recv, srcs, lands, token


def _reduce_b_begin(state, place, after, *, tag):
    wires, send, recv, srcs, lands, _ = state
    grads, recv_a = split_wait(srcs, lands, send, recv, _stage_a_copies, after, name=f"reduce_a_wait_{tag}")
    parts = [sum_stage_a(g, r, place, w, name=f"sum_a_{tag}{i}") for i, (g, r, w) in enumerate(zip(grads, recv_a, wires))]
    lands_b = [lax.empty((3,) + p[1].shape[1:], p[1].dtype) for p in parts]
    send, recv, srcs, lands, token = split_start([p[1] for p in parts], lands_b, _stage_b_copies,
                                                 name=f"reduce_b_start_{tag}")
    return [p[0] for p in parts], send, recv, srcs, lands, token


def _reduce_c_begin(state, place, after, *, tag):
    parts, send, recv, srcs, lands, _ = state
    _, recv_b = split_wait(srcs, lands, send, recv, _stage_b_copies, after, name=f"reduce_b_wait_{tag}")
    halves = [sum_stage_b(p, r, place, name=f"sum_b_{tag}{i}") for i, (p, r) in enumerate(zip(parts, recv_b))]
    send, recv, fulls, _, token = split_start(halves, [], _stage_c_copies, name=f"reduce_c_start_{tag}")
    return send, recv, fulls, token


def _reduce_finish(state, after, *, tag):
    send, recv, fulls, _ = state
    fulls, _ = split_wait(fulls, [], send, recv, _stage_c_copies, after, name=f"reduce_c_wait_{tag}")
    return fulls


def kernel(x, norm_mix, norm_ffn, final_norm, w_in_even, conv_a, w_pool, pool_scale, w_out_even, w_in_odd, sgu_norm, w_spatial, b_spatial, w_out_odd, w_ffn_gate, w_ffn_up, conv_ffn, b_conv_ffn, w_ffn_down, loss_target, m_norm_mix, m_norm_ffn, m_final_norm, m_w_in_even, m_conv_a, m_w_pool, m_pool_scale, m_w_out_even, m_w_in_odd, m_sgu_norm, m_w_spatial, m_b_spatial, m_w_out_odd, m_w_ffn_gate, m_w_ffn_up, m_conv_ffn, m_b_conv_ffn, m_w_ffn_down, v_norm_mix, v_norm_ffn, v_final_norm, v_w_in_even, v_conv_a, v_w_pool, v_pool_scale, v_w_out_even, v_w_in_odd, v_sgu_norm, v_w_spatial, v_b_spatial, v_w_out_odd, v_w_ffn_gate, v_w_ffn_up, v_conv_ffn, v_b_conv_ffn, v_w_ffn_down):
    weights = dict(norm_mix=norm_mix, norm_ffn=norm_ffn, final_norm=final_norm, w_in_even=w_in_even,
                   conv_a=conv_a, w_pool=w_pool, pool_scale=pool_scale, w_out_even=w_out_even,
                   w_in_odd=w_in_odd, sgu_norm=sgu_norm, w_spatial=w_spatial, b_spatial=b_spatial,
                   w_out_odd=w_out_odd, w_ffn_gate=w_ffn_gate, w_ffn_up=w_ffn_up, conv_ffn=conv_ffn,
                   b_conv_ffn=b_conv_ffn, w_ffn_down=w_ffn_down)
    m_in = dict(norm_mix=m_norm_mix, norm_ffn=m_norm_ffn, final_norm=m_final_norm, w_in_even=m_w_in_even,
                conv_a=m_conv_a, w_pool=m_w_pool, pool_scale=m_pool_scale, w_out_even=m_w_out_even,
                w_in_odd=m_w_in_odd, sgu_norm=m_sgu_norm, w_spatial=m_w_spatial, b_spatial=m_b_spatial,
                w_out_odd=m_w_out_odd, w_ffn_gate=m_w_ffn_gate, w_ffn_up=m_w_ffn_up, conv_ffn=m_conv_ffn,
                b_conv_ffn=m_b_conv_ffn, w_ffn_down=m_w_ffn_down)
    v_in = dict(norm_mix=v_norm_mix, norm_ffn=v_norm_ffn, final_norm=v_final_norm, w_in_even=v_w_in_even,
                conv_a=v_conv_a, w_pool=v_w_pool, pool_scale=v_pool_scale, w_out_even=v_w_out_even,
                w_in_odd=v_w_in_odd, sgu_norm=v_sgu_norm, w_spatial=v_w_spatial, b_spatial=v_b_spatial,
                w_out_odd=v_w_out_odd, w_ffn_gate=v_w_ffn_gate, w_ffn_up=v_w_ffn_up, conv_ffn=v_conv_ffn,
                b_conv_ffn=v_b_conv_ffn, w_ffn_down=v_w_ffn_down)
    order = list(weights)

    chip = 2 * lax.axis_index("x") + lax.axis_index("y")
    core = lax.axis_index("c")
    place = jnp.stack([chip, core]).astype(jnp.int32)
    chip_arr = place[:1]

    h0 = x[0]
    target = loss_target[0]
    d_model = h0.shape[1]
    f_shard = w_ffn_gate.shape[-1]

    def turned(a):
        return jnp.transpose(a, (0, 2, 1))

    def own_slot(v):
        return lax.dynamic_update_index_in_dim(jnp.zeros((N_CHIPS,) + v.shape, v.dtype), v, chip, 0)

    first = [[cast_into_slot(w_in_even[0], chip_arr, name="cast_win_e"),
              cast_into_slot(w_out_even[0], chip_arr, name="cast_wout_e")]]
    smalls = [own_slot(conv_a[0]), own_slot(sgu_norm), own_slot(conv_ffn[0]), own_slot(conv_ffn[1])]
    sems0, first, smalls, token0 = gather_start(first, smalls, name="gather_start0")
    rest = [
        [cast_into_slot(turned(w_ffn_gate), chip_arr, l=0, name="cast_wg0", pin=token0),
         cast_into_slot(turned(w_ffn_up), chip_arr, l=0, name="cast_wu0", pin=token0)],
        [cast_into_slot(w_ffn_down, chip_arr, l=0, name="cast_wd0", pin=token0)],
        [cast_into_slot(w_in_odd[0], chip_arr, name="cast_win_o", pin=token0),
         cast_into_slot(w_out_odd[0], chip_arr, name="cast_wout_o", pin=token0)],
        [cast_into_slot(turned(w_ffn_gate), chip_arr, l=1, name="cast_wg1", pin=token0),
         cast_into_slot(turned(w_ffn_up), chip_arr, l=1, name="cast_wu1", pin=token0)],
        [cast_into_slot(w_ffn_down, chip_arr, l=1, name="cast_wd1", pin=token0)]]
    sems1, rest, _, token = gather_start(rest, [], name="gather_start1")
    groups = first + rest
    sems = list(sems0[:2]) + list(sems1) + list(sems0[2:])

    def arrive(gi, after, with_smalls=False):
        kw = dict(smalls=smalls, small_send=sems[-2], small_recv=sems[-1]) if with_smalls else {}
        bufs, small_out = gather_wait(groups[gi], sems[2 * gi], sems[2 * gi + 1], after, name=f"gather_wait{gi}", **kw)
        return gather_forward(bufs, name=f"gather_forward{gi}"), small_out

    def arrive_begin(gi, after):
        bufs, _ = gather_wait(groups[gi], sems[2 * gi], sems[2 * gi + 1], after, name=f"gather_wait{gi}")
        send, recv, bufs, _, tok = split_start(bufs, [], _forward_copies, name=f"gather_forward_start{gi}")
        return (send, recv, bufs), tok

    def arrive_end(state, after, gi):
        send, recv, bufs = state
        return split_wait(bufs, [], send, recv, _forward_copies, after, name=f"gather_forward_wait{gi}")[0]

    cb = b_conv_ffn.reshape(-1, N_CHIPS, 1, f_shard)
    wp_bf = w_pool[0].astype(BF16)
    wp_t_bf = jnp.transpose(w_pool[0], (0, 2, 1)).astype(BF16)
    ws = w_spatial[0]
    bs = b_spatial[0][:, :, None]

    (win_e, wout_e), (ca_g, sn_g, cw0, cw1) = arrive(0, token, with_smalls=True)
    wout_e = wout_e.reshape(-1, d_model)
    ca_full = jnp.transpose(ca_g, (1, 0, 2)).reshape(ca_g.shape[1], -1)
    sn_full = sn_g.reshape(1, -1)
    h1, xn0, rstd0, proj0, mix0, hn0, rstdf0 = even_layer_fwd(
        h0, norm_mix[0:1], win_e, ca_full, wp_bf, pool_scale, wout_e, norm_ffn[0:1], name="l0_fwd")
    (wg0, wu0), _ = arrive(1, hn0)
    g0, up0, gc0, act0 = ffn_in_fwd(hn0, wg0, wu0, cw0, cb[0], name="ffn0_in")
    swap2, tok2 = arrive_begin(2, act0)
    swap3, tok3 = arrive_begin(3, tok2)
    (wd0,) = arrive_end(swap2, tok3, 2)
    h2 = mm_acc(act0, wd0, h1, name="ffn0_down")
    ffn0 = (hn0, rstdf0, g0, up0, gc0)
    swap4, tok4 = arrive_begin(4, h2)
    swap5, tok5 = arrive_begin(5, tok4)
    win_o, wout_o = arrive_end(swap3, tok5, 3)
    wout_o = wout_o.reshape(-1, d_model)
    h3, xn1, rstd1, p1, mix1, rstd_v = odd_layer_fwd(h2, norm_mix[1:2], win_o, sn_full, ws, bs, wout_o, name="l1_fwd")
    wg1, wu1 = arrive_end(swap4, h3, 4)
    (wd1,) = arrive_end(swap5, wg1, 5)
    dh4, dh4_bf, loss_row, d_final, hn1, rstdf1, g1, up1, gc1 = ffn_loss_fwd(
        h3, norm_ffn[1:2], wg1, wu1, wd1, cw1, cb[1], target, final_norm[None], name="ffn1_fwd_loss")
    ffn1 = (hn1, rstdf1, g1, up1, gc1)

    loss = lax.psum(loss_row[0, 0], ("x", "y", "c"))

    dh3, dh3_bf, (dcw1, dcb1, dnf1), red3 = _ffn_bwd(
        dh4, dh4_bf, h3, norm_ffn[1:2], ffn1, wg1, wu1, wd1, cw1, cb[1], place, 1, None)

    def as_blocks(a):
        return a.reshape(N_CHIPS, -1, d_model)

    dp1, dsn, dws, dbs = sgu_bwd(p1, dh3_bf, wout_o, rstd_v, sn_full, ws, bs, name="l1_mix_bwd", pin=red3[-1])
    dwout_o = as_blocks(mm_tn(mix1[None], dh3_bf[None], name="l1_dwout"))
    dwin_o = mm_tn_shared(xn1, dp1, name="l1_dwin")
    red2 = _reduce_a_begin([(dwin_o, BF16), (dwout_o, BF16)], tag="m1")
    dh2, dh2_bf, dnm1 = dx_rms_bwd([(dp1, win_o)], h2, norm_mix[1:2], rstd1, dh3, name="l1_dx", pin=red2[-1])
    red2 = _reduce_b_begin(red2, place, dh2_bf, tag="m1")

    dh1, dh1_bf, (dcw0, dcb0, dnf0), red1 = _ffn_bwd(
        dh2, dh2_bf, h1, norm_ffn[0:1], ffn0, wg0, wu0, wd0, cw0, cb[0], place, 0, red2[-1])

    dproj0, dca, dwp, dps = even_bwd(proj0, dh1_bf, wout_e, ca_full, wp_bf, wp_t_bf, pool_scale,
                                     name="l0_mix_bwd", pin=red1[-1])
    dwout_e = as_blocks(mm_tn(mix0[None], dh1_bf[None], name="l0_dwout"))
    dwin_e = mm_tn_shared(xn0, dproj0, name="l0_dwin")
    dh0, _, dnm0 = dx_rms_bwd([(dproj0, win_e)], h0, norm_mix[0:1], rstd0, dh1, name="l0_dx")
    grad_x = dh0[None]

    small_parts = dict(
        norm_mix=jnp.concatenate([dnm0, dnm1]), norm_ffn=jnp.concatenate([dnf0, dnf1]), final_norm=d_final,
        conv_a=dca, w_pool=dwp, pool_scale=dps, sgu_norm=dsn, w_spatial=dws, b_spatial=dbs,
        conv_ffn=jnp.stack([dcw0, dcw1]), b_conv_ffn=jnp.stack([dcb0, dcb1]))
    flat = jnp.concatenate([v.reshape(-1) for v in small_parts.values()])
    pad = (-flat.shape[0]) % (N_CHIPS * 32 * 128)
    small = jnp.pad(flat, (0, pad)).reshape(N_CHIPS, -1, 128)
    red0 = _reduce_a_begin([(dwin_e, BF16), (dwout_e, BF16), (small, F32)], tag="m0")

    swap_f1 = _reduce_c_begin(red3, place, red0[-1], tag="f1")
    red0 = _reduce_b_begin(red0, place, swap_f1[-1], tag="m0")
    swap_m1 = _reduce_c_begin(red2, place, red0[-1], tag="m1")
    swap_f0 = _reduce_c_begin(red1, place, swap_m1[-1], tag="f0")
    full3 = _reduce_finish(swap_f1, swap_f0[-1], tag="f1")
    full2 = _reduce_finish(swap_m1, full3[0], tag="m1")
    full1 = _reduce_finish(swap_f0, full2[0], tag="f0")
    swap_m0 = _reduce_c_begin(red0, place, full1[0], tag="m0")
    full0 = _reduce_finish(swap_m0, swap_m0[-1], tag="m0")
    small_slots = lax.dynamic_update_index_in_dim(jnp.zeros(small.shape, F32), full0[2], chip, 0)
    small_sum = gather_chip_blocks(small_slots, name="gather_small").reshape(-1)
    grads = {
        "w_in_even": full0[0][None], "w_out_even": full0[1][None],
        "w_in_odd": full2[0][None], "w_out_odd": full2[1][None],
        }
    layered = {"w_ffn_gate": [full1[0], full3[0]], "w_ffn_up": [full1[1], full3[1]],
               "w_ffn_down": [full1[2], full3[2]]}
    off = 0
    small_red = {}
    for nm, v in small_parts.items():
        small_red[nm] = small_sum[off:off + v.size].reshape(v.shape)
        off += v.size
    for nm in ("norm_mix", "norm_ffn", "pool_scale"):
        grads[nm] = small_red[nm].reshape(weights[nm].shape)
    grads["final_norm"] = small_red["final_norm"].reshape(weights["final_norm"].shape)
    grads["w_pool"] = small_red["w_pool"][None]
    grads["w_spatial"] = small_red["w_spatial"][None]
    grads["b_spatial"] = small_red["b_spatial"].reshape(weights["b_spatial"].shape)
    grads["b_conv_ffn"] = small_red["b_conv_ffn"].reshape(weights["b_conv_ffn"].shape)
    grads["conv_a"] = lax.dynamic_slice_in_dim(small_red["conv_a"], chip * conv_a.shape[-1], conv_a.shape[-1], 1)[None]
    grads["sgu_norm"] = lax.dynamic_slice_in_dim(small_red["sgu_norm"], chip * sgu_norm.shape[-1], sgu_norm.shape[-1], 1)
    grads["conv_ffn"] = lax.dynamic_index_in_dim(small_red["conv_ffn"], chip, 1, keepdims=False)

    deltas, new_m, new_v = {}, {}, {}
    for nm, per_layer in layered.items():
        if nm == "w_ffn_down":
            grads[nm], deltas[nm], new_m[nm], new_v[nm] = adamw_layers(
                weights[nm], per_layer, m_in[nm], v_in[nm], name=f"adamw_{nm}")
        else:
            outs = adamw_layers(turned(weights[nm]), per_layer, turned(m_in[nm]), turned(v_in[nm]),
                                name=f"adamw_{nm}")
            grads[nm], deltas[nm], new_m[nm], new_v[nm] = (turned(o) for o in outs)
    for nm in order:
        if nm in layered:
            continue
        w = weights[nm]
        w2 = w[None] if w.ndim == 1 else w
        shp = w2.shape
        d, nm_, nv_ = adamw(w2, grads[nm].reshape(shp), m_in[nm].reshape(shp), v_in[nm].reshape(shp),
                            name=f"adamw_{nm}")
        deltas[nm], new_m[nm], new_v[nm] = d.reshape(w.shape), nm_.reshape(w.shape), nv_.reshape(w.shape)

    return (loss, grad_x, *[grads[n] for n in order], *[deltas[n] for n in order],
            *[new_m[n] for n in order], *[new_v[n] for n in order])
```

```python
import jax
import jax.numpy as jnp
from jax import lax
from jax.experimental import pallas as pl
from jax.experimental.pallas import tpu as pltpu

F32 = jnp.float32
BF16 = jnp.bfloat16
MESH = pl.DeviceIdType.MESH
ANY = pl.BlockSpec(memory_space=pl.ANY)

EPS = 1e-6
POOL_WINDOWS = (2, 4, 8, 16)
GROUP = 128
CHUNK = 128
N_CHIPS = 4
N_DEV = 8
ROW_TILE = 512
HALO = 16
VMEM_LIMIT = 56 * 1024 * 1024

ADAM_LR = 0.001
ADAM_B1 = 0.9
ADAM_B2 = 0.999
ADAM_EPS = 1e-08
ADAM_WD = 0.01
ADAM_STEP = 10


def _params(*sem):
    return pltpu.CompilerParams(dimension_semantics=sem, vmem_limit_bytes=VMEM_LIMIT)


def mm_acc(a, b, res, *, name):
    j_n, s, kj = a.shape
    n = b.shape[-1]
    tm = min(ROW_TILE, s)

    def body(a_ref, b_ref, r_ref, o_ref):
        acc = r_ref[...]
        for j in range(j_n):
            acc = acc + jnp.dot(a_ref[j], b_ref[j], preferred_element_type=F32)
        o_ref[...] = acc

    return pl.pallas_call(
        body, name=name, grid=(s // tm,),
        in_specs=[pl.BlockSpec((j_n, tm, kj), lambda i: (0, i, 0)),
                  pl.BlockSpec((j_n, kj, n), lambda i: (0, 0, 0)),
                  pl.BlockSpec((tm, n), lambda i: (i, 0))],
        out_specs=pl.BlockSpec((tm, n), lambda i: (i, 0)),
        out_shape=jax.ShapeDtypeStruct((s, n), F32),
        compiler_params=_params("parallel"))(a, b, res)


_NT = (((1,), (1,)), ((), ()))
_TN = (((0,), (0,)), ((), ()))


def mm_tn(a, dy, *, name, pin=None):
    ja, s, k = a.shape
    jd, _, n = dy.shape
    j_n = max(ja, jd)
    tk = min(ROW_TILE, s)

    def body(a_ref, d_ref, *rest):
        o_ref = rest[-1]

        @pl.when(pl.program_id(1) == 0)
        def _():
            o_ref[...] = jnp.zeros_like(o_ref)
        o_ref[...] += lax.dot_general(a_ref[...], d_ref[...], _TN, preferred_element_type=F32)

    return pl.pallas_call(
        body, name=name, grid=(j_n, s // tk),
        in_specs=[pl.BlockSpec((None, tk, k), (lambda j, i: (j, i, 0)) if ja > 1 else (lambda j, i: (0, i, 0))),
                  pl.BlockSpec((None, tk, n), (lambda j, i: (j, i, 0)) if jd > 1 else (lambda j, i: (0, i, 0)))]
        + ([ANY] if pin is not None else []),
        out_specs=pl.BlockSpec((None, k, n), lambda j, i: (j, 0, 0)),
        out_shape=jax.ShapeDtypeStruct((j_n, k, n), F32),
        compiler_params=_params("parallel", "arbitrary"))(a, dy, *([pin] if pin is not None else []))


def mm_tn_shared(a, dy, *, name):
    s, k = a.shape
    j_n, _, n = dy.shape
    tk = min(ROW_TILE, s)

    def body(a_ref, d_ref, o_ref):
        @pl.when(pl.program_id(0) == 0)
        def _():
            o_ref[...] = jnp.zeros_like(o_ref)
        a_t = a_ref[...]
        for j in range(j_n):
            o_ref[j] += lax.dot_general(a_t, d_ref[j], _TN, preferred_element_type=F32)

    return pl.pallas_call(
        body, name=name, grid=(s // tk,),
        in_specs=[pl.BlockSpec((tk, k), lambda i: (i, 0)), pl.BlockSpec((j_n, tk, n), lambda i: (0, i, 0))],
        out_specs=pl.BlockSpec((j_n, k, n), lambda i: (0, 0, 0)),
        out_shape=jax.ShapeDtypeStruct((j_n, k, n), F32),
        compiler_params=_params("arbitrary"))(a, dy)


def _back(x, k):
    return pltpu.roll(x, k, 0)


def _fwd(x, k):
    return pltpu.roll(x, x.shape[0] - k, 0)


def _causal_conv(x, w_ref):
    return w_ref[0:1, :] * _back(x, 2) + w_ref[1:2, :] * _back(x, 1) + w_ref[2:3, :] * x


def _causal_conv_t(dy, w_ref):
    return w_ref[2:3, :] * dy + w_ref[1:2, :] * _fwd(dy, 1) + w_ref[0:1, :] * _fwd(dy, 2)


def _gelu(x):
    return 0.5 * x * (1.0 + lax.erf(x * 0.7071067811865476))


def _gelu_grad(x):
    return 0.5 * (1.0 + lax.erf(x * 0.7071067811865476)) + x * jnp.exp(-0.5 * x * x) * 0.3989422804014327


def _colsum(x):
    return jnp.sum(x, axis=0, keepdims=True)


def _halo_specs(n_lead, ts, width, n_tiles):
    hb = ts // HALO
    prev = pl.BlockSpec((n_lead, HALO, width), lambda i: (0, jnp.maximum(i * hb - 1, 0), 0))
    nxt = pl.BlockSpec((n_lead, HALO, width), lambda i: (0, jnp.minimum((i + 1) * hb, n_tiles * hb - 1), 0))
    return prev, nxt


def _pool_fwd(z_ext, g, pos):
    w = POOL_WINDOWS[g]
    zg = z_ext[:, g * GROUP:(g + 1) * GROUP]
    acc = zg
    sh = 1
    while sh < w:
        acc = acc + _back(acc, sh)
        sh *= 2
    return acc[HALO:] / jnp.minimum(pos, float(w)) - zg[HALO:]


def even_layer_fwd(h, gain, win, conv_a, w_pool, pool_scale, wout, gain_next, *, name, pin=None):
    s, d = h.shape
    w = win.shape[-1]
    ts = min(ROW_TILE, s)
    hb = ts // HALO

    def body(h_ref, hp_ref, gain_ref, win_ref, ca_ref, wp_ref, ps_ref, wout_ref, gn_ref, *rest):
        o_ref, xn_ref, r_ref, p_ref, m_ref, hn_ref, rn_ref = rest[-7:]
        i = pl.program_id(0)
        keep = jnp.where(i > 0, 1.0, 0.0)
        h_ext = jnp.concatenate([hp_ref[...], h_ref[...]], axis=0)
        rstd = lax.rsqrt(jnp.mean(h_ext * h_ext, axis=-1, keepdims=True) + EPS)
        xn_ext = (h_ext * rstd * gain_ref[...]).astype(BF16)
        xn_ref[...] = xn_ext[HALO:]
        r_ref[...] = rstd[HALO:]
        p32 = []
        for k in range(4):
            pk = jnp.dot(xn_ext, win_ref[k], preferred_element_type=F32).astype(BF16)
            p_ref[k] = pk[HALO:]
            pk = pk.astype(F32)
            p32.append(jnp.concatenate([pk[:HALO] * keep, pk[HALO:]], axis=0))
        m_ref[:, 0:w] = (p32[0][HALO:] * _causal_conv(p32[1] * p32[2], ca_ref)[HALO:]).astype(BF16)
        pos = (i * ts + lax.broadcasted_iota(jnp.int32, (ts, 1), 0) + 1).astype(F32)
        for g in range(len(POOL_WINDOWS)):
            pooled = _pool_fwd(p32[3], g, pos)
            mixed = jnp.dot(pooled.astype(BF16), wp_ref[g], preferred_element_type=F32)
            cols = slice(g * GROUP, (g + 1) * GROUP)
            m_ref[:, w + g * GROUP:w + (g + 1) * GROUP] = (mixed * ps_ref[:, cols]).astype(BF16)
        out = h_ref[...] + jnp.dot(m_ref[...], wout_ref[...], preferred_element_type=F32)
        o_ref[...] = out
        rstd_n = lax.rsqrt(jnp.mean(out * out, axis=-1, keepdims=True) + EPS)
        hn_ref[...] = (out * rstd_n * gn_ref[...]).astype(BF16)
        rn_ref[...] = rstd_n

    const = lambda shape: pl.BlockSpec(shape, lambda i: (0,) * len(shape))
    row = pl.BlockSpec((ts, d), lambda i: (i, 0))
    col1 = pl.BlockSpec((ts, 1), lambda i: (i, 0))
    return pl.pallas_call(
        body, name=name, grid=(s // ts,),
        in_specs=[row, pl.BlockSpec((HALO, d), lambda i: (jnp.maximum(i * hb - 1, 0), 0)), const((1, d)),
                  const((4, d, w)), const((3, w)), const((4, GROUP, GROUP)), const((1, w)), const((2 * w, d)),
                  const((1, d))] + ([ANY] if pin is not None else []),
        out_specs=[row, row, col1, pl.BlockSpec((4, ts, w), lambda i: (0, i, 0)),
                   pl.BlockSpec((ts, 2 * w), lambda i: (i, 0)), row, col1],
        out_shape=[jax.ShapeDtypeStruct((s, d), F32), jax.ShapeDtypeStruct((s, d), BF16),
                   jax.ShapeDtypeStruct((s, 1), F32), jax.ShapeDtypeStruct((4, s, w), BF16),
                   jax.ShapeDtypeStruct((s, 2 * w), BF16), jax.ShapeDtypeStruct((s, d), BF16),
                   jax.ShapeDtypeStruct((s, 1), F32)],
        compiler_params=_params("parallel"))(
            h, h, gain, win, conv_a, w_pool, pool_scale, wout, gain_next, *([pin] if pin is not None else []))


def even_bwd(proj, mix, dh_bf, wout, conv_a, w_pool, w_pool_t, pool_scale, *, name, pin=None):
    _, s, w = proj.shape
    d = dh_bf.shape[1]
    ts = min(ROW_TILE, s)
    n_t = s // ts
    prev, nxt = _halo_specs(4, ts, w, n_t)
    hb = ts // HALO
    n_ext = ts + HALO

    def body(p_ref, pp_ref, pn_ref, m_ref, dh_ref, dhn_ref, wout_ref, ca_ref, wp_ref, wpt_ref, ps_ref, *rest):
        dp_ref, dwo_ref, dca_ref, dwp_ref, dps_ref = rest[-5:]
        i = pl.program_id(0)

        @pl.when(i == 0)
        def _():
            dwo_ref[...] = jnp.zeros_like(dwo_ref)
            dca_ref[...] = jnp.zeros_like(dca_ref)
            dwp_ref[...] = jnp.zeros_like(dwp_ref)
            dps_ref[...] = jnp.zeros_like(dps_ref)

        keep_p = jnp.where(i > 0, 1.0, 0.0)
        keep_n = jnp.where(i < n_t - 1, 1.0, 0.0)
        dwo_ref[...] += lax.dot_general(m_ref[...], dh_ref[...], _TN, preferred_element_type=F32)
        dmix = lax.dot_general(jnp.concatenate([dh_ref[...], dhn_ref[...]], axis=0), wout_ref[...], _NT,
                               preferred_element_type=F32)
        a_b, a_c, a_v = (p_ref[k].astype(F32) for k in range(3))
        cv_ext = jnp.concatenate([pp_ref[1].astype(F32) * pp_ref[2].astype(F32) * keep_p, a_c * a_v], axis=0)
        dy_a = dmix[:ts, 0:w]
        dp_ref[0] = (dy_a * _causal_conv(cv_ext, ca_ref)[HALO:]).astype(BF16)
        dcc = dy_a * a_b
        dca_ref[2:3, :] += _colsum(dcc * cv_ext[HALO:])
        dca_ref[1:2, :] += _colsum(dcc * _back(cv_ext, 1)[HALO:])
        dca_ref[0:1, :] += _colsum(dcc * _back(cv_ext, 2)[HALO:])
        dcc_ext = jnp.concatenate([dcc, dmix[ts:, 0:w] * pn_ref[0].astype(F32) * keep_n], axis=0)
        dcv = _causal_conv_t(dcc_ext, ca_ref)[:ts]
        dp_ref[1] = (dcv * a_v).astype(BF16)
        dp_ref[2] = (dcv * a_c).astype(BF16)
        z_ext = jnp.concatenate([pp_ref[3].astype(F32) * keep_p, p_ref[3].astype(F32)], axis=0)
        pos = (i * ts + lax.broadcasted_iota(jnp.int32, (ts, 1), 0) + 1).astype(F32)
        pos_ext = (i * ts + lax.broadcasted_iota(jnp.int32, (n_ext, 1), 0) + 1).astype(F32)
        for g, win in enumerate(POOL_WINDOWS):
            cols = slice(g * GROUP, (g + 1) * GROUP)
            ycols = slice(w + g * GROUP, w + (g + 1) * GROUP)
            pooled = _pool_fwd(z_ext, g, pos).astype(BF16)
            mixed = jnp.dot(pooled, wp_ref[g], preferred_element_type=F32)
            dy_b = dmix[:ts, ycols]
            dps_ref[:, cols] += _colsum(dy_b * mixed)
            dmixed_ext = jnp.concatenate([dy_b, dmix[ts:, ycols] * keep_n], axis=0) * ps_ref[:, cols]
            dmixed_ext = dmixed_ext.astype(BF16)
            dwp_ref[g] += lax.dot_general(pooled, dmixed_ext[:ts], _TN, preferred_element_type=F32)
            dpooled = jnp.dot(dmixed_ext, wpt_ref[g], preferred_element_type=F32)
            acc = dpooled / jnp.minimum(pos_ext, float(win))
            sh = 1
            while sh < win:
                acc = acc + _fwd(acc, sh)
                sh *= 2
            dp_ref[3, :, cols] = (acc[:ts] - dpooled[:ts]).astype(BF16)

    tile4 = pl.BlockSpec((4, ts, w), lambda i: (0, i, 0))
    const = lambda shape: pl.BlockSpec(shape, lambda i: (0,) * len(shape))
    return pl.pallas_call(
        body, name=name, grid=(n_t,),
        in_specs=[tile4, prev, nxt, pl.BlockSpec((ts, 2 * w), lambda i: (i, 0)), pl.BlockSpec((ts, d), lambda i: (i, 0)),
                  pl.BlockSpec((HALO, d), lambda i: (jnp.minimum((i + 1) * hb, n_t * hb - 1), 0)),
                  const((2 * w, d)), const((3, w)), const((4, GROUP, GROUP)), const((4, GROUP, GROUP)), const((1, w))]
        + ([ANY] if pin is not None else []),
        out_specs=[tile4, const((2 * w, d)), const((3, w)), const((4, GROUP, GROUP)), const((1, w))],
        out_shape=[jax.ShapeDtypeStruct((4, s, w), BF16), jax.ShapeDtypeStruct((2 * w, d), F32),
                   jax.ShapeDtypeStruct((3, w), F32),
                   jax.ShapeDtypeStruct((4, GROUP, GROUP), F32), jax.ShapeDtypeStruct((1, w), F32)],
        compiler_params=_params("arbitrary"))(
            proj, proj, proj, mix, dh_bf, dh_bf, wout, conv_a, w_pool, w_pool_t, pool_scale,
            *([pin] if pin is not None else []))


def _ffn_next_halo(ts, f, n_t):
    hb = ts // HALO
    return pl.BlockSpec((None, HALO, f), lambda j, i: (j, jnp.minimum((i + 1) * hb, n_t * hb - 1), 0))


def ffn_in_fwd(hn, wg, wu, cw, cb, *, name, pin=None):
    s, d = hn.shape
    j_n, f, _ = wg.shape
    tm = min(ROW_TILE, s)
    hb = tm // HALO

    def body(x_ref, xp_ref, wg_ref, wu_ref, cw_ref, cb_ref, *rest):
        g_ref, u_ref, gc_ref, a_ref = rest[-4:]
        i, j = pl.program_id(0), pl.program_id(1)
        x_ext = jnp.concatenate([xp_ref[...], x_ref[...]], axis=0)
        g_ext = lax.dot_general(x_ext, wg_ref[j], _NT, preferred_element_type=F32).astype(BF16)
        up = lax.dot_general(x_ref[...], wu_ref[j], _NT, preferred_element_type=F32).astype(BF16)
        g_ref[...] = g_ext[HALO:]
        u_ref[...] = up
        a_ref[...] = _ffn_act(g_ext, up, cw_ref, cb_ref, gc_ref, i)

    whole = pl.BlockSpec((j_n, f, d), lambda i, j: (0, 0, 0))
    tile = pl.BlockSpec((None, tm, f), lambda i, j: (j, i, 0))
    shape = jax.ShapeDtypeStruct((j_n, s, f), BF16)
    return pl.pallas_call(
        body, name=name, grid=(s // tm, j_n),
        in_specs=[pl.BlockSpec((tm, d), lambda i, j: (i, 0)),
                  pl.BlockSpec((HALO, d), lambda i, j: (jnp.maximum(i * hb - 1, 0), 0)),
                  whole, whole,
                  pl.BlockSpec((None, 3, f), lambda i, j: (j, 0, 0)),
                  pl.BlockSpec((None, 1, f), lambda i, j: (j, 0, 0))] + ([ANY] if pin is not None else []),
        out_specs=[tile] * 4, out_shape=[shape] * 4,
        compiler_params=_params("parallel", "parallel"))(hn, hn, wg, wu, cw, cb, *([pin] if pin is not None else []))


def _ffn_act(g_ext, up, cw_ref, cb_ref, gc_ref, i):
    keep = jnp.where(i > 0, 1.0, 0.0)
    g32 = jnp.concatenate([g_ext[:HALO].astype(F32) * keep, g_ext[HALO:].astype(F32)], axis=0)
    gc = (_causal_conv(g32, cw_ref)[HALO:] + cb_ref[...]).astype(BF16)
    gc_ref[...] = gc
    gc = gc.astype(F32)
    return (gc * jax.nn.sigmoid(gc) * up.astype(F32)).astype(BF16)


def ffn_loss_fwd(h, gain, wg, wu, wd, cw, cb, target, final_gain, *, name):
    s, d = h.shape
    j_n, f, _ = wg.shape
    tm = min(ROW_TILE, s)
    hb = tm // HALO

    def body(h_ref, hp_ref, gain_ref, wg_ref, wu_ref, wd_ref, cw_ref, cb_ref, t_ref, fg_ref,
             o_ref, ob_ref, l_ref, dfg_ref, xn_ref, r_ref, g_ref, u_ref, gc_ref, x_s, acc_s):
        i, j = pl.program_id(0), pl.program_id(1)

        @pl.when((i == 0) & (j == 0))
        def _():
            l_ref[...] = jnp.zeros_like(l_ref)
            dfg_ref[...] = jnp.zeros_like(dfg_ref)

        @pl.when(j == 0)
        def _():
            h_ext = jnp.concatenate([hp_ref[...], h_ref[...]], axis=0)
            rstd = lax.rsqrt(jnp.mean(h_ext * h_ext, axis=-1, keepdims=True) + EPS)
            x_s[...] = (h_ext * rstd * gain_ref[...]).astype(BF16)
            xn_ref[...] = x_s[HALO:, :]
            r_ref[...] = rstd[HALO:]
            acc_s[...] = h_ref[...]

        g_ext = lax.dot_general(x_s[...], wg_ref[j], _NT, preferred_element_type=F32).astype(BF16)
        up = lax.dot_general(x_s[HALO:, :], wu_ref[j], _NT, preferred_element_type=F32).astype(BF16)
        g_ref[...] = g_ext[HALO:]
        u_ref[...] = up
        act = _ffn_act(g_ext, up, cw_ref, cb_ref, gc_ref, i)
        acc_s[...] += jnp.dot(act, wd_ref[j], preferred_element_type=F32)

        @pl.when(j == j_n - 1)
        def _():
            x = acc_s[...]
            rstd = lax.rsqrt(jnp.mean(x * x, axis=-1, keepdims=True) + EPS)
            hhat = x * rstd
            err = hhat * fg_ref[...] - t_ref[...]
            l_ref[...] += 0.5 * jnp.sum(jnp.mean(err * err, axis=-1, keepdims=True), axis=0, keepdims=True)
            dy = err * (1.0 / d)
            dfg_ref[...] += _colsum(dy * hhat)
            dyg = dy * fg_ref[...]
            dh = rstd * (dyg - hhat * jnp.mean(dyg * hhat, axis=-1, keepdims=True))
            o_ref[...] = dh
            ob_ref[...] = dh.astype(BF16)

    whole = pl.BlockSpec((j_n, f, d), lambda i, j: (0, 0, 0), pipeline_mode=pl.Buffered(1))
    row = pl.BlockSpec((tm, d), lambda i, j: (i, 0))
    vec = pl.BlockSpec((1, d), lambda i, j: (0, 0))
    tile = pl.BlockSpec((None, tm, f), lambda i, j: (j, i, 0))
    return pl.pallas_call(
        body, name=name, grid=(s // tm, j_n),
        in_specs=[row, pl.BlockSpec((HALO, d), lambda i, j: (jnp.maximum(i * hb - 1, 0), 0)),
                  vec, whole, whole, whole,
                  pl.BlockSpec((None, 3, f), lambda i, j: (j, 0, 0)),
                  pl.BlockSpec((None, 1, f), lambda i, j: (j, 0, 0)), row, vec],
        out_specs=[row, row, pl.BlockSpec((1, 128), lambda i, j: (0, 0)), vec,
                   row, pl.BlockSpec((tm, 1), lambda i, j: (i, 0)), tile, tile, tile],
        out_shape=[jax.ShapeDtypeStruct((s, d), F32), jax.ShapeDtypeStruct((s, d), BF16),
                   jax.ShapeDtypeStruct((1, 128), F32), jax.ShapeDtypeStruct((1, d), F32),
                   jax.ShapeDtypeStruct((s, d), BF16), jax.ShapeDtypeStruct((s, 1), F32)]
        + [jax.ShapeDtypeStruct((j_n, s, f), BF16)] * 3,
        scratch_shapes=[pltpu.VMEM((HALO + tm, d), BF16), pltpu.VMEM((tm, d), F32)],
        compiler_params=_params("arbitrary", "arbitrary"))(h, h, gain, wg, wu, wd, cw, cb, target, final_gain)


def ffn_bwd_a(dh_bf, hn, g, up, gc, wd, cw, *, name, pin=None):
    s, d = hn.shape
    j_n, _, f = g.shape
    tm = min(ROW_TILE, s)
    n_t = s // tm
    hb = tm // HALO

    def body(dh_ref, dhn_ref, x_ref, g_ref, gc_ref, gcn_ref, u_ref, un_ref, wd_ref, cw_ref, *rest):
        dg_ref, du_ref, dwg_ref, dwu_ref, dwd_ref, dcw_ref, dcb_ref = rest[-7:]
        i = pl.program_id(1)

        @pl.when(i == 0)
        def _():
            for r in (dwg_ref, dwu_ref, dwd_ref, dcw_ref, dcb_ref):
                r[...] = jnp.zeros_like(r)

        keep_n = jnp.where(i < n_t - 1, 1.0, 0.0)
        dh = dh_ref[...]
        dact = lax.dot_general(jnp.concatenate([dh, dhn_ref[...]], axis=0), wd_ref[...], _NT,
                               preferred_element_type=F32)
        dact = jnp.concatenate([dact[:tm], dact[tm:] * keep_n], axis=0)
        gc_ext = jnp.concatenate([gc_ref[...], gcn_ref[...]], axis=0).astype(F32)
        sig = jax.nn.sigmoid(gc_ext)
        silu = gc_ext * sig
        up_ext = jnp.concatenate([u_ref[...], un_ref[...]], axis=0).astype(F32)
        act = (silu * up_ext)[:tm].astype(BF16)
        dwd_ref[...] += lax.dot_general(act, dh, _TN, preferred_element_type=F32)
        dup = (dact * silu)[:tm].astype(BF16)
        du_ref[...] = dup
        dgc = dact * up_ext * (sig + silu * (1.0 - sig))
        dgc_1, dgc_2 = _fwd(dgc, 1), _fwd(dgc, 2)
        dg = (cw_ref[2:3, :] * dgc + cw_ref[1:2, :] * dgc_1 + cw_ref[0:1, :] * dgc_2)[:tm].astype(BF16)
        dg_ref[...] = dg
        x = x_ref[...]
        dwg_ref[...] += lax.dot_general(dg, x, _TN, preferred_element_type=F32)
        dwu_ref[...] += lax.dot_general(dup, x, _TN, preferred_element_type=F32)
        g32 = g_ref[...].astype(F32)
        dcb_ref[...] += _colsum(dgc[:tm])
        dcw_ref[2:3, :] += _colsum(dgc[:tm] * g32)
        dcw_ref[1:2, :] += _colsum(dgc_1[:tm] * g32)
        dcw_ref[0:1, :] += _colsum(dgc_2[:tm] * g32)

    rows = pl.BlockSpec((tm, d), lambda j, i: (i, 0))
    rows_next = pl.BlockSpec((HALO, d), lambda j, i: (jnp.minimum((i + 1) * hb, n_t * hb - 1), 0))
    tile = pl.BlockSpec((None, tm, f), lambda j, i: (j, i, 0))
    nxt = _ffn_next_halo(tm, f, n_t)
    per_j = lambda r, c: pl.BlockSpec((None, r, c), lambda j, i: (j, 0, 0))
    return pl.pallas_call(
        body, name=name, grid=(j_n, n_t),
        in_specs=[rows, rows_next, rows, tile, tile, nxt, tile, nxt, per_j(f, d), per_j(3, f)]
        + ([ANY] if pin is not None else []),
        out_specs=[tile, tile, per_j(f, d), per_j(f, d), per_j(f, d), per_j(3, f), per_j(1, f)],
        out_shape=[jax.ShapeDtypeStruct((j_n, s, f), BF16), jax.ShapeDtypeStruct((j_n, s, f), BF16),
                   jax.ShapeDtypeStruct((j_n, f, d), F32), jax.ShapeDtypeStruct((j_n, f, d), F32),
                   jax.ShapeDtypeStruct((j_n, f, d), F32), jax.ShapeDtypeStruct((j_n, 3, f), F32),
                   jax.ShapeDtypeStruct((j_n, 1, f), F32)],
        compiler_params=_params("parallel", "arbitrary"))(
            dh_bf, dh_bf, hn, g, gc, gc, up, up, wd, cw, *([pin] if pin is not None else []))


def dx_rms_bwd(pairs, h, gain, rstd, dres, *, name, pin=None):
    j_n, s, f = pairs[0][0].shape
    d = h.shape[1]
    tm = min(ROW_TILE, s)
    n_p = len(pairs)
    dims = [_NT if w.shape[1:] == (d, f) else (((1,), (0,)), ((), ())) for _, w in pairs]

    def body(*refs):
        dy_refs, w_refs = refs[:n_p], refs[n_p:2 * n_p]
        h_ref, g_ref, r_ref, dr_ref = refs[2 * n_p:2 * n_p + 4]
        o_ref, ob_ref, dgain_ref = refs[-3:]

        @pl.when(pl.program_id(0) == 0)
        def _():
            dgain_ref[...] = jnp.zeros_like(dgain_ref)
        dx = jnp.zeros((tm, d), F32)
        for j in range(j_n):
            for p in range(n_p):
                dx = dx + lax.dot_general(dy_refs[p][j], w_refs[p][j], dims[p], preferred_element_type=F32)
        rstd_v = r_ref[...]
        hhat = h_ref[...] * rstd_v
        dgain_ref[...] += _colsum(dx * hhat)
        dxg = dx * g_ref[...]
        dh = dr_ref[...] + rstd_v * (dxg - hhat * jnp.mean(dxg * hhat, axis=-1, keepdims=True))
        o_ref[...] = dh
        ob_ref[...] = dh.astype(BF16)

    tile4 = pl.BlockSpec((j_n, tm, f), lambda i: (0, i, 0))
    whole = [pl.BlockSpec(w.shape, lambda i: (0, 0, 0), pipeline_mode=pl.Buffered(1)) for _, w in pairs]
    row = pl.BlockSpec((tm, d), lambda i: (i, 0))
    vec = pl.BlockSpec((1, d), lambda i: (0, 0))
    return pl.pallas_call(
        body, name=name, grid=(s // tm,),
        in_specs=[tile4] * n_p + whole + [row, vec, pl.BlockSpec((tm, 1), lambda i: (i, 0)), row]
        + ([ANY] if pin is not None else []),
        out_specs=[row, row, vec],
        out_shape=[jax.ShapeDtypeStruct((s, d), F32), jax.ShapeDtypeStruct((s, d), BF16),
                   jax.ShapeDtypeStruct((1, d), F32)],
        compiler_params=_params("arbitrary"))(
            *[p[0] for p in pairs], *[p[1] for p in pairs], h, gain, rstd, dres, *([pin] if pin is not None else []))


def _sgu_gate(vn_bf, ws_ref, bs_ref, h, rows):
    tri = lax.broadcasted_iota(jnp.int32, (CHUNK, CHUNK), 0) >= lax.broadcasted_iota(jnp.int32, (CHUNK, CHUNK), 1)
    ws = jnp.where(tri, ws_ref[h], 0.0).astype(BF16)
    cols = slice((h % 4) * GROUP, (h % 4 + 1) * GROUP)
    return ws, jnp.dot(ws, vn_bf[h // 4][rows, cols], preferred_element_type=F32) + bs_ref[h]


def odd_layer_fwd(h, gain, win, sgu_norm, w_spatial, b_spatial, wout, *, name):
    s, d = h.shape
    w = win.shape[-1]
    ts = min(ROW_TILE, s)
    n_heads = w_spatial.shape[0]

    def body(h_ref, gain_ref, win_ref, n_ref, ws_ref, bs_ref, wout_ref, o_ref, xn_ref, r_ref, p_ref, m_ref, rv_ref):
        x = h_ref[...]
        rstd_x = lax.rsqrt(jnp.mean(x * x, axis=-1, keepdims=True) + EPS)
        xn = (x * rstd_x * gain_ref[...]).astype(BF16)
        xn_ref[...] = xn
        r_ref[...] = rstd_x
        for k in range(4):
            p_ref[k] = jnp.dot(xn, win_ref[k], preferred_element_type=F32).astype(BF16)
        v = [_gelu(p_ref[2].astype(F32)), _gelu(p_ref[3].astype(F32))]
        ms = (jnp.sum(v[0] * v[0], axis=-1, keepdims=True) + jnp.sum(v[1] * v[1], axis=-1, keepdims=True)) / (2 * w)
        rstd = lax.rsqrt(ms + EPS)
        rv_ref[...] = rstd
        vn = [(v[k] * rstd * n_ref[:, k * w:(k + 1) * w]).astype(BF16) for k in range(2)]
        for hd in range(n_heads):
            cols = slice((hd % 4) * GROUP, (hd % 4 + 1) * GROUP)
            for c in range(ts // CHUNK):
                rows = slice(c * CHUNK, (c + 1) * CHUNK)
                _, gate = _sgu_gate(vn, ws_ref, bs_ref, hd, rows)
                u = _gelu(p_ref[hd // 4, rows, cols].astype(F32))
                m_ref[rows, hd * GROUP:(hd + 1) * GROUP] = (u * gate).astype(BF16)
        o_ref[...] = x + jnp.dot(m_ref[...], wout_ref[...], preferred_element_type=F32)

    const = lambda shape: pl.BlockSpec(shape, lambda i: (0,) * len(shape))
    row = pl.BlockSpec((ts, d), lambda i: (i, 0))
    col1 = pl.BlockSpec((ts, 1), lambda i: (i, 0))
    return pl.pallas_call(
        body, name=name, grid=(s // ts,),
        in_specs=[row, const((1, d)), const((4, d, w)), const((1, 2 * w)),
                  const((n_heads, CHUNK, CHUNK)), const((n_heads, CHUNK, 1)), const((2 * w, d))],
        out_specs=[row, row, col1, pl.BlockSpec((4, ts, w), lambda i: (0, i, 0)),
                   pl.BlockSpec((ts, 2 * w), lambda i: (i, 0)), col1],
        out_shape=[jax.ShapeDtypeStruct((s, d), F32), jax.ShapeDtypeStruct((s, d), BF16),
                   jax.ShapeDtypeStruct((s, 1), F32), jax.ShapeDtypeStruct((4, s, w), BF16),
                   jax.ShapeDtypeStruct((s, 2 * w), BF16), jax.ShapeDtypeStruct((s, 1), F32)],
        compiler_params=_params("parallel"))(h, gain, win, sgu_norm, w_spatial, b_spatial, wout)


def sgu_bwd(p, mix, dh_bf, wout, rstd, sgu_norm, w_spatial, b_spatial, *, name, pin=None):
    _, s, w = p.shape
    d = dh_bf.shape[1]
    ts = min(ROW_TILE, s)
    n_heads = w_spatial.shape[0]

    def body(p_ref, m_ref, dh_ref, wout_ref, r_ref, n_ref, ws_ref, bs_ref, *rest):
        dp_ref, dwo_ref, dn_ref, dws_ref, dbs_ref, dvn_ref, dm_ref = rest[-7:]

        @pl.when(pl.program_id(0) == 0)
        def _():
            dwo_ref[...] = jnp.zeros_like(dwo_ref)
            dn_ref[...] = jnp.zeros_like(dn_ref)
            dws_ref[...] = jnp.zeros_like(dws_ref)
            dbs_ref[...] = jnp.zeros_like(dbs_ref)

        dwo_ref[...] += lax.dot_general(m_ref[...], dh_ref[...], _TN, preferred_element_type=F32)
        dm_ref[...] = lax.dot_general(dh_ref[...], wout_ref[...], _NT, preferred_element_type=F32)
        rstd_v = r_ref[...]
        vhat = [_gelu(p_ref[2 + k].astype(F32)) * rstd_v for k in range(2)]
        vn = [(vhat[k] * n_ref[:, k * w:(k + 1) * w]).astype(BF16) for k in range(2)]
        tri = lax.broadcasted_iota(jnp.int32, (CHUNK, CHUNK), 0) >= lax.broadcasted_iota(jnp.int32, (CHUNK, CHUNK), 1)
        for h in range(n_heads):
            cols = slice((h % 4) * GROUP, (h % 4 + 1) * GROUP)
            ocols = slice(h * GROUP, (h + 1) * GROUP)
            for c in range(ts // CHUNK):
                rows = slice(c * CHUNK, (c + 1) * CHUNK)
                ws, gate = _sgu_gate(vn, ws_ref, bs_ref, h, rows)
                pu = p_ref[h // 4, rows, cols].astype(F32)
                dm = dm_ref[rows, ocols]
                dp_ref[h // 4, rows, cols] = (dm * gate * _gelu_grad(pu)).astype(BF16)
                dgate = dm * _gelu(pu)
                dbs_ref[h] += jnp.sum(dgate, axis=-1, keepdims=True)
                dgate_bf = dgate.astype(BF16)
                dws = lax.dot_general(dgate_bf, vn[h // 4][rows, cols], _NT, preferred_element_type=F32)
                dws_ref[h] += jnp.where(tri, dws, 0.0)
                dvn_ref[rows, ocols] = lax.dot_general(ws, dgate_bf, _TN, preferred_element_type=F32)
        for k in range(2):
            kc = slice(k * w, (k + 1) * w)
            dvn = dvn_ref[:, kc]
            dn_ref[:, kc] += _colsum(dvn * vhat[k])
        dvh = [dvn_ref[:, k * w:(k + 1) * w] * n_ref[:, k * w:(k + 1) * w] for k in range(2)]
        dot = (jnp.sum(dvh[0] * vhat[0], axis=-1, keepdims=True)
               + jnp.sum(dvh[1] * vhat[1], axis=-1, keepdims=True)) / (2 * w)
        for k in range(2):
            dv = rstd_v * (dvh[k] - vhat[k] * dot)
            dp_ref[2 + k] = (dv * _gelu_grad(p_ref[2 + k].astype(F32))).astype(BF16)

    const = lambda shape: pl.BlockSpec(shape, lambda i: (0,) * len(shape))
    tile4 = pl.BlockSpec((4, ts, w), lambda i: (0, i, 0))
    return pl.pallas_call(
        body, name=name, grid=(s // ts,),
        in_specs=[tile4, pl.BlockSpec((ts, 2 * w), lambda i: (i, 0)), pl.BlockSpec((ts, d), lambda i: (i, 0)),
                  const((2 * w, d)), pl.BlockSpec((ts, 1), lambda i: (i, 0)),
                  const((1, 2 * w)), const((n_heads, CHUNK, CHUNK)), const((n_heads, CHUNK, 1))]
        + ([ANY] if pin is not None else []),
        out_specs=[tile4, const((2 * w, d)), const((1, 2 * w)), const((n_heads, CHUNK, CHUNK)),
                   const((n_heads, CHUNK, 1))],
        out_shape=[jax.ShapeDtypeStruct((4, s, w), BF16), jax.ShapeDtypeStruct((2 * w, d), F32),
                   jax.ShapeDtypeStruct((1, 2 * w), F32),
                   jax.ShapeDtypeStruct((n_heads, CHUNK, CHUNK), F32),
                   jax.ShapeDtypeStruct((n_heads, CHUNK, 1), F32)],
        scratch_shapes=[pltpu.VMEM((ts, 2 * w), F32), pltpu.VMEM((ts, 2 * w), F32)],
        compiler_params=_params("arbitrary"))(
            p, mix, dh_bf, wout, rstd, sgu_norm, w_spatial, b_spatial, *([pin] if pin is not None else []))


def _row_tile(rows):
    if rows <= ROW_TILE:
        return rows
    for t in (512, 384, 352, 256, 128, 64, 32, 16, 8):
        if rows % t == 0:
            return t
    return rows


def adamw(w, g, m, v, *, name):
    shape = w.shape
    cols = shape[-1]
    rows = w.size // cols
    w2, g2, m2, v2 = (a.reshape(rows, cols) for a in (w, g, m, v))
    tr = _row_tile(rows)
    bc1 = 1.0 - ADAM_B1 ** ADAM_STEP
    bc2 = 1.0 - ADAM_B2 ** ADAM_STEP

    def body(w_ref, g_ref, m_ref, v_ref, d_ref, nm_ref, nv_ref):
        grad = g_ref[...]
        m_new = ADAM_B1 * m_ref[...] + (1.0 - ADAM_B1) * grad
        v_new = ADAM_B2 * v_ref[...] + (1.0 - ADAM_B2) * (grad * grad)
        nm_ref[...] = m_new
        nv_ref[...] = v_new
        d_ref[...] = -ADAM_LR * ((m_new / bc1) / (jnp.sqrt(v_new / bc2) + ADAM_EPS) + ADAM_WD * w_ref[...])

    spec = pl.BlockSpec((tr, cols), lambda i: (i, 0))
    outs = pl.pallas_call(
        body, name=name, grid=(rows // tr,),
        in_specs=[spec] * 4, out_specs=[spec] * 3,
        out_shape=[jax.ShapeDtypeStruct((rows, cols), F32)] * 3,
        compiler_params=_params("parallel"))(w2, g2, m2, v2)
    return tuple(o.reshape(shape) for o in outs)


def adamw_layers(w, grads, m, v, *, name):
    n_l, rows, cols = w.shape
    tr = _row_tile(rows)
    bc1 = 1.0 - ADAM_B1 ** ADAM_STEP
    bc2 = 1.0 - ADAM_B2 ** ADAM_STEP
    outs = None
    for l in range(n_l):
        def body(w_ref, g_ref, m_ref, v_ref, *rest):
            go_ref, d_ref, nm_ref, nv_ref = rest[-4:]
            grad = g_ref[...]
            m_new = ADAM_B1 * m_ref[...] + (1.0 - ADAM_B1) * grad
            v_new = ADAM_B2 * v_ref[...] + (1.0 - ADAM_B2) * (grad * grad)
            go_ref[...] = grad
            nm_ref[...] = m_new
            nv_ref[...] = v_new
            d_ref[...] = -ADAM_LR * ((m_new / bc1) / (jnp.sqrt(v_new / bc2) + ADAM_EPS) + ADAM_WD * w_ref[...])

        layer = pl.BlockSpec((None, tr, cols), lambda i, l=l: (l, i, 0))
        prev = list(outs) if outs is not None else []
        outs = pl.pallas_call(
            body, name=f"{name}{l}", grid=(rows // tr,),
            in_specs=[layer, pl.BlockSpec((tr, cols), lambda i: (i, 0)), layer, layer] + [ANY] * len(prev),
            out_specs=[layer] * 4,
            out_shape=[jax.ShapeDtypeStruct(w.shape, F32)] * 4,
            input_output_aliases={4 + k: k for k in range(len(prev))},
            compiler_params=_params("parallel"))(w, grads[l], m, v, *prev)
    return tuple(outs)


def _place():
    return lax.axis_index("x"), lax.axis_index("y"), lax.axis_index("c")


def _other_chips(x, y):
    return [(1 - x, y), (x, 1 - y), (1 - x, 1 - y)]


HBM = pl.BlockSpec(memory_space=pltpu.HBM)
SEM = pl.BlockSpec(memory_space=pltpu.SEMAPHORE)
DATAFLOW = pltpu.SideEffectType.DATAFLOW_SIDE_EFFECTING


def _in_hbm(a):
    return pltpu.with_memory_space_constraint(a, pltpu.HBM)


def cast_into_slot(w, chip, *, l=None, name, pin=None):
    rows, cols = w.shape[-2:]
    tr = _row_tile(rows)

    def body(chip_ref, w_ref, *rest):
        rest[-1][...] = w_ref[...].astype(BF16)

    in_spec = (pl.BlockSpec((tr, cols), lambda i, chip_ref: (i, 0)) if l is None
               else pl.BlockSpec((None, tr, cols), lambda i, chip_ref: (l, i, 0)))
    return pl.pallas_call(
        body, name=name,
        grid_spec=pltpu.PrefetchScalarGridSpec(
            num_scalar_prefetch=1, grid=(rows // tr,), in_specs=[in_spec] + ([ANY] if pin is not None else []),
            out_specs=pl.BlockSpec((None, tr, cols), lambda i, chip_ref: (chip_ref[0], i, 0))),
        out_shape=jax.ShapeDtypeStruct((N_CHIPS, rows, cols), BF16),
        compiler_params=_params("parallel"))(chip, w, *([pin] if pin is not None else []))


def _half(ref, slot, c):
    half = ref.shape[1] // 2
    return ref.at[slot, pl.ds(c * half, half), :]


def gather_start(groups, smalls, *, name):
    flat = [b for g in groups for b in g]
    n_b, n_s, n_g = len(flat), len(smalls), len(groups)
    n_sem = 2 * n_g + (2 if n_s else 0)

    def body(*refs):
        bufs, small_refs = refs[:n_b], refs[n_b:n_b + n_s]
        sems = refs[n_b + n_s:n_b + n_s + n_sem]
        token = refs[-1]
        x, y, c = _place()
        me = 2 * x + y
        chips = _other_chips(x, y)
        for si in range(n_s):
            piece = small_refs[si].at[me]
            for k, (px, py) in enumerate(chips):
                pltpu.make_async_remote_copy(
                    src_ref=piece, dst_ref=piece,
                    send_sem=sems[2 * n_g].at[3 * si + k], recv_sem=sems[2 * n_g + 1].at[3 * si + k],
                    device_id=(px, py, c), device_id_type=MESH).start()
        t = 0
        for gi, group in enumerate(groups):
            for ti in range(len(group)):
                piece = _half(bufs[t], me, c)
                t += 1
                for k, (px, py) in enumerate(chips):
                    pltpu.make_async_remote_copy(
                        src_ref=piece, dst_ref=piece,
                        send_sem=sems[2 * gi].at[3 * ti + k], recv_sem=sems[2 * gi + 1].at[3 * ti + k],
                        device_id=(px, py, c), device_id_type=MESH).start()
        token[...] = jnp.zeros_like(token)

    sem_shapes = []
    for group in groups:
        sem_shapes += [pltpu.SemaphoreType.DMA((3 * len(group),))] * 2
    if n_s:
        sem_shapes += [pltpu.SemaphoreType.DMA((3 * n_s,))] * 2
    arrays = flat + list(smalls)
    res = pl.pallas_call(
        body, name=name,
        out_shape=tuple(sem_shapes) + tuple(pltpu.HBM(a.shape, a.dtype) for a in arrays)
        + (jax.ShapeDtypeStruct((8, 128), F32),),
        in_specs=[HBM] * len(arrays),
        out_specs=tuple([SEM] * n_sem + [HBM] * len(arrays) + [pl.BlockSpec(memory_space=pltpu.VMEM)]),
        input_output_aliases={i: n_sem + i for i in range(len(arrays))},
        compiler_params=pltpu.CompilerParams(has_side_effects=DATAFLOW))(*[_in_hbm(a) for a in arrays])
    sems, thru, token = res[:n_sem], res[n_sem:-1], res[-1]
    out_groups, t = [], 0
    for group in groups:
        out_groups.append(list(thru[t:t + len(group)]))
        t += len(group)
    return sems, out_groups, list(thru[n_b:]), token


def gather_wait(bufs, send, recv, after, *, name, smalls=(), small_send=None, small_recv=None):
    n_b, n_s = len(bufs), len(smalls)
    arrays = list(bufs) + list(smalls)
    sem_ops = [send, recv] + ([small_send, small_recv] if n_s else [])

    def body(*refs):
        buf_refs, small_refs = refs[:n_b], refs[n_b:n_b + n_s]
        sems = refs[n_b + n_s:n_b + n_s + len(sem_ops)]
        x, y, c = _place()
        me = 2 * x + y
        chips = _other_chips(x, y)
        for ti in range(n_b):
            for k, (px, py) in enumerate(chips):
                done = pltpu.make_async_remote_copy(
                    src_ref=_half(buf_refs[ti], me, c), dst_ref=_half(buf_refs[ti], 2 * px + py, c),
                    send_sem=sems[0].at[3 * ti + k], recv_sem=sems[1].at[3 * ti + k],
                    device_id=(px, py, c), device_id_type=MESH)
                done.wait_send()
                done.wait_recv()
        for si in range(n_s):
            for k, (px, py) in enumerate(chips):
                done = pltpu.make_async_remote_copy(
                    src_ref=small_refs[si].at[me], dst_ref=small_refs[si].at[2 * px + py],
                    send_sem=sems[2].at[3 * si + k], recv_sem=sems[3].at[3 * si + k],
                    device_id=(px, py, c), device_id_type=MESH)
                done.wait_send()
                done.wait_recv()

    res = pl.pallas_call(
        body, name=name,
        out_shape=tuple(pltpu.HBM(a.shape, a.dtype) for a in arrays),
        in_specs=[HBM] * len(arrays) + [SEM] * len(sem_ops) + [ANY],
        out_specs=tuple([HBM] * len(arrays)),
        input_output_aliases={i: i for i in range(len(arrays))},
        compiler_params=pltpu.CompilerParams(has_side_effects=DATAFLOW))(*arrays, *sem_ops, after)
    return list(res[:n_b]), list(res[n_b:])


def gather_forward(bufs, *, name):
    n = len(bufs)

    def body(*refs):
        ins, outs = refs[:n], refs[n:2 * n]
        send_sems, recv_sems = refs[2 * n:]
        x, y, c = _place()
        chips = _other_chips(x, y)
        for t in range(n):
            for k, (px, py) in enumerate(chips):
                pltpu.make_async_remote_copy(
                    src_ref=_half(ins[t], 2 * px + py, c), dst_ref=_half(outs[t], 2 * px + py, c),
                    send_sem=send_sems.at[3 * t + k], recv_sem=recv_sems.at[3 * t + k],
                    device_id=(x, y, 1 - c), device_id_type=MESH).start()
        for t in range(n):
            for k, (px, py) in enumerate(chips):
                done = pltpu.make_async_remote_copy(
                    src_ref=_half(ins[t], 2 * px + py, c), dst_ref=_half(outs[t], 2 * px + py, 1 - c),
                    send_sem=send_sems.at[3 * t + k], recv_sem=recv_sems.at[3 * t + k],
                    device_id=(x, y, 1 - c), device_id_type=MESH)
                done.wait_send()
                done.wait_recv()

    return pl.pallas_call(
        body, name=name, in_specs=[ANY] * n, out_specs=[ANY] * n,
        out_shape=[jax.ShapeDtypeStruct(a.shape, a.dtype) for a in bufs],
        input_output_aliases={i: i for i in range(n)},
        scratch_shapes=[pltpu.SemaphoreType.DMA((3 * n,)), pltpu.SemaphoreType.DMA((3 * n,))],
        compiler_params=pltpu.CompilerParams(has_side_effects=True))(*bufs)


def sum_stage_a(grad, recv, place, wire, *, name):
    j_n, half, cols = recv.shape

    def body(place_ref, g_ref, r_ref, o_ref, ob_ref):
        acc = g_ref[...] + r_ref[...]
        ob_ref[...] = acc.astype(wire)

        @pl.when(pl.program_id(0) == place_ref[0])
        def _():
            o_ref[...] = acc

    blk = (None, half, cols)
    return pl.pallas_call(
        body, name=name,
        grid_spec=pltpu.PrefetchScalarGridSpec(
            num_scalar_prefetch=1, grid=(j_n,),
            in_specs=[pl.BlockSpec(blk, lambda j, place_ref: (j, place_ref[1], 0)),
                      pl.BlockSpec(blk, lambda j, place_ref: (j, 0, 0))],
            out_specs=[pl.BlockSpec((half, cols), lambda j, place_ref: (0, 0)),
                       pl.BlockSpec(blk, lambda j, place_ref: (j, 0, 0))]),
        out_shape=[jax.ShapeDtypeStruct((half, cols), F32), jax.ShapeDtypeStruct(recv.shape, wire)],
        compiler_params=_params("arbitrary"))(place, grad, recv)


def _stage_a_copies(srcs, lands, x, y, c):
    out = []
    for src, land in zip(srcs, lands):
        half = src.shape[1] // 2
        out.append((src.at[:, pl.ds((1 - c) * half, half), :], land, (x, y, 1 - c)))
    return out


def _stage_b_copies(srcs, lands, x, y, c):
    out = []
    for src, land in zip(srcs, lands):
        for k, (px, py) in enumerate(_other_chips(x, y)):
            out.append((src.at[2 * px + py], land.at[k], (px, py, c)))
    return out


def _forward_copies(bufs, _, x, y, c):
    out = []
    for buf in bufs:
        for px, py in _other_chips(x, y):
            out.append((_half(buf, 2 * px + py, c), _half(buf, 2 * px + py, c), (x, y, 1 - c)))
    return out


def _stage_c_copies(fulls, _, x, y, c):
    out = []
    for full in fulls:
        half = full.shape[0] // 2
        mine = full.at[pl.ds(c * half, half), :]
        out.append((mine, mine, (x, y, 1 - c)))
    return out


def split_start(srcs, lands, copies, *, name):
    n, n_all = len(srcs), len(srcs) + len(lands)
    n_c = len(copies(srcs, lands, 0, 0, 0))

    def body(*refs):
        src_refs, land_refs = refs[:n], refs[n:n_all]
        send_sems, recv_sems = refs[n_all], refs[n_all + 1]
        token = refs[-1]
        x, y, c = _place()
        for k, (src, dst, target) in enumerate(copies(src_refs, land_refs, x, y, c)):
            pltpu.make_async_remote_copy(src_ref=src, dst_ref=dst, send_sem=send_sems.at[k], recv_sem=recv_sems.at[k],
                                         device_id=target, device_id_type=MESH).start()
        token[...] = jnp.zeros_like(token)

    arrays = list(srcs) + list(lands)
    res = pl.pallas_call(
        body, name=name,
        out_shape=(pltpu.SemaphoreType.DMA((n_c,)), pltpu.SemaphoreType.DMA((n_c,)))
        + tuple(pltpu.HBM(a.shape, a.dtype) for a in arrays) + (jax.ShapeDtypeStruct((8, 128), F32),),
        in_specs=[HBM] * n_all,
        out_specs=tuple([SEM, SEM] + [HBM] * n_all + [pl.BlockSpec(memory_space=pltpu.VMEM)]),
        input_output_aliases={i: 2 + i for i in range(n_all)},
        compiler_params=pltpu.CompilerParams(has_side_effects=DATAFLOW))(*[_in_hbm(a) for a in arrays])
    return res[0], res[1], list(res[2:2 + n]), list(res[2 + n:2 + n_all]), res[-1]


def split_wait(srcs, lands, send, recv, copies, after, *, name):
    n, n_all = len(srcs), len(srcs) + len(lands)

    def body(*refs):
        src_refs, land_refs = refs[:n], refs[n:n_all]
        send_sems, recv_sems = refs[n_all], refs[n_all + 1]
        x, y, c = _place()
        for k, (src, dst, target) in enumerate(copies(src_refs, land_refs, x, y, c)):
            done = pltpu.make_async_remote_copy(src_ref=src, dst_ref=dst, send_sem=send_sems.at[k],
                                                recv_sem=recv_sems.at[k], device_id=target, device_id_type=MESH)
            done.wait_send()
            done.wait_recv()

    arrays = list(srcs) + list(lands)
    res = pl.pallas_call(
        body, name=name,
        out_shape=tuple(pltpu.HBM(a.shape, a.dtype) for a in arrays),
        in_specs=[HBM] * n_all + [SEM, SEM, ANY],
        out_specs=tuple([HBM] * n_all),
        input_output_aliases={i: i for i in range(n_all)},
        compiler_params=pltpu.CompilerParams(has_side_effects=DATAFLOW))(*arrays, send, recv, after)
    return list(res[:n]), list(res[n:])


def sum_stage_b(part, recv, place, *, name):
    half, cols = part.shape

    def body(place_ref, p_ref, r_ref, o_ref):
        acc = p_ref[...]
        for k in range(3):
            acc = acc + r_ref[k].astype(F32)
        o_ref[...] = acc

    return pl.pallas_call(
        body, name=name,
        grid_spec=pltpu.PrefetchScalarGridSpec(
            num_scalar_prefetch=1, grid=(1,),
            in_specs=[pl.BlockSpec((half, cols), lambda i, place_ref: (0, 0)),
                      pl.BlockSpec((3, half, cols), lambda i, place_ref: (0, 0, 0))],
            out_specs=pl.BlockSpec((half, cols), lambda i, place_ref: (place_ref[1], 0))),
        out_shape=jax.ShapeDtypeStruct((2 * half, cols), F32),
        compiler_params=_params("arbitrary"))(place, part, recv)


def gather_chip_blocks(slots, *, name):
    def body(in_ref, out_ref, send_sems, recv_sems):
        x, y, c = _place()
        me = 2 * x + y
        chips = _other_chips(x, y)
        for k, (px, py) in enumerate(chips):
            pltpu.make_async_remote_copy(
                src_ref=in_ref.at[me], dst_ref=out_ref.at[me],
                send_sem=send_sems.at[k], recv_sem=recv_sems.at[k],
                device_id=(px, py, c), device_id_type=MESH).start()
        for k, (px, py) in enumerate(chips):
            done = pltpu.make_async_remote_copy(
                src_ref=in_ref.at[me], dst_ref=out_ref.at[2 * px + py],
                send_sem=send_sems.at[k], recv_sem=recv_sems.at[k],
                device_id=(px, py, c), device_id_type=MESH)
            done.wait_send()
            done.wait_recv()

    return pl.pallas_call(
        body, name=name, in_specs=[ANY], out_specs=ANY,
        out_shape=jax.ShapeDtypeStruct(slots.shape, slots.dtype),
        input_output_aliases={0: 0},
        scratch_shapes=[pltpu.SemaphoreType.DMA((3,)), pltpu.SemaphoreType.DMA((3,))],
        compiler_params=pltpu.CompilerParams(has_side_effects=True))(slots)


def _ffn_bwd(dh, dh_bf, h, gain, saved, wg, wu, wd, cw, cb, place, l, pin):
    hn, rstd, g, up, gc = saved
    dg, dup, dwg, dwu, dwd, dcw, dcb = ffn_bwd_a(dh_bf, hn, g, up, gc, wd, cw, name=f"ffn{l}_bwd_a", pin=pin)
    red = _reduce_a_begin([(dwg, BF16), (dwu, BF16), (dwd, BF16)], tag=f"f{l}")
    dh_in, dh_in_bf, dgain = dx_rms_bwd([(dg, wg), (dup, wu)], h, gain, rstd, dh, name=f"ffn{l}_bwd_b",
                                        pin=red[-1])
    red = _reduce_b_begin(red, place, dh_in_bf, tag=f"f{l}")
    return dh_in, dh_in_bf, (dcw, dcb, dgain), red


def _reduce_a_begin(grads, *, tag):
    lands = [lax.empty((g.shape[0], g.shape[1] // 2, g.shape[2]), F32) for g, _ in grads]
    send, recv, srcs, lands, token = split_start([g for g, _ in grads], lands, _stage_a_copies,
                                                 name=f"reduce_a_start_{tag}")
    return [w for _, w in grads], send, recv, srcs, lands, token


def _reduce_b_begin(state, place, after, *, tag):
    wires, send, recv, srcs, lands, _ = state
    grads, recv_a = split_wait(srcs, lands, send, recv, _stage_a_copies, after, name=f"reduce_a_wait_{tag}")
    parts = [sum_stage_a(g, r, place, w, name=f"sum_a_{tag}{i}") for i, (g, r, w) in enumerate(zip(grads, recv_a, wires))]
    lands_b = [lax.empty((3,) + p[1].shape[1:], p[1].dtype) for p in parts]
    send, recv, srcs, lands, token = split_start([p[1] for p in parts], lands_b, _stage_b_copies,
                                                 name=f"reduce_b_start_{tag}")
    return [p[0] for p in parts], send, recv, srcs, lands, token


def _reduce_c_begin(state, place, after, *, tag):
    parts, send, recv, srcs, lands, _ = state
    _, recv_b = split_wait(srcs, lands, send, recv, _stage_b_copies, after, name=f"reduce_b_wait_{tag}")
    halves = [sum_stage_b(p, r, place, name=f"sum_b_{tag}{i}") for i, (p, r) in enumerate(zip(parts, recv_b))]
    send, recv, fulls, _, token = split_start(halves, [], _stage_c_copies, name=f"reduce_c_start_{tag}")
    return send, recv, fulls, token


def _reduce_finish(state, after, *, tag):
    send, recv, fulls, _ = state
    fulls, _ = split_wait(fulls, [], send, recv, _stage_c_copies, after, name=f"reduce_c_wait_{tag}")
    return fulls


def kernel(x, norm_mix, norm_ffn, final_norm, w_in_even, conv_a, w_pool, pool_scale, w_out_even, w_in_odd, sgu_norm, w_spatial, b_spatial, w_out_odd, w_ffn_gate, w_ffn_up, conv_ffn, b_conv_ffn, w_ffn_down, loss_target, m_norm_mix, m_norm_ffn, m_final_norm, m_w_in_even, m_conv_a, m_w_pool, m_pool_scale, m_w_out_even, m_w_in_odd, m_sgu_norm, m_w_spatial, m_b_spatial, m_w_out_odd, m_w_ffn_gate, m_w_ffn_up, m_conv_ffn, m_b_conv_ffn, m_w_ffn_down, v_norm_mix, v_norm_ffn, v_final_norm, v_w_in_even, v_conv_a, v_w_pool, v_pool_scale, v_w_out_even, v_w_in_odd, v_sgu_norm, v_w_spatial, v_b_spatial, v_w_out_odd, v_w_ffn_gate, v_w_ffn_up, v_conv_ffn, v_b_conv_ffn, v_w_ffn_down):
    weights = dict(norm_mix=norm_mix, norm_ffn=norm_ffn, final_norm=final_norm, w_in_even=w_in_even,
                   conv_a=conv_a, w_pool=w_pool, pool_scale=pool_scale, w_out_even=w_out_even,
                   w_in_odd=w_in_odd, sgu_norm=sgu_norm, w_spatial=w_spatial, b_spatial=b_spatial,
                   w_out_odd=w_out_odd, w_ffn_gate=w_ffn_gate, w_ffn_up=w_ffn_up, conv_ffn=conv_ffn,
                   b_conv_ffn=b_conv_ffn, w_ffn_down=w_ffn_down)
    m_in = dict(norm_mix=m_norm_mix, norm_ffn=m_norm_ffn, final_norm=m_final_norm, w_in_even=m_w_in_even,
                conv_a=m_conv_a, w_pool=m_w_pool, pool_scale=m_pool_scale, w_out_even=m_w_out_even,
                w_in_odd=m_w_in_odd, sgu_norm=m_sgu_norm, w_spatial=m_w_spatial, b_spatial=m_b_spatial,
                w_out_odd=m_w_out_odd, w_ffn_gate=m_w_ffn_gate, w_ffn_up=m_w_ffn_up, conv_ffn=m_conv_ffn,
                b_conv_ffn=m_b_conv_ffn, w_ffn_down=m_w_ffn_down)
    v_in = dict(norm_mix=v_norm_mix, norm_ffn=v_norm_ffn, final_norm=v_final_norm, w_in_even=v_w_in_even,
                conv_a=v_conv_a, w_pool=v_w_pool, pool_scale=v_pool_scale, w_out_even=v_w_out_even,
                w_in_odd=v_w_in_odd, sgu_norm=v_sgu_norm, w_spatial=v_w_spatial, b_spatial=v_b_spatial,
                w_out_odd=v_w_out_odd, w_ffn_gate=v_w_ffn_gate, w_ffn_up=v_w_ffn_up, conv_ffn=v_conv_ffn,
                b_conv_ffn=v_b_conv_ffn, w_ffn_down=v_w_ffn_down)
    order = list(weights)

    chip = 2 * lax.axis_index("x") + lax.axis_index("y")
    core = lax.axis_index("c")
    place = jnp.stack([chip, core]).astype(jnp.int32)
    chip_arr = place[:1]

    h0 = x[0]
    target = loss_target[0]
    d_model = h0.shape[1]
    f_shard = w_ffn_gate.shape[-1]

    def turned(a):
        return jnp.transpose(a, (0, 2, 1))

    def own_slot(v):
        return lax.dynamic_update_index_in_dim(jnp.zeros((N_CHIPS,) + v.shape, v.dtype), v, chip, 0)

    first = [[cast_into_slot(w_in_even[0], chip_arr, name="cast_win_e"),
              cast_into_slot(w_out_even[0], chip_arr, name="cast_wout_e")]]
    smalls = [own_slot(conv_a[0]), own_slot(sgu_norm), own_slot(conv_ffn[0]), own_slot(conv_ffn[1])]
    sems0, first, smalls, token0 = gather_start(first, smalls, name="gather_start0")
    rest = [
        [cast_into_slot(turned(w_ffn_gate), chip_arr, l=0, name="cast_wg0", pin=token0),
         cast_into_slot(turned(w_ffn_up), chip_arr, l=0, name="cast_wu0", pin=token0)],
        [cast_into_slot(w_ffn_down, chip_arr, l=0, name="cast_wd0", pin=token0)],
        [cast_into_slot(w_in_odd[0], chip_arr, name="cast_win_o", pin=token0),
         cast_into_slot(w_out_odd[0], chip_arr, name="cast_wout_o", pin=token0)],
        [cast_into_slot(turned(w_ffn_gate), chip_arr, l=1, name="cast_wg1", pin=token0),
         cast_into_slot(turned(w_ffn_up), chip_arr, l=1, name="cast_wu1", pin=token0)],
        [cast_into_slot(w_ffn_down, chip_arr, l=1, name="cast_wd1", pin=token0)]]
    sems1, rest, _, token = gather_start(rest, [], name="gather_start1")
    groups = first + rest
    sems = list(sems0[:2]) + list(sems1) + list(sems0[2:])

    def arrive(gi, after, with_smalls=False):
        kw = dict(smalls=smalls, small_send=sems[-2], small_recv=sems[-1]) if with_smalls else {}
        bufs, small_out = gather_wait(groups[gi], sems[2 * gi], sems[2 * gi + 1], after, name=f"gather_wait{gi}", **kw)
        return gather_forward(bufs, name=f"gather_forward{gi}"), small_out

    def arrive_begin(gi, after):
        bufs, _ = gather_wait(groups[gi], sems[2 * gi], sems[2 * gi + 1], after, name=f"gather_wait{gi}")
        send, recv, bufs, _, tok = split_start(bufs, [], _forward_copies, name=f"gather_forward_start{gi}")
        return (send, recv, bufs), tok

    def arrive_end(state, after, gi):
        send, recv, bufs = state
        return split_wait(bufs, [], send, recv, _forward_copies, after, name=f"gather_forward_wait{gi}")[0]

    cb = b_conv_ffn.reshape(-1, N_CHIPS, 1, f_shard)
    wp_bf = w_pool[0].astype(BF16)
    wp_t_bf = jnp.transpose(w_pool[0], (0, 2, 1)).astype(BF16)
    ws = w_spatial[0]
    bs = b_spatial[0][:, :, None]

    (win_e, wout_e), (ca_g, sn_g, cw0, cw1) = arrive(0, token, with_smalls=True)
    wout_e = wout_e.reshape(-1, d_model)
    ca_full = jnp.transpose(ca_g, (1, 0, 2)).reshape(ca_g.shape[1], -1)
    sn_full = sn_g.reshape(1, -1)
    h1, xn0, rstd0, proj0, mix0, hn0, rstdf0 = even_layer_fwd(
        h0, norm_mix[0:1], win_e, ca_full, wp_bf, pool_scale, wout_e, norm_ffn[0:1], name="l0_fwd")
    (wg0, wu0), _ = arrive(1, hn0)
    g0, up0, gc0, act0 = ffn_in_fwd(hn0, wg0, wu0, cw0, cb[0], name="ffn0_in")
    swap2, tok2 = arrive_begin(2, act0)
    swap3, tok3 = arrive_begin(3, tok2)
    (wd0,) = arrive_end(swap2, tok3, 2)
    h2 = mm_acc(act0, wd0, h1, name="ffn0_down")
    ffn0 = (hn0, rstdf0, g0, up0, gc0)
    swap4, tok4 = arrive_begin(4, h2)
    swap5, tok5 = arrive_begin(5, tok4)
    win_o, wout_o = arrive_end(swap3, tok5, 3)
    wout_o = wout_o.reshape(-1, d_model)
    h3, xn1, rstd1, p1, mix1, rstd_v = odd_layer_fwd(h2, norm_mix[1:2], win_o, sn_full, ws, bs, wout_o, name="l1_fwd")
    wg1, wu1 = arrive_end(swap4, h3, 4)
    (wd1,) = arrive_end(swap5, wg1, 5)
    dh4, dh4_bf, loss_row, d_final, hn1, rstdf1, g1, up1, gc1 = ffn_loss_fwd(
        h3, norm_ffn[1:2], wg1, wu1, wd1, cw1, cb[1], target, final_norm[None], name="ffn1_fwd_loss")
    ffn1 = (hn1, rstdf1, g1, up1, gc1)

    loss = lax.psum(loss_row[0, 0], ("x", "y", "c"))

    dh3, dh3_bf, (dcw1, dcb1, dnf1), red3 = _ffn_bwd(
        dh4, dh4_bf, h3, norm_ffn[1:2], ffn1, wg1, wu1, wd1, cw1, cb[1], place, 1, None)

    def as_blocks(a):
        return a.reshape(N_CHIPS, -1, d_model)

    dp1, dwout_o, dsn, dws, dbs = sgu_bwd(p1, mix1, dh3_bf, wout_o, rstd_v, sn_full, ws, bs, name="l1_mix_bwd",
                                          pin=red3[-1])
    dwout_o = as_blocks(dwout_o)
    dwin_o = mm_tn_shared(xn1, dp1, name="l1_dwin")
    red2 = _reduce_a_begin([(dwin_o, BF16), (dwout_o, BF16)], tag="m1")
    dh2, dh2_bf, dnm1 = dx_rms_bwd([(dp1, win_o)], h2, norm_mix[1:2], rstd1, dh3, name="l1_dx", pin=red2[-1])
    red2 = _reduce_b_begin(red2, place, dh2_bf, tag="m1")

    dh1, dh1_bf, (dcw0, dcb0, dnf0), red1 = _ffn_bwd(
        dh2, dh2_bf, h1, norm_ffn[0:1], ffn0, wg0, wu0, wd0, cw0, cb[0], place, 0, red2[-1])

    dproj0, dwout_e, dca, dwp, dps = even_bwd(proj0, mix0, dh1_bf, wout_e, ca_full, wp_bf, wp_t_bf, pool_scale,
                                              name="l0_mix_bwd", pin=red1[-1])
    dwout_e = as_blocks(dwout_e)
    dwin_e = mm_tn_shared(xn0, dproj0, name="l0_dwin")
    dh0, _, dnm0 = dx_rms_bwd([(dproj0, win_e)], h0, norm_mix[0:1], rstd0, dh1, name="l0_dx")
    grad_x = dh0[None]

    small_parts = dict(
        norm_mix=jnp.concatenate([dnm0, dnm1]), norm_ffn=jnp.concatenate([dnf0, dnf1]), final_norm=d_final,
        conv_a=dca, w_pool=dwp, pool_scale=dps, sgu_norm=dsn, w_spatial=dws, b_spatial=dbs,
        conv_ffn=jnp.stack([dcw0, dcw1]), b_conv_ffn=jnp.stack([dcb0, dcb1]))
    flat = jnp.concatenate([v.reshape(-1) for v in small_parts.values()])
    pad = (-flat.shape[0]) % (N_CHIPS * 32 * 128)
    small = jnp.pad(flat, (0, pad)).reshape(N_CHIPS, -1, 128)
    red0 = _reduce_a_begin([(dwin_e, BF16), (dwout_e, BF16), (small, F32)], tag="m0")

    swap_f1 = _reduce_c_begin(red3, place, red0[-1], tag="f1")
    red0 = _reduce_b_begin(red0, place, swap_f1[-1], tag="m0")
    swap_m1 = _reduce_c_begin(red2, place, red0[-1], tag="m1")
    swap_f0 = _reduce_c_begin(red1, place, swap_m1[-1], tag="f0")
    full3 = _reduce_finish(swap_f1, swap_f0[-1], tag="f1")
    full2 = _reduce_finish(swap_m1, full3[0], tag="m1")
    full1 = _reduce_finish(swap_f0, full2[0], tag="f0")
    swap_m0 = _reduce_c_begin(red0, place, full1[0], tag="m0")
    full0 = _reduce_finish(swap_m0, swap_m0[-1], tag="m0")
    small_slots = lax.dynamic_update_index_in_dim(jnp.zeros(small.shape, F32), full0[2], chip, 0)
    small_sum = gather_chip_blocks(small_slots, name="gather_small").reshape(-1)
    grads = {
        "w_in_even": full0[0][None], "w_out_even": full0[1][None],
        "w_in_odd": full2[0][None], "w_out_odd": full2[1][None],
        }
    layered = {"w_ffn_gate": [full1[0], full3[0]], "w_ffn_up": [full1[1], full3[1]],
               "w_ffn_down": [full1[2], full3[2]]}
    off = 0
    small_red = {}
    for nm, v in small_parts.items():
        small_red[nm] = small_sum[off:off + v.size].reshape(v.shape)
        off += v.size
    for nm in ("norm_mix", "norm_ffn", "pool_scale"):
        grads[nm] = small_red[nm].reshape(weights[nm].shape)
    grads["final_norm"] = small_red["final_norm"].reshape(weights["final_norm"].shape)
    grads["w_pool"] = small_red["w_pool"][None]
    grads["w_spatial"] = small_red["w_spatial"][None]
    grads["b_spatial"] = small_red["b_spatial"].reshape(weights["b_spatial"].shape)
    grads["b_conv_ffn"] = small_red["b_conv_ffn"].reshape(weights["b_conv_ffn"].shape)
    grads["conv_a"] = lax.dynamic_slice_in_dim(small_red["conv_a"], chip * conv_a.shape[-1], conv_a.shape[-1], 1)[None]
    grads["sgu_norm"] = lax.dynamic_slice_in_dim(small_red["sgu_norm"], chip * sgu_norm.shape[-1], sgu_norm.shape[-1], 1)
    grads["conv_ffn"] = lax.dynamic_index_in_dim(small_red["conv_ffn"], chip, 1, keepdims=False)

    deltas, new_m, new_v = {}, {}, {}
    for nm, per_layer in layered.items():
        if nm == "w_ffn_down":
            grads[nm], deltas[nm], new_m[nm], new_v[nm] = adamw_layers(
                weights[nm], per_layer, m_in[nm], v_in[nm], name=f"adamw_{nm}")
        else:
            outs = adamw_layers(turned(weights[nm]), per_layer, turned(m_in[nm]), turned(v_in[nm]),
                                name=f"adamw_{nm}")
            grads[nm], deltas[nm], new_m[nm], new_v[nm] = (turned(o) for o in outs)
    for nm in order:
        if nm in layered:
            continue
        w = weights[nm]
        w2 = w[None] if w.ndim == 1 else w
        shp = w2.shape
        d, nm_, nv_ = adamw(w2, grads[nm].reshape(shp), m_in[nm].reshape(shp), v_in[nm].reshape(shp),
                            name=f"adamw_{nm}")
        deltas[nm], new_m[nm], new_v[nm] = d.reshape(w.shape), nm_.reshape(w.shape), nv_.reshape(w.shape)

    return (loss, grad_x, *[grads[n] for n in order], *[deltas[n] for n in order],
            *[new_m[n] for n in order], *[new_v[n] for n in order])
```

```python
import jax
import jax.numpy as jnp
from jax import lax
from jax.experimental import pallas as pl
from jax.experimental.pallas import tpu as pltpu

F32 = jnp.float32
BF16 = jnp.bfloat16
MESH = pl.DeviceIdType.MESH
ANY = pl.BlockSpec(memory_space=pl.ANY)

EPS = 1e-6
POOL_WINDOWS = (2, 4, 8, 16)
GROUP = 128
CHUNK = 128
N_CHIPS = 4
N_DEV = 8
ROW_TILE = 512
HALO = 16
VMEM_LIMIT = 56 * 1024 * 1024

ADAM_LR = 0.001
ADAM_B1 = 0.9
ADAM_B2 = 0.999
ADAM_EPS = 1e-08
ADAM_WD = 0.01
ADAM_STEP = 10


def _params(*sem):
    return pltpu.CompilerParams(dimension_semantics=sem, vmem_limit_bytes=VMEM_LIMIT)


def mm_acc(a, b, res, *, name):
    j_n, s, kj = a.shape
    n = b.shape[-1]
    tm = min(ROW_TILE, s)

    def body(a_ref, b_ref, r_ref, o_ref):
        acc = r_ref[...]
        for j in range(j_n):
            acc = acc + jnp.dot(a_ref[j], b_ref[j], preferred_element_type=F32)
        o_ref[...] = acc

    return pl.pallas_call(
        body, name=name, grid=(s // tm,),
        in_specs=[pl.BlockSpec((j_n, tm, kj), lambda i: (0, i, 0)),
                  pl.BlockSpec((j_n, kj, n), lambda i: (0, 0, 0)),
                  pl.BlockSpec((tm, n), lambda i: (i, 0))],
        out_specs=pl.BlockSpec((tm, n), lambda i: (i, 0)),
        out_shape=jax.ShapeDtypeStruct((s, n), F32),
        compiler_params=_params("parallel"))(a, b, res)


_NT = (((1,), (1,)), ((), ()))
_TN = (((0,), (0,)), ((), ()))


def _back(x, k):
    return pltpu.roll(x, k, 0)


def _fwd(x, k):
    return pltpu.roll(x, x.shape[0] - k, 0)


def _causal_conv(x, w_ref):
    return w_ref[0:1, :] * _back(x, 2) + w_ref[1:2, :] * _back(x, 1) + w_ref[2:3, :] * x


def _causal_conv_t(dy, w_ref):
    return w_ref[2:3, :] * dy + w_ref[1:2, :] * _fwd(dy, 1) + w_ref[0:1, :] * _fwd(dy, 2)


def _gelu(x):
    return 0.5 * x * (1.0 + lax.erf(x * 0.7071067811865476))


def _gelu_grad(x):
    return 0.5 * (1.0 + lax.erf(x * 0.7071067811865476)) + x * jnp.exp(-0.5 * x * x) * 0.3989422804014327


def _colsum(x):
    return jnp.sum(x, axis=0, keepdims=True)


def _halo_specs(n_lead, ts, width, n_tiles):
    hb = ts // HALO
    prev = pl.BlockSpec((n_lead, HALO, width), lambda i: (0, jnp.maximum(i * hb - 1, 0), 0))
    nxt = pl.BlockSpec((n_lead, HALO, width), lambda i: (0, jnp.minimum((i + 1) * hb, n_tiles * hb - 1), 0))
    return prev, nxt


def _pool_fwd(z_ext, g, pos):
    w = POOL_WINDOWS[g]
    zg = z_ext[:, g * GROUP:(g + 1) * GROUP]
    acc = zg
    sh = 1
    while sh < w:
        acc = acc + _back(acc, sh)
        sh *= 2
    return acc[HALO:] / jnp.minimum(pos, float(w)) - zg[HALO:]


def even_layer_fwd(h, gain, win, conv_a, w_pool, pool_scale, wout, gain_next, *, name, pin=None):
    s, d = h.shape
    w = win.shape[-1]
    ts = min(ROW_TILE, s)
    hb = ts // HALO

    def body(h_ref, hp_ref, gain_ref, win_ref, ca_ref, wp_ref, ps_ref, wout_ref, gn_ref, *rest):
        o_ref, xn_ref, r_ref, p_ref, m_ref, hn_ref, rn_ref = rest[-7:]
        i = pl.program_id(0)
        keep = jnp.where(i > 0, 1.0, 0.0)
        h_ext = jnp.concatenate([hp_ref[...], h_ref[...]], axis=0)
        rstd = lax.rsqrt(jnp.mean(h_ext * h_ext, axis=-1, keepdims=True) + EPS)
        xn_ext = (h_ext * rstd * gain_ref[...]).astype(BF16)
        xn_ref[...] = xn_ext[HALO:]
        r_ref[...] = rstd[HALO:]
        p32 = []
        for k in range(4):
            pk = jnp.dot(xn_ext, win_ref[k], preferred_element_type=F32).astype(BF16)
            p_ref[k] = pk[HALO:]
            pk = pk.astype(F32)
            p32.append(jnp.concatenate([pk[:HALO] * keep, pk[HALO:]], axis=0))
        m_ref[:, 0:w] = (p32[0][HALO:] * _causal_conv(p32[1] * p32[2], ca_ref)[HALO:]).astype(BF16)
        pos = (i * ts + lax.broadcasted_iota(jnp.int32, (ts, 1), 0) + 1).astype(F32)
        for g in range(len(POOL_WINDOWS)):
            pooled = _pool_fwd(p32[3], g, pos)
            mixed = jnp.dot(pooled.astype(BF16), wp_ref[g], preferred_element_type=F32)
            cols = slice(g * GROUP, (g + 1) * GROUP)
            m_ref[:, w + g * GROUP:w + (g + 1) * GROUP] = (mixed * ps_ref[:, cols]).astype(BF16)
        out = h_ref[...] + jnp.dot(m_ref[...], wout_ref[...], preferred_element_type=F32)
        o_ref[...] = out
        rstd_n = lax.rsqrt(jnp.mean(out * out, axis=-1, keepdims=True) + EPS)
        hn_ref[...] = (out * rstd_n * gn_ref[...]).astype(BF16)
        rn_ref[...] = rstd_n

    const = lambda shape: pl.BlockSpec(shape, lambda i: (0,) * len(shape))
    row = pl.BlockSpec((ts, d), lambda i: (i, 0))
    col1 = pl.BlockSpec((ts, 1), lambda i: (i, 0))
    return pl.pallas_call(
        body, name=name, grid=(s // ts,),
        in_specs=[row, pl.BlockSpec((HALO, d), lambda i: (jnp.maximum(i * hb - 1, 0), 0)), const((1, d)),
                  const((4, d, w)), const((3, w)), const((4, GROUP, GROUP)), const((1, w)), const((2 * w, d)),
                  const((1, d))] + ([ANY] if pin is not None else []),
        out_specs=[row, row, col1, pl.BlockSpec((4, ts, w), lambda i: (0, i, 0)),
                   pl.BlockSpec((ts, 2 * w), lambda i: (i, 0)), row, col1],
        out_shape=[jax.ShapeDtypeStruct((s, d), F32), jax.ShapeDtypeStruct((s, d), BF16),
                   jax.ShapeDtypeStruct((s, 1), F32), jax.ShapeDtypeStruct((4, s, w), BF16),
                   jax.ShapeDtypeStruct((s, 2 * w), BF16), jax.ShapeDtypeStruct((s, d), BF16),
                   jax.ShapeDtypeStruct((s, 1), F32)],
        compiler_params=_params("parallel"))(
            h, h, gain, win, conv_a, w_pool, pool_scale, wout, gain_next, *([pin] if pin is not None else []))


def even_bwd(proj, mix, xn, dh_bf, wout, conv_a, w_pool, w_pool_t, pool_scale, *, name, pin=None):
    _, s, w = proj.shape
    d = dh_bf.shape[1]
    ts = min(ROW_TILE, s)
    n_t = s // ts
    prev, nxt = _halo_specs(4, ts, w, n_t)
    hb = ts // HALO
    n_ext = ts + HALO

    def body(p_ref, pp_ref, pn_ref, m_ref, x_ref, dh_ref, dhn_ref, wout_ref, ca_ref, wp_ref, wpt_ref, ps_ref, *rest):
        dp_ref, dwi_ref, dwo_ref, dca_ref, dwp_ref, dps_ref = rest[-6:]
        i = pl.program_id(0)

        @pl.when(i == 0)
        def _():
            dwi_ref[...] = jnp.zeros_like(dwi_ref)
            dwo_ref[...] = jnp.zeros_like(dwo_ref)
            dca_ref[...] = jnp.zeros_like(dca_ref)
            dwp_ref[...] = jnp.zeros_like(dwp_ref)
            dps_ref[...] = jnp.zeros_like(dps_ref)

        keep_p = jnp.where(i > 0, 1.0, 0.0)
        keep_n = jnp.where(i < n_t - 1, 1.0, 0.0)
        dwo_ref[...] += lax.dot_general(m_ref[...], dh_ref[...], _TN, preferred_element_type=F32)
        dmix = lax.dot_general(jnp.concatenate([dh_ref[...], dhn_ref[...]], axis=0), wout_ref[...], _NT,
                               preferred_element_type=F32)
        a_b, a_c, a_v = (p_ref[k].astype(F32) for k in range(3))
        cv_ext = jnp.concatenate([pp_ref[1].astype(F32) * pp_ref[2].astype(F32) * keep_p, a_c * a_v], axis=0)
        dy_a = dmix[:ts, 0:w]
        dp_ref[0] = (dy_a * _causal_conv(cv_ext, ca_ref)[HALO:]).astype(BF16)
        dcc = dy_a * a_b
        dca_ref[2:3, :] += _colsum(dcc * cv_ext[HALO:])
        dca_ref[1:2, :] += _colsum(dcc * _back(cv_ext, 1)[HALO:])
        dca_ref[0:1, :] += _colsum(dcc * _back(cv_ext, 2)[HALO:])
        dcc_ext = jnp.concatenate([dcc, dmix[ts:, 0:w] * pn_ref[0].astype(F32) * keep_n], axis=0)
        dcv = _causal_conv_t(dcc_ext, ca_ref)[:ts]
        dp_ref[1] = (dcv * a_v).astype(BF16)
        dp_ref[2] = (dcv * a_c).astype(BF16)
        z_ext = jnp.concatenate([pp_ref[3].astype(F32) * keep_p, p_ref[3].astype(F32)], axis=0)
        pos = (i * ts + lax.broadcasted_iota(jnp.int32, (ts, 1), 0) + 1).astype(F32)
        pos_ext = (i * ts + lax.broadcasted_iota(jnp.int32, (n_ext, 1), 0) + 1).astype(F32)
        for g, win in enumerate(POOL_WINDOWS):
            cols = slice(g * GROUP, (g + 1) * GROUP)
            ycols = slice(w + g * GROUP, w + (g + 1) * GROUP)
            pooled = _pool_fwd(z_ext, g, pos).astype(BF16)
            mixed = jnp.dot(pooled, wp_ref[g], preferred_element_type=F32)
            dy_b = dmix[:ts, ycols]
            dps_ref[:, cols] += _colsum(dy_b * mixed)
            dmixed_ext = jnp.concatenate([dy_b, dmix[ts:, ycols] * keep_n], axis=0) * ps_ref[:, cols]
            dmixed_ext = dmixed_ext.astype(BF16)
            dwp_ref[g] += lax.dot_general(pooled, dmixed_ext[:ts], _TN, preferred_element_type=F32)
            dpooled = jnp.dot(dmixed_ext, wpt_ref[g], preferred_element_type=F32)
            acc = dpooled / jnp.minimum(pos_ext, float(win))
            sh = 1
            while sh < win:
                acc = acc + _fwd(acc, sh)
                sh *= 2
            dp_ref[3, :, cols] = (acc[:ts] - dpooled[:ts]).astype(BF16)
        for k in range(4):
            dwi_ref[k] += lax.dot_general(x_ref[...], dp_ref[k], _TN, preferred_element_type=F32)

    tile4 = pl.BlockSpec((4, ts, w), lambda i: (0, i, 0))
    const = lambda shape: pl.BlockSpec(shape, lambda i: (0,) * len(shape))
    return pl.pallas_call(
        body, name=name, grid=(n_t,),
        in_specs=[tile4, prev, nxt, pl.BlockSpec((ts, 2 * w), lambda i: (i, 0)), pl.BlockSpec((ts, d), lambda i: (i, 0)),
                  pl.BlockSpec((ts, d), lambda i: (i, 0)),
                  pl.BlockSpec((HALO, d), lambda i: (jnp.minimum((i + 1) * hb, n_t * hb - 1), 0)),
                  const((2 * w, d)), const((3, w)), const((4, GROUP, GROUP)), const((4, GROUP, GROUP)), const((1, w))]
        + ([ANY] if pin is not None else []),
        out_specs=[tile4, const((4, d, w)), const((2 * w, d)), const((3, w)), const((4, GROUP, GROUP)), const((1, w))],
        out_shape=[jax.ShapeDtypeStruct((4, s, w), BF16), jax.ShapeDtypeStruct((4, d, w), F32),
                   jax.ShapeDtypeStruct((2 * w, d), F32), jax.ShapeDtypeStruct((3, w), F32),
                   jax.ShapeDtypeStruct((4, GROUP, GROUP), F32), jax.ShapeDtypeStruct((1, w), F32)],
        compiler_params=_params("arbitrary"))(
            proj, proj, proj, mix, xn, dh_bf, dh_bf, wout, conv_a, w_pool, w_pool_t, pool_scale,
            *([pin] if pin is not None else []))


def _ffn_next_halo(ts, f, n_t):
    hb = ts // HALO
    return pl.BlockSpec((None, HALO, f), lambda j, i: (j, jnp.minimum((i + 1) * hb, n_t * hb - 1), 0))


def ffn_in_fwd(hn, wg, wu, cw, cb, *, name, pin=None):
    s, d = hn.shape
    j_n, f, _ = wg.shape
    tm = min(ROW_TILE, s)
    hb = tm // HALO

    def body(x_ref, xp_ref, wg_ref, wu_ref, cw_ref, cb_ref, *rest):
        g_ref, u_ref, gc_ref, a_ref = rest[-4:]
        i, j = pl.program_id(0), pl.program_id(1)
        x_ext = jnp.concatenate([xp_ref[...], x_ref[...]], axis=0)
        g_ext = lax.dot_general(x_ext, wg_ref[j], _NT, preferred_element_type=F32).astype(BF16)
        up = lax.dot_general(x_ref[...], wu_ref[j], _NT, preferred_element_type=F32).astype(BF16)
        g_ref[...] = g_ext[HALO:]
        u_ref[...] = up
        a_ref[...] = _ffn_act(g_ext, up, cw_ref, cb_ref, gc_ref, i)

    whole = pl.BlockSpec((j_n, f, d), lambda i, j: (0, 0, 0))
    tile = pl.BlockSpec((None, tm, f), lambda i, j: (j, i, 0))
    shape = jax.ShapeDtypeStruct((j_n, s, f), BF16)
    return pl.pallas_call(
        body, name=name, grid=(s // tm, j_n),
        in_specs=[pl.BlockSpec((tm, d), lambda i, j: (i, 0)),
                  pl.BlockSpec((HALO, d), lambda i, j: (jnp.maximum(i * hb - 1, 0), 0)),
                  whole, whole,
                  pl.BlockSpec((None, 3, f), lambda i, j: (j, 0, 0)),
                  pl.BlockSpec((None, 1, f), lambda i, j: (j, 0, 0))] + ([ANY] if pin is not None else []),
        out_specs=[tile] * 4, out_shape=[shape] * 4,
        compiler_params=_params("parallel", "parallel"))(hn, hn, wg, wu, cw, cb, *([pin] if pin is not None else []))


def _ffn_act(g_ext, up, cw_ref, cb_ref, gc_ref, i):
    keep = jnp.where(i > 0, 1.0, 0.0)
    g32 = jnp.concatenate([g_ext[:HALO].astype(F32) * keep, g_ext[HALO:].astype(F32)], axis=0)
    gc = (_causal_conv(g32, cw_ref)[HALO:] + cb_ref[...]).astype(BF16)
    gc_ref[...] = gc
    gc = gc.astype(F32)
    return (gc * jax.nn.sigmoid(gc) * up.astype(F32)).astype(BF16)


def ffn_loss_fwd(h, gain, wg, wu, wd, cw, cb, target, final_gain, *, name):
    s, d = h.shape
    j_n, f, _ = wg.shape
    tm = min(ROW_TILE, s)
    hb = tm // HALO

    def body(h_ref, hp_ref, gain_ref, wg_ref, wu_ref, wd_ref, cw_ref, cb_ref, t_ref, fg_ref,
             o_ref, ob_ref, l_ref, dfg_ref, xn_ref, r_ref, g_ref, u_ref, gc_ref, x_s, acc_s):
        i, j = pl.program_id(0), pl.program_id(1)

        @pl.when((i == 0) & (j == 0))
        def _():
            l_ref[...] = jnp.zeros_like(l_ref)
            dfg_ref[...] = jnp.zeros_like(dfg_ref)

        @pl.when(j == 0)
        def _():
            h_ext = jnp.concatenate([hp_ref[...], h_ref[...]], axis=0)
            rstd = lax.rsqrt(jnp.mean(h_ext * h_ext, axis=-1, keepdims=True) + EPS)
            x_s[...] = (h_ext * rstd * gain_ref[...]).astype(BF16)
            xn_ref[...] = x_s[HALO:, :]
            r_ref[...] = rstd[HALO:]
            acc_s[...] = h_ref[...]

        g_ext = lax.dot_general(x_s[...], wg_ref[j], _NT, preferred_element_type=F32).astype(BF16)
        up = lax.dot_general(x_s[HALO:, :], wu_ref[j], _NT, preferred_element_type=F32).astype(BF16)
        g_ref[...] = g_ext[HALO:]
        u_ref[...] = up
        act = _ffn_act(g_ext, up, cw_ref, cb_ref, gc_ref, i)
        acc_s[...] += jnp.dot(act, wd_ref[j], preferred_element_type=F32)

        @pl.when(j == j_n - 1)
        def _():
            x = acc_s[...]
            rstd = lax.rsqrt(jnp.mean(x * x, axis=-1, keepdims=True) + EPS)
            hhat = x * rstd
            err = hhat * fg_ref[...] - t_ref[...]
            l_ref[...] += 0.5 * jnp.sum(jnp.mean(err * err, axis=-1, keepdims=True), axis=0, keepdims=True)
            dy = err * (1.0 / d)
            dfg_ref[...] += _colsum(dy * hhat)
            dyg = dy * fg_ref[...]
            dh = rstd * (dyg - hhat * jnp.mean(dyg * hhat, axis=-1, keepdims=True))
            o_ref[...] = dh
            ob_ref[...] = dh.astype(BF16)

    whole = pl.BlockSpec((j_n, f, d), lambda i, j: (0, 0, 0), pipeline_mode=pl.Buffered(1))
    row = pl.BlockSpec((tm, d), lambda i, j: (i, 0))
    vec = pl.BlockSpec((1, d), lambda i, j: (0, 0))
    tile = pl.BlockSpec((None, tm, f), lambda i, j: (j, i, 0))
    return pl.pallas_call(
        body, name=name, grid=(s // tm, j_n),
        in_specs=[row, pl.BlockSpec((HALO, d), lambda i, j: (jnp.maximum(i * hb - 1, 0), 0)),
                  vec, whole, whole, whole,
                  pl.BlockSpec((None, 3, f), lambda i, j: (j, 0, 0)),
                  pl.BlockSpec((None, 1, f), lambda i, j: (j, 0, 0)), row, vec],
        out_specs=[row, row, pl.BlockSpec((1, 128), lambda i, j: (0, 0)), vec,
                   row, pl.BlockSpec((tm, 1), lambda i, j: (i, 0)), tile, tile, tile],
        out_shape=[jax.ShapeDtypeStruct((s, d), F32), jax.ShapeDtypeStruct((s, d), BF16),
                   jax.ShapeDtypeStruct((1, 128), F32), jax.ShapeDtypeStruct((1, d), F32),
                   jax.ShapeDtypeStruct((s, d), BF16), jax.ShapeDtypeStruct((s, 1), F32)]
        + [jax.ShapeDtypeStruct((j_n, s, f), BF16)] * 3,
        scratch_shapes=[pltpu.VMEM((HALO + tm, d), BF16), pltpu.VMEM((tm, d), F32)],
        compiler_params=_params("arbitrary", "arbitrary"))(h, h, gain, wg, wu, wd, cw, cb, target, final_gain)


def ffn_bwd_a(dh_bf, hn, g, up, gc, wd, cw, *, name, pin=None):
    s, d = hn.shape
    j_n, _, f = g.shape
    tm = min(ROW_TILE, s)
    n_t = s // tm
    hb = tm // HALO

    def body(dh_ref, dhn_ref, x_ref, g_ref, gc_ref, gcn_ref, u_ref, un_ref, wd_ref, cw_ref, *rest):
        dg_ref, du_ref, dwg_ref, dwu_ref, dwd_ref, dcw_ref, dcb_ref = rest[-7:]
        i = pl.program_id(1)

        @pl.when(i == 0)
        def _():
            for r in (dwg_ref, dwu_ref, dwd_ref, dcw_ref, dcb_ref):
                r[...] = jnp.zeros_like(r)

        keep_n = jnp.where(i < n_t - 1, 1.0, 0.0)
        dh = dh_ref[...]
        dact = lax.dot_general(jnp.concatenate([dh, dhn_ref[...]], axis=0), wd_ref[...], _NT,
                               preferred_element_type=F32)
        dact = jnp.concatenate([dact[:tm], dact[tm:] * keep_n], axis=0)
        gc_ext = jnp.concatenate([gc_ref[...], gcn_ref[...]], axis=0).astype(F32)
        sig = jax.nn.sigmoid(gc_ext)
        silu = gc_ext * sig
        up_ext = jnp.concatenate([u_ref[...], un_ref[...]], axis=0).astype(F32)
        act = (silu * up_ext)[:tm].astype(BF16)
        dwd_ref[...] += lax.dot_general(act, dh, _TN, preferred_element_type=F32)
        dup = (dact * silu)[:tm].astype(BF16)
        du_ref[...] = dup
        dgc = dact * up_ext * (sig + silu * (1.0 - sig))
        dgc_1, dgc_2 = _fwd(dgc, 1), _fwd(dgc, 2)
        dg = (cw_ref[2:3, :] * dgc + cw_ref[1:2, :] * dgc_1 + cw_ref[0:1, :] * dgc_2)[:tm].astype(BF16)
        dg_ref[...] = dg
        x = x_ref[...]
        dwg_ref[...] += lax.dot_general(dg, x, _TN, preferred_element_type=F32)
        dwu_ref[...] += lax.dot_general(dup, x, _TN, preferred_element_type=F32)
        g32 = g_ref[...].astype(F32)
        dcb_ref[...] += _colsum(dgc[:tm])
        dcw_ref[2:3, :] += _colsum(dgc[:tm] * g32)
        dcw_ref[1:2, :] += _colsum(dgc_1[:tm] * g32)
        dcw_ref[0:1, :] += _colsum(dgc_2[:tm] * g32)

    rows = pl.BlockSpec((tm, d), lambda j, i: (i, 0))
    rows_next = pl.BlockSpec((HALO, d), lambda j, i: (jnp.minimum((i + 1) * hb, n_t * hb - 1), 0))
    tile = pl.BlockSpec((None, tm, f), lambda j, i: (j, i, 0))
    nxt = _ffn_next_halo(tm, f, n_t)
    per_j = lambda r, c: pl.BlockSpec((None, r, c), lambda j, i: (j, 0, 0))
    return pl.pallas_call(
        body, name=name, grid=(j_n, n_t),
        in_specs=[rows, rows_next, rows, tile, tile, nxt, tile, nxt, per_j(f, d), per_j(3, f)]
        + ([ANY] if pin is not None else []),
        out_specs=[tile, tile, per_j(f, d), per_j(f, d), per_j(f, d), per_j(3, f), per_j(1, f)],
        out_shape=[jax.ShapeDtypeStruct((j_n, s, f), BF16), jax.ShapeDtypeStruct((j_n, s, f), BF16),
                   jax.ShapeDtypeStruct((j_n, f, d), F32), jax.ShapeDtypeStruct((j_n, f, d), F32),
                   jax.ShapeDtypeStruct((j_n, f, d), F32), jax.ShapeDtypeStruct((j_n, 3, f), F32),
                   jax.ShapeDtypeStruct((j_n, 1, f), F32)],
        compiler_params=_params("parallel", "arbitrary"))(
            dh_bf, dh_bf, hn, g, gc, gc, up, up, wd, cw, *([pin] if pin is not None else []))


def dx_rms_bwd(pairs, h, gain, rstd, dres, *, name, pin=None):
    j_n, s, f = pairs[0][0].shape
    d = h.shape[1]
    tm = min(ROW_TILE, s)
    n_p = len(pairs)
    dims = [_NT if w.shape[1:] == (d, f) else (((1,), (0,)), ((), ())) for _, w in pairs]

    def body(*refs):
        dy_refs, w_refs = refs[:n_p], refs[n_p:2 * n_p]
        h_ref, g_ref, r_ref, dr_ref = refs[2 * n_p:2 * n_p + 4]
        o_ref, ob_ref, dgain_ref = refs[-3:]

        @pl.when(pl.program_id(0) == 0)
        def _():
            dgain_ref[...] = jnp.zeros_like(dgain_ref)
        dx = jnp.zeros((tm, d), F32)
        for j in range(j_n):
            for p in range(n_p):
                dx = dx + lax.dot_general(dy_refs[p][j], w_refs[p][j], dims[p], preferred_element_type=F32)
        rstd_v = r_ref[...]
        hhat = h_ref[...] * rstd_v
        dgain_ref[...] += _colsum(dx * hhat)
        dxg = dx * g_ref[...]
        dh = dr_ref[...] + rstd_v * (dxg - hhat * jnp.mean(dxg * hhat, axis=-1, keepdims=True))
        o_ref[...] = dh
        ob_ref[...] = dh.astype(BF16)

    tile4 = pl.BlockSpec((j_n, tm, f), lambda i: (0, i, 0))
    whole = [pl.BlockSpec(w.shape, lambda i: (0, 0, 0), pipeline_mode=pl.Buffered(1)) for _, w in pairs]
    row = pl.BlockSpec((tm, d), lambda i: (i, 0))
    vec = pl.BlockSpec((1, d), lambda i: (0, 0))
    return pl.pallas_call(
        body, name=name, grid=(s // tm,),
        in_specs=[tile4] * n_p + whole + [row, vec, pl.BlockSpec((tm, 1), lambda i: (i, 0)), row]
        + ([ANY] if pin is not None else []),
        out_specs=[row, row, vec],
        out_shape=[jax.ShapeDtypeStruct((s, d), F32), jax.ShapeDtypeStruct((s, d), BF16),
                   jax.ShapeDtypeStruct((1, d), F32)],
        compiler_params=_params("arbitrary"))(
            *[p[0] for p in pairs], *[p[1] for p in pairs], h, gain, rstd, dres, *([pin] if pin is not None else []))


def _sgu_gate(vn_bf, ws_ref, bs_ref, h, rows):
    tri = lax.broadcasted_iota(jnp.int32, (CHUNK, CHUNK), 0) >= lax.broadcasted_iota(jnp.int32, (CHUNK, CHUNK), 1)
    ws = jnp.where(tri, ws_ref[h], 0.0).astype(BF16)
    cols = slice((h % 4) * GROUP, (h % 4 + 1) * GROUP)
    return ws, jnp.dot(ws, vn_bf[h // 4][rows, cols], preferred_element_type=F32) + bs_ref[h]


def odd_layer_fwd(h, gain, win, sgu_norm, w_spatial, b_spatial, wout, *, name):
    s, d = h.shape
    w = win.shape[-1]
    ts = min(ROW_TILE, s)
    n_heads = w_spatial.shape[0]

    def body(h_ref, gain_ref, win_ref, n_ref, ws_ref, bs_ref, wout_ref, o_ref, xn_ref, r_ref, p_ref, m_ref, rv_ref):
        x = h_ref[...]
        rstd_x = lax.rsqrt(jnp.mean(x * x, axis=-1, keepdims=True) + EPS)
        xn = (x * rstd_x * gain_ref[...]).astype(BF16)
        xn_ref[...] = xn
        r_ref[...] = rstd_x
        for k in range(4):
            p_ref[k] = jnp.dot(xn, win_ref[k], preferred_element_type=F32).astype(BF16)
        v = [_gelu(p_ref[2].astype(F32)), _gelu(p_ref[3].astype(F32))]
        ms = (jnp.sum(v[0] * v[0], axis=-1, keepdims=True) + jnp.sum(v[1] * v[1], axis=-1, keepdims=True)) / (2 * w)
        rstd = lax.rsqrt(ms + EPS)
        rv_ref[...] = rstd
        vn = [(v[k] * rstd * n_ref[:, k * w:(k + 1) * w]).astype(BF16) for k in range(2)]
        for hd in range(n_heads):
            cols = slice((hd % 4) * GROUP, (hd % 4 + 1) * GROUP)
            for c in range(ts // CHUNK):
                rows = slice(c * CHUNK, (c + 1) * CHUNK)
                _, gate = _sgu_gate(vn, ws_ref, bs_ref, hd, rows)
                u = _gelu(p_ref[hd // 4, rows, cols].astype(F32))
                m_ref[rows, hd * GROUP:(hd + 1) * GROUP] = (u * gate).astype(BF16)
        o_ref[...] = x + jnp.dot(m_ref[...], wout_ref[...], preferred_element_type=F32)

    const = lambda shape: pl.BlockSpec(shape, lambda i: (0,) * len(shape))
    row = pl.BlockSpec((ts, d), lambda i: (i, 0))
    col1 = pl.BlockSpec((ts, 1), lambda i: (i, 0))
    return pl.pallas_call(
        body, name=name, grid=(s // ts,),
        in_specs=[row, const((1, d)), const((4, d, w)), const((1, 2 * w)),
                  const((n_heads, CHUNK, CHUNK)), const((n_heads, CHUNK, 1)), const((2 * w, d))],
        out_specs=[row, row, col1, pl.BlockSpec((4, ts, w), lambda i: (0, i, 0)),
                   pl.BlockSpec((ts, 2 * w), lambda i: (i, 0)), col1],
        out_shape=[jax.ShapeDtypeStruct((s, d), F32), jax.ShapeDtypeStruct((s, d), BF16),
                   jax.ShapeDtypeStruct((s, 1), F32), jax.ShapeDtypeStruct((4, s, w), BF16),
                   jax.ShapeDtypeStruct((s, 2 * w), BF16), jax.ShapeDtypeStruct((s, 1), F32)],
        compiler_params=_params("parallel"))(h, gain, win, sgu_norm, w_spatial, b_spatial, wout)


def sgu_bwd(p, mix, xn, dh_bf, wout, rstd, sgu_norm, w_spatial, b_spatial, *, name, pin=None):
    _, s, w = p.shape
    d = dh_bf.shape[1]
    ts = min(ROW_TILE, s)
    n_heads = w_spatial.shape[0]

    def body(p_ref, m_ref, x_ref, dh_ref, wout_ref, r_ref, n_ref, ws_ref, bs_ref, *rest):
        dp_ref, dwi_ref, dwo_ref, dn_ref, dws_ref, dbs_ref, dvn_ref, dm_ref = rest[-8:]

        @pl.when(pl.program_id(0) == 0)
        def _():
            dwi_ref[...] = jnp.zeros_like(dwi_ref)
            dwo_ref[...] = jnp.zeros_like(dwo_ref)
            dn_ref[...] = jnp.zeros_like(dn_ref)
            dws_ref[...] = jnp.zeros_like(dws_ref)
            dbs_ref[...] = jnp.zeros_like(dbs_ref)

        dwo_ref[...] += lax.dot_general(m_ref[...], dh_ref[...], _TN, preferred_element_type=F32)
        dm_ref[...] = lax.dot_general(dh_ref[...], wout_ref[...], _NT, preferred_element_type=F32)
        rstd_v = r_ref[...]
        vhat = [_gelu(p_ref[2 + k].astype(F32)) * rstd_v for k in range(2)]
        vn = [(vhat[k] * n_ref[:, k * w:(k + 1) * w]).astype(BF16) for k in range(2)]
        tri = lax.broadcasted_iota(jnp.int32, (CHUNK, CHUNK), 0) >= lax.broadcasted_iota(jnp.int32, (CHUNK, CHUNK), 1)
        for h in range(n_heads):
            cols = slice((h % 4) * GROUP, (h % 4 + 1) * GROUP)
            ocols = slice(h * GROUP, (h + 1) * GROUP)
            for c in range(ts // CHUNK):
                rows = slice(c * CHUNK, (c + 1) * CHUNK)
                ws, gate = _sgu_gate(vn, ws_ref, bs_ref, h, rows)
                pu = p_ref[h // 4, rows, cols].astype(F32)
                dm = dm_ref[rows, ocols]
                dp_ref[h // 4, rows, cols] = (dm * gate * _gelu_grad(pu)).astype(BF16)
                dgate = dm * _gelu(pu)
                dbs_ref[h] += jnp.sum(dgate, axis=-1, keepdims=True)
                dgate_bf = dgate.astype(BF16)
                dws = lax.dot_general(dgate_bf, vn[h // 4][rows, cols], _NT, preferred_element_type=F32)
                dws_ref[h] += jnp.where(tri, dws, 0.0)
                dvn_ref[rows, ocols] = lax.dot_general(ws, dgate_bf, _TN, preferred_element_type=F32)
        for k in range(2):
            kc = slice(k * w, (k + 1) * w)
            dvn = dvn_ref[:, kc]
            dn_ref[:, kc] += _colsum(dvn * vhat[k])
        dvh = [dvn_ref[:, k * w:(k + 1) * w] * n_ref[:, k * w:(k + 1) * w] for k in range(2)]
        dot = (jnp.sum(dvh[0] * vhat[0], axis=-1, keepdims=True)
               + jnp.sum(dvh[1] * vhat[1], axis=-1, keepdims=True)) / (2 * w)
        for k in range(2):
            dv = rstd_v * (dvh[k] - vhat[k] * dot)
            dp_ref[2 + k] = (dv * _gelu_grad(p_ref[2 + k].astype(F32))).astype(BF16)
        for k in range(4):
            dwi_ref[k] += lax.dot_general(x_ref[...], dp_ref[k], _TN, preferred_element_type=F32)

    const = lambda shape: pl.BlockSpec(shape, lambda i: (0,) * len(shape))
    tile4 = pl.BlockSpec((4, ts, w), lambda i: (0, i, 0))
    return pl.pallas_call(
        body, name=name, grid=(s // ts,),
        in_specs=[tile4, pl.BlockSpec((ts, 2 * w), lambda i: (i, 0)), pl.BlockSpec((ts, d), lambda i: (i, 0)),
                  pl.BlockSpec((ts, d), lambda i: (i, 0)), const((2 * w, d)), pl.BlockSpec((ts, 1), lambda i: (i, 0)),
                  const((1, 2 * w)), const((n_heads, CHUNK, CHUNK)), const((n_heads, CHUNK, 1))]
        + ([ANY] if pin is not None else []),
        out_specs=[tile4, const((4, d, w)), const((2 * w, d)), const((1, 2 * w)), const((n_heads, CHUNK, CHUNK)),
                   const((n_heads, CHUNK, 1))],
        out_shape=[jax.ShapeDtypeStruct((4, s, w), BF16), jax.ShapeDtypeStruct((4, d, w), F32),
                   jax.ShapeDtypeStruct((2 * w, d), F32),
                   jax.ShapeDtypeStruct((1, 2 * w), F32),
                   jax.ShapeDtypeStruct((n_heads, CHUNK, CHUNK), F32),
                   jax.ShapeDtypeStruct((n_heads, CHUNK, 1), F32)],
        scratch_shapes=[pltpu.VMEM((ts, 2 * w), F32), pltpu.VMEM((ts, 2 * w), F32)],
        compiler_params=_params("arbitrary"))(
            p, mix, xn, dh_bf, wout, rstd, sgu_norm, w_spatial, b_spatial, *([pin] if pin is not None else []))


def _row_tile(rows):
    if rows <= ROW_TILE:
        return rows
    for t in (512, 384, 352, 256, 128, 64, 32, 16, 8):
        if rows % t == 0:
            return t
    return rows


def adamw(w, g, m, v, *, name):
    shape = w.shape
    cols = shape[-1]
    rows = w.size // cols
    w2, g2, m2, v2 = (a.reshape(rows, cols) for a in (w, g, m, v))
    tr = _row_tile(rows)
    bc1 = 1.0 - ADAM_B1 ** ADAM_STEP
    bc2 = 1.0 - ADAM_B2 ** ADAM_STEP

    def body(w_ref, g_ref, m_ref, v_ref, d_ref, nm_ref, nv_ref):
        grad = g_ref[...]
        m_new = ADAM_B1 * m_ref[...] + (1.0 - ADAM_B1) * grad
        v_new = ADAM_B2 * v_ref[...] + (1.0 - ADAM_B2) * (grad * grad)
        nm_ref[...] = m_new
        nv_ref[...] = v_new
        d_ref[...] = -ADAM_LR * ((m_new / bc1) / (jnp.sqrt(v_new / bc2) + ADAM_EPS) + ADAM_WD * w_ref[...])

    spec = pl.BlockSpec((tr, cols), lambda i: (i, 0))
    outs = pl.pallas_call(
        body, name=name, grid=(rows // tr,),
        in_specs=[spec] * 4, out_specs=[spec] * 3,
        out_shape=[jax.ShapeDtypeStruct((rows, cols), F32)] * 3,
        compiler_params=_params("parallel"))(w2, g2, m2, v2)
    return tuple(o.reshape(shape) for o in outs)


def adamw_layers(w, grads, m, v, *, name):
    n_l, rows, cols = w.shape
    tr = _row_tile(rows)
    bc1 = 1.0 - ADAM_B1 ** ADAM_STEP
    bc2 = 1.0 - ADAM_B2 ** ADAM_STEP
    outs = None
    for l in range(n_l):
        def body(w_ref, g_ref, m_ref, v_ref, *rest):
            go_ref, d_ref, nm_ref, nv_ref = rest[-4:]
            grad = g_ref[...]
            m_new = ADAM_B1 * m_ref[...] + (1.0 - ADAM_B1) * grad
            v_new = ADAM_B2 * v_ref[...] + (1.0 - ADAM_B2) * (grad * grad)
            go_ref[...] = grad
            nm_ref[...] = m_new
            nv_ref[...] = v_new
            d_ref[...] = -ADAM_LR * ((m_new / bc1) / (jnp.sqrt(v_new / bc2) + ADAM_EPS) + ADAM_WD * w_ref[...])

        layer = pl.BlockSpec((None, tr, cols), lambda i, l=l: (l, i, 0))
        prev = list(outs) if outs is not None else []
        outs = pl.pallas_call(
            body, name=f"{name}{l}", grid=(rows // tr,),
            in_specs=[layer, pl.BlockSpec((tr, cols), lambda i: (i, 0)), layer, layer] + [ANY] * len(prev),
            out_specs=[layer] * 4,
            out_shape=[jax.ShapeDtypeStruct(w.shape, F32)] * 4,
            input_output_aliases={4 + k: k for k in range(len(prev))},
            compiler_params=_params("parallel"))(w, grads[l], m, v, *prev)
    return tuple(outs)


def _place():
    return lax.axis_index("x"), lax.axis_index("y"), lax.axis_index("c")


def _other_chips(x, y):
    return [(1 - x, y), (x, 1 - y), (1 - x, 1 - y)]


HBM = pl.BlockSpec(memory_space=pltpu.HBM)
SEM = pl.BlockSpec(memory_space=pltpu.SEMAPHORE)
DATAFLOW = pltpu.SideEffectType.DATAFLOW_SIDE_EFFECTING


def _in_hbm(a):
    return pltpu.with_memory_space_constraint(a, pltpu.HBM)


def cast_into_slot(w, chip, *, l=None, name, pin=None):
    rows, cols = w.shape[-2:]
    tr = _row_tile(rows)

    def body(chip_ref, w_ref, *rest):
        rest[-1][...] = w_ref[...].astype(BF16)

    in_spec = (pl.BlockSpec((tr, cols), lambda i, chip_ref: (i, 0)) if l is None
               else pl.BlockSpec((None, tr, cols), lambda i, chip_ref: (l, i, 0)))
    return pl.pallas_call(
        body, name=name,
        grid_spec=pltpu.PrefetchScalarGridSpec(
            num_scalar_prefetch=1, grid=(rows // tr,), in_specs=[in_spec] + ([ANY] if pin is not None else []),
            out_specs=pl.BlockSpec((None, tr, cols), lambda i, chip_ref: (chip_ref[0], i, 0))),
        out_shape=jax.ShapeDtypeStruct((N_CHIPS, rows, cols), BF16),
        compiler_params=_params("parallel"))(chip, w, *([pin] if pin is not None else []))


def _half(ref, slot, c):
    half = ref.shape[1] // 2
    return ref.at[slot, pl.ds(c * half, half), :]


def gather_start(groups, smalls, *, name):
    flat = [b for g in groups for b in g]
    n_b, n_s, n_g = len(flat), len(smalls), len(groups)
    n_sem = 2 * n_g + (2 if n_s else 0)

    def body(*refs):
        bufs, small_refs = refs[:n_b], refs[n_b:n_b + n_s]
        sems = refs[n_b + n_s:n_b + n_s + n_sem]
        token = refs[-1]
        x, y, c = _place()
        me = 2 * x + y
        chips = _other_chips(x, y)
        for si in range(n_s):
            piece = small_refs[si].at[me]
            for k, (px, py) in enumerate(chips):
                pltpu.make_async_remote_copy(
                    src_ref=piece, dst_ref=piece,
                    send_sem=sems[2 * n_g].at[3 * si + k], recv_sem=sems[2 * n_g + 1].at[3 * si + k],
                    device_id=(px, py, c), device_id_type=MESH).start()
        t = 0
        for gi, group in enumerate(groups):
            for ti in range(len(group)):
                piece = _half(bufs[t], me, c)
                t += 1
                for k, (px, py) in enumerate(chips):
                    pltpu.make_async_remote_copy(
                        src_ref=piece, dst_ref=piece,
                        send_sem=sems[2 * gi].at[3 * ti + k], recv_sem=sems[2 * gi + 1].at[3 * ti + k],
                        device_id=(px, py, c), device_id_type=MESH).start()
        token[...] = jnp.zeros_like(token)

    sem_shapes = []
    for group in groups:
        sem_shapes += [pltpu.SemaphoreType.DMA((3 * len(group),))] * 2
    if n_s:
        sem_shapes += [pltpu.SemaphoreType.DMA((3 * n_s,))] * 2
    arrays = flat + list(smalls)
    res = pl.pallas_call(
        body, name=name,
        out_shape=tuple(sem_shapes) + tuple(pltpu.HBM(a.shape, a.dtype) for a in arrays)
        + (jax.ShapeDtypeStruct((8, 128), F32),),
        in_specs=[HBM] * len(arrays),
        out_specs=tuple([SEM] * n_sem + [HBM] * len(arrays) + [pl.BlockSpec(memory_space=pltpu.VMEM)]),
        input_output_aliases={i: n_sem + i for i in range(len(arrays))},
        compiler_params=pltpu.CompilerParams(has_side_effects=DATAFLOW))(*[_in_hbm(a) for a in arrays])
    sems, thru, token = res[:n_sem], res[n_sem:-1], res[-1]
    out_groups, t = [], 0
    for group in groups:
        out_groups.append(list(thru[t:t + len(group)]))
        t += len(group)
    return sems, out_groups, list(thru[n_b:]), token


def gather_wait(bufs, send, recv, after, *, name, smalls=(), small_send=None, small_recv=None):
    n_b, n_s = len(bufs), len(smalls)
    arrays = list(bufs) + list(smalls)
    sem_ops = [send, recv] + ([small_send, small_recv] if n_s else [])

    def body(*refs):
        buf_refs, small_refs = refs[:n_b], refs[n_b:n_b + n_s]
        sems = refs[n_b + n_s:n_b + n_s + len(sem_ops)]
        x, y, c = _place()
        me = 2 * x + y
        chips = _other_chips(x, y)
        for ti in range(n_b):
            for k, (px, py) in enumerate(chips):
                done = pltpu.make_async_remote_copy(
                    src_ref=_half(buf_refs[ti], me, c), dst_ref=_half(buf_refs[ti], 2 * px + py, c),
                    send_sem=sems[0].at[3 * ti + k], recv_sem=sems[1].at[3 * ti + k],
                    device_id=(px, py, c), device_id_type=MESH)
                done.wait_send()
                done.wait_recv()
        for si in range(n_s):
            for k, (px, py) in enumerate(chips):
                done = pltpu.make_async_remote_copy(
                    src_ref=small_refs[si].at[me], dst_ref=small_refs[si].at[2 * px + py],
                    send_sem=sems[2].at[3 * si + k], recv_sem=sems[3].at[3 * si + k],
                    device_id=(px, py, c), device_id_type=MESH)
                done.wait_send()
                done.wait_recv()

    res = pl.pallas_call(
        body, name=name,
        out_shape=tuple(pltpu.HBM(a.shape, a.dtype) for a in arrays),
        in_specs=[HBM] * len(arrays) + [SEM] * len(sem_ops) + [ANY],
        out_specs=tuple([HBM] * len(arrays)),
        input_output_aliases={i: i for i in range(len(arrays))},
        compiler_params=pltpu.CompilerParams(has_side_effects=DATAFLOW))(*arrays, *sem_ops, after)
    return list(res[:n_b]), list(res[n_b:])


def gather_forward(bufs, *, name):
    n = len(bufs)

    def body(*refs):
        ins, outs = refs[:n], refs[n:2 * n]
        send_sems, recv_sems = refs[2 * n:]
        x, y, c = _place()
        chips = _other_chips(x, y)
        for t in range(n):
            for k, (px, py) in enumerate(chips):
                pltpu.make_async_remote_copy(
                    src_ref=_half(ins[t], 2 * px + py, c), dst_ref=_half(outs[t], 2 * px + py, c),
                    send_sem=send_sems.at[3 * t + k], recv_sem=recv_sems.at[3 * t + k],
                    device_id=(x, y, 1 - c), device_id_type=MESH).start()
        for t in range(n):
            for k, (px, py) in enumerate(chips):
                done = pltpu.make_async_remote_copy(
                    src_ref=_half(ins[t], 2 * px + py, c), dst_ref=_half(outs[t], 2 * px + py, 1 - c),
                    send_sem=send_sems.at[3 * t + k], recv_sem=recv_sems.at[3 * t + k],
                    device_id=(x, y, 1 - c), device_id_type=MESH)
                done.wait_send()
                done.wait_recv()

    return pl.pallas_call(
        body, name=name, in_specs=[ANY] * n, out_specs=[ANY] * n,
        out_shape=[jax.ShapeDtypeStruct(a.shape, a.dtype) for a in bufs],
        input_output_aliases={i: i for i in range(n)},
        scratch_shapes=[pltpu.SemaphoreType.DMA((3 * n,)), pltpu.SemaphoreType.DMA((3 * n,))],
        compiler_params=pltpu.CompilerParams(has_side_effects=True))(*bufs)


def sum_stage_a(grad, recv, place, wire, *, name):
    j_n, half, cols = recv.shape

    def body(place_ref, g_ref, r_ref, o_ref, ob_ref):
        acc = g_ref[...] + r_ref[...]
        ob_ref[...] = acc.astype(wire)

        @pl.when(pl.program_id(0) == place_ref[0])
        def _():
            o_ref[...] = acc

    blk = (None, half, cols)
    return pl.pallas_call(
        body, name=name,
        grid_spec=pltpu.PrefetchScalarGridSpec(
            num_scalar_prefetch=1, grid=(j_n,),
            in_specs=[pl.BlockSpec(blk, lambda j, place_ref: (j, place_ref[1], 0)),
                      pl.BlockSpec(blk, lambda j, place_ref: (j, 0, 0))],
            out_specs=[pl.BlockSpec((half, cols), lambda j, place_ref: (0, 0)),
                       pl.BlockSpec(blk, lambda j, place_ref: (j, 0, 0))]),
        out_shape=[jax.ShapeDtypeStruct((half, cols), F32), jax.ShapeDtypeStruct(recv.shape, wire)],
        compiler_params=_params("arbitrary"))(place, grad, recv)


def _stage_a_copies(srcs, lands, x, y, c):
    out = []
    for src, land in zip(srcs, lands):
        half = src.shape[1] // 2
        out.append((src.at[:, pl.ds((1 - c) * half, half), :], land, (x, y, 1 - c)))
    return out


def _stage_b_copies(srcs, lands, x, y, c):
    out = []
    for src, land in zip(srcs, lands):
        for k, (px, py) in enumerate(_other_chips(x, y)):
            out.append((src.at[2 * px + py], land.at[k], (px, py, c)))
    return out


def _forward_copies(bufs, _, x, y, c):
    out = []
    for buf in bufs:
        for px, py in _other_chips(x, y):
            out.append((_half(buf, 2 * px + py, c), _half(buf, 2 * px + py, c), (x, y, 1 - c)))
    return out


def _stage_c_copies(fulls, _, x, y, c):
    out = []
    for full in fulls:
        half = full.shape[0] // 2
        mine = full.at[pl.ds(c * half, half), :]
        out.append((mine, mine, (x, y, 1 - c)))
    return out


def split_start(srcs, lands, copies, *, name):
    n, n_all = len(srcs), len(srcs) + len(lands)
    n_c = len(copies(srcs, lands, 0, 0, 0))

    def body(*refs):
        src_refs, land_refs = refs[:n], refs[n:n_all]
        send_sems, recv_sems = refs[n_all], refs[n_all + 1]
        token = refs[-1]
        x, y, c = _place()
        for k, (src, dst, target) in enumerate(copies(src_refs, land_refs, x, y, c)):
            pltpu.make_async_remote_copy(src_ref=src, dst_ref=dst, send_sem=send_sems.at[k], recv_sem=recv_sems.at[k],
                                         device_id=target, device_id_type=MESH).start()
        token[...] = jnp.zeros_like(token)

    arrays = list(srcs) + list(lands)
    res = pl.pallas_call(
        body, name=name,
        out_shape=(pltpu.SemaphoreType.DMA((n_c,)), pltpu.SemaphoreType.DMA((n_c,)))
        + tuple(pltpu.HBM(a.shape, a.dtype) for a in arrays) + (jax.ShapeDtypeStruct((8, 128), F32),),
        in_specs=[HBM] * n_all,
        out_specs=tuple([SEM, SEM] + [HBM] * n_all + [pl.BlockSpec(memory_space=pltpu.VMEM)]),
        input_output_aliases={i: 2 + i for i in range(n_all)},
        compiler_params=pltpu.CompilerParams(has_side_effects=DATAFLOW))(*[_in_hbm(a) for a in arrays])
    return res[0], res[1], list(res[2:2 + n]), list(res[2 + n:2 + n_all]), res[-1]


def split_wait(srcs, lands, send, recv, copies, after, *, name):
    n, n_all = len(srcs), len(srcs) + len(lands)

    def body(*refs):
        src_refs, land_refs = refs[:n], refs[n:n_all]
        send_sems, recv_sems = refs[n_all], refs[n_all + 1]
        x, y, c = _place()
        for k, (src, dst, target) in enumerate(copies(src_refs, land_refs, x, y, c)):
            done = pltpu.make_async_remote_copy(src_ref=src, dst_ref=dst, send_sem=send_sems.at[k],
                                                recv_sem=recv_sems.at[k], device_id=target, device_id_type=MESH)
            done.wait_send()
            done.wait_recv()

    arrays = list(srcs) + list(lands)
    res = pl.pallas_call(
        body, name=name,
        out_shape=tuple(pltpu.HBM(a.shape, a.dtype) for a in arrays),
        in_specs=[HBM] * n_all + [SEM, SEM, ANY],
        out_specs=tuple([HBM] * n_all),
        input_output_aliases={i: i for i in range(n_all)},
        compiler_params=pltpu.CompilerParams(has_side_effects=DATAFLOW))(*arrays, send, recv, after)
    return list(res[:n]), list(res[n:])


def sum_stage_b(part, recv, place, *, name):
    half, cols = part.shape

    def body(place_ref, p_ref, r_ref, o_ref):
        acc = p_ref[...]
        for k in range(3):
            acc = acc + r_ref[k].astype(F32)
        o_ref[...] = acc

    return pl.pallas_call(
        body, name=name,
        grid_spec=pltpu.PrefetchScalarGridSpec(
            num_scalar_prefetch=1, grid=(1,),
            in_specs=[pl.BlockSpec((half, cols), lambda i, place_ref: (0, 0)),
                      pl.BlockSpec((3, half, cols), lambda i, place_ref: (0, 0, 0))],
            out_specs=pl.BlockSpec((half, cols), lambda i, place_ref: (place_ref[1], 0))),
        out_shape=jax.ShapeDtypeStruct((2 * half, cols), F32),
        compiler_params=_params("arbitrary"))(place, part, recv)


def gather_chip_blocks(slots, *, name):
    def body(in_ref, out_ref, send_sems, recv_sems):
        x, y, c = _place()
        me = 2 * x + y
        chips = _other_chips(x, y)
        for k, (px, py) in enumerate(chips):
            pltpu.make_async_remote_copy(
                src_ref=in_ref.at[me], dst_ref=out_ref.at[me],
                send_sem=send_sems.at[k], recv_sem=recv_sems.at[k],
                device_id=(px, py, c), device_id_type=MESH).start()
        for k, (px, py) in enumerate(chips):
            done = pltpu.make_async_remote_copy(
                src_ref=in_ref.at[me], dst_ref=out_ref.at[2 * px + py],
                send_sem=send_sems.at[k], recv_sem=recv_sems.at[k],
                device_id=(px, py, c), device_id_type=MESH)
            done.wait_send()
            done.wait_recv()

    return pl.pallas_call(
        body, name=name, in_specs=[ANY], out_specs=ANY,
        out_shape=jax.ShapeDtypeStruct(slots.shape, slots.dtype),
        input_output_aliases={0: 0},
        scratch_shapes=[pltpu.SemaphoreType.DMA((3,)), pltpu.SemaphoreType.DMA((3,))],
        compiler_params=pltpu.CompilerParams(has_side_effects=True))(slots)


def _ffn_bwd(dh, dh_bf, h, gain, saved, wg, wu, wd, cw, cb, place, l, pin):
    hn, rstd, g, up, gc = saved
    dg, dup, dwg, dwu, dwd, dcw, dcb = ffn_bwd_a(dh_bf, hn, g, up, gc, wd, cw, name=f"ffn{l}_bwd_a", pin=pin)
    red = _reduce_a_begin([(dwg, BF16), (dwu, BF16), (dwd, BF16)], tag=f"f{l}")
    dh_in, dh_in_bf, dgain = dx_rms_bwd([(dg, wg), (dup, wu)], h, gain, rstd, dh, name=f"ffn{l}_bwd_b",
                                        pin=red[-1])
    red = _reduce_b_begin(red, place, dh_in_bf, tag=f"f{l}")
    return dh_in, dh_in_bf, (dcw, dcb, dgain), red


def _reduce_a_begin(grads, *, tag):
    lands = [lax.empty((g.shape[0], g.shape[1] // 2, g.shape[2]), F32) for g, _ in grads]
    send, recv, srcs, lands, token = split_start([g for g, _ in grads], lands, _stage_a_copies,
                                                 name=f"reduce_a_start_{tag}")
    return [w for _, w in grads], send, recv, srcs, lands, token


def _reduce_b_begin(state, place, after, *, tag):
    wires, send, recv, srcs, lands, _ = state
    grads, recv_a = split_wait(srcs, lands, send, recv, _stage_a_copies, after, name=f"reduce_a_wait_{tag}")
    parts = [sum_stage_a(g, r, place, w, name=f"sum_a_{tag}{i}") for i, (g, r, w) in enumerate(zip(grads, recv_a, wires))]
    lands_b = [lax.empty((3,) + p[1].shape[1:], p[1].dtype) for p in parts]
    send, recv, srcs, lands, token = split_start([p[1] for p in parts], lands_b, _stage_b_copies,
                                                 name=f"reduce_b_start_{tag}")
    return [p[0] for p in parts], send, recv, srcs, lands, token


def _reduce_c_begin(state, place, after, *, tag):
    parts, send, recv, srcs, lands, _ = state
    _, recv_b = split_wait(srcs, lands, send, recv, _stage_b_copies, after, name=f"reduce_b_wait_{tag}")
    halves = [sum_stage_b(p, r, place, name=f"sum_b_{tag}{i}") for i, (p, r) in enumerate(zip(parts, recv_b))]
    send, recv, fulls, _, token = split_start(halves, [], _stage_c_copies, name=f"reduce_c_start_{tag}")
    return send, recv, fulls, token


def _reduce_finish(state, after, *, tag):
    send, recv, fulls, _ = state
    fulls, _ = split_wait(fulls, [], send, recv, _stage_c_copies, after, name=f"reduce_c_wait_{tag}")
    return fulls


def kernel(x, norm_mix, norm_ffn, final_norm, w_in_even, conv_a, w_pool, pool_scale, w_out_even, w_in_odd, sgu_norm, w_spatial, b_spatial, w_out_odd, w_ffn_gate, w_ffn_up, conv_ffn, b_conv_ffn, w_ffn_down, loss_target, m_norm_mix, m_norm_ffn, m_final_norm, m_w_in_even, m_conv_a, m_w_pool, m_pool_scale, m_w_out_even, m_w_in_odd, m_sgu_norm, m_w_spatial, m_b_spatial, m_w_out_odd, m_w_ffn_gate, m_w_ffn_up, m_conv_ffn, m_b_conv_ffn, m_w_ffn_down, v_norm_mix, v_norm_ffn, v_final_norm, v_w_in_even, v_conv_a, v_w_pool, v_pool_scale, v_w_out_even, v_w_in_odd, v_sgu_norm, v_w_spatial, v_b_spatial, v_w_out_odd, v_w_ffn_gate, v_w_ffn_up, v_conv_ffn, v_b_conv_ffn, v_w_ffn_down):
    weights = dict(norm_mix=norm_mix, norm_ffn=norm_ffn, final_norm=final_norm, w_in_even=w_in_even,
                   conv_a=conv_a, w_pool=w_pool, pool_scale=pool_scale, w_out_even=w_out_even,
                   w_in_odd=w_in_odd, sgu_norm=sgu_norm, w_spatial=w_spatial, b_spatial=b_spatial,
                   w_out_odd=w_out_odd, w_ffn_gate=w_ffn_gate, w_ffn_up=w_ffn_up, conv_ffn=conv_ffn,
                   b_conv_ffn=b_conv_ffn, w_ffn_down=w_ffn_down)
    m_in = dict(norm_mix=m_norm_mix, norm_ffn=m_norm_ffn, final_norm=m_final_norm, w_in_even=m_w_in_even,
                conv_a=m_conv_a, w_pool=m_w_pool, pool_scale=m_pool_scale, w_out_even=m_w_out_even,
                w_in_odd=m_w_in_odd, sgu_norm=m_sgu_norm, w_spatial=m_w_spatial, b_spatial=m_b_spatial,
                w_out_odd=m_w_out_odd, w_ffn_gate=m_w_ffn_gate, w_ffn_up=m_w_ffn_up, conv_ffn=m_conv_ffn,
                b_conv_ffn=m_b_conv_ffn, w_ffn_down=m_w_ffn_down)
    v_in = dict(norm_mix=v_norm_mix, norm_ffn=v_norm_ffn, final_norm=v_final_norm, w_in_even=v_w_in_even,
                conv_a=v_conv_a, w_pool=v_w_pool, pool_scale=v_pool_scale, w_out_even=v_w_out_even,
                w_in_odd=v_w_in_odd, sgu_norm=v_sgu_norm, w_spatial=v_w_spatial, b_spatial=v_b_spatial,
                w_out_odd=v_w_out_odd, w_ffn_gate=v_w_ffn_gate, w_ffn_up=v_w_ffn_up, conv_ffn=v_conv_ffn,
                b_conv_ffn=v_b_conv_ffn, w_ffn_down=v_w_ffn_down)
    order = list(weights)

    chip = 2 * lax.axis_index("x") + lax.axis_index("y")
    core = lax.axis_index("c")
    place = jnp.stack([chip, core]).astype(jnp.int32)
    chip_arr = place[:1]

    h0 = x[0]
    target = loss_target[0]
    d_model = h0.shape[1]
    f_shard = w_ffn_gate.shape[-1]

    def turned(a):
        return jnp.transpose(a, (0, 2, 1))

    def own_slot(v):
        return lax.dynamic_update_index_in_dim(jnp.zeros((N_CHIPS,) + v.shape, v.dtype), v, chip, 0)

    first = [[cast_into_slot(w_in_even[0], chip_arr, name="cast_win_e"),
              cast_into_slot(w_out_even[0], chip_arr, name="cast_wout_e")]]
    smalls = [own_slot(conv_a[0]), own_slot(sgu_norm), own_slot(conv_ffn[0]), own_slot(conv_ffn[1])]
    sems0, first, smalls, token0 = gather_start(first, smalls, name="gather_start0")
    rest = [
        [cast_into_slot(turned(w_ffn_gate), chip_arr, l=0, name="cast_wg0", pin=token0),
         cast_into_slot(turned(w_ffn_up), chip_arr, l=0, name="cast_wu0", pin=token0)],
        [cast_into_slot(w_ffn_down, chip_arr, l=0, name="cast_wd0", pin=token0)],
        [cast_into_slot(w_in_odd[0], chip_arr, name="cast_win_o", pin=token0),
         cast_into_slot(w_out_odd[0], chip_arr, name="cast_wout_o", pin=token0)],
        [cast_into_slot(turned(w_ffn_gate), chip_arr, l=1, name="cast_wg1", pin=token0),
         cast_into_slot(turned(w_ffn_up), chip_arr, l=1, name="cast_wu1", pin=token0)],
        [cast_into_slot(w_ffn_down, chip_arr, l=1, name="cast_wd1", pin=token0)]]
    sems1, rest, _, token = gather_start(rest, [], name="gather_start1")
    groups = first + rest
    sems = list(sems0[:2]) + list(sems1) + list(sems0[2:])

    def arrive(gi, after, with_smalls=False):
        kw = dict(smalls=smalls, small_send=sems[-2], small_recv=sems[-1]) if with_smalls else {}
        bufs, small_out = gather_wait(groups[gi], sems[2 * gi], sems[2 * gi + 1], after, name=f"gather_wait{gi}", **kw)
        return gather_forward(bufs, name=f"gather_forward{gi}"), small_out

    def arrive_begin(gi, after):
        bufs, _ = gather_wait(groups[gi], sems[2 * gi], sems[2 * gi + 1], after, name=f"gather_wait{gi}")
        send, recv, bufs, _, tok = split_start(bufs, [], _forward_copies, name=f"gather_forward_start{gi}")
        return (send, recv, bufs), tok

    def arrive_end(state, after, gi):
        send, recv, bufs = state
        return split_wait(bufs, [], send, recv, _forward_copies, after, name=f"gather_forward_wait{gi}")[0]

    cb = b_conv_ffn.reshape(-1, N_CHIPS, 1, f_shard)
    wp_bf = w_pool[0].astype(BF16)
    wp_t_bf = jnp.transpose(w_pool[0], (0, 2, 1)).astype(BF16)
    ws = w_spatial[0]
    bs = b_spatial[0][:, :, None]

    (win_e, wout_e), (ca_g, sn_g, cw0, cw1) = arrive(0, token, with_smalls=True)
    wout_e = wout_e.reshape(-1, d_model)
    ca_full = jnp.transpose(ca_g, (1, 0, 2)).reshape(ca_g.shape[1], -1)
    sn_full = sn_g.reshape(1, -1)
    h1, xn0, rstd0, proj0, mix0, hn0, rstdf0 = even_layer_fwd(
        h0, norm_mix[0:1], win_e, ca_full, wp_bf, pool_scale, wout_e, norm_ffn[0:1], name="l0_fwd")
    (wg0, wu0), _ = arrive(1, hn0)
    g0, up0, gc0, act0 = ffn_in_fwd(hn0, wg0, wu0, cw0, cb[0], name="ffn0_in")
    swap2, tok2 = arrive_begin(2, act0)
    swap3, tok3 = arrive_begin(3, tok2)
    (wd0,) = arrive_end(swap2, tok3, 2)
    h2 = mm_acc(act0, wd0, h1, name="ffn0_down")
    ffn0 = (hn0, rstdf0, g0, up0, gc0)
    swap4, tok4 = arrive_begin(4, h2)
    swap5, tok5 = arrive_begin(5, tok4)
    win_o, wout_o = arrive_end(swap3, tok5, 3)
    wout_o = wout_o.reshape(-1, d_model)
    h3, xn1, rstd1, p1, mix1, rstd_v = odd_layer_fwd(h2, norm_mix[1:2], win_o, sn_full, ws, bs, wout_o, name="l1_fwd")
    wg1, wu1 = arrive_end(swap4, h3, 4)
    (wd1,) = arrive_end(swap5, wg1, 5)
    dh4, dh4_bf, loss_row, d_final, hn1, rstdf1, g1, up1, gc1 = ffn_loss_fwd(
        h3, norm_ffn[1:2], wg1, wu1, wd1, cw1, cb[1], target, final_norm[None], name="ffn1_fwd_loss")
    ffn1 = (hn1, rstdf1, g1, up1, gc1)

    loss = lax.psum(loss_row[0, 0], ("x", "y", "c"))

    dh3, dh3_bf, (dcw1, dcb1, dnf1), red3 = _ffn_bwd(
        dh4, dh4_bf, h3, norm_ffn[1:2], ffn1, wg1, wu1, wd1, cw1, cb[1], place, 1, None)

    def as_blocks(a):
        return a.reshape(N_CHIPS, -1, d_model)

    dp1, dwin_o, dwout_o, dsn, dws, dbs = sgu_bwd(p1, mix1, xn1, dh3_bf, wout_o, rstd_v, sn_full, ws, bs,
                                                  name="l1_mix_bwd", pin=red3[-1])
    dwout_o = as_blocks(dwout_o)
    red2 = _reduce_a_begin([(dwin_o, BF16), (dwout_o, BF16)], tag="m1")
    dh2, dh2_bf, dnm1 = dx_rms_bwd([(dp1, win_o)], h2, norm_mix[1:2], rstd1, dh3, name="l1_dx", pin=red2[-1])
    red2 = _reduce_b_begin(red2, place, dh2_bf, tag="m1")

    dh1, dh1_bf, (dcw0, dcb0, dnf0), red1 = _ffn_bwd(
        dh2, dh2_bf, h1, norm_ffn[0:1], ffn0, wg0, wu0, wd0, cw0, cb[0], place, 0, red2[-1])

    dproj0, dwin_e, dwout_e, dca, dwp, dps = even_bwd(proj0, mix0, xn0, dh1_bf, wout_e, ca_full, wp_bf, wp_t_bf,
                                                      pool_scale, name="l0_mix_bwd", pin=red1[-1])
    dwout_e = as_blocks(dwout_e)
    dh0, _, dnm0 = dx_rms_bwd([(dproj0, win_e)], h0, norm_mix[0:1], rstd0, dh1, name="l0_dx")
    grad_x = dh0[None]

    small_parts = dict(
        norm_mix=jnp.concatenate([dnm0, dnm1]), norm_ffn=jnp.concatenate([dnf0, dnf1]), final_norm=d_final,
        conv_a=dca, w_pool=dwp, pool_scale=dps, sgu_norm=dsn, w_spatial=dws, b_spatial=dbs,
        conv_ffn=jnp.stack([dcw0, dcw1]), b_conv_ffn=jnp.stack([dcb0, dcb1]))
    flat = jnp.concatenate([v.reshape(-1) for v in small_parts.values()])
    pad = (-flat.shape[0]) % (N_CHIPS * 32 * 128)
    small = jnp.pad(flat, (0, pad)).reshape(N_CHIPS, -1, 128)
    red0 = _reduce_a_begin([(dwin_e, BF16), (dwout_e, BF16), (small, F32)], tag="m0")

    swap_f1 = _reduce_c_begin(red3, place, red0[-1], tag="f1")
    red0 = _reduce_b_begin(red0, place, swap_f1[-1], tag="m0")
    swap_m1 = _reduce_c_begin(red2, place, red0[-1], tag="m1")
    swap_f0 = _reduce_c_begin(red1, place, swap_m1[-1], tag="f0")
    full3 = _reduce_finish(swap_f1, swap_f0[-1], tag="f1")
    full2 = _reduce_finish(swap_m1, full3[0], tag="m1")
    full1 = _reduce_finish(swap_f0, full2[0], tag="f0")
    swap_m0 = _reduce_c_begin(red0, place, full1[0], tag="m0")
    full0 = _reduce_finish(swap_m0, swap_m0[-1], tag="m0")
    small_slots = lax.dynamic_update_index_in_dim(jnp.zeros(small.shape, F32), full0[2], chip, 0)
    small_sum = gather_chip_blocks(small_slots, name="gather_small").reshape(-1)
    grads = {
        "w_in_even": full0[0][None], "w_out_even": full0[1][None],
        "w_in_odd": full2[0][None], "w_out_odd": full2[1][None],
        }
    layered = {"w_ffn_gate": [full1[0], full3[0]], "w_ffn_up": [full1[1], full3[1]],
               "w_ffn_down": [full1[2], full3[2]]}
    off = 0
    small_red = {}
    for nm, v in small_parts.items():
        small_red[nm] = small_sum[off:off + v.size].reshape(v.shape)
        off += v.size
    for nm in ("norm_mix", "norm_ffn", "pool_scale"):
        grads[nm] = small_red[nm].reshape(weights[nm].shape)
    grads["final_norm"] = small_red["final_norm"].reshape(weights["final_norm"].shape)
    grads["w_pool"] = small_red["w_pool"][None]
    grads["w_spatial"] = small_red["w_spatial"][None]
    grads["b_spatial"] = small_red["b_spatial"].reshape(weights["b_spatial"].shape)
    grads["b_conv_ffn"] = small_red["b_conv_ffn"].reshape(weights["b_conv_ffn"].shape)
    grads["conv_a"] = lax.dynamic_slice_in_dim(small_red["conv_a"], chip * conv_a.shape[-1], conv_a.shape[-1], 1)[None]
    grads["sgu_norm"] = lax.dynamic_slice_in_dim(small_red["sgu_norm"], chip * sgu_norm.shape[-1], sgu_norm.shape[-1], 1)
    grads["conv_ffn"] = lax.dynamic_index_in_dim(small_red["conv_ffn"], chip, 1, keepdims=False)

    deltas, new_m, new_v = {}, {}, {}
    for nm, per_layer in layered.items():
        if nm == "w_ffn_down":
            grads[nm], deltas[nm], new_m[nm], new_v[nm] = adamw_layers(
                weights[nm], per_layer, m_in[nm], v_in[nm], name=f"adamw_{nm}")
        else:
            outs = adamw_layers(turned(weights[nm]), per_layer, turned(m_in[nm]), turned(v_in[nm]),
                                name=f"adamw_{nm}")
            grads[nm], deltas[nm], new_m[nm], new_v[nm] = (turned(o) for o in outs)
    for nm in order:
        if nm in layered:
            continue
        w = weights[nm]
        w2 = w[None] if w.ndim == 1 else w
        shp = w2.shape
        d, nm_, nv_ = adamw(w2, grads[nm].reshape(shp), m_in[nm].reshape(shp), v_in[nm].reshape(shp),
                            name=f"adamw_{nm}")
        deltas[nm], new_m[nm], new_v[nm] = d.reshape(w.shape), nm_.reshape(w.shape), nv_.reshape(w.shape)

    return (loss, grad_x, *[grads[n] for n in order], *[deltas[n] for n in order],
            *[new_m[n] for n in order], *[new_v[n] for n in order])
```

```python
import jax
import jax.numpy as jnp
from jax import lax
from jax.experimental import pallas as pl
from jax.experimental.pallas import tpu as pltpu

F32 = jnp.float32
BF16 = jnp.bfloat16
MESH = pl.DeviceIdType.MESH
ANY = pl.BlockSpec(memory_space=pl.ANY)

EPS = 1e-6
POOL_WINDOWS = (2, 4, 8, 16)
GROUP = 128
CHUNK = 128
N_CHIPS = 4
N_DEV = 8
ROW_TILE = 512
HALO = 16
VMEM_LIMIT = 56 * 1024 * 1024

ADAM_LR = 0.001
ADAM_B1 = 0.9
ADAM_B2 = 0.999
ADAM_EPS = 1e-08
ADAM_WD = 0.01
ADAM_STEP = 10


def _params(*sem):
    return pltpu.CompilerParams(dimension_semantics=sem, vmem_limit_bytes=VMEM_LIMIT)


def mm_acc(a, b, res, *, name):
    j_n, s, kj = a.shape
    n = b.shape[-1]
    tm = min(ROW_TILE, s)

    def body(a_ref, b_ref, r_ref, o_ref):
        acc = r_ref[...]
        for j in range(j_n):
            acc = acc + jnp.dot(a_ref[j], b_ref[j], preferred_element_type=F32)
        o_ref[...] = acc

    return pl.pallas_call(
        body, name=name, grid=(s // tm,),
        in_specs=[pl.BlockSpec((j_n, tm, kj), lambda i: (0, i, 0)),
                  pl.BlockSpec((j_n, kj, n), lambda i: (0, 0, 0)),
                  pl.BlockSpec((tm, n), lambda i: (i, 0))],
        out_specs=pl.BlockSpec((tm, n), lambda i: (i, 0)),
        out_shape=jax.ShapeDtypeStruct((s, n), F32),
        compiler_params=_params("parallel"))(a, b, res)


_NT = (((1,), (1,)), ((), ()))
_TN = (((0,), (0,)), ((), ()))


def _back(x, k):
    return pltpu.roll(x, k, 0)


def _fwd(x, k):
    return pltpu.roll(x, x.shape[0] - k, 0)


def _causal_conv(x, w_ref):
    return w_ref[0:1, :] * _back(x, 2) + w_ref[1:2, :] * _back(x, 1) + w_ref[2:3, :] * x


def _causal_conv_t(dy, w_ref):
    return w_ref[2:3, :] * dy + w_ref[1:2, :] * _fwd(dy, 1) + w_ref[0:1, :] * _fwd(dy, 2)


def _gelu(x):
    return 0.5 * x * (1.0 + lax.erf(x * 0.7071067811865476))


def _gelu_grad(x):
    return 0.5 * (1.0 + lax.erf(x * 0.7071067811865476)) + x * jnp.exp(-0.5 * x * x) * 0.3989422804014327


def _colsum(x):
    return jnp.sum(x, axis=0, keepdims=True)


def _halo_specs(n_lead, ts, width, n_tiles):
    hb = ts // HALO
    prev = pl.BlockSpec((n_lead, HALO, width), lambda i: (0, jnp.maximum(i * hb - 1, 0), 0))
    nxt = pl.BlockSpec((n_lead, HALO, width), lambda i: (0, jnp.minimum((i + 1) * hb, n_tiles * hb - 1), 0))
    return prev, nxt


def _pool_fwd(z_ext, g, pos):
    w = POOL_WINDOWS[g]
    zg = z_ext[:, g * GROUP:(g + 1) * GROUP]
    acc = zg
    sh = 1
    while sh < w:
        acc = acc + _back(acc, sh)
        sh *= 2
    return acc[HALO:] / jnp.minimum(pos, float(w)) - zg[HALO:]


def even_layer_fwd(h, gain, win, conv_a, w_pool, pool_scale, wout, gain_next, *, name, pin=None):
    s, d = h.shape
    w = win.shape[-1]
    ts = min(ROW_TILE, s)
    hb = ts // HALO

    def body(h_ref, hp_ref, gain_ref, win_ref, ca_ref, wp_ref, ps_ref, wout_ref, gn_ref, *rest):
        o_ref, xn_ref, r_ref, p_ref, m_ref, hn_ref, rn_ref = rest[-7:]
        i = pl.program_id(0)
        keep = jnp.where(i > 0, 1.0, 0.0)
        h_ext = jnp.concatenate([hp_ref[...], h_ref[...]], axis=0)
        rstd = lax.rsqrt(jnp.mean(h_ext * h_ext, axis=-1, keepdims=True) + EPS)
        xn_ext = (h_ext * rstd * gain_ref[...]).astype(BF16)
        xn_ref[...] = xn_ext[HALO:]
        r_ref[...] = rstd[HALO:]
        p32 = []
        for k in range(4):
            pk = jnp.dot(xn_ext, win_ref[k], preferred_element_type=F32).astype(BF16)
            p_ref[k] = pk[HALO:]
            pk = pk.astype(F32)
            p32.append(jnp.concatenate([pk[:HALO] * keep, pk[HALO:]], axis=0))
        m_ref[:, 0:w] = (p32[0][HALO:] * _causal_conv(p32[1] * p32[2], ca_ref)[HALO:]).astype(BF16)
        pos = (i * ts + lax.broadcasted_iota(jnp.int32, (ts, 1), 0) + 1).astype(F32)
        for g in range(len(POOL_WINDOWS)):
            pooled = _pool_fwd(p32[3], g, pos)
            mixed = jnp.dot(pooled.astype(BF16), wp_ref[g], preferred_element_type=F32)
            cols = slice(g * GROUP, (g + 1) * GROUP)
            m_ref[:, w + g * GROUP:w + (g + 1) * GROUP] = (mixed * ps_ref[:, cols]).astype(BF16)
        out = h_ref[...] + jnp.dot(m_ref[...], wout_ref[...], preferred_element_type=F32)
        o_ref[...] = out
        rstd_n = lax.rsqrt(jnp.mean(out * out, axis=-1, keepdims=True) + EPS)
        hn_ref[...] = (out * rstd_n * gn_ref[...]).astype(BF16)
        rn_ref[...] = rstd_n

    const = lambda shape: pl.BlockSpec(shape, lambda i: (0,) * len(shape))
    row = pl.BlockSpec((ts, d), lambda i: (i, 0))
    col1 = pl.BlockSpec((ts, 1), lambda i: (i, 0))
    return pl.pallas_call(
        body, name=name, grid=(s // ts,),
        in_specs=[row, pl.BlockSpec((HALO, d), lambda i: (jnp.maximum(i * hb - 1, 0), 0)), const((1, d)),
                  const((4, d, w)), const((3, w)), const((4, GROUP, GROUP)), const((1, w)), const((2 * w, d)),
                  const((1, d))] + ([ANY] if pin is not None else []),
        out_specs=[row, row, col1, pl.BlockSpec((4, ts, w), lambda i: (0, i, 0)),
                   pl.BlockSpec((ts, 2 * w), lambda i: (i, 0)), row, col1],
        out_shape=[jax.ShapeDtypeStruct((s, d), F32), jax.ShapeDtypeStruct((s, d), BF16),
                   jax.ShapeDtypeStruct((s, 1), F32), jax.ShapeDtypeStruct((4, s, w), BF16),
                   jax.ShapeDtypeStruct((s, 2 * w), BF16), jax.ShapeDtypeStruct((s, d), BF16),
                   jax.ShapeDtypeStruct((s, 1), F32)],
        compiler_params=_params("parallel"))(
            h, h, gain, win, conv_a, w_pool, pool_scale, wout, gain_next, *([pin] if pin is not None else []))


def even_bwd(proj, mix, xn, dh_bf, wout, conv_a, w_pool, w_pool_t, pool_scale, *, name, pin=None):
    _, s, w = proj.shape
    d = dh_bf.shape[1]
    ts = min(ROW_TILE, s)
    n_t = s // ts
    prev, nxt = _halo_specs(4, ts, w, n_t)
    hb = ts // HALO
    n_ext = ts + HALO

    def body(p_ref, pp_ref, pn_ref, m_ref, x_ref, dh_ref, dhn_ref, wout_ref, ca_ref, wp_ref, wpt_ref, ps_ref, *rest):
        dp_ref, dwi_ref, dwo_ref, dca_ref, dwp_ref, dps_ref = rest[-6:]
        i = pl.program_id(0)

        @pl.when(i == 0)
        def _():
            dwi_ref[...] = jnp.zeros_like(dwi_ref)
            dwo_ref[...] = jnp.zeros_like(dwo_ref)
            dca_ref[...] = jnp.zeros_like(dca_ref)
            dwp_ref[...] = jnp.zeros_like(dwp_ref)
            dps_ref[...] = jnp.zeros_like(dps_ref)

        keep_p = jnp.where(i > 0, 1.0, 0.0)
        keep_n = jnp.where(i < n_t - 1, 1.0, 0.0)
        dwo_ref[...] += lax.dot_general(m_ref[...], dh_ref[...], _TN, preferred_element_type=F32)
        dmix = lax.dot_general(jnp.concatenate([dh_ref[...], dhn_ref[...]], axis=0), wout_ref[...], _NT,
                               preferred_element_type=F32)
        a_b, a_c, a_v = (p_ref[k].astype(F32) for k in range(3))
        cv_ext = jnp.concatenate([pp_ref[1].astype(F32) * pp_ref[2].astype(F32) * keep_p, a_c * a_v], axis=0)
        dy_a = dmix[:ts, 0:w]
        dp_ref[0] = (dy_a * _causal_conv(cv_ext, ca_ref)[HALO:]).astype(BF16)
        dcc = dy_a * a_b
        dca_ref[2:3, :] += _colsum(dcc * cv_ext[HALO:])
        dca_ref[1:2, :] += _colsum(dcc * _back(cv_ext, 1)[HALO:])
        dca_ref[0:1, :] += _colsum(dcc * _back(cv_ext, 2)[HALO:])
        dcc_ext = jnp.concatenate([dcc, dmix[ts:, 0:w] * pn_ref[0].astype(F32) * keep_n], axis=0)
        dcv = _causal_conv_t(dcc_ext, ca_ref)[:ts]
        dp_ref[1] = (dcv * a_v).astype(BF16)
        dp_ref[2] = (dcv * a_c).astype(BF16)
        z_ext = jnp.concatenate([pp_ref[3].astype(F32) * keep_p, p_ref[3].astype(F32)], axis=0)
        pos = (i * ts + lax.broadcasted_iota(jnp.int32, (ts, 1), 0) + 1).astype(F32)
        pos_ext = (i * ts + lax.broadcasted_iota(jnp.int32, (n_ext, 1), 0) + 1).astype(F32)
        for g, win in enumerate(POOL_WINDOWS):
            cols = slice(g * GROUP, (g + 1) * GROUP)
            ycols = slice(w + g * GROUP, w + (g + 1) * GROUP)
            pooled = _pool_fwd(z_ext, g, pos).astype(BF16)
            mixed = jnp.dot(pooled, wp_ref[g], preferred_element_type=F32)
            dy_b = dmix[:ts, ycols]
            dps_ref[:, cols] += _colsum(dy_b * mixed)
            dmixed_ext = jnp.concatenate([dy_b, dmix[ts:, ycols] * keep_n], axis=0) * ps_ref[:, cols]
            dmixed_ext = dmixed_ext.astype(BF16)
            dwp_ref[g] += lax.dot_general(pooled, dmixed_ext[:ts], _TN, preferred_element_type=F32)
            dpooled = jnp.dot(dmixed_ext, wpt_ref[g], preferred_element_type=F32)
            acc = dpooled / jnp.minimum(pos_ext, float(win))
            sh = 1
            while sh < win:
                acc = acc + _fwd(acc, sh)
                sh *= 2
            dp_ref[3, :, cols] = (acc[:ts] - dpooled[:ts]).astype(BF16)
        for k in range(4):
            dwi_ref[k] += lax.dot_general(x_ref[...], dp_ref[k], _TN, preferred_element_type=F32)

    tile4 = pl.BlockSpec((4, ts, w), lambda i: (0, i, 0))
    const = lambda shape: pl.BlockSpec(shape, lambda i: (0,) * len(shape))
    return pl.pallas_call(
        body, name=name, grid=(n_t,),
        in_specs=[tile4, prev, nxt, pl.BlockSpec((ts, 2 * w), lambda i: (i, 0)), pl.BlockSpec((ts, d), lambda i: (i, 0)),
                  pl.BlockSpec((ts, d), lambda i: (i, 0)),
                  pl.BlockSpec((HALO, d), lambda i: (jnp.minimum((i + 1) * hb, n_t * hb - 1), 0)),
                  const((2 * w, d)), const((3, w)), const((4, GROUP, GROUP)), const((4, GROUP, GROUP)), const((1, w))]
        + ([ANY] if pin is not None else []),
        out_specs=[tile4, const((4, d, w)), const((2 * w, d)), const((3, w)), const((4, GROUP, GROUP)), const((1, w))],
        out_shape=[jax.ShapeDtypeStruct((4, s, w), BF16), jax.ShapeDtypeStruct((4, d, w), F32),
                   jax.ShapeDtypeStruct((2 * w, d), F32), jax.ShapeDtypeStruct((3, w), F32),
                   jax.ShapeDtypeStruct((4, GROUP, GROUP), F32), jax.ShapeDtypeStruct((1, w), F32)],
        compiler_params=_params("arbitrary"))(
            proj, proj, proj, mix, xn, dh_bf, dh_bf, wout, conv_a, w_pool, w_pool_t, pool_scale,
            *([pin] if pin is not None else []))


def _ffn_next_halo(ts, f, n_t):
    hb = ts // HALO
    return pl.BlockSpec((None, HALO, f), lambda j, i: (j, jnp.minimum((i + 1) * hb, n_t * hb - 1), 0))


def ffn_in_fwd(hn, wg, wu, cw, cb, *, name, pin=None):
    s, d = hn.shape
    j_n, f, _ = wg.shape
    tm = min(ROW_TILE, s)
    hb = tm // HALO

    def body(x_ref, xp_ref, wg_ref, wu_ref, cw_ref, cb_ref, *rest):
        g_ref, u_ref, gc_ref, a_ref = rest[-4:]
        i, j = pl.program_id(0), pl.program_id(1)
        x_ext = jnp.concatenate([xp_ref[...], x_ref[...]], axis=0)
        g_ext = lax.dot_general(x_ext, wg_ref[j], _NT, preferred_element_type=F32).astype(BF16)
        up = lax.dot_general(x_ref[...], wu_ref[j], _NT, preferred_element_type=F32).astype(BF16)
        g_ref[...] = g_ext[HALO:]
        u_ref[...] = up
        a_ref[...] = _ffn_act(g_ext, up, cw_ref, cb_ref, gc_ref, i)

    whole = pl.BlockSpec((j_n, f, d), lambda i, j: (0, 0, 0))
    tile = pl.BlockSpec((None, tm, f), lambda i, j: (j, i, 0))
    shape = jax.ShapeDtypeStruct((j_n, s, f), BF16)
    return pl.pallas_call(
        body, name=name, grid=(s // tm, j_n),
        in_specs=[pl.BlockSpec((tm, d), lambda i, j: (i, 0)),
                  pl.BlockSpec((HALO, d), lambda i, j: (jnp.maximum(i * hb - 1, 0), 0)),
                  whole, whole,
                  pl.BlockSpec((None, 3, f), lambda i, j: (j, 0, 0)),
                  pl.BlockSpec((None, 1, f), lambda i, j: (j, 0, 0))] + ([ANY] if pin is not None else []),
        out_specs=[tile] * 4, out_shape=[shape] * 4,
        compiler_params=_params("parallel", "parallel"))(hn, hn, wg, wu, cw, cb, *([pin] if pin is not None else []))


def _ffn_act(g_ext, up, cw_ref, cb_ref, gc_ref, i):
    keep = jnp.where(i > 0, 1.0, 0.0)
    g32 = jnp.concatenate([g_ext[:HALO].astype(F32) * keep, g_ext[HALO:].astype(F32)], axis=0)
    gc = (_causal_conv(g32, cw_ref)[HALO:] + cb_ref[...]).astype(BF16)
    gc_ref[...] = gc
    gc = gc.astype(F32)
    return (gc * jax.nn.sigmoid(gc) * up.astype(F32)).astype(BF16)


def ffn_loss_fwd(h, gain, wg, wu, wd, cw, cb, target, final_gain, *, name):
    s, d = h.shape
    j_n, f, _ = wg.shape
    tm = min(ROW_TILE, s)
    hb = tm // HALO

    def body(h_ref, hp_ref, gain_ref, wg_ref, wu_ref, wd_ref, cw_ref, cb_ref, t_ref, fg_ref,
             o_ref, ob_ref, l_ref, dfg_ref, xn_ref, r_ref, g_ref, u_ref, gc_ref, x_s, acc_s):
        i, j = pl.program_id(0), pl.program_id(1)

        @pl.when((i == 0) & (j == 0))
        def _():
            l_ref[...] = jnp.zeros_like(l_ref)
            dfg_ref[...] = jnp.zeros_like(dfg_ref)

        @pl.when(j == 0)
        def _():
            h_ext = jnp.concatenate([hp_ref[...], h_ref[...]], axis=0)
            rstd = lax.rsqrt(jnp.mean(h_ext * h_ext, axis=-1, keepdims=True) + EPS)
            x_s[...] = (h_ext * rstd * gain_ref[...]).astype(BF16)
            xn_ref[...] = x_s[HALO:, :]
            r_ref[...] = rstd[HALO:]
            acc_s[...] = h_ref[...]

        g_ext = lax.dot_general(x_s[...], wg_ref[j], _NT, preferred_element_type=F32).astype(BF16)
        up = lax.dot_general(x_s[HALO:, :], wu_ref[j], _NT, preferred_element_type=F32).astype(BF16)
        g_ref[...] = g_ext[HALO:]
        u_ref[...] = up
        act = _ffn_act(g_ext, up, cw_ref, cb_ref, gc_ref, i)
        acc_s[...] += jnp.dot(act, wd_ref[j], preferred_element_type=F32)

        @pl.when(j == j_n - 1)
        def _():
            x = acc_s[...]
            rstd = lax.rsqrt(jnp.mean(x * x, axis=-1, keepdims=True) + EPS)
            hhat = x * rstd
            err = hhat * fg_ref[...] - t_ref[...]
            l_ref[...] += 0.5 * jnp.sum(jnp.mean(err * err, axis=-1, keepdims=True), axis=0, keepdims=True)
            dy = err * (1.0 / d)
            dfg_ref[...] += _colsum(dy * hhat)
            dyg = dy * fg_ref[...]
            dh = rstd * (dyg - hhat * jnp.mean(dyg * hhat, axis=-1, keepdims=True))
            o_ref[...] = dh
            ob_ref[...] = dh.astype(BF16)

    whole = pl.BlockSpec((j_n, f, d), lambda i, j: (0, 0, 0), pipeline_mode=pl.Buffered(1))
    row = pl.BlockSpec((tm, d), lambda i, j: (i, 0))
    vec = pl.BlockSpec((1, d), lambda i, j: (0, 0))
    tile = pl.BlockSpec((None, tm, f), lambda i, j: (j, i, 0))
    return pl.pallas_call(
        body, name=name, grid=(s // tm, j_n),
        in_specs=[row, pl.BlockSpec((HALO, d), lambda i, j: (jnp.maximum(i * hb - 1, 0), 0)),
                  vec, whole, whole, whole,
                  pl.BlockSpec((None, 3, f), lambda i, j: (j, 0, 0)),
                  pl.BlockSpec((None, 1, f), lambda i, j: (j, 0, 0)), row, vec],
        out_specs=[row, row, pl.BlockSpec((1, 128), lambda i, j: (0, 0)), vec,
                   row, pl.BlockSpec((tm, 1), lambda i, j: (i, 0)), tile, tile, tile],
        out_shape=[jax.ShapeDtypeStruct((s, d), F32), jax.ShapeDtypeStruct((s, d), BF16),
                   jax.ShapeDtypeStruct((1, 128), F32), jax.ShapeDtypeStruct((1, d), F32),
                   jax.ShapeDtypeStruct((s, d), BF16), jax.ShapeDtypeStruct((s, 1), F32)]
        + [jax.ShapeDtypeStruct((j_n, s, f), BF16)] * 3,
        scratch_shapes=[pltpu.VMEM((HALO + tm, d), BF16), pltpu.VMEM((tm, d), F32)],
        compiler_params=_params("arbitrary", "arbitrary"))(h, h, gain, wg, wu, wd, cw, cb, target, final_gain)


def ffn_bwd_a(dh_bf, hn, g, up, gc, wd, cw, *, name, pin=None):
    s, d = hn.shape
    j_n, _, f = g.shape
    tm = min(ROW_TILE, s)
    n_t = s // tm
    hb = tm // HALO

    def body(dh_ref, dhn_ref, x_ref, g_ref, gc_ref, gcn_ref, u_ref, un_ref, wd_ref, cw_ref, *rest):
        dg_ref, du_ref, dwg_ref, dwu_ref, dwd_ref, dcw_ref, dcb_ref = rest[-7:]
        i = pl.program_id(1)

        @pl.when(i == 0)
        def _():
            for r in (dwg_ref, dwu_ref, dwd_ref, dcw_ref, dcb_ref):
                r[...] = jnp.zeros_like(r)

        keep_n = jnp.where(i < n_t - 1, 1.0, 0.0)
        dh = dh_ref[...]
        dact = lax.dot_general(jnp.concatenate([dh, dhn_ref[...]], axis=0), wd_ref[...], _NT,
                               preferred_element_type=F32)
        dact = jnp.concatenate([dact[:tm], dact[tm:] * keep_n], axis=0)
        gc_ext = jnp.concatenate([gc_ref[...], gcn_ref[...]], axis=0).astype(F32)
        sig = jax.nn.sigmoid(gc_ext)
        silu = gc_ext * sig
        up_ext = jnp.concatenate([u_ref[...], un_ref[...]], axis=0).astype(F32)
        act = (silu * up_ext)[:tm].astype(BF16)
        dwd_ref[...] += lax.dot_general(act, dh, _TN, preferred_element_type=F32)
        dup = (dact * silu)[:tm].astype(BF16)
        du_ref[...] = dup
        dgc = dact * up_ext * (sig + silu * (1.0 - sig))
        dgc_1, dgc_2 = _fwd(dgc, 1), _fwd(dgc, 2)
        dg = (cw_ref[2:3, :] * dgc + cw_ref[1:2, :] * dgc_1 + cw_ref[0:1, :] * dgc_2)[:tm].astype(BF16)
        dg_ref[...] = dg
        x = x_ref[...]
        dwg_ref[...] += lax.dot_general(dg, x, _TN, preferred_element_type=F32)
        dwu_ref[...] += lax.dot_general(dup, x, _TN, preferred_element_type=F32)
        g32 = g_ref[...].astype(F32)
        dcb_ref[...] += _colsum(dgc[:tm])
        dcw_ref[2:3, :] += _colsum(dgc[:tm] * g32)
        dcw_ref[1:2, :] += _colsum(dgc_1[:tm] * g32)
        dcw_ref[0:1, :] += _colsum(dgc_2[:tm] * g32)

    rows = pl.BlockSpec((tm, d), lambda j, i: (i, 0))
    rows_next = pl.BlockSpec((HALO, d), lambda j, i: (jnp.minimum((i + 1) * hb, n_t * hb - 1), 0))
    tile = pl.BlockSpec((None, tm, f), lambda j, i: (j, i, 0))
    nxt = _ffn_next_halo(tm, f, n_t)
    per_j = lambda r, c: pl.BlockSpec((None, r, c), lambda j, i: (j, 0, 0))
    return pl.pallas_call(
        body, name=name, grid=(j_n, n_t),
        in_specs=[rows, rows_next, rows, tile, tile, nxt, tile, nxt, per_j(f, d), per_j(3, f)]
        + ([ANY] if pin is not None else []),
        out_specs=[tile, tile, per_j(f, d), per_j(f, d), per_j(f, d), per_j(3, f), per_j(1, f)],
        out_shape=[jax.ShapeDtypeStruct((j_n, s, f), BF16), jax.ShapeDtypeStruct((j_n, s, f), BF16),
                   jax.ShapeDtypeStruct((j_n, f, d), F32), jax.ShapeDtypeStruct((j_n, f, d), F32),
                   jax.ShapeDtypeStruct((j_n, f, d), F32), jax.ShapeDtypeStruct((j_n, 3, f), F32),
                   jax.ShapeDtypeStruct((j_n, 1, f), F32)],
        compiler_params=_params("parallel", "arbitrary"))(
            dh_bf, dh_bf, hn, g, gc, gc, up, up, wd, cw, *([pin] if pin is not None else []))


def dx_rms_bwd(pairs, h, gain, rstd, dres, *, name, pin=None):
    j_n, s, f = pairs[0][0].shape
    d = h.shape[1]
    tm = min(ROW_TILE, s)
    n_p = len(pairs)
    dims = [_NT if w.shape[1:] == (d, f) else (((1,), (0,)), ((), ())) for _, w in pairs]

    def body(*refs):
        dy_refs, w_refs = refs[:n_p], refs[n_p:2 * n_p]
        h_ref, g_ref, r_ref, dr_ref = refs[2 * n_p:2 * n_p + 4]
        o_ref, ob_ref, dgain_ref = refs[-3:]

        @pl.when(pl.program_id(0) == 0)
        def _():
            dgain_ref[...] = jnp.zeros_like(dgain_ref)
        dx = jnp.zeros((tm, d), F32)
        for j in range(j_n):
            for p in range(n_p):
                dx = dx + lax.dot_general(dy_refs[p][j], w_refs[p][j], dims[p], preferred_element_type=F32)
        rstd_v = r_ref[...]
        hhat = h_ref[...] * rstd_v
        dgain_ref[...] += _colsum(dx * hhat)
        dxg = dx * g_ref[...]
        dh = dr_ref[...] + rstd_v * (dxg - hhat * jnp.mean(dxg * hhat, axis=-1, keepdims=True))
        o_ref[...] = dh
        ob_ref[...] = dh.astype(BF16)

    tile4 = pl.BlockSpec((j_n, tm, f), lambda i: (0, i, 0))
    whole = [pl.BlockSpec(w.shape, lambda i: (0, 0, 0), pipeline_mode=pl.Buffered(1)) for _, w in pairs]
    row = pl.BlockSpec((tm, d), lambda i: (i, 0))
    vec = pl.BlockSpec((1, d), lambda i: (0, 0))
    return pl.pallas_call(
        body, name=name, grid=(s // tm,),
        in_specs=[tile4] * n_p + whole + [row, vec, pl.BlockSpec((tm, 1), lambda i: (i, 0)), row]
        + ([ANY] if pin is not None else []),
        out_specs=[row, row, vec],
        out_shape=[jax.ShapeDtypeStruct((s, d), F32), jax.ShapeDtypeStruct((s, d), BF16),
                   jax.ShapeDtypeStruct((1, d), F32)],
        compiler_params=_params("arbitrary"))(
            *[p[0] for p in pairs], *[p[1] for p in pairs], h, gain, rstd, dres, *([pin] if pin is not None else []))


def _sgu_gate(vn_bf, ws_ref, bs_ref, h, rows):
    tri = lax.broadcasted_iota(jnp.int32, (CHUNK, CHUNK), 0) >= lax.broadcasted_iota(jnp.int32, (CHUNK, CHUNK), 1)
    ws = jnp.where(tri, ws_ref[h], 0.0).astype(BF16)
    cols = slice((h % 4) * GROUP, (h % 4 + 1) * GROUP)
    return ws, jnp.dot(ws, vn_bf[h // 4][rows, cols], preferred_element_type=F32) + bs_ref[h]


def odd_layer_fwd(h, gain, win, sgu_norm, w_spatial, b_spatial, wout, *, name):
    s, d = h.shape
    w = win.shape[-1]
    ts = min(ROW_TILE, s)
    n_heads = w_spatial.shape[0]

    def body(h_ref, gain_ref, win_ref, n_ref, ws_ref, bs_ref, wout_ref, o_ref, xn_ref, r_ref, p_ref, m_ref, rv_ref):
        x = h_ref[...]
        rstd_x = lax.rsqrt(jnp.mean(x * x, axis=-1, keepdims=True) + EPS)
        xn = (x * rstd_x * gain_ref[...]).astype(BF16)
        xn_ref[...] = xn
        r_ref[...] = rstd_x
        for k in range(4):
            p_ref[k] = jnp.dot(xn, win_ref[k], preferred_element_type=F32).astype(BF16)
        v = [_gelu(p_ref[2].astype(F32)), _gelu(p_ref[3].astype(F32))]
        ms = (jnp.sum(v[0] * v[0], axis=-1, keepdims=True) + jnp.sum(v[1] * v[1], axis=-1, keepdims=True)) / (2 * w)
        rstd = lax.rsqrt(ms + EPS)
        rv_ref[...] = rstd
        vn = [(v[k] * rstd * n_ref[:, k * w:(k + 1) * w]).astype(BF16) for k in range(2)]
        for hd in range(n_heads):
            cols = slice((hd % 4) * GROUP, (hd % 4 + 1) * GROUP)
            for c in range(ts // CHUNK):
                rows = slice(c * CHUNK, (c + 1) * CHUNK)
                _, gate = _sgu_gate(vn, ws_ref, bs_ref, hd, rows)
                u = _gelu(p_ref[hd // 4, rows, cols].astype(F32))
                m_ref[rows, hd * GROUP:(hd + 1) * GROUP] = (u * gate).astype(BF16)
        o_ref[...] = x + jnp.dot(m_ref[...], wout_ref[...], preferred_element_type=F32)

    const = lambda shape: pl.BlockSpec(shape, lambda i: (0,) * len(shape))
    row = pl.BlockSpec((ts, d), lambda i: (i, 0))
    col1 = pl.BlockSpec((ts, 1), lambda i: (i, 0))
    return pl.pallas_call(
        body, name=name, grid=(s // ts,),
        in_specs=[row, const((1, d)), const((4, d, w)), const((1, 2 * w)),
                  const((n_heads, CHUNK, CHUNK)), const((n_heads, CHUNK, 1)), const((2 * w, d))],
        out_specs=[row, row, col1, pl.BlockSpec((4, ts, w), lambda i: (0, i, 0)),
                   pl.BlockSpec((ts, 2 * w), lambda i: (i, 0)), col1],
        out_shape=[jax.ShapeDtypeStruct((s, d), F32), jax.ShapeDtypeStruct((s, d), BF16),
                   jax.ShapeDtypeStruct((s, 1), F32), jax.ShapeDtypeStruct((4, s, w), BF16),
                   jax.ShapeDtypeStruct((s, 2 * w), BF16), jax.ShapeDtypeStruct((s, 1), F32)],
        compiler_params=_params("parallel"))(h, gain, win, sgu_norm, w_spatial, b_spatial, wout)


def sgu_bwd(p, mix, xn, dh_bf, wout, rstd, sgu_norm, w_spatial, b_spatial, *, name, pin=None):
    _, s, w = p.shape
    d = dh_bf.shape[1]
    ts = min(ROW_TILE, s)
    n_heads = w_spatial.shape[0]

    def body(p_ref, m_ref, x_ref, dh_ref, wout_ref, r_ref, n_ref, ws_ref, bs_ref, *rest):
        dp_ref, dwi_ref, dwo_ref, dn_ref, dws_ref, dbs_ref, dvn_ref, dm_ref = rest[-8:]

        @pl.when(pl.program_id(0) == 0)
        def _():
            dwi_ref[...] = jnp.zeros_like(dwi_ref)
            dwo_ref[...] = jnp.zeros_like(dwo_ref)
            dn_ref[...] = jnp.zeros_like(dn_ref)
            dws_ref[...] = jnp.zeros_like(dws_ref)
            dbs_ref[...] = jnp.zeros_like(dbs_ref)

        dwo_ref[...] += lax.dot_general(m_ref[...], dh_ref[...], _TN, preferred_element_type=F32)
        dm_ref[...] = lax.dot_general(dh_ref[...], wout_ref[...], _NT, preferred_element_type=F32)
        rstd_v = r_ref[...]
        vhat = [_gelu(p_ref[2 + k].astype(F32)) * rstd_v for k in range(2)]
        vn = [(vhat[k] * n_ref[:, k * w:(k + 1) * w]).astype(BF16) for k in range(2)]
        tri = lax.broadcasted_iota(jnp.int32, (CHUNK, CHUNK), 0) >= lax.broadcasted_iota(jnp.int32, (CHUNK, CHUNK), 1)
        for h in range(n_heads):
            cols = slice((h % 4) * GROUP, (h % 4 + 1) * GROUP)
            ocols = slice(h * GROUP, (h + 1) * GROUP)
            for c in range(ts // CHUNK):
                rows = slice(c * CHUNK, (c + 1) * CHUNK)
                ws, gate = _sgu_gate(vn, ws_ref, bs_ref, h, rows)
                pu = p_ref[h // 4, rows, cols].astype(F32)
                dm = dm_ref[rows, ocols]
                dp_ref[h // 4, rows, cols] = (dm * gate * _gelu_grad(pu)).astype(BF16)
                dgate = dm * _gelu(pu)
                dbs_ref[h] += jnp.sum(dgate, axis=-1, keepdims=True)
                dgate_bf = dgate.astype(BF16)
                dws = lax.dot_general(dgate_bf, vn[h // 4][rows, cols], _NT, preferred_element_type=F32)
                dws_ref[h] += jnp.where(tri, dws, 0.0)
                dvn_ref[rows, ocols] = lax.dot_general(ws, dgate_bf, _TN, preferred_element_type=F32)
        for k in range(2):
            kc = slice(k * w, (k + 1) * w)
            dvn = dvn_ref[:, kc]
            dn_ref[:, kc] += _colsum(dvn * vhat[k])
        dvh = [dvn_ref[:, k * w:(k + 1) * w] * n_ref[:, k * w:(k + 1) * w] for k in range(2)]
        dot = (jnp.sum(dvh[0] * vhat[0], axis=-1, keepdims=True)
               + jnp.sum(dvh[1] * vhat[1], axis=-1, keepdims=True)) / (2 * w)
        for k in range(2):
            dv = rstd_v * (dvh[k] - vhat[k] * dot)
            dp_ref[2 + k] = (dv * _gelu_grad(p_ref[2 + k].astype(F32))).astype(BF16)
        for k in range(4):
            dwi_ref[k] += lax.dot_general(x_ref[...], dp_ref[k], _TN, preferred_element_type=F32)

    const = lambda shape: pl.BlockSpec(shape, lambda i: (0,) * len(shape))
    tile4 = pl.BlockSpec((4, ts, w), lambda i: (0, i, 0))
    return pl.pallas_call(
        body, name=name, grid=(s // ts,),
        in_specs=[tile4, pl.BlockSpec((ts, 2 * w), lambda i: (i, 0)), pl.BlockSpec((ts, d), lambda i: (i, 0)),
                  pl.BlockSpec((ts, d), lambda i: (i, 0)), const((2 * w, d)), pl.BlockSpec((ts, 1), lambda i: (i, 0)),
                  const((1, 2 * w)), const((n_heads, CHUNK, CHUNK)), const((n_heads, CHUNK, 1))]
        + ([ANY] if pin is not None else []),
        out_specs=[tile4, const((4, d, w)), const((2 * w, d)), const((1, 2 * w)), const((n_heads, CHUNK, CHUNK)),
                   const((n_heads, CHUNK, 1))],
        out_shape=[jax.ShapeDtypeStruct((4, s, w), BF16), jax.ShapeDtypeStruct((4, d, w), F32),
                   jax.ShapeDtypeStruct((2 * w, d), F32),
                   jax.ShapeDtypeStruct((1, 2 * w), F32),
                   jax.ShapeDtypeStruct((n_heads, CHUNK, CHUNK), F32),
                   jax.ShapeDtypeStruct((n_heads, CHUNK, 1), F32)],
        scratch_shapes=[pltpu.VMEM((ts, 2 * w), F32), pltpu.VMEM((ts, 2 * w), F32)],
        compiler_params=_params("arbitrary"))(
            p, mix, xn, dh_bf, wout, rstd, sgu_norm, w_spatial, b_spatial, *([pin] if pin is not None else []))


def _row_tile(rows):
    if rows <= ROW_TILE:
        return rows
    for t in (512, 384, 352, 256, 128, 64, 32, 16, 8):
        if rows % t == 0:
            return t
    return rows


def adamw(w, g, m, v, *, name):
    shape = w.shape
    cols = shape[-1]
    rows = w.size // cols
    w2, g2, m2, v2 = (a.reshape(rows, cols) for a in (w, g, m, v))
    tr = _row_tile(rows)
    bc1 = 1.0 - ADAM_B1 ** ADAM_STEP
    bc2 = 1.0 - ADAM_B2 ** ADAM_STEP

    def body(w_ref, g_ref, m_ref, v_ref, d_ref, nm_ref, nv_ref):
        grad = g_ref[...]
        m_new = ADAM_B1 * m_ref[...] + (1.0 - ADAM_B1) * grad
        v_new = ADAM_B2 * v_ref[...] + (1.0 - ADAM_B2) * (grad * grad)
        nm_ref[...] = m_new
        nv_ref[...] = v_new
        d_ref[...] = -ADAM_LR * ((m_new / bc1) / (jnp.sqrt(v_new / bc2) + ADAM_EPS) + ADAM_WD * w_ref[...])

    spec = pl.BlockSpec((tr, cols), lambda i: (i, 0))
    outs = pl.pallas_call(
        body, name=name, grid=(rows // tr,),
        in_specs=[spec] * 4, out_specs=[spec] * 3,
        out_shape=[jax.ShapeDtypeStruct((rows, cols), F32)] * 3,
        compiler_params=_params("parallel"))(w2, g2, m2, v2)
    return tuple(o.reshape(shape) for o in outs)


def adamw_layers(w, grads, m, v, *, name):
    n_l, rows, cols = w.shape
    tr = _row_tile(rows)
    bc1 = 1.0 - ADAM_B1 ** ADAM_STEP
    bc2 = 1.0 - ADAM_B2 ** ADAM_STEP
    outs = None
    for l in range(n_l):
        def body(w_ref, g_ref, m_ref, v_ref, *rest):
            go_ref, d_ref, nm_ref, nv_ref = rest[-4:]
            grad = g_ref[...]
            m_new = ADAM_B1 * m_ref[...] + (1.0 - ADAM_B1) * grad
            v_new = ADAM_B2 * v_ref[...] + (1.0 - ADAM_B2) * (grad * grad)
            go_ref[...] = grad
            nm_ref[...] = m_new
            nv_ref[...] = v_new
            d_ref[...] = -ADAM_LR * ((m_new / bc1) / (jnp.sqrt(v_new / bc2) + ADAM_EPS) + ADAM_WD * w_ref[...])

        layer = pl.BlockSpec((None, tr, cols), lambda i, l=l: (l, i, 0))
        prev = list(outs) if outs is not None else []
        outs = pl.pallas_call(
            body, name=f"{name}{l}", grid=(rows // tr,),
            in_specs=[layer, pl.BlockSpec((tr, cols), lambda i: (i, 0)), layer, layer] + [ANY] * len(prev),
            out_specs=[layer] * 4,
            out_shape=[jax.ShapeDtypeStruct(w.shape, F32)] * 4,
            input_output_aliases={4 + k: k for k in range(len(prev))},
            compiler_params=_params("parallel"))(w, grads[l], m, v, *prev)
    return tuple(outs)


def _place():
    return lax.axis_index("x"), lax.axis_index("y"), lax.axis_index("c")


def _other_chips(x, y):
    return [(1 - x, y), (x, 1 - y), (1 - x, 1 - y)]


HBM = pl.BlockSpec(memory_space=pltpu.HBM)
SEM = pl.BlockSpec(memory_space=pltpu.SEMAPHORE)
DATAFLOW = pltpu.SideEffectType.DATAFLOW_SIDE_EFFECTING


def _in_hbm(a):
    return pltpu.with_memory_space_constraint(a, pltpu.HBM)


def cast_into_slot(w, chip, *, l=None, name, pin=None):
    rows, cols = w.shape[-2:]
    tr = _row_tile(rows)

    def body(chip_ref, w_ref, *rest):
        rest[-1][...] = w_ref[...].astype(BF16)

    in_spec = (pl.BlockSpec((tr, cols), lambda i, chip_ref: (i, 0)) if l is None
               else pl.BlockSpec((None, tr, cols), lambda i, chip_ref: (l, i, 0)))
    return pl.pallas_call(
        body, name=name,
        grid_spec=pltpu.PrefetchScalarGridSpec(
            num_scalar_prefetch=1, grid=(rows // tr,), in_specs=[in_spec] + ([ANY] if pin is not None else []),
            out_specs=pl.BlockSpec((None, tr, cols), lambda i, chip_ref: (chip_ref[0], i, 0))),
        out_shape=jax.ShapeDtypeStruct((N_CHIPS, rows, cols), BF16),
        compiler_params=_params("parallel"))(chip, w, *([pin] if pin is not None else []))


def _half(ref, slot, c):
    half = ref.shape[1] // 2
    return ref.at[slot, pl.ds(c * half, half), :]


def gather_start(groups, smalls, *, name):
    flat = [b for g in groups for b in g]
    n_b, n_s, n_g = len(flat), len(smalls), len(groups)
    n_sem = 2 * n_g + (2 if n_s else 0)

    def body(*refs):
        bufs, small_refs = refs[:n_b], refs[n_b:n_b + n_s]
        sems = refs[n_b + n_s:n_b + n_s + n_sem]
        token = refs[-1]
        x, y, c = _place()
        me = 2 * x + y
        chips = _other_chips(x, y)
        for si in range(n_s):
            piece = small_refs[si].at[me]
            for k, (px, py) in enumerate(chips):
                pltpu.make_async_remote_copy(
                    src_ref=piece, dst_ref=piece,
                    send_sem=sems[2 * n_g].at[3 * si + k], recv_sem=sems[2 * n_g + 1].at[3 * si + k],
                    device_id=(px, py, c), device_id_type=MESH).start()
        t = 0
        for gi, group in enumerate(groups):
            for ti in range(len(group)):
                piece = _half(bufs[t], me, c)
                t += 1
                for k, (px, py) in enumerate(chips):
                    pltpu.make_async_remote_copy(
                        src_ref=piece, dst_ref=piece,
                        send_sem=sems[2 * gi].at[3 * ti + k], recv_sem=sems[2 * gi + 1].at[3 * ti + k],
                        device_id=(px, py, c), device_id_type=MESH).start()
        token[...] = jnp.zeros_like(token)

    sem_shapes = []
    for group in groups:
        sem_shapes += [pltpu.SemaphoreType.DMA((3 * len(group),))] * 2
    if n_s:
        sem_shapes += [pltpu.SemaphoreType.DMA((3 * n_s,))] * 2
    arrays = flat + list(smalls)
    res = pl.pallas_call(
        body, name=name,
        out_shape=tuple(sem_shapes) + tuple(pltpu.HBM(a.shape, a.dtype) for a in arrays)
        + (jax.ShapeDtypeStruct((8, 128), F32),),
        in_specs=[HBM] * len(arrays),
        out_specs=tuple([SEM] * n_sem + [HBM] * len(arrays) + [pl.BlockSpec(memory_space=pltpu.VMEM)]),
        input_output_aliases={i: n_sem + i for i in range(len(arrays))},
        compiler_params=pltpu.CompilerParams(has_side_effects=DATAFLOW))(*[_in_hbm(a) for a in arrays])
    sems, thru, token = res[:n_sem], res[n_sem:-1], res[-1]
    out_groups, t = [], 0
    for group in groups:
        out_groups.append(list(thru[t:t + len(group)]))
        t += len(group)
    return sems, out_groups, list(thru[n_b:]), token


def gather_wait(bufs, send, recv, after, *, name, smalls=(), small_send=None, small_recv=None):
    n_b, n_s = len(bufs), len(smalls)
    arrays = list(bufs) + list(smalls)
    sem_ops = [send, recv] + ([small_send, small_recv] if n_s else [])

    def body(*refs):
        buf_refs, small_refs = refs[:n_b], refs[n_b:n_b + n_s]
        sems = refs[n_b + n_s:n_b + n_s + len(sem_ops)]
        x, y, c = _place()
        me = 2 * x + y
        chips = _other_chips(x, y)
        for ti in range(n_b):
            for k, (px, py) in enumerate(chips):
                done = pltpu.make_async_remote_copy(
                    src_ref=_half(buf_refs[ti], me, c), dst_ref=_half(buf_refs[ti], 2 * px + py, c),
                    send_sem=sems[0].at[3 * ti + k], recv_sem=sems[1].at[3 * ti + k],
                    device_id=(px, py, c), device_id_type=MESH)
                done.wait_send()
                done.wait_recv()
        for si in range(n_s):
            for k, (px, py) in enumerate(chips):
                done = pltpu.make_async_remote_copy(
                    src_ref=small_refs[si].at[me], dst_ref=small_refs[si].at[2 * px + py],
                    send_sem=sems[2].at[3 * si + k], recv_sem=sems[3].at[3 * si + k],
                    device_id=(px, py, c), device_id_type=MESH)
                done.wait_send()
                done.wait_recv()

    res = pl.pallas_call(
        body, name=name,
        out_shape=tuple(pltpu.HBM(a.shape, a.dtype) for a in arrays),
        in_specs=[HBM] * len(arrays) + [SEM] * len(sem_ops) + [ANY],
        out_specs=tuple([HBM] * len(arrays)),
        input_output_aliases={i: i for i in range(len(arrays))},
        compiler_params=pltpu.CompilerParams(has_side_effects=DATAFLOW))(*arrays, *sem_ops, after)
    return list(res[:n_b]), list(res[n_b:])


def gather_forward(bufs, *, name):
    n = len(bufs)

    def body(*refs):
        ins, outs = refs[:n], refs[n:2 * n]
        send_sems, recv_sems = refs[2 * n:]
        x, y, c = _place()
        chips = _other_chips(x, y)
        for t in range(n):
            for k, (px, py) in enumerate(chips):
                pltpu.make_async_remote_copy(
                    src_ref=_half(ins[t], 2 * px + py, c), dst_ref=_half(outs[t], 2 * px + py, c),
                    send_sem=send_sems.at[3 * t + k], recv_sem=recv_sems.at[3 * t + k],
                    device_id=(x, y, 1 - c), device_id_type=MESH).start()
        for t in range(n):
            for k, (px, py) in enumerate(chips):
                done = pltpu.make_async_remote_copy(
                    src_ref=_half(ins[t], 2 * px + py, c), dst_ref=_half(outs[t], 2 * px + py, 1 - c),
                    send_sem=send_sems.at[3 * t + k], recv_sem=recv_sems.at[3 * t + k],
                    device_id=(x, y, 1 - c), device_id_type=MESH)
                done.wait_send()
                done.wait_recv()

    return pl.pallas_call(
        body, name=name, in_specs=[ANY] * n, out_specs=[ANY] * n,
        out_shape=[jax.ShapeDtypeStruct(a.shape, a.dtype) for a in bufs],
        input_output_aliases={i: i for i in range(n)},
        scratch_shapes=[pltpu.SemaphoreType.DMA((3 * n,)), pltpu.SemaphoreType.DMA((3 * n,))],
        compiler_params=pltpu.CompilerParams(has_side_effects=True))(*bufs)


def sum_stage_a(grad, recv, place, wire, *, name):
    j_n, half, cols = recv.shape

    def body(place_ref, g_ref, r_ref, o_ref, ob_ref):
        acc = g_ref[...] + r_ref[...]
        ob_ref[...] = acc.astype(wire)

        @pl.when(pl.program_id(0) == place_ref[0])
        def _():
            o_ref[...] = acc

    blk = (None, half, cols)
    return pl.pallas_call(
        body, name=name,
        grid_spec=pltpu.PrefetchScalarGridSpec(
            num_scalar_prefetch=1, grid=(j_n,),
            in_specs=[pl.BlockSpec(blk, lambda j, place_ref: (j, place_ref[1], 0)),
                      pl.BlockSpec(blk, lambda j, place_ref: (j, 0, 0))],
            out_specs=[pl.BlockSpec((half, cols), lambda j, place_ref: (0, 0)),
                       pl.BlockSpec(blk, lambda j, place_ref: (j, 0, 0))]),
        out_shape=[jax.ShapeDtypeStruct((half, cols), F32), jax.ShapeDtypeStruct(recv.shape, wire)],
        compiler_params=_params("arbitrary"))(place, grad, recv)


def _stage_a_copies(srcs, lands, x, y, c):
    out = []
    for src, land in zip(srcs, lands):
        half = src.shape[1] // 2
        out.append((src.at[:, pl.ds((1 - c) * half, half), :], land, (x, y, 1 - c)))
    return out


def _stage_b_copies(srcs, lands, x, y, c):
    out = []
    for src, land in zip(srcs, lands):
        for k, (px, py) in enumerate(_other_chips(x, y)):
            out.append((src.at[2 * px + py], land.at[k], (px, py, c)))
    return out


def _forward_copies(bufs, _, x, y, c):
    out = []
    for buf in bufs:
        for px, py in _other_chips(x, y):
            out.append((_half(buf, 2 * px + py, c), _half(buf, 2 * px + py, c), (x, y, 1 - c)))
    return out


def _stage_c_copies(fulls, _, x, y, c):
    out = []
    for full in fulls:
        half = full.shape[0] // 2
        mine = full.at[pl.ds(c * half, half), :]
        out.append((mine, mine, (x, y, 1 - c)))
    return out


def split_start(srcs, lands, copies, *, name):
    n, n_all = len(srcs), len(srcs) + len(lands)
    n_c = len(copies(srcs, lands, 0, 0, 0))

    def body(*refs):
        src_refs, land_refs = refs[:n], refs[n:n_all]
        send_sems, recv_sems = refs[n_all], refs[n_all + 1]
        token = refs[-1]
        x, y, c = _place()
        for k, (src, dst, target) in enumerate(copies(src_refs, land_refs, x, y, c)):
            pltpu.make_async_remote_copy(src_ref=src, dst_ref=dst, send_sem=send_sems.at[k], recv_sem=recv_sems.at[k],
                                         device_id=target, device_id_type=MESH).start()
        token[...] = jnp.zeros_like(token)

    arrays = list(srcs) + list(lands)
    res = pl.pallas_call(
        body, name=name,
        out_shape=(pltpu.SemaphoreType.DMA((n_c,)), pltpu.SemaphoreType.DMA((n_c,)))
        + tuple(pltpu.HBM(a.shape, a.dtype) for a in arrays) + (jax.ShapeDtypeStruct((8, 128), F32),),
        in_specs=[HBM] * n_all,
        out_specs=tuple([SEM, SEM] + [HBM] * n_all + [pl.BlockSpec(memory_space=pltpu.VMEM)]),
        input_output_aliases={i: 2 + i for i in range(n_all)},
        compiler_params=pltpu.CompilerParams(has_side_effects=DATAFLOW))(*[_in_hbm(a) for a in arrays])
    return res[0], res[1], list(res[2:2 + n]), list(res[2 + n:2 + n_all]), res[-1]


def split_wait(srcs, lands, send, recv, copies, after, *, name):
    n, n_all = len(srcs), len(srcs) + len(lands)

    def body(*refs):
        src_refs, land_refs = refs[:n], refs[n:n_all]
        send_sems, recv_sems = refs[n_all], refs[n_all + 1]
        x, y, c = _place()
        for k, (src, dst, target) in enumerate(copies(src_refs, land_refs, x, y, c)):
            done = pltpu.make_async_remote_copy(src_ref=src, dst_ref=dst, send_sem=send_sems.at[k],
                                                recv_sem=recv_sems.at[k], device_id=target, device_id_type=MESH)
            done.wait_send()
            done.wait_recv()

    arrays = list(srcs) + list(lands)
    res = pl.pallas_call(
        body, name=name,
        out_shape=tuple(pltpu.HBM(a.shape, a.dtype) for a in arrays),
        in_specs=[HBM] * n_all + [SEM, SEM, ANY],
        out_specs=tuple([HBM] * n_all),
        input_output_aliases={i: i for i in range(n_all)},
        compiler_params=pltpu.CompilerParams(has_side_effects=DATAFLOW))(*arrays, send, recv, after)
    return list(res[:n]), list(res[n:])


def sum_stage_b(part, recv, place, *, name):
    half, cols = part.shape

    def body(place_ref, p_ref, r_ref, o_ref):
        acc = p_ref[...]
        for k in range(3):
            acc = acc + r_ref[k].astype(F32)
        o_ref[...] = acc

    return pl.pallas_call(
        body, name=name,
        grid_spec=pltpu.PrefetchScalarGridSpec(
            num_scalar_prefetch=1, grid=(1,),
            in_specs=[pl.BlockSpec((half, cols), lambda i, place_ref: (0, 0)),
                      pl.BlockSpec((3, half, cols), lambda i, place_ref: (0, 0, 0))],
            out_specs=pl.BlockSpec((half, cols), lambda i, place_ref: (place_ref[1], 0))),
        out_shape=jax.ShapeDtypeStruct((2 * half, cols), F32),
        compiler_params=_params("arbitrary"))(place, part, recv)


def gather_chip_blocks(slots, *, name):
    def body(in_ref, out_ref, send_sems, recv_sems):
        x, y, c = _place()
        me = 2 * x + y
        chips = _other_chips(x, y)
        for k, (px, py) in enumerate(chips):
            pltpu.make_async_remote_copy(
                src_ref=in_ref.at[me], dst_ref=out_ref.at[me],
                send_sem=send_sems.at[k], recv_sem=recv_sems.at[k],
                device_id=(px, py, c), device_id_type=MESH).start()
        for k, (px, py) in enumerate(chips):
            done = pltpu.make_async_remote_copy(
                src_ref=in_ref.at[me], dst_ref=out_ref.at[2 * px + py],
                send_sem=send_sems.at[k], recv_sem=recv_sems.at[k],
                device_id=(px, py, c), device_id_type=MESH)
            done.wait_send()
            done.wait_recv()

    return pl.pallas_call(
        body, name=name, in_specs=[ANY], out_specs=ANY,
        out_shape=jax.ShapeDtypeStruct(slots.shape, slots.dtype),
        input_output_aliases={0: 0},
        scratch_shapes=[pltpu.SemaphoreType.DMA((3,)), pltpu.SemaphoreType.DMA((3,))],
        compiler_params=pltpu.CompilerParams(has_side_effects=True))(slots)


def _ffn_bwd(dh, dh_bf, h, gain, saved, wg, wu, wd, cw, cb, place, l, pin):
    hn, rstd, g, up, gc = saved
    dg, dup, dwg, dwu, dwd, dcw, dcb = ffn_bwd_a(dh_bf, hn, g, up, gc, wd, cw, name=f"ffn{l}_bwd_a", pin=pin)
    red = _reduce_a_begin([(dwg, BF16), (dwu, BF16), (dwd, BF16)], tag=f"f{l}")
    dh_in, dh_in_bf, dgain = dx_rms_bwd([(dg, wg), (dup, wu)], h, gain, rstd, dh, name=f"ffn{l}_bwd_b",
                                        pin=red[-1])
    red = _reduce_b_begin(red, place, dh_in_bf, tag=f"f{l}")
    return dh_in, dh_in_bf, (dcw, dcb, dgain), red


def _reduce_a_begin(grads, *, tag):
    lands = [lax.empty((g.shape[0], g.shape[1] // 2, g.shape[2]), F32) for g, _ in grads]
    send, recv, srcs, lands, token = split_start([g for g, _ in grads], lands, _stage_a_copies,
                                                 name=f"reduce_a_start_{tag}")
    return [w for _, w in grads], send, recv, srcs, lands, token


def _reduce_b_begin(state, place, after, *, tag):
    wires, send, recv, srcs, lands, _ = state
    grads, recv_a = split_wait(srcs, lands, send, recv, _stage_a_copies, after, name=f"reduce_a_wait_{tag}")
    parts = [sum_stage_a(g, r, place, w, name=f"sum_a_{tag}{i}") for i, (g, r, w) in enumerate(zip(grads, recv_a, wires))]
    lands_b = [lax.empty((3,) + p[1].shape[1:], p[1].dtype) for p in parts]
    send, recv, srcs, lands, token = split_start([p[1] for p in parts], lands_b, _stage_b_copies,
                                                 name=f"reduce_b_start_{tag}")
    return [p[0] for p in parts], send, recv, srcs, lands, token


def _reduce_c_begin(state, place, after, *, tag):
    parts, send, recv, srcs, lands, _ = state
    _, recv_b = split_wait(srcs, lands, send, recv, _stage_b_copies, after, name=f"reduce_b_wait_{tag}")
    halves = [sum_stage_b(p, r, place, name=f"sum_b_{tag}{i}") for i, (p, r) in enumerate(zip(parts, recv_b))]
    send, recv, fulls, _, token = split_start(halves, [], _stage_c_copies, name=f"reduce_c_start_{tag}")
    return send, recv, fulls, token


def _reduce_finish(state, after, *, tag):
    send, recv, fulls, _ = state
    fulls, _ = split_wait(fulls, [], send, recv, _stage_c_copies, after, name=f"reduce_c_wait_{tag}")
    return fulls


def kernel(x, norm_mix, norm_ffn, final_norm, w_in_even, conv_a, w_pool, pool_scale, w_out_even, w_in_odd, sgu_norm, w_spatial, b_spatial, w_out_odd, w_ffn_gate, w_ffn_up, conv_ffn, b_conv_ffn, w_ffn_down, loss_target, m_norm_mix, m_norm_ffn, m_final_norm, m_w_in_even, m_conv_a, m_w_pool, m_pool_scale, m_w_out_even, m_w_in_odd, m_sgu_norm, m_w_spatial, m_b_spatial, m_w_out_odd, m_w_ffn_gate, m_w_ffn_up, m_conv_ffn, m_b_conv_ffn, m_w_ffn_down, v_norm_mix, v_norm_ffn, v_final_norm, v_w_in_even, v_conv_a, v_w_pool, v_pool_scale, v_w_out_even, v_w_in_odd, v_sgu_norm, v_w_spatial, v_b_spatial, v_w_out_odd, v_w_ffn_gate, v_w_ffn_up, v_conv_ffn, v_b_conv_ffn, v_w_ffn_down):
    weights = dict(norm_mix=norm_mix, norm_ffn=norm_ffn, final_norm=final_norm, w_in_even=w_in_even,
                   conv_a=conv_a, w_pool=w_pool, pool_scale=pool_scale, w_out_even=w_out_even,
                   w_in_odd=w_in_odd, sgu_norm=sgu_norm, w_spatial=w_spatial, b_spatial=b_spatial,
                   w_out_odd=w_out_odd, w_ffn_gate=w_ffn_gate, w_ffn_up=w_ffn_up, conv_ffn=conv_ffn,
                   b_conv_ffn=b_conv_ffn, w_ffn_down=w_ffn_down)
    m_in = dict(norm_mix=m_norm_mix, norm_ffn=m_norm_ffn, final_norm=m_final_norm, w_in_even=m_w_in_even,
                conv_a=m_conv_a, w_pool=m_w_pool, pool_scale=m_pool_scale, w_out_even=m_w_out_even,
                w_in_odd=m_w_in_odd, sgu_norm=m_sgu_norm, w_spatial=m_w_spatial, b_spatial=m_b_spatial,
                w_out_odd=m_w_out_odd, w_ffn_gate=m_w_ffn_gate, w_ffn_up=m_w_ffn_up, conv_ffn=m_conv_ffn,
                b_conv_ffn=m_b_conv_ffn, w_ffn_down=m_w_ffn_down)
    v_in = dict(norm_mix=v_norm_mix, norm_ffn=v_norm_ffn, final_norm=v_final_norm, w_in_even=v_w_in_even,
                conv_a=v_conv_a, w_pool=v_w_pool, pool_scale=v_pool_scale, w_out_even=v_w_out_even,
                w_in_odd=v_w_in_odd, sgu_norm=v_sgu_norm, w_spatial=v_w_spatial, b_spatial=v_b_spatial,
                w_out_odd=v_w_out_odd, w_ffn_gate=v_w_ffn_gate, w_ffn_up=v_w_ffn_up, conv_ffn=v_conv_ffn,
                b_conv_ffn=v_b_conv_ffn, w_ffn_down=v_w_ffn_down)
    order = list(weights)

    chip = 2 * lax.axis_index("x") + lax.axis_index("y")
    core = lax.axis_index("c")
    place = jnp.stack([chip, core]).astype(jnp.int32)
    chip_arr = place[:1]

    h0 = x[0]
    target = loss_target[0]
    d_model = h0.shape[1]
    f_shard = w_ffn_gate.shape[-1]

    def turned(a):
        return jnp.transpose(a, (0, 2, 1))

    def own_slot(v):
        return lax.dynamic_update_index_in_dim(jnp.zeros((N_CHIPS,) + v.shape, v.dtype), v, chip, 0)

    first = [[cast_into_slot(w_in_even[0], chip_arr, name="cast_win_e"),
              cast_into_slot(w_out_even[0], chip_arr, name="cast_wout_e")]]
    smalls = [own_slot(conv_a[0]), own_slot(sgu_norm), own_slot(conv_ffn[0]), own_slot(conv_ffn[1])]
    sems0, first, smalls, token0 = gather_start(first, smalls, name="gather_start0")
    rest = [
        [cast_into_slot(turned(w_ffn_gate), chip_arr, l=0, name="cast_wg0", pin=token0),
         cast_into_slot(turned(w_ffn_up), chip_arr, l=0, name="cast_wu0", pin=token0)],
        [cast_into_slot(w_ffn_down, chip_arr, l=0, name="cast_wd0", pin=token0)],
        [cast_into_slot(w_in_odd[0], chip_arr, name="cast_win_o", pin=token0),
         cast_into_slot(w_out_odd[0], chip_arr, name="cast_wout_o", pin=token0)],
        [cast_into_slot(turned(w_ffn_gate), chip_arr, l=1, name="cast_wg1", pin=token0),
         cast_into_slot(turned(w_ffn_up), chip_arr, l=1, name="cast_wu1", pin=token0)],
        [cast_into_slot(w_ffn_down, chip_arr, l=1, name="cast_wd1", pin=token0)]]
    sems1, rest, _, token = gather_start(rest, [], name="gather_start1")
    groups = first + rest
    sems = list(sems0[:2]) + list(sems1) + list(sems0[2:])

    def arrive(gi, after, with_smalls=False):
        kw = dict(smalls=smalls, small_send=sems[-2], small_recv=sems[-1]) if with_smalls else {}
        bufs, small_out = gather_wait(groups[gi], sems[2 * gi], sems[2 * gi + 1], after, name=f"gather_wait{gi}", **kw)
        return gather_forward(bufs, name=f"gather_forward{gi}"), small_out

    def arrive_begin(gi, after):
        bufs, _ = gather_wait(groups[gi], sems[2 * gi], sems[2 * gi + 1], after, name=f"gather_wait{gi}")
        send, recv, bufs, _, tok = split_start(bufs, [], _forward_copies, name=f"gather_forward_start{gi}")
        return (send, recv, bufs), tok

    def arrive_end(state, after, gi):
        send, recv, bufs = state
        return split_wait(bufs, [], send, recv, _forward_copies, after, name=f"gather_forward_wait{gi}")[0]

    cb = b_conv_ffn.reshape(-1, N_CHIPS, 1, f_shard)
    wp_bf = w_pool[0].astype(BF16)
    wp_t_bf = jnp.transpose(w_pool[0], (0, 2, 1)).astype(BF16)
    ws = w_spatial[0]
    bs = b_spatial[0][:, :, None]

    (win_e, wout_e), (ca_g, sn_g, cw0, cw1) = arrive(0, token, with_smalls=True)
    wout_e = wout_e.reshape(-1, d_model)
    ca_full = jnp.transpose(ca_g, (1, 0, 2)).reshape(ca_g.shape[1], -1)
    sn_full = sn_g.reshape(1, -1)
    h1, xn0, rstd0, proj0, mix0, hn0, rstdf0 = even_layer_fwd(
        h0, norm_mix[0:1], win_e, ca_full, wp_bf, pool_scale, wout_e, norm_ffn[0:1], name="l0_fwd")
    (wg0, wu0), _ = arrive(1, hn0)
    g0, up0, gc0, act0 = ffn_in_fwd(hn0, wg0, wu0, cw0, cb[0], name="ffn0_in")
    swap2, tok2 = arrive_begin(2, act0)
    swap3, tok3 = arrive_begin(3, tok2)
    (wd0,) = arrive_end(swap2, tok3, 2)
    h2 = mm_acc(act0, wd0, h1, name="ffn0_down")
    ffn0 = (hn0, rstdf0, g0, up0, gc0)
    swap4, tok4 = arrive_begin(4, h2)
    win_o, wout_o = arrive_end(swap3, tok4, 3)
    wout_o = wout_o.reshape(-1, d_model)
    h3, xn1, rstd1, p1, mix1, rstd_v = odd_layer_fwd(h2, norm_mix[1:2], win_o, sn_full, ws, bs, wout_o, name="l1_fwd")
    wg1, wu1 = arrive_end(swap4, h3, 4)
    swap5, tok5 = arrive_begin(5, wg1)
    (wd1,) = arrive_end(swap5, tok5, 5)
    dh4, dh4_bf, loss_row, d_final, hn1, rstdf1, g1, up1, gc1 = ffn_loss_fwd(
        h3, norm_ffn[1:2], wg1, wu1, wd1, cw1, cb[1], target, final_norm[None], name="ffn1_fwd_loss")
    ffn1 = (hn1, rstdf1, g1, up1, gc1)


    dh3, dh3_bf, (dcw1, dcb1, dnf1), red3 = _ffn_bwd(
        dh4, dh4_bf, h3, norm_ffn[1:2], ffn1, wg1, wu1, wd1, cw1, cb[1], place, 1, None)

    def as_blocks(a):
        return a.reshape(N_CHIPS, -1, d_model)

    dp1, dwin_o, dwout_o, dsn, dws, dbs = sgu_bwd(p1, mix1, xn1, dh3_bf, wout_o, rstd_v, sn_full, ws, bs,
                                                  name="l1_mix_bwd", pin=red3[-1])
    dwout_o = as_blocks(dwout_o)
    red2 = _reduce_a_begin([(dwin_o, BF16), (dwout_o, BF16)], tag="m1")
    dh2, dh2_bf, dnm1 = dx_rms_bwd([(dp1, win_o)], h2, norm_mix[1:2], rstd1, dh3, name="l1_dx", pin=red2[-1])
    red2 = _reduce_b_begin(red2, place, dh2_bf, tag="m1")

    dh1, dh1_bf, (dcw0, dcb0, dnf0), red1 = _ffn_bwd(
        dh2, dh2_bf, h1, norm_ffn[0:1], ffn0, wg0, wu0, wd0, cw0, cb[0], place, 0, red2[-1])

    dproj0, dwin_e, dwout_e, dca, dwp, dps = even_bwd(proj0, mix0, xn0, dh1_bf, wout_e, ca_full, wp_bf, wp_t_bf,
                                                      pool_scale, name="l0_mix_bwd", pin=red1[-1])
    dwout_e = as_blocks(dwout_e)
    dh0, _, dnm0 = dx_rms_bwd([(dproj0, win_e)], h0, norm_mix[0:1], rstd0, dh1, name="l0_dx")
    grad_x = dh0[None]

    small_parts = dict(
        loss=loss_row[:, :1],
        norm_mix=jnp.concatenate([dnm0, dnm1]), norm_ffn=jnp.concatenate([dnf0, dnf1]), final_norm=d_final,
        conv_a=dca, w_pool=dwp, pool_scale=dps, sgu_norm=dsn, w_spatial=dws, b_spatial=dbs,
        conv_ffn=jnp.stack([dcw0, dcw1]), b_conv_ffn=jnp.stack([dcb0, dcb1]))
    flat = jnp.concatenate([v.reshape(-1) for v in small_parts.values()])
    pad = (-flat.shape[0]) % (N_CHIPS * 32 * 128)
    small = jnp.pad(flat, (0, pad)).reshape(N_CHIPS, -1, 128)
    red0 = _reduce_a_begin([(dwin_e, BF16), (dwout_e, BF16), (small, F32)], tag="m0")

    swap_f1 = _reduce_c_begin(red3, place, red0[-1], tag="f1")
    red0 = _reduce_b_begin(red0, place, swap_f1[-1], tag="m0")
    swap_m1 = _reduce_c_begin(red2, place, red0[-1], tag="m1")
    swap_f0 = _reduce_c_begin(red1, place, swap_m1[-1], tag="f0")
    full3 = _reduce_finish(swap_f1, swap_f0[-1], tag="f1")
    full2 = _reduce_finish(swap_m1, full3[0], tag="m1")
    full1 = _reduce_finish(swap_f0, full2[0], tag="f0")
    swap_m0 = _reduce_c_begin(red0, place, full1[0], tag="m0")
    full0 = _reduce_finish(swap_m0, swap_m0[-1], tag="m0")
    small_slots = lax.dynamic_update_index_in_dim(jnp.zeros(small.shape, F32), full0[2], chip, 0)
    small_sum = gather_chip_blocks(small_slots, name="gather_small").reshape(-1)
    grads = {
        "w_in_even": full0[0][None], "w_out_even": full0[1][None],
        "w_in_odd": full2[0][None], "w_out_odd": full2[1][None],
        }
    layered = {"w_ffn_gate": [full1[0], full3[0]], "w_ffn_up": [full1[1], full3[1]],
               "w_ffn_down": [full1[2], full3[2]]}
    off = 0
    small_red = {}
    for nm, v in small_parts.items():
        small_red[nm] = small_sum[off:off + v.size].reshape(v.shape)
        off += v.size
    loss = small_red["loss"].reshape(())
    for nm in ("norm_mix", "norm_ffn", "pool_scale"):
        grads[nm] = small_red[nm].reshape(weights[nm].shape)
    grads["final_norm"] = small_red["final_norm"].reshape(weights["final_norm"].shape)
    grads["w_pool"] = small_red["w_pool"][None]
    grads["w_spatial"] = small_red["w_spatial"][None]
    grads["b_spatial"] = small_red["b_spatial"].reshape(weights["b_spatial"].shape)
    grads["b_conv_ffn"] = small_red["b_conv_ffn"].reshape(weights["b_conv_ffn"].shape)
    grads["conv_a"] = lax.dynamic_slice_in_dim(small_red["conv_a"], chip * conv_a.shape[-1], conv_a.shape[-1], 1)[None]
    grads["sgu_norm"] = lax.dynamic_slice_in_dim(small_red["sgu_norm"], chip * sgu_norm.shape[-1], sgu_norm.shape[-1], 1)
    grads["conv_ffn"] = lax.dynamic_index_in_dim(small_red["conv_ffn"], chip, 1, keepdims=False)

    deltas, new_m, new_v = {}, {}, {}
    for nm, per_layer in layered.items():
        if nm == "w_ffn_down":
            grads[nm], deltas[nm], new_m[nm], new_v[nm] = adamw_layers(
                weights[nm], per_layer, m_in[nm], v_in[nm], name=f"adamw_{nm}")
        else:
            outs = adamw_layers(turned(weights[nm]), per_layer, turned(m_in[nm]), turned(v_in[nm]),
                                name=f"adamw_{nm}")
            grads[nm], deltas[nm], new_m[nm], new_v[nm] = (turned(o) for o in outs)
    for nm in order:
        if nm in layered:
            continue
        w = weights[nm]
        w2 = w[None] if w.ndim == 1 else w
        shp = w2.shape
        d, nm_, nv_ = adamw(w2, grads[nm].reshape(shp), m_in[nm].reshape(shp), v_in[nm].reshape(shp),
                            name=f"adamw_{nm}")
        deltas[nm], new_m[nm], new_v[nm] = d.reshape(w.shape), nm_.reshape(w.shape), nv_.reshape(w.shape)

    return (loss, grad_x, *[grads[n] for n in order], *[deltas[n] for n in order],
            *[new_m[n] for n in order], *[new_v[n] for n in order])
```

```python
import jax
import jax.numpy as jnp
from jax import lax
from jax.experimental import pallas as pl
from jax.experimental.pallas import tpu as pltpu

F32 = jnp.float32
BF16 = jnp.bfloat16
MESH = pl.DeviceIdType.MESH
ANY = pl.BlockSpec(memory_space=pl.ANY)

EPS = 1e-6
POOL_WINDOWS = (2, 4, 8, 16)
GROUP = 128
CHUNK = 128
N_CHIPS = 4
N_DEV = 8
ROW_TILE = 512
HALO = 16
VMEM_LIMIT = 56 * 1024 * 1024

ADAM_LR = 0.001
ADAM_B1 = 0.9
ADAM_B2 = 0.999
ADAM_EPS = 1e-08
ADAM_WD = 0.01
ADAM_STEP = 10


def _params(*sem):
    return pltpu.CompilerParams(dimension_semantics=sem, vmem_limit_bytes=VMEM_LIMIT)


def mm_acc(a, b, res, *, name):
    j_n, s, kj = a.shape
    n = b.shape[-1]
    tm = min(ROW_TILE, s)

    def body(a_ref, b_ref, r_ref, o_ref):
        acc = r_ref[...]
        for j in range(j_n):
            acc = acc + jnp.dot(a_ref[j], b_ref[j], preferred_element_type=F32)
        o_ref[...] = acc

    return pl.pallas_call(
        body, name=name, grid=(s // tm,),
        in_specs=[pl.BlockSpec((j_n, tm, kj), lambda i: (0, i, 0)),
                  pl.BlockSpec((j_n, kj, n), lambda i: (0, 0, 0)),
                  pl.BlockSpec((tm, n), lambda i: (i, 0))],
        out_specs=pl.BlockSpec((tm, n), lambda i: (i, 0)),
        out_shape=jax.ShapeDtypeStruct((s, n), F32),
        compiler_params=_params("parallel"))(a, b, res)


_NT = (((1,), (1,)), ((), ()))
_TN = (((0,), (0,)), ((), ()))


def _back(x, k):
    return pltpu.roll(x, k, 0)


def _fwd(x, k):
    return pltpu.roll(x, x.shape[0] - k, 0)


def _causal_conv(x, w_ref):
    return w_ref[0:1, :] * _back(x, 2) + w_ref[1:2, :] * _back(x, 1) + w_ref[2:3, :] * x


def _causal_conv_t(dy, w_ref):
    return w_ref[2:3, :] * dy + w_ref[1:2, :] * _fwd(dy, 1) + w_ref[0:1, :] * _fwd(dy, 2)


def _gelu(x):
    return 0.5 * x * (1.0 + lax.erf(x * 0.7071067811865476))


def _gelu_grad(x):
    return 0.5 * (1.0 + lax.erf(x * 0.7071067811865476)) + x * jnp.exp(-0.5 * x * x) * 0.3989422804014327


def _colsum(x):
    return jnp.sum(x, axis=0, keepdims=True)


def _halo_specs(n_lead, ts, width, n_tiles):
    hb = ts // HALO
    prev = pl.BlockSpec((n_lead, HALO, width), lambda i: (0, jnp.maximum(i * hb - 1, 0), 0))
    nxt = pl.BlockSpec((n_lead, HALO, width), lambda i: (0, jnp.minimum((i + 1) * hb, n_tiles * hb - 1), 0))
    return prev, nxt


def _pool_fwd(z_ext, g, pos):
    w = POOL_WINDOWS[g]
    zg = z_ext[:, g * GROUP:(g + 1) * GROUP]
    acc = zg
    sh = 1
    while sh < w:
        acc = acc + _back(acc, sh)
        sh *= 2
    return acc[HALO:] / jnp.minimum(pos, float(w)) - zg[HALO:]


def even_layer_fwd(h, gain, win, conv_a, w_pool, pool_scale, wout, gain_next, *, name, pin=None):
    s, d = h.shape
    w = win.shape[-1]
    ts = min(ROW_TILE, s)
    hb = ts // HALO

    def body(h_ref, hp_ref, gain_ref, win_ref, ca_ref, wp_ref, ps_ref, wout_ref, gn_ref, *rest):
        o_ref, xn_ref, r_ref, p_ref, m_ref, hn_ref, rn_ref = rest[-7:]
        i = pl.program_id(0)
        keep = jnp.where(i > 0, 1.0, 0.0)
        h_ext = jnp.concatenate([hp_ref[...], h_ref[...]], axis=0)
        rstd = lax.rsqrt(jnp.mean(h_ext * h_ext, axis=-1, keepdims=True) + EPS)
        xn_ext = (h_ext * rstd * gain_ref[...]).astype(BF16)
        xn_ref[...] = xn_ext[HALO:]
        r_ref[...] = rstd[HALO:]
        p32 = []
        for k in range(4):
            pk = jnp.dot(xn_ext, win_ref[k], preferred_element_type=F32).astype(BF16)
            p_ref[k] = pk[HALO:]
            pk = pk.astype(F32)
            p32.append(jnp.concatenate([pk[:HALO] * keep, pk[HALO:]], axis=0))
        m_ref[:, 0:w] = (p32[0][HALO:] * _causal_conv(p32[1] * p32[2], ca_ref)[HALO:]).astype(BF16)
        pos = (i * ts + lax.broadcasted_iota(jnp.int32, (ts, 1), 0) + 1).astype(F32)
        for g in range(len(POOL_WINDOWS)):
            pooled = _pool_fwd(p32[3], g, pos)
            mixed = jnp.dot(pooled.astype(BF16), wp_ref[g], preferred_element_type=F32)
            cols = slice(g * GROUP, (g + 1) * GROUP)
            m_ref[:, w + g * GROUP:w + (g + 1) * GROUP] = (mixed * ps_ref[:, cols]).astype(BF16)
        out = h_ref[...] + jnp.dot(m_ref[...], wout_ref[...], preferred_element_type=F32)
        o_ref[...] = out
        rstd_n = lax.rsqrt(jnp.mean(out * out, axis=-1, keepdims=True) + EPS)
        hn_ref[...] = (out * rstd_n * gn_ref[...]).astype(BF16)
        rn_ref[...] = rstd_n

    const = lambda shape: pl.BlockSpec(shape, lambda i: (0,) * len(shape))
    row = pl.BlockSpec((ts, d), lambda i: (i, 0))
    col1 = pl.BlockSpec((ts, 1), lambda i: (i, 0))
    return pl.pallas_call(
        body, name=name, grid=(s // ts,),
        in_specs=[row, pl.BlockSpec((HALO, d), lambda i: (jnp.maximum(i * hb - 1, 0), 0)), const((1, d)),
                  const((4, d, w)), const((3, w)), const((4, GROUP, GROUP)), const((1, w)), const((2 * w, d)),
                  const((1, d))] + ([ANY] if pin is not None else []),
        out_specs=[row, row, col1, pl.BlockSpec((4, ts, w), lambda i: (0, i, 0)),
                   pl.BlockSpec((ts, 2 * w), lambda i: (i, 0)), row, col1],
        out_shape=[jax.ShapeDtypeStruct((s, d), F32), jax.ShapeDtypeStruct((s, d), BF16),
                   jax.ShapeDtypeStruct((s, 1), F32), jax.ShapeDtypeStruct((4, s, w), BF16),
                   jax.ShapeDtypeStruct((s, 2 * w), BF16), jax.ShapeDtypeStruct((s, d), BF16),
                   jax.ShapeDtypeStruct((s, 1), F32)],
        compiler_params=_params("parallel"))(
            h, h, gain, win, conv_a, w_pool, pool_scale, wout, gain_next, *([pin] if pin is not None else []))


def even_bwd(proj, mix, xn, dh_bf, wout, conv_a, w_pool, w_pool_t, pool_scale, *, name, pin=None):
    _, s, w = proj.shape
    d = dh_bf.shape[1]
    ts = min(ROW_TILE, s)
    n_t = s // ts
    prev, nxt = _halo_specs(4, ts, w, n_t)
    hb = ts // HALO
    n_ext = ts + HALO

    def body(p_ref, pp_ref, pn_ref, m_ref, x_ref, dh_ref, dhn_ref, wout_ref, ca_ref, wp_ref, wpt_ref, ps_ref, *rest):
        dp_ref, dwi_ref, dwo_ref, dca_ref, dwp_ref, dps_ref = rest[-6:]
        i = pl.program_id(0)

        @pl.when(i == 0)
        def _():
            dwi_ref[...] = jnp.zeros_like(dwi_ref)
            dwo_ref[...] = jnp.zeros_like(dwo_ref)
            dca_ref[...] = jnp.zeros_like(dca_ref)
            dwp_ref[...] = jnp.zeros_like(dwp_ref)
            dps_ref[...] = jnp.zeros_like(dps_ref)

        keep_p = jnp.where(i > 0, 1.0, 0.0)
        keep_n = jnp.where(i < n_t - 1, 1.0, 0.0)
        dwo_ref[...] += lax.dot_general(m_ref[...], dh_ref[...], _TN, preferred_element_type=F32)
        dmix = lax.dot_general(jnp.concatenate([dh_ref[...], dhn_ref[...]], axis=0), wout_ref[...], _NT,
                               preferred_element_type=F32)
        a_b, a_c, a_v = (p_ref[k].astype(F32) for k in range(3))
        cv_ext = jnp.concatenate([pp_ref[1].astype(F32) * pp_ref[2].astype(F32) * keep_p, a_c * a_v], axis=0)
        dy_a = dmix[:ts, 0:w]
        dp_ref[0] = (dy_a * _causal_conv(cv_ext, ca_ref)[HALO:]).astype(BF16)
        dcc = dy_a * a_b
        dca_ref[2:3, :] += _colsum(dcc * cv_ext[HALO:])
        dca_ref[1:2, :] += _colsum(dcc * _back(cv_ext, 1)[HALO:])
        dca_ref[0:1, :] += _colsum(dcc * _back(cv_ext, 2)[HALO:])
        dcc_ext = jnp.concatenate([dcc, dmix[ts:, 0:w] * pn_ref[0].astype(F32) * keep_n], axis=0)
        dcv = _causal_conv_t(dcc_ext, ca_ref)[:ts]
        dp_ref[1] = (dcv * a_v).astype(BF16)
        dp_ref[2] = (dcv * a_c).astype(BF16)
        z_ext = jnp.concatenate([pp_ref[3].astype(F32) * keep_p, p_ref[3].astype(F32)], axis=0)
        pos = (i * ts + lax.broadcasted_iota(jnp.int32, (ts, 1), 0) + 1).astype(F32)
        pos_ext = (i * ts + lax.broadcasted_iota(jnp.int32, (n_ext, 1), 0) + 1).astype(F32)
        for g, win in enumerate(POOL_WINDOWS):
            cols = slice(g * GROUP, (g + 1) * GROUP)
            ycols = slice(w + g * GROUP, w + (g + 1) * GROUP)
            pooled = _pool_fwd(z_ext, g, pos).astype(BF16)
            mixed = jnp.dot(pooled, wp_ref[g], preferred_element_type=F32)
            dy_b = dmix[:ts, ycols]
            dps_ref[:, cols] += _colsum(dy_b * mixed)
            dmixed_ext = jnp.concatenate([dy_b, dmix[ts:, ycols] * keep_n], axis=0) * ps_ref[:, cols]
            dmixed_ext = dmixed_ext.astype(BF16)
            dwp_ref[g] += lax.dot_general(pooled, dmixed_ext[:ts], _TN, preferred_element_type=F32)
            dpooled = jnp.dot(dmixed_ext, wpt_ref[g], preferred_element_type=F32)
            acc = dpooled / jnp.minimum(pos_ext, float(win))
            sh = 1
            while sh < win:
                acc = acc + _fwd(acc, sh)
                sh *= 2
            dp_ref[3, :, cols] = (acc[:ts] - dpooled[:ts]).astype(BF16)
        for k in range(4):
            dwi_ref[k] += lax.dot_general(x_ref[...], dp_ref[k], _TN, preferred_element_type=F32)

    tile4 = pl.BlockSpec((4, ts, w), lambda i: (0, i, 0))
    const = lambda shape: pl.BlockSpec(shape, lambda i: (0,) * len(shape))
    return pl.pallas_call(
        body, name=name, grid=(n_t,),
        in_specs=[tile4, prev, nxt, pl.BlockSpec((ts, 2 * w), lambda i: (i, 0)), pl.BlockSpec((ts, d), lambda i: (i, 0)),
                  pl.BlockSpec((ts, d), lambda i: (i, 0)),
                  pl.BlockSpec((HALO, d), lambda i: (jnp.minimum((i + 1) * hb, n_t * hb - 1), 0)),
                  const((2 * w, d)), const((3, w)), const((4, GROUP, GROUP)), const((4, GROUP, GROUP)), const((1, w))]
        + ([ANY] if pin is not None else []),
        out_specs=[tile4, const((4, d, w)), const((2 * w, d)), const((3, w)), const((4, GROUP, GROUP)), const((1, w))],
        out_shape=[jax.ShapeDtypeStruct((4, s, w), BF16), jax.ShapeDtypeStruct((4, d, w), F32),
                   jax.ShapeDtypeStruct((2 * w, d), F32), jax.ShapeDtypeStruct((3, w), F32),
                   jax.ShapeDtypeStruct((4, GROUP, GROUP), F32), jax.ShapeDtypeStruct((1, w), F32)],
        compiler_params=_params("arbitrary"))(
            proj, proj, proj, mix, xn, dh_bf, dh_bf, wout, conv_a, w_pool, w_pool_t, pool_scale,
            *([pin] if pin is not None else []))


def _ffn_next_halo(ts, f, n_t):
    hb = ts // HALO
    return pl.BlockSpec((None, HALO, f), lambda j, i: (j, jnp.minimum((i + 1) * hb, n_t * hb - 1), 0))


def ffn_in_fwd(hn, wg, wu, cw, cb, *, name, pin=None):
    s, d = hn.shape
    j_n, f, _ = wg.shape
    tm = min(ROW_TILE, s)
    hb = tm // HALO

    def body(x_ref, xp_ref, wg_ref, wu_ref, cw_ref, cb_ref, *rest):
        g_ref, u_ref, gc_ref, a_ref = rest[-4:]
        i, j = pl.program_id(0), pl.program_id(1)
        x_ext = jnp.concatenate([xp_ref[...], x_ref[...]], axis=0)
        g_ext = lax.dot_general(x_ext, wg_ref[j], _NT, preferred_element_type=F32).astype(BF16)
        up = lax.dot_general(x_ref[...], wu_ref[j], _NT, preferred_element_type=F32).astype(BF16)
        g_ref[...] = g_ext[HALO:]
        u_ref[...] = up
        a_ref[...] = _ffn_act(g_ext, up, cw_ref, cb_ref, gc_ref, i)

    whole = pl.BlockSpec((j_n, f, d), lambda i, j: (0, 0, 0))
    tile = pl.BlockSpec((None, tm, f), lambda i, j: (j, i, 0))
    shape = jax.ShapeDtypeStruct((j_n, s, f), BF16)
    return pl.pallas_call(
        body, name=name, grid=(s // tm, j_n),
        in_specs=[pl.BlockSpec((tm, d), lambda i, j: (i, 0)),
                  pl.BlockSpec((HALO, d), lambda i, j: (jnp.maximum(i * hb - 1, 0), 0)),
                  whole, whole,
                  pl.BlockSpec((None, 3, f), lambda i, j: (j, 0, 0)),
                  pl.BlockSpec((None, 1, f), lambda i, j: (j, 0, 0))] + ([ANY] if pin is not None else []),
        out_specs=[tile] * 4, out_shape=[shape] * 4,
        compiler_params=_params("parallel", "parallel"))(hn, hn, wg, wu, cw, cb, *([pin] if pin is not None else []))


def _ffn_act(g_ext, up, cw_ref, cb_ref, gc_ref, i):
    keep = jnp.where(i > 0, 1.0, 0.0)
    g32 = jnp.concatenate([g_ext[:HALO].astype(F32) * keep, g_ext[HALO:].astype(F32)], axis=0)
    gc = (_causal_conv(g32, cw_ref)[HALO:] + cb_ref[...]).astype(BF16)
    gc_ref[...] = gc
    gc = gc.astype(F32)
    return (gc * jax.nn.sigmoid(gc) * up.astype(F32)).astype(BF16)


def ffn_loss_fwd(h, gain, wg, wu, wd, cw, cb, target, final_gain, *, name):
    s, d = h.shape
    j_n, f, _ = wg.shape
    tm = min(ROW_TILE, s)
    hb = tm // HALO

    def body(h_ref, hp_ref, gain_ref, wg_ref, wu_ref, wd_ref, cw_ref, cb_ref, t_ref, fg_ref,
             o_ref, ob_ref, l_ref, dfg_ref, xn_ref, r_ref, g_ref, u_ref, gc_ref, x_s, acc_s):
        i, j = pl.program_id(0), pl.program_id(1)

        @pl.when((i == 0) & (j == 0))
        def _():
            l_ref[...] = jnp.zeros_like(l_ref)
            dfg_ref[...] = jnp.zeros_like(dfg_ref)

        @pl.when(j == 0)
        def _():
            h_ext = jnp.concatenate([hp_ref[...], h_ref[...]], axis=0)
            rstd = lax.rsqrt(jnp.mean(h_ext * h_ext, axis=-1, keepdims=True) + EPS)
            x_s[...] = (h_ext * rstd * gain_ref[...]).astype(BF16)
            xn_ref[...] = x_s[HALO:, :]
            r_ref[...] = rstd[HALO:]
            acc_s[...] = h_ref[...]

        g_ext = lax.dot_general(x_s[...], wg_ref[j], _NT, preferred_element_type=F32).astype(BF16)
        up = lax.dot_general(x_s[HALO:, :], wu_ref[j], _NT, preferred_element_type=F32).astype(BF16)
        g_ref[...] = g_ext[HALO:]
        u_ref[...] = up
        act = _ffn_act(g_ext, up, cw_ref, cb_ref, gc_ref, i)
        acc_s[...] += jnp.dot(act, wd_ref[j], preferred_element_type=F32)

        @pl.when(j == j_n - 1)
        def _():
            x = acc_s[...]
            rstd = lax.rsqrt(jnp.mean(x * x, axis=-1, keepdims=True) + EPS)
            hhat = x * rstd
            err = hhat * fg_ref[...] - t_ref[...]
            l_ref[...] += 0.5 * jnp.sum(jnp.mean(err * err, axis=-1, keepdims=True), axis=0, keepdims=True)
            dy = err * (1.0 / d)
            dfg_ref[...] += _colsum(dy * hhat)
            dyg = dy * fg_ref[...]
            dh = rstd * (dyg - hhat * jnp.mean(dyg * hhat, axis=-1, keepdims=True))
            o_ref[...] = dh
            ob_ref[...] = dh.astype(BF16)

    whole = pl.BlockSpec((j_n, f, d), lambda i, j: (0, 0, 0), pipeline_mode=pl.Buffered(1))
    row = pl.BlockSpec((tm, d), lambda i, j: (i, 0))
    vec = pl.BlockSpec((1, d), lambda i, j: (0, 0))
    tile = pl.BlockSpec((None, tm, f), lambda i, j: (j, i, 0))
    return pl.pallas_call(
        body, name=name, grid=(s // tm, j_n),
        in_specs=[row, pl.BlockSpec((HALO, d), lambda i, j: (jnp.maximum(i * hb - 1, 0), 0)),
                  vec, whole, whole, whole,
                  pl.BlockSpec((None, 3, f), lambda i, j: (j, 0, 0)),
                  pl.BlockSpec((None, 1, f), lambda i, j: (j, 0, 0)), row, vec],
        out_specs=[row, row, pl.BlockSpec((1, 128), lambda i, j: (0, 0)), vec,
                   row, pl.BlockSpec((tm, 1), lambda i, j: (i, 0)), tile, tile, tile],
        out_shape=[jax.ShapeDtypeStruct((s, d), F32), jax.ShapeDtypeStruct((s, d), BF16),
                   jax.ShapeDtypeStruct((1, 128), F32), jax.ShapeDtypeStruct((1, d), F32),
                   jax.ShapeDtypeStruct((s, d), BF16), jax.ShapeDtypeStruct((s, 1), F32)]
        + [jax.ShapeDtypeStruct((j_n, s, f), BF16)] * 3,
        scratch_shapes=[pltpu.VMEM((HALO + tm, d), BF16), pltpu.VMEM((tm, d), F32)],
        compiler_params=_params("arbitrary", "arbitrary"))(h, h, gain, wg, wu, wd, cw, cb, target, final_gain)


def ffn_bwd_a(dh_bf, hn, g, up, gc, wd, cw, *, name, pin=None):
    s, d = hn.shape
    j_n, _, f = g.shape
    tm = min(ROW_TILE, s)
    n_t = s // tm
    hb = tm // HALO

    def body(dh_ref, dhn_ref, x_ref, g_ref, gc_ref, gcn_ref, u_ref, un_ref, wd_ref, cw_ref, *rest):
        dg_ref, du_ref, dwg_ref, dwu_ref, dwd_ref, dcw_ref, dcb_ref = rest[-7:]
        i = pl.program_id(1)

        @pl.when(i == 0)
        def _():
            for r in (dwg_ref, dwu_ref, dwd_ref, dcw_ref, dcb_ref):
                r[...] = jnp.zeros_like(r)

        keep_n = jnp.where(i < n_t - 1, 1.0, 0.0)
        dh = dh_ref[...]
        dact = lax.dot_general(jnp.concatenate([dh, dhn_ref[...]], axis=0), wd_ref[...], _NT,
                               preferred_element_type=F32)
        dact = jnp.concatenate([dact[:tm], dact[tm:] * keep_n], axis=0)
        gc_ext = jnp.concatenate([gc_ref[...], gcn_ref[...]], axis=0).astype(F32)
        sig = jax.nn.sigmoid(gc_ext)
        silu = gc_ext * sig
        up_ext = jnp.concatenate([u_ref[...], un_ref[...]], axis=0).astype(F32)
        act = (silu * up_ext)[:tm].astype(BF16)
        dwd_ref[...] += lax.dot_general(act, dh, _TN, preferred_element_type=F32)
        dup = (dact * silu)[:tm].astype(BF16)
        du_ref[...] = dup
        dgc = dact * up_ext * (sig + silu * (1.0 - sig))
        dgc_1, dgc_2 = _fwd(dgc, 1), _fwd(dgc, 2)
        dg = (cw_ref[2:3, :] * dgc + cw_ref[1:2, :] * dgc_1 + cw_ref[0:1, :] * dgc_2)[:tm].astype(BF16)
        dg_ref[...] = dg
        x = x_ref[...]
        dwg_ref[...] += lax.dot_general(dg, x, _TN, preferred_element_type=F32)
        dwu_ref[...] += lax.dot_general(dup, x, _TN, preferred_element_type=F32)
        g32 = g_ref[...].astype(F32)
        dcb_ref[...] += _colsum(dgc[:tm])
        dcw_ref[2:3, :] += _colsum(dgc[:tm] * g32)
        dcw_ref[1:2, :] += _colsum(dgc_1[:tm] * g32)
        dcw_ref[0:1, :] += _colsum(dgc_2[:tm] * g32)

    rows = pl.BlockSpec((tm, d), lambda j, i: (i, 0))
    rows_next = pl.BlockSpec((HALO, d), lambda j, i: (jnp.minimum((i + 1) * hb, n_t * hb - 1), 0))
    tile = pl.BlockSpec((None, tm, f), lambda j, i: (j, i, 0))
    nxt = _ffn_next_halo(tm, f, n_t)
    per_j = lambda r, c: pl.BlockSpec((None, r, c), lambda j, i: (j, 0, 0))
    return pl.pallas_call(
        body, name=name, grid=(j_n, n_t),
        in_specs=[rows, rows_next, rows, tile, tile, nxt, tile, nxt, per_j(f, d), per_j(3, f)]
        + ([ANY] if pin is not None else []),
        out_specs=[tile, tile, per_j(f, d), per_j(f, d), per_j(f, d), per_j(3, f), per_j(1, f)],
        out_shape=[jax.ShapeDtypeStruct((j_n, s, f), BF16), jax.ShapeDtypeStruct((j_n, s, f), BF16),
                   jax.ShapeDtypeStruct((j_n, f, d), F32), jax.ShapeDtypeStruct((j_n, f, d), F32),
                   jax.ShapeDtypeStruct((j_n, f, d), F32), jax.ShapeDtypeStruct((j_n, 3, f), F32),
                   jax.ShapeDtypeStruct((j_n, 1, f), F32)],
        compiler_params=_params("parallel", "arbitrary"))(
            dh_bf, dh_bf, hn, g, gc, gc, up, up, wd, cw, *([pin] if pin is not None else []))


def dx_rms_bwd(pairs, h, gain, rstd, dres, *, name, pin=None):
    j_n, s, f = pairs[0][0].shape
    d = h.shape[1]
    tm = min(ROW_TILE, s)
    n_p = len(pairs)
    dims = [_NT if w.shape[1:] == (d, f) else (((1,), (0,)), ((), ())) for _, w in pairs]

    def body(*refs):
        dy_refs, w_refs = refs[:n_p], refs[n_p:2 * n_p]
        h_ref, g_ref, r_ref, dr_ref = refs[2 * n_p:2 * n_p + 4]
        o_ref, ob_ref, dgain_ref = refs[-3:]

        @pl.when(pl.program_id(0) == 0)
        def _():
            dgain_ref[...] = jnp.zeros_like(dgain_ref)
        dx = jnp.zeros((tm, d), F32)
        for j in range(j_n):
            for p in range(n_p):
                dx = dx + lax.dot_general(dy_refs[p][j], w_refs[p][j], dims[p], preferred_element_type=F32)
        rstd_v = r_ref[...]
        hhat = h_ref[...] * rstd_v
        dgain_ref[...] += _colsum(dx * hhat)
        dxg = dx * g_ref[...]
        dh = dr_ref[...] + rstd_v * (dxg - hhat * jnp.mean(dxg * hhat, axis=-1, keepdims=True))
        o_ref[...] = dh
        ob_ref[...] = dh.astype(BF16)

    tile4 = pl.BlockSpec((j_n, tm, f), lambda i: (0, i, 0))
    whole = [pl.BlockSpec(w.shape, lambda i: (0, 0, 0), pipeline_mode=pl.Buffered(1)) for _, w in pairs]
    row = pl.BlockSpec((tm, d), lambda i: (i, 0))
    vec = pl.BlockSpec((1, d), lambda i: (0, 0))
    return pl.pallas_call(
        body, name=name, grid=(s // tm,),
        in_specs=[tile4] * n_p + whole + [row, vec, pl.BlockSpec((tm, 1), lambda i: (i, 0)), row]
        + ([ANY] if pin is not None else []),
        out_specs=[row, row, vec],
        out_shape=[jax.ShapeDtypeStruct((s, d), F32), jax.ShapeDtypeStruct((s, d), BF16),
                   jax.ShapeDtypeStruct((1, d), F32)],
        compiler_params=_params("arbitrary"))(
            *[p[0] for p in pairs], *[p[1] for p in pairs], h, gain, rstd, dres, *([pin] if pin is not None else []))


def _sgu_gate(vn_bf, ws_ref, bs_ref, h, rows):
    tri = lax.broadcasted_iota(jnp.int32, (CHUNK, CHUNK), 0) >= lax.broadcasted_iota(jnp.int32, (CHUNK, CHUNK), 1)
    ws = jnp.where(tri, ws_ref[h], 0.0).astype(BF16)
    cols = slice((h % 4) * GROUP, (h % 4 + 1) * GROUP)
    return ws, jnp.dot(ws, vn_bf[h // 4][rows, cols], preferred_element_type=F32) + bs_ref[h]


def odd_layer_fwd(h, gain, win, sgu_norm, w_spatial, b_spatial, wout, *, name):
    s, d = h.shape
    w = win.shape[-1]
    ts = min(ROW_TILE, s)
    n_heads = w_spatial.shape[0]

    def body(h_ref, gain_ref, win_ref, n_ref, ws_ref, bs_ref, wout_ref, o_ref, xn_ref, r_ref, p_ref, m_ref, rv_ref):
        x = h_ref[...]
        rstd_x = lax.rsqrt(jnp.mean(x * x, axis=-1, keepdims=True) + EPS)
        xn = (x * rstd_x * gain_ref[...]).astype(BF16)
        xn_ref[...] = xn
        r_ref[...] = rstd_x
        for k in range(4):
            p_ref[k] = jnp.dot(xn, win_ref[k], preferred_element_type=F32).astype(BF16)
        v = [_gelu(p_ref[2].astype(F32)), _gelu(p_ref[3].astype(F32))]
        ms = (jnp.sum(v[0] * v[0], axis=-1, keepdims=True) + jnp.sum(v[1] * v[1], axis=-1, keepdims=True)) / (2 * w)
        rstd = lax.rsqrt(ms + EPS)
        rv_ref[...] = rstd
        vn = [(v[k] * rstd * n_ref[:, k * w:(k + 1) * w]).astype(BF16) for k in range(2)]
        for hd in range(n_heads):
            cols = slice((hd % 4) * GROUP, (hd % 4 + 1) * GROUP)
            for c in range(ts // CHUNK):
                rows = slice(c * CHUNK, (c + 1) * CHUNK)
                _, gate = _sgu_gate(vn, ws_ref, bs_ref, hd, rows)
                u = _gelu(p_ref[hd // 4, rows, cols].astype(F32))
                m_ref[rows, hd * GROUP:(hd + 1) * GROUP] = (u * gate).astype(BF16)
        o_ref[...] = x + jnp.dot(m_ref[...], wout_ref[...], preferred_element_type=F32)

    const = lambda shape: pl.BlockSpec(shape, lambda i: (0,) * len(shape))
    row = pl.BlockSpec((ts, d), lambda i: (i, 0))
    col1 = pl.BlockSpec((ts, 1), lambda i: (i, 0))
    return pl.pallas_call(
        body, name=name, grid=(s // ts,),
        in_specs=[row, const((1, d)), const((4, d, w)), const((1, 2 * w)),
                  const((n_heads, CHUNK, CHUNK)), const((n_heads, CHUNK, 1)), const((2 * w, d))],
        out_specs=[row, row, col1, pl.BlockSpec((4, ts, w), lambda i: (0, i, 0)),
                   pl.BlockSpec((ts, 2 * w), lambda i: (i, 0)), col1],
        out_shape=[jax.ShapeDtypeStruct((s, d), F32), jax.ShapeDtypeStruct((s, d), BF16),
                   jax.ShapeDtypeStruct((s, 1), F32), jax.ShapeDtypeStruct((4, s, w), BF16),
                   jax.ShapeDtypeStruct((s, 2 * w), BF16), jax.ShapeDtypeStruct((s, 1), F32)],
        compiler_params=_params("parallel"))(h, gain, win, sgu_norm, w_spatial, b_spatial, wout)


def sgu_bwd(p, mix, xn, dh_bf, wout, rstd, sgu_norm, w_spatial, b_spatial, *, name, pin=None):
    _, s, w = p.shape
    d = dh_bf.shape[1]
    ts = min(ROW_TILE, s)
    n_heads = w_spatial.shape[0]

    def body(p_ref, m_ref, x_ref, dh_ref, wout_ref, r_ref, n_ref, ws_ref, bs_ref, *rest):
        dp_ref, dwi_ref, dwo_ref, dn_ref, dws_ref, dbs_ref, dvn_ref, dm_ref = rest[-8:]

        @pl.when(pl.program_id(0) == 0)
        def _():
            dwi_ref[...] = jnp.zeros_like(dwi_ref)
            dwo_ref[...] = jnp.zeros_like(dwo_ref)
            dn_ref[...] = jnp.zeros_like(dn_ref)
            dws_ref[...] = jnp.zeros_like(dws_ref)
            dbs_ref[...] = jnp.zeros_like(dbs_ref)

        dwo_ref[...] += lax.dot_general(m_ref[...], dh_ref[...], _TN, preferred_element_type=F32)
        dm_ref[...] = lax.dot_general(dh_ref[...], wout_ref[...], _NT, preferred_element_type=F32)
        rstd_v = r_ref[...]
        vhat = [_gelu(p_ref[2 + k].astype(F32)) * rstd_v for k in range(2)]
        vn = [(vhat[k] * n_ref[:, k * w:(k + 1) * w]).astype(BF16) for k in range(2)]
        tri = lax.broadcasted_iota(jnp.int32, (CHUNK, CHUNK), 0) >= lax.broadcasted_iota(jnp.int32, (CHUNK, CHUNK), 1)
        for h in range(n_heads):
            cols = slice((h % 4) * GROUP, (h % 4 + 1) * GROUP)
            ocols = slice(h * GROUP, (h + 1) * GROUP)
            for c in range(ts // CHUNK):
                rows = slice(c * CHUNK, (c + 1) * CHUNK)
                ws, gate = _sgu_gate(vn, ws_ref, bs_ref, h, rows)
                pu = p_ref[h // 4, rows, cols].astype(F32)
                dm = dm_ref[rows, ocols]
                dp_ref[h // 4, rows, cols] = (dm * gate * _gelu_grad(pu)).astype(BF16)
                dgate = dm * _gelu(pu)
                dbs_ref[h] += jnp.sum(dgate, axis=-1, keepdims=True)
                dgate_bf = dgate.astype(BF16)
                dws = lax.dot_general(dgate_bf, vn[h // 4][rows, cols], _NT, preferred_element_type=F32)
                dws_ref[h] += jnp.where(tri, dws, 0.0)
                dvn_ref[rows, ocols] = lax.dot_general(ws, dgate_bf, _TN, preferred_element_type=F32)
        for k in range(2):
            kc = slice(k * w, (k + 1) * w)
            dvn = dvn_ref[:, kc]
            dn_ref[:, kc] += _colsum(dvn * vhat[k])
        dvh = [dvn_ref[:, k * w:(k + 1) * w] * n_ref[:, k * w:(k + 1) * w] for k in range(2)]
        dot = (jnp.sum(dvh[0] * vhat[0], axis=-1, keepdims=True)
               + jnp.sum(dvh[1] * vhat[1], axis=-1, keepdims=True)) / (2 * w)
        for k in range(2):
            dv = rstd_v * (dvh[k] - vhat[k] * dot)
            dp_ref[2 + k] = (dv * _gelu_grad(p_ref[2 + k].astype(F32))).astype(BF16)
        for k in range(4):
            dwi_ref[k] += lax.dot_general(x_ref[...], dp_ref[k], _TN, preferred_element_type=F32)

    const = lambda shape: pl.BlockSpec(shape, lambda i: (0,) * len(shape))
    tile4 = pl.BlockSpec((4, ts, w), lambda i: (0, i, 0))
    return pl.pallas_call(
        body, name=name, grid=(s // ts,),
        in_specs=[tile4, pl.BlockSpec((ts, 2 * w), lambda i: (i, 0)), pl.BlockSpec((ts, d), lambda i: (i, 0)),
                  pl.BlockSpec((ts, d), lambda i: (i, 0)), const((2 * w, d)), pl.BlockSpec((ts, 1), lambda i: (i, 0)),
                  const((1, 2 * w)), const((n_heads, CHUNK, CHUNK)), const((n_heads, CHUNK, 1))]
        + ([ANY] if pin is not None else []),
        out_specs=[tile4, const((4, d, w)), const((2 * w, d)), const((1, 2 * w)), const((n_heads, CHUNK, CHUNK)),
                   const((n_heads, CHUNK, 1))],
        out_shape=[jax.ShapeDtypeStruct((4, s, w), BF16), jax.ShapeDtypeStruct((4, d, w), F32),
                   jax.ShapeDtypeStruct((2 * w, d), F32),
                   jax.ShapeDtypeStruct((1, 2 * w), F32),
                   jax.ShapeDtypeStruct((n_heads, CHUNK, CHUNK), F32),
                   jax.ShapeDtypeStruct((n_heads, CHUNK, 1), F32)],
        scratch_shapes=[pltpu.VMEM((ts, 2 * w), F32), pltpu.VMEM((ts, 2 * w), F32)],
        compiler_params=_params("arbitrary"))(
            p, mix, xn, dh_bf, wout, rstd, sgu_norm, w_spatial, b_spatial, *([pin] if pin is not None else []))


def _row_tile(rows):
    if rows <= ROW_TILE:
        return rows
    for t in (512, 384, 352, 256, 128, 64, 32, 16, 8):
        if rows % t == 0:
            return t
    return rows


def adamw(w, g, m, v, *, name):
    shape = w.shape
    cols = shape[-1]
    rows = w.size // cols
    w2, g2, m2, v2 = (a.reshape(rows, cols) for a in (w, g, m, v))
    tr = _row_tile(rows)
    bc1 = 1.0 - ADAM_B1 ** ADAM_STEP
    bc2 = 1.0 - ADAM_B2 ** ADAM_STEP

    def body(w_ref, g_ref, m_ref, v_ref, d_ref, nm_ref, nv_ref):
        grad = g_ref[...]
        m_new = ADAM_B1 * m_ref[...] + (1.0 - ADAM_B1) * grad
        v_new = ADAM_B2 * v_ref[...] + (1.0 - ADAM_B2) * (grad * grad)
        nm_ref[...] = m_new
        nv_ref[...] = v_new
        d_ref[...] = -ADAM_LR * ((m_new / bc1) / (jnp.sqrt(v_new / bc2) + ADAM_EPS) + ADAM_WD * w_ref[...])

    spec = pl.BlockSpec((tr, cols), lambda i: (i, 0))
    outs = pl.pallas_call(
        body, name=name, grid=(rows // tr,),
        in_specs=[spec] * 4, out_specs=[spec] * 3,
        out_shape=[jax.ShapeDtypeStruct((rows, cols), F32)] * 3,
        compiler_params=_params("parallel"))(w2, g2, m2, v2)
    return tuple(o.reshape(shape) for o in outs)


def adamw_layers(w, grads, m, v, *, name):
    n_l, rows, cols = w.shape
    tr = _row_tile(rows)
    bc1 = 1.0 - ADAM_B1 ** ADAM_STEP
    bc2 = 1.0 - ADAM_B2 ** ADAM_STEP
    outs = None
    for l in range(n_l):
        def body(w_ref, g_ref, m_ref, v_ref, *rest):
            go_ref, d_ref, nm_ref, nv_ref = rest[-4:]
            grad = g_ref[...]
            m_new = ADAM_B1 * m_ref[...] + (1.0 - ADAM_B1) * grad
            v_new = ADAM_B2 * v_ref[...] + (1.0 - ADAM_B2) * (grad * grad)
            go_ref[...] = grad
            nm_ref[...] = m_new
            nv_ref[...] = v_new
            d_ref[...] = -ADAM_LR * ((m_new / bc1) / (jnp.sqrt(v_new / bc2) + ADAM_EPS) + ADAM_WD * w_ref[...])

        layer = pl.BlockSpec((None, tr, cols), lambda i, l=l: (l, i, 0))
        prev = list(outs) if outs is not None else []
        outs = pl.pallas_call(
            body, name=f"{name}{l}", grid=(rows // tr,),
            in_specs=[layer, pl.BlockSpec((tr, cols), lambda i: (i, 0)), layer, layer] + [ANY] * len(prev),
            out_specs=[layer] * 4,
            out_shape=[jax.ShapeDtypeStruct(w.shape, F32)] * 4,
            input_output_aliases={4 + k: k for k in range(len(prev))},
            compiler_params=_params("parallel"))(w, grads[l], m, v, *prev)
    return tuple(outs)


def _place():
    return lax.axis_index("x"), lax.axis_index("y"), lax.axis_index("c")


def _other_chips(x, y):
    return [(1 - x, y), (x, 1 - y), (1 - x, 1 - y)]


HBM = pl.BlockSpec(memory_space=pltpu.HBM)
SEM = pl.BlockSpec(memory_space=pltpu.SEMAPHORE)
DATAFLOW = pltpu.SideEffectType.DATAFLOW_SIDE_EFFECTING


def _in_hbm(a):
    return pltpu.with_memory_space_constraint(a, pltpu.HBM)


def cast_into_slot(w, chip, *, l=None, name, pin=None):
    rows, cols = w.shape[-2:]
    tr = _row_tile(rows)

    def body(chip_ref, w_ref, *rest):
        rest[-1][...] = w_ref[...].astype(BF16)

    in_spec = (pl.BlockSpec((tr, cols), lambda i, chip_ref: (i, 0)) if l is None
               else pl.BlockSpec((None, tr, cols), lambda i, chip_ref: (l, i, 0)))
    return pl.pallas_call(
        body, name=name,
        grid_spec=pltpu.PrefetchScalarGridSpec(
            num_scalar_prefetch=1, grid=(rows // tr,), in_specs=[in_spec] + ([ANY] if pin is not None else []),
            out_specs=pl.BlockSpec((None, tr, cols), lambda i, chip_ref: (chip_ref[0], i, 0))),
        out_shape=jax.ShapeDtypeStruct((N_CHIPS, rows, cols), BF16),
        compiler_params=_params("parallel"))(chip, w, *([pin] if pin is not None else []))


def _half(ref, slot, c):
    half = ref.shape[1] // 2
    return ref.at[slot, pl.ds(c * half, half), :]


def gather_start(groups, smalls, *, name):
    flat = [b for g in groups for b in g]
    n_b, n_s, n_g = len(flat), len(smalls), len(groups)
    n_sem = 2 * n_g + (2 if n_s else 0)

    def body(*refs):
        bufs, small_refs = refs[:n_b], refs[n_b:n_b + n_s]
        sems = refs[n_b + n_s:n_b + n_s + n_sem]
        token = refs[-1]
        x, y, c = _place()
        me = 2 * x + y
        chips = _other_chips(x, y)
        for si in range(n_s):
            piece = small_refs[si].at[me]
            for k, (px, py) in enumerate(chips):
                pltpu.make_async_remote_copy(
                    src_ref=piece, dst_ref=piece,
                    send_sem=sems[2 * n_g].at[3 * si + k], recv_sem=sems[2 * n_g + 1].at[3 * si + k],
                    device_id=(px, py, c), device_id_type=MESH).start()
        t = 0
        for gi, group in enumerate(groups):
            for ti in range(len(group)):
                piece = _half(bufs[t], me, c)
                t += 1
                for k, (px, py) in enumerate(chips):
                    pltpu.make_async_remote_copy(
                        src_ref=piece, dst_ref=piece,
                        send_sem=sems[2 * gi].at[3 * ti + k], recv_sem=sems[2 * gi + 1].at[3 * ti + k],
                        device_id=(px, py, c), device_id_type=MESH).start()
        token[...] = jnp.zeros_like(token)

    sem_shapes = []
    for group in groups:
        sem_shapes += [pltpu.SemaphoreType.DMA((3 * len(group),))] * 2
    if n_s:
        sem_shapes += [pltpu.SemaphoreType.DMA((3 * n_s,))] * 2
    arrays = flat + list(smalls)
    res = pl.pallas_call(
        body, name=name,
        out_shape=tuple(sem_shapes) + tuple(pltpu.HBM(a.shape, a.dtype) for a in arrays)
        + (jax.ShapeDtypeStruct((8, 128), F32),),
        in_specs=[HBM] * len(arrays),
        out_specs=tuple([SEM] * n_sem + [HBM] * len(arrays) + [pl.BlockSpec(memory_space=pltpu.VMEM)]),
        input_output_aliases={i: n_sem + i for i in range(len(arrays))},
        compiler_params=pltpu.CompilerParams(has_side_effects=DATAFLOW))(*[_in_hbm(a) for a in arrays])
    sems, thru, token = res[:n_sem], res[n_sem:-1], res[-1]
    out_groups, t = [], 0
    for group in groups:
        out_groups.append(list(thru[t:t + len(group)]))
        t += len(group)
    return sems, out_groups, list(thru[n_b:]), token


def gather_wait(bufs, send, recv, after, *, name, smalls=(), small_send=None, small_recv=None):
    n_b, n_s = len(bufs), len(smalls)
    arrays = list(bufs) + list(smalls)
    sem_ops = [send, recv] + ([small_send, small_recv] if n_s else [])

    def body(*refs):
        buf_refs, small_refs = refs[:n_b], refs[n_b:n_b + n_s]
        sems = refs[n_b + n_s:n_b + n_s + len(sem_ops)]
        x, y, c = _place()
        me = 2 * x + y
        chips = _other_chips(x, y)
        for ti in range(n_b):
            for k, (px, py) in enumerate(chips):
                done = pltpu.make_async_remote_copy(
                    src_ref=_half(buf_refs[ti], me, c), dst_ref=_half(buf_refs[ti], 2 * px + py, c),
                    send_sem=sems[0].at[3 * ti + k], recv_sem=sems[1].at[3 * ti + k],
                    device_id=(px, py, c), device_id_type=MESH)
                done.wait_send()
                done.wait_recv()
        for si in range(n_s):
            for k, (px, py) in enumerate(chips):
                done = pltpu.make_async_remote_copy(
                    src_ref=small_refs[si].at[me], dst_ref=small_refs[si].at[2 * px + py],
                    send_sem=sems[2].at[3 * si + k], recv_sem=sems[3].at[3 * si + k],
                    device_id=(px, py, c), device_id_type=MESH)
                done.wait_send()
                done.wait_recv()

    res = pl.pallas_call(
        body, name=name,
        out_shape=tuple(pltpu.HBM(a.shape, a.dtype) for a in arrays),
        in_specs=[HBM] * len(arrays) + [SEM] * len(sem_ops) + [ANY],
        out_specs=tuple([HBM] * len(arrays)),
        input_output_aliases={i: i for i in range(len(arrays))},
        compiler_params=pltpu.CompilerParams(has_side_effects=DATAFLOW))(*arrays, *sem_ops, after)
    return list(res[:n_b]), list(res[n_b:])


def gather_forward(bufs, *, name):
    n = len(bufs)

    def body(*refs):
        ins, outs = refs[:n], refs[n:2 * n]
        send_sems, recv_sems = refs[2 * n:]
        x, y, c = _place()
        chips = _other_chips(x, y)
        for t in range(n):
            for k, (px, py) in enumerate(chips):
                pltpu.make_async_remote_copy(
                    src_ref=_half(ins[t], 2 * px + py, c), dst_ref=_half(outs[t], 2 * px + py, c),
                    send_sem=send_sems.at[3 * t + k], recv_sem=recv_sems.at[3 * t + k],
                    device_id=(x, y, 1 - c), device_id_type=MESH).start()
        for t in range(n):
            for k, (px, py) in enumerate(chips):
                done = pltpu.make_async_remote_copy(
                    src_ref=_half(ins[t], 2 * px + py, c), dst_ref=_half(outs[t], 2 * px + py, 1 - c),
                    send_sem=send_sems.at[3 * t + k], recv_sem=recv_sems.at[3 * t + k],
                    device_id=(x, y, 1 - c), device_id_type=MESH)
                done.wait_send()
                done.wait_recv()

    return pl.pallas_call(
        body, name=name, in_specs=[ANY] * n, out_specs=[ANY] * n,
        out_shape=[jax.ShapeDtypeStruct(a.shape, a.dtype) for a in bufs],
        input_output_aliases={i: i for i in range(n)},
        scratch_shapes=[pltpu.SemaphoreType.DMA((3 * n,)), pltpu.SemaphoreType.DMA((3 * n,))],
        compiler_params=pltpu.CompilerParams(has_side_effects=True))(*bufs)


def sum_stage_a(grad, recv, place, wire, *, name):
    j_n, half, cols = recv.shape

    def body(place_ref, g_ref, r_ref, o_ref, ob_ref):
        acc = g_ref[...] + r_ref[...]
        ob_ref[...] = acc.astype(wire)

        @pl.when(pl.program_id(0) == place_ref[0])
        def _():
            o_ref[...] = acc

    blk = (None, half, cols)
    return pl.pallas_call(
        body, name=name,
        grid_spec=pltpu.PrefetchScalarGridSpec(
            num_scalar_prefetch=1, grid=(j_n,),
            in_specs=[pl.BlockSpec(blk, lambda j, place_ref: (j, place_ref[1], 0)),
                      pl.BlockSpec(blk, lambda j, place_ref: (j, 0, 0))],
            out_specs=[pl.BlockSpec((half, cols), lambda j, place_ref: (0, 0)),
                       pl.BlockSpec(blk, lambda j, place_ref: (j, 0, 0))]),
        out_shape=[jax.ShapeDtypeStruct((half, cols), F32), jax.ShapeDtypeStruct(recv.shape, wire)],
        compiler_params=_params("arbitrary"))(place, grad, recv)


def _stage_a_copies(srcs, lands, x, y, c):
    out = []
    for src, land in zip(srcs, lands):
        half = src.shape[1] // 2
        out.append((src.at[:, pl.ds((1 - c) * half, half), :], land, (x, y, 1 - c)))
    return out


def _stage_b_copies(srcs, lands, x, y, c):
    out = []
    for src, land in zip(srcs, lands):
        for k, (px, py) in enumerate(_other_chips(x, y)):
            out.append((src.at[2 * px + py], land.at[k], (px, py, c)))
    return out


def _forward_copies(bufs, _, x, y, c):
    out = []
    for buf in bufs:
        for px, py in _other_chips(x, y):
            out.append((_half(buf, 2 * px + py, c), _half(buf, 2 * px + py, c), (x, y, 1 - c)))
    return out


def _stage_c_copies(fulls, _, x, y, c):
    out = []
    for full in fulls:
        half = full.shape[0] // 2
        mine = full.at[pl.ds(c * half, half), :]
        out.append((mine, mine, (x, y, 1 - c)))
    return out


def split_start(srcs, lands, copies, *, name):
    n, n_all = len(srcs), len(srcs) + len(lands)
    n_c = len(copies(srcs, lands, 0, 0, 0))

    def body(*refs):
        src_refs, land_refs = refs[:n], refs[n:n_all]
        send_sems, recv_sems = refs[n_all], refs[n_all + 1]
        token = refs[-1]
        x, y, c = _place()
        for k, (src, dst, target) in enumerate(copies(src_refs, land_refs, x, y, c)):
            pltpu.make_async_remote_copy(src_ref=src, dst_ref=dst, send_sem=send_sems.at[k], recv_sem=recv_sems.at[k],
                                         device_id=target, device_id_type=MESH).start()
        token[...] = jnp.zeros_like(token)

    arrays = list(srcs) + list(lands)
    res = pl.pallas_call(
        body, name=name,
        out_shape=(pltpu.SemaphoreType.DMA((n_c,)), pltpu.SemaphoreType.DMA((n_c,)))
        + tuple(pltpu.HBM(a.shape, a.dtype) for a in arrays) + (jax.ShapeDtypeStruct((8, 128), F32),),
        in_specs=[HBM] * n_all,
        out_specs=tuple([SEM, SEM] + [HBM] * n_all + [pl.BlockSpec(memory_space=pltpu.VMEM)]),
        input_output_aliases={i: 2 + i for i in range(n_all)},
        compiler_params=pltpu.CompilerParams(has_side_effects=DATAFLOW))(*[_in_hbm(a) for a in arrays])
    return res[0], res[1], list(res[2:2 + n]), list(res[2 + n:2 + n_all]), res[-1]


def split_wait(srcs, lands, send, recv, copies, after, *, name):
    n, n_all = len(srcs), len(srcs) + len(lands)

    def body(*refs):
        src_refs, land_refs = refs[:n], refs[n:n_all]
        send_sems, recv_sems = refs[n_all], refs[n_all + 1]
        x, y, c = _place()
        for k, (src, dst, target) in enumerate(copies(src_refs, land_refs, x, y, c)):
            done = pltpu.make_async_remote_copy(src_ref=src, dst_ref=dst, send_sem=send_sems.at[k],
                                                recv_sem=recv_sems.at[k], device_id=target, device_id_type=MESH)
            done.wait_send()
            done.wait_recv()

    arrays = list(srcs) + list(lands)
    res = pl.pallas_call(
        body, name=name,
        out_shape=tuple(pltpu.HBM(a.shape, a.dtype) for a in arrays),
        in_specs=[HBM] * n_all + [SEM, SEM, ANY],
        out_specs=tuple([HBM] * n_all),
        input_output_aliases={i: i for i in range(n_all)},
        compiler_params=pltpu.CompilerParams(has_side_effects=DATAFLOW))(*arrays, send, recv, after)
    return list(res[:n]), list(res[n:])


def split_wait_all(sets, copies, after, *, name):
    flat = [a for srcs, lands, _, _ in sets for a in list(srcs) + list(lands)]
    n_flat = len(flat)

    def body(*refs):
        x, y, c = _place()
        pos = 0
        for si, (srcs, lands, _, _) in enumerate(sets):
            src_refs = refs[pos:pos + len(srcs)]
            land_refs = refs[pos + len(srcs):pos + len(srcs) + len(lands)]
            pos += len(srcs) + len(lands)
            send_sems, recv_sems = refs[n_flat + 2 * si], refs[n_flat + 2 * si + 1]
            for k, (src, dst, target) in enumerate(copies(src_refs, land_refs, x, y, c)):
                done = pltpu.make_async_remote_copy(src_ref=src, dst_ref=dst, send_sem=send_sems.at[k],
                                                    recv_sem=recv_sems.at[k], device_id=target, device_id_type=MESH)
                done.wait_send()
                done.wait_recv()

    sems = [s_ for _, _, send, recv in sets for s_ in (send, recv)]
    res = pl.pallas_call(
        body, name=name,
        out_shape=tuple(pltpu.HBM(a.shape, a.dtype) for a in flat),
        in_specs=[HBM] * n_flat + [SEM] * len(sems) + [ANY],
        out_specs=tuple([HBM] * n_flat),
        input_output_aliases={i: i for i in range(n_flat)},
        compiler_params=pltpu.CompilerParams(has_side_effects=DATAFLOW))(*flat, *sems, after)
    out, pos = [], 0
    for srcs, lands, _, _ in sets:
        out.append(list(res[pos + len(srcs):pos + len(srcs) + len(lands)]))
        pos += len(srcs) + len(lands)
    return out


def sum_stage_b(part, recv, place, *, name):
    half, cols = part.shape

    def body(place_ref, p_ref, r_ref, o_ref):
        acc = p_ref[...]
        for k in range(3):
            acc = acc + r_ref[k].astype(F32)
        o_ref[...] = acc

    return pl.pallas_call(
        body, name=name,
        grid_spec=pltpu.PrefetchScalarGridSpec(
            num_scalar_prefetch=1, grid=(1,),
            in_specs=[pl.BlockSpec((half, cols), lambda i, place_ref: (0, 0)),
                      pl.BlockSpec((3, half, cols), lambda i, place_ref: (0, 0, 0))],
            out_specs=pl.BlockSpec((half, cols), lambda i, place_ref: (place_ref[1], 0))),
        out_shape=jax.ShapeDtypeStruct((2 * half, cols), F32),
        compiler_params=_params("arbitrary"))(place, part, recv)


def gather_chip_blocks(slots, *, name):
    def body(in_ref, out_ref, send_sems, recv_sems):
        x, y, c = _place()
        me = 2 * x + y
        chips = _other_chips(x, y)
        for k, (px, py) in enumerate(chips):
            pltpu.make_async_remote_copy(
                src_ref=in_ref.at[me], dst_ref=out_ref.at[me],
                send_sem=send_sems.at[k], recv_sem=recv_sems.at[k],
                device_id=(px, py, c), device_id_type=MESH).start()
        for k, (px, py) in enumerate(chips):
            done = pltpu.make_async_remote_copy(
                src_ref=in_ref.at[me], dst_ref=out_ref.at[2 * px + py],
                send_sem=send_sems.at[k], recv_sem=recv_sems.at[k],
                device_id=(px, py, c), device_id_type=MESH)
            done.wait_send()
            done.wait_recv()

    return pl.pallas_call(
        body, name=name, in_specs=[ANY], out_specs=ANY,
        out_shape=jax.ShapeDtypeStruct(slots.shape, slots.dtype),
        input_output_aliases={0: 0},
        scratch_shapes=[pltpu.SemaphoreType.DMA((3,)), pltpu.SemaphoreType.DMA((3,))],
        compiler_params=pltpu.CompilerParams(has_side_effects=True))(slots)


def _ffn_bwd(dh, dh_bf, h, gain, saved, wg, wu, wd, cw, cb, place, l, pin):
    hn, rstd, g, up, gc = saved
    dg, dup, dwg, dwu, dwd, dcw, dcb = ffn_bwd_a(dh_bf, hn, g, up, gc, wd, cw, name=f"ffn{l}_bwd_a", pin=pin)
    red = _reduce_a_begin([(dwg, BF16), (dwu, BF16), (dwd, BF16)], tag=f"f{l}")
    dh_in, dh_in_bf, dgain = dx_rms_bwd([(dg, wg), (dup, wu)], h, gain, rstd, dh, name=f"ffn{l}_bwd_b",
                                        pin=red[-1])
    red = _reduce_b_begin(red, place, dh_in_bf, tag=f"f{l}")
    return dh_in, dh_in_bf, (dcw, dcb, dgain), red


def _reduce_a_begin(grads, *, tag):
    lands = [lax.empty((g.shape[0], g.shape[1] // 2, g.shape[2]), F32) for g, _ in grads]
    send, recv, srcs, lands, token = split_start([g for g, _ in grads], lands, _stage_a_copies,
                                                 name=f"reduce_a_start_{tag}")
    return [w for _, w in grads], send, recv, srcs, lands, token


def _reduce_b_begin(state, place, after, *, tag):
    wires, send, recv, srcs, lands, _ = state
    grads, recv_a = split_wait(srcs, lands, send, recv, _stage_a_copies, after, name=f"reduce_a_wait_{tag}")
    parts = [sum_stage_a(g, r, place, w, name=f"sum_a_{tag}{i}") for i, (g, r, w) in enumerate(zip(grads, recv_a, wires))]
    lands_b = [lax.empty((3,) + p[1].shape[1:], p[1].dtype) for p in parts]
    send, recv, srcs, lands, token = split_start([p[1] for p in parts], lands_b, _stage_b_copies,
                                                 name=f"reduce_b_start_{tag}")
    return [p[0] for p in parts], send, recv, srcs, lands, token


def _reduce_c_begin(state, place, after, *, tag):
    parts, send, recv, srcs, lands, _ = state
    _, recv_b = split_wait(srcs, lands, send, recv, _stage_b_copies, after, name=f"reduce_b_wait_{tag}")
    halves = [sum_stage_b(p, r, place, name=f"sum_b_{tag}{i}") for i, (p, r) in enumerate(zip(parts, recv_b))]
    send, recv, fulls, _, token = split_start(halves, [], _stage_c_copies, name=f"reduce_c_start_{tag}")
    return send, recv, fulls, token


def _reduce_finish(state, after, *, tag):
    send, recv, fulls, _ = state
    fulls, _ = split_wait(fulls, [], send, recv, _stage_c_copies, after, name=f"reduce_c_wait_{tag}")
    return fulls


def kernel(x, norm_mix, norm_ffn, final_norm, w_in_even, conv_a, w_pool, pool_scale, w_out_even, w_in_odd, sgu_norm, w_spatial, b_spatial, w_out_odd, w_ffn_gate, w_ffn_up, conv_ffn, b_conv_ffn, w_ffn_down, loss_target, m_norm_mix, m_norm_ffn, m_final_norm, m_w_in_even, m_conv_a, m_w_pool, m_pool_scale, m_w_out_even, m_w_in_odd, m_sgu_norm, m_w_spatial, m_b_spatial, m_w_out_odd, m_w_ffn_gate, m_w_ffn_up, m_conv_ffn, m_b_conv_ffn, m_w_ffn_down, v_norm_mix, v_norm_ffn, v_final_norm, v_w_in_even, v_conv_a, v_w_pool, v_pool_scale, v_w_out_even, v_w_in_odd, v_sgu_norm, v_w_spatial, v_b_spatial, v_w_out_odd, v_w_ffn_gate, v_w_ffn_up, v_conv_ffn, v_b_conv_ffn, v_w_ffn_down):
    weights = dict(norm_mix=norm_mix, norm_ffn=norm_ffn, final_norm=final_norm, w_in_even=w_in_even,
                   conv_a=conv_a, w_pool=w_pool, pool_scale=pool_scale, w_out_even=w_out_even,
                   w_in_odd=w_in_odd, sgu_norm=sgu_norm, w_spatial=w_spatial, b_spatial=b_spatial,
                   w_out_odd=w_out_odd, w_ffn_gate=w_ffn_gate, w_ffn_up=w_ffn_up, conv_ffn=conv_ffn,
                   b_conv_ffn=b_conv_ffn, w_ffn_down=w_ffn_down)
    m_in = dict(norm_mix=m_norm_mix, norm_ffn=m_norm_ffn, final_norm=m_final_norm, w_in_even=m_w_in_even,
                conv_a=m_conv_a, w_pool=m_w_pool, pool_scale=m_pool_scale, w_out_even=m_w_out_even,
                w_in_odd=m_w_in_odd, sgu_norm=m_sgu_norm, w_spatial=m_w_spatial, b_spatial=m_b_spatial,
                w_out_odd=m_w_out_odd, w_ffn_gate=m_w_ffn_gate, w_ffn_up=m_w_ffn_up, conv_ffn=m_conv_ffn,
                b_conv_ffn=m_b_conv_ffn, w_ffn_down=m_w_ffn_down)
    v_in = dict(norm_mix=v_norm_mix, norm_ffn=v_norm_ffn, final_norm=v_final_norm, w_in_even=v_w_in_even,
                conv_a=v_conv_a, w_pool=v_w_pool, pool_scale=v_pool_scale, w_out_even=v_w_out_even,
                w_in_odd=v_w_in_odd, sgu_norm=v_sgu_norm, w_spatial=v_w_spatial, b_spatial=v_b_spatial,
                w_out_odd=v_w_out_odd, w_ffn_gate=v_w_ffn_gate, w_ffn_up=v_w_ffn_up, conv_ffn=v_conv_ffn,
                b_conv_ffn=v_b_conv_ffn, w_ffn_down=v_w_ffn_down)
    order = list(weights)

    chip = 2 * lax.axis_index("x") + lax.axis_index("y")
    core = lax.axis_index("c")
    place = jnp.stack([chip, core]).astype(jnp.int32)
    chip_arr = place[:1]

    h0 = x[0]
    target = loss_target[0]
    d_model = h0.shape[1]
    f_shard = w_ffn_gate.shape[-1]

    def turned(a):
        return jnp.transpose(a, (0, 2, 1))

    def own_slot(v):
        return lax.dynamic_update_index_in_dim(jnp.zeros((N_CHIPS,) + v.shape, v.dtype), v, chip, 0)

    first = [[cast_into_slot(w_in_even[0], chip_arr, name="cast_win_e"),
              cast_into_slot(w_out_even[0], chip_arr, name="cast_wout_e")]]
    smalls = [own_slot(conv_a[0]), own_slot(sgu_norm), own_slot(conv_ffn[0]), own_slot(conv_ffn[1])]
    sems0, first, smalls, token0 = gather_start(first, smalls, name="gather_start0")
    rest = [
        [cast_into_slot(turned(w_ffn_gate), chip_arr, l=0, name="cast_wg0", pin=token0),
         cast_into_slot(turned(w_ffn_up), chip_arr, l=0, name="cast_wu0", pin=token0)],
        [cast_into_slot(w_ffn_down, chip_arr, l=0, name="cast_wd0", pin=token0)],
        [cast_into_slot(w_in_odd[0], chip_arr, name="cast_win_o", pin=token0),
         cast_into_slot(w_out_odd[0], chip_arr, name="cast_wout_o", pin=token0)],
        [cast_into_slot(turned(w_ffn_gate), chip_arr, l=1, name="cast_wg1", pin=token0),
         cast_into_slot(turned(w_ffn_up), chip_arr, l=1, name="cast_wu1", pin=token0)],
        [cast_into_slot(w_ffn_down, chip_arr, l=1, name="cast_wd1", pin=token0)]]
    sems1, rest, _, token = gather_start(rest, [], name="gather_start1")
    groups = first + rest
    sems = list(sems0[:2]) + list(sems1) + list(sems0[2:])

    def arrive(gi, after, with_smalls=False):
        kw = dict(smalls=smalls, small_send=sems[-2], small_recv=sems[-1]) if with_smalls else {}
        bufs, small_out = gather_wait(groups[gi], sems[2 * gi], sems[2 * gi + 1], after, name=f"gather_wait{gi}", **kw)
        return gather_forward(bufs, name=f"gather_forward{gi}"), small_out

    def arrive_begin(gi, after):
        bufs, _ = gather_wait(groups[gi], sems[2 * gi], sems[2 * gi + 1], after, name=f"gather_wait{gi}")
        send, recv, bufs, _, tok = split_start(bufs, [], _forward_copies, name=f"gather_forward_start{gi}")
        return (send, recv, bufs), tok

    def arrive_end(state, after, gi):
        send, recv, bufs = state
        return split_wait(bufs, [], send, recv, _forward_copies, after, name=f"gather_forward_wait{gi}")[0]

    cb = b_conv_ffn.reshape(-1, N_CHIPS, 1, f_shard)
    wp_bf = w_pool[0].astype(BF16)
    wp_t_bf = jnp.transpose(w_pool[0], (0, 2, 1)).astype(BF16)
    ws = w_spatial[0]
    bs = b_spatial[0][:, :, None]

    (win_e, wout_e), (ca_g, sn_g, cw0, cw1) = arrive(0, token, with_smalls=True)
    wout_e = wout_e.reshape(-1, d_model)
    ca_full = jnp.transpose(ca_g, (1, 0, 2)).reshape(ca_g.shape[1], -1)
    sn_full = sn_g.reshape(1, -1)
    h1, xn0, rstd0, proj0, mix0, hn0, rstdf0 = even_layer_fwd(
        h0, norm_mix[0:1], win_e, ca_full, wp_bf, pool_scale, wout_e, norm_ffn[0:1], name="l0_fwd")
    (wg0, wu0), _ = arrive(1, hn0)
    g0, up0, gc0, act0 = ffn_in_fwd(hn0, wg0, wu0, cw0, cb[0], name="ffn0_in")
    swap2, tok2 = arrive_begin(2, act0)
    swap3, tok3 = arrive_begin(3, tok2)
    (wd0,) = arrive_end(swap2, tok3, 2)
    h2 = mm_acc(act0, wd0, h1, name="ffn0_down")
    ffn0 = (hn0, rstdf0, g0, up0, gc0)
    swap4, tok4 = arrive_begin(4, h2)
    swap5, tok5 = arrive_begin(5, tok4)
    win_o, wout_o = arrive_end(swap3, tok5, 3)
    wout_o = wout_o.reshape(-1, d_model)
    h3, xn1, rstd1, p1, mix1, rstd_v = odd_layer_fwd(h2, norm_mix[1:2], win_o, sn_full, ws, bs, wout_o, name="l1_fwd")
    wg1, wu1 = arrive_end(swap4, h3, 4)
    (wd1,) = arrive_end(swap5, wg1, 5)
    dh4, dh4_bf, loss_row, d_final, hn1, rstdf1, g1, up1, gc1 = ffn_loss_fwd(
        h3, norm_ffn[1:2], wg1, wu1, wd1, cw1, cb[1], target, final_norm[None], name="ffn1_fwd_loss")
    ffn1 = (hn1, rstdf1, g1, up1, gc1)

    loss = lax.psum(loss_row[0, 0], ("x", "y", "c"))

    dh3, dh3_bf, (dcw1, dcb1, dnf1), red3 = _ffn_bwd(
        dh4, dh4_bf, h3, norm_ffn[1:2], ffn1, wg1, wu1, wd1, cw1, cb[1], place, 1, None)

    def as_blocks(a):
        return a.reshape(N_CHIPS, -1, d_model)

    dp1, dwin_o, dwout_o, dsn, dws, dbs = sgu_bwd(p1, mix1, xn1, dh3_bf, wout_o, rstd_v, sn_full, ws, bs,
                                                  name="l1_mix_bwd", pin=red3[-1])
    dwout_o = as_blocks(dwout_o)
    red2 = _reduce_a_begin([(dwin_o, BF16), (dwout_o, BF16)], tag="m1")
    dh2, dh2_bf, dnm1 = dx_rms_bwd([(dp1, win_o)], h2, norm_mix[1:2], rstd1, dh3, name="l1_dx", pin=red2[-1])
    red2 = _reduce_b_begin(red2, place, dh2_bf, tag="m1")

    dh1, dh1_bf, (dcw0, dcb0, dnf0), red1 = _ffn_bwd(
        dh2, dh2_bf, h1, norm_ffn[0:1], ffn0, wg0, wu0, wd0, cw0, cb[0], place, 0, red2[-1])

    dproj0, dwin_e, dwout_e, dca, dwp, dps = even_bwd(proj0, mix0, xn0, dh1_bf, wout_e, ca_full, wp_bf, wp_t_bf,
                                                      pool_scale, name="l0_mix_bwd", pin=red1[-1])
    dwout_e = as_blocks(dwout_e)
    dh0, _, dnm0 = dx_rms_bwd([(dproj0, win_e)], h0, norm_mix[0:1], rstd0, dh1, name="l0_dx")
    grad_x = dh0[None]

    small_parts = dict(
        norm_mix=jnp.concatenate([dnm0, dnm1]), norm_ffn=jnp.concatenate([dnf0, dnf1]), final_norm=d_final,
        conv_a=dca, w_pool=dwp, pool_scale=dps, sgu_norm=dsn, w_spatial=dws, b_spatial=dbs,
        conv_ffn=jnp.stack([dcw0, dcw1]), b_conv_ffn=jnp.stack([dcb0, dcb1]))
    flat = jnp.concatenate([v.reshape(-1) for v in small_parts.values()])
    pad = (-flat.shape[0]) % (N_CHIPS * 32 * 128)
    small = jnp.pad(flat, (0, pad)).reshape(N_CHIPS, -1, 128)
    red0 = _reduce_a_begin([(dwin_e, BF16), (dwout_e, BF16), (small, F32)], tag="m0")

    early = [(red3, "f1"), (red2, "m1"), (red1, "f0")]
    landed = split_wait_all([(st[3], st[4], st[1], st[2]) for st, _ in early], _stage_b_copies, red0[-1],
                            name="reduce_b_wait_early")
    halves = [sum_stage_b(p, r, place, name=f"sum_b_{tag}{i}")
              for (st, tag), recv_b in zip(early, landed) for i, (p, r) in enumerate(zip(st[0], recv_b))]
    send_c, recv_c, halves, _, tok_c = split_start(halves, [], _stage_c_copies, name="reduce_c_start_early")
    red0 = _reduce_b_begin(red0, place, tok_c, tag="m0")
    fulls = _reduce_finish((send_c, recv_c, halves, tok_c), red0[-1], tag="early")
    full3, full2, full1 = fulls[0:3], fulls[3:5], fulls[5:8]
    swap_m0 = _reduce_c_begin(red0, place, full1[0], tag="m0")
    full0 = _reduce_finish(swap_m0, swap_m0[-1], tag="m0")
    small_slots = lax.dynamic_update_index_in_dim(jnp.zeros(small.shape, F32), full0[2], chip, 0)
    small_sum = gather_chip_blocks(small_slots, name="gather_small").reshape(-1)
    grads = {
        "w_in_even": full0[0][None], "w_out_even": full0[1][None],
        "w_in_odd": full2[0][None], "w_out_odd": full2[1][None],
        }
    layered = {"w_ffn_gate": [full1[0], full3[0]], "w_ffn_up": [full1[1], full3[1]],
               "w_ffn_down": [full1[2], full3[2]]}
    off = 0
    small_red = {}
    for nm, v in small_parts.items():
        small_red[nm] = small_sum[off:off + v.size].reshape(v.shape)
        off += v.size
    for nm in ("norm_mix", "norm_ffn", "pool_scale"):
        grads[nm] = small_red[nm].reshape(weights[nm].shape)
    grads["final_norm"] = small_red["final_norm"].reshape(weights["final_norm"].shape)
    grads["w_pool"] = small_red["w_pool"][None]
    grads["w_spatial"] = small_red["w_spatial"][None]
    grads["b_spatial"] = small_red["b_spatial"].reshape(weights["b_spatial"].shape)
    grads["b_conv_ffn"] = small_red["b_conv_ffn"].reshape(weights["b_conv_ffn"].shape)
    grads["conv_a"] = lax.dynamic_slice_in_dim(small_red["conv_a"], chip * conv_a.shape[-1], conv_a.shape[-1], 1)[None]
    grads["sgu_norm"] = lax.dynamic_slice_in_dim(small_red["sgu_norm"], chip * sgu_norm.shape[-1], sgu_norm.shape[-1], 1)
    grads["conv_ffn"] = lax.dynamic_index_in_dim(small_red["conv_ffn"], chip, 1, keepdims=False)

    deltas, new_m, new_v = {}, {}, {}
    for nm, per_layer in layered.items():
        if nm == "w_ffn_down":
            grads[nm], deltas[nm], new_m[nm], new_v[nm] = adamw_layers(
                weights[nm], per_layer, m_in[nm], v_in[nm], name=f"adamw_{nm}")
        else:
            outs = adamw_layers(turned(weights[nm]), per_layer, turned(m_in[nm]), turned(v_in[nm]),
                                name=f"adamw_{nm}")
            grads[nm], deltas[nm], new_m[nm], new_v[nm] = (turned(o) for o in outs)
    for nm in order:
        if nm in layered:
            continue
        w = weights[nm]
        w2 = w[None] if w.ndim == 1 else w
        shp = w2.shape
        d, nm_, nv_ = adamw(w2, grads[nm].reshape(shp), m_in[nm].reshape(shp), v_in[nm].reshape(shp),
                            name=f"adamw_{nm}")
        deltas[nm], new_m[nm], new_v[nm] = d.reshape(w.shape), nm_.reshape(w.shape), nv_.reshape(w.shape)

    return (loss, grad_x, *[grads[n] for n in order], *[deltas[n] for n in order],
            *[new_m[n] for n in order], *[new_v[n] for n in order])
```

```python
import jax
import jax.numpy as jnp
from jax import lax
from jax.experimental import pallas as pl
from jax.experimental.pallas import tpu as pltpu

F32 = jnp.float32
BF16 = jnp.bfloat16
MESH = pl.DeviceIdType.MESH
ANY = pl.BlockSpec(memory_space=pl.ANY)

EPS = 1e-6
POOL_WINDOWS = (2, 4, 8, 16)
GROUP = 128
CHUNK = 128
N_CHIPS = 4
N_DEV = 8
ROW_TILE = 512
HALO = 16
VMEM_LIMIT = 56 * 1024 * 1024

ADAM_LR = 0.001
ADAM_B1 = 0.9
ADAM_B2 = 0.999
ADAM_EPS = 1e-08
ADAM_WD = 0.01
ADAM_STEP = 10


def _params(*sem):
    return pltpu.CompilerParams(dimension_semantics=sem, vmem_limit_bytes=VMEM_LIMIT)


def mm_acc(a, b, res, *, name):
    j_n, s, kj = a.shape
    n = b.shape[-1]
    tm = min(ROW_TILE, s)

    def body(a_ref, b_ref, r_ref, o_ref):
        acc = r_ref[...]
        for j in range(j_n):
            acc = acc + jnp.dot(a_ref[j], b_ref[j], preferred_element_type=F32)
        o_ref[...] = acc

    return pl.pallas_call(
        body, name=name, grid=(s // tm,),
        in_specs=[pl.BlockSpec((j_n, tm, kj), lambda i: (0, i, 0)),
                  pl.BlockSpec((j_n, kj, n), lambda i: (0, 0, 0)),
                  pl.BlockSpec((tm, n), lambda i: (i, 0))],
        out_specs=pl.BlockSpec((tm, n), lambda i: (i, 0)),
        out_shape=jax.ShapeDtypeStruct((s, n), F32),
        compiler_params=_params("parallel"))(a, b, res)


_NT = (((1,), (1,)), ((), ()))
_TN = (((0,), (0,)), ((), ()))


def _back(x, k):
    return pltpu.roll(x, k, 0)


def _fwd(x, k):
    return pltpu.roll(x, x.shape[0] - k, 0)


def _causal_conv(x, w_ref):
    return w_ref[0:1, :] * _back(x, 2) + w_ref[1:2, :] * _back(x, 1) + w_ref[2:3, :] * x


def _causal_conv_t(dy, w_ref):
    return w_ref[2:3, :] * dy + w_ref[1:2, :] * _fwd(dy, 1) + w_ref[0:1, :] * _fwd(dy, 2)


def _gelu(x):
    return 0.5 * x * (1.0 + lax.erf(x * 0.7071067811865476))


def _gelu_grad(x):
    return 0.5 * (1.0 + lax.erf(x * 0.7071067811865476)) + x * jnp.exp(-0.5 * x * x) * 0.3989422804014327


def _colsum(x):
    return jnp.sum(x, axis=0, keepdims=True)


def _halo_specs(n_lead, ts, width, n_tiles):
    hb = ts // HALO
    prev = pl.BlockSpec((n_lead, HALO, width), lambda i: (0, jnp.maximum(i * hb - 1, 0), 0))
    nxt = pl.BlockSpec((n_lead, HALO, width), lambda i: (0, jnp.minimum((i + 1) * hb, n_tiles * hb - 1), 0))
    return prev, nxt


def _pool_fwd(z_ext, g, pos):
    w = POOL_WINDOWS[g]
    zg = z_ext[:, g * GROUP:(g + 1) * GROUP]
    acc = zg
    sh = 1
    while sh < w:
        acc = acc + _back(acc, sh)
        sh *= 2
    return acc[HALO:] / jnp.minimum(pos, float(w)) - zg[HALO:]


def even_layer_fwd(h, gain, win, conv_a, w_pool, pool_scale, wout, gain_next, *, name, pin=None):
    s, d = h.shape
    w = win.shape[-1]
    ts = min(ROW_TILE, s)
    hb = ts // HALO

    def body(h_ref, hp_ref, gain_ref, win_ref, ca_ref, wp_ref, ps_ref, wout_ref, gn_ref, *rest):
        o_ref, xn_ref, r_ref, p_ref, m_ref, hn_ref, rn_ref = rest[-7:]
        i = pl.program_id(0)
        keep = jnp.where(i > 0, 1.0, 0.0)
        h_ext = jnp.concatenate([hp_ref[...], h_ref[...]], axis=0)
        rstd = lax.rsqrt(jnp.mean(h_ext * h_ext, axis=-1, keepdims=True) + EPS)
        xn_ext = (h_ext * rstd * gain_ref[...]).astype(BF16)
        xn_ref[...] = xn_ext[HALO:]
        r_ref[...] = rstd[HALO:]
        p32 = []
        for k in range(4):
            pk = jnp.dot(xn_ext, win_ref[k], preferred_element_type=F32).astype(BF16)
            p_ref[k] = pk[HALO:]
            pk = pk.astype(F32)
            p32.append(jnp.concatenate([pk[:HALO] * keep, pk[HALO:]], axis=0))
        m_ref[:, 0:w] = (p32[0][HALO:] * _causal_conv(p32[1] * p32[2], ca_ref)[HALO:]).astype(BF16)
        pos = (i * ts + lax.broadcasted_iota(jnp.int32, (ts, 1), 0) + 1).astype(F32)
        for g in range(len(POOL_WINDOWS)):
            pooled = _pool_fwd(p32[3], g, pos)
            mixed = jnp.dot(pooled.astype(BF16), wp_ref[g], preferred_element_type=F32)
            cols = slice(g * GROUP, (g + 1) * GROUP)
            m_ref[:, w + g * GROUP:w + (g + 1) * GROUP] = (mixed * ps_ref[:, cols]).astype(BF16)
        out = h_ref[...] + jnp.dot(m_ref[...], wout_ref[...], preferred_element_type=F32)
        o_ref[...] = out
        rstd_n = lax.rsqrt(jnp.mean(out * out, axis=-1, keepdims=True) + EPS)
        hn_ref[...] = (out * rstd_n * gn_ref[...]).astype(BF16)
        rn_ref[...] = rstd_n

    const = lambda shape: pl.BlockSpec(shape, lambda i: (0,) * len(shape))
    row = pl.BlockSpec((ts, d), lambda i: (i, 0))
    col1 = pl.BlockSpec((ts, 1), lambda i: (i, 0))
    return pl.pallas_call(
        body, name=name, grid=(s // ts,),
        in_specs=[row, pl.BlockSpec((HALO, d), lambda i: (jnp.maximum(i * hb - 1, 0), 0)), const((1, d)),
                  const((4, d, w)), const((3, w)), const((4, GROUP, GROUP)), const((1, w)), const((2 * w, d)),
                  const((1, d))] + ([ANY] if pin is not None else []),
        out_specs=[row, row, col1, pl.BlockSpec((4, ts, w), lambda i: (0, i, 0)),
                   pl.BlockSpec((ts, 2 * w), lambda i: (i, 0)), row, col1],
        out_shape=[jax.ShapeDtypeStruct((s, d), F32), jax.ShapeDtypeStruct((s, d), BF16),
                   jax.ShapeDtypeStruct((s, 1), F32), jax.ShapeDtypeStruct((4, s, w), BF16),
                   jax.ShapeDtypeStruct((s, 2 * w), BF16), jax.ShapeDtypeStruct((s, d), BF16),
                   jax.ShapeDtypeStruct((s, 1), F32)],
        compiler_params=_params("parallel"))(
            h, h, gain, win, conv_a, w_pool, pool_scale, wout, gain_next, *([pin] if pin is not None else []))


def even_bwd(proj, mix, xn, dh_bf, wout, conv_a, w_pool, w_pool_t, pool_scale, *, name, pin=None):
    _, s, w = proj.shape
    d = dh_bf.shape[1]
    ts = min(ROW_TILE, s)
    n_t = s // ts
    prev, nxt = _halo_specs(4, ts, w, n_t)
    hb = ts // HALO
    n_ext = ts + HALO

    def body(p_ref, pp_ref, pn_ref, m_ref, x_ref, dh_ref, dhn_ref, wout_ref, ca_ref, wp_ref, wpt_ref, ps_ref, *rest):
        dp_ref, dwi_ref, dwo_ref, dca_ref, dwp_ref, dps_ref = rest[-6:]
        i = pl.program_id(0)

        @pl.when(i == 0)
        def _():
            dwi_ref[...] = jnp.zeros_like(dwi_ref)
            dwo_ref[...] = jnp.zeros_like(dwo_ref)
            dca_ref[...] = jnp.zeros_like(dca_ref)
            dwp_ref[...] = jnp.zeros_like(dwp_ref)
            dps_ref[...] = jnp.zeros_like(dps_ref)

        keep_p = jnp.where(i > 0, 1.0, 0.0)
        keep_n = jnp.where(i < n_t - 1, 1.0, 0.0)
        dwo_ref[...] += lax.dot_general(m_ref[...], dh_ref[...], _TN, preferred_element_type=F32)
        dmix = lax.dot_general(jnp.concatenate([dh_ref[...], dhn_ref[...]], axis=0), wout_ref[...], _NT,
                               preferred_element_type=F32)
        a_b, a_c, a_v = (p_ref[k].astype(F32) for k in range(3))
        cv_ext = jnp.concatenate([pp_ref[1].astype(F32) * pp_ref[2].astype(F32) * keep_p, a_c * a_v], axis=0)
        dy_a = dmix[:ts, 0:w]
        dp_ref[0] = (dy_a * _causal_conv(cv_ext, ca_ref)[HALO:]).astype(BF16)
        dcc = dy_a * a_b
        dca_ref[2:3, :] += _colsum(dcc * cv_ext[HALO:])
        dca_ref[1:2, :] += _colsum(dcc * _back(cv_ext, 1)[HALO:])
        dca_ref[0:1, :] += _colsum(dcc * _back(cv_ext, 2)[HALO:])
        dcc_ext = jnp.concatenate([dcc, dmix[ts:, 0:w] * pn_ref[0].astype(F32) * keep_n], axis=0)
        dcv = _causal_conv_t(dcc_ext, ca_ref)[:ts]
        dp_ref[1] = (dcv * a_v).astype(BF16)
        dp_ref[2] = (dcv * a_c).astype(BF16)
        z_ext = jnp.concatenate([pp_ref[3].astype(F32) * keep_p, p_ref[3].astype(F32)], axis=0)
        pos = (i * ts + lax.broadcasted_iota(jnp.int32, (ts, 1), 0) + 1).astype(F32)
        pos_ext = (i * ts + lax.broadcasted_iota(jnp.int32, (n_ext, 1), 0) + 1).astype(F32)
        for g, win in enumerate(POOL_WINDOWS):
            cols = slice(g * GROUP, (g + 1) * GROUP)
            ycols = slice(w + g * GROUP, w + (g + 1) * GROUP)
            pooled = _pool_fwd(z_ext, g, pos).astype(BF16)
            mixed = jnp.dot(pooled, wp_ref[g], preferred_element_type=F32)
            dy_b = dmix[:ts, ycols]
            dps_ref[:, cols] += _colsum(dy_b * mixed)
            dmixed_ext = jnp.concatenate([dy_b, dmix[ts:, ycols] * keep_n], axis=0) * ps_ref[:, cols]
            dmixed_ext = dmixed_ext.astype(BF16)
            dwp_ref[g] += lax.dot_general(pooled, dmixed_ext[:ts], _TN, preferred_element_type=F32)
            dpooled = jnp.dot(dmixed_ext, wpt_ref[g], preferred_element_type=F32)
            acc = dpooled / jnp.minimum(pos_ext, float(win))
            sh = 1
            while sh < win:
                acc = acc + _fwd(acc, sh)
                sh *= 2
            dp_ref[3, :, cols] = (acc[:ts] - dpooled[:ts]).astype(BF16)
        for k in range(4):
            dwi_ref[k] += lax.dot_general(x_ref[...], dp_ref[k], _TN, preferred_element_type=F32)

    tile4 = pl.BlockSpec((4, ts, w), lambda i: (0, i, 0))
    const = lambda shape: pl.BlockSpec(shape, lambda i: (0,) * len(shape))
    return pl.pallas_call(
        body, name=name, grid=(n_t,),
        in_specs=[tile4, prev, nxt, pl.BlockSpec((ts, 2 * w), lambda i: (i, 0)), pl.BlockSpec((ts, d), lambda i: (i, 0)),
                  pl.BlockSpec((ts, d), lambda i: (i, 0)),
                  pl.BlockSpec((HALO, d), lambda i: (jnp.minimum((i + 1) * hb, n_t * hb - 1), 0)),
                  const((2 * w, d)), const((3, w)), const((4, GROUP, GROUP)), const((4, GROUP, GROUP)), const((1, w))]
        + ([ANY] if pin is not None else []),
        out_specs=[tile4, const((4, d, w)), const((2 * w, d)), const((3, w)), const((4, GROUP, GROUP)), const((1, w))],
        out_shape=[jax.ShapeDtypeStruct((4, s, w), BF16), jax.ShapeDtypeStruct((4, d, w), F32),
                   jax.ShapeDtypeStruct((2 * w, d), F32), jax.ShapeDtypeStruct((3, w), F32),
                   jax.ShapeDtypeStruct((4, GROUP, GROUP), F32), jax.ShapeDtypeStruct((1, w), F32)],
        compiler_params=_params("arbitrary"))(
            proj, proj, proj, mix, xn, dh_bf, dh_bf, wout, conv_a, w_pool, w_pool_t, pool_scale,
            *([pin] if pin is not None else []))


def _ffn_next_halo(ts, f, n_t):
    hb = ts // HALO
    return pl.BlockSpec((None, HALO, f), lambda j, i: (j, jnp.minimum((i + 1) * hb, n_t * hb - 1), 0))


def ffn_in_fwd(hn, wg, wu, cw, cb, *, name, pin=None):
    s, d = hn.shape
    j_n, f, _ = wg.shape
    tm = min(ROW_TILE, s)
    hb = tm // HALO

    def body(x_ref, xp_ref, wg_ref, wu_ref, cw_ref, cb_ref, *rest):
        g_ref, u_ref, gc_ref, a_ref = rest[-4:]
        i, j = pl.program_id(0), pl.program_id(1)
        x_ext = jnp.concatenate([xp_ref[...], x_ref[...]], axis=0)
        g_ext = lax.dot_general(x_ext, wg_ref[j], _NT, preferred_element_type=F32).astype(BF16)
        up = lax.dot_general(x_ref[...], wu_ref[j], _NT, preferred_element_type=F32).astype(BF16)
        g_ref[...] = g_ext[HALO:]
        u_ref[...] = up
        a_ref[...] = _ffn_act(g_ext, up, cw_ref, cb_ref, gc_ref, i)

    whole = pl.BlockSpec((j_n, f, d), lambda i, j: (0, 0, 0))
    tile = pl.BlockSpec((None, tm, f), lambda i, j: (j, i, 0))
    shape = jax.ShapeDtypeStruct((j_n, s, f), BF16)
    return pl.pallas_call(
        body, name=name, grid=(s // tm, j_n),
        in_specs=[pl.BlockSpec((tm, d), lambda i, j: (i, 0)),
                  pl.BlockSpec((HALO, d), lambda i, j: (jnp.maximum(i * hb - 1, 0), 0)),
                  whole, whole,
                  pl.BlockSpec((None, 3, f), lambda i, j: (j, 0, 0)),
                  pl.BlockSpec((None, 1, f), lambda i, j: (j, 0, 0))] + ([ANY] if pin is not None else []),
        out_specs=[tile] * 4, out_shape=[shape] * 4,
        compiler_params=_params("parallel", "parallel"))(hn, hn, wg, wu, cw, cb, *([pin] if pin is not None else []))


def _ffn_act(g_ext, up, cw_ref, cb_ref, gc_ref, i):
    keep = jnp.where(i > 0, 1.0, 0.0)
    g32 = jnp.concatenate([g_ext[:HALO].astype(F32) * keep, g_ext[HALO:].astype(F32)], axis=0)
    gc = (_causal_conv(g32, cw_ref)[HALO:] + cb_ref[...]).astype(BF16)
    gc_ref[...] = gc
    gc = gc.astype(F32)
    return (gc * jax.nn.sigmoid(gc) * up.astype(F32)).astype(BF16)


def ffn_loss_fwd(h, gain, wg, wu, wd, cw, cb, target, final_gain, *, name):
    s, d = h.shape
    j_n, f, _ = wg.shape
    tm = min(ROW_TILE, s)
    hb = tm // HALO

    def body(h_ref, hp_ref, gain_ref, wg_ref, wu_ref, wd_ref, cw_ref, cb_ref, t_ref, fg_ref,
             o_ref, ob_ref, l_ref, dfg_ref, xn_ref, r_ref, g_ref, u_ref, gc_ref, x_s, acc_s):
        i, j = pl.program_id(0), pl.program_id(1)

        @pl.when((i == 0) & (j == 0))
        def _():
            l_ref[...] = jnp.zeros_like(l_ref)
            dfg_ref[...] = jnp.zeros_like(dfg_ref)

        @pl.when(j == 0)
        def _():
            h_ext = jnp.concatenate([hp_ref[...], h_ref[...]], axis=0)
            rstd = lax.rsqrt(jnp.mean(h_ext * h_ext, axis=-1, keepdims=True) + EPS)
            x_s[...] = (h_ext * rstd * gain_ref[...]).astype(BF16)
            xn_ref[...] = x_s[HALO:, :]
            r_ref[...] = rstd[HALO:]
            acc_s[...] = h_ref[...]

        g_ext = lax.dot_general(x_s[...], wg_ref[j], _NT, preferred_element_type=F32).astype(BF16)
        up = lax.dot_general(x_s[HALO:, :], wu_ref[j], _NT, preferred_element_type=F32).astype(BF16)
        g_ref[...] = g_ext[HALO:]
        u_ref[...] = up
        act = _ffn_act(g_ext, up, cw_ref, cb_ref, gc_ref, i)
        acc_s[...] += jnp.dot(act, wd_ref[j], preferred_element_type=F32)

        @pl.when(j == j_n - 1)
        def _():
            x = acc_s[...]
            rstd = lax.rsqrt(jnp.mean(x * x, axis=-1, keepdims=True) + EPS)
            hhat = x * rstd
            err = hhat * fg_ref[...] - t_ref[...]
            l_ref[...] += 0.5 * jnp.sum(jnp.mean(err * err, axis=-1, keepdims=True), axis=0, keepdims=True)
            dy = err * (1.0 / d)
            dfg_ref[...] += _colsum(dy * hhat)
            dyg = dy * fg_ref[...]
            dh = rstd * (dyg - hhat * jnp.mean(dyg * hhat, axis=-1, keepdims=True))
            o_ref[...] = dh
            ob_ref[...] = dh.astype(BF16)

    whole = pl.BlockSpec((j_n, f, d), lambda i, j: (0, 0, 0), pipeline_mode=pl.Buffered(1))
    row = pl.BlockSpec((tm, d), lambda i, j: (i, 0))
    vec = pl.BlockSpec((1, d), lambda i, j: (0, 0))
    tile = pl.BlockSpec((None, tm, f), lambda i, j: (j, i, 0))
    return pl.pallas_call(
        body, name=name, grid=(s // tm, j_n),
        in_specs=[row, pl.BlockSpec((HALO, d), lambda i, j: (jnp.maximum(i * hb - 1, 0), 0)),
                  vec, whole, whole, whole,
                  pl.BlockSpec((None, 3, f), lambda i, j: (j, 0, 0)),
                  pl.BlockSpec((None, 1, f), lambda i, j: (j, 0, 0)), row, vec],
        out_specs=[row, row, pl.BlockSpec((1, 128), lambda i, j: (0, 0)), vec,
                   row, pl.BlockSpec((tm, 1), lambda i, j: (i, 0)), tile, tile, tile],
        out_shape=[jax.ShapeDtypeStruct((s, d), F32), jax.ShapeDtypeStruct((s, d), BF16),
                   jax.ShapeDtypeStruct((1, 128), F32), jax.ShapeDtypeStruct((1, d), F32),
                   jax.ShapeDtypeStruct((s, d), BF16), jax.ShapeDtypeStruct((s, 1), F32)]
        + [jax.ShapeDtypeStruct((j_n, s, f), BF16)] * 3,
        scratch_shapes=[pltpu.VMEM((HALO + tm, d), BF16), pltpu.VMEM((tm, d), F32)],
        compiler_params=_params("arbitrary", "arbitrary"))(h, h, gain, wg, wu, wd, cw, cb, target, final_gain)


def ffn_bwd_a(dh_bf, hn, g, up, gc, wd, cw, *, name, pin=None):
    s, d = hn.shape
    j_n, _, f = g.shape
    tm = min(ROW_TILE, s)
    n_t = s // tm
    hb = tm // HALO

    def body(dh_ref, dhn_ref, x_ref, g_ref, gc_ref, gcn_ref, u_ref, un_ref, wd_ref, cw_ref, *rest):
        dg_ref, du_ref, dwg_ref, dwu_ref, dwd_ref, dcw_ref, dcb_ref = rest[-7:]
        i = pl.program_id(1)

        @pl.when(i == 0)
        def _():
            for r in (dwg_ref, dwu_ref, dwd_ref, dcw_ref, dcb_ref):
                r[...] = jnp.zeros_like(r)

        keep_n = jnp.where(i < n_t - 1, 1.0, 0.0)
        dh = dh_ref[...]
        dact = lax.dot_general(jnp.concatenate([dh, dhn_ref[...]], axis=0), wd_ref[...], _NT,
                               preferred_element_type=F32)
        dact = jnp.concatenate([dact[:tm], dact[tm:] * keep_n], axis=0)
        gc_ext = jnp.concatenate([gc_ref[...], gcn_ref[...]], axis=0).astype(F32)
        sig = jax.nn.sigmoid(gc_ext)
        silu = gc_ext * sig
        up_ext = jnp.concatenate([u_ref[...], un_ref[...]], axis=0).astype(F32)
        act = (silu * up_ext)[:tm].astype(BF16)
        dwd_ref[...] += lax.dot_general(act, dh, _TN, preferred_element_type=F32)
        dup = (dact * silu)[:tm].astype(BF16)
        du_ref[...] = dup
        dgc = dact * up_ext * (sig + silu * (1.0 - sig))
        dgc_1, dgc_2 = _fwd(dgc, 1), _fwd(dgc, 2)
        dg = (cw_ref[2:3, :] * dgc + cw_ref[1:2, :] * dgc_1 + cw_ref[0:1, :] * dgc_2)[:tm].astype(BF16)
        dg_ref[...] = dg
        x = x_ref[...]
        dwg_ref[...] += lax.dot_general(dg, x, _TN, preferred_element_type=F32)
        dwu_ref[...] += lax.dot_general(dup, x, _TN, preferred_element_type=F32)
        g32 = g_ref[...].astype(F32)
        dcb_ref[...] += _colsum(dgc[:tm])
        dcw_ref[2:3, :] += _colsum(dgc[:tm] * g32)
        dcw_ref[1:2, :] += _colsum(dgc_1[:tm] * g32)
        dcw_ref[0:1, :] += _colsum(dgc_2[:tm] * g32)

    rows = pl.BlockSpec((tm, d), lambda j, i: (i, 0))
    rows_next = pl.BlockSpec((HALO, d), lambda j, i: (jnp.minimum((i + 1) * hb, n_t * hb - 1), 0))
    tile = pl.BlockSpec((None, tm, f), lambda j, i: (j, i, 0))
    nxt = _ffn_next_halo(tm, f, n_t)
    per_j = lambda r, c: pl.BlockSpec((None, r, c), lambda j, i: (j, 0, 0))
    return pl.pallas_call(
        body, name=name, grid=(j_n, n_t),
        in_specs=[rows, rows_next, rows, tile, tile, nxt, tile, nxt, per_j(f, d), per_j(3, f)]
        + ([ANY] if pin is not None else []),
        out_specs=[tile, tile, per_j(f, d), per_j(f, d), per_j(f, d), per_j(3, f), per_j(1, f)],
        out_shape=[jax.ShapeDtypeStruct((j_n, s, f), BF16), jax.ShapeDtypeStruct((j_n, s, f), BF16),
                   jax.ShapeDtypeStruct((j_n, f, d), F32), jax.ShapeDtypeStruct((j_n, f, d), F32),
                   jax.ShapeDtypeStruct((j_n, f, d), F32), jax.ShapeDtypeStruct((j_n, 3, f), F32),
                   jax.ShapeDtypeStruct((j_n, 1, f), F32)],
        compiler_params=_params("parallel", "arbitrary"))(
            dh_bf, dh_bf, hn, g, gc, gc, up, up, wd, cw, *([pin] if pin is not None else []))


def dx_rms_bwd(pairs, h, gain, rstd, dres, *, name, pin=None):
    j_n, s, f = pairs[0][0].shape
    d = h.shape[1]
    tm = min(ROW_TILE, s)
    n_p = len(pairs)
    dims = [_NT if w.shape[1:] == (d, f) else (((1,), (0,)), ((), ())) for _, w in pairs]

    def body(*refs):
        dy_refs, w_refs = refs[:n_p], refs[n_p:2 * n_p]
        h_ref, g_ref, r_ref, dr_ref = refs[2 * n_p:2 * n_p + 4]
        o_ref, ob_ref, dgain_ref = refs[-3:]

        @pl.when(pl.program_id(0) == 0)
        def _():
            dgain_ref[...] = jnp.zeros_like(dgain_ref)
        dx = jnp.zeros((tm, d), F32)
        for j in range(j_n):
            for p in range(n_p):
                dx = dx + lax.dot_general(dy_refs[p][j], w_refs[p][j], dims[p], preferred_element_type=F32)
        rstd_v = r_ref[...]
        hhat = h_ref[...] * rstd_v
        dgain_ref[...] += _colsum(dx * hhat)
        dxg = dx * g_ref[...]
        dh = dr_ref[...] + rstd_v * (dxg - hhat * jnp.mean(dxg * hhat, axis=-1, keepdims=True))
        o_ref[...] = dh
        ob_ref[...] = dh.astype(BF16)

    tile4 = pl.BlockSpec((j_n, tm, f), lambda i: (0, i, 0))
    whole = [pl.BlockSpec(w.shape, lambda i: (0, 0, 0), pipeline_mode=pl.Buffered(1)) for _, w in pairs]
    row = pl.BlockSpec((tm, d), lambda i: (i, 0))
    vec = pl.BlockSpec((1, d), lambda i: (0, 0))
    return pl.pallas_call(
        body, name=name, grid=(s // tm,),
        in_specs=[tile4] * n_p + whole + [row, vec, pl.BlockSpec((tm, 1), lambda i: (i, 0)), row]
        + ([ANY] if pin is not None else []),
        out_specs=[row, row, vec],
        out_shape=[jax.ShapeDtypeStruct((s, d), F32), jax.ShapeDtypeStruct((s, d), BF16),
                   jax.ShapeDtypeStruct((1, d), F32)],
        compiler_params=_params("arbitrary"))(
            *[p[0] for p in pairs], *[p[1] for p in pairs], h, gain, rstd, dres, *([pin] if pin is not None else []))


def _sgu_gate(vn_bf, ws_ref, bs_ref, h, rows):
    tri = lax.broadcasted_iota(jnp.int32, (CHUNK, CHUNK), 0) >= lax.broadcasted_iota(jnp.int32, (CHUNK, CHUNK), 1)
    ws = jnp.where(tri, ws_ref[h], 0.0).astype(BF16)
    cols = slice((h % 4) * GROUP, (h % 4 + 1) * GROUP)
    return ws, jnp.dot(ws, vn_bf[h // 4][rows, cols], preferred_element_type=F32) + bs_ref[h]


def odd_layer_fwd(h, gain, win, sgu_norm, w_spatial, b_spatial, wout, *, name):
    s, d = h.shape
    w = win.shape[-1]
    ts = min(ROW_TILE, s)
    n_heads = w_spatial.shape[0]

    def body(h_ref, gain_ref, win_ref, n_ref, ws_ref, bs_ref, wout_ref, o_ref, xn_ref, r_ref, p_ref, m_ref, rv_ref):
        x = h_ref[...]
        rstd_x = lax.rsqrt(jnp.mean(x * x, axis=-1, keepdims=True) + EPS)
        xn = (x * rstd_x * gain_ref[...]).astype(BF16)
        xn_ref[...] = xn
        r_ref[...] = rstd_x
        for k in range(4):
            p_ref[k] = jnp.dot(xn, win_ref[k], preferred_element_type=F32).astype(BF16)
        v = [_gelu(p_ref[2].astype(F32)), _gelu(p_ref[3].astype(F32))]
        ms = (jnp.sum(v[0] * v[0], axis=-1, keepdims=True) + jnp.sum(v[1] * v[1], axis=-1, keepdims=True)) / (2 * w)
        rstd = lax.rsqrt(ms + EPS)
        rv_ref[...] = rstd
        vn = [(v[k] * rstd * n_ref[:, k * w:(k + 1) * w]).astype(BF16) for k in range(2)]
        for hd in range(n_heads):
            cols = slice((hd % 4) * GROUP, (hd % 4 + 1) * GROUP)
            for c in range(ts // CHUNK):
                rows = slice(c * CHUNK, (c + 1) * CHUNK)
                _, gate = _sgu_gate(vn, ws_ref, bs_ref, hd, rows)
                u = _gelu(p_ref[hd // 4, rows, cols].astype(F32))
                m_ref[rows, hd * GROUP:(hd + 1) * GROUP] = (u * gate).astype(BF16)
        o_ref[...] = x + jnp.dot(m_ref[...], wout_ref[...], preferred_element_type=F32)

    const = lambda shape: pl.BlockSpec(shape, lambda i: (0,) * len(shape))
    row = pl.BlockSpec((ts, d), lambda i: (i, 0))
    col1 = pl.BlockSpec((ts, 1), lambda i: (i, 0))
    return pl.pallas_call(
        body, name=name, grid=(s // ts,),
        in_specs=[row, const((1, d)), const((4, d, w)), const((1, 2 * w)),
                  const((n_heads, CHUNK, CHUNK)), const((n_heads, CHUNK, 1)), const((2 * w, d))],
        out_specs=[row, row, col1, pl.BlockSpec((4, ts, w), lambda i: (0, i, 0)),
                   pl.BlockSpec((ts, 2 * w), lambda i: (i, 0)), col1],
        out_shape=[jax.ShapeDtypeStruct((s, d), F32), jax.ShapeDtypeStruct((s, d), BF16),
                   jax.ShapeDtypeStruct((s, 1), F32), jax.ShapeDtypeStruct((4, s, w), BF16),
                   jax.ShapeDtypeStruct((s, 2 * w), BF16), jax.ShapeDtypeStruct((s, 1), F32)],
        compiler_params=_params("parallel"))(h, gain, win, sgu_norm, w_spatial, b_spatial, wout)


def sgu_bwd(p, mix, xn, dh_bf, wout, rstd, sgu_norm, w_spatial, b_spatial, *, name, pin=None):
    _, s, w = p.shape
    d = dh_bf.shape[1]
    ts = min(ROW_TILE, s)
    n_heads = w_spatial.shape[0]

    def body(p_ref, m_ref, x_ref, dh_ref, wout_ref, r_ref, n_ref, ws_ref, bs_ref, *rest):
        dp_ref, dwi_ref, dwo_ref, dn_ref, dws_ref, dbs_ref, dvn_ref, dm_ref = rest[-8:]

        @pl.when(pl.program_id(0) == 0)
        def _():
            dwi_ref[...] = jnp.zeros_like(dwi_ref)
            dwo_ref[...] = jnp.zeros_like(dwo_ref)
            dn_ref[...] = jnp.zeros_like(dn_ref)
            dws_ref[...] = jnp.zeros_like(dws_ref)
            dbs_ref[...] = jnp.zeros_like(dbs_ref)

        dwo_ref[...] += lax.dot_general(m_ref[...], dh_ref[...], _TN, preferred_element_type=F32)
        dm_ref[...] = lax.dot_general(dh_ref[...], wout_ref[...], _NT, preferred_element_type=F32)
        rstd_v = r_ref[...]
        vhat = [_gelu(p_ref[2 + k].astype(F32)) * rstd_v for k in range(2)]
        vn = [(vhat[k] * n_ref[:, k * w:(k + 1) * w]).astype(BF16) for k in range(2)]
        tri = lax.broadcasted_iota(jnp.int32, (CHUNK, CHUNK), 0) >= lax.broadcasted_iota(jnp.int32, (CHUNK, CHUNK), 1)
        for h in range(n_heads):
            cols = slice((h % 4) * GROUP, (h % 4 + 1) * GROUP)
            ocols = slice(h * GROUP, (h + 1) * GROUP)
            for c in range(ts // CHUNK):
                rows = slice(c * CHUNK, (c + 1) * CHUNK)
                ws, gate = _sgu_gate(vn, ws_ref, bs_ref, h, rows)
                pu = p_ref[h // 4, rows, cols].astype(F32)
                dm = dm_ref[rows, ocols]
                dp_ref[h // 4, rows, cols] = (dm * gate * _gelu_grad(pu)).astype(BF16)
                dgate = dm * _gelu(pu)
                dbs_ref[h] += jnp.sum(dgate, axis=-1, keepdims=True)
                dgate_bf = dgate.astype(BF16)
                dws = lax.dot_general(dgate_bf, vn[h // 4][rows, cols], _NT, preferred_element_type=F32)
                dws_ref[h] += jnp.where(tri, dws, 0.0)
                dvn_ref[rows, ocols] = lax.dot_general(ws, dgate_bf, _TN, preferred_element_type=F32)
        for k in range(2):
            kc = slice(k * w, (k + 1) * w)
            dvn = dvn_ref[:, kc]
            dn_ref[:, kc] += _colsum(dvn * vhat[k])
        dvh = [dvn_ref[:, k * w:(k + 1) * w] * n_ref[:, k * w:(k + 1) * w] for k in range(2)]
        dot = (jnp.sum(dvh[0] * vhat[0], axis=-1, keepdims=True)
               + jnp.sum(dvh[1] * vhat[1], axis=-1, keepdims=True)) / (2 * w)
        for k in range(2):
            dv = rstd_v * (dvh[k] - vhat[k] * dot)
            dp_ref[2 + k] = (dv * _gelu_grad(p_ref[2 + k].astype(F32))).astype(BF16)
        for k in range(4):
            dwi_ref[k] += lax.dot_general(x_ref[...], dp_ref[k], _TN, preferred_element_type=F32)

    const = lambda shape: pl.BlockSpec(shape, lambda i: (0,) * len(shape))
    tile4 = pl.BlockSpec((4, ts, w), lambda i: (0, i, 0))
    return pl.pallas_call(
        body, name=name, grid=(s // ts,),
        in_specs=[tile4, pl.BlockSpec((ts, 2 * w), lambda i: (i, 0)), pl.BlockSpec((ts, d), lambda i: (i, 0)),
                  pl.BlockSpec((ts, d), lambda i: (i, 0)), const((2 * w, d)), pl.BlockSpec((ts, 1), lambda i: (i, 0)),
                  const((1, 2 * w)), const((n_heads, CHUNK, CHUNK)), const((n_heads, CHUNK, 1))]
        + ([ANY] if pin is not None else []),
        out_specs=[tile4, const((4, d, w)), const((2 * w, d)), const((1, 2 * w)), const((n_heads, CHUNK, CHUNK)),
                   const((n_heads, CHUNK, 1))],
        out_shape=[jax.ShapeDtypeStruct((4, s, w), BF16), jax.ShapeDtypeStruct((4, d, w), F32),
                   jax.ShapeDtypeStruct((2 * w, d), F32),
                   jax.ShapeDtypeStruct((1, 2 * w), F32),
                   jax.ShapeDtypeStruct((n_heads, CHUNK, CHUNK), F32),
                   jax.ShapeDtypeStruct((n_heads, CHUNK, 1), F32)],
        scratch_shapes=[pltpu.VMEM((ts, 2 * w), F32), pltpu.VMEM((ts, 2 * w), F32)],
        compiler_params=_params("arbitrary"))(
            p, mix, xn, dh_bf, wout, rstd, sgu_norm, w_spatial, b_spatial, *([pin] if pin is not None else []))


def _row_tile(rows):
    if rows <= ROW_TILE:
        return rows
    for t in (512, 384, 352, 256, 128, 64, 32, 16, 8):
        if rows % t == 0:
            return t
    return rows


def adamw(w, g, m, v, *, name):
    shape = w.shape
    cols = shape[-1]
    rows = w.size // cols
    w2, g2, m2, v2 = (a.reshape(rows, cols) for a in (w, g, m, v))
    tr = _row_tile(rows)
    bc1 = 1.0 - ADAM_B1 ** ADAM_STEP
    bc2 = 1.0 - ADAM_B2 ** ADAM_STEP

    def body(w_ref, g_ref, m_ref, v_ref, d_ref, nm_ref, nv_ref):
        grad = g_ref[...]
        m_new = ADAM_B1 * m_ref[...] + (1.0 - ADAM_B1) * grad
        v_new = ADAM_B2 * v_ref[...] + (1.0 - ADAM_B2) * (grad * grad)
        nm_ref[...] = m_new
        nv_ref[...] = v_new
        d_ref[...] = -ADAM_LR * ((m_new / bc1) / (jnp.sqrt(v_new / bc2) + ADAM_EPS) + ADAM_WD * w_ref[...])

    spec = pl.BlockSpec((tr, cols), lambda i: (i, 0))
    outs = pl.pallas_call(
        body, name=name, grid=(rows // tr,),
        in_specs=[spec] * 4, out_specs=[spec] * 3,
        out_shape=[jax.ShapeDtypeStruct((rows, cols), F32)] * 3,
        compiler_params=_params("parallel"))(w2, g2, m2, v2)
    return tuple(o.reshape(shape) for o in outs)


def adamw_layers(w, grads, m, v, *, name):
    n_l, rows, cols = w.shape
    tr = _row_tile(rows)
    bc1 = 1.0 - ADAM_B1 ** ADAM_STEP
    bc2 = 1.0 - ADAM_B2 ** ADAM_STEP
    outs = None
    for l in range(n_l):
        def body(w_ref, g_ref, m_ref, v_ref, *rest):
            go_ref, d_ref, nm_ref, nv_ref = rest[-4:]
            grad = g_ref[...]
            m_new = ADAM_B1 * m_ref[...] + (1.0 - ADAM_B1) * grad
            v_new = ADAM_B2 * v_ref[...] + (1.0 - ADAM_B2) * (grad * grad)
            go_ref[...] = grad
            nm_ref[...] = m_new
            nv_ref[...] = v_new
            d_ref[...] = -ADAM_LR * ((m_new / bc1) / (jnp.sqrt(v_new / bc2) + ADAM_EPS) + ADAM_WD * w_ref[...])

        layer = pl.BlockSpec((None, tr, cols), lambda i, l=l: (l, i, 0))
        prev = list(outs) if outs is not None else []
        outs = pl.pallas_call(
            body, name=f"{name}{l}", grid=(rows // tr,),
            in_specs=[layer, pl.BlockSpec((tr, cols), lambda i: (i, 0)), layer, layer] + [ANY] * len(prev),
            out_specs=[layer] * 4,
            out_shape=[jax.ShapeDtypeStruct(w.shape, F32)] * 4,
            input_output_aliases={4 + k: k for k in range(len(prev))},
            compiler_params=_params("parallel"))(w, grads[l], m, v, *prev)
    return tuple(outs)


def _place():
    return lax.axis_index("x"), lax.axis_index("y"), lax.axis_index("c")


def _other_chips(x, y):
    return [(1 - x, y), (x, 1 - y), (1 - x, 1 - y)]


HBM = pl.BlockSpec(memory_space=pltpu.HBM)
SEM = pl.BlockSpec(memory_space=pltpu.SEMAPHORE)
DATAFLOW = pltpu.SideEffectType.DATAFLOW_SIDE_EFFECTING


def _in_hbm(a):
    return pltpu.with_memory_space_constraint(a, pltpu.HBM)


def cast_into_slot(w, chip, *, l=None, name, pin=None):
    rows, cols = w.shape[-2:]
    tr = _row_tile(rows)

    def body(chip_ref, w_ref, *rest):
        rest[-1][...] = w_ref[...].astype(BF16)

    in_spec = (pl.BlockSpec((tr, cols), lambda i, chip_ref: (i, 0)) if l is None
               else pl.BlockSpec((None, tr, cols), lambda i, chip_ref: (l, i, 0)))
    return pl.pallas_call(
        body, name=name,
        grid_spec=pltpu.PrefetchScalarGridSpec(
            num_scalar_prefetch=1, grid=(rows // tr,), in_specs=[in_spec] + ([ANY] if pin is not None else []),
            out_specs=pl.BlockSpec((None, tr, cols), lambda i, chip_ref: (chip_ref[0], i, 0))),
        out_shape=jax.ShapeDtypeStruct((N_CHIPS, rows, cols), BF16),
        compiler_params=_params("parallel"))(chip, w, *([pin] if pin is not None else []))


def _half(ref, slot, c):
    half = ref.shape[1] // 2
    return ref.at[slot, pl.ds(c * half, half), :]


def gather_start(groups, smalls, *, name):
    flat = [b for g in groups for b in g]
    n_b, n_s, n_g = len(flat), len(smalls), len(groups)
    n_sem = 2 * n_g + (2 if n_s else 0)

    def body(*refs):
        bufs, small_refs = refs[:n_b], refs[n_b:n_b + n_s]
        sems = refs[n_b + n_s:n_b + n_s + n_sem]
        token = refs[-1]
        x, y, c = _place()
        me = 2 * x + y
        chips = _other_chips(x, y)
        for si in range(n_s):
            piece = small_refs[si].at[me]
            for k, (px, py) in enumerate(chips):
                pltpu.make_async_remote_copy(
                    src_ref=piece, dst_ref=piece,
                    send_sem=sems[2 * n_g].at[3 * si + k], recv_sem=sems[2 * n_g + 1].at[3 * si + k],
                    device_id=(px, py, c), device_id_type=MESH).start()
        t = 0
        for gi, group in enumerate(groups):
            for ti in range(len(group)):
                piece = _half(bufs[t], me, c)
                t += 1
                for k, (px, py) in enumerate(chips):
                    pltpu.make_async_remote_copy(
                        src_ref=piece, dst_ref=piece,
                        send_sem=sems[2 * gi].at[3 * ti + k], recv_sem=sems[2 * gi + 1].at[3 * ti + k],
                        device_id=(px, py, c), device_id_type=MESH).start()
        token[...] = jnp.zeros_like(token)

    sem_shapes = []
    for group in groups:
        sem_shapes += [pltpu.SemaphoreType.DMA((3 * len(group),))] * 2
    if n_s:
        sem_shapes += [pltpu.SemaphoreType.DMA((3 * n_s,))] * 2
    arrays = flat + list(smalls)
    res = pl.pallas_call(
        body, name=name,
        out_shape=tuple(sem_shapes) + tuple(pltpu.HBM(a.shape, a.dtype) for a in arrays)
        + (jax.ShapeDtypeStruct((8, 128), F32),),
        in_specs=[HBM] * len(arrays),
        out_specs=tuple([SEM] * n_sem + [HBM] * len(arrays) + [pl.BlockSpec(memory_space=pltpu.VMEM)]),
        input_output_aliases={i: n_sem + i for i in range(len(arrays))},
        compiler_params=pltpu.CompilerParams(has_side_effects=DATAFLOW))(*[_in_hbm(a) for a in arrays])
    sems, thru, token = res[:n_sem], res[n_sem:-1], res[-1]
    out_groups, t = [], 0
    for group in groups:
        out_groups.append(list(thru[t:t + len(group)]))
        t += len(group)
    return sems, out_groups, list(thru[n_b:]), token


def gather_wait(bufs, send, recv, after, *, name, smalls=(), small_send=None, small_recv=None):
    n_b, n_s = len(bufs), len(smalls)
    arrays = list(bufs) + list(smalls)
    sem_ops = [send, recv] + ([small_send, small_recv] if n_s else [])

    def body(*refs):
        buf_refs, small_refs = refs[:n_b], refs[n_b:n_b + n_s]
        sems = refs[n_b + n_s:n_b + n_s + len(sem_ops)]
        x, y, c = _place()
        me = 2 * x + y
        chips = _other_chips(x, y)
        for ti in range(n_b):
            for k, (px, py) in enumerate(chips):
                done = pltpu.make_async_remote_copy(
                    src_ref=_half(buf_refs[ti], me, c), dst_ref=_half(buf_refs[ti], 2 * px + py, c),
                    send_sem=sems[0].at[3 * ti + k], recv_sem=sems[1].at[3 * ti + k],
                    device_id=(px, py, c), device_id_type=MESH)
                done.wait_send()
                done.wait_recv()
        for si in range(n_s):
            for k, (px, py) in enumerate(chips):
                done = pltpu.make_async_remote_copy(
                    src_ref=small_refs[si].at[me], dst_ref=small_refs[si].at[2 * px + py],
                    send_sem=sems[2].at[3 * si + k], recv_sem=sems[3].at[3 * si + k],
                    device_id=(px, py, c), device_id_type=MESH)
                done.wait_send()
                done.wait_recv()

    res = pl.pallas_call(
        body, name=name,
        out_shape=tuple(pltpu.HBM(a.shape, a.dtype) for a in arrays),
        in_specs=[HBM] * len(arrays) + [SEM] * len(sem_ops) + [ANY],
        out_specs=tuple([HBM] * len(arrays)),
        input_output_aliases={i: i for i in range(len(arrays))},
        compiler_params=pltpu.CompilerParams(has_side_effects=DATAFLOW))(*arrays, *sem_ops, after)
    return list(res[:n_b]), list(res[n_b:])


def gather_forward(bufs, *, name):
    n = len(bufs)

    def body(*refs):
        ins, outs = refs[:n], refs[n:2 * n]
        send_sems, recv_sems = refs[2 * n:]
        x, y, c = _place()
        chips = _other_chips(x, y)
        for t in range(n):
            for k, (px, py) in enumerate(chips):
                pltpu.make_async_remote_copy(
                    src_ref=_half(ins[t], 2 * px + py, c), dst_ref=_half(outs[t], 2 * px + py, c),
                    send_sem=send_sems.at[3 * t + k], recv_sem=recv_sems.at[3 * t + k],
                    device_id=(x, y, 1 - c), device_id_type=MESH).start()
        for t in range(n):
            for k, (px, py) in enumerate(chips):
                done = pltpu.make_async_remote_copy(
                    src_ref=_half(ins[t], 2 * px + py, c), dst_ref=_half(outs[t], 2 * px + py, 1 - c),
                    send_sem=send_sems.at[3 * t + k], recv_sem=recv_sems.at[3 * t + k],
                    device_id=(x, y, 1 - c), device_id_type=MESH)
                done.wait_send()
                done.wait_recv()

    return pl.pallas_call(
        body, name=name, in_specs=[ANY] * n, out_specs=[ANY] * n,
        out_shape=[jax.ShapeDtypeStruct(a.shape, a.dtype) for a in bufs],
        input_output_aliases={i: i for i in range(n)},
        scratch_shapes=[pltpu.SemaphoreType.DMA((3 * n,)), pltpu.SemaphoreType.DMA((3 * n,))],
        compiler_params=pltpu.CompilerParams(has_side_effects=True))(*bufs)


def sum_stage_a(grad, recv, place, wire, *, name):
    j_n, half, cols = recv.shape

    def body(place_ref, g_ref, r_ref, o_ref, ob_ref):
        acc = g_ref[...] + r_ref[...]
        ob_ref[...] = acc.astype(wire)

        @pl.when(pl.program_id(0) == place_ref[0])
        def _():
            o_ref[...] = acc

    blk = (None, half, cols)
    return pl.pallas_call(
        body, name=name,
        grid_spec=pltpu.PrefetchScalarGridSpec(
            num_scalar_prefetch=1, grid=(j_n,),
            in_specs=[pl.BlockSpec(blk, lambda j, place_ref: (j, place_ref[1], 0)),
                      pl.BlockSpec(blk, lambda j, place_ref: (j, 0, 0))],
            out_specs=[pl.BlockSpec((half, cols), lambda j, place_ref: (0, 0)),
                       pl.BlockSpec(blk, lambda j, place_ref: (j, 0, 0))]),
        out_shape=[jax.ShapeDtypeStruct((half, cols), F32), jax.ShapeDtypeStruct(recv.shape, wire)],
        compiler_params=_params("arbitrary"))(place, grad, recv)


def _stage_a_copies(srcs, lands, x, y, c):
    out = []
    for src, land in zip(srcs, lands):
        half = src.shape[1] // 2
        out.append((src.at[:, pl.ds((1 - c) * half, half), :], land, (x, y, 1 - c)))
    return out


def _stage_b_copies(srcs, lands, x, y, c):
    out = []
    for src, land in zip(srcs, lands):
        for k, (px, py) in enumerate(_other_chips(x, y)):
            out.append((src.at[2 * px + py], land.at[k], (px, py, c)))
    return out


def _forward_copies(bufs, _, x, y, c):
    out = []
    for buf in bufs:
        for px, py in _other_chips(x, y):
            out.append((_half(buf, 2 * px + py, c), _half(buf, 2 * px + py, c), (x, y, 1 - c)))
    return out


def _stage_c_copies(fulls, _, x, y, c):
    out = []
    for full in fulls:
        half = full.shape[0] // 2
        mine = full.at[pl.ds(c * half, half), :]
        out.append((mine, mine, (x, y, 1 - c)))
    return out


def split_start(srcs, lands, copies, *, name):
    n, n_all = len(srcs), len(srcs) + len(lands)
    n_c = len(copies(srcs, lands, 0, 0, 0))

    def body(*refs):
        src_refs, land_refs = refs[:n], refs[n:n_all]
        send_sems, recv_sems = refs[n_all], refs[n_all + 1]
        token = refs[-1]
        x, y, c = _place()
        for k, (src, dst, target) in enumerate(copies(src_refs, land_refs, x, y, c)):
            pltpu.make_async_remote_copy(src_ref=src, dst_ref=dst, send_sem=send_sems.at[k], recv_sem=recv_sems.at[k],
                                         device_id=target, device_id_type=MESH).start()
        token[...] = jnp.zeros_like(token)

    arrays = list(srcs) + list(lands)
    res = pl.pallas_call(
        body, name=name,
        out_shape=(pltpu.SemaphoreType.DMA((n_c,)), pltpu.SemaphoreType.DMA((n_c,)))
        + tuple(pltpu.HBM(a.shape, a.dtype) for a in arrays) + (jax.ShapeDtypeStruct((8, 128), F32),),
        in_specs=[HBM] * n_all,
        out_specs=tuple([SEM, SEM] + [HBM] * n_all + [pl.BlockSpec(memory_space=pltpu.VMEM)]),
        input_output_aliases={i: 2 + i for i in range(n_all)},
        compiler_params=pltpu.CompilerParams(has_side_effects=DATAFLOW))(*[_in_hbm(a) for a in arrays])
    return res[0], res[1], list(res[2:2 + n]), list(res[2 + n:2 + n_all]), res[-1]


def split_wait(srcs, lands, send, recv, copies, after, *, name):
    n, n_all = len(srcs), len(srcs) + len(lands)

    def body(*refs):
        src_refs, land_refs = refs[:n], refs[n:n_all]
        send_sems, recv_sems = refs[n_all], refs[n_all + 1]
        x, y, c = _place()
        for k, (src, dst, target) in enumerate(copies(src_refs, land_refs, x, y, c)):
            done = pltpu.make_async_remote_copy(src_ref=src, dst_ref=dst, send_sem=send_sems.at[k],
                                                recv_sem=recv_sems.at[k], device_id=target, device_id_type=MESH)
            done.wait_send()
            done.wait_recv()

    arrays = list(srcs) + list(lands)
    res = pl.pallas_call(
        body, name=name,
        out_shape=tuple(pltpu.HBM(a.shape, a.dtype) for a in arrays),
        in_specs=[HBM] * n_all + [SEM, SEM, ANY],
        out_specs=tuple([HBM] * n_all),
        input_output_aliases={i: i for i in range(n_all)},
        compiler_params=pltpu.CompilerParams(has_side_effects=DATAFLOW))(*arrays, send, recv, after)
    return list(res[:n]), list(res[n:])


def split_wait_all(sets, copies, after, *, name):
    flat = [a for srcs, lands, _, _ in sets for a in list(srcs) + list(lands)]
    n_flat = len(flat)

    def body(*refs):
        x, y, c = _place()
        pos = 0
        for si, (srcs, lands, _, _) in enumerate(sets):
            src_refs = refs[pos:pos + len(srcs)]
            land_refs = refs[pos + len(srcs):pos + len(srcs) + len(lands)]
            pos += len(srcs) + len(lands)
            send_sems, recv_sems = refs[n_flat + 2 * si], refs[n_flat + 2 * si + 1]
            for k, (src, dst, target) in enumerate(copies(src_refs, land_refs, x, y, c)):
                done = pltpu.make_async_remote_copy(src_ref=src, dst_ref=dst, send_sem=send_sems.at[k],
                                                    recv_sem=recv_sems.at[k], device_id=target, device_id_type=MESH)
                done.wait_send()
                done.wait_recv()

    sems = [s_ for _, _, send, recv in sets for s_ in (send, recv)]
    res = pl.pallas_call(
        body, name=name,
        out_shape=tuple(pltpu.HBM(a.shape, a.dtype) for a in flat),
        in_specs=[HBM] * n_flat + [SEM] * len(sems) + [ANY],
        out_specs=tuple([HBM] * n_flat),
        input_output_aliases={i: i for i in range(n_flat)},
        compiler_params=pltpu.CompilerParams(has_side_effects=DATAFLOW))(*flat, *sems, after)
    out, pos = [], 0
    for srcs, lands, _, _ in sets:
        out.append(list(res[pos + len(srcs):pos + len(srcs) + len(lands)]))
        pos += len(srcs) + len(lands)
    return out


def sum_stage_b(part, recv, place, *, name):
    half, cols = part.shape

    def body(place_ref, p_ref, r_ref, o_ref):
        acc = p_ref[...]
        for k in range(3):
            acc = acc + r_ref[k].astype(F32)
        o_ref[...] = acc

    return pl.pallas_call(
        body, name=name,
        grid_spec=pltpu.PrefetchScalarGridSpec(
            num_scalar_prefetch=1, grid=(1,),
            in_specs=[pl.BlockSpec((half, cols), lambda i, place_ref: (0, 0)),
                      pl.BlockSpec((3, half, cols), lambda i, place_ref: (0, 0, 0))],
            out_specs=pl.BlockSpec((half, cols), lambda i, place_ref: (place_ref[1], 0))),
        out_shape=jax.ShapeDtypeStruct((2 * half, cols), F32),
        compiler_params=_params("arbitrary"))(place, part, recv)


def gather_chip_blocks(slots, *, name):
    def body(in_ref, out_ref, send_sems, recv_sems):
        x, y, c = _place()
        me = 2 * x + y
        chips = _other_chips(x, y)
        for k, (px, py) in enumerate(chips):
            pltpu.make_async_remote_copy(
                src_ref=in_ref.at[me], dst_ref=out_ref.at[me],
                send_sem=send_sems.at[k], recv_sem=recv_sems.at[k],
                device_id=(px, py, c), device_id_type=MESH).start()
        for k, (px, py) in enumerate(chips):
            done = pltpu.make_async_remote_copy(
                src_ref=in_ref.at[me], dst_ref=out_ref.at[2 * px + py],
                send_sem=send_sems.at[k], recv_sem=recv_sems.at[k],
                device_id=(px, py, c), device_id_type=MESH)
            done.wait_send()
            done.wait_recv()

    return pl.pallas_call(
        body, name=name, in_specs=[ANY], out_specs=ANY,
        out_shape=jax.ShapeDtypeStruct(slots.shape, slots.dtype),
        input_output_aliases={0: 0},
        scratch_shapes=[pltpu.SemaphoreType.DMA((3,)), pltpu.SemaphoreType.DMA((3,))],
        compiler_params=pltpu.CompilerParams(has_side_effects=True))(slots)


def _ffn_bwd(dh, dh_bf, h, gain, saved, wg, wu, wd, cw, cb, place, l, pin):
    hn, rstd, g, up, gc = saved
    dg, dup, dwg, dwu, dwd, dcw, dcb = ffn_bwd_a(dh_bf, hn, g, up, gc, wd, cw, name=f"ffn{l}_bwd_a", pin=pin)
    red = _reduce_a_begin([(dwg, BF16), (dwu, BF16), (dwd, BF16)], tag=f"f{l}")
    dh_in, dh_in_bf, dgain = dx_rms_bwd([(dg, wg), (dup, wu)], h, gain, rstd, dh, name=f"ffn{l}_bwd_b",
                                        pin=red[-1])
    red = _reduce_b_begin(red, place, dh_in_bf, tag=f"f{l}")
    return dh_in, dh_in_bf, (dcw, dcb, dgain), red


def _reduce_a_begin(grads, *, tag):
    lands = [lax.empty((g.shape[0], g.shape[1] // 2, g.shape[2]), F32) for g, _ in grads]
    send, recv, srcs, lands, token = split_start([g for g, _ in grads], lands, _stage_a_copies,
                                                 name=f"reduce_a_start_{tag}")
    return [w for _, w in grads], send, recv, srcs, lands, token


def _reduce_b_begin(state, place, after, *, tag):
    wires, send, recv, srcs, lands, _ = state
    grads, recv_a = split_wait(srcs, lands, send, recv, _stage_a_copies, after, name=f"reduce_a_wait_{tag}")
    parts = [sum_stage_a(g, r, place, w, name=f"sum_a_{tag}{i}") for i, (g, r, w) in enumerate(zip(grads, recv_a, wires))]
    lands_b = [lax.empty((3,) + p[1].shape[1:], p[1].dtype) for p in parts]
    send, recv, srcs, lands, token = split_start([p[1] for p in parts], lands_b, _stage_b_copies,
                                                 name=f"reduce_b_start_{tag}")
    return [p[0] for p in parts], send, recv, srcs, lands, token


def _reduce_c_begin(state, place, after, *, tag):
    parts, send, recv, srcs, lands, _ = state
    _, recv_b = split_wait(srcs, lands, send, recv, _stage_b_copies, after, name=f"reduce_b_wait_{tag}")
    halves = [sum_stage_b(p, r, place, name=f"sum_b_{tag}{i}") for i, (p, r) in enumerate(zip(parts, recv_b))]
    send, recv, fulls, _, token = split_start(halves, [], _stage_c_copies, name=f"reduce_c_start_{tag}")
    return send, recv, fulls, token


def _reduce_finish(state, after, *, tag):
    send, recv, fulls, _ = state
    fulls, _ = split_wait(fulls, [], send, recv, _stage_c_copies, after, name=f"reduce_c_wait_{tag}")
    return fulls


def kernel(x, norm_mix, norm_ffn, final_norm, w_in_even, conv_a, w_pool, pool_scale, w_out_even, w_in_odd, sgu_norm, w_spatial, b_spatial, w_out_odd, w_ffn_gate, w_ffn_up, conv_ffn, b_conv_ffn, w_ffn_down, loss_target, m_norm_mix, m_norm_ffn, m_final_norm, m_w_in_even, m_conv_a, m_w_pool, m_pool_scale, m_w_out_even, m_w_in_odd, m_sgu_norm, m_w_spatial, m_b_spatial, m_w_out_odd, m_w_ffn_gate, m_w_ffn_up, m_conv_ffn, m_b_conv_ffn, m_w_ffn_down, v_norm_mix, v_norm_ffn, v_final_norm, v_w_in_even, v_conv_a, v_w_pool, v_pool_scale, v_w_out_even, v_w_in_odd, v_sgu_norm, v_w_spatial, v_b_spatial, v_w_out_odd, v_w_ffn_gate, v_w_ffn_up, v_conv_ffn, v_b_conv_ffn, v_w_ffn_down):
    weights = dict(norm_mix=norm_mix, norm_ffn=norm_ffn, final_norm=final_norm, w_in_even=w_in_even,
                   conv_a=conv_a, w_pool=w_pool, pool_scale=pool_scale, w_out_even=w_out_even,
                   w_in_odd=w_in_odd, sgu_norm=sgu_norm, w_spatial=w_spatial, b_spatial=b_spatial,
                   w_out_odd=w_out_odd, w_ffn_gate=w_ffn_gate, w_ffn_up=w_ffn_up, conv_ffn=conv_ffn,
                   b_conv_ffn=b_conv_ffn, w_ffn_down=w_ffn_down)
    m_in = dict(norm_mix=m_norm_mix, norm_ffn=m_norm_ffn, final_norm=m_final_norm, w_in_even=m_w_in_even,
                conv_a=m_conv_a, w_pool=m_w_pool, pool_scale=m_pool_scale, w_out_even=m_w_out_even,
                w_in_odd=m_w_in_odd, sgu_norm=m_sgu_norm, w_spatial=m_w_spatial, b_spatial=m_b_spatial,
                w_out_odd=m_w_out_odd, w_ffn_gate=m_w_ffn_gate, w_ffn_up=m_w_ffn_up, conv_ffn=m_conv_ffn,
                b_conv_ffn=m_b_conv_ffn, w_ffn_down=m_w_ffn_down)
    v_in = dict(norm_mix=v_norm_mix, norm_ffn=v_norm_ffn, final_norm=v_final_norm, w_in_even=v_w_in_even,
                conv_a=v_conv_a, w_pool=v_w_pool, pool_scale=v_pool_scale, w_out_even=v_w_out_even,
                w_in_odd=v_w_in_odd, sgu_norm=v_sgu_norm, w_spatial=v_w_spatial, b_spatial=v_b_spatial,
                w_out_odd=v_w_out_odd, w_ffn_gate=v_w_ffn_gate, w_ffn_up=v_w_ffn_up, conv_ffn=v_conv_ffn,
                b_conv_ffn=v_b_conv_ffn, w_ffn_down=v_w_ffn_down)
    order = list(weights)

    chip = 2 * lax.axis_index("x") + lax.axis_index("y")
    core = lax.axis_index("c")
    place = jnp.stack([chip, core]).astype(jnp.int32)
    chip_arr = place[:1]

    h0 = x[0]
    target = loss_target[0]
    d_model = h0.shape[1]
    f_shard = w_ffn_gate.shape[-1]

    def turned(a):
        return jnp.transpose(a, (0, 2, 1))

    def own_slot(v):
        return lax.dynamic_update_index_in_dim(jnp.zeros((N_CHIPS,) + v.shape, v.dtype), v, chip, 0)

    first = [[cast_into_slot(w_in_even[0], chip_arr, name="cast_win_e"),
              cast_into_slot(w_out_even[0], chip_arr, name="cast_wout_e")]]
    smalls = [own_slot(conv_a[0]), own_slot(sgu_norm), own_slot(conv_ffn[0]), own_slot(conv_ffn[1])]
    sems0, first, smalls, token0 = gather_start(first, smalls, name="gather_start0")
    rest = [
        [cast_into_slot(turned(w_ffn_gate), chip_arr, l=0, name="cast_wg0", pin=token0),
         cast_into_slot(turned(w_ffn_up), chip_arr, l=0, name="cast_wu0", pin=token0)],
        [cast_into_slot(w_ffn_down, chip_arr, l=0, name="cast_wd0", pin=token0)],
        [cast_into_slot(w_in_odd[0], chip_arr, name="cast_win_o", pin=token0),
         cast_into_slot(w_out_odd[0], chip_arr, name="cast_wout_o", pin=token0)],
        [cast_into_slot(turned(w_ffn_gate), chip_arr, l=1, name="cast_wg1", pin=token0),
         cast_into_slot(turned(w_ffn_up), chip_arr, l=1, name="cast_wu1", pin=token0)],
        [cast_into_slot(w_ffn_down, chip_arr, l=1, name="cast_wd1", pin=token0)]]
    sems1, rest, _, token = gather_start(rest, [], name="gather_start1")
    groups = first + rest
    sems = list(sems0[:2]) + list(sems1) + list(sems0[2:])

    def arrive(gi, after, with_smalls=False):
        kw = dict(smalls=smalls, small_send=sems[-2], small_recv=sems[-1]) if with_smalls else {}
        bufs, small_out = gather_wait(groups[gi], sems[2 * gi], sems[2 * gi + 1], after, name=f"gather_wait{gi}", **kw)
        return gather_forward(bufs, name=f"gather_forward{gi}"), small_out

    def arrive_begin(gi, after):
        bufs, _ = gather_wait(groups[gi], sems[2 * gi], sems[2 * gi + 1], after, name=f"gather_wait{gi}")
        send, recv, bufs, _, tok = split_start(bufs, [], _forward_copies, name=f"gather_forward_start{gi}")
        return (send, recv, bufs), tok

    def arrive_end(state, after, gi):
        send, recv, bufs = state
        return split_wait(bufs, [], send, recv, _forward_copies, after, name=f"gather_forward_wait{gi}")[0]

    cb = b_conv_ffn.reshape(-1, N_CHIPS, 1, f_shard)
    wp_bf = w_pool[0].astype(BF16)
    wp_t_bf = jnp.transpose(w_pool[0], (0, 2, 1)).astype(BF16)
    ws = w_spatial[0]
    bs = b_spatial[0][:, :, None]

    (win_e, wout_e), (ca_g, sn_g, cw0, cw1) = arrive(0, token, with_smalls=True)
    wout_e = wout_e.reshape(-1, d_model)
    ca_full = jnp.transpose(ca_g, (1, 0, 2)).reshape(ca_g.shape[1], -1)
    sn_full = sn_g.reshape(1, -1)
    h1, xn0, rstd0, proj0, mix0, hn0, rstdf0 = even_layer_fwd(
        h0, norm_mix[0:1], win_e, ca_full, wp_bf, pool_scale, wout_e, norm_ffn[0:1], name="l0_fwd")
    (wg0, wu0), _ = arrive(1, hn0)
    g0, up0, gc0, act0 = ffn_in_fwd(hn0, wg0, wu0, cw0, cb[0], name="ffn0_in")
    swap2, tok2 = arrive_begin(2, act0)
    swap3, tok3 = arrive_begin(3, tok2)
    (wd0,) = arrive_end(swap2, tok3, 2)
    h2 = mm_acc(act0, wd0, h1, name="ffn0_down")
    ffn0 = (hn0, rstdf0, g0, up0, gc0)
    bufs4, _ = gather_wait(groups[4], sems[8], sems[9], h2, name="gather_wait4")
    bufs5, _ = gather_wait(groups[5], sems[10], sems[11], bufs4[0], name="gather_wait5")
    send45, recv45, bufs45, _, tok45 = split_start(bufs4 + bufs5, [], _forward_copies, name="gather_forward_start45")
    win_o, wout_o = arrive_end(swap3, tok45, 3)
    wout_o = wout_o.reshape(-1, d_model)
    h3, xn1, rstd1, p1, mix1, rstd_v = odd_layer_fwd(h2, norm_mix[1:2], win_o, sn_full, ws, bs, wout_o, name="l1_fwd")
    wg1, wu1, wd1 = arrive_end((send45, recv45, bufs45), h3, 45)
    dh4, dh4_bf, loss_row, d_final, hn1, rstdf1, g1, up1, gc1 = ffn_loss_fwd(
        h3, norm_ffn[1:2], wg1, wu1, wd1, cw1, cb[1], target, final_norm[None], name="ffn1_fwd_loss")
    ffn1 = (hn1, rstdf1, g1, up1, gc1)

    loss = lax.psum(loss_row[0, 0], ("x", "y", "c"))

    dh3, dh3_bf, (dcw1, dcb1, dnf1), red3 = _ffn_bwd(
        dh4, dh4_bf, h3, norm_ffn[1:2], ffn1, wg1, wu1, wd1, cw1, cb[1], place, 1, None)

    def as_blocks(a):
        return a.reshape(N_CHIPS, -1, d_model)

    dp1, dwin_o, dwout_o, dsn, dws, dbs = sgu_bwd(p1, mix1, xn1, dh3_bf, wout_o, rstd_v, sn_full, ws, bs,
                                                  name="l1_mix_bwd", pin=red3[-1])
    dwout_o = as_blocks(dwout_o)
    red2 = _reduce_a_begin([(dwin_o, BF16), (dwout_o, BF16)], tag="m1")
    dh2, dh2_bf, dnm1 = dx_rms_bwd([(dp1, win_o)], h2, norm_mix[1:2], rstd1, dh3, name="l1_dx", pin=red2[-1])
    red2 = _reduce_b_begin(red2, place, dh2_bf, tag="m1")

    dh1, dh1_bf, (dcw0, dcb0, dnf0), red1 = _ffn_bwd(
        dh2, dh2_bf, h1, norm_ffn[0:1], ffn0, wg0, wu0, wd0, cw0, cb[0], place, 0, red2[-1])

    dproj0, dwin_e, dwout_e, dca, dwp, dps = even_bwd(proj0, mix0, xn0, dh1_bf, wout_e, ca_full, wp_bf, wp_t_bf,
                                                      pool_scale, name="l0_mix_bwd", pin=red1[-1])
    dwout_e = as_blocks(dwout_e)
    dh0, _, dnm0 = dx_rms_bwd([(dproj0, win_e)], h0, norm_mix[0:1], rstd0, dh1, name="l0_dx")
    grad_x = dh0[None]

    small_parts = dict(
        norm_mix=jnp.concatenate([dnm0, dnm1]), norm_ffn=jnp.concatenate([dnf0, dnf1]), final_norm=d_final,
        conv_a=dca, w_pool=dwp, pool_scale=dps, sgu_norm=dsn, w_spatial=dws, b_spatial=dbs,
        conv_ffn=jnp.stack([dcw0, dcw1]), b_conv_ffn=jnp.stack([dcb0, dcb1]))
    flat = jnp.concatenate([v.reshape(-1) for v in small_parts.values()])
    pad = (-flat.shape[0]) % (N_CHIPS * 32 * 128)
    small = jnp.pad(flat, (0, pad)).reshape(N_CHIPS, -1, 128)
    red0 = _reduce_a_begin([(dwin_e, BF16), (dwout_e, BF16), (small, F32)], tag="m0")

    early = [(red3, "f1"), (red2, "m1"), (red1, "f0")]
    landed = split_wait_all([(st[3], st[4], st[1], st[2]) for st, _ in early], _stage_b_copies, red0[-1],
                            name="reduce_b_wait_early")
    halves = [sum_stage_b(p, r, place, name=f"sum_b_{tag}{i}")
              for (st, tag), recv_b in zip(early, landed) for i, (p, r) in enumerate(zip(st[0], recv_b))]
    send_c, recv_c, halves, _, tok_c = split_start(halves, [], _stage_c_copies, name="reduce_c_start_early")
    red0 = _reduce_b_begin(red0, place, tok_c, tag="m0")
    fulls = _reduce_finish((send_c, recv_c, halves, tok_c), red0[-1], tag="early")
    full3, full2, full1 = fulls[0:3], fulls[3:5], fulls[5:8]
    swap_m0 = _reduce_c_begin(red0, place, full1[0], tag="m0")
    full0 = _reduce_finish(swap_m0, swap_m0[-1], tag="m0")
    small_slots = lax.dynamic_update_index_in_dim(jnp.zeros(small.shape, F32), full0[2], chip, 0)
    small_sum = gather_chip_blocks(small_slots, name="gather_small").reshape(-1)
    grads = {
        "w_in_even": full0[0][None], "w_out_even": full0[1][None],
        "w_in_odd": full2[0][None], "w_out_odd": full2[1][None],
        }
    layered = {"w_ffn_gate": [full1[0], full3[0]], "w_ffn_up": [full1[1], full3[1]],
               "w_ffn_down": [full1[2], full3[2]]}
    off = 0
    small_red = {}
    for nm, v in small_parts.items():
        small_red[nm] = small_sum[off:off + v.size].reshape(v.shape)
        off += v.size
    for nm in ("norm_mix", "norm_ffn", "pool_scale"):
        grads[nm] = small_red[nm].reshape(weights[nm].shape)
    grads["final_norm"] = small_red["final_norm"].reshape(weights["final_norm"].shape)
    grads["w_pool"] = small_red["w_pool"][None]
    grads["w_spatial"] = small_red["w_spatial"][None]
    grads["b_spatial"] = small_red["b_spatial"].reshape(weights["b_spatial"].shape)
    grads["b_conv_ffn"] = small_red["b_conv_ffn"].reshape(weights["b_conv_ffn"].shape)
    grads["conv_a"] = lax.dynamic_slice_in_dim(small_red["conv_a"], chip * conv_a.shape[-1], conv_a.shape[-1], 1)[None]
    grads["sgu_norm"] = lax.dynamic_slice_in_dim(small_red["sgu_norm"], chip * sgu_norm.shape[-1], sgu_norm.shape[-1], 1)
    grads["conv_ffn"] = lax.dynamic_index_in_dim(small_red["conv_ffn"], chip, 1, keepdims=False)

    deltas, new_m, new_v = {}, {}, {}
    for nm, per_layer in layered.items():
        if nm == "w_ffn_down":
            grads[nm], deltas[nm], new_m[nm], new_v[nm] = adamw_layers(
                weights[nm], per_layer, m_in[nm], v_in[nm], name=f"adamw_{nm}")
        else:
            outs = adamw_layers(turned(weights[nm]), per_layer, turned(m_in[nm]), turned(v_in[nm]),
                                name=f"adamw_{nm}")
            grads[nm], deltas[nm], new_m[nm], new_v[nm] = (turned(o) for o in outs)
    for nm in order:
        if nm in layered:
            continue
        w = weights[nm]
        w2 = w[None] if w.ndim == 1 else w
        shp = w2.shape
        d, nm_, nv_ = adamw(w2, grads[nm].reshape(shp), m_in[nm].reshape(shp), v_in[nm].reshape(shp),
                            name=f"adamw_{nm}")
        deltas[nm], new_m[nm], new_v[nm] = d.reshape(w.shape), nm_.reshape(w.shape), nv_.reshape(w.shape)

    return (loss, grad_x, *[grads[n] for n in order], *[deltas[n] for n in order],
            *[new_m[n] for n in order], *[new_v[n] for n in order])
```

```python
import jax
import jax.numpy as jnp
from jax import lax
from jax.experimental import pallas as pl
from jax.experimental.pallas import tpu as pltpu

F32 = jnp.float32
BF16 = jnp.bfloat16
MESH = pl.DeviceIdType.MESH
ANY = pl.BlockSpec(memory_space=pl.ANY)

EPS = 1e-6
POOL_WINDOWS = (2, 4, 8, 16)
GROUP = 128
CHUNK = 128
N_CHIPS = 4
N_DEV = 8
ROW_TILE = 512
HALO = 16
VMEM_LIMIT = 56 * 1024 * 1024

ADAM_LR = 0.001
ADAM_B1 = 0.9
ADAM_B2 = 0.999
ADAM_EPS = 1e-08
ADAM_WD = 0.01
ADAM_STEP = 10


def _params(*sem):
    return pltpu.CompilerParams(dimension_semantics=sem, vmem_limit_bytes=VMEM_LIMIT)


def mm_acc(a, b, res, *, name):
    j_n, s, kj = a.shape
    n = b.shape[-1]
    tm = min(ROW_TILE, s)

    def body(a_ref, b_ref, r_ref, o_ref):
        acc = r_ref[...]
        for j in range(j_n):
            acc = acc + jnp.dot(a_ref[j], b_ref[j], preferred_element_type=F32)
        o_ref[...] = acc

    return pl.pallas_call(
        body, name=name, grid=(s // tm,),
        in_specs=[pl.BlockSpec((j_n, tm, kj), lambda i: (0, i, 0)),
                  pl.BlockSpec((j_n, kj, n), lambda i: (0, 0, 0)),
                  pl.BlockSpec((tm, n), lambda i: (i, 0))],
        out_specs=pl.BlockSpec((tm, n), lambda i: (i, 0)),
        out_shape=jax.ShapeDtypeStruct((s, n), F32),
        compiler_params=_params("parallel"))(a, b, res)


_NT = (((1,), (1,)), ((), ()))
_TN = (((0,), (0,)), ((), ()))


def _back(x, k):
    return pltpu.roll(x, k, 0)


def _fwd(x, k):
    return pltpu.roll(x, x.shape[0] - k, 0)


def _causal_conv(x, w_ref):
    return w_ref[0:1, :] * _back(x, 2) + w_ref[1:2, :] * _back(x, 1) + w_ref[2:3, :] * x


def _causal_conv_t(dy, w_ref):
    return w_ref[2:3, :] * dy + w_ref[1:2, :] * _fwd(dy, 1) + w_ref[0:1, :] * _fwd(dy, 2)


def _gelu(x):
    return 0.5 * x * (1.0 + lax.erf(x * 0.7071067811865476))


def _gelu_grad(x):
    return 0.5 * (1.0 + lax.erf(x * 0.7071067811865476)) + x * jnp.exp(-0.5 * x * x) * 0.3989422804014327


def _colsum(x):
    return jnp.sum(x, axis=0, keepdims=True)


def _halo_specs(n_lead, ts, width, n_tiles):
    hb = ts // HALO
    prev = pl.BlockSpec((n_lead, HALO, width), lambda i: (0, jnp.maximum(i * hb - 1, 0), 0))
    nxt = pl.BlockSpec((n_lead, HALO, width), lambda i: (0, jnp.minimum((i + 1) * hb, n_tiles * hb - 1), 0))
    return prev, nxt


def _pool_fwd(z_ext, g, pos):
    w = POOL_WINDOWS[g]
    zg = z_ext[:, g * GROUP:(g + 1) * GROUP]
    acc = zg
    sh = 1
    while sh < w:
        acc = acc + _back(acc, sh)
        sh *= 2
    return acc[HALO:] / jnp.minimum(pos, float(w)) - zg[HALO:]


def even_layer_fwd(h, gain, win, conv_a, w_pool, pool_scale, wout, gain_next, *, name, pin=None):
    s, d = h.shape
    w = win.shape[-1]
    ts = min(ROW_TILE, s)
    hb = ts // HALO

    def body(h_ref, hp_ref, gain_ref, win_ref, ca_ref, wp_ref, ps_ref, wout_ref, gn_ref, *rest):
        o_ref, xn_ref, r_ref, p_ref, m_ref, hn_ref, rn_ref = rest[-7:]
        i = pl.program_id(0)
        keep = jnp.where(i > 0, 1.0, 0.0)
        h_ext = jnp.concatenate([hp_ref[...], h_ref[...]], axis=0)
        rstd = lax.rsqrt(jnp.mean(h_ext * h_ext, axis=-1, keepdims=True) + EPS)
        xn_ext = (h_ext * rstd * gain_ref[...]).astype(BF16)
        xn_ref[...] = xn_ext[HALO:]
        r_ref[...] = rstd[HALO:]
        p32 = []
        for k in range(4):
            pk = jnp.dot(xn_ext, win_ref[k], preferred_element_type=F32).astype(BF16)
            p_ref[k] = pk[HALO:]
            pk = pk.astype(F32)
            p32.append(jnp.concatenate([pk[:HALO] * keep, pk[HALO:]], axis=0))
        m_ref[:, 0:w] = (p32[0][HALO:] * _causal_conv(p32[1] * p32[2], ca_ref)[HALO:]).astype(BF16)
        pos = (i * ts + lax.broadcasted_iota(jnp.int32, (ts, 1), 0) + 1).astype(F32)
        for g in range(len(POOL_WINDOWS)):
            pooled = _pool_fwd(p32[3], g, pos)
            mixed = jnp.dot(pooled.astype(BF16), wp_ref[g], preferred_element_type=F32)
            cols = slice(g * GROUP, (g + 1) * GROUP)
            m_ref[:, w + g * GROUP:w + (g + 1) * GROUP] = (mixed * ps_ref[:, cols]).astype(BF16)
        out = h_ref[...] + jnp.dot(m_ref[...], wout_ref[...], preferred_element_type=F32)
        o_ref[...] = out
        rstd_n = lax.rsqrt(jnp.mean(out * out, axis=-1, keepdims=True) + EPS)
        hn_ref[...] = (out * rstd_n * gn_ref[...]).astype(BF16)
        rn_ref[...] = rstd_n

    const = lambda shape: pl.BlockSpec(shape, lambda i: (0,) * len(shape))
    row = pl.BlockSpec((ts, d), lambda i: (i, 0))
    col1 = pl.BlockSpec((ts, 1), lambda i: (i, 0))
    return pl.pallas_call(
        body, name=name, grid=(s // ts,),
        in_specs=[row, pl.BlockSpec((HALO, d), lambda i: (jnp.maximum(i * hb - 1, 0), 0)), const((1, d)),
                  const((4, d, w)), const((3, w)), const((4, GROUP, GROUP)), const((1, w)), const((2 * w, d)),
                  const((1, d))] + ([ANY] if pin is not None else []),
        out_specs=[row, row, col1, pl.BlockSpec((4, ts, w), lambda i: (0, i, 0)),
                   pl.BlockSpec((ts, 2 * w), lambda i: (i, 0)), row, col1],
        out_shape=[jax.ShapeDtypeStruct((s, d), F32), jax.ShapeDtypeStruct((s, d), BF16),
                   jax.ShapeDtypeStruct((s, 1), F32), jax.ShapeDtypeStruct((4, s, w), BF16),
                   jax.ShapeDtypeStruct((s, 2 * w), BF16), jax.ShapeDtypeStruct((s, d), BF16),
                   jax.ShapeDtypeStruct((s, 1), F32)],
        compiler_params=_params("parallel"))(
            h, h, gain, win, conv_a, w_pool, pool_scale, wout, gain_next, *([pin] if pin is not None else []))


def even_bwd(proj, mix, xn, dh_bf, wout, conv_a, w_pool, w_pool_t, pool_scale, *, name, pin=None):
    _, s, w = proj.shape
    d = dh_bf.shape[1]
    ts = min(ROW_TILE, s)
    n_t = s // ts
    prev, nxt = _halo_specs(4, ts, w, n_t)
    hb = ts // HALO
    n_ext = ts + HALO

    def body(p_ref, pp_ref, pn_ref, m_ref, x_ref, dh_ref, dhn_ref, wout_ref, ca_ref, wp_ref, wpt_ref, ps_ref, *rest):
        dp_ref, dwi_ref, dwo_ref, dca_ref, dwp_ref, dps_ref = rest[-6:]
        i = pl.program_id(0)

        @pl.when(i == 0)
        def _():
            dwi_ref[...] = jnp.zeros_like(dwi_ref)
            dwo_ref[...] = jnp.zeros_like(dwo_ref)
            dca_ref[...] = jnp.zeros_like(dca_ref)
            dwp_ref[...] = jnp.zeros_like(dwp_ref)
            dps_ref[...] = jnp.zeros_like(dps_ref)

        keep_p = jnp.where(i > 0, 1.0, 0.0)
        keep_n = jnp.where(i < n_t - 1, 1.0, 0.0)
        dwo_ref[...] += lax.dot_general(m_ref[...], dh_ref[...], _TN, preferred_element_type=F32)
        dmix = lax.dot_general(jnp.concatenate([dh_ref[...], dhn_ref[...]], axis=0), wout_ref[...], _NT,
                               preferred_element_type=F32)
        a_b, a_c, a_v = (p_ref[k].astype(F32) for k in range(3))
        cv_ext = jnp.concatenate([pp_ref[1].astype(F32) * pp_ref[2].astype(F32) * keep_p, a_c * a_v], axis=0)
        dy_a = dmix[:ts, 0:w]
        dp_ref[0] = (dy_a * _causal_conv(cv_ext, ca_ref)[HALO:]).astype(BF16)
        dcc = dy_a * a_b
        dca_ref[2:3, :] += _colsum(dcc * cv_ext[HALO:])
        dca_ref[1:2, :] += _colsum(dcc * _back(cv_ext, 1)[HALO:])
        dca_ref[0:1, :] += _colsum(dcc * _back(cv_ext, 2)[HALO:])
        dcc_ext = jnp.concatenate([dcc, dmix[ts:, 0:w] * pn_ref[0].astype(F32) * keep_n], axis=0)
        dcv = _causal_conv_t(dcc_ext, ca_ref)[:ts]
        dp_ref[1] = (dcv * a_v).astype(BF16)
        dp_ref[2] = (dcv * a_c).astype(BF16)
        z_ext = jnp.concatenate([pp_ref[3].astype(F32) * keep_p, p_ref[3].astype(F32)], axis=0)
        pos = (i * ts + lax.broadcasted_iota(jnp.int32, (ts, 1), 0) + 1).astype(F32)
        pos_ext = (i * ts + lax.broadcasted_iota(jnp.int32, (n_ext, 1), 0) + 1).astype(F32)
        for g, win in enumerate(POOL_WINDOWS):
            cols = slice(g * GROUP, (g + 1) * GROUP)
            ycols = slice(w + g * GROUP, w + (g + 1) * GROUP)
            pooled = _pool_fwd(z_ext, g, pos).astype(BF16)
            mixed = jnp.dot(pooled, wp_ref[g], preferred_element_type=F32)
            dy_b = dmix[:ts, ycols]
            dps_ref[:, cols] += _colsum(dy_b * mixed)
            dmixed_ext = jnp.concatenate([dy_b, dmix[ts:, ycols] * keep_n], axis=0) * ps_ref[:, cols]
            dmixed_ext = dmixed_ext.astype(BF16)
            dwp_ref[g] += lax.dot_general(pooled, dmixed_ext[:ts], _TN, preferred_element_type=F32)
            dpooled = jnp.dot(dmixed_ext, wpt_ref[g], preferred_element_type=F32)
            acc = dpooled / jnp.minimum(pos_ext, float(win))
            sh = 1
            while sh < win:
                acc = acc + _fwd(acc, sh)
                sh *= 2
            dp_ref[3, :, cols] = (acc[:ts] - dpooled[:ts]).astype(BF16)
        for k in range(4):
            dwi_ref[k] += lax.dot_general(x_ref[...], dp_ref[k], _TN, preferred_element_type=F32)

    tile4 = pl.BlockSpec((4, ts, w), lambda i: (0, i, 0))
    const = lambda shape: pl.BlockSpec(shape, lambda i: (0,) * len(shape))
    return pl.pallas_call(
        body, name=name, grid=(n_t,),
        in_specs=[tile4, prev, nxt, pl.BlockSpec((ts, 2 * w), lambda i: (i, 0)), pl.BlockSpec((ts, d), lambda i: (i, 0)),
                  pl.BlockSpec((ts, d), lambda i: (i, 0)),
                  pl.BlockSpec((HALO, d), lambda i: (jnp.minimum((i + 1) * hb, n_t * hb - 1), 0)),
                  const((2 * w, d)), const((3, w)), const((4, GROUP, GROUP)), const((4, GROUP, GROUP)), const((1, w))]
        + ([ANY] if pin is not None else []),
        out_specs=[tile4, const((4, d, w)), const((2 * w, d)), const((3, w)), const((4, GROUP, GROUP)), const((1, w))],
        out_shape=[jax.ShapeDtypeStruct((4, s, w), BF16), jax.ShapeDtypeStruct((4, d, w), F32),
                   jax.ShapeDtypeStruct((2 * w, d), F32), jax.ShapeDtypeStruct((3, w), F32),
                   jax.ShapeDtypeStruct((4, GROUP, GROUP), F32), jax.ShapeDtypeStruct((1, w), F32)],
        compiler_params=_params("arbitrary"))(
            proj, proj, proj, mix, xn, dh_bf, dh_bf, wout, conv_a, w_pool, w_pool_t, pool_scale,
            *([pin] if pin is not None else []))


def _ffn_next_halo(ts, f, n_t):
    hb = ts // HALO
    return pl.BlockSpec((None, HALO, f), lambda j, i: (j, jnp.minimum((i + 1) * hb, n_t * hb - 1), 0))


def ffn_in_fwd(hn, wg, wu, cw, cb, *, name, pin=None):
    s, d = hn.shape
    j_n, f, _ = wg.shape
    tm = min(ROW_TILE, s)
    hb = tm // HALO

    def body(x_ref, xp_ref, wg_ref, wu_ref, cw_ref, cb_ref, *rest):
        g_ref, u_ref, gc_ref, a_ref = rest[-4:]
        i, j = pl.program_id(0), pl.program_id(1)
        x_ext = jnp.concatenate([xp_ref[...], x_ref[...]], axis=0)
        g_ext = lax.dot_general(x_ext, wg_ref[j], _NT, preferred_element_type=F32).astype(BF16)
        up = lax.dot_general(x_ref[...], wu_ref[j], _NT, preferred_element_type=F32).astype(BF16)
        g_ref[...] = g_ext[HALO:]
        u_ref[...] = up
        a_ref[...] = _ffn_act(g_ext, up, cw_ref, cb_ref, gc_ref, i)

    whole = pl.BlockSpec((j_n, f, d), lambda i, j: (0, 0, 0))
    tile = pl.BlockSpec((None, tm, f), lambda i, j: (j, i, 0))
    shape = jax.ShapeDtypeStruct((j_n, s, f), BF16)
    return pl.pallas_call(
        body, name=name, grid=(s // tm, j_n),
        in_specs=[pl.BlockSpec((tm, d), lambda i, j: (i, 0)),
                  pl.BlockSpec((HALO, d), lambda i, j: (jnp.maximum(i * hb - 1, 0), 0)),
                  whole, whole,
                  pl.BlockSpec((None, 3, f), lambda i, j: (j, 0, 0)),
                  pl.BlockSpec((None, 1, f), lambda i, j: (j, 0, 0))] + ([ANY] if pin is not None else []),
        out_specs=[tile] * 4, out_shape=[shape] * 4,
        compiler_params=_params("parallel", "parallel"))(hn, hn, wg, wu, cw, cb, *([pin] if pin is not None else []))


def _ffn_act(g_ext, up, cw_ref, cb_ref, gc_ref, i):
    keep = jnp.where(i > 0, 1.0, 0.0)
    g32 = jnp.concatenate([g_ext[:HALO].astype(F32) * keep, g_ext[HALO:].astype(F32)], axis=0)
    gc = (_causal_conv(g32, cw_ref)[HALO:] + cb_ref[...]).astype(BF16)
    gc_ref[...] = gc
    gc = gc.astype(F32)
    return (gc * jax.nn.sigmoid(gc) * up.astype(F32)).astype(BF16)


def ffn_loss_fwd(h, gain, wg, wu, wd, cw, cb, target, final_gain, *, name):
    s, d = h.shape
    j_n, f, _ = wg.shape
    tm = min(ROW_TILE, s)
    hb = tm // HALO

    def body(h_ref, hp_ref, gain_ref, wg_ref, wu_ref, wd_ref, cw_ref, cb_ref, t_ref, fg_ref,
             o_ref, ob_ref, l_ref, dfg_ref, xn_ref, r_ref, g_ref, u_ref, gc_ref, x_s, acc_s):
        i, j = pl.program_id(0), pl.program_id(1)

        @pl.when((i == 0) & (j == 0))
        def _():
            l_ref[...] = jnp.zeros_like(l_ref)
            dfg_ref[...] = jnp.zeros_like(dfg_ref)

        @pl.when(j == 0)
        def _():
            h_ext = jnp.concatenate([hp_ref[...], h_ref[...]], axis=0)
            rstd = lax.rsqrt(jnp.mean(h_ext * h_ext, axis=-1, keepdims=True) + EPS)
            x_s[...] = (h_ext * rstd * gain_ref[...]).astype(BF16)
            xn_ref[...] = x_s[HALO:, :]
            r_ref[...] = rstd[HALO:]
            acc_s[...] = h_ref[...]

        g_ext = lax.dot_general(x_s[...], wg_ref[j], _NT, preferred_element_type=F32).astype(BF16)
        up = lax.dot_general(x_s[HALO:, :], wu_ref[j], _NT, preferred_element_type=F32).astype(BF16)
        g_ref[...] = g_ext[HALO:]
        u_ref[...] = up
        act = _ffn_act(g_ext, up, cw_ref, cb_ref, gc_ref, i)
        acc_s[...] += jnp.dot(act, wd_ref[j], preferred_element_type=F32)

        @pl.when(j == j_n - 1)
        def _():
            x = acc_s[...]
            rstd = lax.rsqrt(jnp.mean(x * x, axis=-1, keepdims=True) + EPS)
            hhat = x * rstd
            err = hhat * fg_ref[...] - t_ref[...]
            l_ref[...] += 0.5 * jnp.sum(jnp.mean(err * err, axis=-1, keepdims=True), axis=0, keepdims=True)
            dy = err * (1.0 / d)
            dfg_ref[...] += _colsum(dy * hhat)
            dyg = dy * fg_ref[...]
            dh = rstd * (dyg - hhat * jnp.mean(dyg * hhat, axis=-1, keepdims=True))
            o_ref[...] = dh
            ob_ref[...] = dh.astype(BF16)

    whole = pl.BlockSpec((j_n, f, d), lambda i, j: (0, 0, 0), pipeline_mode=pl.Buffered(1))
    row = pl.BlockSpec((tm, d), lambda i, j: (i, 0))
    vec = pl.BlockSpec((1, d), lambda i, j: (0, 0))
    tile = pl.BlockSpec((None, tm, f), lambda i, j: (j, i, 0))
    return pl.pallas_call(
        body, name=name, grid=(s // tm, j_n),
        in_specs=[row, pl.BlockSpec((HALO, d), lambda i, j: (jnp.maximum(i * hb - 1, 0), 0)),
                  vec, whole, whole, whole,
                  pl.BlockSpec((None, 3, f), lambda i, j: (j, 0, 0)),
                  pl.BlockSpec((None, 1, f), lambda i, j: (j, 0, 0)), row, vec],
        out_specs=[row, row, pl.BlockSpec((1, 128), lambda i, j: (0, 0)), vec,
                   row, pl.BlockSpec((tm, 1), lambda i, j: (i, 0)), tile, tile, tile],
        out_shape=[jax.ShapeDtypeStruct((s, d), F32), jax.ShapeDtypeStruct((s, d), BF16),
                   jax.ShapeDtypeStruct((1, 128), F32), jax.ShapeDtypeStruct((1, d), F32),
                   jax.ShapeDtypeStruct((s, d), BF16), jax.ShapeDtypeStruct((s, 1), F32)]
        + [jax.ShapeDtypeStruct((j_n, s, f), BF16)] * 3,
        scratch_shapes=[pltpu.VMEM((HALO + tm, d), BF16), pltpu.VMEM((tm, d), F32)],
        compiler_params=_params("arbitrary", "arbitrary"))(h, h, gain, wg, wu, wd, cw, cb, target, final_gain)


def ffn_bwd_a(dh_bf, hn, g, up, gc, wd, cw, *, name, pin=None):
    s, d = hn.shape
    j_n, _, f = g.shape
    tm = min(ROW_TILE, s)
    n_t = s // tm
    hb = tm // HALO

    def body(dh_ref, dhn_ref, x_ref, g_ref, gc_ref, gcn_ref, u_ref, un_ref, wd_ref, cw_ref, *rest):
        dg_ref, du_ref, dwg_ref, dwu_ref, dwd_ref, dcw_ref, dcb_ref = rest[-7:]
        i = pl.program_id(1)

        @pl.when(i == 0)
        def _():
            for r in (dwg_ref, dwu_ref, dwd_ref, dcw_ref, dcb_ref):
                r[...] = jnp.zeros_like(r)

        keep_n = jnp.where(i < n_t - 1, 1.0, 0.0)
        dh = dh_ref[...]
        dact = lax.dot_general(jnp.concatenate([dh, dhn_ref[...]], axis=0), wd_ref[...], _NT,
                               preferred_element_type=F32)
        dact = jnp.concatenate([dact[:tm], dact[tm:] * keep_n], axis=0)
        gc_ext = jnp.concatenate([gc_ref[...], gcn_ref[...]], axis=0).astype(F32)
        sig = jax.nn.sigmoid(gc_ext)
        silu = gc_ext * sig
        up_ext = jnp.concatenate([u_ref[...], un_ref[...]], axis=0).astype(F32)
        act = (silu * up_ext)[:tm].astype(BF16)
        dwd_ref[...] += lax.dot_general(act, dh, _TN, preferred_element_type=F32)
        dup = (dact * silu)[:tm].astype(BF16)
        du_ref[...] = dup
        dgc = dact * up_ext * (sig + silu * (1.0 - sig))
        dgc_1, dgc_2 = _fwd(dgc, 1), _fwd(dgc, 2)
        dg = (cw_ref[2:3, :] * dgc + cw_ref[1:2, :] * dgc_1 + cw_ref[0:1, :] * dgc_2)[:tm].astype(BF16)
        dg_ref[...] = dg
        x = x_ref[...]
        dwg_ref[...] += lax.dot_general(dg, x, _TN, preferred_element_type=F32)
        dwu_ref[...] += lax.dot_general(dup, x, _TN, preferred_element_type=F32)
        g32 = g_ref[...].astype(F32)
        dcb_ref[...] += _colsum(dgc[:tm])
        dcw_ref[2:3, :] += _colsum(dgc[:tm] * g32)
        dcw_ref[1:2, :] += _colsum(dgc_1[:tm] * g32)
        dcw_ref[0:1, :] += _colsum(dgc_2[:tm] * g32)

    rows = pl.BlockSpec((tm, d), lambda j, i: (i, 0))
    rows_next = pl.BlockSpec((HALO, d), lambda j, i: (jnp.minimum((i + 1) * hb, n_t * hb - 1), 0))
    tile = pl.BlockSpec((None, tm, f), lambda j, i: (j, i, 0))
    nxt = _ffn_next_halo(tm, f, n_t)
    per_j = lambda r, c: pl.BlockSpec((None, r, c), lambda j, i: (j, 0, 0))
    return pl.pallas_call(
        body, name=name, grid=(j_n, n_t),
        in_specs=[rows, rows_next, rows, tile, tile, nxt, tile, nxt, per_j(f, d), per_j(3, f)]
        + ([ANY] if pin is not None else []),
        out_specs=[tile, tile, per_j(f, d), per_j(f, d), per_j(f, d), per_j(3, f), per_j(1, f)],
        out_shape=[jax.ShapeDtypeStruct((j_n, s, f), BF16), jax.ShapeDtypeStruct((j_n, s, f), BF16),
                   jax.ShapeDtypeStruct((j_n, f, d), F32), jax.ShapeDtypeStruct((j_n, f, d), F32),
                   jax.ShapeDtypeStruct((j_n, f, d), F32), jax.ShapeDtypeStruct((j_n, 3, f), F32),
                   jax.ShapeDtypeStruct((j_n, 1, f), F32)],
        compiler_params=_params("parallel", "arbitrary"))(
            dh_bf, dh_bf, hn, g, gc, gc, up, up, wd, cw, *([pin] if pin is not None else []))


def dx_rms_bwd(pairs, h, gain, rstd, dres, *, name, pin=None):
    j_n, s, f = pairs[0][0].shape
    d = h.shape[1]
    tm = min(ROW_TILE, s)
    n_p = len(pairs)
    dims = [_NT if w.shape[1:] == (d, f) else (((1,), (0,)), ((), ())) for _, w in pairs]

    def body(*refs):
        dy_refs, w_hbm = refs[:n_p], refs[n_p:2 * n_p]
        h_ref, g_ref, r_ref, dr_ref = refs[2 * n_p:2 * n_p + 4]
        o_ref, ob_ref, dgain_ref = refs[-(n_p + 4):-(n_p + 1)]
        w_refs, sems = refs[-(n_p + 1):-1], refs[-1]
        first = pl.program_id(0) == 0

        def load(p, j):
            return pltpu.make_async_copy(w_hbm[p].at[j], w_refs[p].at[j], sems.at[p * j_n + j])

        @pl.when(first)
        def _():
            dgain_ref[...] = jnp.zeros_like(dgain_ref)
            for j in range(j_n):
                for p in range(n_p):
                    load(p, j).start()
        dx = jnp.zeros((tm, d), F32)
        for j in range(j_n):
            @pl.when(first)
            def _():
                for p in range(n_p):
                    load(p, j).wait()
            for p in range(n_p):
                dx = dx + lax.dot_general(dy_refs[p][j], w_refs[p][j], dims[p], preferred_element_type=F32)
        rstd_v = r_ref[...]
        hhat = h_ref[...] * rstd_v
        dgain_ref[...] += _colsum(dx * hhat)
        dxg = dx * g_ref[...]
        dh = dr_ref[...] + rstd_v * (dxg - hhat * jnp.mean(dxg * hhat, axis=-1, keepdims=True))
        o_ref[...] = dh
        ob_ref[...] = dh.astype(BF16)

    tile4 = pl.BlockSpec((j_n, tm, f), lambda i: (0, i, 0))
    row = pl.BlockSpec((tm, d), lambda i: (i, 0))
    vec = pl.BlockSpec((1, d), lambda i: (0, 0))
    return pl.pallas_call(
        body, name=name, grid=(s // tm,),
        in_specs=[tile4] * n_p + [ANY] * n_p + [row, vec, pl.BlockSpec((tm, 1), lambda i: (i, 0)), row]
        + ([ANY] if pin is not None else []),
        out_specs=[row, row, vec],
        out_shape=[jax.ShapeDtypeStruct((s, d), F32), jax.ShapeDtypeStruct((s, d), BF16),
                   jax.ShapeDtypeStruct((1, d), F32)],
        scratch_shapes=[pltpu.VMEM(w.shape, w.dtype) for _, w in pairs] + [pltpu.SemaphoreType.DMA((n_p * j_n,))],
        compiler_params=_params("arbitrary"))(
            *[p[0] for p in pairs], *[p[1] for p in pairs], h, gain, rstd, dres, *([pin] if pin is not None else []))


def _sgu_gate(vn_bf, ws_ref, bs_ref, h, rows):
    tri = lax.broadcasted_iota(jnp.int32, (CHUNK, CHUNK), 0) >= lax.broadcasted_iota(jnp.int32, (CHUNK, CHUNK), 1)
    ws = jnp.where(tri, ws_ref[h], 0.0).astype(BF16)
    cols = slice((h % 4) * GROUP, (h % 4 + 1) * GROUP)
    return ws, jnp.dot(ws, vn_bf[h // 4][rows, cols], preferred_element_type=F32) + bs_ref[h]


def odd_layer_fwd(h, gain, win, sgu_norm, w_spatial, b_spatial, wout, *, name):
    s, d = h.shape
    w = win.shape[-1]
    ts = min(ROW_TILE, s)
    n_heads = w_spatial.shape[0]

    def body(h_ref, gain_ref, win_ref, n_ref, ws_ref, bs_ref, wout_ref, o_ref, xn_ref, r_ref, p_ref, m_ref, rv_ref):
        x = h_ref[...]
        rstd_x = lax.rsqrt(jnp.mean(x * x, axis=-1, keepdims=True) + EPS)
        xn = (x * rstd_x * gain_ref[...]).astype(BF16)
        xn_ref[...] = xn
        r_ref[...] = rstd_x
        for k in range(4):
            p_ref[k] = jnp.dot(xn, win_ref[k], preferred_element_type=F32).astype(BF16)
        v = [_gelu(p_ref[2].astype(F32)), _gelu(p_ref[3].astype(F32))]
        ms = (jnp.sum(v[0] * v[0], axis=-1, keepdims=True) + jnp.sum(v[1] * v[1], axis=-1, keepdims=True)) / (2 * w)
        rstd = lax.rsqrt(ms + EPS)
        rv_ref[...] = rstd
        vn = [(v[k] * rstd * n_ref[:, k * w:(k + 1) * w]).astype(BF16) for k in range(2)]
        for hd in range(n_heads):
            cols = slice((hd % 4) * GROUP, (hd % 4 + 1) * GROUP)
            for c in range(ts // CHUNK):
                rows = slice(c * CHUNK, (c + 1) * CHUNK)
                _, gate = _sgu_gate(vn, ws_ref, bs_ref, hd, rows)
                u = _gelu(p_ref[hd // 4, rows, cols].astype(F32))
                m_ref[rows, hd * GROUP:(hd + 1) * GROUP] = (u * gate).astype(BF16)
        o_ref[...] = x + jnp.dot(m_ref[...], wout_ref[...], preferred_element_type=F32)

    const = lambda shape: pl.BlockSpec(shape, lambda i: (0,) * len(shape))
    row = pl.BlockSpec((ts, d), lambda i: (i, 0))
    col1 = pl.BlockSpec((ts, 1), lambda i: (i, 0))
    return pl.pallas_call(
        body, name=name, grid=(s // ts,),
        in_specs=[row, const((1, d)), const((4, d, w)), const((1, 2 * w)),
                  const((n_heads, CHUNK, CHUNK)), const((n_heads, CHUNK, 1)), const((2 * w, d))],
        out_specs=[row, row, col1, pl.BlockSpec((4, ts, w), lambda i: (0, i, 0)),
                   pl.BlockSpec((ts, 2 * w), lambda i: (i, 0)), col1],
        out_shape=[jax.ShapeDtypeStruct((s, d), F32), jax.ShapeDtypeStruct((s, d), BF16),
                   jax.ShapeDtypeStruct((s, 1), F32), jax.ShapeDtypeStruct((4, s, w), BF16),
                   jax.ShapeDtypeStruct((s, 2 * w), BF16), jax.ShapeDtypeStruct((s, 1), F32)],
        compiler_params=_params("parallel"))(h, gain, win, sgu_norm, w_spatial, b_spatial, wout)


def sgu_bwd(p, mix, xn, dh_bf, wout, rstd, sgu_norm, w_spatial, b_spatial, *, name, pin=None):
    _, s, w = p.shape
    d = dh_bf.shape[1]
    ts = min(ROW_TILE, s)
    n_heads = w_spatial.shape[0]

    def body(p_ref, m_ref, x_ref, dh_ref, wout_ref, r_ref, n_ref, ws_ref, bs_ref, *rest):
        dp_ref, dwi_ref, dwo_ref, dn_ref, dws_ref, dbs_ref, dvn_ref, dm_ref = rest[-8:]

        @pl.when(pl.program_id(0) == 0)
        def _():
            dwi_ref[...] = jnp.zeros_like(dwi_ref)
            dwo_ref[...] = jnp.zeros_like(dwo_ref)
            dn_ref[...] = jnp.zeros_like(dn_ref)
            dws_ref[...] = jnp.zeros_like(dws_ref)
            dbs_ref[...] = jnp.zeros_like(dbs_ref)

        dwo_ref[...] += lax.dot_general(m_ref[...], dh_ref[...], _TN, preferred_element_type=F32)
        dm_ref[...] = lax.dot_general(dh_ref[...], wout_ref[...], _NT, preferred_element_type=F32)
        rstd_v = r_ref[...]
        vhat = [_gelu(p_ref[2 + k].astype(F32)) * rstd_v for k in range(2)]
        vn = [(vhat[k] * n_ref[:, k * w:(k + 1) * w]).astype(BF16) for k in range(2)]
        tri = lax.broadcasted_iota(jnp.int32, (CHUNK, CHUNK), 0) >= lax.broadcasted_iota(jnp.int32, (CHUNK, CHUNK), 1)
        for h in range(n_heads):
            cols = slice((h % 4) * GROUP, (h % 4 + 1) * GROUP)
            ocols = slice(h * GROUP, (h + 1) * GROUP)
            for c in range(ts // CHUNK):
                rows = slice(c * CHUNK, (c + 1) * CHUNK)
                ws, gate = _sgu_gate(vn, ws_ref, bs_ref, h, rows)
                pu = p_ref[h // 4, rows, cols].astype(F32)
                dm = dm_ref[rows, ocols]
                dp_ref[h // 4, rows, cols] = (dm * gate * _gelu_grad(pu)).astype(BF16)
                dgate = dm * _gelu(pu)
                dbs_ref[h] += jnp.sum(dgate, axis=-1, keepdims=True)
                dgate_bf = dgate.astype(BF16)
                dws = lax.dot_general(dgate_bf, vn[h // 4][rows, cols], _NT, preferred_element_type=F32)
                dws_ref[h] += jnp.where(tri, dws, 0.0)
                dvn_ref[rows, ocols] = lax.dot_general(ws, dgate_bf, _TN, preferred_element_type=F32)
        for k in range(2):
            kc = slice(k * w, (k + 1) * w)
            dvn = dvn_ref[:, kc]
            dn_ref[:, kc] += _colsum(dvn * vhat[k])
        dvh = [dvn_ref[:, k * w:(k + 1) * w] * n_ref[:, k * w:(k + 1) * w] for k in range(2)]
        dot = (jnp.sum(dvh[0] * vhat[0], axis=-1, keepdims=True)
               + jnp.sum(dvh[1] * vhat[1], axis=-1, keepdims=True)) / (2 * w)
        for k in range(2):
            dv = rstd_v * (dvh[k] - vhat[k] * dot)
            dp_ref[2 + k] = (dv * _gelu_grad(p_ref[2 + k].astype(F32))).astype(BF16)
        for k in range(4):
            dwi_ref[k] += lax.dot_general(x_ref[...], dp_ref[k], _TN, preferred_element_type=F32)

    const = lambda shape: pl.BlockSpec(shape, lambda i: (0,) * len(shape))
    tile4 = pl.BlockSpec((4, ts, w), lambda i: (0, i, 0))
    return pl.pallas_call(
        body, name=name, grid=(s // ts,),
        in_specs=[tile4, pl.BlockSpec((ts, 2 * w), lambda i: (i, 0)), pl.BlockSpec((ts, d), lambda i: (i, 0)),
                  pl.BlockSpec((ts, d), lambda i: (i, 0)), const((2 * w, d)), pl.BlockSpec((ts, 1), lambda i: (i, 0)),
                  const((1, 2 * w)), const((n_heads, CHUNK, CHUNK)), const((n_heads, CHUNK, 1))]
        + ([ANY] if pin is not None else []),
        out_specs=[tile4, const((4, d, w)), const((2 * w, d)), const((1, 2 * w)), const((n_heads, CHUNK, CHUNK)),
                   const((n_heads, CHUNK, 1))],
        out_shape=[jax.ShapeDtypeStruct((4, s, w), BF16), jax.ShapeDtypeStruct((4, d, w), F32),
                   jax.ShapeDtypeStruct((2 * w, d), F32),
                   jax.ShapeDtypeStruct((1, 2 * w), F32),
                   jax.ShapeDtypeStruct((n_heads, CHUNK, CHUNK), F32),
                   jax.ShapeDtypeStruct((n_heads, CHUNK, 1), F32)],
        scratch_shapes=[pltpu.VMEM((ts, 2 * w), F32), pltpu.VMEM((ts, 2 * w), F32)],
        compiler_params=_params("arbitrary"))(
            p, mix, xn, dh_bf, wout, rstd, sgu_norm, w_spatial, b_spatial, *([pin] if pin is not None else []))


def _row_tile(rows):
    if rows <= ROW_TILE:
        return rows
    for t in (512, 384, 352, 256, 128, 64, 32, 16, 8):
        if rows % t == 0:
            return t
    return rows


def adamw(w, g, m, v, *, name):
    shape = w.shape
    cols = shape[-1]
    rows = w.size // cols
    w2, g2, m2, v2 = (a.reshape(rows, cols) for a in (w, g, m, v))
    tr = _row_tile(rows)
    bc1 = 1.0 - ADAM_B1 ** ADAM_STEP
    bc2 = 1.0 - ADAM_B2 ** ADAM_STEP

    def body(w_ref, g_ref, m_ref, v_ref, d_ref, nm_ref, nv_ref):
        grad = g_ref[...]
        m_new = ADAM_B1 * m_ref[...] + (1.0 - ADAM_B1) * grad
        v_new = ADAM_B2 * v_ref[...] + (1.0 - ADAM_B2) * (grad * grad)
        nm_ref[...] = m_new
        nv_ref[...] = v_new
        d_ref[...] = -ADAM_LR * ((m_new / bc1) / (jnp.sqrt(v_new / bc2) + ADAM_EPS) + ADAM_WD * w_ref[...])

    spec = pl.BlockSpec((tr, cols), lambda i: (i, 0))
    outs = pl.pallas_call(
        body, name=name, grid=(rows // tr,),
        in_specs=[spec] * 4, out_specs=[spec] * 3,
        out_shape=[jax.ShapeDtypeStruct((rows, cols), F32)] * 3,
        compiler_params=_params("parallel"))(w2, g2, m2, v2)
    return tuple(o.reshape(shape) for o in outs)


def adamw_layers(w, grads, m, v, *, name):
    n_l, rows, cols = w.shape
    tr = _row_tile(rows)
    bc1 = 1.0 - ADAM_B1 ** ADAM_STEP
    bc2 = 1.0 - ADAM_B2 ** ADAM_STEP
    outs = None
    for l in range(n_l):
        def body(w_ref, g_ref, m_ref, v_ref, *rest):
            go_ref, d_ref, nm_ref, nv_ref = rest[-4:]
            grad = g_ref[...]
            m_new = ADAM_B1 * m_ref[...] + (1.0 - ADAM_B1) * grad
            v_new = ADAM_B2 * v_ref[...] + (1.0 - ADAM_B2) * (grad * grad)
            go_ref[...] = grad
            nm_ref[...] = m_new
            nv_ref[...] = v_new
            d_ref[...] = -ADAM_LR * ((m_new / bc1) / (jnp.sqrt(v_new / bc2) + ADAM_EPS) + ADAM_WD * w_ref[...])

        layer = pl.BlockSpec((None, tr, cols), lambda i, l=l: (l, i, 0))
        prev = list(outs) if outs is not None else []
        outs = pl.pallas_call(
            body, name=f"{name}{l}", grid=(rows // tr,),
            in_specs=[layer, pl.BlockSpec((tr, cols), lambda i: (i, 0)), layer, layer] + [ANY] * len(prev),
            out_specs=[layer] * 4,
            out_shape=[jax.ShapeDtypeStruct(w.shape, F32)] * 4,
            input_output_aliases={4 + k: k for k in range(len(prev))},
            compiler_params=_params("parallel"))(w, grads[l], m, v, *prev)
    return tuple(outs)


def _place():
    return lax.axis_index("x"), lax.axis_index("y"), lax.axis_index("c")


def _other_chips(x, y):
    return [(1 - x, y), (x, 1 - y), (1 - x, 1 - y)]


HBM = pl.BlockSpec(memory_space=pltpu.HBM)
SEM = pl.BlockSpec(memory_space=pltpu.SEMAPHORE)
DATAFLOW = pltpu.SideEffectType.DATAFLOW_SIDE_EFFECTING


def _in_hbm(a):
    return pltpu.with_memory_space_constraint(a, pltpu.HBM)


def cast_into_slot(w, chip, *, l=None, name, pin=None):
    rows, cols = w.shape[-2:]
    tr = _row_tile(rows)

    def body(chip_ref, w_ref, *rest):
        rest[-1][...] = w_ref[...].astype(BF16)

    in_spec = (pl.BlockSpec((tr, cols), lambda i, chip_ref: (i, 0)) if l is None
               else pl.BlockSpec((None, tr, cols), lambda i, chip_ref: (l, i, 0)))
    return pl.pallas_call(
        body, name=name,
        grid_spec=pltpu.PrefetchScalarGridSpec(
            num_scalar_prefetch=1, grid=(rows // tr,), in_specs=[in_spec] + ([ANY] if pin is not None else []),
            out_specs=pl.BlockSpec((None, tr, cols), lambda i, chip_ref: (chip_ref[0], i, 0))),
        out_shape=jax.ShapeDtypeStruct((N_CHIPS, rows, cols), BF16),
        compiler_params=_params("parallel"))(chip, w, *([pin] if pin is not None else []))


def _half(ref, slot, c):
    half = ref.shape[1] // 2
    return ref.at[slot, pl.ds(c * half, half), :]


def gather_start(groups, smalls, *, name):
    flat = [b for g in groups for b in g]
    n_b, n_s, n_g = len(flat), len(smalls), len(groups)
    n_sem = 2 * n_g + (2 if n_s else 0)

    def body(*refs):
        bufs, small_refs = refs[:n_b], refs[n_b:n_b + n_s]
        sems = refs[n_b + n_s:n_b + n_s + n_sem]
        token = refs[-1]
        x, y, c = _place()
        me = 2 * x + y
        chips = _other_chips(x, y)
        for si in range(n_s):
            piece = small_refs[si].at[me]
            for k, (px, py) in enumerate(chips):
                pltpu.make_async_remote_copy(
                    src_ref=piece, dst_ref=piece,
                    send_sem=sems[2 * n_g].at[3 * si + k], recv_sem=sems[2 * n_g + 1].at[3 * si + k],
                    device_id=(px, py, c), device_id_type=MESH).start()
        t = 0
        for gi, group in enumerate(groups):
            for ti in range(len(group)):
                piece = _half(bufs[t], me, c)
                t += 1
                for k, (px, py) in enumerate(chips):
                    pltpu.make_async_remote_copy(
                        src_ref=piece, dst_ref=piece,
                        send_sem=sems[2 * gi].at[3 * ti + k], recv_sem=sems[2 * gi + 1].at[3 * ti + k],
                        device_id=(px, py, c), device_id_type=MESH).start()
        token[...] = jnp.zeros_like(token)

    sem_shapes = []
    for group in groups:
        sem_shapes += [pltpu.SemaphoreType.DMA((3 * len(group),))] * 2
    if n_s:
        sem_shapes += [pltpu.SemaphoreType.DMA((3 * n_s,))] * 2
    arrays = flat + list(smalls)
    res = pl.pallas_call(
        body, name=name,
        out_shape=tuple(sem_shapes) + tuple(pltpu.HBM(a.shape, a.dtype) for a in arrays)
        + (jax.ShapeDtypeStruct((8, 128), F32),),
        in_specs=[HBM] * len(arrays),
        out_specs=tuple([SEM] * n_sem + [HBM] * len(arrays) + [pl.BlockSpec(memory_space=pltpu.VMEM)]),
        input_output_aliases={i: n_sem + i for i in range(len(arrays))},
        compiler_params=pltpu.CompilerParams(has_side_effects=DATAFLOW))(*[_in_hbm(a) for a in arrays])
    sems, thru, token = res[:n_sem], res[n_sem:-1], res[-1]
    out_groups, t = [], 0
    for group in groups:
        out_groups.append(list(thru[t:t + len(group)]))
        t += len(group)
    return sems, out_groups, list(thru[n_b:]), token


def gather_wait(bufs, send, recv, after, *, name, smalls=(), small_send=None, small_recv=None):
    n_b, n_s = len(bufs), len(smalls)
    arrays = list(bufs) + list(smalls)
    sem_ops = [send, recv] + ([small_send, small_recv] if n_s else [])

    def body(*refs):
        buf_refs, small_refs = refs[:n_b], refs[n_b:n_b + n_s]
        sems = refs[n_b + n_s:n_b + n_s + len(sem_ops)]
        x, y, c = _place()
        me = 2 * x + y
        chips = _other_chips(x, y)
        for ti in range(n_b):
            for k, (px, py) in enumerate(chips):
                done = pltpu.make_async_remote_copy(
                    src_ref=_half(buf_refs[ti], me, c), dst_ref=_half(buf_refs[ti], 2 * px + py, c),
                    send_sem=sems[0].at[3 * ti + k], recv_sem=sems[1].at[3 * ti + k],
                    device_id=(px, py, c), device_id_type=MESH)
                done.wait_send()
                done.wait_recv()
        for si in range(n_s):
            for k, (px, py) in enumerate(chips):
                done = pltpu.make_async_remote_copy(
                    src_ref=small_refs[si].at[me], dst_ref=small_refs[si].at[2 * px + py],
                    send_sem=sems[2].at[3 * si + k], recv_sem=sems[3].at[3 * si + k],
                    device_id=(px, py, c), device_id_type=MESH)
                done.wait_send()
                done.wait_recv()

    res = pl.pallas_call(
        body, name=name,
        out_shape=tuple(pltpu.HBM(a.shape, a.dtype) for a in arrays),
        in_specs=[HBM] * len(arrays) + [SEM] * len(sem_ops) + [ANY],
        out_specs=tuple([HBM] * len(arrays)),
        input_output_aliases={i: i for i in range(len(arrays))},
        compiler_params=pltpu.CompilerParams(has_side_effects=DATAFLOW))(*arrays, *sem_ops, after)
    return list(res[:n_b]), list(res[n_b:])


def gather_forward(bufs, *, name):
    n = len(bufs)

    def body(*refs):
        ins, outs = refs[:n], refs[n:2 * n]
        send_sems, recv_sems = refs[2 * n:]
        x, y, c = _place()
        chips = _other_chips(x, y)
        for t in range(n):
            for k, (px, py) in enumerate(chips):
                pltpu.make_async_remote_copy(
                    src_ref=_half(ins[t], 2 * px + py, c), dst_ref=_half(outs[t], 2 * px + py, c),
                    send_sem=send_sems.at[3 * t + k], recv_sem=recv_sems.at[3 * t + k],
                    device_id=(x, y, 1 - c), device_id_type=MESH).start()
        for t in range(n):
            for k, (px, py) in enumerate(chips):
                done = pltpu.make_async_remote_copy(
                    src_ref=_half(ins[t], 2 * px + py, c), dst_ref=_half(outs[t], 2 * px + py, 1 - c),
                    send_sem=send_sems.at[3 * t + k], recv_sem=recv_sems.at[3 * t + k],
                    device_id=(x, y, 1 - c), device_id_type=MESH)
                done.wait_send()
                done.wait_recv()

    return pl.pallas_call(
        body, name=name, in_specs=[ANY] * n, out_specs=[ANY] * n,
        out_shape=[jax.ShapeDtypeStruct(a.shape, a.dtype) for a in bufs],
        input_output_aliases={i: i for i in range(n)},
        scratch_shapes=[pltpu.SemaphoreType.DMA((3 * n,)), pltpu.SemaphoreType.DMA((3 * n,))],
        compiler_params=pltpu.CompilerParams(has_side_effects=True))(*bufs)


def sum_stage_a(grad, recv, place, wire, *, name):
    j_n, half, cols = recv.shape

    def body(place_ref, g_ref, r_ref, o_ref, ob_ref):
        acc = g_ref[...] + r_ref[...]
        ob_ref[...] = acc.astype(wire)

        @pl.when(pl.program_id(0) == place_ref[0])
        def _():
            o_ref[...] = acc

    blk = (None, half, cols)
    return pl.pallas_call(
        body, name=name,
        grid_spec=pltpu.PrefetchScalarGridSpec(
            num_scalar_prefetch=1, grid=(j_n,),
            in_specs=[pl.BlockSpec(blk, lambda j, place_ref: (j, place_ref[1], 0)),
                      pl.BlockSpec(blk, lambda j, place_ref: (j, 0, 0))],
            out_specs=[pl.BlockSpec((half, cols), lambda j, place_ref: (0, 0)),
                       pl.BlockSpec(blk, lambda j, place_ref: (j, 0, 0))]),
        out_shape=[jax.ShapeDtypeStruct((half, cols), F32), jax.ShapeDtypeStruct(recv.shape, wire)],
        compiler_params=_params("arbitrary"))(place, grad, recv)


def _stage_a_copies(srcs, lands, x, y, c):
    out = []
    for src, land in zip(srcs, lands):
        half = src.shape[1] // 2
        out.append((src.at[:, pl.ds((1 - c) * half, half), :], land, (x, y, 1 - c)))
    return out


def _stage_b_copies(srcs, lands, x, y, c):
    out = []
    for src, land in zip(srcs, lands):
        for k, (px, py) in enumerate(_other_chips(x, y)):
            out.append((src.at[2 * px + py], land.at[k], (px, py, c)))
    return out


def _forward_copies(bufs, _, x, y, c):
    out = []
    for buf in bufs:
        for px, py in _other_chips(x, y):
            out.append((_half(buf, 2 * px + py, c), _half(buf, 2 * px + py, c), (x, y, 1 - c)))
    return out


def _stage_c_copies(fulls, _, x, y, c):
    out = []
    for full in fulls:
        half = full.shape[0] // 2
        mine = full.at[pl.ds(c * half, half), :]
        out.append((mine, mine, (x, y, 1 - c)))
    return out


def split_start(srcs, lands, copies, *, name):
    n, n_all = len(srcs), len(srcs) + len(lands)
    n_c = len(copies(srcs, lands, 0, 0, 0))

    def body(*refs):
        src_refs, land_refs = refs[:n], refs[n:n_all]
        send_sems, recv_sems = refs[n_all], refs[n_all + 1]
        token = refs[-1]
        x, y, c = _place()
        for k, (src, dst, target) in enumerate(copies(src_refs, land_refs, x, y, c)):
            pltpu.make_async_remote_copy(src_ref=src, dst_ref=dst, send_sem=send_sems.at[k], recv_sem=recv_sems.at[k],
                                         device_id=target, device_id_type=MESH).start()
        token[...] = jnp.zeros_like(token)

    arrays = list(srcs) + list(lands)
    res = pl.pallas_call(
        body, name=name,
        out_shape=(pltpu.SemaphoreType.DMA((n_c,)), pltpu.SemaphoreType.DMA((n_c,)))
        + tuple(pltpu.HBM(a.shape, a.dtype) for a in arrays) + (jax.ShapeDtypeStruct((8, 128), F32),),
        in_specs=[HBM] * n_all,
        out_specs=tuple([SEM, SEM] + [HBM] * n_all + [pl.BlockSpec(memory_space=pltpu.VMEM)]),
        input_output_aliases={i: 2 + i for i in range(n_all)},
        compiler_params=pltpu.CompilerParams(has_side_effects=DATAFLOW))(*[_in_hbm(a) for a in arrays])
    return res[0], res[1], list(res[2:2 + n]), list(res[2 + n:2 + n_all]), res[-1]


def split_wait(srcs, lands, send, recv, copies, after, *, name):
    n, n_all = len(srcs), len(srcs) + len(lands)

    def body(*refs):
        src_refs, land_refs = refs[:n], refs[n:n_all]
        send_sems, recv_sems = refs[n_all], refs[n_all + 1]
        x, y, c = _place()
        for k, (src, dst, target) in enumerate(copies(src_refs, land_refs, x, y, c)):
            done = pltpu.make_async_remote_copy(src_ref=src, dst_ref=dst, send_sem=send_sems.at[k],
                                                recv_sem=recv_sems.at[k], device_id=target, device_id_type=MESH)
            done.wait_send()
            done.wait_recv()

    arrays = list(srcs) + list(lands)
    res = pl.pallas_call(
        body, name=name,
        out_shape=tuple(pltpu.HBM(a.shape, a.dtype) for a in arrays),
        in_specs=[HBM] * n_all + [SEM, SEM, ANY],
        out_specs=tuple([HBM] * n_all),
        input_output_aliases={i: i for i in range(n_all)},
        compiler_params=pltpu.CompilerParams(has_side_effects=DATAFLOW))(*arrays, send, recv, after)
    return list(res[:n]), list(res[n:])


def split_wait_all(sets, copies, after, *, name):
    flat = [a for srcs, lands, _, _ in sets for a in list(srcs) + list(lands)]
    n_flat = len(flat)

    def body(*refs):
        x, y, c = _place()
        pos = 0
        for si, (srcs, lands, _, _) in enumerate(sets):
            src_refs = refs[pos:pos + len(srcs)]
            land_refs = refs[pos + len(srcs):pos + len(srcs) + len(lands)]
            pos += len(srcs) + len(lands)
            send_sems, recv_sems = refs[n_flat + 2 * si], refs[n_flat + 2 * si + 1]
            for k, (src, dst, target) in enumerate(copies(src_refs, land_refs, x, y, c)):
                done = pltpu.make_async_remote_copy(src_ref=src, dst_ref=dst, send_sem=send_sems.at[k],
                                                    recv_sem=recv_sems.at[k], device_id=target, device_id_type=MESH)
                done.wait_send()
                done.wait_recv()

    sems = [s_ for _, _, send, recv in sets for s_ in (send, recv)]
    res = pl.pallas_call(
        body, name=name,
        out_shape=tuple(pltpu.HBM(a.shape, a.dtype) for a in flat),
        in_specs=[HBM] * n_flat + [SEM] * len(sems) + [ANY],
        out_specs=tuple([HBM] * n_flat),
        input_output_aliases={i: i for i in range(n_flat)},
        compiler_params=pltpu.CompilerParams(has_side_effects=DATAFLOW))(*flat, *sems, after)
    out, pos = [], 0
    for srcs, lands, _, _ in sets:
        out.append(list(res[pos + len(srcs):pos + len(srcs) + len(lands)]))
        pos += len(srcs) + len(lands)
    return out


def sum_stage_b(part, recv, place, *, name):
    half, cols = part.shape

    def body(place_ref, p_ref, r_ref, o_ref):
        acc = p_ref[...]
        for k in range(3):
            acc = acc + r_ref[k].astype(F32)
        o_ref[...] = acc

    return pl.pallas_call(
        body, name=name,
        grid_spec=pltpu.PrefetchScalarGridSpec(
            num_scalar_prefetch=1, grid=(1,),
            in_specs=[pl.BlockSpec((half, cols), lambda i, place_ref: (0, 0)),
                      pl.BlockSpec((3, half, cols), lambda i, place_ref: (0, 0, 0))],
            out_specs=pl.BlockSpec((half, cols), lambda i, place_ref: (place_ref[1], 0))),
        out_shape=jax.ShapeDtypeStruct((2 * half, cols), F32),
        compiler_params=_params("arbitrary"))(place, part, recv)


def gather_chip_blocks(slots, *, name):
    def body(in_ref, out_ref, send_sems, recv_sems):
        x, y, c = _place()
        me = 2 * x + y
        chips = _other_chips(x, y)
        for k, (px, py) in enumerate(chips):
            pltpu.make_async_remote_copy(
                src_ref=in_ref.at[me], dst_ref=out_ref.at[me],
                send_sem=send_sems.at[k], recv_sem=recv_sems.at[k],
                device_id=(px, py, c), device_id_type=MESH).start()
        for k, (px, py) in enumerate(chips):
            done = pltpu.make_async_remote_copy(
                src_ref=in_ref.at[me], dst_ref=out_ref.at[2 * px + py],
                send_sem=send_sems.at[k], recv_sem=recv_sems.at[k],
                device_id=(px, py, c), device_id_type=MESH)
            done.wait_send()
            done.wait_recv()

    return pl.pallas_call(
        body, name=name, in_specs=[ANY], out_specs=ANY,
        out_shape=jax.ShapeDtypeStruct(slots.shape, slots.dtype),
        input_output_aliases={0: 0},
        scratch_shapes=[pltpu.SemaphoreType.DMA((3,)), pltpu.SemaphoreType.DMA((3,))],
        compiler_params=pltpu.CompilerParams(has_side_effects=True))(slots)


def _ffn_bwd(dh, dh_bf, h, gain, saved, wg, wu, wd, cw, cb, place, l, pin):
    hn, rstd, g, up, gc = saved
    dg, dup, dwg, dwu, dwd, dcw, dcb = ffn_bwd_a(dh_bf, hn, g, up, gc, wd, cw, name=f"ffn{l}_bwd_a", pin=pin)
    red = _reduce_a_begin([(dwg, BF16), (dwu, BF16), (dwd, BF16)], tag=f"f{l}")
    dh_in, dh_in_bf, dgain = dx_rms_bwd([(dg, wg), (dup, wu)], h, gain, rstd, dh, name=f"ffn{l}_bwd_b",
                                        pin=red[-1])
    red = _reduce_b_begin(red, place, dh_in_bf, tag=f"f{l}")
    return dh_in, dh_in_bf, (dcw, dcb, dgain), red


def _reduce_a_begin(grads, *, tag):
    lands = [lax.empty((g.shape[0], g.shape[1] // 2, g.shape[2]), F32) for g, _ in grads]
    send, recv, srcs, lands, token = split_start([g for g, _ in grads], lands, _stage_a_copies,
                                                 name=f"reduce_a_start_{tag}")
    return [w for _, w in grads], send, recv, srcs, lands, token


def _reduce_b_begin(state, place, after, *, tag):
    wires, send, recv, srcs, lands, _ = state
    grads, recv_a = split_wait(srcs, lands, send, recv, _stage_a_copies, after, name=f"reduce_a_wait_{tag}")
    parts = [sum_stage_a(g, r, place, w, name=f"sum_a_{tag}{i}") for i, (g, r, w) in enumerate(zip(grads, recv_a, wires))]
    lands_b = [lax.empty((3,) + p[1].shape[1:], p[1].dtype) for p in parts]
    send, recv, srcs, lands, token = split_start([p[1] for p in parts], lands_b, _stage_b_copies,
                                                 name=f"reduce_b_start_{tag}")
    return [p[0] for p in parts], send, recv, srcs, lands, token


def _reduce_c_begin(state, place, after, *, tag):
    parts, send, recv, srcs, lands, _ = state
    _, recv_b = split_wait(srcs, lands, send, recv, _stage_b_copies, after, name=f"reduce_b_wait_{tag}")
    halves = [sum_stage_b(p, r, place, name=f"sum_b_{tag}{i}") for i, (p, r) in enumerate(zip(parts, recv_b))]
    send, recv, fulls, _, token = split_start(halves, [], _stage_c_copies, name=f"reduce_c_start_{tag}")
    return send, recv, fulls, token


def _reduce_finish(state, after, *, tag):
    send, recv, fulls, _ = state
    fulls, _ = split_wait(fulls, [], send, recv, _stage_c_copies, after, name=f"reduce_c_wait_{tag}")
    return fulls


def kernel(x, norm_mix, norm_ffn, final_norm, w_in_even, conv_a, w_pool, pool_scale, w_out_even, w_in_odd, sgu_norm, w_spatial, b_spatial, w_out_odd, w_ffn_gate, w_ffn_up, conv_ffn, b_conv_ffn, w_ffn_down, loss_target, m_norm_mix, m_norm_ffn, m_final_norm, m_w_in_even, m_conv_a, m_w_pool, m_pool_scale, m_w_out_even, m_w_in_odd, m_sgu_norm, m_w_spatial, m_b_spatial, m_w_out_odd, m_w_ffn_gate, m_w_ffn_up, m_conv_ffn, m_b_conv_ffn, m_w_ffn_down, v_norm_mix, v_norm_ffn, v_final_norm, v_w_in_even, v_conv_a, v_w_pool, v_pool_scale, v_w_out_even, v_w_in_odd, v_sgu_norm, v_w_spatial, v_b_spatial, v_w_out_odd, v_w_ffn_gate, v_w_ffn_up, v_conv_ffn, v_b_conv_ffn, v_w_ffn_down):
    weights = dict(norm_mix=norm_mix, norm_ffn=norm_ffn, final_norm=final_norm, w_in_even=w_in_even,
                   conv_a=conv_a, w_pool=w_pool, pool_scale=pool_scale, w_out_even=w_out_even,
                   w_in_odd=w_in_odd, sgu_norm=sgu_norm, w_spatial=w_spatial, b_spatial=b_spatial,
                   w_out_odd=w_out_odd, w_ffn_gate=w_ffn_gate, w_ffn_up=w_ffn_up, conv_ffn=conv_ffn,
                   b_conv_ffn=b_conv_ffn, w_ffn_down=w_ffn_down)
    m_in = dict(norm_mix=m_norm_mix, norm_ffn=m_norm_ffn, final_norm=m_final_norm, w_in_even=m_w_in_even,
                conv_a=m_conv_a, w_pool=m_w_pool, pool_scale=m_pool_scale, w_out_even=m_w_out_even,
                w_in_odd=m_w_in_odd, sgu_norm=m_sgu_norm, w_spatial=m_w_spatial, b_spatial=m_b_spatial,
                w_out_odd=m_w_out_odd, w_ffn_gate=m_w_ffn_gate, w_ffn_up=m_w_ffn_up, conv_ffn=m_conv_ffn,
                b_conv_ffn=m_b_conv_ffn, w_ffn_down=m_w_ffn_down)
    v_in = dict(norm_mix=v_norm_mix, norm_ffn=v_norm_ffn, final_norm=v_final_norm, w_in_even=v_w_in_even,
                conv_a=v_conv_a, w_pool=v_w_pool, pool_scale=v_pool_scale, w_out_even=v_w_out_even,
                w_in_odd=v_w_in_odd, sgu_norm=v_sgu_norm, w_spatial=v_w_spatial, b_spatial=v_b_spatial,
                w_out_odd=v_w_out_odd, w_ffn_gate=v_w_ffn_gate, w_ffn_up=v_w_ffn_up, conv_ffn=v_conv_ffn,
                b_conv_ffn=v_b_conv_ffn, w_ffn_down=v_w_ffn_down)
    order = list(weights)

    chip = 2 * lax.axis_index("x") + lax.axis_index("y")
    core = lax.axis_index("c")
    place = jnp.stack([chip, core]).astype(jnp.int32)
    chip_arr = place[:1]

    h0 = x[0]
    target = loss_target[0]
    d_model = h0.shape[1]
    f_shard = w_ffn_gate.shape[-1]

    def turned(a):
        return jnp.transpose(a, (0, 2, 1))

    def own_slot(v):
        return lax.dynamic_update_index_in_dim(jnp.zeros((N_CHIPS,) + v.shape, v.dtype), v, chip, 0)

    first = [[cast_into_slot(w_in_even[0], chip_arr, name="cast_win_e"),
              cast_into_slot(w_out_even[0], chip_arr, name="cast_wout_e")]]
    smalls = [own_slot(conv_a[0]), own_slot(sgu_norm), own_slot(conv_ffn[0]), own_slot(conv_ffn[1])]
    sems0, first, smalls, token0 = gather_start(first, smalls, name="gather_start0")
    rest = [
        [cast_into_slot(turned(w_ffn_gate), chip_arr, l=0, name="cast_wg0", pin=token0),
         cast_into_slot(turned(w_ffn_up), chip_arr, l=0, name="cast_wu0", pin=token0)],
        [cast_into_slot(w_ffn_down, chip_arr, l=0, name="cast_wd0", pin=token0)],
        [cast_into_slot(w_in_odd[0], chip_arr, name="cast_win_o", pin=token0),
         cast_into_slot(w_out_odd[0], chip_arr, name="cast_wout_o", pin=token0)],
        [cast_into_slot(turned(w_ffn_gate), chip_arr, l=1, name="cast_wg1", pin=token0),
         cast_into_slot(turned(w_ffn_up), chip_arr, l=1, name="cast_wu1", pin=token0)],
        [cast_into_slot(w_ffn_down, chip_arr, l=1, name="cast_wd1", pin=token0)]]
    sems1, rest, _, token = gather_start(rest, [], name="gather_start1")
    groups = first + rest
    sems = list(sems0[:2]) + list(sems1) + list(sems0[2:])

    def arrive(gi, after, with_smalls=False):
        kw = dict(smalls=smalls, small_send=sems[-2], small_recv=sems[-1]) if with_smalls else {}
        bufs, small_out = gather_wait(groups[gi], sems[2 * gi], sems[2 * gi + 1], after, name=f"gather_wait{gi}", **kw)
        return gather_forward(bufs, name=f"gather_forward{gi}"), small_out

    def arrive_begin(gi, after):
        bufs, _ = gather_wait(groups[gi], sems[2 * gi], sems[2 * gi + 1], after, name=f"gather_wait{gi}")
        send, recv, bufs, _, tok = split_start(bufs, [], _forward_copies, name=f"gather_forward_start{gi}")
        return (send, recv, bufs), tok

    def arrive_end(state, after, gi):
        send, recv, bufs = state
        return split_wait(bufs, [], send, recv, _forward_copies, after, name=f"gather_forward_wait{gi}")[0]

    cb = b_conv_ffn.reshape(-1, N_CHIPS, 1, f_shard)
    wp_bf = w_pool[0].astype(BF16)
    wp_t_bf = jnp.transpose(w_pool[0], (0, 2, 1)).astype(BF16)
    ws = w_spatial[0]
    bs = b_spatial[0][:, :, None]

    (win_e, wout_e), (ca_g, sn_g, cw0, cw1) = arrive(0, token, with_smalls=True)
    wout_e = wout_e.reshape(-1, d_model)
    ca_full = jnp.transpose(ca_g, (1, 0, 2)).reshape(ca_g.shape[1], -1)
    sn_full = sn_g.reshape(1, -1)
    h1, xn0, rstd0, proj0, mix0, hn0, rstdf0 = even_layer_fwd(
        h0, norm_mix[0:1], win_e, ca_full, wp_bf, pool_scale, wout_e, norm_ffn[0:1], name="l0_fwd")
    (wg0, wu0), _ = arrive(1, hn0)
    g0, up0, gc0, act0 = ffn_in_fwd(hn0, wg0, wu0, cw0, cb[0], name="ffn0_in")
    swap2, tok2 = arrive_begin(2, act0)
    swap3, tok3 = arrive_begin(3, tok2)
    (wd0,) = arrive_end(swap2, tok3, 2)
    h2 = mm_acc(act0, wd0, h1, name="ffn0_down")
    ffn0 = (hn0, rstdf0, g0, up0, gc0)
    swap4, tok4 = arrive_begin(4, h2)
    swap5, tok5 = arrive_begin(5, tok4)
    win_o, wout_o = arrive_end(swap3, tok5, 3)
    wout_o = wout_o.reshape(-1, d_model)
    h3, xn1, rstd1, p1, mix1, rstd_v = odd_layer_fwd(h2, norm_mix[1:2], win_o, sn_full, ws, bs, wout_o, name="l1_fwd")
    wg1, wu1 = arrive_end(swap4, h3, 4)
    (wd1,) = arrive_end(swap5, wg1, 5)
    dh4, dh4_bf, loss_row, d_final, hn1, rstdf1, g1, up1, gc1 = ffn_loss_fwd(
        h3, norm_ffn[1:2], wg1, wu1, wd1, cw1, cb[1], target, final_norm[None], name="ffn1_fwd_loss")
    ffn1 = (hn1, rstdf1, g1, up1, gc1)

    loss = lax.psum(loss_row[0, 0], ("x", "y", "c"))

    dh3, dh3_bf, (dcw1, dcb1, dnf1), red3 = _ffn_bwd(
        dh4, dh4_bf, h3, norm_ffn[1:2], ffn1, wg1, wu1, wd1, cw1, cb[1], place, 1, None)

    def as_blocks(a):
        return a.reshape(N_CHIPS, -1, d_model)

    dp1, dwin_o, dwout_o, dsn, dws, dbs = sgu_bwd(p1, mix1, xn1, dh3_bf, wout_o, rstd_v, sn_full, ws, bs,
                                                  name="l1_mix_bwd", pin=red3[-1])
    dwout_o = as_blocks(dwout_o)
    red2 = _reduce_a_begin([(dwin_o, BF16), (dwout_o, BF16)], tag="m1")
    dh2, dh2_bf, dnm1 = dx_rms_bwd([(dp1, win_o)], h2, norm_mix[1:2], rstd1, dh3, name="l1_dx", pin=red2[-1])
    red2 = _reduce_b_begin(red2, place, dh2_bf, tag="m1")

    dh1, dh1_bf, (dcw0, dcb0, dnf0), red1 = _ffn_bwd(
        dh2, dh2_bf, h1, norm_ffn[0:1], ffn0, wg0, wu0, wd0, cw0, cb[0], place, 0, red2[-1])

    dproj0, dwin_e, dwout_e, dca, dwp, dps = even_bwd(proj0, mix0, xn0, dh1_bf, wout_e, ca_full, wp_bf, wp_t_bf,
                                                      pool_scale, name="l0_mix_bwd", pin=red1[-1])
    dwout_e = as_blocks(dwout_e)
    dh0, _, dnm0 = dx_rms_bwd([(dproj0, win_e)], h0, norm_mix[0:1], rstd0, dh1, name="l0_dx")
    grad_x = dh0[None]

    small_parts = dict(
        norm_mix=jnp.concatenate([dnm0, dnm1]), norm_ffn=jnp.concatenate([dnf0, dnf1]), final_norm=d_final,
        conv_a=dca, w_pool=dwp, pool_scale=dps, sgu_norm=dsn, w_spatial=dws, b_spatial=dbs,
        conv_ffn=jnp.stack([dcw0, dcw1]), b_conv_ffn=jnp.stack([dcb0, dcb1]))
    flat = jnp.concatenate([v.reshape(-1) for v in small_parts.values()])
    pad = (-flat.shape[0]) % (N_CHIPS * 32 * 128)
    small = jnp.pad(flat, (0, pad)).reshape(N_CHIPS, -1, 128)
    red0 = _reduce_a_begin([(dwin_e, BF16), (dwout_e, BF16), (small, F32)], tag="m0")

    early = [(red3, "f1"), (red2, "m1"), (red1, "f0")]
    landed = split_wait_all([(st[3], st[4], st[1], st[2]) for st, _ in early], _stage_b_copies, red0[-1],
                            name="reduce_b_wait_early")
    halves = [sum_stage_b(p, r, place, name=f"sum_b_{tag}{i}")
              for (st, tag), recv_b in zip(early, landed) for i, (p, r) in enumerate(zip(st[0], recv_b))]
    send_c, recv_c, halves, _, tok_c = split_start(halves, [], _stage_c_copies, name="reduce_c_start_early")
    red0 = _reduce_b_begin(red0, place, tok_c, tag="m0")
    fulls = _reduce_finish((send_c, recv_c, halves, tok_c), red0[-1], tag="early")
    full3, full2, full1 = fulls[0:3], fulls[3:5], fulls[5:8]
    swap_m0 = _reduce_c_begin(red0, place, full1[0], tag="m0")
    full0 = _reduce_finish(swap_m0, swap_m0[-1], tag="m0")
    small_slots = lax.dynamic_update_index_in_dim(jnp.zeros(small.shape, F32), full0[2], chip, 0)
    small_sum = gather_chip_blocks(small_slots, name="gather_small").reshape(-1)
    grads = {
        "w_in_even": full0[0][None], "w_out_even": full0[1][None],
        "w_in_odd": full2[0][None], "w_out_odd": full2[1][None],
        }
    layered = {"w_ffn_gate": [full1[0], full3[0]], "w_ffn_up": [full1[1], full3[1]],
               "w_ffn_down": [full1[2], full3[2]]}
    off = 0
    small_red = {}
    for nm, v in small_parts.items():
        small_red[nm] = small_sum[off:off + v.size].reshape(v.shape)
        off += v.size
    for nm in ("norm_mix", "norm_ffn", "pool_scale"):
        grads[nm] = small_red[nm].reshape(weights[nm].shape)
    grads["final_norm"] = small_red["final_norm"].reshape(weights["final_norm"].shape)
    grads["w_pool"] = small_red["w_pool"][None]
    grads["w_spatial"] = small_red["w_spatial"][None]
    grads["b_spatial"] = small_red["b_spatial"].reshape(weights["b_spatial"].shape)
    grads["b_conv_ffn"] = small_red["b_conv_ffn"].reshape(weights["b_conv_ffn"].shape)
    grads["conv_a"] = lax.dynamic_slice_in_dim(small_red["conv_a"], chip * conv_a.shape[-1], conv_a.shape[-1], 1)[None]
    grads["sgu_norm"] = lax.dynamic_slice_in_dim(small_red["sgu_norm"], chip * sgu_norm.shape[-1], sgu_norm.shape[-1], 1)
    grads["conv_ffn"] = lax.dynamic_index_in_dim(small_red["conv_ffn"], chip, 1, keepdims=False)

    deltas, new_m, new_v = {}, {}, {}
    for nm, per_layer in layered.items():
        if nm == "w_ffn_down":
            grads[nm], deltas[nm], new_m[nm], new_v[nm] = adamw_layers(
                weights[nm], per_layer, m_in[nm], v_in[nm], name=f"adamw_{nm}")
        else:
            outs = adamw_layers(turned(weights[nm]), per_layer, turned(m_in[nm]), turned(v_in[nm]),
                                name=f"adamw_{nm}")
            grads[nm], deltas[nm], new_m[nm], new_v[nm] = (turned(o) for o in outs)
    for nm in order:
        if nm in layered:
            continue
        w = weights[nm]
        w2 = w[None] if w.ndim == 1 else w
        shp = w2.shape
        d, nm_, nv_ = adamw(w2, grads[nm].reshape(shp), m_in[nm].reshape(shp), v_in[nm].reshape(shp),
                            name=f"adamw_{nm}")
        deltas[nm], new_m[nm], new_v[nm] = d.reshape(w.shape), nm_.reshape(w.shape), nv_.reshape(w.shape)

    return (loss, grad_x, *[grads[n] for n in order], *[deltas[n] for n in order],
            *[new_m[n] for n in order], *[new_v[n] for n in order])
```
